```python
import jax, jax.numpy as jnp
from jax import lax
import numpy as np

D_MODEL = 1024
BATCH = 8
SEQ = 4096
DEPTH = 1

PLE_DIM = 256
POOL_GROUPS = 4
POOL_GROUP_DIM = 128
POOL_WINDOWS = (2, 4, 8, 16)
POOL_DIM = POOL_GROUPS * POOL_GROUP_DIM
N_HEADS = 8
N_KV_HEADS = 2
HEAD_DIM = 64
Q_GROUP = N_HEADS // N_KV_HEADS
Q_DIM = N_HEADS * HEAD_DIM
KV_DIM = N_KV_HEADS * HEAD_DIM
WINDOW = 128
ATTN_BLOCK = 128
IN_DIM = POOL_DIM + Q_DIM + 2 * KV_DIM
N_EXPERTS = 64
N_EXPERT_GROUPS = 8
TOPK_GROUPS = 4
TOP_K = 8
EXPERT_DIM = 256
SHARED_DIM = 256
ROUTED_SCALE = 2.5
MOE_BLOCK = 256
EPS = 1e-6

kernel_name = "hybrid_pool_swa_moe_block"


def rmsnorm(x, g):
    xf = x.astype(jnp.float32)
    xf = xf * lax.rsqrt(jnp.mean(xf * xf, axis=-1, keepdims=True) + EPS)
    return xf.astype(x.dtype) * g


def alibi_slopes(n_heads):
    return jnp.asarray(2.0 ** (-8.0 * (np.arange(n_heads) + 1) / n_heads), dtype=jnp.float32)


def pool_mixer(u, w_mix, scale):
    B, S, _ = u.shape
    uf = u.astype(jnp.float32).reshape(B, S, POOL_GROUPS, POOL_GROUP_DIM)
    cs = jnp.concatenate([jnp.zeros((B, 1, POOL_GROUPS, POOL_GROUP_DIM), jnp.float32),
                          jnp.cumsum(uf, axis=1)], axis=1)
    t = jnp.arange(S)
    outs = []
    for g, w in enumerate(POOL_WINDOWS):
        lo = jnp.maximum(t + 1 - w, 0)
        window_sum = cs[:, 1:, g] - cs[:, lo, g]
        count = jnp.minimum(t + 1, w).astype(jnp.float32)[None, :, None]
        outs.append(window_sum / count - uf[:, :, g])
    pooled = jnp.stack(outs, axis=2).astype(u.dtype)
    mixed = jnp.einsum('bsgc,gcd->bsgd', pooled, w_mix)
    return mixed.reshape(B, S, POOL_DIM) * scale


def swa_attention(q, k, v, q_gain, k_gain, sinks):
    B, S, _ = q.shape
    nb = S // ATTN_BLOCK
    q = rmsnorm(q.reshape(B, S, N_HEADS, HEAD_DIM), q_gain)
    k = rmsnorm(k.reshape(B, S, N_KV_HEADS, HEAD_DIM), k_gain)
    v = v.reshape(B, S, N_KV_HEADS, HEAD_DIM)
    qb = q.reshape(B, nb, ATTN_BLOCK, N_KV_HEADS, Q_GROUP, HEAD_DIM).astype(jnp.float32)

    def band(t):
        tb = t.reshape(B, nb, ATTN_BLOCK, N_KV_HEADS, HEAD_DIM)
        prev = jnp.concatenate([jnp.zeros_like(tb[:, :1]), tb[:, :-1]], axis=1)
        return jnp.concatenate([prev, tb], axis=2).astype(jnp.float32)

    kw, vw = band(k), band(v)
    scores = jnp.einsum('bnqkgd,bnskd->bnkgqs', qb, kw) * (HEAD_DIM ** -0.5)
    i = jnp.arange(ATTN_BLOCK)[:, None]
    j = jnp.arange(2 * ATTN_BLOCK)[None, :]
    dist = (ATTN_BLOCK + i - j)
    in_window = (dist >= 0) & (dist < WINDOW)
    key_exists = (jnp.arange(nb)[:, None] > 0) | (jnp.arange(2 * ATTN_BLOCK)[None, :] >= ATTN_BLOCK)
    mask = in_window[None] & key_exists[:, None, :]
    slopes = alibi_slopes(N_HEADS).reshape(N_KV_HEADS, Q_GROUP)
    scores = scores - slopes[:, :, None, None] * dist.astype(jnp.float32)
    scores = jnp.where(mask[None, :, None, None], scores, -jnp.inf)
    sink = sinks.astype(jnp.float32).reshape(N_KV_HEADS, Q_GROUP)[None, None, :, :, None, None]
    m = jnp.maximum(jnp.max(scores, axis=-1, keepdims=True), sink)
    e = jnp.exp(scores - m)
    probs = e / (jnp.sum(e, axis=-1, keepdims=True) + jnp.exp(sink - m))
    out = jnp.einsum('bnkgqs,bnskd->bnqkgd', probs, vw)
    return out.reshape(B, S, Q_DIM).astype(q.dtype)


def swiglu(x, wg, wu, wd):
    return (jax.nn.silu(x @ wg) * (x @ wu)) @ wd


def route(h, w_router, bias):
    T = h.shape[0]
    scores = jax.nn.sigmoid(h.astype(jnp.float32) @ w_router.astype(jnp.float32))
    choice = scores + bias.astype(jnp.float32)
    grp = choice.reshape(T, N_EXPERT_GROUPS, N_EXPERTS // N_EXPERT_GROUPS)
    grp_score = lax.top_k(grp, 2)[0].sum(-1)
    _, top_groups = lax.top_k(grp_score, TOPK_GROUPS)
    gmask = jax.nn.one_hot(top_groups, N_EXPERT_GROUPS, dtype=jnp.float32).sum(1) > 0
    emask = jnp.repeat(gmask, N_EXPERTS // N_EXPERT_GROUPS, axis=1)
    _, idx = lax.top_k(jnp.where(emask, choice, -jnp.inf), TOP_K)
    w = jnp.take_along_axis(scores, idx, axis=1)
    w = w / (jnp.sum(w, axis=-1, keepdims=True) + 1e-20) * ROUTED_SCALE
    return idx, w


def routed_experts(h, idx, w, w_gate, w_up, w_down):
    T, D = h.shape
    A = T * TOP_K
    e_flat = idx.reshape(A).astype(jnp.int32)
    tok_flat = (jnp.arange(A, dtype=jnp.int32) // TOP_K)
    w_flat = w.reshape(A)
    order = jnp.argsort(e_flat)
    e_sorted = e_flat[order]
    counts = jnp.bincount(e_flat, length=N_EXPERTS).astype(jnp.int32)
    start = jnp.cumsum(counts) - counts
    padded = (counts + MOE_BLOCK - 1) // MOE_BLOCK * MOE_BLOCK
    pend = jnp.cumsum(padded)
    pstart = pend - padded
    dest = pstart[e_sorted] + (jnp.arange(A, dtype=jnp.int32) - start[e_sorted])
    n_blocks = -(-A // MOE_BLOCK) + N_EXPERTS
    P = n_blocks * MOE_BLOCK
    tok_buf = jnp.full((P,), T, jnp.int32).at[dest].set(tok_flat[order])
    w_buf = jnp.zeros((P,), jnp.float32).at[dest].set(w_flat[order])
    block_e = jnp.minimum(jnp.searchsorted(pend, jnp.arange(n_blocks, dtype=jnp.int32) * MOE_BLOCK,
                                           side='right'), N_EXPERTS - 1)
    h_pad = jnp.concatenate([h, jnp.zeros((1, D), h.dtype)], axis=0)

    def body(acc, xs):
        tok, wt, e = xs
        y = swiglu(h_pad[tok], w_gate[e], w_up[e], w_down[e])
        return acc.at[tok].add(y.astype(jnp.float32) * wt[:, None]), None

    acc, _ = lax.scan(body, jnp.zeros((T + 1, D), jnp.float32),
                      (tok_buf.reshape(n_blocks, MOE_BLOCK), w_buf.reshape(n_blocks, MOE_BLOCK), block_e))
    return acc[:T].astype(h.dtype)


def setup_inputs(seed: int = 0) -> dict:
    key = jax.random.key(seed)
    ks = jax.random.split(key, 32)
    f32 = jnp.float32
    L, D = DEPTH, D_MODEL

    def nrm(k, shape, fan_in):
        return jax.random.normal(k, shape, f32) * (fan_in ** -0.5)

    def gain(k, shape):
        return 1.0 + 0.05 * jax.random.normal(k, shape, f32)

    return {
        "x": jax.random.normal(ks[0], (BATCH, SEQ, D), f32),
        "p": jax.random.normal(ks[1], (DEPTH, BATCH, SEQ, PLE_DIM), f32),
        "g_mix": gain(ks[2], (L, D)),
        "w_in": nrm(ks[3], (L, D, IN_DIM), D),
        "w_pool_mix": nrm(ks[4], (L, POOL_GROUPS, POOL_GROUP_DIM, POOL_GROUP_DIM), POOL_GROUP_DIM),
        "pool_scale": gain(ks[5], (L, POOL_DIM)),
        "w_branch_a": nrm(ks[6], (L, POOL_DIM, D), POOL_DIM),
        "q_gain": gain(ks[7], (L, HEAD_DIM)),
        "k_gain": gain(ks[8], (L, HEAD_DIM)),
        "attn_sinks": 0.5 * jax.random.normal(ks[9], (L, N_HEADS), f32),
        "w_branch_b": nrm(ks[10], (L, Q_DIM, D), Q_DIM),
        "w_gate": nrm(ks[11], (L, D, 2 * D), D),
        "b_gate": 0.1 * jax.random.normal(ks[12], (L, 2 * D), f32),
        "w_out": nrm(ks[13], (L, D, D), D),
        "g_ffn": gain(ks[14], (L, D)),
        "w_router": nrm(ks[15], (L, D, N_EXPERTS), D),
        "router_bias": 0.01 * jax.random.normal(ks[16], (L, N_EXPERTS), f32),
        "w_exp_gate": nrm(ks[17], (L, N_EXPERTS, D, EXPERT_DIM), D),
        "w_exp_up": nrm(ks[18], (L, N_EXPERTS, D, EXPERT_DIM), D),
        "w_exp_down": nrm(ks[19], (L, N_EXPERTS, EXPERT_DIM, D), EXPERT_DIM),
        "w_sh_gate": nrm(ks[20], (L, D, SHARED_DIM), D),
        "w_sh_up": nrm(ks[21], (L, D, SHARED_DIM), D),
        "w_sh_down": nrm(ks[22], (L, SHARED_DIM, D), SHARED_DIM),
        "g_ple": gain(ks[23], (L, D)),
        "w_ple_gate": nrm(ks[24], (L, D, D), D),
        "w_ple_proj": nrm(ks[25], (L, PLE_DIM, D), PLE_DIM),
        "g_ple_post": gain(ks[26], (L, D)),
    }


def reference(x, p, g_mix, w_in, w_pool_mix, pool_scale, w_branch_a, q_gain, k_gain, attn_sinks,
              w_branch_b, w_gate, b_gate, w_out, g_ffn, w_router, router_bias, w_exp_gate, w_exp_up,
              w_exp_down, w_sh_gate, w_sh_up, w_sh_down, g_ple, w_ple_gate, w_ple_proj, g_ple_post):
    B, S, D = x.shape
    T = B * S
    for i in range(DEPTH):
        h = rmsnorm(x, g_mix[i])
        u = h @ w_in[i]
        u_pool = u[..., :POOL_DIM]
        q = u[..., POOL_DIM:POOL_DIM + Q_DIM]
        k = u[..., POOL_DIM + Q_DIM:POOL_DIM + Q_DIM + KV_DIM]
        v = u[..., POOL_DIM + Q_DIM + KV_DIM:]
        y_a = pool_mixer(u_pool, w_pool_mix[i], pool_scale[i]) @ w_branch_a[i]
        y_b = swa_attention(q, k, v, q_gain[i], k_gain[i], attn_sinks[i]) @ w_branch_b[i]
        gates = jax.nn.sigmoid(h @ w_gate[i] + b_gate[i])
        merged = gates[..., :D] * y_a + gates[..., D:] * y_b
        x = x + merged @ w_out[i]
        h2 = rmsnorm(x, g_ffn[i]).reshape(T, D)
        idx, wts = route(h2, w_router[i], router_bias[i])
        moe = routed_experts(h2, idx, wts, w_exp_gate[i], w_exp_up[i], w_exp_down[i]) \
            + swiglu(h2, w_sh_gate[i], w_sh_up[i], w_sh_down[i])
        x = x + moe.reshape(B, S, D)
        ple_gate = jax.nn.sigmoid(rmsnorm(x, g_ple[i]) @ w_ple_gate[i])
        x = x + ple_gate * rmsnorm(p[i] @ w_ple_proj[i], g_ple_post[i])
    return x
```

```python
import functools

import numpy as np
import jax
import jax.numpy as jnp
from jax import lax
from jax.experimental import pallas as pl
from jax.experimental.pallas import tpu as pltpu

F32 = jnp.float32
BF16 = jnp.bfloat16
I32 = jnp.int32

D_MODEL = 1024
PLE_DIM = 256
POOL_WINDOWS = (2, 4, 8, 16)
POOL_GROUP_DIM = 128
POOL_DIM = 512
N_HEADS = 8
N_KV_HEADS = 2
Q_GROUP = 4
HEAD_DIM = 64
HEAD_PAD = 128
Q_DIM = 512
KV_DIM = 128
ATTN_BLOCK = 128
N_EXPERTS = 64
N_EXPERT_GROUPS = 8
GROUP_SIZE = 8
TOPK_GROUPS = 4
TOP_K = 8
EXPERT_DIM = 256
ROUTED_SCALE = 2.5
MOE_BLOCK = 256
EPS = 1e-6
HALF = D_MODEL // 2
MASKED = -1e30

COL_Q = POOL_DIM
COL_K = COL_Q + N_HEADS * HEAD_PAD
COL_VLO = COL_K + N_KV_HEADS * HEAD_PAD
COL_VHI = COL_VLO + N_KV_HEADS * HEAD_PAD
IN_PAD = COL_VHI + N_KV_HEADS * HEAD_PAD

TM_MIX = 512
TM_FFN = 512
TM_ROW = 256
POOL_TAIL = 8
VMEM_LIMIT = 56 * 1024 * 1024

_NT = (((1,), (1,)), ((), ()))


def _dot(a, b):
    return jnp.dot(a, b, preferred_element_type=F32)


def _rms(x, g):
    ms = jnp.mean(x * x, axis=-1, keepdims=True)
    return x * lax.rsqrt(ms + EPS) * g


def _sigmoid(x):
    return 1.0 / (1.0 + jnp.exp(-x))


def _pack(lo, hi):
    lo_bits = lax.bitcast_convert_type(lo.astype(BF16).astype(F32), I32)
    hi_bits = lax.bitcast_convert_type(hi.astype(BF16).astype(F32), I32)
    return (hi_bits & jnp.int32(-65536)) | lax.shift_right_logical(lo_bits, jnp.int32(16))


def _unpack(w):
    lo = lax.bitcast_convert_type(lax.shift_left(w, jnp.int32(16)), F32)
    hi = lax.bitcast_convert_type(w & jnp.int32(-65536), F32)
    return lo, hi


def _mixer_kernel(x_ref, gmix_ref, win_ref, qg_ref, kg_ref, bias_ref, sink_ref, wmix_ref,
                  pscale_ref, wa_ref, wb_ref, wg_ref, bg_ref, wout_ref,
                  x1_ref, kbuf, vlo_buf, vhi_buf, pbuf, zb_buf):
    s = pl.program_id(1)
    tm = x_ref.shape[0]

    @pl.when(s == 0)
    def _():
        kbuf[0:ATTN_BLOCK, :] = jnp.zeros((ATTN_BLOCK, kbuf.shape[1]), BF16)
        vlo_buf[0:ATTN_BLOCK, :] = jnp.zeros((ATTN_BLOCK, vlo_buf.shape[1]), BF16)
        vhi_buf[0:ATTN_BLOCK, :] = jnp.zeros((ATTN_BLOCK, vhi_buf.shape[1]), BF16)
        pbuf[:, 0:POOL_TAIL, :] = jnp.zeros((pbuf.shape[0], POOL_TAIL, POOL_DIM), F32)

    x = x_ref[...]
    hb = _rms(x, gmix_ref[...]).astype(BF16)
    u = _dot(hb, win_ref[...])

    up = u[:, 0:POOL_DIM]
    t0 = POOL_TAIL
    pbuf[0, t0:t0 + tm, :] = up
    s2 = up + pbuf[0, t0 - 1:t0 - 1 + tm, :]
    pbuf[1, t0:t0 + tm, 128:512] = s2[:, 128:512]
    s4 = s2[:, 128:512] + pbuf[1, t0 - 2:t0 - 2 + tm, 128:512]
    pbuf[2, t0:t0 + tm, 256:512] = s4[:, 128:384]
    s8 = s4[:, 128:384] + pbuf[2, t0 - 4:t0 - 4 + tm, 256:512]
    pbuf[3, t0:t0 + tm, 384:512] = s8[:, 128:256]
    s16 = s8[:, 128:256] + pbuf[3, t0 - 8:t0 - 8 + tm, 384:512]
    for lvl in range(4):
        pbuf[lvl, 0:POOL_TAIL, 128 * lvl:512] = pbuf[lvl, tm:tm + POOL_TAIL, 128 * lvl:512]
    wsums = (s2[:, 0:128], s4[:, 0:128], s8[:, 0:128], s16)
    tpos = (s * tm).astype(F32) + lax.broadcasted_iota(I32, (tm, 1), 0).astype(F32)
    za_parts = []
    for g, w in enumerate(POOL_WINDOWS):
        inv_count = 1.0 / jnp.minimum(tpos + 1.0, float(w))
        pooled = wsums[g] * inv_count - up[:, 128 * g:128 * (g + 1)]
        mixed = _dot(pooled.astype(BF16), wmix_ref[g]) * pscale_ref[:, 128 * g:128 * (g + 1)]
        za_parts.append(mixed.astype(BF16))
    za = jnp.concatenate(za_parts, axis=1)

    qg = qg_ref[...]
    qn = []
    for h in range(N_HEADS):
        qh = u[:, COL_Q + HEAD_PAD * h:COL_Q + HEAD_PAD * (h + 1)]
        ms = jnp.sum(qh * qh, axis=-1, keepdims=True) * (1.0 / HEAD_DIM)
        qn.append((qh * lax.rsqrt(ms + EPS) * qg).astype(BF16))
    kg = kg_ref[...]
    for kv in range(N_KV_HEADS):
        kh = u[:, COL_K + HEAD_PAD * kv:COL_K + HEAD_PAD * (kv + 1)]
        ms = jnp.sum(kh * kh, axis=-1, keepdims=True) * (1.0 / HEAD_DIM)
        kbuf[ATTN_BLOCK:ATTN_BLOCK + tm, HEAD_PAD * kv:HEAD_PAD * (kv + 1)] = (
            kh * lax.rsqrt(ms + EPS) * kg).astype(BF16)
    vlo_buf[ATTN_BLOCK:ATTN_BLOCK + tm, :] = u[:, COL_VLO:COL_VHI].astype(BF16)
    vhi_buf[ATTN_BLOCK:ATTN_BLOCK + tm, :] = u[:, COL_VHI:IN_PAD].astype(BF16)

    first = jnp.where(s == 0, 1, 0)
    for n in range(tm // ATTN_BLOCK):
        r0 = ATTN_BLOCK * n
        for kv in range(N_KV_HEADS):
            c0 = HEAD_PAD * kv
            qs = jnp.concatenate([qn[kv * Q_GROUP + g][r0:r0 + ATTN_BLOCK] for g in range(Q_GROUP)],
                                 axis=0)
            kk = kbuf[r0:r0 + 2 * ATTN_BLOCK, c0:c0 + HEAD_PAD]
            sc = lax.dot_general(qs, kk, _NT, preferred_element_type=F32)
            sc = sc + (bias_ref[first, kv] if n == 0 else bias_ref[0, kv])
            sink = sink_ref[kv]
            m = jnp.maximum(jnp.max(sc, axis=-1, keepdims=True), sink)
            e = jnp.exp(sc - m)
            den = jnp.sum(e, axis=-1, keepdims=True) + jnp.exp(sink - m)
            p = (e * (1.0 / den)).astype(BF16)
            vl = vlo_buf[r0:r0 + 2 * ATTN_BLOCK, c0:c0 + HEAD_PAD]
            vh = vhi_buf[r0:r0 + 2 * ATTN_BLOCK, c0:c0 + HEAD_PAD]
            for pair in range(2):
                pr = 2 * pair * ATTN_BLOCK
                o = _dot(p[pr:pr + ATTN_BLOCK], vl) + _dot(p[pr + ATTN_BLOCK:pr + 2 * ATTN_BLOCK], vh)
                col = kv * Q_GROUP * HEAD_DIM + pair * HEAD_PAD
                zb_buf[r0:r0 + ATTN_BLOCK, col:col + HEAD_PAD] = o.astype(BF16)
    kbuf[0:ATTN_BLOCK, :] = kbuf[tm:tm + ATTN_BLOCK, :]
    vlo_buf[0:ATTN_BLOCK, :] = vlo_buf[tm:tm + ATTN_BLOCK, :]
    vhi_buf[0:ATTN_BLOCK, :] = vhi_buf[tm:tm + ATTN_BLOCK, :]

    y_a = _dot(za, wa_ref[...])
    y_b = _dot(zb_buf[...], wb_ref[...])
    g_a = _sigmoid(_dot(hb, wg_ref[:, 0:D_MODEL]) + bg_ref[:, 0:D_MODEL])
    merged = g_a * y_a
    g_b = _sigmoid(_dot(hb, wg_ref[:, D_MODEL:2 * D_MODEL]) + bg_ref[:, D_MODEL:2 * D_MODEL])
    merged = (merged + g_b * y_b).astype(BF16)
    x1_ref[...] = x + _dot(merged, wout_ref[...])


def _attn_tables(attn_sinks):
    slopes = 2.0 ** (-8.0 * (np.arange(N_HEADS) + 1) / N_HEADS)
    i = np.arange(ATTN_BLOCK)[:, None]
    j = np.arange(2 * ATTN_BLOCK)[None, :]
    dist = ATTN_BLOCK + i - j
    in_window = (dist >= 0) & (dist < ATTN_BLOCK)
    bias = np.empty((2, N_KV_HEADS, Q_GROUP * ATTN_BLOCK, 2 * ATTN_BLOCK), np.float32)
    for first in range(2):
        ok = in_window & ((j >= ATTN_BLOCK) if first else True)
        for kv in range(N_KV_HEADS):
            for g in range(Q_GROUP):
                sl = np.float32(slopes[kv * Q_GROUP + g])
                val = -(sl * dist.astype(np.float32))
                bias[first, kv, g * ATTN_BLOCK:(g + 1) * ATTN_BLOCK] = np.where(ok, val, MASKED)
    sink = jnp.repeat(attn_sinks.astype(F32).reshape(N_KV_HEADS, Q_GROUP), ATTN_BLOCK, axis=1)
    return jnp.asarray(bias), sink.reshape(N_KV_HEADS, Q_GROUP * ATTN_BLOCK, 1)


def _pad_heads(w, n_heads, offset=0):
    k = w.shape[0]
    w = w.reshape(k, n_heads, HEAD_DIM)
    w = jnp.pad(w, ((0, 0), (0, 0), (offset, HEAD_PAD - HEAD_DIM - offset)))
    return w.reshape(k, n_heads * HEAD_PAD)


def _const_spec(shape):
    nd = len(shape)
    return pl.BlockSpec(shape, lambda *_: (0,) * nd)


def _mixer(x2d, batch, seq, g_mix, w_in, w_pool_mix, pool_scale, w_branch_a, q_gain, k_gain,
           attn_sinks, w_branch_b, w_gate, b_gate, w_out):
    t = x2d.shape[0]
    tm = TM_MIX
    ns = seq // tm
    w_q = w_in[:, POOL_DIM:POOL_DIM + Q_DIM]
    w_k = w_in[:, POOL_DIM + Q_DIM:POOL_DIM + Q_DIM + KV_DIM]
    w_v = w_in[:, POOL_DIM + Q_DIM + KV_DIM:]
    win_p = jnp.concatenate([w_in[:, :POOL_DIM], _pad_heads(w_q, N_HEADS), _pad_heads(w_k, N_KV_HEADS),
                             _pad_heads(w_v, N_KV_HEADS), _pad_heads(w_v, N_KV_HEADS, HEAD_DIM)],
                            axis=1).astype(BF16)
    qg = jnp.pad(q_gain * (HEAD_DIM ** -0.5), (0, HEAD_PAD - HEAD_DIM)).reshape(1, HEAD_PAD)
    kg = jnp.pad(k_gain, (0, HEAD_PAD - HEAD_DIM)).reshape(1, HEAD_PAD)
    bias, sink = _attn_tables(attn_sinks)
    operands = (x2d, g_mix.reshape(1, D_MODEL), win_p, qg, kg, bias, sink, w_pool_mix.astype(BF16),
                pool_scale.reshape(1, POOL_DIM), w_branch_a.astype(BF16), w_branch_b.astype(BF16),
                w_gate.astype(BF16), b_gate.reshape(1, 2 * D_MODEL), w_out.astype(BF16))
    in_specs = [pl.BlockSpec((tm, D_MODEL), lambda b, s: (b * ns + s, 0))]
    in_specs += [_const_spec(op.shape) for op in operands[1:]]
    return pl.pallas_call(
        _mixer_kernel,
        grid=(batch, ns),
        in_specs=in_specs,
        out_specs=pl.BlockSpec((tm, D_MODEL), lambda b, s: (b * ns + s, 0)),
        out_shape=jax.ShapeDtypeStruct((t, D_MODEL), F32),
        scratch_shapes=[
            pltpu.VMEM((ATTN_BLOCK + tm, N_KV_HEADS * HEAD_PAD), BF16),
            pltpu.VMEM((ATTN_BLOCK + tm, N_KV_HEADS * HEAD_PAD), BF16),
            pltpu.VMEM((ATTN_BLOCK + tm, N_KV_HEADS * HEAD_PAD), BF16),
            pltpu.VMEM((4, POOL_TAIL + tm, POOL_DIM), F32),
            pltpu.VMEM((tm, Q_DIM), BF16),
        ],
        compiler_params=pltpu.CompilerParams(
            dimension_semantics=("arbitrary", "arbitrary"), vmem_limit_bytes=VMEM_LIMIT),
        name="mixer",
    )(*operands)


def _ffn_pre_kernel(x1_ref, gffn_ref, wsg_ref, wsu_ref, wsd_ref, wrh_ref, wrl_ref, rb_ref, tri_ref,
                    xp_ref, h2p_ref, idx_ref, wt_ref, rk_ref, cnt_ref, run_ref):
    i = pl.program_id(0)
    tm = x1_ref.shape[0]

    @pl.when(i == 0)
    def _():
        run_ref[...] = jnp.zeros_like(run_ref)

    x1 = x1_ref[...]
    h2 = _rms(x1, gffn_ref[...])
    hb = h2.astype(BF16)
    g = _dot(hb, wsg_ref[...])
    a = (g * _sigmoid(g) * _dot(hb, wsu_ref[...])).astype(BF16)
    xp_ref[...] = x1 + _dot(a, wsd_ref[...])
    h2p_ref[...] = _pack(h2[:, 0:HALF], h2[:, HALF:D_MODEL])

    lo = (h2 - hb.astype(F32)).astype(BF16)
    wrh = wrh_ref[...]
    logits = (lax.dot_general(wrh, hb, _NT, preferred_element_type=F32)
              + lax.dot_general(wrl_ref[...], hb, _NT, preferred_element_type=F32)
              + lax.dot_general(wrh, lo, _NT, preferred_element_type=F32))
    scores = _sigmoid(logits)
    choice = scores + rb_ref[...]

    neg = -jnp.inf
    sub8 = lax.broadcasted_iota(I32, (GROUP_SIZE, tm), 0).astype(F32)
    grp_rows = []
    for gi in range(N_EXPERT_GROUPS):
        cg = choice[GROUP_SIZE * gi:GROUP_SIZE * (gi + 1)]
        m1 = jnp.max(cg, axis=0, keepdims=True)
        first = jnp.min(jnp.where(cg == m1, sub8, float(GROUP_SIZE)), axis=0, keepdims=True)
        m2 = jnp.max(jnp.where(sub8 == first, neg, cg), axis=0, keepdims=True)
        grp_rows.append(m1 + m2)
    gs = jnp.concatenate(grp_rows, axis=0)
    beaten = jnp.zeros((N_EXPERT_GROUPS, tm), F32)
    for gi in range(N_EXPERT_GROUPS):
        row = grp_rows[gi]
        wins = (row > gs) | ((row == gs) & (sub8 > float(gi)))
        beaten = beaten + jnp.where(wins, 1.0, 0.0)
    gsel = beaten < float(TOPK_GROUPS)
    cm = jnp.concatenate(
        [jnp.where(gsel[gi:gi + 1], choice[GROUP_SIZE * gi:GROUP_SIZE * (gi + 1)], neg)
         for gi in range(N_EXPERT_GROUPS)], axis=0)

    iota_e = lax.broadcasted_iota(I32, (N_EXPERTS, tm), 0).astype(F32)
    idx_rows, s_rows = [], []
    sel = jnp.zeros((N_EXPERTS, tm), F32)
    for _ in range(TOP_K):
        m = jnp.max(cm, axis=0, keepdims=True)
        idx = jnp.min(jnp.where(cm == m, iota_e, float(N_EXPERTS)), axis=0, keepdims=True)
        oh = iota_e == idx
        s_rows.append(jnp.sum(jnp.where(oh, scores, 0.0), axis=0, keepdims=True))
        idx_rows.append(idx)
        cm = jnp.where(oh, neg, cm)
        sel = sel + jnp.where(oh, 1.0, 0.0)

    run = run_ref[:, 0:1]
    cum = _dot(sel.astype(BF16), tri_ref[...])
    before = run + cum - sel
    rk_rows = [jnp.sum(jnp.where(iota_e == idx, before, 0.0), axis=0, keepdims=True) for idx in idx_rows]
    new_run = run + jnp.sum(sel, axis=1, keepdims=True)
    run_ref[...] = jnp.broadcast_to(new_run, run_ref.shape)
    cnt_ref[...] = jnp.broadcast_to(new_run, cnt_ref.shape)

    ssum = s_rows[0]
    for r in range(1, TOP_K):
        ssum = ssum + s_rows[r]
    denom = ssum + 1e-20
    idx_ref[...] = jnp.concatenate(idx_rows, axis=0).astype(I32)
    wt_ref[...] = jnp.concatenate([sr / denom * ROUTED_SCALE for sr in s_rows], axis=0)
    rk_ref[...] = jnp.concatenate(rk_rows, axis=0).astype(I32)


def _ffn_pre(x1, g_ffn, w_sh_gate, w_sh_up, w_sh_down, w_router, router_bias):
    t = x1.shape[0]
    tm = TM_FFN
    wr_t = w_router.astype(F32).T
    wr_hi = wr_t.astype(BF16)
    wr_lo = (wr_t - wr_hi.astype(F32)).astype(BF16)
    tri = (np.arange(tm)[:, None] <= np.arange(tm)[None, :]).astype(np.float32)
    operands = (x1, g_ffn.reshape(1, D_MODEL), w_sh_gate.astype(BF16), w_sh_up.astype(BF16),
                w_sh_down.astype(BF16), wr_hi, wr_lo, router_bias.astype(F32).reshape(N_EXPERTS, 1),
                jnp.asarray(tri, BF16))
    in_specs = [pl.BlockSpec((tm, D_MODEL), lambda i: (i, 0))]
    in_specs += [_const_spec(op.shape) for op in operands[1:]]
    row8 = pl.BlockSpec((TOP_K, tm), lambda i: (0, i))
    return pl.pallas_call(
        _ffn_pre_kernel,
        grid=(t // tm,),
        in_specs=in_specs,
        out_specs=[pl.BlockSpec((tm, D_MODEL), lambda i: (i, 0)),
                   pl.BlockSpec((tm, HALF), lambda i: (i, 0)),
                   row8, row8, row8,
                   pl.BlockSpec((N_EXPERTS, 128), lambda i: (0, 0))],
        out_shape=[jax.ShapeDtypeStruct((t, D_MODEL), F32),
                   jax.ShapeDtypeStruct((t, HALF), I32),
                   jax.ShapeDtypeStruct((TOP_K, t), I32),
                   jax.ShapeDtypeStruct((TOP_K, t), F32),
                   jax.ShapeDtypeStruct((TOP_K, t), I32),
                   jax.ShapeDtypeStruct((N_EXPERTS, 128), F32)],
        scratch_shapes=[pltpu.VMEM((N_EXPERTS, 128), F32)],
        compiler_params=pltpu.CompilerParams(
            dimension_semantics=("arbitrary",), vmem_limit_bytes=VMEM_LIMIT),
        name="ffn_pre",
    )(*operands)


def _row_copy(src, src_row, dst, dst_row, sem):
    return pltpu.make_async_copy(src.at[pl.ds(src_row, 1)], dst.at[pl.ds(dst_row, 1)], sem)


def _dispatch_kernel(pos_ref, h2p_ref, xs_ref, sem):
    tm = h2p_ref.shape[0]

    def issue(t, carry):
        for k in range(TOP_K):
            _row_copy(h2p_ref, t, xs_ref, pos_ref[k, t], sem).start()
        return carry

    lax.fori_loop(0, tm, issue, 0)

    def drain(t, carry):
        for k in range(TOP_K):
            _row_copy(h2p_ref, t, xs_ref, pos_ref[k, t], sem).wait()
        return carry

    lax.fori_loop(0, tm, drain, 0)


def _dispatch(pos, h2p, n_rows):
    t = h2p.shape[0]
    tm = TM_ROW
    return pl.pallas_call(
        _dispatch_kernel,
        grid=(t // tm,),
        in_specs=[pl.BlockSpec((TOP_K, tm), lambda i: (0, i), memory_space=pltpu.SMEM),
                  pl.BlockSpec((tm, HALF), lambda i: (i, 0))],
        out_specs=pl.BlockSpec(memory_space=pl.ANY),
        out_shape=jax.ShapeDtypeStruct((n_rows, HALF), I32),
        scratch_shapes=[pltpu.SemaphoreType.DMA],
        compiler_params=pltpu.CompilerParams(dimension_semantics=("arbitrary",)),
        name="dispatch",
    )(pos, h2p)


def _expert_kernel(be_ref, nv_ref, src_ref, xs_ref, wg_ref, wu_ref, wd_ref, ys_ref):
    b = pl.program_id(0)
    nv = nv_ref[b]

    @pl.when(nv > 0)
    def _():
        rows = lax.broadcasted_iota(I32, xs_ref.shape, 0)
        lo, hi = _unpack(jnp.where(rows < nv, xs_ref[...], 0))
        lo = lo.astype(BF16)
        hi = hi.astype(BF16)
        g = _dot(lo, wg_ref[0, 0:HALF, :]) + _dot(hi, wg_ref[0, HALF:D_MODEL, :])
        u = _dot(lo, wu_ref[0, 0:HALF, :]) + _dot(hi, wu_ref[0, HALF:D_MODEL, :])
        a = (g * _sigmoid(g) * u).astype(BF16)
        y = _dot(a, wd_ref[0])
        ys_ref[...] = _pack(y[:, 0:HALF], y[:, HALF:D_MODEL])


def _experts(block_e, n_valid, block_src, xs, w_gate, w_up, w_down):
    n_blocks = block_e.shape[0]
    grid_spec = pltpu.PrefetchScalarGridSpec(
        num_scalar_prefetch=3,
        grid=(n_blocks,),
        in_specs=[pl.BlockSpec((MOE_BLOCK, HALF), lambda b, be, nv, src: (src[b], 0)),
                  pl.BlockSpec((1, D_MODEL, EXPERT_DIM), lambda b, be, nv, src: (be[b], 0, 0)),
                  pl.BlockSpec((1, D_MODEL, EXPERT_DIM), lambda b, be, nv, src: (be[b], 0, 0)),
                  pl.BlockSpec((1, EXPERT_DIM, D_MODEL), lambda b, be, nv, src: (be[b], 0, 0))],
        out_specs=pl.BlockSpec((MOE_BLOCK, HALF), lambda b, be, nv, src: (src[b], 0)),
    )
    return pl.pallas_call(
        _expert_kernel,
        grid_spec=grid_spec,
        out_shape=jax.ShapeDtypeStruct(xs.shape, I32),
        compiler_params=pltpu.CompilerParams(
            dimension_semantics=("arbitrary",), vmem_limit_bytes=VMEM_LIMIT),
        name="experts",
    )(block_e, n_valid, block_src, xs, w_gate, w_up, w_down)


def _combine_kernel(pos_ref, xp_ref, wt_ref, p_ref, gple_ref, wpg_ref, wpp_ref, gpost_ref, ys_ref,
                    out_ref, ybuf, sem):
    tm = xp_ref.shape[0]

    def issue(t, carry):
        for k in range(TOP_K):
            _row_copy(ys_ref, pos_ref[k, t], ybuf.at[k], t, sem).start()
        return carry

    lax.fori_loop(0, tm, issue, 0)

    proj = _rms(_dot(p_ref[...].astype(BF16), wpp_ref[...]), gpost_ref[...])

    def drain(t, carry):
        for k in range(TOP_K):
            _row_copy(ys_ref, pos_ref[k, t], ybuf.at[k], t, sem).wait()
        return carry

    lax.fori_loop(0, tm, drain, 0)

    xp = xp_ref[...]
    acc_lo = xp[:, 0:HALF]
    acc_hi = xp[:, HALF:D_MODEL]
    wt = wt_ref[...]
    for k in range(TOP_K):
        lo, hi = _unpack(ybuf[k])
        wk = wt[:, k:k + 1]
        acc_lo = acc_lo + wk * lo
        acc_hi = acc_hi + wk * hi
    x2 = jnp.concatenate([acc_lo, acc_hi], axis=1)
    gate = _sigmoid(_dot(_rms(x2, gple_ref[...]).astype(BF16), wpg_ref[...]))
    out_ref[...] = x2 + gate * proj


def _combine(pos, xp, wt_t, p2d, g_ple, w_ple_gate, w_ple_proj, g_ple_post, ys):
    t = xp.shape[0]
    tm = TM_ROW
    operands = (pos, xp, wt_t, p2d, g_ple.reshape(1, D_MODEL), w_ple_gate.astype(BF16),
                w_ple_proj.astype(BF16), g_ple_post.reshape(1, D_MODEL), ys)
    in_specs = [pl.BlockSpec((TOP_K, tm), lambda i: (0, i), memory_space=pltpu.SMEM),
                pl.BlockSpec((tm, D_MODEL), lambda i: (i, 0)),
                pl.BlockSpec((tm, TOP_K), lambda i: (i, 0)),
                pl.BlockSpec((tm, PLE_DIM), lambda i: (i, 0)),
                _const_spec((1, D_MODEL)),
                _const_spec((D_MODEL, D_MODEL)),
                _const_spec((PLE_DIM, D_MODEL)),
                _const_spec((1, D_MODEL)),
                pl.BlockSpec(memory_space=pl.ANY)]
    return pl.pallas_call(
        _combine_kernel,
        grid=(t // tm,),
        in_specs=in_specs,
        out_specs=pl.BlockSpec((tm, D_MODEL), lambda i: (i, 0)),
        out_shape=jax.ShapeDtypeStruct((t, D_MODEL), F32),
        scratch_shapes=[pltpu.VMEM((TOP_K, tm, HALF), I32), pltpu.SemaphoreType.DMA],
        compiler_params=pltpu.CompilerParams(
            dimension_semantics=("arbitrary",), vmem_limit_bytes=VMEM_LIMIT),
        name="combine",
    )(*operands)


def _layer(x2d, p2d, batch, seq, g_mix, w_in, w_pool_mix, pool_scale, w_branch_a, q_gain, k_gain,
           attn_sinks, w_branch_b, w_gate, b_gate, w_out, g_ffn, w_router, router_bias, w_exp_gate,
           w_exp_up, w_exp_down, w_sh_gate, w_sh_up, w_sh_down, g_ple, w_ple_gate, w_ple_proj,
           g_ple_post):
    t = x2d.shape[0]
    x1 = _mixer(x2d, batch, seq, g_mix, w_in, w_pool_mix, pool_scale, w_branch_a, q_gain, k_gain,
                attn_sinks, w_branch_b, w_gate, b_gate, w_out)
    xp, h2p, idx8, wt8, rk8, cnt = _ffn_pre(x1, g_ffn, w_sh_gate, w_sh_up, w_sh_down, w_router,
                                            router_bias)

    counts = cnt[:, 0].astype(I32)
    padded = (counts + MOE_BLOCK - 1) // MOE_BLOCK * MOE_BLOCK
    pend = jnp.cumsum(padded)
    pstart = pend - padded
    e_ids = jnp.arange(N_EXPERTS, dtype=I32)
    pos = rk8 + jnp.sum(jnp.where(idx8[None] == e_ids[:, None, None], pstart[:, None, None], 0), axis=0)
    n_blocks = -(-(t * TOP_K) // MOE_BLOCK) + N_EXPERTS
    n_used = pend[-1] // MOE_BLOCK
    block_src = jnp.minimum(jnp.arange(n_blocks, dtype=I32), n_used - 1)
    blk_row = block_src * MOE_BLOCK
    block_e = jnp.minimum(jnp.searchsorted(pend, blk_row, side='right'), N_EXPERTS - 1).astype(I32)
    n_valid = jnp.clip(counts[block_e] - (blk_row - pstart[block_e]), 0, MOE_BLOCK)
    n_valid = jnp.where(jnp.arange(n_blocks) < n_used, n_valid, 0).astype(I32)

    xs = _dispatch(pos, h2p, n_blocks * MOE_BLOCK)
    ys = _experts(block_e, n_valid, block_src, xs, w_exp_gate.astype(BF16), w_exp_up.astype(BF16),
                  w_exp_down.astype(BF16))
    return _combine(pos, xp, wt8.T, p2d, g_ple, w_ple_gate, w_ple_proj, g_ple_post, ys)


def kernel(x, p, g_mix, w_in, w_pool_mix, pool_scale, w_branch_a, q_gain, k_gain, attn_sinks,
           w_branch_b, w_gate, b_gate, w_out, g_ffn, w_router, router_bias, w_exp_gate, w_exp_up,
           w_exp_down, w_sh_gate, w_sh_up, w_sh_down, g_ple, w_ple_gate, w_ple_proj, g_ple_post):
    batch, seq, d = x.shape
    depth = p.shape[0]
    x2d = x.reshape(batch * seq, d)
    for i in range(depth):
        x2d = _layer(x2d, p[i].reshape(batch * seq, PLE_DIM), batch, seq, g_mix[i], w_in[i],
                     w_pool_mix[i], pool_scale[i], w_branch_a[i], q_gain[i], k_gain[i], attn_sinks[i],
                     w_branch_b[i], w_gate[i], b_gate[i], w_out[i], g_ffn[i], w_router[i],
                     router_bias[i], w_exp_gate[i], w_exp_up[i], w_exp_down[i], w_sh_gate[i],
                     w_sh_up[i], w_sh_down[i], g_ple[i], w_ple_gate[i], w_ple_proj[i], g_ple_post[i])
    return x2d.reshape(batch, seq, d)
```

```python
import functools

import numpy as np
import jax
import jax.numpy as jnp
from jax import lax
from jax.experimental import pallas as pl
from jax.experimental.pallas import tpu as pltpu

F32 = jnp.float32
BF16 = jnp.bfloat16
I32 = jnp.int32

D_MODEL = 1024
PLE_DIM = 256
POOL_WINDOWS = (2, 4, 8, 16)
POOL_GROUP_DIM = 128
POOL_DIM = 512
N_HEADS = 8
N_KV_HEADS = 2
Q_GROUP = 4
HEAD_DIM = 64
HEAD_PAD = 128
Q_DIM = 512
KV_DIM = 128
ATTN_BLOCK = 128
N_EXPERTS = 64
N_EXPERT_GROUPS = 8
GROUP_SIZE = 8
TOPK_GROUPS = 4
TOP_K = 8
EXPERT_DIM = 256
ROUTED_SCALE = 2.5
MOE_BLOCK = 256
EPS = 1e-6
HALF = D_MODEL // 2
MASKED = -1e30

COL_Q = POOL_DIM
COL_K = COL_Q + N_HEADS * HEAD_PAD
COL_VLO = COL_K + N_KV_HEADS * HEAD_PAD
COL_VHI = COL_VLO + N_KV_HEADS * HEAD_PAD
IN_PAD = COL_VHI + N_KV_HEADS * HEAD_PAD

TM_MIX = 512
TM_FFN = 512
TM_ROW = 256
POOL_TAIL = 8
VMEM_LIMIT = 56 * 1024 * 1024

_NT = (((1,), (1,)), ((), ()))


def _dot(a, b):
    return jnp.dot(a, b, preferred_element_type=F32)


def _rms(x, g):
    ms = jnp.mean(x * x, axis=-1, keepdims=True)
    return x * lax.rsqrt(ms + EPS) * g


def _sigmoid(x):
    return 1.0 / (1.0 + jnp.exp(-x))


def _pack(lo, hi):
    lo_bits = lax.bitcast_convert_type(lo.astype(BF16).astype(F32), I32)
    hi_bits = lax.bitcast_convert_type(hi.astype(BF16).astype(F32), I32)
    return (hi_bits & jnp.int32(-65536)) | lax.shift_right_logical(lo_bits, jnp.int32(16))


def _unpack(w):
    lo = lax.bitcast_convert_type(lax.shift_left(w, jnp.int32(16)), F32)
    hi = lax.bitcast_convert_type(w & jnp.int32(-65536), F32)
    return lo, hi


def _mixer_kernel(x_ref, gmix_ref, win_ref, qg_ref, kg_ref, bias_ref, sink_ref, wmix_ref,
                  pscale_ref, wa_ref, wb_ref, wg_ref, bg_ref, wout_ref,
                  x1_ref, kbuf, vlo_buf, vhi_buf, pbuf, zb_buf):
    s = pl.program_id(1)
    tm = x_ref.shape[0]

    @pl.when(s == 0)
    def _():
        kbuf[0:ATTN_BLOCK, :] = jnp.zeros((ATTN_BLOCK, kbuf.shape[1]), BF16)
        vlo_buf[0:ATTN_BLOCK, :] = jnp.zeros((ATTN_BLOCK, vlo_buf.shape[1]), BF16)
        vhi_buf[0:ATTN_BLOCK, :] = jnp.zeros((ATTN_BLOCK, vhi_buf.shape[1]), BF16)
        pbuf[:, 0:POOL_TAIL, :] = jnp.zeros((pbuf.shape[0], POOL_TAIL, POOL_DIM), F32)

    x = x_ref[...]
    hb = _rms(x, gmix_ref[...]).astype(BF16)
    u = _dot(hb, win_ref[...])

    up = u[:, 0:POOL_DIM]
    t0 = POOL_TAIL
    pbuf[0, t0:t0 + tm, :] = up
    s2 = up + pbuf[0, t0 - 1:t0 - 1 + tm, :]
    pbuf[1, t0:t0 + tm, 128:512] = s2[:, 128:512]
    s4 = s2[:, 128:512] + pbuf[1, t0 - 2:t0 - 2 + tm, 128:512]
    pbuf[2, t0:t0 + tm, 256:512] = s4[:, 128:384]
    s8 = s4[:, 128:384] + pbuf[2, t0 - 4:t0 - 4 + tm, 256:512]
    pbuf[3, t0:t0 + tm, 384:512] = s8[:, 128:256]
    s16 = s8[:, 128:256] + pbuf[3, t0 - 8:t0 - 8 + tm, 384:512]
    for lvl in range(4):
        pbuf[lvl, 0:POOL_TAIL, 128 * lvl:512] = pbuf[lvl, tm:tm + POOL_TAIL, 128 * lvl:512]
    wsums = (s2[:, 0:128], s4[:, 0:128], s8[:, 0:128], s16)
    tpos = (s * tm).astype(F32) + lax.broadcasted_iota(I32, (tm, 1), 0).astype(F32)
    za_parts = []
    for g, w in enumerate(POOL_WINDOWS):
        inv_count = 1.0 / jnp.minimum(tpos + 1.0, float(w))
        pooled = wsums[g] * inv_count - up[:, 128 * g:128 * (g + 1)]
        mixed = _dot(pooled.astype(BF16), wmix_ref[g]) * pscale_ref[:, 128 * g:128 * (g + 1)]
        za_parts.append(mixed.astype(BF16))
    za = jnp.concatenate(za_parts, axis=1)

    qg = qg_ref[...]
    qn = []
    for h in range(N_HEADS):
        qh = u[:, COL_Q + HEAD_PAD * h:COL_Q + HEAD_PAD * (h + 1)]
        ms = jnp.sum(qh * qh, axis=-1, keepdims=True) * (1.0 / HEAD_DIM)
        qn.append((qh * lax.rsqrt(ms + EPS) * qg).astype(BF16))
    kg = kg_ref[...]
    for kv in range(N_KV_HEADS):
        kh = u[:, COL_K + HEAD_PAD * kv:COL_K + HEAD_PAD * (kv + 1)]
        ms = jnp.sum(kh * kh, axis=-1, keepdims=True) * (1.0 / HEAD_DIM)
        kbuf[ATTN_BLOCK:ATTN_BLOCK + tm, HEAD_PAD * kv:HEAD_PAD * (kv + 1)] = (
            kh * lax.rsqrt(ms + EPS) * kg).astype(BF16)
    vlo_buf[ATTN_BLOCK:ATTN_BLOCK + tm, :] = u[:, COL_VLO:COL_VHI].astype(BF16)
    vhi_buf[ATTN_BLOCK:ATTN_BLOCK + tm, :] = u[:, COL_VHI:IN_PAD].astype(BF16)

    first = jnp.where(s == 0, 1, 0)
    for n in range(tm // ATTN_BLOCK):
        r0 = ATTN_BLOCK * n
        for kv in range(N_KV_HEADS):
            c0 = HEAD_PAD * kv
            qs = jnp.concatenate([qn[kv * Q_GROUP + g][r0:r0 + ATTN_BLOCK] for g in range(Q_GROUP)],
                                 axis=0)
            kk = kbuf[r0:r0 + 2 * ATTN_BLOCK, c0:c0 + HEAD_PAD]
            sc = lax.dot_general(qs, kk, _NT, preferred_element_type=F32)
            sc = sc + (bias_ref[first, kv] if n == 0 else bias_ref[0, kv])
            sink = sink_ref[kv]
            m = jnp.maximum(jnp.max(sc, axis=-1, keepdims=True), sink)
            e = jnp.exp(sc - m)
            den = jnp.sum(e, axis=-1, keepdims=True) + jnp.exp(sink - m)
            p = (e * (1.0 / den)).astype(BF16)
            vl = vlo_buf[r0:r0 + 2 * ATTN_BLOCK, c0:c0 + HEAD_PAD]
            vh = vhi_buf[r0:r0 + 2 * ATTN_BLOCK, c0:c0 + HEAD_PAD]
            for pair in range(2):
                pr = 2 * pair * ATTN_BLOCK
                o = _dot(p[pr:pr + ATTN_BLOCK], vl) + _dot(p[pr + ATTN_BLOCK:pr + 2 * ATTN_BLOCK], vh)
                col = kv * Q_GROUP * HEAD_DIM + pair * HEAD_PAD
                zb_buf[r0:r0 + ATTN_BLOCK, col:col + HEAD_PAD] = o.astype(BF16)
    kbuf[0:ATTN_BLOCK, :] = kbuf[tm:tm + ATTN_BLOCK, :]
    vlo_buf[0:ATTN_BLOCK, :] = vlo_buf[tm:tm + ATTN_BLOCK, :]
    vhi_buf[0:ATTN_BLOCK, :] = vhi_buf[tm:tm + ATTN_BLOCK, :]

    y_a = _dot(za, wa_ref[...])
    y_b = _dot(zb_buf[...], wb_ref[...])
    g_a = _sigmoid(_dot(hb, wg_ref[:, 0:D_MODEL]) + bg_ref[:, 0:D_MODEL])
    merged = g_a * y_a
    g_b = _sigmoid(_dot(hb, wg_ref[:, D_MODEL:2 * D_MODEL]) + bg_ref[:, D_MODEL:2 * D_MODEL])
    merged = (merged + g_b * y_b).astype(BF16)
    x1_ref[...] = x + _dot(merged, wout_ref[...])


def _attn_tables(attn_sinks):
    slopes = 2.0 ** (-8.0 * (np.arange(N_HEADS) + 1) / N_HEADS)
    i = np.arange(ATTN_BLOCK)[:, None]
    j = np.arange(2 * ATTN_BLOCK)[None, :]
    dist = ATTN_BLOCK + i - j
    in_window = (dist >= 0) & (dist < ATTN_BLOCK)
    bias = np.empty((2, N_KV_HEADS, Q_GROUP * ATTN_BLOCK, 2 * ATTN_BLOCK), np.float32)
    for first in range(2):
        ok = in_window & ((j >= ATTN_BLOCK) if first else True)
        for kv in range(N_KV_HEADS):
            for g in range(Q_GROUP):
                sl = np.float32(slopes[kv * Q_GROUP + g])
                val = -(sl * dist.astype(np.float32))
                bias[first, kv, g * ATTN_BLOCK:(g + 1) * ATTN_BLOCK] = np.where(ok, val, MASKED)
    sink = jnp.repeat(attn_sinks.astype(F32).reshape(N_KV_HEADS, Q_GROUP), ATTN_BLOCK, axis=1)
    return jnp.asarray(bias), sink.reshape(N_KV_HEADS, Q_GROUP * ATTN_BLOCK, 1)


def _pad_heads(w, n_heads, offset=0):
    k = w.shape[0]
    w = w.reshape(k, n_heads, HEAD_DIM)
    w = jnp.pad(w, ((0, 0), (0, 0), (offset, HEAD_PAD - HEAD_DIM - offset)))
    return w.reshape(k, n_heads * HEAD_PAD)


def _const_spec(shape):
    nd = len(shape)
    return pl.BlockSpec(shape, lambda *_: (0,) * nd)


def _mixer(x2d, batch, seq, g_mix, w_in, w_pool_mix, pool_scale, w_branch_a, q_gain, k_gain,
           attn_sinks, w_branch_b, w_gate, b_gate, w_out):
    t = x2d.shape[0]
    tm = TM_MIX
    ns = seq // tm
    w_q = w_in[:, POOL_DIM:POOL_DIM + Q_DIM]
    w_k = w_in[:, POOL_DIM + Q_DIM:POOL_DIM + Q_DIM + KV_DIM]
    w_v = w_in[:, POOL_DIM + Q_DIM + KV_DIM:]
    win_p = jnp.concatenate([w_in[:, :POOL_DIM], _pad_heads(w_q, N_HEADS), _pad_heads(w_k, N_KV_HEADS),
                             _pad_heads(w_v, N_KV_HEADS), _pad_heads(w_v, N_KV_HEADS, HEAD_DIM)],
                            axis=1).astype(BF16)
    qg = jnp.pad(q_gain * (HEAD_DIM ** -0.5), (0, HEAD_PAD - HEAD_DIM)).reshape(1, HEAD_PAD)
    kg = jnp.pad(k_gain, (0, HEAD_PAD - HEAD_DIM)).reshape(1, HEAD_PAD)
    bias, sink = _attn_tables(attn_sinks)
    operands = (x2d, g_mix.reshape(1, D_MODEL), win_p, qg, kg, bias, sink, w_pool_mix.astype(BF16),
                pool_scale.reshape(1, POOL_DIM), w_branch_a.astype(BF16), w_branch_b.astype(BF16),
                w_gate.astype(BF16), b_gate.reshape(1, 2 * D_MODEL), w_out.astype(BF16))
    in_specs = [pl.BlockSpec((tm, D_MODEL), lambda b, s: (b * ns + s, 0))]
    in_specs += [_const_spec(op.shape) for op in operands[1:]]
    return pl.pallas_call(
        _mixer_kernel,
        grid=(batch, ns),
        in_specs=in_specs,
        out_specs=pl.BlockSpec((tm, D_MODEL), lambda b, s: (b * ns + s, 0)),
        out_shape=jax.ShapeDtypeStruct((t, D_MODEL), F32),
        scratch_shapes=[
            pltpu.VMEM((ATTN_BLOCK + tm, N_KV_HEADS * HEAD_PAD), BF16),
            pltpu.VMEM((ATTN_BLOCK + tm, N_KV_HEADS * HEAD_PAD), BF16),
            pltpu.VMEM((ATTN_BLOCK + tm, N_KV_HEADS * HEAD_PAD), BF16),
            pltpu.VMEM((4, POOL_TAIL + tm, POOL_DIM), F32),
            pltpu.VMEM((tm, Q_DIM), BF16),
        ],
        compiler_params=pltpu.CompilerParams(
            dimension_semantics=("arbitrary", "arbitrary"), vmem_limit_bytes=VMEM_LIMIT),
        name="mixer",
    )(*operands)


def _ffn_pre_kernel(x1_ref, gffn_ref, wsg_ref, wsu_ref, wsd_ref, wrh_ref, wrl_ref, rb_ref, tri_ref,
                    xp_ref, h2p_ref, idx_ref, wt_ref, rk_ref, cnt_ref, run_ref):
    i = pl.program_id(0)
    tm = x1_ref.shape[0]

    @pl.when(i == 0)
    def _():
        run_ref[...] = jnp.zeros_like(run_ref)

    x1 = x1_ref[...]
    h2 = _rms(x1, gffn_ref[...])
    hb = h2.astype(BF16)
    g = _dot(hb, wsg_ref[...])
    a = (g * _sigmoid(g) * _dot(hb, wsu_ref[...])).astype(BF16)
    xp_ref[...] = x1 + _dot(a, wsd_ref[...])
    h2p_ref[...] = _pack(h2[:, 0:HALF], h2[:, HALF:D_MODEL])

    lo = (h2 - hb.astype(F32)).astype(BF16)
    wrh = wrh_ref[...]
    logits = (lax.dot_general(wrh, hb, _NT, preferred_element_type=F32)
              + lax.dot_general(wrl_ref[...], hb, _NT, preferred_element_type=F32)
              + lax.dot_general(wrh, lo, _NT, preferred_element_type=F32))
    scores = _sigmoid(logits)
    choice = scores + rb_ref[...]

    neg = -jnp.inf
    sub8 = lax.broadcasted_iota(I32, (GROUP_SIZE, tm), 0).astype(F32)
    grp_rows = []
    for gi in range(N_EXPERT_GROUPS):
        cg = choice[GROUP_SIZE * gi:GROUP_SIZE * (gi + 1)]
        m1 = jnp.max(cg, axis=0, keepdims=True)
        first = jnp.min(jnp.where(cg == m1, sub8, float(GROUP_SIZE)), axis=0, keepdims=True)
        m2 = jnp.max(jnp.where(sub8 == first, neg, cg), axis=0, keepdims=True)
        grp_rows.append(m1 + m2)
    gs = jnp.concatenate(grp_rows, axis=0)
    beaten = jnp.zeros((N_EXPERT_GROUPS, tm), F32)
    for gi in range(N_EXPERT_GROUPS):
        row = grp_rows[gi]
        wins = (row > gs) | ((row == gs) & (sub8 > float(gi)))
        beaten = beaten + jnp.where(wins, 1.0, 0.0)
    gsel = beaten < float(TOPK_GROUPS)
    cm = jnp.concatenate(
        [jnp.where(gsel[gi:gi + 1], choice[GROUP_SIZE * gi:GROUP_SIZE * (gi + 1)], neg)
         for gi in range(N_EXPERT_GROUPS)], axis=0)

    iota_e = lax.broadcasted_iota(I32, (N_EXPERTS, tm), 0).astype(F32)
    idx_rows, s_rows = [], []
    sel = jnp.zeros((N_EXPERTS, tm), F32)
    for _ in range(TOP_K):
        m = jnp.max(cm, axis=0, keepdims=True)
        idx = jnp.min(jnp.where(cm == m, iota_e, float(N_EXPERTS)), axis=0, keepdims=True)
        oh = iota_e == idx
        s_rows.append(jnp.sum(jnp.where(oh, scores, 0.0), axis=0, keepdims=True))
        idx_rows.append(idx)
        cm = jnp.where(oh, neg, cm)
        sel = sel + jnp.where(oh, 1.0, 0.0)

    run = run_ref[:, 0:1]
    cum = _dot(sel.astype(BF16), tri_ref[...])
    before = run + cum - sel
    rk_rows = [jnp.sum(jnp.where(iota_e == idx, before, 0.0), axis=0, keepdims=True) for idx in idx_rows]
    new_run = run + jnp.sum(sel, axis=1, keepdims=True)
    run_ref[...] = jnp.broadcast_to(new_run, run_ref.shape)
    cnt_ref[...] = jnp.broadcast_to(new_run, cnt_ref.shape)

    ssum = s_rows[0]
    for r in range(1, TOP_K):
        ssum = ssum + s_rows[r]
    denom = ssum + 1e-20
    idx_ref[...] = jnp.concatenate(idx_rows, axis=0).astype(I32)
    wt_ref[...] = jnp.concatenate([sr / denom * ROUTED_SCALE for sr in s_rows], axis=0)
    rk_ref[...] = jnp.concatenate(rk_rows, axis=0).astype(I32)


def _ffn_pre(x1, g_ffn, w_sh_gate, w_sh_up, w_sh_down, w_router, router_bias):
    t = x1.shape[0]
    tm = TM_FFN
    wr_t = w_router.astype(F32).T
    wr_hi = wr_t.astype(BF16)
    wr_lo = (wr_t - wr_hi.astype(F32)).astype(BF16)
    tri = (np.arange(tm)[:, None] <= np.arange(tm)[None, :]).astype(np.float32)
    operands = (x1, g_ffn.reshape(1, D_MODEL), w_sh_gate.astype(BF16), w_sh_up.astype(BF16),
                w_sh_down.astype(BF16), wr_hi, wr_lo, router_bias.astype(F32).reshape(N_EXPERTS, 1),
                jnp.asarray(tri, BF16))
    in_specs = [pl.BlockSpec((tm, D_MODEL), lambda i: (i, 0))]
    in_specs += [_const_spec(op.shape) for op in operands[1:]]
    row8 = pl.BlockSpec((TOP_K, tm), lambda i: (0, i))
    return pl.pallas_call(
        _ffn_pre_kernel,
        grid=(t // tm,),
        in_specs=in_specs,
        out_specs=[pl.BlockSpec((tm, D_MODEL), lambda i: (i, 0)),
                   pl.BlockSpec((tm, HALF), lambda i: (i, 0)),
                   row8, row8, row8,
                   pl.BlockSpec((N_EXPERTS, 128), lambda i: (0, 0))],
        out_shape=[jax.ShapeDtypeStruct((t, D_MODEL), F32),
                   jax.ShapeDtypeStruct((t, HALF), I32),
                   jax.ShapeDtypeStruct((TOP_K, t), I32),
                   jax.ShapeDtypeStruct((TOP_K, t), F32),
                   jax.ShapeDtypeStruct((TOP_K, t), I32),
                   jax.ShapeDtypeStruct((N_EXPERTS, 128), F32)],
        scratch_shapes=[pltpu.VMEM((N_EXPERTS, 128), F32)],
        compiler_params=pltpu.CompilerParams(
            dimension_semantics=("arbitrary",), vmem_limit_bytes=VMEM_LIMIT),
        name="ffn_pre",
    )(*operands)


def _row_copy(src, src_row, dst, dst_row, sem):
    return pltpu.make_async_copy(src.at[pl.ds(src_row, 1)], dst.at[pl.ds(dst_row, 1)], sem)


def _dispatch_kernel(pos_ref, h2p_ref, xs_ref, sem):
    tm = h2p_ref.shape[0]

    def issue(t, carry):
        for k in range(TOP_K):
            _row_copy(h2p_ref, t, xs_ref, pos_ref[k, t], sem).start()
        return carry

    lax.fori_loop(0, tm, issue, 0)

    def drain(t, carry):
        for k in range(TOP_K):
            _row_copy(h2p_ref, t, xs_ref, pos_ref[k, t], sem).wait()
        return carry

    lax.fori_loop(0, tm, drain, 0)


def _dispatch(pos, h2p, n_rows):
    t = h2p.shape[0]
    tm = TM_ROW
    return pl.pallas_call(
        _dispatch_kernel,
        grid=(t // tm,),
        in_specs=[pl.BlockSpec((TOP_K, tm), lambda i: (0, i), memory_space=pltpu.SMEM),
                  pl.BlockSpec((tm, HALF), lambda i: (i, 0))],
        out_specs=pl.BlockSpec(memory_space=pl.ANY),
        out_shape=jax.ShapeDtypeStruct((n_rows, HALF), I32),
        scratch_shapes=[pltpu.SemaphoreType.DMA],
        compiler_params=pltpu.CompilerParams(dimension_semantics=("arbitrary",)),
        name="dispatch",
    )(pos, h2p)


def _expert_kernel(be_ref, nv_ref, src_ref, xs_ref, wg_ref, wu_ref, wd_ref, ys_ref):
    b = pl.program_id(0)
    nv = nv_ref[b]

    @pl.when(nv > 0)
    def _():
        rows = lax.broadcasted_iota(I32, xs_ref.shape, 0)
        lo, hi = _unpack(jnp.where(rows < nv, xs_ref[...], 0))
        lo = lo.astype(BF16)
        hi = hi.astype(BF16)
        g = _dot(lo, wg_ref[0, 0:HALF, :]) + _dot(hi, wg_ref[0, HALF:D_MODEL, :])
        u = _dot(lo, wu_ref[0, 0:HALF, :]) + _dot(hi, wu_ref[0, HALF:D_MODEL, :])
        a = (g * _sigmoid(g) * u).astype(BF16)
        y = _dot(a, wd_ref[0])
        ys_ref[...] = _pack(y[:, 0:HALF], y[:, HALF:D_MODEL])


def _experts(block_e, n_valid, block_src, xs, w_gate, w_up, w_down):
    n_blocks = block_e.shape[0]
    grid_spec = pltpu.PrefetchScalarGridSpec(
        num_scalar_prefetch=3,
        grid=(n_blocks,),
        in_specs=[pl.BlockSpec((MOE_BLOCK, HALF), lambda b, be, nv, src: (src[b], 0)),
                  pl.BlockSpec((1, D_MODEL, EXPERT_DIM), lambda b, be, nv, src: (be[b], 0, 0)),
                  pl.BlockSpec((1, D_MODEL, EXPERT_DIM), lambda b, be, nv, src: (be[b], 0, 0)),
                  pl.BlockSpec((1, EXPERT_DIM, D_MODEL), lambda b, be, nv, src: (be[b], 0, 0))],
        out_specs=pl.BlockSpec((MOE_BLOCK, HALF), lambda b, be, nv, src: (src[b], 0)),
    )
    return pl.pallas_call(
        _expert_kernel,
        grid_spec=grid_spec,
        out_shape=jax.ShapeDtypeStruct(xs.shape, I32),
        compiler_params=pltpu.CompilerParams(
            dimension_semantics=("arbitrary",), vmem_limit_bytes=VMEM_LIMIT),
        name="experts",
    )(block_e, n_valid, block_src, xs, w_gate, w_up, w_down)


def _combine_kernel(pos_ref, xp_ref, wt_ref, p_ref, gple_ref, wpg_ref, wpp_ref, gpost_ref, ys_ref,
                    out_ref, ybuf, sem):
    tm = xp_ref.shape[0]

    def issue(t, carry):
        for k in range(TOP_K):
            _row_copy(ys_ref, pos_ref[k, t], ybuf.at[k], t, sem).start()
        return carry

    lax.fori_loop(0, tm, issue, 0)

    proj = _rms(_dot(p_ref[...].astype(BF16), wpp_ref[...]), gpost_ref[...])

    def drain(t, carry):
        for k in range(TOP_K):
            _row_copy(ys_ref, pos_ref[k, t], ybuf.at[k], t, sem).wait()
        return carry

    lax.fori_loop(0, tm, drain, 0)

    xp = xp_ref[...]
    acc_lo = xp[:, 0:HALF]
    acc_hi = xp[:, HALF:D_MODEL]
    wt = wt_ref[...]
    for k in range(TOP_K):
        lo, hi = _unpack(ybuf[k])
        wk = wt[:, k:k + 1]
        acc_lo = acc_lo + wk * lo
        acc_hi = acc_hi + wk * hi
    x2 = jnp.concatenate([acc_lo, acc_hi], axis=1)
    gate = _sigmoid(_dot(_rms(x2, gple_ref[...]).astype(BF16), wpg_ref[...]))
    out_ref[...] = x2 + gate * proj


def _combine(pos, xp, wt_t, p2d, g_ple, w_ple_gate, w_ple_proj, g_ple_post, ys):
    t = xp.shape[0]
    tm = TM_ROW
    operands = (pos, xp, wt_t, p2d, g_ple.reshape(1, D_MODEL), w_ple_gate.astype(BF16),
                w_ple_proj.astype(BF16), g_ple_post.reshape(1, D_MODEL), ys)
    in_specs = [pl.BlockSpec((TOP_K, tm), lambda i: (0, i), memory_space=pltpu.SMEM),
                pl.BlockSpec((tm, D_MODEL), lambda i: (i, 0)),
                pl.BlockSpec((tm, TOP_K), lambda i: (i, 0)),
                pl.BlockSpec((tm, PLE_DIM), lambda i: (i, 0)),
                _const_spec((1, D_MODEL)),
                _const_spec((D_MODEL, D_MODEL)),
                _const_spec((PLE_DIM, D_MODEL)),
                _const_spec((1, D_MODEL)),
                pl.BlockSpec(memory_space=pl.ANY)]
    return pl.pallas_call(
        _combine_kernel,
        grid=(t // tm,),
        in_specs=in_specs,
        out_specs=pl.BlockSpec((tm, D_MODEL), lambda i: (i, 0)),
        out_shape=jax.ShapeDtypeStruct((t, D_MODEL), F32),
        scratch_shapes=[pltpu.VMEM((TOP_K, tm, HALF), I32), pltpu.SemaphoreType.DMA],
        compiler_params=pltpu.CompilerParams(
            dimension_semantics=("arbitrary",), vmem_limit_bytes=VMEM_LIMIT),
        name="combine",
    )(*operands)


def _layer(x2d, p2d, batch, seq, g_mix, w_in, w_pool_mix, pool_scale, w_branch_a, q_gain, k_gain,
           attn_sinks, w_branch_b, w_gate, b_gate, w_out, g_ffn, w_router, router_bias, w_exp_gate,
           w_exp_up, w_exp_down, w_sh_gate, w_sh_up, w_sh_down, g_ple, w_ple_gate, w_ple_proj,
           g_ple_post):
    t = x2d.shape[0]
    x1 = _mixer(x2d, batch, seq, g_mix, w_in, w_pool_mix, pool_scale, w_branch_a, q_gain, k_gain,
                attn_sinks, w_branch_b, w_gate, b_gate, w_out)
    xp, h2p, idx8, wt8, rk8, cnt = _ffn_pre(x1, g_ffn, w_sh_gate, w_sh_up, w_sh_down, w_router,
                                            router_bias)

    counts = cnt[:, 0].astype(I32)
    padded = (counts + MOE_BLOCK - 1) // MOE_BLOCK * MOE_BLOCK
    pend = jnp.cumsum(padded)
    pstart = pend - padded
    e_ids = jnp.arange(N_EXPERTS, dtype=I32)
    pos = rk8 + jnp.sum(jnp.where(idx8[None] == e_ids[:, None, None], pstart[:, None, None], 0), axis=0)
    n_blocks = -(-(t * TOP_K) // MOE_BLOCK) + N_EXPERTS
    n_used = pend[-1] // MOE_BLOCK
    block_src = jnp.minimum(jnp.arange(n_blocks, dtype=I32), n_used - 1)
    blk_row = block_src * MOE_BLOCK
    block_e = jnp.minimum(jnp.sum(pend[None, :] <= blk_row[:, None], axis=1), N_EXPERTS - 1).astype(I32)
    n_valid = jnp.clip(counts[block_e] - (blk_row - pstart[block_e]), 0, MOE_BLOCK)
    n_valid = jnp.where(jnp.arange(n_blocks) < n_used, n_valid, 0).astype(I32)

    xs = _dispatch(pos, h2p, n_blocks * MOE_BLOCK)
    ys = _experts(block_e, n_valid, block_src, xs, w_exp_gate.astype(BF16), w_exp_up.astype(BF16),
                  w_exp_down.astype(BF16))
    return _combine(pos, xp, wt8.T, p2d, g_ple, w_ple_gate, w_ple_proj, g_ple_post, ys)


def kernel(x, p, g_mix, w_in, w_pool_mix, pool_scale, w_branch_a, q_gain, k_gain, attn_sinks,
           w_branch_b, w_gate, b_gate, w_out, g_ffn, w_router, router_bias, w_exp_gate, w_exp_up,
           w_exp_down, w_sh_gate, w_sh_up, w_sh_down, g_ple, w_ple_gate, w_ple_proj, g_ple_post):
    batch, seq, d = x.shape
    depth = p.shape[0]
    x2d = x.reshape(batch * seq, d)
    for i in range(depth):
        x2d = _layer(x2d, p[i].reshape(batch * seq, PLE_DIM), batch, seq, g_mix[i], w_in[i],
                     w_pool_mix[i], pool_scale[i], w_branch_a[i], q_gain[i], k_gain[i], attn_sinks[i],
                     w_branch_b[i], w_gate[i], b_gate[i], w_out[i], g_ffn[i], w_router[i],
                     router_bias[i], w_exp_gate[i], w_exp_up[i], w_exp_down[i], w_sh_gate[i],
                     w_sh_up[i], w_sh_down[i], g_ple[i], w_ple_gate[i], w_ple_proj[i], g_ple_post[i])
    return x2d.reshape(batch, seq, d)
```

```python
import functools

import numpy as np
import jax
import jax.numpy as jnp
from jax import lax
from jax.experimental import pallas as pl
from jax.experimental.pallas import tpu as pltpu
from jax.experimental.pallas import tpu_sc as plsc

F32 = jnp.float32
BF16 = jnp.bfloat16
I32 = jnp.int32

D_MODEL = 1024
PLE_DIM = 256
POOL_WINDOWS = (2, 4, 8, 16)
POOL_GROUP_DIM = 128
POOL_DIM = 512
N_HEADS = 8
N_KV_HEADS = 2
Q_GROUP = 4
HEAD_DIM = 64
HEAD_PAD = 128
Q_DIM = 512
KV_DIM = 128
ATTN_BLOCK = 128
N_EXPERTS = 64
N_EXPERT_GROUPS = 8
GROUP_SIZE = 8
TOPK_GROUPS = 4
TOP_K = 8
EXPERT_DIM = 256
ROUTED_SCALE = 2.5
MOE_BLOCK = 256
EPS = 1e-6
HALF = D_MODEL // 2
MASKED = -1e30

COL_Q = POOL_DIM
COL_K = COL_Q + N_HEADS * HEAD_PAD
COL_VLO = COL_K + N_KV_HEADS * HEAD_PAD
COL_VHI = COL_VLO + N_KV_HEADS * HEAD_PAD
IN_PAD = COL_VHI + N_KV_HEADS * HEAD_PAD

TM_MIX = 512
TM_FFN = 512
TM_COMBINE = 512
SC_SCATTER_ROWS = 128
SC_GATHER_ROWS = 64
POOL_TAIL = 8
VMEM_LIMIT = 56 * 1024 * 1024

_NT = (((1,), (1,)), ((), ()))


def _dot(a, b):
    return jnp.dot(a, b, preferred_element_type=F32)


def _rms(x, g):
    ms = jnp.mean(x * x, axis=-1, keepdims=True)
    return x * lax.rsqrt(ms + EPS) * g


def _sigmoid(x):
    return 1.0 / (1.0 + jnp.exp(-x))


def _pack(lo, hi):
    lo_bits = lax.bitcast_convert_type(lo.astype(BF16).astype(F32), I32)
    hi_bits = lax.bitcast_convert_type(hi.astype(BF16).astype(F32), I32)
    return (hi_bits & jnp.int32(-65536)) | lax.shift_right_logical(lo_bits, jnp.int32(16))


def _unpack(w):
    lo = lax.bitcast_convert_type(lax.shift_left(w, jnp.int32(16)), F32)
    hi = lax.bitcast_convert_type(w & jnp.int32(-65536), F32)
    return lo, hi


def _mixer_kernel(x_ref, gmix_ref, win_ref, qg_ref, kg_ref, bias_ref, sink_ref, wmix_ref,
                  pscale_ref, wa_ref, wb_ref, wg_ref, bg_ref, wout_ref,
                  x1_ref, kbuf, vlo_buf, vhi_buf, pbuf, zb_buf):
    s = pl.program_id(1)
    tm = x_ref.shape[0]

    @pl.when(s == 0)
    def _():
        kbuf[0:ATTN_BLOCK, :] = jnp.zeros((ATTN_BLOCK, kbuf.shape[1]), BF16)
        vlo_buf[0:ATTN_BLOCK, :] = jnp.zeros((ATTN_BLOCK, vlo_buf.shape[1]), BF16)
        vhi_buf[0:ATTN_BLOCK, :] = jnp.zeros((ATTN_BLOCK, vhi_buf.shape[1]), BF16)
        pbuf[:, 0:POOL_TAIL, :] = jnp.zeros((pbuf.shape[0], POOL_TAIL, POOL_DIM), F32)

    x = x_ref[...]
    hb = _rms(x, gmix_ref[...]).astype(BF16)
    u = _dot(hb, win_ref[...])

    up = u[:, 0:POOL_DIM]
    t0 = POOL_TAIL
    pbuf[0, t0:t0 + tm, :] = up
    s2 = up + pbuf[0, t0 - 1:t0 - 1 + tm, :]
    pbuf[1, t0:t0 + tm, 128:512] = s2[:, 128:512]
    s4 = s2[:, 128:512] + pbuf[1, t0 - 2:t0 - 2 + tm, 128:512]
    pbuf[2, t0:t0 + tm, 256:512] = s4[:, 128:384]
    s8 = s4[:, 128:384] + pbuf[2, t0 - 4:t0 - 4 + tm, 256:512]
    pbuf[3, t0:t0 + tm, 384:512] = s8[:, 128:256]
    s16 = s8[:, 128:256] + pbuf[3, t0 - 8:t0 - 8 + tm, 384:512]
    for lvl in range(4):
        pbuf[lvl, 0:POOL_TAIL, 128 * lvl:512] = pbuf[lvl, tm:tm + POOL_TAIL, 128 * lvl:512]
    wsums = (s2[:, 0:128], s4[:, 0:128], s8[:, 0:128], s16)
    tpos = (s * tm).astype(F32) + lax.broadcasted_iota(I32, (tm, 1), 0).astype(F32)
    za_parts = []
    for g, w in enumerate(POOL_WINDOWS):
        inv_count = 1.0 / jnp.minimum(tpos + 1.0, float(w))
        pooled = wsums[g] * inv_count - up[:, 128 * g:128 * (g + 1)]
        mixed = _dot(pooled.astype(BF16), wmix_ref[g]) * pscale_ref[:, 128 * g:128 * (g + 1)]
        za_parts.append(mixed.astype(BF16))
    za = jnp.concatenate(za_parts, axis=1)

    qg = qg_ref[...]
    qn = []
    for h in range(N_HEADS):
        qh = u[:, COL_Q + HEAD_PAD * h:COL_Q + HEAD_PAD * (h + 1)]
        ms = jnp.sum(qh * qh, axis=-1, keepdims=True) * (1.0 / HEAD_DIM)
        qn.append((qh * lax.rsqrt(ms + EPS) * qg).astype(BF16))
    kg = kg_ref[...]
    for kv in range(N_KV_HEADS):
        kh = u[:, COL_K + HEAD_PAD * kv:COL_K + HEAD_PAD * (kv + 1)]
        ms = jnp.sum(kh * kh, axis=-1, keepdims=True) * (1.0 / HEAD_DIM)
        kbuf[ATTN_BLOCK:ATTN_BLOCK + tm, HEAD_PAD * kv:HEAD_PAD * (kv + 1)] = (
            kh * lax.rsqrt(ms + EPS) * kg).astype(BF16)
    vlo_buf[ATTN_BLOCK:ATTN_BLOCK + tm, :] = u[:, COL_VLO:COL_VHI].astype(BF16)
    vhi_buf[ATTN_BLOCK:ATTN_BLOCK + tm, :] = u[:, COL_VHI:IN_PAD].astype(BF16)

    first = jnp.where(s == 0, 1, 0)
    for n in range(tm // ATTN_BLOCK):
        r0 = ATTN_BLOCK * n
        for kv in range(N_KV_HEADS):
            c0 = HEAD_PAD * kv
            qs = jnp.concatenate([qn[kv * Q_GROUP + g][r0:r0 + ATTN_BLOCK] for g in range(Q_GROUP)],
                                 axis=0)
            kk = kbuf[r0:r0 + 2 * ATTN_BLOCK, c0:c0 + HEAD_PAD]
            sc = lax.dot_general(qs, kk, _NT, preferred_element_type=F32)
            sc = sc + (bias_ref[first, kv] if n == 0 else bias_ref[0, kv])
            sink = sink_ref[kv]
            m = jnp.maximum(jnp.max(sc, axis=-1, keepdims=True), sink)
            e = jnp.exp(sc - m)
            den = jnp.sum(e, axis=-1, keepdims=True) + jnp.exp(sink - m)
            p = (e * (1.0 / den)).astype(BF16)
            vl = vlo_buf[r0:r0 + 2 * ATTN_BLOCK, c0:c0 + HEAD_PAD]
            vh = vhi_buf[r0:r0 + 2 * ATTN_BLOCK, c0:c0 + HEAD_PAD]
            for pair in range(2):
                pr = 2 * pair * ATTN_BLOCK
                o = _dot(p[pr:pr + ATTN_BLOCK], vl) + _dot(p[pr + ATTN_BLOCK:pr + 2 * ATTN_BLOCK], vh)
                col = kv * Q_GROUP * HEAD_DIM + pair * HEAD_PAD
                zb_buf[r0:r0 + ATTN_BLOCK, col:col + HEAD_PAD] = o.astype(BF16)
    kbuf[0:ATTN_BLOCK, :] = kbuf[tm:tm + ATTN_BLOCK, :]
    vlo_buf[0:ATTN_BLOCK, :] = vlo_buf[tm:tm + ATTN_BLOCK, :]
    vhi_buf[0:ATTN_BLOCK, :] = vhi_buf[tm:tm + ATTN_BLOCK, :]

    y_a = _dot(za, wa_ref[...])
    y_b = _dot(zb_buf[...], wb_ref[...])
    g_a = _sigmoid(_dot(hb, wg_ref[:, 0:D_MODEL]) + bg_ref[:, 0:D_MODEL])
    merged = g_a * y_a
    g_b = _sigmoid(_dot(hb, wg_ref[:, D_MODEL:2 * D_MODEL]) + bg_ref[:, D_MODEL:2 * D_MODEL])
    merged = (merged + g_b * y_b).astype(BF16)
    x1_ref[...] = x + _dot(merged, wout_ref[...])


def _attn_tables(attn_sinks):
    slopes = 2.0 ** (-8.0 * (np.arange(N_HEADS) + 1) / N_HEADS)
    i = np.arange(ATTN_BLOCK)[:, None]
    j = np.arange(2 * ATTN_BLOCK)[None, :]
    dist = ATTN_BLOCK + i - j
    in_window = (dist >= 0) & (dist < ATTN_BLOCK)
    bias = np.empty((2, N_KV_HEADS, Q_GROUP * ATTN_BLOCK, 2 * ATTN_BLOCK), np.float32)
    for first in range(2):
        ok = in_window & ((j >= ATTN_BLOCK) if first else True)
        for kv in range(N_KV_HEADS):
            for g in range(Q_GROUP):
                sl = np.float32(slopes[kv * Q_GROUP + g])
                val = -(sl * dist.astype(np.float32))
                bias[first, kv, g * ATTN_BLOCK:(g + 1) * ATTN_BLOCK] = np.where(ok, val, MASKED)
    sink = jnp.repeat(attn_sinks.astype(F32).reshape(N_KV_HEADS, Q_GROUP), ATTN_BLOCK, axis=1)
    return jnp.asarray(bias), sink.reshape(N_KV_HEADS, Q_GROUP * ATTN_BLOCK, 1)


def _pad_heads(w, n_heads, offset=0):
    k = w.shape[0]
    w = w.reshape(k, n_heads, HEAD_DIM)
    w = jnp.pad(w, ((0, 0), (0, 0), (offset, HEAD_PAD - HEAD_DIM - offset)))
    return w.reshape(k, n_heads * HEAD_PAD)


def _const_spec(shape):
    nd = len(shape)
    return pl.BlockSpec(shape, lambda *_: (0,) * nd)


def _mixer(x2d, batch, seq, g_mix, w_in, w_pool_mix, pool_scale, w_branch_a, q_gain, k_gain,
           attn_sinks, w_branch_b, w_gate, b_gate, w_out):
    t = x2d.shape[0]
    tm = TM_MIX
    ns = seq // tm
    w_q = w_in[:, POOL_DIM:POOL_DIM + Q_DIM]
    w_k = w_in[:, POOL_DIM + Q_DIM:POOL_DIM + Q_DIM + KV_DIM]
    w_v = w_in[:, POOL_DIM + Q_DIM + KV_DIM:]
    win_p = jnp.concatenate([w_in[:, :POOL_DIM], _pad_heads(w_q, N_HEADS), _pad_heads(w_k, N_KV_HEADS),
                             _pad_heads(w_v, N_KV_HEADS), _pad_heads(w_v, N_KV_HEADS, HEAD_DIM)],
                            axis=1).astype(BF16)
    qg = jnp.pad(q_gain * (HEAD_DIM ** -0.5), (0, HEAD_PAD - HEAD_DIM)).reshape(1, HEAD_PAD)
    kg = jnp.pad(k_gain, (0, HEAD_PAD - HEAD_DIM)).reshape(1, HEAD_PAD)
    bias, sink = _attn_tables(attn_sinks)
    operands = (x2d, g_mix.reshape(1, D_MODEL), win_p, qg, kg, bias, sink, w_pool_mix.astype(BF16),
                pool_scale.reshape(1, POOL_DIM), w_branch_a.astype(BF16), w_branch_b.astype(BF16),
                w_gate.astype(BF16), b_gate.reshape(1, 2 * D_MODEL), w_out.astype(BF16))
    in_specs = [pl.BlockSpec((tm, D_MODEL), lambda b, s: (b * ns + s, 0))]
    in_specs += [_const_spec(op.shape) for op in operands[1:]]
    return pl.pallas_call(
        _mixer_kernel,
        grid=(batch, ns),
        in_specs=in_specs,
        out_specs=pl.BlockSpec((tm, D_MODEL), lambda b, s: (b * ns + s, 0)),
        out_shape=jax.ShapeDtypeStruct((t, D_MODEL), F32),
        scratch_shapes=[
            pltpu.VMEM((ATTN_BLOCK + tm, N_KV_HEADS * HEAD_PAD), BF16),
            pltpu.VMEM((ATTN_BLOCK + tm, N_KV_HEADS * HEAD_PAD), BF16),
            pltpu.VMEM((ATTN_BLOCK + tm, N_KV_HEADS * HEAD_PAD), BF16),
            pltpu.VMEM((4, POOL_TAIL + tm, POOL_DIM), F32),
            pltpu.VMEM((tm, Q_DIM), BF16),
        ],
        compiler_params=pltpu.CompilerParams(
            dimension_semantics=("arbitrary", "arbitrary"), vmem_limit_bytes=VMEM_LIMIT),
        name="mixer",
    )(*operands)


def _ffn_pre_kernel(x1_ref, gffn_ref, wsg_ref, wsu_ref, wsd_ref, wrh_ref, wrl_ref, rb_ref, tri_ref,
                    xp_ref, h2p_ref, idx_ref, wt_ref, rk_ref, cnt_ref, run_ref):
    i = pl.program_id(0)
    tm = x1_ref.shape[0]

    @pl.when(i == 0)
    def _():
        run_ref[...] = jnp.zeros_like(run_ref)

    x1 = x1_ref[...]
    h2 = _rms(x1, gffn_ref[...])
    hb = h2.astype(BF16)
    g = _dot(hb, wsg_ref[...])
    a = (g * _sigmoid(g) * _dot(hb, wsu_ref[...])).astype(BF16)
    xp_ref[...] = x1 + _dot(a, wsd_ref[...])
    h2p_ref[...] = _pack(h2[:, 0:HALF], h2[:, HALF:D_MODEL])

    lo = (h2 - hb.astype(F32)).astype(BF16)
    wrh = wrh_ref[...]
    logits = (lax.dot_general(wrh, hb, _NT, preferred_element_type=F32)
              + lax.dot_general(wrl_ref[...], hb, _NT, preferred_element_type=F32)
              + lax.dot_general(wrh, lo, _NT, preferred_element_type=F32))
    scores = _sigmoid(logits)
    choice = scores + rb_ref[...]

    neg = -jnp.inf
    sub8 = lax.broadcasted_iota(I32, (GROUP_SIZE, tm), 0).astype(F32)
    grp_rows = []
    for gi in range(N_EXPERT_GROUPS):
        cg = choice[GROUP_SIZE * gi:GROUP_SIZE * (gi + 1)]
        m1 = jnp.max(cg, axis=0, keepdims=True)
        first = jnp.min(jnp.where(cg == m1, sub8, float(GROUP_SIZE)), axis=0, keepdims=True)
        m2 = jnp.max(jnp.where(sub8 == first, neg, cg), axis=0, keepdims=True)
        grp_rows.append(m1 + m2)
    gs = jnp.concatenate(grp_rows, axis=0)
    beaten = jnp.zeros((N_EXPERT_GROUPS, tm), F32)
    for gi in range(N_EXPERT_GROUPS):
        row = grp_rows[gi]
        wins = (row > gs) | ((row == gs) & (sub8 > float(gi)))
        beaten = beaten + jnp.where(wins, 1.0, 0.0)
    gsel = beaten < float(TOPK_GROUPS)
    cm = jnp.concatenate(
        [jnp.where(gsel[gi:gi + 1], choice[GROUP_SIZE * gi:GROUP_SIZE * (gi + 1)], neg)
         for gi in range(N_EXPERT_GROUPS)], axis=0)

    iota_e = lax.broadcasted_iota(I32, (N_EXPERTS, tm), 0).astype(F32)
    idx_rows, s_rows = [], []
    sel = jnp.zeros((N_EXPERTS, tm), F32)
    for _ in range(TOP_K):
        m = jnp.max(cm, axis=0, keepdims=True)
        idx = jnp.min(jnp.where(cm == m, iota_e, float(N_EXPERTS)), axis=0, keepdims=True)
        oh = iota_e == idx
        s_rows.append(jnp.sum(jnp.where(oh, scores, 0.0), axis=0, keepdims=True))
        idx_rows.append(idx)
        cm = jnp.where(oh, neg, cm)
        sel = sel + jnp.where(oh, 1.0, 0.0)

    run = run_ref[:, 0:1]
    cum = _dot(sel.astype(BF16), tri_ref[...])
    before = run + cum - sel
    rk_rows = [jnp.sum(jnp.where(iota_e == idx, before, 0.0), axis=0, keepdims=True) for idx in idx_rows]
    new_run = run + jnp.sum(sel, axis=1, keepdims=True)
    run_ref[...] = jnp.broadcast_to(new_run, run_ref.shape)
    cnt_ref[...] = jnp.broadcast_to(new_run, cnt_ref.shape)

    ssum = s_rows[0]
    for r in range(1, TOP_K):
        ssum = ssum + s_rows[r]
    denom = ssum + 1e-20
    idx_ref[...] = jnp.concatenate(idx_rows, axis=0).astype(I32)
    wt_ref[...] = jnp.concatenate([sr / denom * ROUTED_SCALE for sr in s_rows], axis=0)
    rk_ref[...] = jnp.concatenate(rk_rows, axis=0).astype(I32)


def _ffn_pre(x1, g_ffn, w_sh_gate, w_sh_up, w_sh_down, w_router, router_bias):
    t = x1.shape[0]
    tm = TM_FFN
    wr_t = w_router.astype(F32).T
    wr_hi = wr_t.astype(BF16)
    wr_lo = (wr_t - wr_hi.astype(F32)).astype(BF16)
    tri = (np.arange(tm)[:, None] <= np.arange(tm)[None, :]).astype(np.float32)
    operands = (x1, g_ffn.reshape(1, D_MODEL), w_sh_gate.astype(BF16), w_sh_up.astype(BF16),
                w_sh_down.astype(BF16), wr_hi, wr_lo, router_bias.astype(F32).reshape(N_EXPERTS, 1),
                jnp.asarray(tri, BF16))
    in_specs = [pl.BlockSpec((tm, D_MODEL), lambda i: (i, 0))]
    in_specs += [_const_spec(op.shape) for op in operands[1:]]
    row8 = pl.BlockSpec((TOP_K, tm), lambda i: (0, i))
    return pl.pallas_call(
        _ffn_pre_kernel,
        grid=(t // tm,),
        in_specs=in_specs,
        out_specs=[pl.BlockSpec((tm, D_MODEL), lambda i: (i, 0)),
                   pl.BlockSpec((tm, HALF), lambda i: (i, 0)),
                   row8, row8, row8,
                   pl.BlockSpec((N_EXPERTS, 128), lambda i: (0, 0))],
        out_shape=[jax.ShapeDtypeStruct((t, D_MODEL), F32),
                   jax.ShapeDtypeStruct((t, HALF), I32),
                   jax.ShapeDtypeStruct((TOP_K, t), I32),
                   jax.ShapeDtypeStruct((TOP_K, t), F32),
                   jax.ShapeDtypeStruct((TOP_K, t), I32),
                   jax.ShapeDtypeStruct((N_EXPERTS, 128), F32)],
        scratch_shapes=[pltpu.VMEM((N_EXPERTS, 128), F32)],
        compiler_params=pltpu.CompilerParams(
            dimension_semantics=("arbitrary",), vmem_limit_bytes=VMEM_LIMIT),
        name="ffn_pre",
    )(*operands)


def _sc_workers():
    info = plsc.get_sparse_core_info()
    return info.num_cores, info.num_cores * info.num_subcores


def _dispatch(pos, h2p, n_rows):
    t = h2p.shape[0]
    ch = SC_SCATTER_ROWS
    n_cores, n_workers = _sc_workers()
    n_ch = t // (n_workers * ch)
    pos3 = pos.reshape(TOP_K, t // ch, ch).transpose(1, 0, 2)

    def body(pos_hbm, h_hbm, xs_hbm, idx_v, rows_v, sem):
        wid = lax.axis_index("s") * n_cores + lax.axis_index("c")

        @pl.loop(0, n_ch)
        def _(c):
            chunk = wid * n_ch + c
            pltpu.sync_copy(h_hbm.at[pl.ds(chunk * ch, ch)], rows_v)
            pltpu.sync_copy(pos_hbm.at[chunk], idx_v)
            copies = [pltpu.async_copy(rows_v, xs_hbm.at[idx_v.at[k]], sem) for k in range(TOP_K)]
            for cp in copies:
                cp.wait()

    return pl.kernel(
        body,
        out_type=jax.ShapeDtypeStruct((n_rows, HALF), I32),
        mesh=plsc.VectorSubcoreMesh(core_axis_name="c", subcore_axis_name="s"),
        scratch_types=[pltpu.VMEM((TOP_K, ch), I32), pltpu.VMEM((ch, HALF), I32),
                       pltpu.SemaphoreType.DMA],
        name="dispatch",
    )(pos3, h2p)


def _gather_back(pos, ys):
    t = pos.shape[1]
    ch = SC_GATHER_ROWS
    n_cores, n_workers = _sc_workers()
    n_ch = t // (n_workers * ch)
    posg = pos.reshape(TOP_K, t // ch, ch).transpose(1, 0, 2).reshape(-1)

    def body(pos_hbm, ys_hbm, yt_hbm, idx_v, buf0, buf1, gsem, wsem0, wsem1):
        wid = lax.axis_index("s") * n_cores + lax.axis_index("c")
        bufs = (buf0, buf1)
        wsems = (wsem0, wsem1)

        @pl.loop(0, n_ch)
        def _(c):
            chunk = wid * n_ch + c
            pltpu.sync_copy(pos_hbm.at[pl.ds(chunk * (TOP_K * ch), TOP_K * ch)], idx_v)
            writes = [None, None]
            for k in range(TOP_K):
                b = k % 2
                if writes[b] is not None:
                    writes[b].wait()
                pltpu.async_copy(ys_hbm.at[idx_v.at[pl.ds(k * ch, ch)]], bufs[b], gsem).wait()
                writes[b] = pltpu.async_copy(bufs[b], yt_hbm.at[k, pl.ds(chunk * ch, ch)], wsems[b])
            writes[0].wait()
            writes[1].wait()

    return pl.kernel(
        body,
        out_type=jax.ShapeDtypeStruct((TOP_K, t, HALF), I32),
        mesh=plsc.VectorSubcoreMesh(core_axis_name="c", subcore_axis_name="s"),
        scratch_types=[pltpu.VMEM((TOP_K * ch,), I32), pltpu.VMEM((ch, HALF), I32),
                       pltpu.VMEM((ch, HALF), I32), pltpu.SemaphoreType.DMA,
                       pltpu.SemaphoreType.DMA, pltpu.SemaphoreType.DMA],
        name="gather_back",
    )(posg, ys)


def _expert_kernel(be_ref, nv_ref, src_ref, xs_ref, wg_ref, wu_ref, wd_ref, ys_ref):
    b = pl.program_id(0)
    nv = nv_ref[b]

    @pl.when(nv > 0)
    def _():
        rows = lax.broadcasted_iota(I32, xs_ref.shape, 0)
        lo, hi = _unpack(jnp.where(rows < nv, xs_ref[...], 0))
        lo = lo.astype(BF16)
        hi = hi.astype(BF16)
        g = _dot(lo, wg_ref[0, 0:HALF, :]) + _dot(hi, wg_ref[0, HALF:D_MODEL, :])
        u = _dot(lo, wu_ref[0, 0:HALF, :]) + _dot(hi, wu_ref[0, HALF:D_MODEL, :])
        a = (g * _sigmoid(g) * u).astype(BF16)
        y = _dot(a, wd_ref[0])
        ys_ref[...] = _pack(y[:, 0:HALF], y[:, HALF:D_MODEL])


def _experts(block_e, n_valid, block_src, xs, w_gate, w_up, w_down):
    n_blocks = block_e.shape[0]
    grid_spec = pltpu.PrefetchScalarGridSpec(
        num_scalar_prefetch=3,
        grid=(n_blocks,),
        in_specs=[pl.BlockSpec((MOE_BLOCK, HALF), lambda b, be, nv, src: (src[b], 0)),
                  pl.BlockSpec((1, D_MODEL, EXPERT_DIM), lambda b, be, nv, src: (be[b], 0, 0)),
                  pl.BlockSpec((1, D_MODEL, EXPERT_DIM), lambda b, be, nv, src: (be[b], 0, 0)),
                  pl.BlockSpec((1, EXPERT_DIM, D_MODEL), lambda b, be, nv, src: (be[b], 0, 0))],
        out_specs=pl.BlockSpec((MOE_BLOCK, HALF), lambda b, be, nv, src: (src[b], 0)),
    )
    return pl.pallas_call(
        _expert_kernel,
        grid_spec=grid_spec,
        out_shape=jax.ShapeDtypeStruct(xs.shape, I32),
        compiler_params=pltpu.CompilerParams(
            dimension_semantics=("arbitrary",), vmem_limit_bytes=VMEM_LIMIT),
        name="experts",
    )(block_e, n_valid, block_src, xs, w_gate, w_up, w_down)


def _combine_kernel(xp_ref, yt_ref, wt_ref, p_ref, gple_ref, wpg_ref, wpp_ref, gpost_ref, out_ref):
    proj = _rms(_dot(p_ref[...].astype(BF16), wpp_ref[...]), gpost_ref[...])
    xp = xp_ref[...]
    acc_lo = xp[:, 0:HALF]
    acc_hi = xp[:, HALF:D_MODEL]
    wt = wt_ref[...]
    for k in range(TOP_K):
        lo, hi = _unpack(yt_ref[k])
        wk = wt[:, k:k + 1]
        acc_lo = acc_lo + wk * lo
        acc_hi = acc_hi + wk * hi
    x2 = jnp.concatenate([acc_lo, acc_hi], axis=1)
    gate = _sigmoid(_dot(_rms(x2, gple_ref[...]).astype(BF16), wpg_ref[...]))
    out_ref[...] = x2 + gate * proj


def _combine(xp, yt, wt_t, p2d, g_ple, w_ple_gate, w_ple_proj, g_ple_post):
    t = xp.shape[0]
    tm = TM_COMBINE
    operands = (xp, yt, wt_t, p2d, g_ple.reshape(1, D_MODEL), w_ple_gate.astype(BF16),
                w_ple_proj.astype(BF16), g_ple_post.reshape(1, D_MODEL))
    in_specs = [pl.BlockSpec((tm, D_MODEL), lambda i: (i, 0)),
                pl.BlockSpec((TOP_K, tm, HALF), lambda i: (0, i, 0)),
                pl.BlockSpec((tm, TOP_K), lambda i: (i, 0)),
                pl.BlockSpec((tm, PLE_DIM), lambda i: (i, 0)),
                _const_spec((1, D_MODEL)),
                _const_spec((D_MODEL, D_MODEL)),
                _const_spec((PLE_DIM, D_MODEL)),
                _const_spec((1, D_MODEL))]
    return pl.pallas_call(
        _combine_kernel,
        grid=(t // tm,),
        in_specs=in_specs,
        out_specs=pl.BlockSpec((tm, D_MODEL), lambda i: (i, 0)),
        out_shape=jax.ShapeDtypeStruct((t, D_MODEL), F32),
        compiler_params=pltpu.CompilerParams(
            dimension_semantics=("arbitrary",), vmem_limit_bytes=VMEM_LIMIT),
        name="combine",
    )(*operands)


def _layer(x2d, p2d, batch, seq, g_mix, w_in, w_pool_mix, pool_scale, w_branch_a, q_gain, k_gain,
           attn_sinks, w_branch_b, w_gate, b_gate, w_out, g_ffn, w_router, router_bias, w_exp_gate,
           w_exp_up, w_exp_down, w_sh_gate, w_sh_up, w_sh_down, g_ple, w_ple_gate, w_ple_proj,
           g_ple_post):
    t = x2d.shape[0]
    x1 = _mixer(x2d, batch, seq, g_mix, w_in, w_pool_mix, pool_scale, w_branch_a, q_gain, k_gain,
                attn_sinks, w_branch_b, w_gate, b_gate, w_out)
    xp, h2p, idx8, wt8, rk8, cnt = _ffn_pre(x1, g_ffn, w_sh_gate, w_sh_up, w_sh_down, w_router,
                                            router_bias)

    counts = cnt[:, 0].astype(I32)
    padded = (counts + MOE_BLOCK - 1) // MOE_BLOCK * MOE_BLOCK
    pend = jnp.cumsum(padded)
    pstart = pend - padded
    e_ids = jnp.arange(N_EXPERTS, dtype=I32)
    pos = rk8 + jnp.sum(jnp.where(idx8[None] == e_ids[:, None, None], pstart[:, None, None], 0), axis=0)
    n_blocks = -(-(t * TOP_K) // MOE_BLOCK) + N_EXPERTS
    n_used = pend[-1] // MOE_BLOCK
    block_src = jnp.minimum(jnp.arange(n_blocks, dtype=I32), n_used - 1)
    blk_row = block_src * MOE_BLOCK
    block_e = jnp.minimum(jnp.sum(pend[None, :] <= blk_row[:, None], axis=1), N_EXPERTS - 1).astype(I32)
    n_valid = jnp.clip(counts[block_e] - (blk_row - pstart[block_e]), 0, MOE_BLOCK)
    n_valid = jnp.where(jnp.arange(n_blocks) < n_used, n_valid, 0).astype(I32)

    xs = _dispatch(pos, h2p, n_blocks * MOE_BLOCK)
    ys = _experts(block_e, n_valid, block_src, xs, w_exp_gate.astype(BF16), w_exp_up.astype(BF16),
                  w_exp_down.astype(BF16))
    yt = _gather_back(pos, ys)
    return _combine(xp, yt, wt8.T, p2d, g_ple, w_ple_gate, w_ple_proj, g_ple_post)


def kernel(x, p, g_mix, w_in, w_pool_mix, pool_scale, w_branch_a, q_gain, k_gain, attn_sinks,
           w_branch_b, w_gate, b_gate, w_out, g_ffn, w_router, router_bias, w_exp_gate, w_exp_up,
           w_exp_down, w_sh_gate, w_sh_up, w_sh_down, g_ple, w_ple_gate, w_ple_proj, g_ple_post):
    batch, seq, d = x.shape
    depth = p.shape[0]
    x2d = x.reshape(batch * seq, d)
    for i in range(depth):
        x2d = _layer(x2d, p[i].reshape(batch * seq, PLE_DIM), batch, seq, g_mix[i], w_in[i],
                     w_pool_mix[i], pool_scale[i], w_branch_a[i], q_gain[i], k_gain[i], attn_sinks[i],
                     w_branch_b[i], w_gate[i], b_gate[i], w_out[i], g_ffn[i], w_router[i],
                     router_bias[i], w_exp_gate[i], w_exp_up[i], w_exp_down[i], w_sh_gate[i],
                     w_sh_up[i], w_sh_down[i], g_ple[i], w_ple_gate[i], w_ple_proj[i], g_ple_post[i])
    return x2d.reshape(batch, seq, d)
```

```python
import functools

import numpy as np
import jax
import jax.numpy as jnp
from jax import lax
from jax.experimental import pallas as pl
from jax.experimental.pallas import tpu as pltpu
from jax.experimental.pallas import tpu_sc as plsc

F32 = jnp.float32
BF16 = jnp.bfloat16
I32 = jnp.int32

D_MODEL = 1024
PLE_DIM = 256
POOL_WINDOWS = (2, 4, 8, 16)
POOL_GROUP_DIM = 128
POOL_DIM = 512
N_HEADS = 8
N_KV_HEADS = 2
Q_GROUP = 4
HEAD_DIM = 64
HEAD_PAD = 128
Q_DIM = 512
KV_DIM = 128
ATTN_BLOCK = 128
N_EXPERTS = 64
N_EXPERT_GROUPS = 8
GROUP_SIZE = 8
TOPK_GROUPS = 4
TOP_K = 8
EXPERT_DIM = 256
ROUTED_SCALE = 2.5
MOE_BLOCK = 512
EPS = 1e-6
HALF = D_MODEL // 2
MASKED = -1e30

COL_Q = POOL_DIM
COL_K = COL_Q + N_HEADS * HEAD_PAD
COL_VLO = COL_K + N_KV_HEADS * HEAD_PAD
COL_VHI = COL_VLO + N_KV_HEADS * HEAD_PAD
IN_PAD = COL_VHI + N_KV_HEADS * HEAD_PAD

TM_MIX = 512
TM_FFN = 512
TM_COMBINE = 512
TM_POS = 4096
SC_SCATTER_ROWS = 128
SC_GATHER_ROWS = 64
POOL_TAIL = 8
VMEM_LIMIT = 56 * 1024 * 1024

_NT = (((1,), (1,)), ((), ()))


def _dot(a, b):
    return jnp.dot(a, b, preferred_element_type=F32)


def _rms(x, g):
    ms = jnp.mean(x * x, axis=-1, keepdims=True)
    return x * lax.rsqrt(ms + EPS) * g


def _sigmoid(x):
    return 1.0 / (1.0 + jnp.exp(-x))


def _pack(lo, hi):
    lo_bits = lax.bitcast_convert_type(lo.astype(BF16).astype(F32), I32)
    hi_bits = lax.bitcast_convert_type(hi.astype(BF16).astype(F32), I32)
    return (hi_bits & jnp.int32(-65536)) | lax.shift_right_logical(lo_bits, jnp.int32(16))


def _unpack(w):
    lo = lax.bitcast_convert_type(lax.shift_left(w, jnp.int32(16)), F32)
    hi = lax.bitcast_convert_type(w & jnp.int32(-65536), F32)
    return lo, hi


def _mixer_kernel(x_ref, gmix_ref, win_ref, qg_ref, kg_ref, bias_ref, sink_ref, wmix_ref,
                  pscale_ref, wa_ref, wb_ref, wg_ref, bg_ref, wout_ref,
                  x1_ref, kbuf, vlo_buf, vhi_buf, pbuf, zb_buf):
    s = pl.program_id(1)
    tm = x_ref.shape[0]

    @pl.when(s == 0)
    def _():
        kbuf[0:ATTN_BLOCK, :] = jnp.zeros((ATTN_BLOCK, kbuf.shape[1]), BF16)
        vlo_buf[0:ATTN_BLOCK, :] = jnp.zeros((ATTN_BLOCK, vlo_buf.shape[1]), BF16)
        vhi_buf[0:ATTN_BLOCK, :] = jnp.zeros((ATTN_BLOCK, vhi_buf.shape[1]), BF16)
        pbuf[:, 0:POOL_TAIL, :] = jnp.zeros((pbuf.shape[0], POOL_TAIL, POOL_DIM), F32)

    x = x_ref[...]
    hb = _rms(x, gmix_ref[...]).astype(BF16)
    u = _dot(hb, win_ref[...])

    up = u[:, 0:POOL_DIM]
    t0 = POOL_TAIL
    pbuf[0, t0:t0 + tm, :] = up
    s2 = up + pbuf[0, t0 - 1:t0 - 1 + tm, :]
    pbuf[1, t0:t0 + tm, 128:512] = s2[:, 128:512]
    s4 = s2[:, 128:512] + pbuf[1, t0 - 2:t0 - 2 + tm, 128:512]
    pbuf[2, t0:t0 + tm, 256:512] = s4[:, 128:384]
    s8 = s4[:, 128:384] + pbuf[2, t0 - 4:t0 - 4 + tm, 256:512]
    pbuf[3, t0:t0 + tm, 384:512] = s8[:, 128:256]
    s16 = s8[:, 128:256] + pbuf[3, t0 - 8:t0 - 8 + tm, 384:512]
    for lvl in range(4):
        pbuf[lvl, 0:POOL_TAIL, 128 * lvl:512] = pbuf[lvl, tm:tm + POOL_TAIL, 128 * lvl:512]
    wsums = (s2[:, 0:128], s4[:, 0:128], s8[:, 0:128], s16)
    tpos = (s * tm).astype(F32) + lax.broadcasted_iota(I32, (tm, 1), 0).astype(F32)
    za_parts = []
    for g, w in enumerate(POOL_WINDOWS):
        inv_count = 1.0 / jnp.minimum(tpos + 1.0, float(w))
        pooled = wsums[g] * inv_count - up[:, 128 * g:128 * (g + 1)]
        mixed = _dot(pooled.astype(BF16), wmix_ref[g]) * pscale_ref[:, 128 * g:128 * (g + 1)]
        za_parts.append(mixed.astype(BF16))
    za = jnp.concatenate(za_parts, axis=1)

    qg = qg_ref[...]
    qn = []
    for h in range(N_HEADS):
        qh = u[:, COL_Q + HEAD_PAD * h:COL_Q + HEAD_PAD * (h + 1)]
        ms = jnp.sum(qh * qh, axis=-1, keepdims=True) * (1.0 / HEAD_DIM)
        qn.append((qh * lax.rsqrt(ms + EPS) * qg).astype(BF16))
    kg = kg_ref[...]
    for kv in range(N_KV_HEADS):
        kh = u[:, COL_K + HEAD_PAD * kv:COL_K + HEAD_PAD * (kv + 1)]
        ms = jnp.sum(kh * kh, axis=-1, keepdims=True) * (1.0 / HEAD_DIM)
        kbuf[ATTN_BLOCK:ATTN_BLOCK + tm, HEAD_PAD * kv:HEAD_PAD * (kv + 1)] = (
            kh * lax.rsqrt(ms + EPS) * kg).astype(BF16)
    vlo_buf[ATTN_BLOCK:ATTN_BLOCK + tm, :] = u[:, COL_VLO:COL_VHI].astype(BF16)
    vhi_buf[ATTN_BLOCK:ATTN_BLOCK + tm, :] = u[:, COL_VHI:IN_PAD].astype(BF16)

    first = jnp.where(s == 0, 1, 0)
    for n in range(tm // ATTN_BLOCK):
        r0 = ATTN_BLOCK * n
        for kv in range(N_KV_HEADS):
            c0 = HEAD_PAD * kv
            qs = jnp.concatenate([qn[kv * Q_GROUP + g][r0:r0 + ATTN_BLOCK] for g in range(Q_GROUP)],
                                 axis=0)
            kk = kbuf[r0:r0 + 2 * ATTN_BLOCK, c0:c0 + HEAD_PAD]
            sc = lax.dot_general(qs, kk, _NT, preferred_element_type=F32)
            sc = sc + (bias_ref[first, kv] if n == 0 else bias_ref[0, kv])
            sink = sink_ref[kv]
            m = jnp.maximum(jnp.max(sc, axis=-1, keepdims=True), sink)
            e = jnp.exp(sc - m)
            den = jnp.sum(e, axis=-1, keepdims=True) + jnp.exp(sink - m)
            p = (e * (1.0 / den)).astype(BF16)
            vl = vlo_buf[r0:r0 + 2 * ATTN_BLOCK, c0:c0 + HEAD_PAD]
            vh = vhi_buf[r0:r0 + 2 * ATTN_BLOCK, c0:c0 + HEAD_PAD]
            for pair in range(2):
                pr = 2 * pair * ATTN_BLOCK
                o = _dot(p[pr:pr + ATTN_BLOCK], vl) + _dot(p[pr + ATTN_BLOCK:pr + 2 * ATTN_BLOCK], vh)
                col = kv * Q_GROUP * HEAD_DIM + pair * HEAD_PAD
                zb_buf[r0:r0 + ATTN_BLOCK, col:col + HEAD_PAD] = o.astype(BF16)
    kbuf[0:ATTN_BLOCK, :] = kbuf[tm:tm + ATTN_BLOCK, :]
    vlo_buf[0:ATTN_BLOCK, :] = vlo_buf[tm:tm + ATTN_BLOCK, :]
    vhi_buf[0:ATTN_BLOCK, :] = vhi_buf[tm:tm + ATTN_BLOCK, :]

    y_a = _dot(za, wa_ref[...])
    y_b = _dot(zb_buf[...], wb_ref[...])
    g_a = _sigmoid(_dot(hb, wg_ref[:, 0:D_MODEL]) + bg_ref[:, 0:D_MODEL])
    merged = g_a * y_a
    g_b = _sigmoid(_dot(hb, wg_ref[:, D_MODEL:2 * D_MODEL]) + bg_ref[:, D_MODEL:2 * D_MODEL])
    merged = (merged + g_b * y_b).astype(BF16)
    x1_ref[...] = x + _dot(merged, wout_ref[...])


def _attn_tables(attn_sinks):
    slopes = 2.0 ** (-8.0 * (np.arange(N_HEADS) + 1) / N_HEADS)
    i = np.arange(ATTN_BLOCK)[:, None]
    j = np.arange(2 * ATTN_BLOCK)[None, :]
    dist = ATTN_BLOCK + i - j
    in_window = (dist >= 0) & (dist < ATTN_BLOCK)
    bias = np.empty((2, N_KV_HEADS, Q_GROUP * ATTN_BLOCK, 2 * ATTN_BLOCK), np.float32)
    for first in range(2):
        ok = in_window & ((j >= ATTN_BLOCK) if first else True)
        for kv in range(N_KV_HEADS):
            for g in range(Q_GROUP):
                sl = np.float32(slopes[kv * Q_GROUP + g])
                val = -(sl * dist.astype(np.float32))
                bias[first, kv, g * ATTN_BLOCK:(g + 1) * ATTN_BLOCK] = np.where(ok, val, MASKED)
    sink = jnp.repeat(attn_sinks.astype(F32).reshape(N_KV_HEADS, Q_GROUP), ATTN_BLOCK, axis=1)
    return jnp.asarray(bias), sink.reshape(N_KV_HEADS, Q_GROUP * ATTN_BLOCK, 1)


def _pad_heads(w, n_heads, offset=0):
    k = w.shape[0]
    w = w.reshape(k, n_heads, HEAD_DIM)
    w = jnp.pad(w, ((0, 0), (0, 0), (offset, HEAD_PAD - HEAD_DIM - offset)))
    return w.reshape(k, n_heads * HEAD_PAD)


def _const_spec(shape):
    nd = len(shape)
    return pl.BlockSpec(shape, lambda *_: (0,) * nd)


def _mixer(x2d, batch, seq, g_mix, w_in, w_pool_mix, pool_scale, w_branch_a, q_gain, k_gain,
           attn_sinks, w_branch_b, w_gate, b_gate, w_out):
    t = x2d.shape[0]
    tm = TM_MIX
    ns = seq // tm
    w_q = w_in[:, POOL_DIM:POOL_DIM + Q_DIM]
    w_k = w_in[:, POOL_DIM + Q_DIM:POOL_DIM + Q_DIM + KV_DIM]
    w_v = w_in[:, POOL_DIM + Q_DIM + KV_DIM:]
    win_p = jnp.concatenate([w_in[:, :POOL_DIM], _pad_heads(w_q, N_HEADS), _pad_heads(w_k, N_KV_HEADS),
                             _pad_heads(w_v, N_KV_HEADS), _pad_heads(w_v, N_KV_HEADS, HEAD_DIM)],
                            axis=1).astype(BF16)
    qg = jnp.pad(q_gain * (HEAD_DIM ** -0.5), (0, HEAD_PAD - HEAD_DIM)).reshape(1, HEAD_PAD)
    kg = jnp.pad(k_gain, (0, HEAD_PAD - HEAD_DIM)).reshape(1, HEAD_PAD)
    bias, sink = _attn_tables(attn_sinks)
    operands = (x2d, g_mix.reshape(1, D_MODEL), win_p, qg, kg, bias, sink, w_pool_mix.astype(BF16),
                pool_scale.reshape(1, POOL_DIM), w_branch_a.astype(BF16), w_branch_b.astype(BF16),
                w_gate.astype(BF16), b_gate.reshape(1, 2 * D_MODEL), w_out.astype(BF16))
    in_specs = [pl.BlockSpec((tm, D_MODEL), lambda b, s: (b * ns + s, 0))]
    in_specs += [_const_spec(op.shape) for op in operands[1:]]
    return pl.pallas_call(
        _mixer_kernel,
        grid=(batch, ns),
        in_specs=in_specs,
        out_specs=pl.BlockSpec((tm, D_MODEL), lambda b, s: (b * ns + s, 0)),
        out_shape=jax.ShapeDtypeStruct((t, D_MODEL), F32),
        scratch_shapes=[
            pltpu.VMEM((ATTN_BLOCK + tm, N_KV_HEADS * HEAD_PAD), BF16),
            pltpu.VMEM((ATTN_BLOCK + tm, N_KV_HEADS * HEAD_PAD), BF16),
            pltpu.VMEM((ATTN_BLOCK + tm, N_KV_HEADS * HEAD_PAD), BF16),
            pltpu.VMEM((4, POOL_TAIL + tm, POOL_DIM), F32),
            pltpu.VMEM((tm, Q_DIM), BF16),
        ],
        compiler_params=pltpu.CompilerParams(
            dimension_semantics=("arbitrary", "arbitrary"), vmem_limit_bytes=VMEM_LIMIT),
        name="mixer",
    )(*operands)


def _ffn_pre_kernel(x1_ref, gffn_ref, wsg_ref, wsu_ref, wsd_ref, wrh_ref, wrl_ref, rb_ref, tri_ref,
                    xp_ref, h2p_ref, idx_ref, wt_ref, rk_ref, cnt_ref, run_ref):
    i = pl.program_id(0)
    tm = x1_ref.shape[0]

    @pl.when(i == 0)
    def _():
        run_ref[...] = jnp.zeros_like(run_ref)

    x1 = x1_ref[...]
    h2 = _rms(x1, gffn_ref[...])
    hb = h2.astype(BF16)
    g = _dot(hb, wsg_ref[...])
    a = (g * _sigmoid(g) * _dot(hb, wsu_ref[...])).astype(BF16)
    xp_ref[...] = x1 + _dot(a, wsd_ref[...])
    h2p_ref[...] = _pack(h2[:, 0:HALF], h2[:, HALF:D_MODEL])

    lo = (h2 - hb.astype(F32)).astype(BF16)
    wrh = wrh_ref[...]
    logits = (lax.dot_general(wrh, hb, _NT, preferred_element_type=F32)
              + lax.dot_general(wrl_ref[...], hb, _NT, preferred_element_type=F32)
              + lax.dot_general(wrh, lo, _NT, preferred_element_type=F32))
    scores = _sigmoid(logits)
    choice = scores + rb_ref[...]

    neg = -jnp.inf
    sub8 = lax.broadcasted_iota(I32, (GROUP_SIZE, tm), 0).astype(F32)
    grp_rows = []
    for gi in range(N_EXPERT_GROUPS):
        cg = choice[GROUP_SIZE * gi:GROUP_SIZE * (gi + 1)]
        m1 = jnp.max(cg, axis=0, keepdims=True)
        first = jnp.min(jnp.where(cg == m1, sub8, float(GROUP_SIZE)), axis=0, keepdims=True)
        m2 = jnp.max(jnp.where(sub8 == first, neg, cg), axis=0, keepdims=True)
        grp_rows.append(m1 + m2)
    gs = jnp.concatenate(grp_rows, axis=0)
    beaten = jnp.zeros((N_EXPERT_GROUPS, tm), F32)
    for gi in range(N_EXPERT_GROUPS):
        row = grp_rows[gi]
        wins = (row > gs) | ((row == gs) & (sub8 > float(gi)))
        beaten = beaten + jnp.where(wins, 1.0, 0.0)
    gsel = beaten < float(TOPK_GROUPS)
    cm = jnp.concatenate(
        [jnp.where(gsel[gi:gi + 1], choice[GROUP_SIZE * gi:GROUP_SIZE * (gi + 1)], neg)
         for gi in range(N_EXPERT_GROUPS)], axis=0)

    iota_e = lax.broadcasted_iota(I32, (N_EXPERTS, tm), 0).astype(F32)
    idx_rows, s_rows = [], []
    sel = jnp.zeros((N_EXPERTS, tm), F32)
    for _ in range(TOP_K):
        m = jnp.max(cm, axis=0, keepdims=True)
        idx = jnp.min(jnp.where(cm == m, iota_e, float(N_EXPERTS)), axis=0, keepdims=True)
        oh = iota_e == idx
        s_rows.append(jnp.sum(jnp.where(oh, scores, 0.0), axis=0, keepdims=True))
        idx_rows.append(idx)
        cm = jnp.where(oh, neg, cm)
        sel = sel + jnp.where(oh, 1.0, 0.0)

    run = run_ref[:, 0:1]
    cum = _dot(sel.astype(BF16), tri_ref[...])
    before = run + cum - sel
    rk_rows = [jnp.sum(jnp.where(iota_e == idx, before, 0.0), axis=0, keepdims=True) for idx in idx_rows]
    new_run = run + jnp.sum(sel, axis=1, keepdims=True)
    run_ref[...] = jnp.broadcast_to(new_run, run_ref.shape)
    cnt_ref[...] = jnp.broadcast_to(new_run, cnt_ref.shape)

    ssum = s_rows[0]
    for r in range(1, TOP_K):
        ssum = ssum + s_rows[r]
    denom = ssum + 1e-20
    idx_ref[...] = jnp.concatenate(idx_rows, axis=0).astype(I32)
    wt_ref[...] = jnp.concatenate([sr / denom * ROUTED_SCALE for sr in s_rows], axis=0)
    rk_ref[...] = jnp.concatenate(rk_rows, axis=0).astype(I32)


def _ffn_pre(x1, g_ffn, w_sh_gate, w_sh_up, w_sh_down, w_router, router_bias):
    t = x1.shape[0]
    tm = TM_FFN
    wr_t = w_router.astype(F32).T
    wr_hi = wr_t.astype(BF16)
    wr_lo = (wr_t - wr_hi.astype(F32)).astype(BF16)
    tri = (np.arange(tm)[:, None] <= np.arange(tm)[None, :]).astype(np.float32)
    operands = (x1, g_ffn.reshape(1, D_MODEL), w_sh_gate.astype(BF16), w_sh_up.astype(BF16),
                w_sh_down.astype(BF16), wr_hi, wr_lo, router_bias.astype(F32).reshape(N_EXPERTS, 1),
                jnp.asarray(tri, BF16))
    in_specs = [pl.BlockSpec((tm, D_MODEL), lambda i: (i, 0))]
    in_specs += [_const_spec(op.shape) for op in operands[1:]]
    row8 = pl.BlockSpec((TOP_K, tm), lambda i: (0, i))
    return pl.pallas_call(
        _ffn_pre_kernel,
        grid=(t // tm,),
        in_specs=in_specs,
        out_specs=[pl.BlockSpec((tm, D_MODEL), lambda i: (i, 0)),
                   pl.BlockSpec((tm, HALF), lambda i: (i, 0)),
                   row8, row8, row8,
                   pl.BlockSpec((N_EXPERTS, 128), lambda i: (0, 0))],
        out_shape=[jax.ShapeDtypeStruct((t, D_MODEL), F32),
                   jax.ShapeDtypeStruct((t, HALF), I32),
                   jax.ShapeDtypeStruct((TOP_K, t), I32),
                   jax.ShapeDtypeStruct((TOP_K, t), F32),
                   jax.ShapeDtypeStruct((TOP_K, t), I32),
                   jax.ShapeDtypeStruct((N_EXPERTS, 128), F32)],
        scratch_shapes=[pltpu.VMEM((N_EXPERTS, 128), F32)],
        compiler_params=pltpu.CompilerParams(
            dimension_semantics=("arbitrary",), vmem_limit_bytes=VMEM_LIMIT),
        name="ffn_pre",
    )(*operands)


def _positions_kernel(pstart_ref, idx_ref, rk_ref, pos_ref):
    idx = idx_ref[...]
    acc = rk_ref[...]
    for e in range(N_EXPERTS):
        acc = acc + jnp.where(idx == e, pstart_ref[e], 0)
    pos_ref[...] = acc


def _positions(pstart, idx8, rk8):
    t = idx8.shape[1]
    tm = TM_POS
    spec = pl.BlockSpec((TOP_K, tm), lambda i, ps: (0, i))
    return pl.pallas_call(
        _positions_kernel,
        grid_spec=pltpu.PrefetchScalarGridSpec(num_scalar_prefetch=1, grid=(t // tm,),
                                               in_specs=[spec, spec], out_specs=spec),
        out_shape=jax.ShapeDtypeStruct((TOP_K, t), I32),
        name="positions",
    )(pstart, idx8, rk8)


def _sc_workers():
    info = plsc.get_sparse_core_info()
    return info.num_cores, info.num_cores * info.num_subcores


def _dispatch(pos, h2p, n_rows):
    t = h2p.shape[0]
    ch = SC_SCATTER_ROWS
    n_cores, n_workers = _sc_workers()
    n_ch = t // (n_workers * ch)
    pos3 = pos.reshape(TOP_K, t // ch, ch).transpose(1, 0, 2)

    def body(pos_hbm, h_hbm, xs_hbm, idx_v, rows_v, sem):
        wid = lax.axis_index("s") * n_cores + lax.axis_index("c")

        @pl.loop(0, n_ch)
        def _(c):
            chunk = wid * n_ch + c
            pltpu.sync_copy(h_hbm.at[pl.ds(chunk * ch, ch)], rows_v)
            pltpu.sync_copy(pos_hbm.at[chunk], idx_v)
            copies = [pltpu.async_copy(rows_v, xs_hbm.at[idx_v.at[k]], sem) for k in range(TOP_K)]
            for cp in copies:
                cp.wait()

    return pl.kernel(
        body,
        out_type=jax.ShapeDtypeStruct((n_rows, HALF), I32),
        mesh=plsc.VectorSubcoreMesh(core_axis_name="c", subcore_axis_name="s"),
        scratch_types=[pltpu.VMEM((TOP_K, ch), I32), pltpu.VMEM((ch, HALF), I32),
                       pltpu.SemaphoreType.DMA],
        name="dispatch",
    )(pos3, h2p)


def _gather_back(pos, ys):
    t = pos.shape[1]
    ch = SC_GATHER_ROWS
    n_cores, n_workers = _sc_workers()
    n_ch = t // (n_workers * ch)
    posg = pos.reshape(TOP_K, t // ch, ch).transpose(1, 0, 2).reshape(-1)

    def body(pos_hbm, ys_hbm, yt_hbm, idx_v, buf0, buf1, gsem, wsem0, wsem1):
        wid = lax.axis_index("s") * n_cores + lax.axis_index("c")
        bufs = (buf0, buf1)
        wsems = (wsem0, wsem1)

        @pl.loop(0, n_ch)
        def _(c):
            chunk = wid * n_ch + c
            pltpu.sync_copy(pos_hbm.at[pl.ds(chunk * (TOP_K * ch), TOP_K * ch)], idx_v)
            writes = [None, None]
            for k in range(TOP_K):
                b = k % 2
                if writes[b] is not None:
                    writes[b].wait()
                pltpu.async_copy(ys_hbm.at[idx_v.at[pl.ds(k * ch, ch)]], bufs[b], gsem).wait()
                writes[b] = pltpu.async_copy(bufs[b], yt_hbm.at[k, pl.ds(chunk * ch, ch)], wsems[b])
            writes[0].wait()
            writes[1].wait()

    return pl.kernel(
        body,
        out_type=jax.ShapeDtypeStruct((TOP_K, t, HALF), I32),
        mesh=plsc.VectorSubcoreMesh(core_axis_name="c", subcore_axis_name="s"),
        scratch_types=[pltpu.VMEM((TOP_K * ch,), I32), pltpu.VMEM((ch, HALF), I32),
                       pltpu.VMEM((ch, HALF), I32), pltpu.SemaphoreType.DMA,
                       pltpu.SemaphoreType.DMA, pltpu.SemaphoreType.DMA],
        name="gather_back",
    )(posg, ys)


def _expert_kernel(be_ref, nv_ref, src_ref, new_ref, xs_ref, wg_ref, wu_ref, wd_ref, ys_ref,
                   wg_s, wu_s, wd_s):
    b = pl.program_id(0)
    nv = nv_ref[b]

    @pl.when(new_ref[b] > 0)
    def _():
        wg_s[...] = wg_ref[0].astype(BF16)
        wu_s[...] = wu_ref[0].astype(BF16)
        wd_s[...] = wd_ref[0].astype(BF16)

    @pl.when(nv > 0)
    def _():
        rows = lax.broadcasted_iota(I32, xs_ref.shape, 0)
        lo, hi = _unpack(jnp.where(rows < nv, xs_ref[...], 0))
        lo = lo.astype(BF16)
        hi = hi.astype(BF16)
        g = _dot(lo, wg_s[0:HALF, :]) + _dot(hi, wg_s[HALF:D_MODEL, :])
        u = _dot(lo, wu_s[0:HALF, :]) + _dot(hi, wu_s[HALF:D_MODEL, :])
        a = (g * _sigmoid(g) * u).astype(BF16)
        y = _dot(a, wd_s[...])
        ys_ref[...] = _pack(y[:, 0:HALF], y[:, HALF:D_MODEL])


def _experts(block_e, n_valid, block_src, block_new, xs, w_gate, w_up, w_down):
    n_blocks = block_e.shape[0]
    grid_spec = pltpu.PrefetchScalarGridSpec(
        num_scalar_prefetch=4,
        grid=(n_blocks,),
        in_specs=[pl.BlockSpec((MOE_BLOCK, HALF), lambda b, be, nv, src, new: (src[b], 0)),
                  pl.BlockSpec((1, D_MODEL, EXPERT_DIM), lambda b, be, nv, src, new: (be[b], 0, 0)),
                  pl.BlockSpec((1, D_MODEL, EXPERT_DIM), lambda b, be, nv, src, new: (be[b], 0, 0)),
                  pl.BlockSpec((1, EXPERT_DIM, D_MODEL), lambda b, be, nv, src, new: (be[b], 0, 0))],
        out_specs=pl.BlockSpec((MOE_BLOCK, HALF), lambda b, be, nv, src, new: (src[b], 0)),
        scratch_shapes=[pltpu.VMEM((D_MODEL, EXPERT_DIM), BF16), pltpu.VMEM((D_MODEL, EXPERT_DIM), BF16),
                        pltpu.VMEM((EXPERT_DIM, D_MODEL), BF16)],
    )
    return pl.pallas_call(
        _expert_kernel,
        grid_spec=grid_spec,
        out_shape=jax.ShapeDtypeStruct(xs.shape, I32),
        compiler_params=pltpu.CompilerParams(
            dimension_semantics=("arbitrary",), vmem_limit_bytes=VMEM_LIMIT),
        name="experts",
    )(block_e, n_valid, block_src, block_new, xs, w_gate, w_up, w_down)


def _combine_kernel(xp_ref, yt_ref, wt_ref, p_ref, gple_ref, wpg_ref, wpp_ref, gpost_ref, out_ref):
    proj = _rms(_dot(p_ref[...].astype(BF16), wpp_ref[...]), gpost_ref[...])
    xp = xp_ref[...]
    acc_lo = xp[:, 0:HALF]
    acc_hi = xp[:, HALF:D_MODEL]
    wt = wt_ref[...]
    for k in range(TOP_K):
        lo, hi = _unpack(yt_ref[k])
        wk = wt[:, k:k + 1]
        acc_lo = acc_lo + wk * lo
        acc_hi = acc_hi + wk * hi
    x2 = jnp.concatenate([acc_lo, acc_hi], axis=1)
    gate = _sigmoid(_dot(_rms(x2, gple_ref[...]).astype(BF16), wpg_ref[...]))
    out_ref[...] = x2 + gate * proj


def _combine(xp, yt, wt_t, p2d, g_ple, w_ple_gate, w_ple_proj, g_ple_post):
    t = xp.shape[0]
    tm = TM_COMBINE
    operands = (xp, yt, wt_t, p2d, g_ple.reshape(1, D_MODEL), w_ple_gate.astype(BF16),
                w_ple_proj.astype(BF16), g_ple_post.reshape(1, D_MODEL))
    in_specs = [pl.BlockSpec((tm, D_MODEL), lambda i: (i, 0)),
                pl.BlockSpec((TOP_K, tm, HALF), lambda i: (0, i, 0)),
                pl.BlockSpec((tm, TOP_K), lambda i: (i, 0)),
                pl.BlockSpec((tm, PLE_DIM), lambda i: (i, 0)),
                _const_spec((1, D_MODEL)),
                _const_spec((D_MODEL, D_MODEL)),
                _const_spec((PLE_DIM, D_MODEL)),
                _const_spec((1, D_MODEL))]
    return pl.pallas_call(
        _combine_kernel,
        grid=(t // tm,),
        in_specs=in_specs,
        out_specs=pl.BlockSpec((tm, D_MODEL), lambda i: (i, 0)),
        out_shape=jax.ShapeDtypeStruct((t, D_MODEL), F32),
        compiler_params=pltpu.CompilerParams(
            dimension_semantics=("arbitrary",), vmem_limit_bytes=VMEM_LIMIT),
        name="combine",
    )(*operands)


def _layer(x2d, p2d, batch, seq, g_mix, w_in, w_pool_mix, pool_scale, w_branch_a, q_gain, k_gain,
           attn_sinks, w_branch_b, w_gate, b_gate, w_out, g_ffn, w_router, router_bias, w_exp_gate,
           w_exp_up, w_exp_down, w_sh_gate, w_sh_up, w_sh_down, g_ple, w_ple_gate, w_ple_proj,
           g_ple_post):
    t = x2d.shape[0]
    x1 = _mixer(x2d, batch, seq, g_mix, w_in, w_pool_mix, pool_scale, w_branch_a, q_gain, k_gain,
                attn_sinks, w_branch_b, w_gate, b_gate, w_out)
    xp, h2p, idx8, wt8, rk8, cnt = _ffn_pre(x1, g_ffn, w_sh_gate, w_sh_up, w_sh_down, w_router,
                                            router_bias)

    counts = cnt[:, 0].astype(I32)
    padded = (counts + MOE_BLOCK - 1) // MOE_BLOCK * MOE_BLOCK
    pend = jnp.cumsum(padded)
    pstart = pend - padded
    pos = _positions(pstart.astype(I32), idx8, rk8)
    n_blocks = -(-(t * TOP_K) // MOE_BLOCK) + N_EXPERTS
    n_used = pend[-1] // MOE_BLOCK
    block_src = jnp.minimum(jnp.arange(n_blocks, dtype=I32), n_used - 1)
    blk_row = block_src * MOE_BLOCK
    block_e = jnp.minimum(jnp.sum(pend[None, :] <= blk_row[:, None], axis=1), N_EXPERTS - 1).astype(I32)
    n_valid = jnp.clip(counts[block_e] - (blk_row - pstart[block_e]), 0, MOE_BLOCK)
    n_valid = jnp.where(jnp.arange(n_blocks) < n_used, n_valid, 0).astype(I32)

    xs = _dispatch(pos, h2p, n_blocks * MOE_BLOCK)
    block_new = jnp.concatenate([jnp.ones((1,), I32), (block_e[1:] != block_e[:-1]).astype(I32)])
    block_new = jnp.where(n_valid > 0, block_new, 0)
    ys = _experts(block_e, n_valid, block_src, block_new, xs, w_exp_gate, w_exp_up, w_exp_down)
    yt = _gather_back(pos, ys)
    return _combine(xp, yt, wt8.T, p2d, g_ple, w_ple_gate, w_ple_proj, g_ple_post)


def kernel(x, p, g_mix, w_in, w_pool_mix, pool_scale, w_branch_a, q_gain, k_gain, attn_sinks,
           w_branch_b, w_gate, b_gate, w_out, g_ffn, w_router, router_bias, w_exp_gate, w_exp_up,
           w_exp_down, w_sh_gate, w_sh_up, w_sh_down, g_ple, w_ple_gate, w_ple_proj, g_ple_post):
    batch, seq, d = x.shape
    depth = p.shape[0]
    x2d = x.reshape(batch * seq, d)
    for i in range(depth):
        x2d = _layer(x2d, p[i].reshape(batch * seq, PLE_DIM), batch, seq, g_mix[i], w_in[i],
                     w_pool_mix[i], pool_scale[i], w_branch_a[i], q_gain[i], k_gain[i], attn_sinks[i],
                     w_branch_b[i], w_gate[i], b_gate[i], w_out[i], g_ffn[i], w_router[i],
                     router_bias[i], w_exp_gate[i], w_exp_up[i], w_exp_down[i], w_sh_gate[i],
                     w_sh_up[i], w_sh_down[i], g_ple[i], w_ple_gate[i], w_ple_proj[i], g_ple_post[i])
    return x2d.reshape(batch, seq, d)
```

```python
import functools

import numpy as np
import jax
import jax.numpy as jnp
from jax import lax
from jax.experimental import pallas as pl
from jax.experimental.pallas import tpu as pltpu
from jax.experimental.pallas import tpu_sc as plsc

F32 = jnp.float32
BF16 = jnp.bfloat16
I32 = jnp.int32

D_MODEL = 1024
PLE_DIM = 256
POOL_WINDOWS = (2, 4, 8, 16)
POOL_GROUP_DIM = 128
POOL_DIM = 512
N_HEADS = 8
N_KV_HEADS = 2
Q_GROUP = 4
HEAD_DIM = 64
HEAD_PAD = 128
Q_DIM = 512
KV_DIM = 128
ATTN_BLOCK = 128
N_EXPERTS = 64
N_EXPERT_GROUPS = 8
GROUP_SIZE = 8
TOPK_GROUPS = 4
TOP_K = 8
EXPERT_DIM = 256
ROUTED_SCALE = 2.5
MOE_BLOCK = 512
EPS = 1e-6
HALF = D_MODEL // 2
MASKED = -1e30

COL_Q = POOL_DIM
COL_K = COL_Q + N_HEADS * HEAD_PAD
COL_V = COL_K + N_KV_HEADS * HEAD_PAD
IN_PAD = COL_V + N_KV_HEADS * HEAD_PAD
HEAD_ORDER = (0, 2, 1, 3)

TM_MIX = 512
TM_FFN = 512
TM_COMBINE = 512
TM_POS = 4096
SC_SCATTER_ROWS = 128
SC_GATHER_ROWS = 64
POOL_TAIL = 8
VMEM_LIMIT = 56 * 1024 * 1024

_NT = (((1,), (1,)), ((), ()))


def _dot(a, b):
    return jnp.dot(a, b, preferred_element_type=F32)


def _rms(x, g):
    ms = jnp.mean(x * x, axis=-1, keepdims=True)
    return x * lax.rsqrt(ms + EPS) * g


def _sigmoid(x):
    return 0.5 * jnp.tanh(0.5 * x) + 0.5


def _pack(lo, hi):
    lo_bits = lax.bitcast_convert_type(lo.astype(BF16).astype(F32), I32)
    hi_bits = lax.bitcast_convert_type(hi.astype(BF16).astype(F32), I32)
    return (hi_bits & jnp.int32(-65536)) | lax.shift_right_logical(lo_bits, jnp.int32(16))


def _unpack(w):
    lo = lax.bitcast_convert_type(lax.shift_left(w, jnp.int32(16)), F32)
    hi = lax.bitcast_convert_type(w & jnp.int32(-65536), F32)
    return lo, hi


def _mixer_kernel(x_ref, gmix_ref, win_ref, qg_ref, kg_ref, bias_ref, sink_ref, ones_ref, wmix_ref,
                  pscale_ref, wa_ref, wb_ref, wg_ref, bg_ref, wout_ref,
                  x1_ref, kbuf, vt_buf, pbuf, zbt_buf):
    s = pl.program_id(1)
    tm = x_ref.shape[0]

    @pl.when(s == 0)
    def _():
        kbuf[0:ATTN_BLOCK, :] = jnp.zeros((ATTN_BLOCK, kbuf.shape[1]), BF16)
        vt_buf[:, 0:ATTN_BLOCK] = jnp.zeros((vt_buf.shape[0], ATTN_BLOCK), BF16)
        pbuf[:, 0:POOL_TAIL, :] = jnp.zeros((pbuf.shape[0], POOL_TAIL, POOL_DIM), F32)

    x = x_ref[...]
    hb = _rms(x, gmix_ref[...]).astype(BF16)
    u = _dot(hb, win_ref[...])

    up = u[:, 0:POOL_DIM]
    t0 = POOL_TAIL
    pbuf[0, t0:t0 + tm, :] = up
    s2 = up + pbuf[0, t0 - 1:t0 - 1 + tm, :]
    pbuf[1, t0:t0 + tm, 128:512] = s2[:, 128:512]
    s4 = s2[:, 128:512] + pbuf[1, t0 - 2:t0 - 2 + tm, 128:512]
    pbuf[2, t0:t0 + tm, 256:512] = s4[:, 128:384]
    s8 = s4[:, 128:384] + pbuf[2, t0 - 4:t0 - 4 + tm, 256:512]
    pbuf[3, t0:t0 + tm, 384:512] = s8[:, 128:256]
    s16 = s8[:, 128:256] + pbuf[3, t0 - 8:t0 - 8 + tm, 384:512]
    for lvl in range(4):
        pbuf[lvl, 0:POOL_TAIL, 128 * lvl:512] = pbuf[lvl, tm:tm + POOL_TAIL, 128 * lvl:512]
    wsums = (s2[:, 0:128], s4[:, 0:128], s8[:, 0:128], s16)
    tpos = (s * tm).astype(F32) + lax.broadcasted_iota(I32, (tm, 1), 0).astype(F32)
    za_parts = []
    for g, w in enumerate(POOL_WINDOWS):
        inv_count = 1.0 / jnp.minimum(tpos + 1.0, float(w))
        pooled = wsums[g] * inv_count - up[:, 128 * g:128 * (g + 1)]
        mixed = _dot(pooled.astype(BF16), wmix_ref[g]) * pscale_ref[:, 128 * g:128 * (g + 1)]
        za_parts.append(mixed.astype(BF16))
    za = jnp.concatenate(za_parts, axis=1)

    ones2 = ones_ref[...]
    q = u[:, COL_Q:COL_K]
    q2 = (q * q).astype(BF16)
    qss = jnp.concatenate([_dot(q2[:, 256 * c:256 * (c + 1)], ones2) for c in range(N_HEADS // 2)], axis=1)
    qn = (q * lax.rsqrt(qss * (1.0 / HEAD_DIM) + EPS) * qg_ref[...]).astype(BF16)
    k = u[:, COL_K:COL_V]
    kss = _dot((k * k).astype(BF16), ones2)
    kbuf[ATTN_BLOCK:ATTN_BLOCK + tm, :] = (k * lax.rsqrt(kss * (1.0 / HEAD_DIM) + EPS) * kg_ref[...]).astype(BF16)
    vt_buf[:, ATTN_BLOCK:ATTN_BLOCK + tm] = u[:, COL_V:IN_PAD].T.astype(BF16)

    nq = Q_GROUP * ATTN_BLOCK
    key_j = lax.broadcasted_iota(I32, (ATTN_BLOCK, nq), 0)
    qry_i = lax.broadcasted_iota(I32, (ATTN_BLOCK, nq), 1) & (ATTN_BLOCK - 1)
    from_prev = key_j > qry_i
    first = jnp.where(s == 0, 1, 0)
    for n in range(tm // ATTN_BLOCK):
        r0 = ATTN_BLOCK * n
        for kv in range(N_KV_HEADS):
            c0 = HEAD_PAD * kv
            qs = jnp.concatenate(
                [qn[r0:r0 + ATTN_BLOCK, HEAD_PAD * (kv * Q_GROUP + g):HEAD_PAD * (kv * Q_GROUP + g + 1)]
                 for g in HEAD_ORDER], axis=0)
            kk = kbuf[r0:r0 + 2 * ATTN_BLOCK, c0:c0 + HEAD_PAD]
            st = lax.dot_general(kk, qs, _NT, preferred_element_type=F32)
            sc = jnp.where(from_prev, st[0:ATTN_BLOCK], st[ATTN_BLOCK:2 * ATTN_BLOCK])
            sc = sc + (bias_ref[first, kv] if n == 0 else bias_ref[0, kv])
            sink = sink_ref[kv]
            m = jnp.maximum(jnp.max(sc, axis=0, keepdims=True), sink)
            e = jnp.exp(sc - m)
            inv_den = 1.0 / (jnp.sum(e, axis=0, keepdims=True) + jnp.exp(sink - m))
            pt = jnp.concatenate([jnp.where(from_prev, e, 0.0), jnp.where(from_prev, 0.0, e)],
                                 axis=0).astype(BF16)
            v_even = vt_buf[c0:c0 + HEAD_PAD, r0:r0 + 2 * ATTN_BLOCK]
            v_odd = jnp.concatenate([v_even[HEAD_DIM:HEAD_PAD], v_even[0:HEAD_DIM]], axis=0)
            half = 2 * ATTN_BLOCK
            o = (_dot(v_even, pt[:, 0:half]) * inv_den[:, 0:half]
                 + _dot(v_odd, pt[:, half:2 * half]) * inv_den[:, half:2 * half])
            ch = kv * Q_GROUP * HEAD_DIM
            zbt_buf[ch:ch + HEAD_PAD, r0:r0 + ATTN_BLOCK] = o[:, 0:ATTN_BLOCK]
            zbt_buf[ch + HEAD_PAD:ch + 2 * HEAD_PAD, r0:r0 + ATTN_BLOCK] = o[:, ATTN_BLOCK:half]
    kbuf[0:ATTN_BLOCK, :] = kbuf[tm:tm + ATTN_BLOCK, :]
    vt_buf[:, 0:ATTN_BLOCK] = vt_buf[:, tm:tm + ATTN_BLOCK]

    y_a = _dot(za, wa_ref[...])
    y_b = _dot(zbt_buf[...].T.astype(BF16), wb_ref[...])
    g_a = _sigmoid(_dot(hb, wg_ref[:, 0:D_MODEL]) + bg_ref[:, 0:D_MODEL])
    merged = g_a * y_a
    g_b = _sigmoid(_dot(hb, wg_ref[:, D_MODEL:2 * D_MODEL]) + bg_ref[:, D_MODEL:2 * D_MODEL])
    merged = (merged + g_b * y_b).astype(BF16)
    x1_ref[...] = x + _dot(merged, wout_ref[...])


def _attn_tables(attn_sinks):
    slopes = 2.0 ** (-8.0 * (np.arange(N_HEADS) + 1) / N_HEADS)
    j = np.arange(ATTN_BLOCK)[:, None]
    i = np.arange(ATTN_BLOCK)[None, :]
    from_prev = j > i
    dist = np.where(from_prev, ATTN_BLOCK + i - j, i - j)
    bias = np.empty((2, N_KV_HEADS, ATTN_BLOCK, Q_GROUP * ATTN_BLOCK), np.float32)
    for first in range(2):
        ok = ~from_prev if first else np.ones_like(from_prev)
        for kv in range(N_KV_HEADS):
            for slot, g in enumerate(HEAD_ORDER):
                sl = np.float32(slopes[kv * Q_GROUP + g])
                val = -(sl * dist.astype(np.float32))
                bias[first, kv, :, slot * ATTN_BLOCK:(slot + 1) * ATTN_BLOCK] = np.where(ok, val, MASKED)
    sink = attn_sinks.astype(F32).reshape(N_KV_HEADS, Q_GROUP)[:, np.asarray(HEAD_ORDER)]
    sink = jnp.repeat(sink, ATTN_BLOCK, axis=1)
    return jnp.asarray(bias), sink.reshape(N_KV_HEADS, 1, Q_GROUP * ATTN_BLOCK)


def _pad_heads(w, n_heads):
    k = w.shape[0]
    w = w.reshape(k, n_heads, HEAD_DIM)
    w = jnp.pad(w, ((0, 0), (0, 0), (0, HEAD_PAD - HEAD_DIM)))
    return w.reshape(k, n_heads * HEAD_PAD)


def _const_spec(shape):
    nd = len(shape)
    return pl.BlockSpec(shape, lambda *_: (0,) * nd)


def _mixer(x2d, batch, seq, g_mix, w_in, w_pool_mix, pool_scale, w_branch_a, q_gain, k_gain,
           attn_sinks, w_branch_b, w_gate, b_gate, w_out):
    t = x2d.shape[0]
    tm = TM_MIX
    ns = seq // tm
    w_q = w_in[:, POOL_DIM:POOL_DIM + Q_DIM]
    w_k = w_in[:, POOL_DIM + Q_DIM:POOL_DIM + Q_DIM + KV_DIM]
    w_v = w_in[:, POOL_DIM + Q_DIM + KV_DIM:]
    win_p = jnp.concatenate([w_in[:, :POOL_DIM], _pad_heads(w_q, N_HEADS), _pad_heads(w_k, N_KV_HEADS),
                             _pad_heads(w_v, N_KV_HEADS)], axis=1).astype(BF16)
    qg = jnp.tile(jnp.pad(q_gain * (HEAD_DIM ** -0.5), (0, HEAD_PAD - HEAD_DIM)), N_HEADS).reshape(1, -1)
    kg = jnp.tile(jnp.pad(k_gain, (0, HEAD_PAD - HEAD_DIM)), N_KV_HEADS).reshape(1, -1)
    bias, sink = _attn_tables(attn_sinks)
    ones2 = jnp.asarray(np.kron(np.eye(2), np.ones((HEAD_PAD, HEAD_PAD))), BF16)
    operands = (x2d, g_mix.reshape(1, D_MODEL), win_p, qg, kg, bias, sink, ones2, w_pool_mix.astype(BF16),
                pool_scale.reshape(1, POOL_DIM), w_branch_a.astype(BF16), w_branch_b.astype(BF16),
                w_gate.astype(BF16), b_gate.reshape(1, 2 * D_MODEL), w_out.astype(BF16))
    in_specs = [pl.BlockSpec((tm, D_MODEL), lambda b, s: (b * ns + s, 0))]
    in_specs += [_const_spec(op.shape) for op in operands[1:]]
    return pl.pallas_call(
        _mixer_kernel,
        grid=(batch, ns),
        in_specs=in_specs,
        out_specs=pl.BlockSpec((tm, D_MODEL), lambda b, s: (b * ns + s, 0)),
        out_shape=jax.ShapeDtypeStruct((t, D_MODEL), F32),
        scratch_shapes=[
            pltpu.VMEM((ATTN_BLOCK + tm, N_KV_HEADS * HEAD_PAD), BF16),
            pltpu.VMEM((N_KV_HEADS * HEAD_PAD, ATTN_BLOCK + tm), BF16),
            pltpu.VMEM((4, POOL_TAIL + tm, POOL_DIM), F32),
            pltpu.VMEM((Q_DIM, tm), F32),
        ],
        compiler_params=pltpu.CompilerParams(
            dimension_semantics=("arbitrary", "arbitrary"), vmem_limit_bytes=VMEM_LIMIT),
        name="mixer",
    )(*operands)


def _ffn_pre_kernel(x1_ref, gffn_ref, wsg_ref, wsu_ref, wsd_ref, wrh_ref, wrl_ref, rb_ref, tri_ref,
                    xp_ref, h2p_ref, idx_ref, wt_ref, rk_ref, cnt_ref, run_ref):
    i = pl.program_id(0)
    tm = x1_ref.shape[0]

    @pl.when(i == 0)
    def _():
        run_ref[...] = jnp.zeros_like(run_ref)

    x1 = x1_ref[...]
    h2 = _rms(x1, gffn_ref[...])
    hb = h2.astype(BF16)
    g = _dot(hb, wsg_ref[...])
    a = (g * _sigmoid(g) * _dot(hb, wsu_ref[...])).astype(BF16)
    xp_ref[...] = x1 + _dot(a, wsd_ref[...])
    h2p_ref[...] = _pack(h2[:, 0:HALF], h2[:, HALF:D_MODEL])

    lo = (h2 - hb.astype(F32)).astype(BF16)
    wrh = wrh_ref[...]
    logits = (lax.dot_general(wrh, hb, _NT, preferred_element_type=F32)
              + lax.dot_general(wrl_ref[...], hb, _NT, preferred_element_type=F32)
              + lax.dot_general(wrh, lo, _NT, preferred_element_type=F32))
    scores = _sigmoid(logits)
    choice = scores + rb_ref[...]

    neg = -jnp.inf
    sub8 = lax.broadcasted_iota(I32, (GROUP_SIZE, tm), 0).astype(F32)
    grp_rows = []
    for gi in range(N_EXPERT_GROUPS):
        cg = choice[GROUP_SIZE * gi:GROUP_SIZE * (gi + 1)]
        m1 = jnp.max(cg, axis=0, keepdims=True)
        first = jnp.min(jnp.where(cg == m1, sub8, float(GROUP_SIZE)), axis=0, keepdims=True)
        m2 = jnp.max(jnp.where(sub8 == first, neg, cg), axis=0, keepdims=True)
        grp_rows.append(m1 + m2)
    gs = jnp.concatenate(grp_rows, axis=0)
    beaten = jnp.zeros((N_EXPERT_GROUPS, tm), F32)
    for gi in range(N_EXPERT_GROUPS):
        row = grp_rows[gi]
        wins = (row > gs) | ((row == gs) & (sub8 > float(gi)))
        beaten = beaten + jnp.where(wins, 1.0, 0.0)
    gsel = beaten < float(TOPK_GROUPS)
    cm = jnp.concatenate(
        [jnp.where(gsel[gi:gi + 1], choice[GROUP_SIZE * gi:GROUP_SIZE * (gi + 1)], neg)
         for gi in range(N_EXPERT_GROUPS)], axis=0)

    iota_e = lax.broadcasted_iota(I32, (N_EXPERTS, tm), 0).astype(F32)
    idx_rows, s_rows = [], []
    sel = jnp.zeros((N_EXPERTS, tm), F32)
    for _ in range(TOP_K):
        m = jnp.max(cm, axis=0, keepdims=True)
        idx = jnp.min(jnp.where(cm == m, iota_e, float(N_EXPERTS)), axis=0, keepdims=True)
        oh = iota_e == idx
        s_rows.append(jnp.sum(jnp.where(oh, scores, 0.0), axis=0, keepdims=True))
        idx_rows.append(idx)
        cm = jnp.where(oh, neg, cm)
        sel = sel + jnp.where(oh, 1.0, 0.0)

    run = run_ref[:, 0:1]
    cum = _dot(sel.astype(BF16), tri_ref[...])
    before = run + cum - sel
    rk_rows = [jnp.sum(jnp.where(iota_e == idx, before, 0.0), axis=0, keepdims=True) for idx in idx_rows]
    new_run = run + jnp.sum(sel, axis=1, keepdims=True)
    run_ref[...] = jnp.broadcast_to(new_run, run_ref.shape)
    cnt_ref[...] = jnp.broadcast_to(new_run, cnt_ref.shape)

    ssum = s_rows[0]
    for r in range(1, TOP_K):
        ssum = ssum + s_rows[r]
    denom = ssum + 1e-20
    idx_ref[...] = jnp.concatenate(idx_rows, axis=0).astype(I32)
    wt_ref[...] = jnp.concatenate([sr / denom * ROUTED_SCALE for sr in s_rows], axis=0)
    rk_ref[...] = jnp.concatenate(rk_rows, axis=0).astype(I32)


def _ffn_pre(x1, g_ffn, w_sh_gate, w_sh_up, w_sh_down, w_router, router_bias):
    t = x1.shape[0]
    tm = TM_FFN
    wr_t = w_router.astype(F32).T
    wr_hi = wr_t.astype(BF16)
    wr_lo = (wr_t - wr_hi.astype(F32)).astype(BF16)
    tri = (np.arange(tm)[:, None] <= np.arange(tm)[None, :]).astype(np.float32)
    operands = (x1, g_ffn.reshape(1, D_MODEL), w_sh_gate.astype(BF16), w_sh_up.astype(BF16),
                w_sh_down.astype(BF16), wr_hi, wr_lo, router_bias.astype(F32).reshape(N_EXPERTS, 1),
                jnp.asarray(tri, BF16))
    in_specs = [pl.BlockSpec((tm, D_MODEL), lambda i: (i, 0))]
    in_specs += [_const_spec(op.shape) for op in operands[1:]]
    row8 = pl.BlockSpec((TOP_K, tm), lambda i: (0, i))
    return pl.pallas_call(
        _ffn_pre_kernel,
        grid=(t // tm,),
        in_specs=in_specs,
        out_specs=[pl.BlockSpec((tm, D_MODEL), lambda i: (i, 0)),
                   pl.BlockSpec((tm, HALF), lambda i: (i, 0)),
                   row8, row8, row8,
                   pl.BlockSpec((N_EXPERTS, 128), lambda i: (0, 0))],
        out_shape=[jax.ShapeDtypeStruct((t, D_MODEL), F32),
                   jax.ShapeDtypeStruct((t, HALF), I32),
                   jax.ShapeDtypeStruct((TOP_K, t), I32),
                   jax.ShapeDtypeStruct((TOP_K, t), F32),
                   jax.ShapeDtypeStruct((TOP_K, t), I32),
                   jax.ShapeDtypeStruct((N_EXPERTS, 128), F32)],
        scratch_shapes=[pltpu.VMEM((N_EXPERTS, 128), F32)],
        compiler_params=pltpu.CompilerParams(
            dimension_semantics=("arbitrary",), vmem_limit_bytes=VMEM_LIMIT),
        name="ffn_pre",
    )(*operands)


def _positions_kernel(pstart_ref, idx_ref, rk_ref, pos_ref):
    idx = idx_ref[...]
    acc = rk_ref[...]
    for e in range(N_EXPERTS):
        acc = acc + jnp.where(idx == e, pstart_ref[e], 0)
    pos_ref[...] = acc


def _positions(pstart, idx8, rk8):
    t = idx8.shape[1]
    tm = TM_POS
    spec = pl.BlockSpec((TOP_K, tm), lambda i, ps: (0, i))
    return pl.pallas_call(
        _positions_kernel,
        grid_spec=pltpu.PrefetchScalarGridSpec(num_scalar_prefetch=1, grid=(t // tm,),
                                               in_specs=[spec, spec], out_specs=spec),
        out_shape=jax.ShapeDtypeStruct((TOP_K, t), I32),
        name="positions",
    )(pstart, idx8, rk8)


def _sc_workers():
    info = plsc.get_sparse_core_info()
    return info.num_cores, info.num_cores * info.num_subcores


def _dispatch(pos, h2p, n_rows):
    t = h2p.shape[0]
    ch = SC_SCATTER_ROWS
    n_cores, n_workers = _sc_workers()
    n_ch = t // (n_workers * ch)
    pos3 = pos.reshape(TOP_K, t // ch, ch).transpose(1, 0, 2)

    def body(pos_hbm, h_hbm, xs_hbm, idx_v, rows_v, sem):
        wid = lax.axis_index("s") * n_cores + lax.axis_index("c")

        @pl.loop(0, n_ch)
        def _(c):
            chunk = wid * n_ch + c
            pltpu.sync_copy(h_hbm.at[pl.ds(chunk * ch, ch)], rows_v)
            pltpu.sync_copy(pos_hbm.at[chunk], idx_v)
            copies = [pltpu.async_copy(rows_v, xs_hbm.at[idx_v.at[k]], sem) for k in range(TOP_K)]
            for cp in copies:
                cp.wait()

    return pl.kernel(
        body,
        out_type=jax.ShapeDtypeStruct((n_rows, HALF), I32),
        mesh=plsc.VectorSubcoreMesh(core_axis_name="c", subcore_axis_name="s"),
        scratch_types=[pltpu.VMEM((TOP_K, ch), I32), pltpu.VMEM((ch, HALF), I32),
                       pltpu.SemaphoreType.DMA],
        name="dispatch",
    )(pos3, h2p)


def _gather_back(pos, ys):
    t = pos.shape[1]
    ch = SC_GATHER_ROWS
    n_cores, n_workers = _sc_workers()
    n_ch = t // (n_workers * ch)
    posg = pos.reshape(TOP_K, t // ch, ch).transpose(1, 0, 2).reshape(-1)

    def body(pos_hbm, ys_hbm, yt_hbm, idx_v, buf0, buf1, gsem, wsem0, wsem1):
        wid = lax.axis_index("s") * n_cores + lax.axis_index("c")
        bufs = (buf0, buf1)
        wsems = (wsem0, wsem1)

        @pl.loop(0, n_ch)
        def _(c):
            chunk = wid * n_ch + c
            pltpu.sync_copy(pos_hbm.at[pl.ds(chunk * (TOP_K * ch), TOP_K * ch)], idx_v)
            writes = [None, None]
            for k in range(TOP_K):
                b = k % 2
                if writes[b] is not None:
                    writes[b].wait()
                pltpu.async_copy(ys_hbm.at[idx_v.at[pl.ds(k * ch, ch)]], bufs[b], gsem).wait()
                writes[b] = pltpu.async_copy(bufs[b], yt_hbm.at[k, pl.ds(chunk * ch, ch)], wsems[b])
            writes[0].wait()
            writes[1].wait()

    return pl.kernel(
        body,
        out_type=jax.ShapeDtypeStruct((TOP_K, t, HALF), I32),
        mesh=plsc.VectorSubcoreMesh(core_axis_name="c", subcore_axis_name="s"),
        scratch_types=[pltpu.VMEM((TOP_K * ch,), I32), pltpu.VMEM((ch, HALF), I32),
                       pltpu.VMEM((ch, HALF), I32), pltpu.SemaphoreType.DMA,
                       pltpu.SemaphoreType.DMA, pltpu.SemaphoreType.DMA],
        name="gather_back",
    )(posg, ys)


def _expert_kernel(be_ref, nv_ref, src_ref, new_ref, xs_ref, wg_ref, wu_ref, wd_ref, ys_ref,
                   wg_s, wu_s, wd_s):
    b = pl.program_id(0)
    nv = nv_ref[b]

    @pl.when(new_ref[b] > 0)
    def _():
        wg_s[...] = wg_ref[0].astype(BF16)
        wu_s[...] = wu_ref[0].astype(BF16)
        wd_s[...] = wd_ref[0].astype(BF16)

    @pl.when(nv > 0)
    def _():
        rows = lax.broadcasted_iota(I32, xs_ref.shape, 0)
        lo, hi = _unpack(jnp.where(rows < nv, xs_ref[...], 0))
        lo = lo.astype(BF16)
        hi = hi.astype(BF16)
        g = _dot(lo, wg_s[0:HALF, :]) + _dot(hi, wg_s[HALF:D_MODEL, :])
        u = _dot(lo, wu_s[0:HALF, :]) + _dot(hi, wu_s[HALF:D_MODEL, :])
        a = (g * _sigmoid(g) * u).astype(BF16)
        y = _dot(a, wd_s[...])
        ys_ref[...] = _pack(y[:, 0:HALF], y[:, HALF:D_MODEL])


def _experts(block_e, n_valid, block_src, block_new, xs, w_gate, w_up, w_down):
    n_blocks = block_e.shape[0]
    grid_spec = pltpu.PrefetchScalarGridSpec(
        num_scalar_prefetch=4,
        grid=(n_blocks,),
        in_specs=[pl.BlockSpec((MOE_BLOCK, HALF), lambda b, be, nv, src, new: (src[b], 0)),
                  pl.BlockSpec((1, D_MODEL, EXPERT_DIM), lambda b, be, nv, src, new: (be[b], 0, 0)),
                  pl.BlockSpec((1, D_MODEL, EXPERT_DIM), lambda b, be, nv, src, new: (be[b], 0, 0)),
                  pl.BlockSpec((1, EXPERT_DIM, D_MODEL), lambda b, be, nv, src, new: (be[b], 0, 0))],
        out_specs=pl.BlockSpec((MOE_BLOCK, HALF), lambda b, be, nv, src, new: (src[b], 0)),
        scratch_shapes=[pltpu.VMEM((D_MODEL, EXPERT_DIM), BF16), pltpu.VMEM((D_MODEL, EXPERT_DIM), BF16),
                        pltpu.VMEM((EXPERT_DIM, D_MODEL), BF16)],
    )
    return pl.pallas_call(
        _expert_kernel,
        grid_spec=grid_spec,
        out_shape=jax.ShapeDtypeStruct(xs.shape, I32),
        compiler_params=pltpu.CompilerParams(
            dimension_semantics=("arbitrary",), vmem_limit_bytes=VMEM_LIMIT),
        name="experts",
    )(block_e, n_valid, block_src, block_new, xs, w_gate, w_up, w_down)


def _combine_kernel(xp_ref, yt_ref, wt_ref, p_ref, gple_ref, wpg_ref, wpp_ref, gpost_ref, out_ref):
    proj = _rms(_dot(p_ref[...].astype(BF16), wpp_ref[...]), gpost_ref[...])
    xp = xp_ref[...]
    acc_lo = xp[:, 0:HALF]
    acc_hi = xp[:, HALF:D_MODEL]
    wt = wt_ref[...]
    for k in range(TOP_K):
        lo, hi = _unpack(yt_ref[k])
        wk = wt[:, k:k + 1]
        acc_lo = acc_lo + wk * lo
        acc_hi = acc_hi + wk * hi
    x2 = jnp.concatenate([acc_lo, acc_hi], axis=1)
    gate = _sigmoid(_dot(_rms(x2, gple_ref[...]).astype(BF16), wpg_ref[...]))
    out_ref[...] = x2 + gate * proj


def _combine(xp, yt, wt_t, p2d, g_ple, w_ple_gate, w_ple_proj, g_ple_post):
    t = xp.shape[0]
    tm = TM_COMBINE
    operands = (xp, yt, wt_t, p2d, g_ple.reshape(1, D_MODEL), w_ple_gate.astype(BF16),
                w_ple_proj.astype(BF16), g_ple_post.reshape(1, D_MODEL))
    in_specs = [pl.BlockSpec((tm, D_MODEL), lambda i: (i, 0)),
                pl.BlockSpec((TOP_K, tm, HALF), lambda i: (0, i, 0)),
                pl.BlockSpec((tm, TOP_K), lambda i: (i, 0)),
                pl.BlockSpec((tm, PLE_DIM), lambda i: (i, 0)),
                _const_spec((1, D_MODEL)),
                _const_spec((D_MODEL, D_MODEL)),
                _const_spec((PLE_DIM, D_MODEL)),
                _const_spec((1, D_MODEL))]
    return pl.pallas_call(
        _combine_kernel,
        grid=(t // tm,),
        in_specs=in_specs,
        out_specs=pl.BlockSpec((tm, D_MODEL), lambda i: (i, 0)),
        out_shape=jax.ShapeDtypeStruct((t, D_MODEL), F32),
        compiler_params=pltpu.CompilerParams(
            dimension_semantics=("arbitrary",), vmem_limit_bytes=VMEM_LIMIT),
        name="combine",
    )(*operands)


def _layer(x2d, p2d, batch, seq, g_mix, w_in, w_pool_mix, pool_scale, w_branch_a, q_gain, k_gain,
           attn_sinks, w_branch_b, w_gate, b_gate, w_out, g_ffn, w_router, router_bias, w_exp_gate,
           w_exp_up, w_exp_down, w_sh_gate, w_sh_up, w_sh_down, g_ple, w_ple_gate, w_ple_proj,
           g_ple_post):
    t = x2d.shape[0]
    x1 = _mixer(x2d, batch, seq, g_mix, w_in, w_pool_mix, pool_scale, w_branch_a, q_gain, k_gain,
                attn_sinks, w_branch_b, w_gate, b_gate, w_out)
    xp, h2p, idx8, wt8, rk8, cnt = _ffn_pre(x1, g_ffn, w_sh_gate, w_sh_up, w_sh_down, w_router,
                                            router_bias)

    counts = cnt[:, 0].astype(I32)
    padded = (counts + MOE_BLOCK - 1) // MOE_BLOCK * MOE_BLOCK
    pend = jnp.cumsum(padded)
    pstart = pend - padded
    pos = _positions(pstart.astype(I32), idx8, rk8)
    n_blocks = -(-(t * TOP_K) // MOE_BLOCK) + N_EXPERTS
    n_used = pend[-1] // MOE_BLOCK
    block_src = jnp.minimum(jnp.arange(n_blocks, dtype=I32), n_used - 1)
    blk_row = block_src * MOE_BLOCK
    block_e = jnp.minimum(jnp.sum(pend[None, :] <= blk_row[:, None], axis=1), N_EXPERTS - 1).astype(I32)
    is_e = block_e[:, None] == jnp.arange(N_EXPERTS, dtype=I32)[None, :]
    blk_count = jnp.sum(jnp.where(is_e, counts[None, :], 0), axis=1)
    blk_start = jnp.sum(jnp.where(is_e, pstart[None, :], 0), axis=1)
    n_valid = jnp.clip(blk_count - (blk_row - blk_start), 0, MOE_BLOCK)
    n_valid = jnp.where(jnp.arange(n_blocks) < n_used, n_valid, 0).astype(I32)

    xs = _dispatch(pos, h2p, n_blocks * MOE_BLOCK)
    block_new = jnp.concatenate([jnp.ones((1,), I32), (block_e[1:] != block_e[:-1]).astype(I32)])
    block_new = jnp.where(n_valid > 0, block_new, 0)
    ys = _experts(block_e, n_valid, block_src, block_new, xs, w_exp_gate, w_exp_up, w_exp_down)
    yt = _gather_back(pos, ys)
    return _combine(xp, yt, wt8.T, p2d, g_ple, w_ple_gate, w_ple_proj, g_ple_post)


def kernel(x, p, g_mix, w_in, w_pool_mix, pool_scale, w_branch_a, q_gain, k_gain, attn_sinks,
           w_branch_b, w_gate, b_gate, w_out, g_ffn, w_router, router_bias, w_exp_gate, w_exp_up,
           w_exp_down, w_sh_gate, w_sh_up, w_sh_down, g_ple, w_ple_gate, w_ple_proj, g_ple_post):
    batch, seq, d = x.shape
    depth = p.shape[0]
    x2d = x.reshape(batch * seq, d)
    for i in range(depth):
        x2d = _layer(x2d, p[i].reshape(batch * seq, PLE_DIM), batch, seq, g_mix[i], w_in[i],
                     w_pool_mix[i], pool_scale[i], w_branch_a[i], q_gain[i], k_gain[i], attn_sinks[i],
                     w_branch_b[i], w_gate[i], b_gate[i], w_out[i], g_ffn[i], w_router[i],
                     router_bias[i], w_exp_gate[i], w_exp_up[i], w_exp_down[i], w_sh_gate[i],
                     w_sh_up[i], w_sh_down[i], g_ple[i], w_ple_gate[i], w_ple_proj[i], g_ple_post[i])
    return x2d.reshape(batch, seq, d)
```

```python
import functools

import numpy as np
import jax
import jax.numpy as jnp
from jax import lax
from jax.experimental import pallas as pl
from jax.experimental.pallas import tpu as pltpu
from jax.experimental.pallas import tpu_sc as plsc

F32 = jnp.float32
BF16 = jnp.bfloat16
I32 = jnp.int32

D_MODEL = 1024
PLE_DIM = 256
POOL_WINDOWS = (2, 4, 8, 16)
POOL_GROUP_DIM = 128
POOL_DIM = 512
N_HEADS = 8
N_KV_HEADS = 2
Q_GROUP = 4
HEAD_DIM = 64
HEAD_PAD = 128
Q_DIM = 512
KV_DIM = 128
ATTN_BLOCK = 128
N_EXPERTS = 64
N_EXPERT_GROUPS = 8
GROUP_SIZE = 8
TOPK_GROUPS = 4
TOP_K = 8
EXPERT_DIM = 256
ROUTED_SCALE = 2.5
MOE_BLOCK = 512
EPS = 1e-6
HALF = D_MODEL // 2
MASKED = -1e30

COL_Q = POOL_DIM
COL_K = COL_Q + N_HEADS * HEAD_PAD
COL_V = COL_K + N_KV_HEADS * HEAD_PAD
IN_PAD = COL_V + N_KV_HEADS * HEAD_PAD
HEAD_ORDER = (0, 2, 1, 3)

TM_MIX = 512
TM_FFN = 512
TM_COMBINE = 512
TM_POS = 4096
COMBINE_CHUNKS = 4
SC_SCATTER_ROWS = 128
SC_GATHER_ROWS = 64
POOL_TAIL = 8
VMEM_LIMIT = 56 * 1024 * 1024

_NT = (((1,), (1,)), ((), ()))


def _dot(a, b):
    return jnp.dot(a, b, preferred_element_type=F32)


def _rms(x, g):
    ms = jnp.mean(x * x, axis=-1, keepdims=True)
    return x * lax.rsqrt(ms + EPS) * g


def _sigmoid(x):
    return 0.5 * jnp.tanh(0.5 * x) + 0.5


def _pack(lo, hi):
    lo_bits = lax.bitcast_convert_type(lo.astype(BF16).astype(F32), I32)
    hi_bits = lax.bitcast_convert_type(hi.astype(BF16).astype(F32), I32)
    return (hi_bits & jnp.int32(-65536)) | lax.shift_right_logical(lo_bits, jnp.int32(16))


def _unpack(w):
    lo = lax.bitcast_convert_type(lax.shift_left(w, jnp.int32(16)), F32)
    hi = lax.bitcast_convert_type(w & jnp.int32(-65536), F32)
    return lo, hi


def _mixer_kernel(x_ref, gmix_ref, win_ref, qg_ref, kg_ref, bias_ref, sink_ref, ones_ref, wmix_ref,
                  pscale_ref, wa_ref, wb_ref, wg_ref, bg_ref, wout_ref,
                  x1_ref, kbuf, vt_buf, pbuf, zbt_buf):
    s = pl.program_id(1)
    tm = x_ref.shape[0]

    @pl.when(s == 0)
    def _():
        kbuf[0:ATTN_BLOCK, :] = jnp.zeros((ATTN_BLOCK, kbuf.shape[1]), BF16)
        vt_buf[:, 0:ATTN_BLOCK] = jnp.zeros((vt_buf.shape[0], ATTN_BLOCK), BF16)
        pbuf[:, 0:POOL_TAIL, :] = jnp.zeros((pbuf.shape[0], POOL_TAIL, POOL_DIM), F32)

    x = x_ref[...]
    hb = _rms(x, gmix_ref[...]).astype(BF16)
    u = _dot(hb, win_ref[...])

    up = u[:, 0:POOL_DIM]
    t0 = POOL_TAIL
    pbuf[0, t0:t0 + tm, :] = up
    s2 = up + pbuf[0, t0 - 1:t0 - 1 + tm, :]
    pbuf[1, t0:t0 + tm, 128:512] = s2[:, 128:512]
    s4 = s2[:, 128:512] + pbuf[1, t0 - 2:t0 - 2 + tm, 128:512]
    pbuf[2, t0:t0 + tm, 256:512] = s4[:, 128:384]
    s8 = s4[:, 128:384] + pbuf[2, t0 - 4:t0 - 4 + tm, 256:512]
    pbuf[3, t0:t0 + tm, 384:512] = s8[:, 128:256]
    s16 = s8[:, 128:256] + pbuf[3, t0 - 8:t0 - 8 + tm, 384:512]
    for lvl in range(4):
        pbuf[lvl, 0:POOL_TAIL, 128 * lvl:512] = pbuf[lvl, tm:tm + POOL_TAIL, 128 * lvl:512]
    wsums = (s2[:, 0:128], s4[:, 0:128], s8[:, 0:128], s16)
    tpos = (s * tm).astype(F32) + lax.broadcasted_iota(I32, (tm, 1), 0).astype(F32)
    za_parts = []
    for g, w in enumerate(POOL_WINDOWS):
        inv_count = 1.0 / jnp.minimum(tpos + 1.0, float(w))
        pooled = wsums[g] * inv_count - up[:, 128 * g:128 * (g + 1)]
        mixed = _dot(pooled.astype(BF16), wmix_ref[g]) * pscale_ref[:, 128 * g:128 * (g + 1)]
        za_parts.append(mixed.astype(BF16))
    za = jnp.concatenate(za_parts, axis=1)

    ones2 = ones_ref[...]
    q = u[:, COL_Q:COL_K]
    q2 = (q * q).astype(BF16)
    qss = jnp.concatenate([_dot(q2[:, 256 * c:256 * (c + 1)], ones2) for c in range(N_HEADS // 2)], axis=1)
    qn = (q * lax.rsqrt(qss * (1.0 / HEAD_DIM) + EPS) * qg_ref[...]).astype(BF16)
    k = u[:, COL_K:COL_V]
    kss = _dot((k * k).astype(BF16), ones2)
    kbuf[ATTN_BLOCK:ATTN_BLOCK + tm, :] = (k * lax.rsqrt(kss * (1.0 / HEAD_DIM) + EPS) * kg_ref[...]).astype(BF16)
    vt_buf[:, ATTN_BLOCK:ATTN_BLOCK + tm] = u[:, COL_V:IN_PAD].T.astype(BF16)

    nq = Q_GROUP * ATTN_BLOCK
    key_j = lax.broadcasted_iota(I32, (ATTN_BLOCK, nq), 0)
    qry_i = lax.broadcasted_iota(I32, (ATTN_BLOCK, nq), 1) & (ATTN_BLOCK - 1)
    from_prev = key_j > qry_i
    first = jnp.where(s == 0, 1, 0)
    for n in range(tm // ATTN_BLOCK):
        r0 = ATTN_BLOCK * n
        for kv in range(N_KV_HEADS):
            c0 = HEAD_PAD * kv
            qs = jnp.concatenate(
                [qn[r0:r0 + ATTN_BLOCK, HEAD_PAD * (kv * Q_GROUP + g):HEAD_PAD * (kv * Q_GROUP + g + 1)]
                 for g in HEAD_ORDER], axis=0)
            kk = kbuf[r0:r0 + 2 * ATTN_BLOCK, c0:c0 + HEAD_PAD]
            st = lax.dot_general(kk, qs, _NT, preferred_element_type=F32)
            sc = jnp.where(from_prev, st[0:ATTN_BLOCK], st[ATTN_BLOCK:2 * ATTN_BLOCK])
            sc = sc + (bias_ref[first, kv] if n == 0 else bias_ref[0, kv])
            sink = sink_ref[kv]
            m = jnp.maximum(jnp.max(sc, axis=0, keepdims=True), sink)
            e = jnp.exp(sc - m)
            inv_den = 1.0 / (jnp.sum(e, axis=0, keepdims=True) + jnp.exp(sink - m))
            pt = jnp.concatenate([jnp.where(from_prev, e, 0.0), jnp.where(from_prev, 0.0, e)],
                                 axis=0).astype(BF16)
            v_even = vt_buf[c0:c0 + HEAD_PAD, r0:r0 + 2 * ATTN_BLOCK]
            v_odd = jnp.concatenate([v_even[HEAD_DIM:HEAD_PAD], v_even[0:HEAD_DIM]], axis=0)
            half = 2 * ATTN_BLOCK
            o = (_dot(v_even, pt[:, 0:half]) * inv_den[:, 0:half]
                 + _dot(v_odd, pt[:, half:2 * half]) * inv_den[:, half:2 * half])
            ch = kv * Q_GROUP * HEAD_DIM
            zbt_buf[ch:ch + HEAD_PAD, r0:r0 + ATTN_BLOCK] = o[:, 0:ATTN_BLOCK]
            zbt_buf[ch + HEAD_PAD:ch + 2 * HEAD_PAD, r0:r0 + ATTN_BLOCK] = o[:, ATTN_BLOCK:half]
    kbuf[0:ATTN_BLOCK, :] = kbuf[tm:tm + ATTN_BLOCK, :]
    vt_buf[:, 0:ATTN_BLOCK] = vt_buf[:, tm:tm + ATTN_BLOCK]

    y_a = _dot(za, wa_ref[...])
    y_b = _dot(zbt_buf[...].T.astype(BF16), wb_ref[...])
    g_a = _sigmoid(_dot(hb, wg_ref[:, 0:D_MODEL]) + bg_ref[:, 0:D_MODEL])
    merged = g_a * y_a
    g_b = _sigmoid(_dot(hb, wg_ref[:, D_MODEL:2 * D_MODEL]) + bg_ref[:, D_MODEL:2 * D_MODEL])
    merged = (merged + g_b * y_b).astype(BF16)
    x1_ref[...] = x + _dot(merged, wout_ref[...])


def _attn_tables(attn_sinks):
    slopes = 2.0 ** (-8.0 * (np.arange(N_HEADS) + 1) / N_HEADS)
    j = np.arange(ATTN_BLOCK)[:, None]
    i = np.arange(ATTN_BLOCK)[None, :]
    from_prev = j > i
    dist = np.where(from_prev, ATTN_BLOCK + i - j, i - j)
    bias = np.empty((2, N_KV_HEADS, ATTN_BLOCK, Q_GROUP * ATTN_BLOCK), np.float32)
    for first in range(2):
        ok = ~from_prev if first else np.ones_like(from_prev)
        for kv in range(N_KV_HEADS):
            for slot, g in enumerate(HEAD_ORDER):
                sl = np.float32(slopes[kv * Q_GROUP + g])
                val = -(sl * dist.astype(np.float32))
                bias[first, kv, :, slot * ATTN_BLOCK:(slot + 1) * ATTN_BLOCK] = np.where(ok, val, MASKED)
    sink = attn_sinks.astype(F32).reshape(N_KV_HEADS, Q_GROUP)[:, np.asarray(HEAD_ORDER)]
    sink = jnp.repeat(sink, ATTN_BLOCK, axis=1)
    return jnp.asarray(bias), sink.reshape(N_KV_HEADS, 1, Q_GROUP * ATTN_BLOCK)


def _pad_heads(w, n_heads):
    k = w.shape[0]
    w = w.reshape(k, n_heads, HEAD_DIM)
    w = jnp.pad(w, ((0, 0), (0, 0), (0, HEAD_PAD - HEAD_DIM)))
    return w.reshape(k, n_heads * HEAD_PAD)


def _const_spec(shape):
    nd = len(shape)
    return pl.BlockSpec(shape, lambda *_: (0,) * nd)


def _mixer(x2d, batch, seq, g_mix, w_in, w_pool_mix, pool_scale, w_branch_a, q_gain, k_gain,
           attn_sinks, w_branch_b, w_gate, b_gate, w_out):
    t = x2d.shape[0]
    tm = TM_MIX
    ns = seq // tm
    w_q = w_in[:, POOL_DIM:POOL_DIM + Q_DIM]
    w_k = w_in[:, POOL_DIM + Q_DIM:POOL_DIM + Q_DIM + KV_DIM]
    w_v = w_in[:, POOL_DIM + Q_DIM + KV_DIM:]
    win_p = jnp.concatenate([w_in[:, :POOL_DIM], _pad_heads(w_q, N_HEADS), _pad_heads(w_k, N_KV_HEADS),
                             _pad_heads(w_v, N_KV_HEADS)], axis=1).astype(BF16)
    qg = jnp.tile(jnp.pad(q_gain * (HEAD_DIM ** -0.5), (0, HEAD_PAD - HEAD_DIM)), N_HEADS).reshape(1, -1)
    kg = jnp.tile(jnp.pad(k_gain, (0, HEAD_PAD - HEAD_DIM)), N_KV_HEADS).reshape(1, -1)
    bias, sink = _attn_tables(attn_sinks)
    ones2 = jnp.asarray(np.kron(np.eye(2), np.ones((HEAD_PAD, HEAD_PAD))), BF16)
    operands = (x2d, g_mix.reshape(1, D_MODEL), win_p, qg, kg, bias, sink, ones2, w_pool_mix.astype(BF16),
                pool_scale.reshape(1, POOL_DIM), w_branch_a.astype(BF16), w_branch_b.astype(BF16),
                w_gate.astype(BF16), b_gate.reshape(1, 2 * D_MODEL), w_out.astype(BF16))
    in_specs = [pl.BlockSpec((tm, D_MODEL), lambda b, s: (b * ns + s, 0))]
    in_specs += [_const_spec(op.shape) for op in operands[1:]]
    return pl.pallas_call(
        _mixer_kernel,
        grid=(batch, ns),
        in_specs=in_specs,
        out_specs=pl.BlockSpec((tm, D_MODEL), lambda b, s: (b * ns + s, 0)),
        out_shape=jax.ShapeDtypeStruct((t, D_MODEL), F32),
        scratch_shapes=[
            pltpu.VMEM((ATTN_BLOCK + tm, N_KV_HEADS * HEAD_PAD), BF16),
            pltpu.VMEM((N_KV_HEADS * HEAD_PAD, ATTN_BLOCK + tm), BF16),
            pltpu.VMEM((4, POOL_TAIL + tm, POOL_DIM), F32),
            pltpu.VMEM((Q_DIM, tm), F32),
        ],
        compiler_params=pltpu.CompilerParams(
            dimension_semantics=("arbitrary", "arbitrary"), vmem_limit_bytes=VMEM_LIMIT),
        name="mixer",
    )(*operands)


def _ffn_pre_kernel(x1_ref, gffn_ref, wrh_ref, wrl_ref, rb_ref, tri_ref,
                    h2p_ref, idx_ref, wt_ref, rk_ref, cnt_ref, run_ref):
    i = pl.program_id(0)
    tm = x1_ref.shape[0]

    @pl.when(i == 0)
    def _():
        run_ref[...] = jnp.zeros_like(run_ref)

    h2 = _rms(x1_ref[...], gffn_ref[...])
    hb = h2.astype(BF16)
    h2p_ref[...] = _pack(h2[:, 0:HALF], h2[:, HALF:D_MODEL])

    lo = (h2 - hb.astype(F32)).astype(BF16)
    wrh = wrh_ref[...]
    logits = (lax.dot_general(wrh, hb, _NT, preferred_element_type=F32)
              + lax.dot_general(wrl_ref[...], hb, _NT, preferred_element_type=F32)
              + lax.dot_general(wrh, lo, _NT, preferred_element_type=F32))
    scores = _sigmoid(logits)
    choice = scores + rb_ref[...]

    neg = -jnp.inf
    sub8 = lax.broadcasted_iota(I32, (GROUP_SIZE, tm), 0).astype(F32)
    grp_rows = []
    for gi in range(N_EXPERT_GROUPS):
        cg = choice[GROUP_SIZE * gi:GROUP_SIZE * (gi + 1)]
        m1 = jnp.max(cg, axis=0, keepdims=True)
        first = jnp.min(jnp.where(cg == m1, sub8, float(GROUP_SIZE)), axis=0, keepdims=True)
        m2 = jnp.max(jnp.where(sub8 == first, neg, cg), axis=0, keepdims=True)
        grp_rows.append(m1 + m2)
    gs = jnp.concatenate(grp_rows, axis=0)
    beaten = jnp.zeros((N_EXPERT_GROUPS, tm), F32)
    for gi in range(N_EXPERT_GROUPS):
        row = grp_rows[gi]
        wins = (row > gs) | ((row == gs) & (sub8 > float(gi)))
        beaten = beaten + jnp.where(wins, 1.0, 0.0)
    gsel = beaten < float(TOPK_GROUPS)
    cm = jnp.concatenate(
        [jnp.where(gsel[gi:gi + 1], choice[GROUP_SIZE * gi:GROUP_SIZE * (gi + 1)], neg)
         for gi in range(N_EXPERT_GROUPS)], axis=0)

    iota_e = lax.broadcasted_iota(I32, (N_EXPERTS, tm), 0).astype(F32)
    idx_rows, s_rows = [], []
    sel = jnp.zeros((N_EXPERTS, tm), F32)
    for _ in range(TOP_K):
        m = jnp.max(cm, axis=0, keepdims=True)
        idx = jnp.min(jnp.where(cm == m, iota_e, float(N_EXPERTS)), axis=0, keepdims=True)
        oh = iota_e == idx
        s_rows.append(jnp.sum(jnp.where(oh, scores, 0.0), axis=0, keepdims=True))
        idx_rows.append(idx)
        cm = jnp.where(oh, neg, cm)
        sel = sel + jnp.where(oh, 1.0, 0.0)

    run = run_ref[:, 0:1]
    cum = _dot(sel.astype(BF16), tri_ref[...])
    before = run + cum - sel
    rk_rows = [jnp.sum(jnp.where(iota_e == idx, before, 0.0), axis=0, keepdims=True) for idx in idx_rows]
    new_run = run + jnp.sum(sel, axis=1, keepdims=True)
    run_ref[...] = jnp.broadcast_to(new_run, run_ref.shape)
    cnt_ref[...] = jnp.broadcast_to(new_run, cnt_ref.shape)

    ssum = s_rows[0]
    for r in range(1, TOP_K):
        ssum = ssum + s_rows[r]
    denom = ssum + 1e-20
    idx_ref[...] = jnp.concatenate(idx_rows, axis=0).astype(I32)
    wt_ref[...] = jnp.concatenate([sr / denom * ROUTED_SCALE for sr in s_rows], axis=0)
    rk_ref[...] = jnp.concatenate(rk_rows, axis=0).astype(I32)


def _ffn_pre(x1, g_ffn, w_router, router_bias):
    t = x1.shape[0]
    tm = TM_FFN
    wr_t = w_router.astype(F32).T
    wr_hi = wr_t.astype(BF16)
    wr_lo = (wr_t - wr_hi.astype(F32)).astype(BF16)
    tri = (np.arange(tm)[:, None] <= np.arange(tm)[None, :]).astype(np.float32)
    operands = (x1, g_ffn.reshape(1, D_MODEL), wr_hi, wr_lo,
                router_bias.astype(F32).reshape(N_EXPERTS, 1), jnp.asarray(tri, BF16))
    in_specs = [pl.BlockSpec((tm, D_MODEL), lambda i: (i, 0))]
    in_specs += [_const_spec(op.shape) for op in operands[1:]]
    row8 = pl.BlockSpec((TOP_K, tm), lambda i: (0, i))
    return pl.pallas_call(
        _ffn_pre_kernel,
        grid=(t // tm,),
        in_specs=in_specs,
        out_specs=[pl.BlockSpec((tm, HALF), lambda i: (i, 0)),
                   row8, row8, row8,
                   pl.BlockSpec((N_EXPERTS, 128), lambda i: (0, 0))],
        out_shape=[jax.ShapeDtypeStruct((t, HALF), I32),
                   jax.ShapeDtypeStruct((TOP_K, t), I32),
                   jax.ShapeDtypeStruct((TOP_K, t), F32),
                   jax.ShapeDtypeStruct((TOP_K, t), I32),
                   jax.ShapeDtypeStruct((N_EXPERTS, 128), F32)],
        scratch_shapes=[pltpu.VMEM((N_EXPERTS, 128), F32)],
        compiler_params=pltpu.CompilerParams(
            dimension_semantics=("arbitrary",), vmem_limit_bytes=VMEM_LIMIT),
        name="ffn_pre",
    )(*operands)


def _swiglu_packed(x_packed, wg, wu, wd):
    lo, hi = _unpack(x_packed)
    lo = lo.astype(BF16)
    hi = hi.astype(BF16)
    g = _dot(lo, wg[0:HALF, :]) + _dot(hi, wg[HALF:D_MODEL, :])
    u = _dot(lo, wu[0:HALF, :]) + _dot(hi, wu[HALF:D_MODEL, :])
    a = (g * _sigmoid(g) * u).astype(BF16)
    y = _dot(a, wd[...])
    return _pack(y[:, 0:HALF], y[:, HALF:D_MODEL])


def _shared_kernel(h2p_ref, wg_ref, wu_ref, wd_ref, ysh_ref):
    ysh_ref[...] = _swiglu_packed(h2p_ref[...], wg_ref, wu_ref, wd_ref)


def _shared(h2p, w_sh_gate, w_sh_up, w_sh_down):
    t = h2p.shape[0]
    tm = TM_FFN
    operands = (h2p, w_sh_gate.astype(BF16), w_sh_up.astype(BF16), w_sh_down.astype(BF16))
    in_specs = [pl.BlockSpec((tm, HALF), lambda i: (i, 0))]
    in_specs += [_const_spec(op.shape) for op in operands[1:]]
    return pl.pallas_call(
        _shared_kernel,
        grid=(t // tm,),
        in_specs=in_specs,
        out_specs=pl.BlockSpec((tm, HALF), lambda i: (i, 0)),
        out_shape=jax.ShapeDtypeStruct((t, HALF), I32),
        compiler_params=pltpu.CompilerParams(
            dimension_semantics=("arbitrary",), vmem_limit_bytes=VMEM_LIMIT),
        name="shared",
    )(*operands)


def _positions_kernel(pstart_ref, idx_ref, rk_ref, pos_ref):
    idx = idx_ref[...]
    acc = rk_ref[...]
    for e in range(N_EXPERTS):
        acc = acc + jnp.where(idx == e, pstart_ref[e], 0)
    pos_ref[...] = acc


def _positions(pstart, idx8, rk8):
    t = idx8.shape[1]
    tm = TM_POS
    spec = pl.BlockSpec((TOP_K, tm), lambda i, ps: (0, i))
    return pl.pallas_call(
        _positions_kernel,
        grid_spec=pltpu.PrefetchScalarGridSpec(num_scalar_prefetch=1, grid=(t // tm,),
                                               in_specs=[spec, spec], out_specs=spec),
        out_shape=jax.ShapeDtypeStruct((TOP_K, t), I32),
        name="positions",
    )(pstart, idx8, rk8)


def _sc_workers():
    info = plsc.get_sparse_core_info()
    return info.num_cores, info.num_cores * info.num_subcores


def _dispatch(pos, h2p, n_rows):
    t = h2p.shape[0]
    ch = SC_SCATTER_ROWS
    n_cores, n_workers = _sc_workers()
    n_ch = t // (n_workers * ch)
    pos3 = pos.reshape(TOP_K, t // ch, ch).transpose(1, 0, 2)

    def body(pos_hbm, h_hbm, xs_hbm, idx_v, rows_v, sem):
        wid = lax.axis_index("s") * n_cores + lax.axis_index("c")

        @pl.loop(0, n_ch)
        def _(c):
            chunk = wid * n_ch + c
            pltpu.sync_copy(h_hbm.at[pl.ds(chunk * ch, ch)], rows_v)
            pltpu.sync_copy(pos_hbm.at[chunk], idx_v)
            copies = [pltpu.async_copy(rows_v, xs_hbm.at[idx_v.at[k]], sem) for k in range(TOP_K)]
            for cp in copies:
                cp.wait()

    return pl.kernel(
        body,
        out_type=jax.ShapeDtypeStruct((n_rows, HALF), I32),
        mesh=plsc.VectorSubcoreMesh(core_axis_name="c", subcore_axis_name="s"),
        scratch_types=[pltpu.VMEM((TOP_K, ch), I32), pltpu.VMEM((ch, HALF), I32),
                       pltpu.SemaphoreType.DMA],
        name="dispatch",
    )(pos3, h2p)


def _gather_back(pos, ys):
    t = pos.shape[1]
    ch = SC_GATHER_ROWS
    n_cores, n_workers = _sc_workers()
    n_ch = t // (n_workers * ch)
    posg = pos.reshape(TOP_K, t // ch, ch).transpose(1, 0, 2).reshape(-1)

    def body(pos_hbm, ys_hbm, yt_hbm, idx_v, buf0, buf1, gsem, wsem0, wsem1):
        wid = lax.axis_index("s") * n_cores + lax.axis_index("c")
        bufs = (buf0, buf1)
        wsems = (wsem0, wsem1)

        @pl.loop(0, n_ch)
        def _(c):
            chunk = wid * n_ch + c
            pltpu.sync_copy(pos_hbm.at[pl.ds(chunk * (TOP_K * ch), TOP_K * ch)], idx_v)
            writes = [None, None]
            for k in range(TOP_K):
                b = k % 2
                if writes[b] is not None:
                    writes[b].wait()
                pltpu.async_copy(ys_hbm.at[idx_v.at[pl.ds(k * ch, ch)]], bufs[b], gsem).wait()
                writes[b] = pltpu.async_copy(bufs[b], yt_hbm.at[k, pl.ds(chunk * ch, ch)], wsems[b])
            writes[0].wait()
            writes[1].wait()

    return pl.kernel(
        body,
        out_type=jax.ShapeDtypeStruct((TOP_K, t, HALF), I32),
        mesh=plsc.VectorSubcoreMesh(core_axis_name="c", subcore_axis_name="s"),
        scratch_types=[pltpu.VMEM((TOP_K * ch,), I32), pltpu.VMEM((ch, HALF), I32),
                       pltpu.VMEM((ch, HALF), I32), pltpu.SemaphoreType.DMA,
                       pltpu.SemaphoreType.DMA, pltpu.SemaphoreType.DMA],
        name="gather_back",
    )(posg, ys)


def _expert_kernel(be_ref, nv_ref, src_ref, new_ref, xs_ref, wg_ref, wu_ref, wd_ref, ys_ref,
                   wg_s, wu_s, wd_s):
    b = pl.program_id(0)
    nv = nv_ref[b]

    @pl.when(new_ref[b] > 0)
    def _():
        wg_s[...] = wg_ref[0].astype(BF16)
        wu_s[...] = wu_ref[0].astype(BF16)
        wd_s[...] = wd_ref[0].astype(BF16)

    @pl.when(nv > 0)
    def _():
        rows = lax.broadcasted_iota(I32, xs_ref.shape, 0)
        x = jnp.where(rows < nv, xs_ref[...], 0)
        ys_ref[...] = _swiglu_packed(x, wg_s, wu_s, wd_s)


def _experts(block_e, n_valid, block_src, block_new, xs, w_gate, w_up, w_down):
    n_blocks = block_e.shape[0]
    grid_spec = pltpu.PrefetchScalarGridSpec(
        num_scalar_prefetch=4,
        grid=(n_blocks,),
        in_specs=[pl.BlockSpec((MOE_BLOCK, HALF), lambda b, be, nv, src, new: (src[b], 0)),
                  pl.BlockSpec((1, D_MODEL, EXPERT_DIM), lambda b, be, nv, src, new: (be[b], 0, 0)),
                  pl.BlockSpec((1, D_MODEL, EXPERT_DIM), lambda b, be, nv, src, new: (be[b], 0, 0)),
                  pl.BlockSpec((1, EXPERT_DIM, D_MODEL), lambda b, be, nv, src, new: (be[b], 0, 0))],
        out_specs=pl.BlockSpec((MOE_BLOCK, HALF), lambda b, be, nv, src, new: (src[b], 0)),
        scratch_shapes=[pltpu.VMEM((D_MODEL, EXPERT_DIM), BF16), pltpu.VMEM((D_MODEL, EXPERT_DIM), BF16),
                        pltpu.VMEM((EXPERT_DIM, D_MODEL), BF16)],
    )
    return pl.pallas_call(
        _expert_kernel,
        grid_spec=grid_spec,
        out_shape=jax.ShapeDtypeStruct(xs.shape, I32),
        compiler_params=pltpu.CompilerParams(
            dimension_semantics=("arbitrary",), vmem_limit_bytes=VMEM_LIMIT),
        name="experts",
    )(block_e, n_valid, block_src, block_new, xs, w_gate, w_up, w_down)


def _combine_kernel(x1_ref, ysh_ref, yt_ref, wt_ref, p_ref, gple_ref, wpg_ref, wpp_ref, gpost_ref, *rest):
    out_ref = rest[-1]
    proj = _rms(_dot(p_ref[...].astype(BF16), wpp_ref[...]), gpost_ref[...])
    x1 = x1_ref[...]
    sh_lo, sh_hi = _unpack(ysh_ref[...])
    acc_lo = x1[:, 0:HALF] + sh_lo
    acc_hi = x1[:, HALF:D_MODEL] + sh_hi
    wt = wt_ref[...]
    for k in range(TOP_K):
        lo, hi = _unpack(yt_ref[k])
        wk = wt[:, k:k + 1]
        acc_lo = acc_lo + wk * lo
        acc_hi = acc_hi + wk * hi
    x2 = jnp.concatenate([acc_lo, acc_hi], axis=1)
    gate = _sigmoid(_dot(_rms(x2, gple_ref[...]).astype(BF16), wpg_ref[...]))
    out_ref[...] = x2 + gate * proj


def _combine(x1, ysh, yt, wt_t, p2d, g_ple, w_ple_gate, w_ple_proj, g_ple_post, chunk, prev_out):
    t = x1.shape[0]
    tm = TM_COMBINE
    steps = yt.shape[1] // tm
    off = chunk * steps
    operands = [x1, ysh, yt, wt_t, p2d, g_ple.reshape(1, D_MODEL), w_ple_gate.astype(BF16),
                w_ple_proj.astype(BF16), g_ple_post.reshape(1, D_MODEL)]
    in_specs = [pl.BlockSpec((tm, D_MODEL), lambda i: (off + i, 0)),
                pl.BlockSpec((tm, HALF), lambda i: (off + i, 0)),
                pl.BlockSpec((TOP_K, tm, HALF), lambda i: (0, i, 0)),
                pl.BlockSpec((tm, TOP_K), lambda i: (off + i, 0)),
                pl.BlockSpec((tm, PLE_DIM), lambda i: (off + i, 0)),
                _const_spec((1, D_MODEL)),
                _const_spec((D_MODEL, D_MODEL)),
                _const_spec((PLE_DIM, D_MODEL)),
                _const_spec((1, D_MODEL))]
    aliases = {}
    if prev_out is not None:
        aliases = {len(operands): 0}
        operands.append(prev_out)
        in_specs.append(pl.BlockSpec(memory_space=pl.ANY))
    return pl.pallas_call(
        _combine_kernel,
        grid=(steps,),
        in_specs=in_specs,
        out_specs=pl.BlockSpec((tm, D_MODEL), lambda i: (off + i, 0)),
        out_shape=jax.ShapeDtypeStruct((t, D_MODEL), F32),
        input_output_aliases=aliases,
        compiler_params=pltpu.CompilerParams(
            dimension_semantics=("arbitrary",), vmem_limit_bytes=VMEM_LIMIT),
        name="combine",
    )(*operands)


def _layer(x2d, p2d, batch, seq, g_mix, w_in, w_pool_mix, pool_scale, w_branch_a, q_gain, k_gain,
           attn_sinks, w_branch_b, w_gate, b_gate, w_out, g_ffn, w_router, router_bias, w_exp_gate,
           w_exp_up, w_exp_down, w_sh_gate, w_sh_up, w_sh_down, g_ple, w_ple_gate, w_ple_proj,
           g_ple_post):
    t = x2d.shape[0]
    x1 = _mixer(x2d, batch, seq, g_mix, w_in, w_pool_mix, pool_scale, w_branch_a, q_gain, k_gain,
                attn_sinks, w_branch_b, w_gate, b_gate, w_out)
    h2p, idx8, wt8, rk8, cnt = _ffn_pre(x1, g_ffn, w_router, router_bias)

    counts = cnt[:, 0].astype(I32)
    padded = (counts + MOE_BLOCK - 1) // MOE_BLOCK * MOE_BLOCK
    pend = jnp.cumsum(padded)
    pstart = pend - padded
    pos = _positions(pstart.astype(I32), idx8, rk8)
    n_blocks = -(-(t * TOP_K) // MOE_BLOCK) + N_EXPERTS
    n_used = pend[-1] // MOE_BLOCK
    block_src = jnp.minimum(jnp.arange(n_blocks, dtype=I32), n_used - 1)
    blk_row = block_src * MOE_BLOCK
    block_e = jnp.minimum(jnp.sum(pend[None, :] <= blk_row[:, None], axis=1), N_EXPERTS - 1).astype(I32)
    is_e = block_e[:, None] == jnp.arange(N_EXPERTS, dtype=I32)[None, :]
    blk_count = jnp.sum(jnp.where(is_e, counts[None, :], 0), axis=1)
    blk_start = jnp.sum(jnp.where(is_e, pstart[None, :], 0), axis=1)
    n_valid = jnp.clip(blk_count - (blk_row - blk_start), 0, MOE_BLOCK)
    n_valid = jnp.where(jnp.arange(n_blocks) < n_used, n_valid, 0).astype(I32)

    xs = _dispatch(pos, h2p, n_blocks * MOE_BLOCK)
    ysh = _shared(h2p, w_sh_gate, w_sh_up, w_sh_down)
    block_new = jnp.concatenate([jnp.ones((1,), I32), (block_e[1:] != block_e[:-1]).astype(I32)])
    block_new = jnp.where(n_valid > 0, block_new, 0)
    ys = _experts(block_e, n_valid, block_src, block_new, xs, w_exp_gate, w_exp_up, w_exp_down)
    wt_t = wt8.T
    tc = t // COMBINE_CHUNKS
    out = None
    for c in range(COMBINE_CHUNKS):
        yt = _gather_back(pos[:, c * tc:(c + 1) * tc], ys)
        out = _combine(x1, ysh, yt, wt_t, p2d, g_ple, w_ple_gate, w_ple_proj, g_ple_post, c, out)
    return out


def kernel(x, p, g_mix, w_in, w_pool_mix, pool_scale, w_branch_a, q_gain, k_gain, attn_sinks,
           w_branch_b, w_gate, b_gate, w_out, g_ffn, w_router, router_bias, w_exp_gate, w_exp_up,
           w_exp_down, w_sh_gate, w_sh_up, w_sh_down, g_ple, w_ple_gate, w_ple_proj, g_ple_post):
    batch, seq, d = x.shape
    depth = p.shape[0]
    x2d = x.reshape(batch * seq, d)
    for i in range(depth):
        x2d = _layer(x2d, p[i].reshape(batch * seq, PLE_DIM), batch, seq, g_mix[i], w_in[i],
                     w_pool_mix[i], pool_scale[i], w_branch_a[i], q_gain[i], k_gain[i], attn_sinks[i],
                     w_branch_b[i], w_gate[i], b_gate[i], w_out[i], g_ffn[i], w_router[i],
                     router_bias[i], w_exp_gate[i], w_exp_up[i], w_exp_down[i], w_sh_gate[i],
                     w_sh_up[i], w_sh_down[i], g_ple[i], w_ple_gate[i], w_ple_proj[i], g_ple_post[i])
    return x2d.reshape(batch, seq, d)
```

```python
import functools

import numpy as np
import jax
import jax.numpy as jnp
from jax import lax
from jax.experimental import pallas as pl
from jax.experimental.pallas import tpu as pltpu
from jax.experimental.pallas import tpu_sc as plsc

F32 = jnp.float32
BF16 = jnp.bfloat16
I32 = jnp.int32

D_MODEL = 1024
PLE_DIM = 256
POOL_WINDOWS = (2, 4, 8, 16)
POOL_GROUP_DIM = 128
POOL_DIM = 512
N_HEADS = 8
N_KV_HEADS = 2
Q_GROUP = 4
HEAD_DIM = 64
HEAD_PAD = 128
Q_DIM = 512
KV_DIM = 128
ATTN_BLOCK = 128
N_EXPERTS = 64
N_EXPERT_GROUPS = 8
GROUP_SIZE = 8
TOPK_GROUPS = 4
TOP_K = 8
EXPERT_DIM = 256
ROUTED_SCALE = 2.5
MOE_BLOCK = 512
EPS = 1e-6
HALF = D_MODEL // 2
MASKED = -1e30

COL_Q = POOL_DIM
COL_K = COL_Q + N_HEADS * HEAD_PAD
COL_V = COL_K + N_KV_HEADS * HEAD_PAD
IN_PAD = COL_V + N_KV_HEADS * HEAD_PAD
HEAD_ORDER = (0, 2, 1, 3)

TM_MIX = 512
TM_FFN = 512
TM_COMBINE = 512
TM_POS = 4096
COMBINE_CHUNKS = 4
SC_SCATTER_ROWS = 128
SC_GATHER_ROWS = 64
POOL_TAIL = 8
VMEM_LIMIT = 56 * 1024 * 1024

_NT = (((1,), (1,)), ((), ()))


def _dot(a, b):
    return jnp.dot(a, b, preferred_element_type=F32)


def _rms(x, g):
    ms = jnp.mean(x * x, axis=-1, keepdims=True)
    return x * lax.rsqrt(ms + EPS) * g


def _sigmoid(x):
    return 0.5 * jnp.tanh(0.5 * x) + 0.5


def _pack(lo, hi):
    lo_bits = lax.bitcast_convert_type(lo.astype(BF16).astype(F32), I32)
    hi_bits = lax.bitcast_convert_type(hi.astype(BF16).astype(F32), I32)
    return (hi_bits & jnp.int32(-65536)) | lax.shift_right_logical(lo_bits, jnp.int32(16))


def _unpack(w):
    lo = lax.bitcast_convert_type(lax.shift_left(w, jnp.int32(16)), F32)
    hi = lax.bitcast_convert_type(w & jnp.int32(-65536), F32)
    return lo, hi


def _mixer_kernel(x_ref, gmix_ref, win_ref, qg_ref, kg_ref, bias_ref, sink_ref, ones_ref, wmix_ref,
                  pscale_ref, wa_ref, wb_ref, wg_ref, bg_ref, wout_ref,
                  x1_ref, kbuf, vt_buf, pbuf, zbt_buf):
    s = pl.program_id(1)
    tm = x_ref.shape[0]

    @pl.when(s == 0)
    def _():
        kbuf[0:ATTN_BLOCK, :] = jnp.zeros((ATTN_BLOCK, kbuf.shape[1]), BF16)
        vt_buf[:, 0:ATTN_BLOCK] = jnp.zeros((vt_buf.shape[0], ATTN_BLOCK), BF16)
        pbuf[:, 0:POOL_TAIL, :] = jnp.zeros((pbuf.shape[0], POOL_TAIL, POOL_DIM), F32)

    x = x_ref[...]
    hb = _rms(x, gmix_ref[...]).astype(BF16)
    u = _dot(hb, win_ref[...])

    up = u[:, 0:POOL_DIM]
    t0 = POOL_TAIL
    pbuf[0, t0:t0 + tm, :] = up
    s2 = up + pbuf[0, t0 - 1:t0 - 1 + tm, :]
    pbuf[1, t0:t0 + tm, 128:512] = s2[:, 128:512]
    s4 = s2[:, 128:512] + pbuf[1, t0 - 2:t0 - 2 + tm, 128:512]
    pbuf[2, t0:t0 + tm, 256:512] = s4[:, 128:384]
    s8 = s4[:, 128:384] + pbuf[2, t0 - 4:t0 - 4 + tm, 256:512]
    pbuf[3, t0:t0 + tm, 384:512] = s8[:, 128:256]
    s16 = s8[:, 128:256] + pbuf[3, t0 - 8:t0 - 8 + tm, 384:512]
    for lvl in range(4):
        pbuf[lvl, 0:POOL_TAIL, 128 * lvl:512] = pbuf[lvl, tm:tm + POOL_TAIL, 128 * lvl:512]
    wsums = (s2[:, 0:128], s4[:, 0:128], s8[:, 0:128], s16)
    tpos = (s * tm).astype(F32) + lax.broadcasted_iota(I32, (tm, 1), 0).astype(F32)
    za_parts = []
    for g, w in enumerate(POOL_WINDOWS):
        inv_count = 1.0 / jnp.minimum(tpos + 1.0, float(w))
        pooled = wsums[g] * inv_count - up[:, 128 * g:128 * (g + 1)]
        mixed = _dot(pooled.astype(BF16), wmix_ref[g]) * pscale_ref[:, 128 * g:128 * (g + 1)]
        za_parts.append(mixed.astype(BF16))
    za = jnp.concatenate(za_parts, axis=1)

    ones2 = ones_ref[...]
    q = u[:, COL_Q:COL_K]
    q2 = (q * q).astype(BF16)
    qss = jnp.concatenate([_dot(q2[:, 256 * c:256 * (c + 1)], ones2) for c in range(N_HEADS // 2)], axis=1)
    qn = (q * lax.rsqrt(qss * (1.0 / HEAD_DIM) + EPS) * qg_ref[...]).astype(BF16)
    k = u[:, COL_K:COL_V]
    kss = _dot((k * k).astype(BF16), ones2)
    kbuf[ATTN_BLOCK:ATTN_BLOCK + tm, :] = (k * lax.rsqrt(kss * (1.0 / HEAD_DIM) + EPS) * kg_ref[...]).astype(BF16)
    vt_buf[:, ATTN_BLOCK:ATTN_BLOCK + tm] = u[:, COL_V:IN_PAD].T.astype(BF16)

    nq = Q_GROUP * ATTN_BLOCK
    key_j = lax.broadcasted_iota(I32, (ATTN_BLOCK, nq), 0)
    qry_i = lax.broadcasted_iota(I32, (ATTN_BLOCK, nq), 1) & (ATTN_BLOCK - 1)
    from_prev = key_j > qry_i
    first = jnp.where(s == 0, 1, 0)
    for n in range(tm // ATTN_BLOCK):
        r0 = ATTN_BLOCK * n
        for kv in range(N_KV_HEADS):
            c0 = HEAD_PAD * kv
            qs = jnp.concatenate(
                [qn[r0:r0 + ATTN_BLOCK, HEAD_PAD * (kv * Q_GROUP + g):HEAD_PAD * (kv * Q_GROUP + g + 1)]
                 for g in HEAD_ORDER], axis=0)
            kk = kbuf[r0:r0 + 2 * ATTN_BLOCK, c0:c0 + HEAD_PAD]
            st = lax.dot_general(kk, qs, _NT, preferred_element_type=F32)
            sc = jnp.where(from_prev, st[0:ATTN_BLOCK], st[ATTN_BLOCK:2 * ATTN_BLOCK])
            sc = sc + (bias_ref[first, kv] if n == 0 else bias_ref[0, kv])
            sink = sink_ref[kv]
            m = jnp.maximum(jnp.max(sc, axis=0, keepdims=True), sink)
            e = jnp.exp(sc - m)
            inv_den = 1.0 / (jnp.sum(e, axis=0, keepdims=True) + jnp.exp(sink - m))
            pt = jnp.concatenate([jnp.where(from_prev, e, 0.0), jnp.where(from_prev, 0.0, e)],
                                 axis=0).astype(BF16)
            v_even = vt_buf[c0:c0 + HEAD_PAD, r0:r0 + 2 * ATTN_BLOCK]
            v_odd = jnp.concatenate([v_even[HEAD_DIM:HEAD_PAD], v_even[0:HEAD_DIM]], axis=0)
            half = 2 * ATTN_BLOCK
            o = (_dot(v_even, pt[:, 0:half]) * inv_den[:, 0:half]
                 + _dot(v_odd, pt[:, half:2 * half]) * inv_den[:, half:2 * half])
            ch = kv * Q_GROUP * HEAD_DIM
            zbt_buf[ch:ch + HEAD_PAD, r0:r0 + ATTN_BLOCK] = o[:, 0:ATTN_BLOCK]
            zbt_buf[ch + HEAD_PAD:ch + 2 * HEAD_PAD, r0:r0 + ATTN_BLOCK] = o[:, ATTN_BLOCK:half]
    kbuf[0:ATTN_BLOCK, :] = kbuf[tm:tm + ATTN_BLOCK, :]
    vt_buf[:, 0:ATTN_BLOCK] = vt_buf[:, tm:tm + ATTN_BLOCK]

    y_a = _dot(za, wa_ref[...])
    y_b = _dot(zbt_buf[...].T.astype(BF16), wb_ref[...])
    g_a = _sigmoid(_dot(hb, wg_ref[:, 0:D_MODEL]) + bg_ref[:, 0:D_MODEL])
    merged = g_a * y_a
    g_b = _sigmoid(_dot(hb, wg_ref[:, D_MODEL:2 * D_MODEL]) + bg_ref[:, D_MODEL:2 * D_MODEL])
    merged = (merged + g_b * y_b).astype(BF16)
    x1_ref[...] = x + _dot(merged, wout_ref[...])


def _attn_tables(attn_sinks):
    slopes = 2.0 ** (-8.0 * (np.arange(N_HEADS) + 1) / N_HEADS)
    j = np.arange(ATTN_BLOCK)[:, None]
    i = np.arange(ATTN_BLOCK)[None, :]
    from_prev = j > i
    dist = np.where(from_prev, ATTN_BLOCK + i - j, i - j)
    bias = np.empty((2, N_KV_HEADS, ATTN_BLOCK, Q_GROUP * ATTN_BLOCK), np.float32)
    for first in range(2):
        ok = ~from_prev if first else np.ones_like(from_prev)
        for kv in range(N_KV_HEADS):
            for slot, g in enumerate(HEAD_ORDER):
                sl = np.float32(slopes[kv * Q_GROUP + g])
                val = -(sl * dist.astype(np.float32))
                bias[first, kv, :, slot * ATTN_BLOCK:(slot + 1) * ATTN_BLOCK] = np.where(ok, val, MASKED)
    sink = attn_sinks.astype(F32).reshape(N_KV_HEADS, Q_GROUP)[:, np.asarray(HEAD_ORDER)]
    sink = jnp.repeat(sink, ATTN_BLOCK, axis=1)
    return jnp.asarray(bias), sink.reshape(N_KV_HEADS, 1, Q_GROUP * ATTN_BLOCK)


def _pad_heads(w, n_heads):
    k = w.shape[0]
    w = w.reshape(k, n_heads, HEAD_DIM)
    w = jnp.pad(w, ((0, 0), (0, 0), (0, HEAD_PAD - HEAD_DIM)))
    return w.reshape(k, n_heads * HEAD_PAD)


def _const_spec(shape):
    nd = len(shape)
    return pl.BlockSpec(shape, lambda *_: (0,) * nd)


def _mixer(x2d, batch, seq, g_mix, w_in, w_pool_mix, pool_scale, w_branch_a, q_gain, k_gain,
           attn_sinks, w_branch_b, w_gate, b_gate, w_out):
    t = x2d.shape[0]
    tm = TM_MIX
    ns = seq // tm
    w_q = w_in[:, POOL_DIM:POOL_DIM + Q_DIM]
    w_k = w_in[:, POOL_DIM + Q_DIM:POOL_DIM + Q_DIM + KV_DIM]
    w_v = w_in[:, POOL_DIM + Q_DIM + KV_DIM:]
    win_p = jnp.concatenate([w_in[:, :POOL_DIM], _pad_heads(w_q, N_HEADS), _pad_heads(w_k, N_KV_HEADS),
                             _pad_heads(w_v, N_KV_HEADS)], axis=1).astype(BF16)
    qg = jnp.tile(jnp.pad(q_gain * (HEAD_DIM ** -0.5), (0, HEAD_PAD - HEAD_DIM)), N_HEADS).reshape(1, -1)
    kg = jnp.tile(jnp.pad(k_gain, (0, HEAD_PAD - HEAD_DIM)), N_KV_HEADS).reshape(1, -1)
    bias, sink = _attn_tables(attn_sinks)
    ones2 = jnp.asarray(np.kron(np.eye(2), np.ones((HEAD_PAD, HEAD_PAD))), BF16)
    operands = (x2d, g_mix.reshape(1, D_MODEL), win_p, qg, kg, bias, sink, ones2, w_pool_mix.astype(BF16),
                pool_scale.reshape(1, POOL_DIM), w_branch_a.astype(BF16), w_branch_b.astype(BF16),
                w_gate.astype(BF16), b_gate.reshape(1, 2 * D_MODEL), w_out.astype(BF16))
    in_specs = [pl.BlockSpec((tm, D_MODEL), lambda b, s: (b * ns + s, 0))]
    in_specs += [_const_spec(op.shape) for op in operands[1:]]
    return pl.pallas_call(
        _mixer_kernel,
        grid=(batch, ns),
        in_specs=in_specs,
        out_specs=pl.BlockSpec((tm, D_MODEL), lambda b, s: (b * ns + s, 0)),
        out_shape=jax.ShapeDtypeStruct((t, D_MODEL), F32),
        scratch_shapes=[
            pltpu.VMEM((ATTN_BLOCK + tm, N_KV_HEADS * HEAD_PAD), BF16),
            pltpu.VMEM((N_KV_HEADS * HEAD_PAD, ATTN_BLOCK + tm), BF16),
            pltpu.VMEM((4, POOL_TAIL + tm, POOL_DIM), F32),
            pltpu.VMEM((Q_DIM, tm), F32),
        ],
        compiler_params=pltpu.CompilerParams(
            dimension_semantics=("arbitrary", "arbitrary"), vmem_limit_bytes=VMEM_LIMIT),
        name="mixer",
    )(*operands)


def _ffn_pre_kernel(x1_ref, gffn_ref, wrh_ref, wrl_ref, rb_ref, tri_ref,
                    h2p_ref, idx_ref, wt_ref, rk_ref, cnt_ref, run_ref):
    i = pl.program_id(0)
    tm = x1_ref.shape[0]

    @pl.when(i == 0)
    def _():
        run_ref[...] = jnp.zeros_like(run_ref)

    h2 = _rms(x1_ref[...], gffn_ref[...])
    hb = h2.astype(BF16)
    h2p_ref[...] = _pack(h2[:, 0:HALF], h2[:, HALF:D_MODEL])

    lo = (h2 - hb.astype(F32)).astype(BF16)
    wrh = wrh_ref[...]
    logits = (lax.dot_general(wrh, hb, _NT, preferred_element_type=F32)
              + lax.dot_general(wrl_ref[...], hb, _NT, preferred_element_type=F32)
              + lax.dot_general(wrh, lo, _NT, preferred_element_type=F32))
    scores = _sigmoid(logits)
    choice = scores + rb_ref[...]

    neg = -jnp.inf
    sub8 = lax.broadcasted_iota(I32, (GROUP_SIZE, tm), 0).astype(F32)
    grp_rows = []
    for gi in range(N_EXPERT_GROUPS):
        cg = choice[GROUP_SIZE * gi:GROUP_SIZE * (gi + 1)]
        m1 = jnp.max(cg, axis=0, keepdims=True)
        first = jnp.min(jnp.where(cg == m1, sub8, float(GROUP_SIZE)), axis=0, keepdims=True)
        m2 = jnp.max(jnp.where(sub8 == first, neg, cg), axis=0, keepdims=True)
        grp_rows.append(m1 + m2)
    gs = jnp.concatenate(grp_rows, axis=0)
    beaten = jnp.zeros((N_EXPERT_GROUPS, tm), F32)
    for gi in range(N_EXPERT_GROUPS):
        row = grp_rows[gi]
        wins = (row > gs) | ((row == gs) & (sub8 > float(gi)))
        beaten = beaten + jnp.where(wins, 1.0, 0.0)
    gsel = beaten < float(TOPK_GROUPS)
    cm = jnp.concatenate(
        [jnp.where(gsel[gi:gi + 1], choice[GROUP_SIZE * gi:GROUP_SIZE * (gi + 1)], neg)
         for gi in range(N_EXPERT_GROUPS)], axis=0)

    iota_e = lax.broadcasted_iota(I32, (N_EXPERTS, tm), 0).astype(F32)
    idx_rows, s_rows = [], []
    sel = jnp.zeros((N_EXPERTS, tm), F32)
    for _ in range(TOP_K):
        m = jnp.max(cm, axis=0, keepdims=True)
        idx = jnp.min(jnp.where(cm == m, iota_e, float(N_EXPERTS)), axis=0, keepdims=True)
        oh = iota_e == idx
        s_rows.append(jnp.sum(jnp.where(oh, scores, 0.0), axis=0, keepdims=True))
        idx_rows.append(idx)
        cm = jnp.where(oh, neg, cm)
        sel = sel + jnp.where(oh, 1.0, 0.0)

    run = run_ref[:, 0:1]
    cum = _dot(sel.astype(BF16), tri_ref[...])
    before = run + cum - sel
    rk_rows = [jnp.sum(jnp.where(iota_e == idx, before, 0.0), axis=0, keepdims=True) for idx in idx_rows]
    new_run = run + jnp.sum(sel, axis=1, keepdims=True)
    run_ref[...] = jnp.broadcast_to(new_run, run_ref.shape)
    cnt_ref[...] = jnp.broadcast_to(new_run, cnt_ref.shape)

    ssum = s_rows[0]
    for r in range(1, TOP_K):
        ssum = ssum + s_rows[r]
    denom = ssum + 1e-20
    idx_ref[...] = jnp.concatenate(idx_rows, axis=0).astype(I32)
    wt_ref[...] = jnp.concatenate([sr / denom * ROUTED_SCALE for sr in s_rows], axis=0)
    rk_ref[...] = jnp.concatenate(rk_rows, axis=0).astype(I32)


def _ffn_pre(x1, g_ffn, w_router, router_bias):
    t = x1.shape[0]
    tm = TM_FFN
    wr_t = w_router.astype(F32).T
    wr_hi = wr_t.astype(BF16)
    wr_lo = (wr_t - wr_hi.astype(F32)).astype(BF16)
    tri = (np.arange(tm)[:, None] <= np.arange(tm)[None, :]).astype(np.float32)
    operands = (x1, g_ffn.reshape(1, D_MODEL), wr_hi, wr_lo,
                router_bias.astype(F32).reshape(N_EXPERTS, 1), jnp.asarray(tri, BF16))
    in_specs = [pl.BlockSpec((tm, D_MODEL), lambda i: (i, 0))]
    in_specs += [_const_spec(op.shape) for op in operands[1:]]
    row8 = pl.BlockSpec((TOP_K, tm), lambda i: (0, i))
    return pl.pallas_call(
        _ffn_pre_kernel,
        grid=(t // tm,),
        in_specs=in_specs,
        out_specs=[pl.BlockSpec((tm, HALF), lambda i: (i, 0)),
                   row8, row8, row8,
                   pl.BlockSpec((N_EXPERTS, 128), lambda i: (0, 0))],
        out_shape=[jax.ShapeDtypeStruct((t, HALF), I32),
                   jax.ShapeDtypeStruct((TOP_K, t), I32),
                   jax.ShapeDtypeStruct((TOP_K, t), F32),
                   jax.ShapeDtypeStruct((TOP_K, t), I32),
                   jax.ShapeDtypeStruct((N_EXPERTS, 128), F32)],
        scratch_shapes=[pltpu.VMEM((N_EXPERTS, 128), F32)],
        compiler_params=pltpu.CompilerParams(
            dimension_semantics=("arbitrary",), vmem_limit_bytes=VMEM_LIMIT),
        name="ffn_pre",
    )(*operands)


def _swiglu_packed(x_packed, wg, wu, wd):
    lo, hi = _unpack(x_packed)
    lo = lo.astype(BF16)
    hi = hi.astype(BF16)
    g = _dot(lo, wg[0:HALF, :]) + _dot(hi, wg[HALF:D_MODEL, :])
    u = _dot(lo, wu[0:HALF, :]) + _dot(hi, wu[HALF:D_MODEL, :])
    a = (g * _sigmoid(g) * u).astype(BF16)
    y = _dot(a, wd[...])
    return _pack(y[:, 0:HALF], y[:, HALF:D_MODEL])


def _shared_kernel(h2p_ref, wg_ref, wu_ref, wd_ref, ysh_ref):
    ysh_ref[...] = _swiglu_packed(h2p_ref[...], wg_ref, wu_ref, wd_ref)


def _shared(h2p, w_sh_gate, w_sh_up, w_sh_down):
    t = h2p.shape[0]
    tm = TM_FFN
    operands = (h2p, w_sh_gate.astype(BF16), w_sh_up.astype(BF16), w_sh_down.astype(BF16))
    in_specs = [pl.BlockSpec((tm, HALF), lambda i: (i, 0))]
    in_specs += [_const_spec(op.shape) for op in operands[1:]]
    return pl.pallas_call(
        _shared_kernel,
        grid=(t // tm,),
        in_specs=in_specs,
        out_specs=pl.BlockSpec((tm, HALF), lambda i: (i, 0)),
        out_shape=jax.ShapeDtypeStruct((t, HALF), I32),
        compiler_params=pltpu.CompilerParams(
            dimension_semantics=("arbitrary",), vmem_limit_bytes=VMEM_LIMIT),
        name="shared",
    )(*operands)


def _positions_kernel(pstart_ref, idx_ref, rk_ref, pos_ref):
    idx = idx_ref[...]
    acc = rk_ref[...]
    for e in range(N_EXPERTS):
        acc = acc + jnp.where(idx == e, pstart_ref[e], 0)
    pos_ref[...] = acc


def _positions(pstart, idx8, rk8):
    t = idx8.shape[1]
    tm = TM_POS
    spec = pl.BlockSpec((TOP_K, tm), lambda i, ps: (0, i))
    return pl.pallas_call(
        _positions_kernel,
        grid_spec=pltpu.PrefetchScalarGridSpec(num_scalar_prefetch=1, grid=(t // tm,),
                                               in_specs=[spec, spec], out_specs=spec),
        out_shape=jax.ShapeDtypeStruct((TOP_K, t), I32),
        name="positions",
    )(pstart, idx8, rk8)


def _sc_workers():
    info = plsc.get_sparse_core_info()
    return info.num_cores, info.num_cores * info.num_subcores


def _dispatch(pos, h2p, n_rows):
    t = h2p.shape[0]
    ch = SC_SCATTER_ROWS
    n_cores, n_workers = _sc_workers()
    n_ch = t // (n_workers * ch)
    pos3 = pos.reshape(TOP_K, t // ch, ch).transpose(1, 0, 2)

    def body(pos_hbm, h_hbm, xs_hbm, idx_v, rows_v, sem):
        wid = lax.axis_index("s") * n_cores + lax.axis_index("c")

        @pl.loop(0, n_ch)
        def _(c):
            chunk = wid * n_ch + c
            pltpu.sync_copy(h_hbm.at[pl.ds(chunk * ch, ch)], rows_v)
            pltpu.sync_copy(pos_hbm.at[chunk], idx_v)
            copies = [pltpu.async_copy(rows_v, xs_hbm.at[idx_v.at[k]], sem) for k in range(TOP_K)]
            for cp in copies:
                cp.wait()

    return pl.kernel(
        body,
        out_type=jax.ShapeDtypeStruct((n_rows, HALF), I32),
        mesh=plsc.VectorSubcoreMesh(core_axis_name="c", subcore_axis_name="s"),
        scratch_types=[pltpu.VMEM((TOP_K, ch), I32), pltpu.VMEM((ch, HALF), I32),
                       pltpu.SemaphoreType.DMA],
        name="dispatch",
    )(pos3, h2p)


def _gather_back(pos, ys):
    t = pos.shape[1]
    ch = SC_GATHER_ROWS
    n_cores, n_workers = _sc_workers()
    n_ch = t // (n_workers * ch)
    posg = pos.reshape(TOP_K, t // ch, ch).transpose(1, 0, 2).reshape(-1)

    def body(pos_hbm, ys_hbm, yt_hbm, idx_v, buf0, buf1, gsem, wsem0, wsem1):
        wid = lax.axis_index("s") * n_cores + lax.axis_index("c")
        bufs = (buf0, buf1)
        wsems = (wsem0, wsem1)

        @pl.loop(0, n_ch)
        def _(c):
            chunk = wid * n_ch + c
            pltpu.sync_copy(pos_hbm.at[pl.ds(chunk * (TOP_K * ch), TOP_K * ch)], idx_v)
            writes = [None, None]
            for k in range(TOP_K):
                b = k % 2
                if writes[b] is not None:
                    writes[b].wait()
                pltpu.async_copy(ys_hbm.at[idx_v.at[pl.ds(k * ch, ch)]], bufs[b], gsem).wait()
                writes[b] = pltpu.async_copy(bufs[b], yt_hbm.at[k, pl.ds(chunk * ch, ch)], wsems[b])
            writes[0].wait()
            writes[1].wait()

    return pl.kernel(
        body,
        out_type=jax.ShapeDtypeStruct((TOP_K, t, HALF), I32),
        mesh=plsc.VectorSubcoreMesh(core_axis_name="c", subcore_axis_name="s"),
        scratch_types=[pltpu.VMEM((TOP_K * ch,), I32), pltpu.VMEM((ch, HALF), I32),
                       pltpu.VMEM((ch, HALF), I32), pltpu.SemaphoreType.DMA,
                       pltpu.SemaphoreType.DMA, pltpu.SemaphoreType.DMA],
        name="gather_back",
    )(posg, ys)


def _expert_kernel(be_ref, nv_ref, src_ref, new_ref, xs_ref, wg_ref, wu_ref, wd_ref, after_ref, ys_ref,
                   wg_s, wu_s, wd_s):
    del after_ref
    b = pl.program_id(0)
    nv = nv_ref[b]

    @pl.when(new_ref[b] > 0)
    def _():
        wg_s[...] = wg_ref[0].astype(BF16)
        wu_s[...] = wu_ref[0].astype(BF16)
        wd_s[...] = wd_ref[0].astype(BF16)

    @pl.when(nv > 0)
    def _():
        rows = lax.broadcasted_iota(I32, xs_ref.shape, 0)
        x = jnp.where(rows < nv, xs_ref[...], 0)
        ys_ref[...] = _swiglu_packed(x, wg_s, wu_s, wd_s)


def _experts(block_e, n_valid, block_src, block_new, xs, w_gate, w_up, w_down, after):
    n_blocks = block_e.shape[0]
    grid_spec = pltpu.PrefetchScalarGridSpec(
        num_scalar_prefetch=4,
        grid=(n_blocks,),
        in_specs=[pl.BlockSpec((MOE_BLOCK, HALF), lambda b, be, nv, src, new: (src[b], 0)),
                  pl.BlockSpec((1, D_MODEL, EXPERT_DIM), lambda b, be, nv, src, new: (be[b], 0, 0)),
                  pl.BlockSpec((1, D_MODEL, EXPERT_DIM), lambda b, be, nv, src, new: (be[b], 0, 0)),
                  pl.BlockSpec((1, EXPERT_DIM, D_MODEL), lambda b, be, nv, src, new: (be[b], 0, 0)),
                  pl.BlockSpec(memory_space=pl.ANY)],
        out_specs=pl.BlockSpec((MOE_BLOCK, HALF), lambda b, be, nv, src, new: (src[b], 0)),
        scratch_shapes=[pltpu.VMEM((D_MODEL, EXPERT_DIM), BF16), pltpu.VMEM((D_MODEL, EXPERT_DIM), BF16),
                        pltpu.VMEM((EXPERT_DIM, D_MODEL), BF16)],
    )
    return pl.pallas_call(
        _expert_kernel,
        grid_spec=grid_spec,
        out_shape=jax.ShapeDtypeStruct(xs.shape, I32),
        compiler_params=pltpu.CompilerParams(
            dimension_semantics=("arbitrary",), vmem_limit_bytes=VMEM_LIMIT),
        name="experts",
    )(block_e, n_valid, block_src, block_new, xs, w_gate, w_up, w_down, after)


def _combine_kernel(x1_ref, ysh_ref, yt_ref, wt_ref, p_ref, gple_ref, wpg_ref, wpp_ref, gpost_ref, *rest):
    out_ref = rest[-1]
    proj = _rms(_dot(p_ref[...].astype(BF16), wpp_ref[...]), gpost_ref[...])
    x1 = x1_ref[...]
    sh_lo, sh_hi = _unpack(ysh_ref[...])
    acc_lo = x1[:, 0:HALF] + sh_lo
    acc_hi = x1[:, HALF:D_MODEL] + sh_hi
    wt = wt_ref[...]
    for k in range(TOP_K):
        lo, hi = _unpack(yt_ref[k])
        wk = wt[:, k:k + 1]
        acc_lo = acc_lo + wk * lo
        acc_hi = acc_hi + wk * hi
    x2 = jnp.concatenate([acc_lo, acc_hi], axis=1)
    gate = _sigmoid(_dot(_rms(x2, gple_ref[...]).astype(BF16), wpg_ref[...]))
    out_ref[...] = x2 + gate * proj


def _combine(x1, ysh, yt, wt_t, p2d, g_ple, w_ple_gate, w_ple_proj, g_ple_post, chunk, prev_out):
    t = x1.shape[0]
    tm = TM_COMBINE
    steps = yt.shape[1] // tm
    off = chunk * steps
    operands = [x1, ysh, yt, wt_t, p2d, g_ple.reshape(1, D_MODEL), w_ple_gate.astype(BF16),
                w_ple_proj.astype(BF16), g_ple_post.reshape(1, D_MODEL)]
    in_specs = [pl.BlockSpec((tm, D_MODEL), lambda i: (off + i, 0)),
                pl.BlockSpec((tm, HALF), lambda i: (off + i, 0)),
                pl.BlockSpec((TOP_K, tm, HALF), lambda i: (0, i, 0)),
                pl.BlockSpec((tm, TOP_K), lambda i: (off + i, 0)),
                pl.BlockSpec((tm, PLE_DIM), lambda i: (off + i, 0)),
                _const_spec((1, D_MODEL)),
                _const_spec((D_MODEL, D_MODEL)),
                _const_spec((PLE_DIM, D_MODEL)),
                _const_spec((1, D_MODEL))]
    aliases = {}
    if prev_out is not None:
        aliases = {len(operands): 0}
        operands.append(prev_out)
        in_specs.append(pl.BlockSpec(memory_space=pl.ANY))
    return pl.pallas_call(
        _combine_kernel,
        grid=(steps,),
        in_specs=in_specs,
        out_specs=pl.BlockSpec((tm, D_MODEL), lambda i: (off + i, 0)),
        out_shape=jax.ShapeDtypeStruct((t, D_MODEL), F32),
        input_output_aliases=aliases,
        compiler_params=pltpu.CompilerParams(
            dimension_semantics=("arbitrary",), vmem_limit_bytes=VMEM_LIMIT),
        name="combine",
    )(*operands)


def _layer(x2d, p2d, batch, seq, g_mix, w_in, w_pool_mix, pool_scale, w_branch_a, q_gain, k_gain,
           attn_sinks, w_branch_b, w_gate, b_gate, w_out, g_ffn, w_router, router_bias, w_exp_gate,
           w_exp_up, w_exp_down, w_sh_gate, w_sh_up, w_sh_down, g_ple, w_ple_gate, w_ple_proj,
           g_ple_post):
    t = x2d.shape[0]
    x1 = _mixer(x2d, batch, seq, g_mix, w_in, w_pool_mix, pool_scale, w_branch_a, q_gain, k_gain,
                attn_sinks, w_branch_b, w_gate, b_gate, w_out)
    h2p, idx8, wt8, rk8, cnt = _ffn_pre(x1, g_ffn, w_router, router_bias)

    counts = cnt[:, 0].astype(I32)
    padded = (counts + MOE_BLOCK - 1) // MOE_BLOCK * MOE_BLOCK
    pend = jnp.cumsum(padded)
    pstart = pend - padded
    pos = _positions(pstart.astype(I32), idx8, rk8)
    n_blocks = -(-(t * TOP_K) // MOE_BLOCK) + N_EXPERTS
    n_used = pend[-1] // MOE_BLOCK
    block_src = jnp.minimum(jnp.arange(n_blocks, dtype=I32), n_used - 1)
    blk_row = block_src * MOE_BLOCK
    block_e = jnp.minimum(jnp.sum(pend[None, :] <= blk_row[:, None], axis=1), N_EXPERTS - 1).astype(I32)
    is_e = block_e[:, None] == jnp.arange(N_EXPERTS, dtype=I32)[None, :]
    blk_count = jnp.sum(jnp.where(is_e, counts[None, :], 0), axis=1)
    blk_start = jnp.sum(jnp.where(is_e, pstart[None, :], 0), axis=1)
    n_valid = jnp.clip(blk_count - (blk_row - blk_start), 0, MOE_BLOCK)
    n_valid = jnp.where(jnp.arange(n_blocks) < n_used, n_valid, 0).astype(I32)

    xs = _dispatch(pos, h2p, n_blocks * MOE_BLOCK)
    ysh = _shared(h2p, w_sh_gate, w_sh_up, w_sh_down)
    block_new = jnp.concatenate([jnp.ones((1,), I32), (block_e[1:] != block_e[:-1]).astype(I32)])
    block_new = jnp.where(n_valid > 0, block_new, 0)
    ys = _experts(block_e, n_valid, block_src, block_new, xs, w_exp_gate, w_exp_up, w_exp_down, ysh)
    wt_t = wt8.T
    tc = t // COMBINE_CHUNKS
    out = None
    for c in range(COMBINE_CHUNKS):
        yt = _gather_back(pos[:, c * tc:(c + 1) * tc], ys)
        out = _combine(x1, ysh, yt, wt_t, p2d, g_ple, w_ple_gate, w_ple_proj, g_ple_post, c, out)
    return out


def kernel(x, p, g_mix, w_in, w_pool_mix, pool_scale, w_branch_a, q_gain, k_gain, attn_sinks,
           w_branch_b, w_gate, b_gate, w_out, g_ffn, w_router, router_bias, w_exp_gate, w_exp_up,
           w_exp_down, w_sh_gate, w_sh_up, w_sh_down, g_ple, w_ple_gate, w_ple_proj, g_ple_post):
    batch, seq, d = x.shape
    depth = p.shape[0]
    x2d = x.reshape(batch * seq, d)
    for i in range(depth):
        x2d = _layer(x2d, p[i].reshape(batch * seq, PLE_DIM), batch, seq, g_mix[i], w_in[i],
                     w_pool_mix[i], pool_scale[i], w_branch_a[i], q_gain[i], k_gain[i], attn_sinks[i],
                     w_branch_b[i], w_gate[i], b_gate[i], w_out[i], g_ffn[i], w_router[i],
                     router_bias[i], w_exp_gate[i], w_exp_up[i], w_exp_down[i], w_sh_gate[i],
                     w_sh_up[i], w_sh_down[i], g_ple[i], w_ple_gate[i], w_ple_proj[i], g_ple_post[i])
    return x2d.reshape(batch, seq, d)
```

```python
import functools

import numpy as np
import jax
import jax.numpy as jnp
from jax import lax
from jax.experimental import pallas as pl
from jax.experimental.pallas import tpu as pltpu
from jax.experimental.pallas import tpu_sc as plsc

F32 = jnp.float32
BF16 = jnp.bfloat16
I32 = jnp.int32

D_MODEL = 1024
PLE_DIM = 256
POOL_WINDOWS = (2, 4, 8, 16)
POOL_GROUP_DIM = 128
POOL_DIM = 512
N_HEADS = 8
N_KV_HEADS = 2
Q_GROUP = 4
HEAD_DIM = 64
HEAD_PAD = 128
Q_DIM = 512
KV_DIM = 128
ATTN_BLOCK = 128
N_EXPERTS = 64
N_EXPERT_GROUPS = 8
GROUP_SIZE = 8
TOPK_GROUPS = 4
TOP_K = 8
EXPERT_DIM = 256
ROUTED_SCALE = 2.5
MOE_BLOCK = 256
EPS = 1e-6
HALF = D_MODEL // 2
MASKED = -1e30

COL_Q = POOL_DIM
COL_K = COL_Q + N_HEADS * HEAD_PAD
COL_V = COL_K + N_KV_HEADS * HEAD_PAD
IN_PAD = COL_V + N_KV_HEADS * HEAD_PAD
HEAD_ORDER = (0, 2, 1, 3)

TM_MIX = 512
TM_FFN = 512
TM_COMBINE = 512
TM_POS = 4096
COMBINE_CHUNKS = 4
SC_SCATTER_ROWS = 128
SC_GATHER_ROWS = 64
POOL_TAIL = 8
VMEM_LIMIT = 56 * 1024 * 1024

_NT = (((1,), (1,)), ((), ()))


def _dot(a, b):
    return jnp.dot(a, b, preferred_element_type=F32)


def _rms(x, g):
    ms = jnp.mean(x * x, axis=-1, keepdims=True)
    return x * lax.rsqrt(ms + EPS) * g


def _sigmoid(x):
    return 0.5 * jnp.tanh(0.5 * x) + 0.5


def _pack(lo, hi):
    lo_bits = lax.bitcast_convert_type(lo.astype(BF16).astype(F32), I32)
    hi_bits = lax.bitcast_convert_type(hi.astype(BF16).astype(F32), I32)
    return (hi_bits & jnp.int32(-65536)) | lax.shift_right_logical(lo_bits, jnp.int32(16))


def _unpack(w):
    lo = lax.bitcast_convert_type(lax.shift_left(w, jnp.int32(16)), F32)
    hi = lax.bitcast_convert_type(w & jnp.int32(-65536), F32)
    return lo, hi


def _mixer_kernel(x_ref, gmix_ref, win_ref, qg_ref, kg_ref, bias_ref, sink_ref, ones_ref, wmix_ref,
                  pscale_ref, wa_ref, wb_ref, wg_ref, bg_ref, wout_ref,
                  x1_ref, kbuf, vt_buf, pbuf, zbt_buf):
    s = pl.program_id(1)
    tm = x_ref.shape[0]

    @pl.when(s == 0)
    def _():
        kbuf[0:ATTN_BLOCK, :] = jnp.zeros((ATTN_BLOCK, kbuf.shape[1]), BF16)
        vt_buf[:, 0:ATTN_BLOCK] = jnp.zeros((vt_buf.shape[0], ATTN_BLOCK), BF16)
        pbuf[:, 0:POOL_TAIL, :] = jnp.zeros((pbuf.shape[0], POOL_TAIL, POOL_DIM), F32)

    x = x_ref[...]
    hb = _rms(x, gmix_ref[...]).astype(BF16)
    u = _dot(hb, win_ref[...])

    up = u[:, 0:POOL_DIM]
    t0 = POOL_TAIL
    pbuf[0, t0:t0 + tm, :] = up
    s2 = up + pbuf[0, t0 - 1:t0 - 1 + tm, :]
    pbuf[1, t0:t0 + tm, 128:512] = s2[:, 128:512]
    s4 = s2[:, 128:512] + pbuf[1, t0 - 2:t0 - 2 + tm, 128:512]
    pbuf[2, t0:t0 + tm, 256:512] = s4[:, 128:384]
    s8 = s4[:, 128:384] + pbuf[2, t0 - 4:t0 - 4 + tm, 256:512]
    pbuf[3, t0:t0 + tm, 384:512] = s8[:, 128:256]
    s16 = s8[:, 128:256] + pbuf[3, t0 - 8:t0 - 8 + tm, 384:512]
    for lvl in range(4):
        pbuf[lvl, 0:POOL_TAIL, 128 * lvl:512] = pbuf[lvl, tm:tm + POOL_TAIL, 128 * lvl:512]
    wsums = (s2[:, 0:128], s4[:, 0:128], s8[:, 0:128], s16)
    tpos = (s * tm).astype(F32) + lax.broadcasted_iota(I32, (tm, 1), 0).astype(F32)
    za_parts = []
    for g, w in enumerate(POOL_WINDOWS):
        inv_count = 1.0 / jnp.minimum(tpos + 1.0, float(w))
        pooled = wsums[g] * inv_count - up[:, 128 * g:128 * (g + 1)]
        mixed = _dot(pooled.astype(BF16), wmix_ref[g]) * pscale_ref[:, 128 * g:128 * (g + 1)]
        za_parts.append(mixed.astype(BF16))
    za = jnp.concatenate(za_parts, axis=1)

    ones2 = ones_ref[...]
    q = u[:, COL_Q:COL_K]
    q2 = (q * q).astype(BF16)
    qss = jnp.concatenate([_dot(q2[:, 256 * c:256 * (c + 1)], ones2) for c in range(N_HEADS // 2)], axis=1)
    qn = (q * lax.rsqrt(qss * (1.0 / HEAD_DIM) + EPS) * qg_ref[...]).astype(BF16)
    k = u[:, COL_K:COL_V]
    kss = _dot((k * k).astype(BF16), ones2)
    kbuf[ATTN_BLOCK:ATTN_BLOCK + tm, :] = (k * lax.rsqrt(kss * (1.0 / HEAD_DIM) + EPS) * kg_ref[...]).astype(BF16)
    vt_buf[:, ATTN_BLOCK:ATTN_BLOCK + tm] = u[:, COL_V:IN_PAD].T.astype(BF16)

    nq = Q_GROUP * ATTN_BLOCK
    key_j = lax.broadcasted_iota(I32, (ATTN_BLOCK, nq), 0)
    qry_i = lax.broadcasted_iota(I32, (ATTN_BLOCK, nq), 1) & (ATTN_BLOCK - 1)
    from_prev = key_j > qry_i
    first = jnp.where(s == 0, 1, 0)
    for n in range(tm // ATTN_BLOCK):
        r0 = ATTN_BLOCK * n
        for kv in range(N_KV_HEADS):
            c0 = HEAD_PAD * kv
            qs = jnp.concatenate(
                [qn[r0:r0 + ATTN_BLOCK, HEAD_PAD * (kv * Q_GROUP + g):HEAD_PAD * (kv * Q_GROUP + g + 1)]
                 for g in HEAD_ORDER], axis=0)
            kk = kbuf[r0:r0 + 2 * ATTN_BLOCK, c0:c0 + HEAD_PAD]
            st = lax.dot_general(kk, qs, _NT, preferred_element_type=F32)
            sc = jnp.where(from_prev, st[0:ATTN_BLOCK], st[ATTN_BLOCK:2 * ATTN_BLOCK])
            sc = sc + (bias_ref[first, kv] if n == 0 else bias_ref[0, kv])
            sink = sink_ref[kv]
            m = jnp.maximum(jnp.max(sc, axis=0, keepdims=True), sink)
            e = jnp.exp(sc - m)
            inv_den = 1.0 / (jnp.sum(e, axis=0, keepdims=True) + jnp.exp(sink - m))
            pt = jnp.concatenate([jnp.where(from_prev, e, 0.0), jnp.where(from_prev, 0.0, e)],
                                 axis=0).astype(BF16)
            v_even = vt_buf[c0:c0 + HEAD_PAD, r0:r0 + 2 * ATTN_BLOCK]
            v_odd = jnp.concatenate([v_even[HEAD_DIM:HEAD_PAD], v_even[0:HEAD_DIM]], axis=0)
            half = 2 * ATTN_BLOCK
            o = (_dot(v_even, pt[:, 0:half]) * inv_den[:, 0:half]
                 + _dot(v_odd, pt[:, half:2 * half]) * inv_den[:, half:2 * half])
            ch = kv * Q_GROUP * HEAD_DIM
            zbt_buf[ch:ch + HEAD_PAD, r0:r0 + ATTN_BLOCK] = o[:, 0:ATTN_BLOCK]
            zbt_buf[ch + HEAD_PAD:ch + 2 * HEAD_PAD, r0:r0 + ATTN_BLOCK] = o[:, ATTN_BLOCK:half]
    kbuf[0:ATTN_BLOCK, :] = kbuf[tm:tm + ATTN_BLOCK, :]
    vt_buf[:, 0:ATTN_BLOCK] = vt_buf[:, tm:tm + ATTN_BLOCK]

    y_a = _dot(za, wa_ref[...])
    y_b = _dot(zbt_buf[...].T.astype(BF16), wb_ref[...])
    g_a = _sigmoid(_dot(hb, wg_ref[:, 0:D_MODEL]) + bg_ref[:, 0:D_MODEL])
    merged = g_a * y_a
    g_b = _sigmoid(_dot(hb, wg_ref[:, D_MODEL:2 * D_MODEL]) + bg_ref[:, D_MODEL:2 * D_MODEL])
    merged = (merged + g_b * y_b).astype(BF16)
    x1_ref[...] = x + _dot(merged, wout_ref[...])


def _attn_tables(attn_sinks):
    slopes = 2.0 ** (-8.0 * (np.arange(N_HEADS) + 1) / N_HEADS)
    j = np.arange(ATTN_BLOCK)[:, None]
    i = np.arange(ATTN_BLOCK)[None, :]
    from_prev = j > i
    dist = np.where(from_prev, ATTN_BLOCK + i - j, i - j)
    bias = np.empty((2, N_KV_HEADS, ATTN_BLOCK, Q_GROUP * ATTN_BLOCK), np.float32)
    for first in range(2):
        ok = ~from_prev if first else np.ones_like(from_prev)
        for kv in range(N_KV_HEADS):
            for slot, g in enumerate(HEAD_ORDER):
                sl = np.float32(slopes[kv * Q_GROUP + g])
                val = -(sl * dist.astype(np.float32))
                bias[first, kv, :, slot * ATTN_BLOCK:(slot + 1) * ATTN_BLOCK] = np.where(ok, val, MASKED)
    sink = attn_sinks.astype(F32).reshape(N_KV_HEADS, Q_GROUP)[:, np.asarray(HEAD_ORDER)]
    sink = jnp.repeat(sink, ATTN_BLOCK, axis=1)
    return jnp.asarray(bias), sink.reshape(N_KV_HEADS, 1, Q_GROUP * ATTN_BLOCK)


def _pad_heads(w, n_heads):
    k = w.shape[0]
    w = w.reshape(k, n_heads, HEAD_DIM)
    w = jnp.pad(w, ((0, 0), (0, 0), (0, HEAD_PAD - HEAD_DIM)))
    return w.reshape(k, n_heads * HEAD_PAD)


def _const_spec(shape):
    nd = len(shape)
    return pl.BlockSpec(shape, lambda *_: (0,) * nd)


def _mixer(x2d, batch, seq, g_mix, w_in, w_pool_mix, pool_scale, w_branch_a, q_gain, k_gain,
           attn_sinks, w_branch_b, w_gate, b_gate, w_out):
    t = x2d.shape[0]
    tm = TM_MIX
    ns = seq // tm
    w_q = w_in[:, POOL_DIM:POOL_DIM + Q_DIM]
    w_k = w_in[:, POOL_DIM + Q_DIM:POOL_DIM + Q_DIM + KV_DIM]
    w_v = w_in[:, POOL_DIM + Q_DIM + KV_DIM:]
    win_p = jnp.concatenate([w_in[:, :POOL_DIM], _pad_heads(w_q, N_HEADS), _pad_heads(w_k, N_KV_HEADS),
                             _pad_heads(w_v, N_KV_HEADS)], axis=1).astype(BF16)
    qg = jnp.tile(jnp.pad(q_gain * (HEAD_DIM ** -0.5), (0, HEAD_PAD - HEAD_DIM)), N_HEADS).reshape(1, -1)
    kg = jnp.tile(jnp.pad(k_gain, (0, HEAD_PAD - HEAD_DIM)), N_KV_HEADS).reshape(1, -1)
    bias, sink = _attn_tables(attn_sinks)
    ones2 = jnp.asarray(np.kron(np.eye(2), np.ones((HEAD_PAD, HEAD_PAD))), BF16)
    operands = (x2d, g_mix.reshape(1, D_MODEL), win_p, qg, kg, bias, sink, ones2, w_pool_mix.astype(BF16),
                pool_scale.reshape(1, POOL_DIM), w_branch_a.astype(BF16), w_branch_b.astype(BF16),
                w_gate.astype(BF16), b_gate.reshape(1, 2 * D_MODEL), w_out.astype(BF16))
    in_specs = [pl.BlockSpec((tm, D_MODEL), lambda b, s: (b * ns + s, 0))]
    in_specs += [_const_spec(op.shape) for op in operands[1:]]
    return pl.pallas_call(
        _mixer_kernel,
        grid=(batch, ns),
        in_specs=in_specs,
        out_specs=pl.BlockSpec((tm, D_MODEL), lambda b, s: (b * ns + s, 0)),
        out_shape=jax.ShapeDtypeStruct((t, D_MODEL), F32),
        scratch_shapes=[
            pltpu.VMEM((ATTN_BLOCK + tm, N_KV_HEADS * HEAD_PAD), BF16),
            pltpu.VMEM((N_KV_HEADS * HEAD_PAD, ATTN_BLOCK + tm), BF16),
            pltpu.VMEM((4, POOL_TAIL + tm, POOL_DIM), F32),
            pltpu.VMEM((Q_DIM, tm), F32),
        ],
        compiler_params=pltpu.CompilerParams(
            dimension_semantics=("arbitrary", "arbitrary"), vmem_limit_bytes=VMEM_LIMIT),
        name="mixer",
    )(*operands)


def _ffn_pre_kernel(x1_ref, gffn_ref, wrh_ref, wrl_ref, rb_ref, tri_ref,
                    h2p_ref, idx_ref, wt_ref, rk_ref, cnt_ref, run_ref):
    i = pl.program_id(0)
    tm = x1_ref.shape[0]

    @pl.when(i == 0)
    def _():
        run_ref[...] = jnp.zeros_like(run_ref)

    h2 = _rms(x1_ref[...], gffn_ref[...])
    hb = h2.astype(BF16)
    h2p_ref[...] = _pack(h2[:, 0:HALF], h2[:, HALF:D_MODEL])

    lo = (h2 - hb.astype(F32)).astype(BF16)
    wrh = wrh_ref[...]
    logits = (lax.dot_general(wrh, hb, _NT, preferred_element_type=F32)
              + lax.dot_general(wrl_ref[...], hb, _NT, preferred_element_type=F32)
              + lax.dot_general(wrh, lo, _NT, preferred_element_type=F32))
    scores = _sigmoid(logits)
    choice = scores + rb_ref[...]

    neg = -jnp.inf
    sub8 = lax.broadcasted_iota(I32, (GROUP_SIZE, tm), 0).astype(F32)
    grp_rows = []
    for gi in range(N_EXPERT_GROUPS):
        cg = choice[GROUP_SIZE * gi:GROUP_SIZE * (gi + 1)]
        m1 = jnp.max(cg, axis=0, keepdims=True)
        first = jnp.min(jnp.where(cg == m1, sub8, float(GROUP_SIZE)), axis=0, keepdims=True)
        m2 = jnp.max(jnp.where(sub8 == first, neg, cg), axis=0, keepdims=True)
        grp_rows.append(m1 + m2)
    gs = jnp.concatenate(grp_rows, axis=0)
    beaten = jnp.zeros((N_EXPERT_GROUPS, tm), F32)
    for gi in range(N_EXPERT_GROUPS):
        row = grp_rows[gi]
        wins = (row > gs) | ((row == gs) & (sub8 > float(gi)))
        beaten = beaten + jnp.where(wins, 1.0, 0.0)
    gsel = beaten < float(TOPK_GROUPS)
    cm = jnp.concatenate(
        [jnp.where(gsel[gi:gi + 1], choice[GROUP_SIZE * gi:GROUP_SIZE * (gi + 1)], neg)
         for gi in range(N_EXPERT_GROUPS)], axis=0)

    iota_e = lax.broadcasted_iota(I32, (N_EXPERTS, tm), 0).astype(F32)
    idx_rows, s_rows = [], []
    sel = jnp.zeros((N_EXPERTS, tm), F32)
    for _ in range(TOP_K):
        m = jnp.max(cm, axis=0, keepdims=True)
        idx = jnp.min(jnp.where(cm == m, iota_e, float(N_EXPERTS)), axis=0, keepdims=True)
        oh = iota_e == idx
        s_rows.append(jnp.sum(jnp.where(oh, scores, 0.0), axis=0, keepdims=True))
        idx_rows.append(idx)
        cm = jnp.where(oh, neg, cm)
        sel = sel + jnp.where(oh, 1.0, 0.0)

    run = run_ref[:, 0:1]
    cum = _dot(sel.astype(BF16), tri_ref[...])
    before = run + cum - sel
    rk_rows = [jnp.sum(jnp.where(iota_e == idx, before, 0.0), axis=0, keepdims=True) for idx in idx_rows]
    new_run = run + jnp.sum(sel, axis=1, keepdims=True)
    run_ref[...] = jnp.broadcast_to(new_run, run_ref.shape)
    cnt_ref[...] = jnp.broadcast_to(new_run, cnt_ref.shape)

    ssum = s_rows[0]
    for r in range(1, TOP_K):
        ssum = ssum + s_rows[r]
    denom = ssum + 1e-20
    idx_ref[...] = jnp.concatenate(idx_rows, axis=0).astype(I32)
    wt_ref[...] = jnp.concatenate([sr / denom * ROUTED_SCALE for sr in s_rows], axis=0)
    rk_ref[...] = jnp.concatenate(rk_rows, axis=0).astype(I32)


def _ffn_pre(x1, g_ffn, w_router, router_bias):
    t = x1.shape[0]
    tm = TM_FFN
    wr_t = w_router.astype(F32).T
    wr_hi = wr_t.astype(BF16)
    wr_lo = (wr_t - wr_hi.astype(F32)).astype(BF16)
    tri = (np.arange(tm)[:, None] <= np.arange(tm)[None, :]).astype(np.float32)
    operands = (x1, g_ffn.reshape(1, D_MODEL), wr_hi, wr_lo,
                router_bias.astype(F32).reshape(N_EXPERTS, 1), jnp.asarray(tri, BF16))
    in_specs = [pl.BlockSpec((tm, D_MODEL), lambda i: (i, 0))]
    in_specs += [_const_spec(op.shape) for op in operands[1:]]
    row8 = pl.BlockSpec((TOP_K, tm), lambda i: (0, i))
    return pl.pallas_call(
        _ffn_pre_kernel,
        grid=(t // tm,),
        in_specs=in_specs,
        out_specs=[pl.BlockSpec((tm, HALF), lambda i: (i, 0)),
                   row8, row8, row8,
                   pl.BlockSpec((N_EXPERTS, 128), lambda i: (0, 0))],
        out_shape=[jax.ShapeDtypeStruct((t, HALF), I32),
                   jax.ShapeDtypeStruct((TOP_K, t), I32),
                   jax.ShapeDtypeStruct((TOP_K, t), F32),
                   jax.ShapeDtypeStruct((TOP_K, t), I32),
                   jax.ShapeDtypeStruct((N_EXPERTS, 128), F32)],
        scratch_shapes=[pltpu.VMEM((N_EXPERTS, 128), F32)],
        compiler_params=pltpu.CompilerParams(
            dimension_semantics=("arbitrary",), vmem_limit_bytes=VMEM_LIMIT),
        name="ffn_pre",
    )(*operands)


def _swiglu_packed(x_packed, wg, wu, wd):
    lo, hi = _unpack(x_packed)
    lo = lo.astype(BF16)
    hi = hi.astype(BF16)
    g = _dot(lo, wg[0:HALF, :]) + _dot(hi, wg[HALF:D_MODEL, :])
    u = _dot(lo, wu[0:HALF, :]) + _dot(hi, wu[HALF:D_MODEL, :])
    a = (g * _sigmoid(g) * u).astype(BF16)
    y = _dot(a, wd[...])
    return _pack(y[:, 0:HALF], y[:, HALF:D_MODEL])


def _shared_kernel(h2p_ref, wg_ref, wu_ref, wd_ref, ysh_ref):
    ysh_ref[...] = _swiglu_packed(h2p_ref[...], wg_ref, wu_ref, wd_ref)


def _shared(h2p, w_sh_gate, w_sh_up, w_sh_down):
    t = h2p.shape[0]
    tm = TM_FFN
    operands = (h2p, w_sh_gate.astype(BF16), w_sh_up.astype(BF16), w_sh_down.astype(BF16))
    in_specs = [pl.BlockSpec((tm, HALF), lambda i: (i, 0))]
    in_specs += [_const_spec(op.shape) for op in operands[1:]]
    return pl.pallas_call(
        _shared_kernel,
        grid=(t // tm,),
        in_specs=in_specs,
        out_specs=pl.BlockSpec((tm, HALF), lambda i: (i, 0)),
        out_shape=jax.ShapeDtypeStruct((t, HALF), I32),
        compiler_params=pltpu.CompilerParams(
            dimension_semantics=("arbitrary",), vmem_limit_bytes=VMEM_LIMIT),
        name="shared",
    )(*operands)


def _positions_kernel(pstart_ref, idx_ref, rk_ref, pos_ref):
    idx = idx_ref[...]
    acc = rk_ref[...]
    for e in range(N_EXPERTS):
        acc = acc + jnp.where(idx == e, pstart_ref[e], 0)
    pos_ref[...] = acc


def _positions(pstart, idx8, rk8):
    t = idx8.shape[1]
    tm = TM_POS
    spec = pl.BlockSpec((TOP_K, tm), lambda i, ps: (0, i))
    return pl.pallas_call(
        _positions_kernel,
        grid_spec=pltpu.PrefetchScalarGridSpec(num_scalar_prefetch=1, grid=(t // tm,),
                                               in_specs=[spec, spec], out_specs=spec),
        out_shape=jax.ShapeDtypeStruct((TOP_K, t), I32),
        name="positions",
    )(pstart, idx8, rk8)


def _sc_workers():
    info = plsc.get_sparse_core_info()
    return info.num_cores, info.num_cores * info.num_subcores


def _dispatch(pos, h2p, n_rows):
    t = h2p.shape[0]
    ch = SC_SCATTER_ROWS
    n_cores, n_workers = _sc_workers()
    n_ch = t // (n_workers * ch)
    pos3 = pos.reshape(TOP_K, t // ch, ch).transpose(1, 0, 2)

    def body(pos_hbm, h_hbm, xs_hbm, idx_v, rows_v, sem):
        wid = lax.axis_index("s") * n_cores + lax.axis_index("c")

        @pl.loop(0, n_ch)
        def _(c):
            chunk = wid * n_ch + c
            pltpu.sync_copy(h_hbm.at[pl.ds(chunk * ch, ch)], rows_v)
            pltpu.sync_copy(pos_hbm.at[chunk], idx_v)
            copies = [pltpu.async_copy(rows_v, xs_hbm.at[idx_v.at[k]], sem) for k in range(TOP_K)]
            for cp in copies:
                cp.wait()

    return pl.kernel(
        body,
        out_type=jax.ShapeDtypeStruct((n_rows, HALF), I32),
        mesh=plsc.VectorSubcoreMesh(core_axis_name="c", subcore_axis_name="s"),
        scratch_types=[pltpu.VMEM((TOP_K, ch), I32), pltpu.VMEM((ch, HALF), I32),
                       pltpu.SemaphoreType.DMA],
        name="dispatch",
    )(pos3, h2p)


def _gather_back(pos, ys):
    t = pos.shape[1]
    ch = SC_GATHER_ROWS
    n_cores, n_workers = _sc_workers()
    n_ch = t // (n_workers * ch)
    posg = pos.reshape(TOP_K, t // ch, ch).transpose(1, 0, 2).reshape(-1)

    def body(pos_hbm, ys_hbm, yt_hbm, idx_v, buf0, buf1, gsem, wsem0, wsem1):
        wid = lax.axis_index("s") * n_cores + lax.axis_index("c")
        bufs = (buf0, buf1)
        wsems = (wsem0, wsem1)

        @pl.loop(0, n_ch)
        def _(c):
            chunk = wid * n_ch + c
            pltpu.sync_copy(pos_hbm.at[pl.ds(chunk * (TOP_K * ch), TOP_K * ch)], idx_v)
            writes = [None, None]
            for k in range(TOP_K):
                b = k % 2
                if writes[b] is not None:
                    writes[b].wait()
                pltpu.async_copy(ys_hbm.at[idx_v.at[pl.ds(k * ch, ch)]], bufs[b], gsem).wait()
                writes[b] = pltpu.async_copy(bufs[b], yt_hbm.at[k, pl.ds(chunk * ch, ch)], wsems[b])
            writes[0].wait()
            writes[1].wait()

    return pl.kernel(
        body,
        out_type=jax.ShapeDtypeStruct((TOP_K, t, HALF), I32),
        mesh=plsc.VectorSubcoreMesh(core_axis_name="c", subcore_axis_name="s"),
        scratch_types=[pltpu.VMEM((TOP_K * ch,), I32), pltpu.VMEM((ch, HALF), I32),
                       pltpu.VMEM((ch, HALF), I32), pltpu.SemaphoreType.DMA,
                       pltpu.SemaphoreType.DMA, pltpu.SemaphoreType.DMA],
        name="gather_back",
    )(posg, ys)


def _expert_kernel(nblk_ref, blk0_ref, cnt_ref, wg_ref, wu_ref, wd_ref, xs_hbm, after_ref, ys_hbm,
                   wg_s, wu_s, wd_s, xbuf, ybuf, in_sem, out_sem):
    del after_ref
    e = pl.program_id(0)
    nb = nblk_ref[e]
    b0 = blk0_ref[e]
    cnt = cnt_ref[e]
    wg_s[...] = wg_ref[0].astype(BF16)
    wu_s[...] = wu_ref[0].astype(BF16)
    wd_s[...] = wd_ref[0].astype(BF16)

    def rows_of(j):
        return pl.ds(pl.multiple_of((b0 + j) * MOE_BLOCK, MOE_BLOCK), MOE_BLOCK)

    def in_copy(j, slot):
        return pltpu.make_async_copy(xs_hbm.at[rows_of(j)], xbuf.at[slot], in_sem.at[slot])

    def out_copy(j, slot):
        return pltpu.make_async_copy(ybuf.at[slot], ys_hbm.at[rows_of(j)], out_sem.at[slot])

    @pl.when(nb > 0)
    def _():
        in_copy(0, 0).start()

    def two_blocks(jj, carry):
        for slot in range(2):
            j = 2 * jj + slot

            @pl.when(j < nb)
            def _():
                in_copy(j, slot).wait()

                @pl.when(j + 1 < nb)
                def _():
                    in_copy(j + 1, 1 - slot).start()

                @pl.when(j >= 2)
                def _():
                    out_copy(j - 2, slot).wait()

                rows = lax.broadcasted_iota(I32, (MOE_BLOCK, HALF), 0)
                x = jnp.where(rows < cnt - j * MOE_BLOCK, xbuf[slot], 0)
                ybuf[slot] = _swiglu_packed(x, wg_s, wu_s, wd_s)
                out_copy(j, slot).start()
        return carry

    lax.fori_loop(0, (nb + 1) // 2, two_blocks, 0)

    @pl.when(nb >= 1)
    def _():
        out_copy(0, 0).wait()

    @pl.when(nb >= 2)
    def _():
        out_copy(0, 1).wait()


def _experts(n_blk, blk0, counts, n_rows, xs, w_gate, w_up, w_down, after):
    wspec_in = pl.BlockSpec((1, D_MODEL, EXPERT_DIM), lambda e, *_: (e, 0, 0))
    grid_spec = pltpu.PrefetchScalarGridSpec(
        num_scalar_prefetch=3,
        grid=(N_EXPERTS,),
        in_specs=[wspec_in, wspec_in,
                  pl.BlockSpec((1, EXPERT_DIM, D_MODEL), lambda e, *_: (e, 0, 0)),
                  pl.BlockSpec(memory_space=pl.ANY),
                  pl.BlockSpec(memory_space=pl.ANY)],
        out_specs=pl.BlockSpec(memory_space=pl.ANY),
        scratch_shapes=[pltpu.VMEM((D_MODEL, EXPERT_DIM), BF16), pltpu.VMEM((D_MODEL, EXPERT_DIM), BF16),
                        pltpu.VMEM((EXPERT_DIM, D_MODEL), BF16),
                        pltpu.VMEM((2, MOE_BLOCK, HALF), I32), pltpu.VMEM((2, MOE_BLOCK, HALF), I32),
                        pltpu.SemaphoreType.DMA((2,)), pltpu.SemaphoreType.DMA((2,))],
    )
    return pl.pallas_call(
        _expert_kernel,
        grid_spec=grid_spec,
        out_shape=jax.ShapeDtypeStruct((n_rows, HALF), I32),
        compiler_params=pltpu.CompilerParams(
            dimension_semantics=("arbitrary",), vmem_limit_bytes=VMEM_LIMIT),
        name="experts",
    )(n_blk, blk0, counts, w_gate, w_up, w_down, xs, after)


def _combine_kernel(x1_ref, ysh_ref, yt_ref, wt_ref, p_ref, gple_ref, wpg_ref, wpp_ref, gpost_ref, *rest):
    out_ref = rest[-1]
    proj = _rms(_dot(p_ref[...].astype(BF16), wpp_ref[...]), gpost_ref[...])
    x1 = x1_ref[...]
    sh_lo, sh_hi = _unpack(ysh_ref[...])
    acc_lo = x1[:, 0:HALF] + sh_lo
    acc_hi = x1[:, HALF:D_MODEL] + sh_hi
    wt = wt_ref[...]
    for k in range(TOP_K):
        lo, hi = _unpack(yt_ref[k])
        wk = wt[:, k:k + 1]
        acc_lo = acc_lo + wk * lo
        acc_hi = acc_hi + wk * hi
    x2 = jnp.concatenate([acc_lo, acc_hi], axis=1)
    gate = _sigmoid(_dot(_rms(x2, gple_ref[...]).astype(BF16), wpg_ref[...]))
    out_ref[...] = x2 + gate * proj


def _combine(x1, ysh, yt, wt_t, p2d, g_ple, w_ple_gate, w_ple_proj, g_ple_post, chunk, prev_out):
    t = x1.shape[0]
    tm = TM_COMBINE
    steps = yt.shape[1] // tm
    off = chunk * steps
    operands = [x1, ysh, yt, wt_t, p2d, g_ple.reshape(1, D_MODEL), w_ple_gate.astype(BF16),
                w_ple_proj.astype(BF16), g_ple_post.reshape(1, D_MODEL)]
    in_specs = [pl.BlockSpec((tm, D_MODEL), lambda i: (off + i, 0)),
                pl.BlockSpec((tm, HALF), lambda i: (off + i, 0)),
                pl.BlockSpec((TOP_K, tm, HALF), lambda i: (0, i, 0)),
                pl.BlockSpec((tm, TOP_K), lambda i: (off + i, 0)),
                pl.BlockSpec((tm, PLE_DIM), lambda i: (off + i, 0)),
                _const_spec((1, D_MODEL)),
                _const_spec((D_MODEL, D_MODEL)),
                _const_spec((PLE_DIM, D_MODEL)),
                _const_spec((1, D_MODEL))]
    aliases = {}
    if prev_out is not None:
        aliases = {len(operands): 0}
        operands.append(prev_out)
        in_specs.append(pl.BlockSpec(memory_space=pl.ANY))
    return pl.pallas_call(
        _combine_kernel,
        grid=(steps,),
        in_specs=in_specs,
        out_specs=pl.BlockSpec((tm, D_MODEL), lambda i: (off + i, 0)),
        out_shape=jax.ShapeDtypeStruct((t, D_MODEL), F32),
        input_output_aliases=aliases,
        compiler_params=pltpu.CompilerParams(
            dimension_semantics=("arbitrary",), vmem_limit_bytes=VMEM_LIMIT),
        name="combine",
    )(*operands)


def _layer(x2d, p2d, batch, seq, g_mix, w_in, w_pool_mix, pool_scale, w_branch_a, q_gain, k_gain,
           attn_sinks, w_branch_b, w_gate, b_gate, w_out, g_ffn, w_router, router_bias, w_exp_gate,
           w_exp_up, w_exp_down, w_sh_gate, w_sh_up, w_sh_down, g_ple, w_ple_gate, w_ple_proj,
           g_ple_post):
    t = x2d.shape[0]
    x1 = _mixer(x2d, batch, seq, g_mix, w_in, w_pool_mix, pool_scale, w_branch_a, q_gain, k_gain,
                attn_sinks, w_branch_b, w_gate, b_gate, w_out)
    h2p, idx8, wt8, rk8, cnt = _ffn_pre(x1, g_ffn, w_router, router_bias)

    counts = cnt[:, 0].astype(I32)
    padded = (counts + MOE_BLOCK - 1) // MOE_BLOCK * MOE_BLOCK
    pend = jnp.cumsum(padded)
    pstart = pend - padded
    pos = _positions(pstart.astype(I32), idx8, rk8)
    n_rows = (-(-(t * TOP_K) // MOE_BLOCK) + N_EXPERTS) * MOE_BLOCK

    xs = _dispatch(pos, h2p, n_rows)
    ysh = _shared(h2p, w_sh_gate, w_sh_up, w_sh_down)
    ys = _experts((padded // MOE_BLOCK).astype(I32), (pstart // MOE_BLOCK).astype(I32), counts, n_rows,
                  xs, w_exp_gate, w_exp_up, w_exp_down, ysh)
    wt_t = wt8.T
    tc = t // COMBINE_CHUNKS
    out = None
    for c in range(COMBINE_CHUNKS):
        yt = _gather_back(pos[:, c * tc:(c + 1) * tc], ys)
        out = _combine(x1, ysh, yt, wt_t, p2d, g_ple, w_ple_gate, w_ple_proj, g_ple_post, c, out)
    return out


def kernel(x, p, g_mix, w_in, w_pool_mix, pool_scale, w_branch_a, q_gain, k_gain, attn_sinks,
           w_branch_b, w_gate, b_gate, w_out, g_ffn, w_router, router_bias, w_exp_gate, w_exp_up,
           w_exp_down, w_sh_gate, w_sh_up, w_sh_down, g_ple, w_ple_gate, w_ple_proj, g_ple_post):
    batch, seq, d = x.shape
    depth = p.shape[0]
    x2d = x.reshape(batch * seq, d)
    for i in range(depth):
        x2d = _layer(x2d, p[i].reshape(batch * seq, PLE_DIM), batch, seq, g_mix[i], w_in[i],
                     w_pool_mix[i], pool_scale[i], w_branch_a[i], q_gain[i], k_gain[i], attn_sinks[i],
                     w_branch_b[i], w_gate[i], b_gate[i], w_out[i], g_ffn[i], w_router[i],
                     router_bias[i], w_exp_gate[i], w_exp_up[i], w_exp_down[i], w_sh_gate[i],
                     w_sh_up[i], w_sh_down[i], g_ple[i], w_ple_gate[i], w_ple_proj[i], g_ple_post[i])
    return x2d.reshape(batch, seq, d)
```

```python
import functools

import numpy as np
import jax
import jax.numpy as jnp
from jax import lax
from jax.experimental import pallas as pl
from jax.experimental.pallas import tpu as pltpu
from jax.experimental.pallas import tpu_sc as plsc

F32 = jnp.float32
BF16 = jnp.bfloat16
I32 = jnp.int32

D_MODEL = 1024
PLE_DIM = 256
POOL_WINDOWS = (2, 4, 8, 16)
POOL_GROUP_DIM = 128
POOL_DIM = 512
N_HEADS = 8
N_KV_HEADS = 2
Q_GROUP = 4
HEAD_DIM = 64
HEAD_PAD = 128
Q_DIM = 512
KV_DIM = 128
ATTN_BLOCK = 128
N_EXPERTS = 64
N_EXPERT_GROUPS = 8
GROUP_SIZE = 8
TOPK_GROUPS = 4
TOP_K = 8
EXPERT_DIM = 256
ROUTED_SCALE = 2.5
MOE_BLOCK = 256
EXPERT_SLOTS = 4
EPS = 1e-6
HALF = D_MODEL // 2
MASKED = -1e30

COL_Q = POOL_DIM
COL_K = COL_Q + N_HEADS * HEAD_PAD
COL_V = COL_K + N_KV_HEADS * HEAD_PAD
IN_PAD = COL_V + N_KV_HEADS * HEAD_PAD
HEAD_ORDER = (0, 2, 1, 3)

TM_MIX = 512
TM_FFN = 512
TM_COMBINE = 512
TM_POS = 4096
COMBINE_CHUNKS = 4
SC_SCATTER_ROWS = 128
SC_GATHER_ROWS = 64
POOL_TAIL = 8
VMEM_LIMIT = 56 * 1024 * 1024

_NT = (((1,), (1,)), ((), ()))


def _dot(a, b):
    return jnp.dot(a, b, preferred_element_type=F32)


def _rms(x, g):
    ms = jnp.mean(x * x, axis=-1, keepdims=True)
    return x * lax.rsqrt(ms + EPS) * g


def _sigmoid(x):
    return 0.5 * jnp.tanh(0.5 * x) + 0.5


def _pack(lo, hi):
    lo_bits = lax.bitcast_convert_type(lo.astype(BF16).astype(F32), I32)
    hi_bits = lax.bitcast_convert_type(hi.astype(BF16).astype(F32), I32)
    return (hi_bits & jnp.int32(-65536)) | lax.shift_right_logical(lo_bits, jnp.int32(16))


def _unpack(w):
    lo = lax.bitcast_convert_type(lax.shift_left(w, jnp.int32(16)), F32)
    hi = lax.bitcast_convert_type(w & jnp.int32(-65536), F32)
    return lo, hi


def _mixer_kernel(x_ref, gmix_ref, win_ref, qg_ref, kg_ref, bias_ref, sink_ref, ones_ref, wmix_ref,
                  pscale_ref, wa_ref, wb_ref, wg_ref, bg_ref, wout_ref,
                  x1_ref, kbuf, vt_buf, pbuf, zbt_buf):
    s = pl.program_id(1)
    tm = x_ref.shape[0]

    @pl.when(s == 0)
    def _():
        kbuf[0:ATTN_BLOCK, :] = jnp.zeros((ATTN_BLOCK, kbuf.shape[1]), BF16)
        vt_buf[:, 0:ATTN_BLOCK] = jnp.zeros((vt_buf.shape[0], ATTN_BLOCK), BF16)
        pbuf[:, 0:POOL_TAIL, :] = jnp.zeros((pbuf.shape[0], POOL_TAIL, POOL_DIM), F32)

    x = x_ref[...]
    hb = _rms(x, gmix_ref[...]).astype(BF16)
    u = _dot(hb, win_ref[...])

    up = u[:, 0:POOL_DIM]
    t0 = POOL_TAIL
    pbuf[0, t0:t0 + tm, :] = up
    s2 = up + pbuf[0, t0 - 1:t0 - 1 + tm, :]
    pbuf[1, t0:t0 + tm, 128:512] = s2[:, 128:512]
    s4 = s2[:, 128:512] + pbuf[1, t0 - 2:t0 - 2 + tm, 128:512]
    pbuf[2, t0:t0 + tm, 256:512] = s4[:, 128:384]
    s8 = s4[:, 128:384] + pbuf[2, t0 - 4:t0 - 4 + tm, 256:512]
    pbuf[3, t0:t0 + tm, 384:512] = s8[:, 128:256]
    s16 = s8[:, 128:256] + pbuf[3, t0 - 8:t0 - 8 + tm, 384:512]
    for lvl in range(4):
        pbuf[lvl, 0:POOL_TAIL, 128 * lvl:512] = pbuf[lvl, tm:tm + POOL_TAIL, 128 * lvl:512]
    wsums = (s2[:, 0:128], s4[:, 0:128], s8[:, 0:128], s16)
    tpos = (s * tm).astype(F32) + lax.broadcasted_iota(I32, (tm, 1), 0).astype(F32)
    za_parts = []
    for g, w in enumerate(POOL_WINDOWS):
        inv_count = 1.0 / jnp.minimum(tpos + 1.0, float(w))
        pooled = wsums[g] * inv_count - up[:, 128 * g:128 * (g + 1)]
        mixed = _dot(pooled.astype(BF16), wmix_ref[g]) * pscale_ref[:, 128 * g:128 * (g + 1)]
        za_parts.append(mixed.astype(BF16))
    za = jnp.concatenate(za_parts, axis=1)

    ones2 = ones_ref[...]
    q = u[:, COL_Q:COL_K]
    q2 = (q * q).astype(BF16)
    qss = jnp.concatenate([_dot(q2[:, 256 * c:256 * (c + 1)], ones2) for c in range(N_HEADS // 2)], axis=1)
    qn = (q * lax.rsqrt(qss * (1.0 / HEAD_DIM) + EPS) * qg_ref[...]).astype(BF16)
    k = u[:, COL_K:COL_V]
    kss = _dot((k * k).astype(BF16), ones2)
    kbuf[ATTN_BLOCK:ATTN_BLOCK + tm, :] = (k * lax.rsqrt(kss * (1.0 / HEAD_DIM) + EPS) * kg_ref[...]).astype(BF16)
    vt_buf[:, ATTN_BLOCK:ATTN_BLOCK + tm] = u[:, COL_V:IN_PAD].T.astype(BF16)

    nq = Q_GROUP * ATTN_BLOCK
    key_j = lax.broadcasted_iota(I32, (ATTN_BLOCK, nq), 0)
    qry_i = lax.broadcasted_iota(I32, (ATTN_BLOCK, nq), 1) & (ATTN_BLOCK - 1)
    from_prev = key_j > qry_i
    first = jnp.where(s == 0, 1, 0)
    for n in range(tm // ATTN_BLOCK):
        r0 = ATTN_BLOCK * n
        for kv in range(N_KV_HEADS):
            c0 = HEAD_PAD * kv
            qs = jnp.concatenate(
                [qn[r0:r0 + ATTN_BLOCK, HEAD_PAD * (kv * Q_GROUP + g):HEAD_PAD * (kv * Q_GROUP + g + 1)]
                 for g in HEAD_ORDER], axis=0)
            kk = kbuf[r0:r0 + 2 * ATTN_BLOCK, c0:c0 + HEAD_PAD]
            st = lax.dot_general(kk, qs, _NT, preferred_element_type=F32)
            sc = jnp.where(from_prev, st[0:ATTN_BLOCK], st[ATTN_BLOCK:2 * ATTN_BLOCK])
            sc = sc + (bias_ref[first, kv] if n == 0 else bias_ref[0, kv])
            sink = sink_ref[kv]
            m = jnp.maximum(jnp.max(sc, axis=0, keepdims=True), sink)
            e = jnp.exp(sc - m)
            inv_den = 1.0 / (jnp.sum(e, axis=0, keepdims=True) + jnp.exp(sink - m))
            pt = jnp.concatenate([jnp.where(from_prev, e, 0.0), jnp.where(from_prev, 0.0, e)],
                                 axis=0).astype(BF16)
            v_even = vt_buf[c0:c0 + HEAD_PAD, r0:r0 + 2 * ATTN_BLOCK]
            v_odd = jnp.concatenate([v_even[HEAD_DIM:HEAD_PAD], v_even[0:HEAD_DIM]], axis=0)
            half = 2 * ATTN_BLOCK
            o = (_dot(v_even, pt[:, 0:half]) * inv_den[:, 0:half]
                 + _dot(v_odd, pt[:, half:2 * half]) * inv_den[:, half:2 * half])
            ch = kv * Q_GROUP * HEAD_DIM
            zbt_buf[ch:ch + HEAD_PAD, r0:r0 + ATTN_BLOCK] = o[:, 0:ATTN_BLOCK]
            zbt_buf[ch + HEAD_PAD:ch + 2 * HEAD_PAD, r0:r0 + ATTN_BLOCK] = o[:, ATTN_BLOCK:half]
    kbuf[0:ATTN_BLOCK, :] = kbuf[tm:tm + ATTN_BLOCK, :]
    vt_buf[:, 0:ATTN_BLOCK] = vt_buf[:, tm:tm + ATTN_BLOCK]

    y_a = _dot(za, wa_ref[...])
    y_b = _dot(zbt_buf[...].T.astype(BF16), wb_ref[...])
    g_a = _sigmoid(_dot(hb, wg_ref[:, 0:D_MODEL]) + bg_ref[:, 0:D_MODEL])
    merged = g_a * y_a
    g_b = _sigmoid(_dot(hb, wg_ref[:, D_MODEL:2 * D_MODEL]) + bg_ref[:, D_MODEL:2 * D_MODEL])
    merged = (merged + g_b * y_b).astype(BF16)
    x1_ref[...] = x + _dot(merged, wout_ref[...])


def _attn_tables(attn_sinks):
    slopes = 2.0 ** (-8.0 * (np.arange(N_HEADS) + 1) / N_HEADS)
    j = np.arange(ATTN_BLOCK)[:, None]
    i = np.arange(ATTN_BLOCK)[None, :]
    from_prev = j > i
    dist = np.where(from_prev, ATTN_BLOCK + i - j, i - j)
    bias = np.empty((2, N_KV_HEADS, ATTN_BLOCK, Q_GROUP * ATTN_BLOCK), np.float32)
    for first in range(2):
        ok = ~from_prev if first else np.ones_like(from_prev)
        for kv in range(N_KV_HEADS):
            for slot, g in enumerate(HEAD_ORDER):
                sl = np.float32(slopes[kv * Q_GROUP + g])
                val = -(sl * dist.astype(np.float32))
                bias[first, kv, :, slot * ATTN_BLOCK:(slot + 1) * ATTN_BLOCK] = np.where(ok, val, MASKED)
    sink = attn_sinks.astype(F32).reshape(N_KV_HEADS, Q_GROUP)[:, np.asarray(HEAD_ORDER)]
    sink = jnp.repeat(sink, ATTN_BLOCK, axis=1)
    return jnp.asarray(bias), sink.reshape(N_KV_HEADS, 1, Q_GROUP * ATTN_BLOCK)


def _pad_heads(w, n_heads):
    k = w.shape[0]
    w = w.reshape(k, n_heads, HEAD_DIM)
    w = jnp.pad(w, ((0, 0), (0, 0), (0, HEAD_PAD - HEAD_DIM)))
    return w.reshape(k, n_heads * HEAD_PAD)


def _const_spec(shape):
    nd = len(shape)
    return pl.BlockSpec(shape, lambda *_: (0,) * nd)


def _mixer(x2d, batch, seq, g_mix, w_in, w_pool_mix, pool_scale, w_branch_a, q_gain, k_gain,
           attn_sinks, w_branch_b, w_gate, b_gate, w_out):
    t = x2d.shape[0]
    tm = TM_MIX
    ns = seq // tm
    w_q = w_in[:, POOL_DIM:POOL_DIM + Q_DIM]
    w_k = w_in[:, POOL_DIM + Q_DIM:POOL_DIM + Q_DIM + KV_DIM]
    w_v = w_in[:, POOL_DIM + Q_DIM + KV_DIM:]
    win_p = jnp.concatenate([w_in[:, :POOL_DIM], _pad_heads(w_q, N_HEADS), _pad_heads(w_k, N_KV_HEADS),
                             _pad_heads(w_v, N_KV_HEADS)], axis=1).astype(BF16)
    qg = jnp.tile(jnp.pad(q_gain * (HEAD_DIM ** -0.5), (0, HEAD_PAD - HEAD_DIM)), N_HEADS).reshape(1, -1)
    kg = jnp.tile(jnp.pad(k_gain, (0, HEAD_PAD - HEAD_DIM)), N_KV_HEADS).reshape(1, -1)
    bias, sink = _attn_tables(attn_sinks)
    ones2 = jnp.asarray(np.kron(np.eye(2), np.ones((HEAD_PAD, HEAD_PAD))), BF16)
    operands = (x2d, g_mix.reshape(1, D_MODEL), win_p, qg, kg, bias, sink, ones2, w_pool_mix.astype(BF16),
                pool_scale.reshape(1, POOL_DIM), w_branch_a.astype(BF16), w_branch_b.astype(BF16),
                w_gate.astype(BF16), b_gate.reshape(1, 2 * D_MODEL), w_out.astype(BF16))
    in_specs = [pl.BlockSpec((tm, D_MODEL), lambda b, s: (b * ns + s, 0))]
    in_specs += [_const_spec(op.shape) for op in operands[1:]]
    return pl.pallas_call(
        _mixer_kernel,
        grid=(batch, ns),
        in_specs=in_specs,
        out_specs=pl.BlockSpec((tm, D_MODEL), lambda b, s: (b * ns + s, 0)),
        out_shape=jax.ShapeDtypeStruct((t, D_MODEL), F32),
        scratch_shapes=[
            pltpu.VMEM((ATTN_BLOCK + tm, N_KV_HEADS * HEAD_PAD), BF16),
            pltpu.VMEM((N_KV_HEADS * HEAD_PAD, ATTN_BLOCK + tm), BF16),
            pltpu.VMEM((4, POOL_TAIL + tm, POOL_DIM), F32),
            pltpu.VMEM((Q_DIM, tm), F32),
        ],
        compiler_params=pltpu.CompilerParams(
            dimension_semantics=("arbitrary", "arbitrary"), vmem_limit_bytes=VMEM_LIMIT),
        name="mixer",
    )(*operands)


def _ffn_pre_kernel(x1_ref, gffn_ref, wrh_ref, wrl_ref, rb_ref, tri_ref,
                    h2p_ref, idx_ref, wt_ref, rk_ref, cnt_ref, run_ref):
    i = pl.program_id(0)
    tm = x1_ref.shape[0]

    @pl.when(i == 0)
    def _():
        run_ref[...] = jnp.zeros_like(run_ref)

    h2 = _rms(x1_ref[...], gffn_ref[...])
    hb = h2.astype(BF16)
    h2p_ref[...] = _pack(h2[:, 0:HALF], h2[:, HALF:D_MODEL])

    lo = (h2 - hb.astype(F32)).astype(BF16)
    wrh = wrh_ref[...]
    logits = (lax.dot_general(wrh, hb, _NT, preferred_element_type=F32)
              + lax.dot_general(wrl_ref[...], hb, _NT, preferred_element_type=F32)
              + lax.dot_general(wrh, lo, _NT, preferred_element_type=F32))
    scores = _sigmoid(logits)
    choice = scores + rb_ref[...]

    neg = -jnp.inf
    sub8 = lax.broadcasted_iota(I32, (GROUP_SIZE, tm), 0).astype(F32)
    grp_rows = []
    for gi in range(N_EXPERT_GROUPS):
        cg = choice[GROUP_SIZE * gi:GROUP_SIZE * (gi + 1)]
        m1 = jnp.max(cg, axis=0, keepdims=True)
        first = jnp.min(jnp.where(cg == m1, sub8, float(GROUP_SIZE)), axis=0, keepdims=True)
        m2 = jnp.max(jnp.where(sub8 == first, neg, cg), axis=0, keepdims=True)
        grp_rows.append(m1 + m2)
    gs = jnp.concatenate(grp_rows, axis=0)
    beaten = jnp.zeros((N_EXPERT_GROUPS, tm), F32)
    for gi in range(N_EXPERT_GROUPS):
        row = grp_rows[gi]
        wins = (row > gs) | ((row == gs) & (sub8 > float(gi)))
        beaten = beaten + jnp.where(wins, 1.0, 0.0)
    gsel = beaten < float(TOPK_GROUPS)
    cm = jnp.concatenate(
        [jnp.where(gsel[gi:gi + 1], choice[GROUP_SIZE * gi:GROUP_SIZE * (gi + 1)], neg)
         for gi in range(N_EXPERT_GROUPS)], axis=0)

    iota_e = lax.broadcasted_iota(I32, (N_EXPERTS, tm), 0).astype(F32)
    idx_rows, s_rows = [], []
    sel = jnp.zeros((N_EXPERTS, tm), F32)
    for _ in range(TOP_K):
        m = jnp.max(cm, axis=0, keepdims=True)
        idx = jnp.min(jnp.where(cm == m, iota_e, float(N_EXPERTS)), axis=0, keepdims=True)
        oh = iota_e == idx
        s_rows.append(jnp.sum(jnp.where(oh, scores, 0.0), axis=0, keepdims=True))
        idx_rows.append(idx)
        cm = jnp.where(oh, neg, cm)
        sel = sel + jnp.where(oh, 1.0, 0.0)

    run = run_ref[:, 0:1]
    cum = _dot(sel.astype(BF16), tri_ref[...])
    before = run + cum - sel
    rk_rows = [jnp.sum(jnp.where(iota_e == idx, before, 0.0), axis=0, keepdims=True) for idx in idx_rows]
    new_run = run + jnp.sum(sel, axis=1, keepdims=True)
    run_ref[...] = jnp.broadcast_to(new_run, run_ref.shape)
    cnt_ref[...] = jnp.broadcast_to(new_run, cnt_ref.shape)

    ssum = s_rows[0]
    for r in range(1, TOP_K):
        ssum = ssum + s_rows[r]
    denom = ssum + 1e-20
    idx_ref[...] = jnp.concatenate(idx_rows, axis=0).astype(I32)
    wt_ref[...] = jnp.concatenate([sr / denom * ROUTED_SCALE for sr in s_rows], axis=0)
    rk_ref[...] = jnp.concatenate(rk_rows, axis=0).astype(I32)


def _ffn_pre(x1, g_ffn, w_router, router_bias):
    t = x1.shape[0]
    tm = TM_FFN
    wr_t = w_router.astype(F32).T
    wr_hi = wr_t.astype(BF16)
    wr_lo = (wr_t - wr_hi.astype(F32)).astype(BF16)
    tri = (np.arange(tm)[:, None] <= np.arange(tm)[None, :]).astype(np.float32)
    operands = (x1, g_ffn.reshape(1, D_MODEL), wr_hi, wr_lo,
                router_bias.astype(F32).reshape(N_EXPERTS, 1), jnp.asarray(tri, BF16))
    in_specs = [pl.BlockSpec((tm, D_MODEL), lambda i: (i, 0))]
    in_specs += [_const_spec(op.shape) for op in operands[1:]]
    row8 = pl.BlockSpec((TOP_K, tm), lambda i: (0, i))
    return pl.pallas_call(
        _ffn_pre_kernel,
        grid=(t // tm,),
        in_specs=in_specs,
        out_specs=[pl.BlockSpec((tm, HALF), lambda i: (i, 0)),
                   row8, row8, row8,
                   pl.BlockSpec((N_EXPERTS, 128), lambda i: (0, 0))],
        out_shape=[jax.ShapeDtypeStruct((t, HALF), I32),
                   jax.ShapeDtypeStruct((TOP_K, t), I32),
                   jax.ShapeDtypeStruct((TOP_K, t), F32),
                   jax.ShapeDtypeStruct((TOP_K, t), I32),
                   jax.ShapeDtypeStruct((N_EXPERTS, 128), F32)],
        scratch_shapes=[pltpu.VMEM((N_EXPERTS, 128), F32)],
        compiler_params=pltpu.CompilerParams(
            dimension_semantics=("arbitrary",), vmem_limit_bytes=VMEM_LIMIT),
        name="ffn_pre",
    )(*operands)


def _swiglu_packed(x_packed, wg, wu, wd):
    lo, hi = _unpack(x_packed)
    lo = lo.astype(BF16)
    hi = hi.astype(BF16)
    g = _dot(lo, wg[0:HALF, :]) + _dot(hi, wg[HALF:D_MODEL, :])
    u = _dot(lo, wu[0:HALF, :]) + _dot(hi, wu[HALF:D_MODEL, :])
    a = (g * _sigmoid(g) * u).astype(BF16)
    y = _dot(a, wd[...])
    return _pack(y[:, 0:HALF], y[:, HALF:D_MODEL])


def _shared_kernel(h2p_ref, wg_ref, wu_ref, wd_ref, ysh_ref):
    ysh_ref[...] = _swiglu_packed(h2p_ref[...], wg_ref, wu_ref, wd_ref)


def _shared(h2p, w_sh_gate, w_sh_up, w_sh_down):
    t = h2p.shape[0]
    tm = TM_FFN
    operands = (h2p, w_sh_gate.astype(BF16), w_sh_up.astype(BF16), w_sh_down.astype(BF16))
    in_specs = [pl.BlockSpec((tm, HALF), lambda i: (i, 0))]
    in_specs += [_const_spec(op.shape) for op in operands[1:]]
    return pl.pallas_call(
        _shared_kernel,
        grid=(t // tm,),
        in_specs=in_specs,
        out_specs=pl.BlockSpec((tm, HALF), lambda i: (i, 0)),
        out_shape=jax.ShapeDtypeStruct((t, HALF), I32),
        compiler_params=pltpu.CompilerParams(
            dimension_semantics=("arbitrary",), vmem_limit_bytes=VMEM_LIMIT),
        name="shared",
    )(*operands)


def _positions_kernel(pstart_ref, idx_ref, rk_ref, pos_ref):
    idx = idx_ref[...]
    acc = rk_ref[...]
    for e in range(N_EXPERTS):
        acc = acc + jnp.where(idx == e, pstart_ref[e], 0)
    pos_ref[...] = acc


def _positions(pstart, idx8, rk8):
    t = idx8.shape[1]
    tm = TM_POS
    spec = pl.BlockSpec((TOP_K, tm), lambda i, ps: (0, i))
    return pl.pallas_call(
        _positions_kernel,
        grid_spec=pltpu.PrefetchScalarGridSpec(num_scalar_prefetch=1, grid=(t // tm,),
                                               in_specs=[spec, spec], out_specs=spec),
        out_shape=jax.ShapeDtypeStruct((TOP_K, t), I32),
        name="positions",
    )(pstart, idx8, rk8)


def _sc_workers():
    info = plsc.get_sparse_core_info()
    return info.num_cores, info.num_cores * info.num_subcores


def _dispatch(pos, h2p, n_rows):
    t = h2p.shape[0]
    ch = SC_SCATTER_ROWS
    n_cores, n_workers = _sc_workers()
    n_ch = t // (n_workers * ch)
    pos3 = pos.reshape(TOP_K, t // ch, ch).transpose(1, 0, 2)

    def body(pos_hbm, h_hbm, xs_hbm, idx_v, rows_v, sem):
        wid = lax.axis_index("s") * n_cores + lax.axis_index("c")

        @pl.loop(0, n_ch)
        def _(c):
            chunk = wid * n_ch + c
            pltpu.sync_copy(h_hbm.at[pl.ds(chunk * ch, ch)], rows_v)
            pltpu.sync_copy(pos_hbm.at[chunk], idx_v)
            copies = [pltpu.async_copy(rows_v, xs_hbm.at[idx_v.at[k]], sem) for k in range(TOP_K)]
            for cp in copies:
                cp.wait()

    return pl.kernel(
        body,
        out_type=jax.ShapeDtypeStruct((n_rows, HALF), I32),
        mesh=plsc.VectorSubcoreMesh(core_axis_name="c", subcore_axis_name="s"),
        scratch_types=[pltpu.VMEM((TOP_K, ch), I32), pltpu.VMEM((ch, HALF), I32),
                       pltpu.SemaphoreType.DMA],
        name="dispatch",
    )(pos3, h2p)


def _gather_back(pos, ys):
    t = pos.shape[1]
    ch = SC_GATHER_ROWS
    n_cores, n_workers = _sc_workers()
    n_ch = t // (n_workers * ch)
    posg = pos.reshape(TOP_K, t // ch, ch).transpose(1, 0, 2).reshape(-1)

    def body(pos_hbm, ys_hbm, yt_hbm, idx_v, buf0, buf1, gsem, wsem0, wsem1):
        wid = lax.axis_index("s") * n_cores + lax.axis_index("c")
        bufs = (buf0, buf1)
        wsems = (wsem0, wsem1)

        @pl.loop(0, n_ch)
        def _(c):
            chunk = wid * n_ch + c
            pltpu.sync_copy(pos_hbm.at[pl.ds(chunk * (TOP_K * ch), TOP_K * ch)], idx_v)
            writes = [None, None]
            for k in range(TOP_K):
                b = k % 2
                if writes[b] is not None:
                    writes[b].wait()
                pltpu.async_copy(ys_hbm.at[idx_v.at[pl.ds(k * ch, ch)]], bufs[b], gsem).wait()
                writes[b] = pltpu.async_copy(bufs[b], yt_hbm.at[k, pl.ds(chunk * ch, ch)], wsems[b])
            writes[0].wait()
            writes[1].wait()

    return pl.kernel(
        body,
        out_type=jax.ShapeDtypeStruct((TOP_K, t, HALF), I32),
        mesh=plsc.VectorSubcoreMesh(core_axis_name="c", subcore_axis_name="s"),
        scratch_types=[pltpu.VMEM((TOP_K * ch,), I32), pltpu.VMEM((ch, HALF), I32),
                       pltpu.VMEM((ch, HALF), I32), pltpu.SemaphoreType.DMA,
                       pltpu.SemaphoreType.DMA, pltpu.SemaphoreType.DMA],
        name="gather_back",
    )(posg, ys)


def _expert_kernel(nblk_ref, blk0_ref, cnt_ref, wg_ref, wu_ref, wd_ref, xs_hbm, after_ref, ys_hbm,
                   wg_s, wu_s, wd_s, xbuf, ybuf, in_sem, out_sem):
    del after_ref
    e = pl.program_id(0)
    nb = nblk_ref[e]
    b0 = blk0_ref[e]
    cnt = cnt_ref[e]
    n_all = blk0_ref[N_EXPERTS]
    ahead = EXPERT_SLOTS - 1
    wg_s[...] = wg_ref[0].astype(BF16)
    wu_s[...] = wu_ref[0].astype(BF16)
    wd_s[...] = wd_ref[0].astype(BF16)

    def slot_of(g):
        return g & (EXPERT_SLOTS - 1)

    def rows_of(g):
        return pl.ds(pl.multiple_of(g * MOE_BLOCK, MOE_BLOCK), MOE_BLOCK)

    def in_copy(g):
        return pltpu.make_async_copy(xs_hbm.at[rows_of(g)], xbuf.at[slot_of(g)], in_sem.at[slot_of(g)])

    def out_copy(g):
        return pltpu.make_async_copy(ybuf.at[slot_of(g)], ys_hbm.at[rows_of(g)], out_sem.at[slot_of(g)])

    @pl.when(e == 0)
    def _():
        for g in range(ahead):
            @pl.when(g < n_all)
            def _():
                in_copy(g).start()

    def one_block(j, carry):
        g = b0 + j
        in_copy(g).wait()

        @pl.when(g + ahead < n_all)
        def _():
            in_copy(g + ahead).start()

        @pl.when(g >= EXPERT_SLOTS)
        def _():
            out_copy(g - EXPERT_SLOTS).wait()

        rows = lax.broadcasted_iota(I32, (MOE_BLOCK, HALF), 0)
        x = jnp.where(rows < cnt - j * MOE_BLOCK, xbuf[slot_of(g)], 0)
        ybuf[slot_of(g)] = _swiglu_packed(x, wg_s, wu_s, wd_s)
        out_copy(g).start()
        return carry

    lax.fori_loop(0, nb, one_block, 0)

    @pl.when(e == N_EXPERTS - 1)
    def _():
        for back in range(1, EXPERT_SLOTS + 1):
            @pl.when(n_all >= back)
            def _():
                out_copy(n_all - back).wait()


def _experts(n_blk, blk0, counts, n_rows, xs, w_gate, w_up, w_down, after):
    wspec_in = pl.BlockSpec((1, D_MODEL, EXPERT_DIM), lambda e, *_: (e, 0, 0))
    grid_spec = pltpu.PrefetchScalarGridSpec(
        num_scalar_prefetch=3,
        grid=(N_EXPERTS,),
        in_specs=[wspec_in, wspec_in,
                  pl.BlockSpec((1, EXPERT_DIM, D_MODEL), lambda e, *_: (e, 0, 0)),
                  pl.BlockSpec(memory_space=pl.ANY),
                  pl.BlockSpec(memory_space=pl.ANY)],
        out_specs=pl.BlockSpec(memory_space=pl.ANY),
        scratch_shapes=[pltpu.VMEM((D_MODEL, EXPERT_DIM), BF16), pltpu.VMEM((D_MODEL, EXPERT_DIM), BF16),
                        pltpu.VMEM((EXPERT_DIM, D_MODEL), BF16),
                        pltpu.VMEM((EXPERT_SLOTS, MOE_BLOCK, HALF), I32),
                        pltpu.VMEM((EXPERT_SLOTS, MOE_BLOCK, HALF), I32),
                        pltpu.SemaphoreType.DMA((EXPERT_SLOTS,)), pltpu.SemaphoreType.DMA((EXPERT_SLOTS,))],
    )
    return pl.pallas_call(
        _expert_kernel,
        grid_spec=grid_spec,
        out_shape=jax.ShapeDtypeStruct((n_rows, HALF), I32),
        compiler_params=pltpu.CompilerParams(
            dimension_semantics=("arbitrary",), vmem_limit_bytes=VMEM_LIMIT),
        name="experts",
    )(n_blk, blk0, counts, w_gate, w_up, w_down, xs, after)


def _combine_kernel(x1_ref, ysh_ref, yt_ref, wt_ref, p_ref, gple_ref, wpg_ref, wpp_ref, gpost_ref, *rest):
    out_ref = rest[-1]
    proj = _rms(_dot(p_ref[...].astype(BF16), wpp_ref[...]), gpost_ref[...])
    x1 = x1_ref[...]
    sh_lo, sh_hi = _unpack(ysh_ref[...])
    acc_lo = x1[:, 0:HALF] + sh_lo
    acc_hi = x1[:, HALF:D_MODEL] + sh_hi
    wt = wt_ref[...]
    for k in range(TOP_K):
        lo, hi = _unpack(yt_ref[k])
        wk = wt[:, k:k + 1]
        acc_lo = acc_lo + wk * lo
        acc_hi = acc_hi + wk * hi
    x2 = jnp.concatenate([acc_lo, acc_hi], axis=1)
    gate = _sigmoid(_dot(_rms(x2, gple_ref[...]).astype(BF16), wpg_ref[...]))
    out_ref[...] = x2 + gate * proj


def _combine(x1, ysh, yt, wt_t, p2d, g_ple, w_ple_gate, w_ple_proj, g_ple_post, chunk, prev_out):
    t = x1.shape[0]
    tm = TM_COMBINE
    steps = yt.shape[1] // tm
    off = chunk * steps
    operands = [x1, ysh, yt, wt_t, p2d, g_ple.reshape(1, D_MODEL), w_ple_gate.astype(BF16),
                w_ple_proj.astype(BF16), g_ple_post.reshape(1, D_MODEL)]
    in_specs = [pl.BlockSpec((tm, D_MODEL), lambda i: (off + i, 0)),
                pl.BlockSpec((tm, HALF), lambda i: (off + i, 0)),
                pl.BlockSpec((TOP_K, tm, HALF), lambda i: (0, i, 0)),
                pl.BlockSpec((tm, TOP_K), lambda i: (off + i, 0)),
                pl.BlockSpec((tm, PLE_DIM), lambda i: (off + i, 0)),
                _const_spec((1, D_MODEL)),
                _const_spec((D_MODEL, D_MODEL)),
                _const_spec((PLE_DIM, D_MODEL)),
                _const_spec((1, D_MODEL))]
    aliases = {}
    if prev_out is not None:
        aliases = {len(operands): 0}
        operands.append(prev_out)
        in_specs.append(pl.BlockSpec(memory_space=pl.ANY))
    return pl.pallas_call(
        _combine_kernel,
        grid=(steps,),
        in_specs=in_specs,
        out_specs=pl.BlockSpec((tm, D_MODEL), lambda i: (off + i, 0)),
        out_shape=jax.ShapeDtypeStruct((t, D_MODEL), F32),
        input_output_aliases=aliases,
        compiler_params=pltpu.CompilerParams(
            dimension_semantics=("arbitrary",), vmem_limit_bytes=VMEM_LIMIT),
        name="combine",
    )(*operands)


def _layer(x2d, p2d, batch, seq, g_mix, w_in, w_pool_mix, pool_scale, w_branch_a, q_gain, k_gain,
           attn_sinks, w_branch_b, w_gate, b_gate, w_out, g_ffn, w_router, router_bias, w_exp_gate,
           w_exp_up, w_exp_down, w_sh_gate, w_sh_up, w_sh_down, g_ple, w_ple_gate, w_ple_proj,
           g_ple_post):
    t = x2d.shape[0]
    x1 = _mixer(x2d, batch, seq, g_mix, w_in, w_pool_mix, pool_scale, w_branch_a, q_gain, k_gain,
                attn_sinks, w_branch_b, w_gate, b_gate, w_out)
    h2p, idx8, wt8, rk8, cnt = _ffn_pre(x1, g_ffn, w_router, router_bias)

    counts = cnt[:, 0].astype(I32)
    padded = (counts + MOE_BLOCK - 1) // MOE_BLOCK * MOE_BLOCK
    pend = jnp.cumsum(padded)
    pstart = pend - padded
    pos = _positions(pstart.astype(I32), idx8, rk8)
    n_rows = (-(-(t * TOP_K) // MOE_BLOCK) + N_EXPERTS) * MOE_BLOCK

    xs = _dispatch(pos, h2p, n_rows)
    ysh = _shared(h2p, w_sh_gate, w_sh_up, w_sh_down)
    blk0 = jnp.concatenate([pstart, pend[-1:]]) // MOE_BLOCK
    ys = _experts((padded // MOE_BLOCK).astype(I32), blk0.astype(I32), counts, n_rows,
                  xs, w_exp_gate, w_exp_up, w_exp_down, ysh)
    wt_t = wt8.T
    tc = t // COMBINE_CHUNKS
    out = None
    for c in range(COMBINE_CHUNKS):
        yt = _gather_back(pos[:, c * tc:(c + 1) * tc], ys)
        out = _combine(x1, ysh, yt, wt_t, p2d, g_ple, w_ple_gate, w_ple_proj, g_ple_post, c, out)
    return out


def kernel(x, p, g_mix, w_in, w_pool_mix, pool_scale, w_branch_a, q_gain, k_gain, attn_sinks,
           w_branch_b, w_gate, b_gate, w_out, g_ffn, w_router, router_bias, w_exp_gate, w_exp_up,
           w_exp_down, w_sh_gate, w_sh_up, w_sh_down, g_ple, w_ple_gate, w_ple_proj, g_ple_post):
    batch, seq, d = x.shape
    depth = p.shape[0]
    x2d = x.reshape(batch * seq, d)
    for i in range(depth):
        x2d = _layer(x2d, p[i].reshape(batch * seq, PLE_DIM), batch, seq, g_mix[i], w_in[i],
                     w_pool_mix[i], pool_scale[i], w_branch_a[i], q_gain[i], k_gain[i], attn_sinks[i],
                     w_branch_b[i], w_gate[i], b_gate[i], w_out[i], g_ffn[i], w_router[i],
                     router_bias[i], w_exp_gate[i], w_exp_up[i], w_exp_down[i], w_sh_gate[i],
                     w_sh_up[i], w_sh_down[i], g_ple[i], w_ple_gate[i], w_ple_proj[i], g_ple_post[i])
    return x2d.reshape(batch, seq, d)
```

```python
import functools

import numpy as np
import jax
import jax.numpy as jnp
from jax import lax
from jax.experimental import pallas as pl
from jax.experimental.pallas import tpu as pltpu
from jax.experimental.pallas import tpu_sc as plsc

F32 = jnp.float32
BF16 = jnp.bfloat16
I32 = jnp.int32

D_MODEL = 1024
PLE_DIM = 256
POOL_WINDOWS = (2, 4, 8, 16)
POOL_GROUP_DIM = 128
POOL_DIM = 512
N_HEADS = 8
N_KV_HEADS = 2
Q_GROUP = 4
HEAD_DIM = 64
HEAD_PAD = 128
Q_DIM = 512
KV_DIM = 128
ATTN_BLOCK = 128
N_EXPERTS = 64
N_EXPERT_GROUPS = 8
GROUP_SIZE = 8
TOPK_GROUPS = 4
TOP_K = 8
EXPERT_DIM = 256
ROUTED_SCALE = 2.5
MOE_BLOCK = 512
EXPERT_SLOTS = 4
EPS = 1e-6
HALF = D_MODEL // 2
MASKED = -1e30

COL_Q = POOL_DIM
COL_K = COL_Q + N_HEADS * HEAD_PAD
COL_V = COL_K + N_KV_HEADS * HEAD_PAD
IN_PAD = COL_V + N_KV_HEADS * HEAD_PAD
HEAD_ORDER = (0, 2, 1, 3)

TM_MIX = 512
TM_FFN = 512
TM_COMBINE = 512
TM_POS = 4096
COMBINE_CHUNKS = 4
SC_SCATTER_ROWS = 128
SC_GATHER_ROWS = 64
POOL_TAIL = 8
VMEM_LIMIT = 56 * 1024 * 1024

_NT = (((1,), (1,)), ((), ()))


def _dot(a, b):
    return jnp.dot(a, b, preferred_element_type=F32)


def _rms(x, g):
    ms = jnp.mean(x * x, axis=-1, keepdims=True)
    return x * lax.rsqrt(ms + EPS) * g


def _sigmoid(x):
    return 0.5 * jnp.tanh(0.5 * x) + 0.5


def _pack(lo, hi):
    lo_bits = lax.bitcast_convert_type(lo.astype(BF16).astype(F32), I32)
    hi_bits = lax.bitcast_convert_type(hi.astype(BF16).astype(F32), I32)
    return (hi_bits & jnp.int32(-65536)) | lax.shift_right_logical(lo_bits, jnp.int32(16))


def _unpack(w):
    lo = lax.bitcast_convert_type(lax.shift_left(w, jnp.int32(16)), F32)
    hi = lax.bitcast_convert_type(w & jnp.int32(-65536), F32)
    return lo, hi


def _mixer_kernel(x_ref, gmix_ref, win_ref, qg_ref, kg_ref, bias_ref, sink_ref, ones_ref, wmix_ref,
                  pscale_ref, wa_ref, wb_ref, wg_ref, bg_ref, wout_ref,
                  x1_ref, kbuf, vt_buf, pbuf, zbt_buf):
    s = pl.program_id(1)
    tm = x_ref.shape[0]

    @pl.when(s == 0)
    def _():
        kbuf[0:ATTN_BLOCK, :] = jnp.zeros((ATTN_BLOCK, kbuf.shape[1]), BF16)
        vt_buf[:, 0:ATTN_BLOCK] = jnp.zeros((vt_buf.shape[0], ATTN_BLOCK), BF16)
        pbuf[:, 0:POOL_TAIL, :] = jnp.zeros((pbuf.shape[0], POOL_TAIL, POOL_DIM), F32)

    x = x_ref[...]
    hb = _rms(x, gmix_ref[...]).astype(BF16)
    u = _dot(hb, win_ref[...])

    up = u[:, 0:POOL_DIM]
    t0 = POOL_TAIL
    pbuf[0, t0:t0 + tm, :] = up
    s2 = up + pbuf[0, t0 - 1:t0 - 1 + tm, :]
    pbuf[1, t0:t0 + tm, 128:512] = s2[:, 128:512]
    s4 = s2[:, 128:512] + pbuf[1, t0 - 2:t0 - 2 + tm, 128:512]
    pbuf[2, t0:t0 + tm, 256:512] = s4[:, 128:384]
    s8 = s4[:, 128:384] + pbuf[2, t0 - 4:t0 - 4 + tm, 256:512]
    pbuf[3, t0:t0 + tm, 384:512] = s8[:, 128:256]
    s16 = s8[:, 128:256] + pbuf[3, t0 - 8:t0 - 8 + tm, 384:512]
    for lvl in range(4):
        pbuf[lvl, 0:POOL_TAIL, 128 * lvl:512] = pbuf[lvl, tm:tm + POOL_TAIL, 128 * lvl:512]
    wsums = (s2[:, 0:128], s4[:, 0:128], s8[:, 0:128], s16)
    tpos = (s * tm).astype(F32) + lax.broadcasted_iota(I32, (tm, 1), 0).astype(F32)
    za_parts = []
    for g, w in enumerate(POOL_WINDOWS):
        inv_count = 1.0 / jnp.minimum(tpos + 1.0, float(w))
        pooled = wsums[g] * inv_count - up[:, 128 * g:128 * (g + 1)]
        mixed = _dot(pooled.astype(BF16), wmix_ref[g]) * pscale_ref[:, 128 * g:128 * (g + 1)]
        za_parts.append(mixed.astype(BF16))
    za = jnp.concatenate(za_parts, axis=1)

    ones2 = ones_ref[...]
    q = u[:, COL_Q:COL_K]
    q2 = (q * q).astype(BF16)
    qss = jnp.concatenate([_dot(q2[:, 256 * c:256 * (c + 1)], ones2) for c in range(N_HEADS // 2)], axis=1)
    qn = (q * lax.rsqrt(qss * (1.0 / HEAD_DIM) + EPS) * qg_ref[...]).astype(BF16)
    k = u[:, COL_K:COL_V]
    kss = _dot((k * k).astype(BF16), ones2)
    kbuf[ATTN_BLOCK:ATTN_BLOCK + tm, :] = (k * lax.rsqrt(kss * (1.0 / HEAD_DIM) + EPS) * kg_ref[...]).astype(BF16)
    vt_buf[:, ATTN_BLOCK:ATTN_BLOCK + tm] = u[:, COL_V:IN_PAD].T.astype(BF16)

    nq = Q_GROUP * ATTN_BLOCK
    key_j = lax.broadcasted_iota(I32, (ATTN_BLOCK, nq), 0)
    qry_i = lax.broadcasted_iota(I32, (ATTN_BLOCK, nq), 1) & (ATTN_BLOCK - 1)
    from_prev = key_j > qry_i
    first = jnp.where(s == 0, 1, 0)
    for n in range(tm // ATTN_BLOCK):
        r0 = ATTN_BLOCK * n
        for kv in range(N_KV_HEADS):
            c0 = HEAD_PAD * kv
            qs = jnp.concatenate(
                [qn[r0:r0 + ATTN_BLOCK, HEAD_PAD * (kv * Q_GROUP + g):HEAD_PAD * (kv * Q_GROUP + g + 1)]
                 for g in HEAD_ORDER], axis=0)
            kk = kbuf[r0:r0 + 2 * ATTN_BLOCK, c0:c0 + HEAD_PAD]
            st = lax.dot_general(kk, qs, _NT, preferred_element_type=F32)
            sc = jnp.where(from_prev, st[0:ATTN_BLOCK], st[ATTN_BLOCK:2 * ATTN_BLOCK])
            sc = sc + (bias_ref[first, kv] if n == 0 else bias_ref[0, kv])
            sink = sink_ref[kv]
            m = jnp.maximum(jnp.max(sc, axis=0, keepdims=True), sink)
            e = jnp.exp(sc - m)
            inv_den = 1.0 / (jnp.sum(e, axis=0, keepdims=True) + jnp.exp(sink - m))
            pt = jnp.concatenate([jnp.where(from_prev, e, 0.0), jnp.where(from_prev, 0.0, e)],
                                 axis=0).astype(BF16)
            v_even = vt_buf[c0:c0 + HEAD_PAD, r0:r0 + 2 * ATTN_BLOCK]
            v_odd = jnp.concatenate([v_even[HEAD_DIM:HEAD_PAD], v_even[0:HEAD_DIM]], axis=0)
            half = 2 * ATTN_BLOCK
            o = (_dot(v_even, pt[:, 0:half]) * inv_den[:, 0:half]
                 + _dot(v_odd, pt[:, half:2 * half]) * inv_den[:, half:2 * half])
            ch = kv * Q_GROUP * HEAD_DIM
            zbt_buf[ch:ch + HEAD_PAD, r0:r0 + ATTN_BLOCK] = o[:, 0:ATTN_BLOCK]
            zbt_buf[ch + HEAD_PAD:ch + 2 * HEAD_PAD, r0:r0 + ATTN_BLOCK] = o[:, ATTN_BLOCK:half]
    kbuf[0:ATTN_BLOCK, :] = kbuf[tm:tm + ATTN_BLOCK, :]
    vt_buf[:, 0:ATTN_BLOCK] = vt_buf[:, tm:tm + ATTN_BLOCK]

    y_a = _dot(za, wa_ref[...])
    y_b = _dot(zbt_buf[...].T.astype(BF16), wb_ref[...])
    g_a = _sigmoid(_dot(hb, wg_ref[:, 0:D_MODEL]) + bg_ref[:, 0:D_MODEL])
    merged = g_a * y_a
    g_b = _sigmoid(_dot(hb, wg_ref[:, D_MODEL:2 * D_MODEL]) + bg_ref[:, D_MODEL:2 * D_MODEL])
    merged = (merged + g_b * y_b).astype(BF16)
    x1_ref[...] = x + _dot(merged, wout_ref[...])


def _attn_tables(attn_sinks):
    slopes = 2.0 ** (-8.0 * (np.arange(N_HEADS) + 1) / N_HEADS)
    j = np.arange(ATTN_BLOCK)[:, None]
    i = np.arange(ATTN_BLOCK)[None, :]
    from_prev = j > i
    dist = np.where(from_prev, ATTN_BLOCK + i - j, i - j)
    bias = np.empty((2, N_KV_HEADS, ATTN_BLOCK, Q_GROUP * ATTN_BLOCK), np.float32)
    for first in range(2):
        ok = ~from_prev if first else np.ones_like(from_prev)
        for kv in range(N_KV_HEADS):
            for slot, g in enumerate(HEAD_ORDER):
                sl = np.float32(slopes[kv * Q_GROUP + g])
                val = -(sl * dist.astype(np.float32))
                bias[first, kv, :, slot * ATTN_BLOCK:(slot + 1) * ATTN_BLOCK] = np.where(ok, val, MASKED)
    sink = attn_sinks.astype(F32).reshape(N_KV_HEADS, Q_GROUP)[:, np.asarray(HEAD_ORDER)]
    sink = jnp.repeat(sink, ATTN_BLOCK, axis=1)
    return jnp.asarray(bias), sink.reshape(N_KV_HEADS, 1, Q_GROUP * ATTN_BLOCK)


def _pad_heads(w, n_heads):
    k = w.shape[0]
    w = w.reshape(k, n_heads, HEAD_DIM)
    w = jnp.pad(w, ((0, 0), (0, 0), (0, HEAD_PAD - HEAD_DIM)))
    return w.reshape(k, n_heads * HEAD_PAD)


def _const_spec(shape):
    nd = len(shape)
    return pl.BlockSpec(shape, lambda *_: (0,) * nd)


def _mixer(x2d, batch, seq, g_mix, w_in, w_pool_mix, pool_scale, w_branch_a, q_gain, k_gain,
           attn_sinks, w_branch_b, w_gate, b_gate, w_out):
    t = x2d.shape[0]
    tm = TM_MIX
    ns = seq // tm
    w_q = w_in[:, POOL_DIM:POOL_DIM + Q_DIM]
    w_k = w_in[:, POOL_DIM + Q_DIM:POOL_DIM + Q_DIM + KV_DIM]
    w_v = w_in[:, POOL_DIM + Q_DIM + KV_DIM:]
    win_p = jnp.concatenate([w_in[:, :POOL_DIM], _pad_heads(w_q, N_HEADS), _pad_heads(w_k, N_KV_HEADS),
                             _pad_heads(w_v, N_KV_HEADS)], axis=1).astype(BF16)
    qg = jnp.tile(jnp.pad(q_gain * (HEAD_DIM ** -0.5), (0, HEAD_PAD - HEAD_DIM)), N_HEADS).reshape(1, -1)
    kg = jnp.tile(jnp.pad(k_gain, (0, HEAD_PAD - HEAD_DIM)), N_KV_HEADS).reshape(1, -1)
    bias, sink = _attn_tables(attn_sinks)
    ones2 = jnp.asarray(np.kron(np.eye(2), np.ones((HEAD_PAD, HEAD_PAD))), BF16)
    operands = (x2d, g_mix.reshape(1, D_MODEL), win_p, qg, kg, bias, sink, ones2, w_pool_mix.astype(BF16),
                pool_scale.reshape(1, POOL_DIM), w_branch_a.astype(BF16), w_branch_b.astype(BF16),
                w_gate.astype(BF16), b_gate.reshape(1, 2 * D_MODEL), w_out.astype(BF16))
    in_specs = [pl.BlockSpec((tm, D_MODEL), lambda b, s: (b * ns + s, 0))]
    in_specs += [_const_spec(op.shape) for op in operands[1:]]
    return pl.pallas_call(
        _mixer_kernel,
        grid=(batch, ns),
        in_specs=in_specs,
        out_specs=pl.BlockSpec((tm, D_MODEL), lambda b, s: (b * ns + s, 0)),
        out_shape=jax.ShapeDtypeStruct((t, D_MODEL), F32),
        scratch_shapes=[
            pltpu.VMEM((ATTN_BLOCK + tm, N_KV_HEADS * HEAD_PAD), BF16),
            pltpu.VMEM((N_KV_HEADS * HEAD_PAD, ATTN_BLOCK + tm), BF16),
            pltpu.VMEM((4, POOL_TAIL + tm, POOL_DIM), F32),
            pltpu.VMEM((Q_DIM, tm), F32),
        ],
        compiler_params=pltpu.CompilerParams(
            dimension_semantics=("arbitrary", "arbitrary"), vmem_limit_bytes=VMEM_LIMIT),
        name="mixer",
    )(*operands)


def _ffn_pre_kernel(x1_ref, gffn_ref, wrh_ref, wrl_ref, rb_ref, tri_ref,
                    h2p_ref, idx_ref, wt_ref, rk_ref, cnt_ref, run_ref):
    i = pl.program_id(0)
    tm = x1_ref.shape[0]

    @pl.when(i == 0)
    def _():
        run_ref[...] = jnp.zeros_like(run_ref)

    h2 = _rms(x1_ref[...], gffn_ref[...])
    hb = h2.astype(BF16)
    h2p_ref[...] = _pack(h2[:, 0:HALF], h2[:, HALF:D_MODEL])

    lo = (h2 - hb.astype(F32)).astype(BF16)
    wrh = wrh_ref[...]
    logits = (lax.dot_general(wrh, hb, _NT, preferred_element_type=F32)
              + lax.dot_general(wrl_ref[...], hb, _NT, preferred_element_type=F32)
              + lax.dot_general(wrh, lo, _NT, preferred_element_type=F32))
    scores = _sigmoid(logits)
    choice = scores + rb_ref[...]

    neg = -jnp.inf
    sub8 = lax.broadcasted_iota(I32, (GROUP_SIZE, tm), 0).astype(F32)
    grp_rows = []
    for gi in range(N_EXPERT_GROUPS):
        cg = choice[GROUP_SIZE * gi:GROUP_SIZE * (gi + 1)]
        m1 = jnp.max(cg, axis=0, keepdims=True)
        first = jnp.min(jnp.where(cg == m1, sub8, float(GROUP_SIZE)), axis=0, keepdims=True)
        m2 = jnp.max(jnp.where(sub8 == first, neg, cg), axis=0, keepdims=True)
        grp_rows.append(m1 + m2)
    gs = jnp.concatenate(grp_rows, axis=0)
    beaten = jnp.zeros((N_EXPERT_GROUPS, tm), F32)
    for gi in range(N_EXPERT_GROUPS):
        row = grp_rows[gi]
        wins = (row > gs) | ((row == gs) & (sub8 > float(gi)))
        beaten = beaten + jnp.where(wins, 1.0, 0.0)
    gsel = beaten < float(TOPK_GROUPS)
    cm = jnp.concatenate(
        [jnp.where(gsel[gi:gi + 1], choice[GROUP_SIZE * gi:GROUP_SIZE * (gi + 1)], neg)
         for gi in range(N_EXPERT_GROUPS)], axis=0)

    iota_e = lax.broadcasted_iota(I32, (N_EXPERTS, tm), 0).astype(F32)
    idx_rows, s_rows = [], []
    sel = jnp.zeros((N_EXPERTS, tm), F32)
    for _ in range(TOP_K):
        m = jnp.max(cm, axis=0, keepdims=True)
        idx = jnp.min(jnp.where(cm == m, iota_e, float(N_EXPERTS)), axis=0, keepdims=True)
        oh = iota_e == idx
        s_rows.append(jnp.sum(jnp.where(oh, scores, 0.0), axis=0, keepdims=True))
        idx_rows.append(idx)
        cm = jnp.where(oh, neg, cm)
        sel = sel + jnp.where(oh, 1.0, 0.0)

    run = run_ref[:, 0:1]
    cum = _dot(sel.astype(BF16), tri_ref[...])
    before = run + cum - sel
    rk_rows = [jnp.sum(jnp.where(iota_e == idx, before, 0.0), axis=0, keepdims=True) for idx in idx_rows]
    new_run = run + jnp.sum(sel, axis=1, keepdims=True)
    run_ref[...] = jnp.broadcast_to(new_run, run_ref.shape)
    cnt_ref[...] = jnp.broadcast_to(new_run, cnt_ref.shape)

    ssum = s_rows[0]
    for r in range(1, TOP_K):
        ssum = ssum + s_rows[r]
    denom = ssum + 1e-20
    idx_ref[...] = jnp.concatenate(idx_rows, axis=0).astype(I32)
    wt_ref[...] = jnp.concatenate([sr / denom * ROUTED_SCALE for sr in s_rows], axis=0)
    rk_ref[...] = jnp.concatenate(rk_rows, axis=0).astype(I32)


def _ffn_pre(x1, g_ffn, w_router, router_bias):
    t = x1.shape[0]
    tm = TM_FFN
    wr_t = w_router.astype(F32).T
    wr_hi = wr_t.astype(BF16)
    wr_lo = (wr_t - wr_hi.astype(F32)).astype(BF16)
    tri = (np.arange(tm)[:, None] <= np.arange(tm)[None, :]).astype(np.float32)
    operands = (x1, g_ffn.reshape(1, D_MODEL), wr_hi, wr_lo,
                router_bias.astype(F32).reshape(N_EXPERTS, 1), jnp.asarray(tri, BF16))
    in_specs = [pl.BlockSpec((tm, D_MODEL), lambda i: (i, 0))]
    in_specs += [_const_spec(op.shape) for op in operands[1:]]
    row8 = pl.BlockSpec((TOP_K, tm), lambda i: (0, i))
    return pl.pallas_call(
        _ffn_pre_kernel,
        grid=(t // tm,),
        in_specs=in_specs,
        out_specs=[pl.BlockSpec((tm, HALF), lambda i: (i, 0)),
                   row8, row8, row8,
                   pl.BlockSpec((N_EXPERTS, 128), lambda i: (0, 0))],
        out_shape=[jax.ShapeDtypeStruct((t, HALF), I32),
                   jax.ShapeDtypeStruct((TOP_K, t), I32),
                   jax.ShapeDtypeStruct((TOP_K, t), F32),
                   jax.ShapeDtypeStruct((TOP_K, t), I32),
                   jax.ShapeDtypeStruct((N_EXPERTS, 128), F32)],
        scratch_shapes=[pltpu.VMEM((N_EXPERTS, 128), F32)],
        compiler_params=pltpu.CompilerParams(
            dimension_semantics=("arbitrary",), vmem_limit_bytes=VMEM_LIMIT),
        name="ffn_pre",
    )(*operands)


def _swiglu_packed(x_packed, wg, wu, wd):
    lo, hi = _unpack(x_packed)
    lo = lo.astype(BF16)
    hi = hi.astype(BF16)
    g = _dot(lo, wg[0:HALF, :]) + _dot(hi, wg[HALF:D_MODEL, :])
    u = _dot(lo, wu[0:HALF, :]) + _dot(hi, wu[HALF:D_MODEL, :])
    a = (g * _sigmoid(g) * u).astype(BF16)
    y = _dot(a, wd[...])
    return _pack(y[:, 0:HALF], y[:, HALF:D_MODEL])


def _shared_kernel(h2p_ref, wg_ref, wu_ref, wd_ref, ysh_ref):
    ysh_ref[...] = _swiglu_packed(h2p_ref[...], wg_ref, wu_ref, wd_ref)


def _shared(h2p, w_sh_gate, w_sh_up, w_sh_down):
    t = h2p.shape[0]
    tm = TM_FFN
    operands = (h2p, w_sh_gate.astype(BF16), w_sh_up.astype(BF16), w_sh_down.astype(BF16))
    in_specs = [pl.BlockSpec((tm, HALF), lambda i: (i, 0))]
    in_specs += [_const_spec(op.shape) for op in operands[1:]]
    return pl.pallas_call(
        _shared_kernel,
        grid=(t // tm,),
        in_specs=in_specs,
        out_specs=pl.BlockSpec((tm, HALF), lambda i: (i, 0)),
        out_shape=jax.ShapeDtypeStruct((t, HALF), I32),
        compiler_params=pltpu.CompilerParams(
            dimension_semantics=("arbitrary",), vmem_limit_bytes=VMEM_LIMIT),
        name="shared",
    )(*operands)


def _positions_kernel(pstart_ref, idx_ref, rk_ref, pos_ref):
    idx = idx_ref[...]
    acc = rk_ref[...]
    for e in range(N_EXPERTS):
        acc = acc + jnp.where(idx == e, pstart_ref[e], 0)
    pos_ref[...] = acc


def _positions(pstart, idx8, rk8):
    t = idx8.shape[1]
    tm = TM_POS
    spec = pl.BlockSpec((TOP_K, tm), lambda i, ps: (0, i))
    return pl.pallas_call(
        _positions_kernel,
        grid_spec=pltpu.PrefetchScalarGridSpec(num_scalar_prefetch=1, grid=(t // tm,),
                                               in_specs=[spec, spec], out_specs=spec),
        out_shape=jax.ShapeDtypeStruct((TOP_K, t), I32),
        name="positions",
    )(pstart, idx8, rk8)


def _sc_workers():
    info = plsc.get_sparse_core_info()
    return info.num_cores, info.num_cores * info.num_subcores


def _dispatch(pos, h2p, n_rows):
    t = h2p.shape[0]
    ch = SC_SCATTER_ROWS
    n_cores, n_workers = _sc_workers()
    n_ch = t // (n_workers * ch)
    pos3 = pos.reshape(TOP_K, t // ch, ch).transpose(1, 0, 2)

    def body(pos_hbm, h_hbm, xs_hbm, idx_v, rows_v, sem):
        wid = lax.axis_index("s") * n_cores + lax.axis_index("c")

        @pl.loop(0, n_ch)
        def _(c):
            chunk = wid * n_ch + c
            pltpu.sync_copy(h_hbm.at[pl.ds(chunk * ch, ch)], rows_v)
            pltpu.sync_copy(pos_hbm.at[chunk], idx_v)
            copies = [pltpu.async_copy(rows_v, xs_hbm.at[idx_v.at[k]], sem) for k in range(TOP_K)]
            for cp in copies:
                cp.wait()

    return pl.kernel(
        body,
        out_type=jax.ShapeDtypeStruct((n_rows, HALF), I32),
        mesh=plsc.VectorSubcoreMesh(core_axis_name="c", subcore_axis_name="s"),
        scratch_types=[pltpu.VMEM((TOP_K, ch), I32), pltpu.VMEM((ch, HALF), I32),
                       pltpu.SemaphoreType.DMA],
        name="dispatch",
    )(pos3, h2p)


def _gather_back(pos, ys):
    t = pos.shape[1]
    ch = SC_GATHER_ROWS
    n_cores, n_workers = _sc_workers()
    n_ch = t // (n_workers * ch)
    posg = pos.reshape(TOP_K, t // ch, ch).transpose(1, 0, 2).reshape(-1)

    def body(pos_hbm, ys_hbm, yt_hbm, idx_v, buf0, buf1, gsem, wsem0, wsem1):
        wid = lax.axis_index("s") * n_cores + lax.axis_index("c")
        bufs = (buf0, buf1)
        wsems = (wsem0, wsem1)

        @pl.loop(0, n_ch)
        def _(c):
            chunk = wid * n_ch + c
            pltpu.sync_copy(pos_hbm.at[pl.ds(chunk * (TOP_K * ch), TOP_K * ch)], idx_v)
            writes = [None, None]
            for k in range(TOP_K):
                b = k % 2
                if writes[b] is not None:
                    writes[b].wait()
                pltpu.async_copy(ys_hbm.at[idx_v.at[pl.ds(k * ch, ch)]], bufs[b], gsem).wait()
                writes[b] = pltpu.async_copy(bufs[b], yt_hbm.at[k, pl.ds(chunk * ch, ch)], wsems[b])
            writes[0].wait()
            writes[1].wait()

    return pl.kernel(
        body,
        out_type=jax.ShapeDtypeStruct((TOP_K, t, HALF), I32),
        mesh=plsc.VectorSubcoreMesh(core_axis_name="c", subcore_axis_name="s"),
        scratch_types=[pltpu.VMEM((TOP_K * ch,), I32), pltpu.VMEM((ch, HALF), I32),
                       pltpu.VMEM((ch, HALF), I32), pltpu.SemaphoreType.DMA,
                       pltpu.SemaphoreType.DMA, pltpu.SemaphoreType.DMA],
        name="gather_back",
    )(posg, ys)


def _expert_kernel(nblk_ref, blk0_ref, cnt_ref, wg_ref, wu_ref, wd_ref, xs_hbm, after_ref, ys_hbm,
                   wg_s, wu_s, wd_s, xbuf, ybuf, in_sem, out_sem):
    del after_ref
    e = pl.program_id(0)
    nb = nblk_ref[e]
    b0 = blk0_ref[e]
    cnt = cnt_ref[e]
    n_all = blk0_ref[N_EXPERTS]
    ahead = EXPERT_SLOTS - 1
    wg_s[...] = wg_ref[0].astype(BF16)
    wu_s[...] = wu_ref[0].astype(BF16)
    wd_s[...] = wd_ref[0].astype(BF16)

    def slot_of(g):
        return g & (EXPERT_SLOTS - 1)

    def rows_of(g):
        return pl.ds(pl.multiple_of(g * MOE_BLOCK, MOE_BLOCK), MOE_BLOCK)

    def in_copy(g):
        return pltpu.make_async_copy(xs_hbm.at[rows_of(g)], xbuf.at[slot_of(g)], in_sem.at[slot_of(g)])

    def out_copy(g):
        return pltpu.make_async_copy(ybuf.at[slot_of(g)], ys_hbm.at[rows_of(g)], out_sem.at[slot_of(g)])

    @pl.when(e == 0)
    def _():
        for g in range(ahead):
            @pl.when(g < n_all)
            def _():
                in_copy(g).start()

    def one_block(j, carry):
        g = b0 + j
        in_copy(g).wait()

        @pl.when(g + ahead < n_all)
        def _():
            in_copy(g + ahead).start()

        @pl.when(g >= EXPERT_SLOTS)
        def _():
            out_copy(g - EXPERT_SLOTS).wait()

        rows = lax.broadcasted_iota(I32, (MOE_BLOCK, HALF), 0)
        x = jnp.where(rows < cnt - j * MOE_BLOCK, xbuf[slot_of(g)], 0)
        ybuf[slot_of(g)] = _swiglu_packed(x, wg_s, wu_s, wd_s)
        out_copy(g).start()
        return carry

    lax.fori_loop(0, nb, one_block, 0)

    @pl.when(e == N_EXPERTS - 1)
    def _():
        for back in range(1, EXPERT_SLOTS + 1):
            @pl.when(n_all >= back)
            def _():
                out_copy(n_all - back).wait()


def _experts(n_blk, blk0, counts, n_rows, xs, w_gate, w_up, w_down, after):
    wspec_in = pl.BlockSpec((1, D_MODEL, EXPERT_DIM), lambda e, *_: (e, 0, 0))
    grid_spec = pltpu.PrefetchScalarGridSpec(
        num_scalar_prefetch=3,
        grid=(N_EXPERTS,),
        in_specs=[wspec_in, wspec_in,
                  pl.BlockSpec((1, EXPERT_DIM, D_MODEL), lambda e, *_: (e, 0, 0)),
                  pl.BlockSpec(memory_space=pl.ANY),
                  pl.BlockSpec(memory_space=pl.ANY)],
        out_specs=pl.BlockSpec(memory_space=pl.ANY),
        scratch_shapes=[pltpu.VMEM((D_MODEL, EXPERT_DIM), BF16), pltpu.VMEM((D_MODEL, EXPERT_DIM), BF16),
                        pltpu.VMEM((EXPERT_DIM, D_MODEL), BF16),
                        pltpu.VMEM((EXPERT_SLOTS, MOE_BLOCK, HALF), I32),
                        pltpu.VMEM((EXPERT_SLOTS, MOE_BLOCK, HALF), I32),
                        pltpu.SemaphoreType.DMA((EXPERT_SLOTS,)), pltpu.SemaphoreType.DMA((EXPERT_SLOTS,))],
    )
    return pl.pallas_call(
        _expert_kernel,
        grid_spec=grid_spec,
        out_shape=jax.ShapeDtypeStruct((n_rows, HALF), I32),
        compiler_params=pltpu.CompilerParams(
            dimension_semantics=("arbitrary",), vmem_limit_bytes=VMEM_LIMIT),
        name="experts",
    )(n_blk, blk0, counts, w_gate, w_up, w_down, xs, after)


def _combine_kernel(x1_ref, ysh_ref, yt_ref, wt_ref, p_ref, gple_ref, wpg_ref, wpp_ref, gpost_ref, *rest):
    out_ref = rest[-1]
    proj = _rms(_dot(p_ref[...].astype(BF16), wpp_ref[...]), gpost_ref[...])
    x1 = x1_ref[...]
    sh_lo, sh_hi = _unpack(ysh_ref[...])
    acc_lo = x1[:, 0:HALF] + sh_lo
    acc_hi = x1[:, HALF:D_MODEL] + sh_hi
    wt = wt_ref[...]
    for k in range(TOP_K):
        lo, hi = _unpack(yt_ref[k])
        wk = wt[:, k:k + 1]
        acc_lo = acc_lo + wk * lo
        acc_hi = acc_hi + wk * hi
    x2 = jnp.concatenate([acc_lo, acc_hi], axis=1)
    gate = _sigmoid(_dot(_rms(x2, gple_ref[...]).astype(BF16), wpg_ref[...]))
    out_ref[...] = x2 + gate * proj


def _combine(x1, ysh, yt, wt_t, p2d, g_ple, w_ple_gate, w_ple_proj, g_ple_post, chunk, prev_out):
    t = x1.shape[0]
    tm = TM_COMBINE
    steps = yt.shape[1] // tm
    off = chunk * steps
    operands = [x1, ysh, yt, wt_t, p2d, g_ple.reshape(1, D_MODEL), w_ple_gate.astype(BF16),
                w_ple_proj.astype(BF16), g_ple_post.reshape(1, D_MODEL)]
    in_specs = [pl.BlockSpec((tm, D_MODEL), lambda i: (off + i, 0)),
                pl.BlockSpec((tm, HALF), lambda i: (off + i, 0)),
                pl.BlockSpec((TOP_K, tm, HALF), lambda i: (0, i, 0)),
                pl.BlockSpec((tm, TOP_K), lambda i: (off + i, 0)),
                pl.BlockSpec((tm, PLE_DIM), lambda i: (off + i, 0)),
                _const_spec((1, D_MODEL)),
                _const_spec((D_MODEL, D_MODEL)),
                _const_spec((PLE_DIM, D_MODEL)),
                _const_spec((1, D_MODEL))]
    aliases = {}
    if prev_out is not None:
        aliases = {len(operands): 0}
        operands.append(prev_out)
        in_specs.append(pl.BlockSpec(memory_space=pl.ANY))
    return pl.pallas_call(
        _combine_kernel,
        grid=(steps,),
        in_specs=in_specs,
        out_specs=pl.BlockSpec((tm, D_MODEL), lambda i: (off + i, 0)),
        out_shape=jax.ShapeDtypeStruct((t, D_MODEL), F32),
        input_output_aliases=aliases,
        compiler_params=pltpu.CompilerParams(
            dimension_semantics=("arbitrary",), vmem_limit_bytes=VMEM_LIMIT),
        name="combine",
    )(*operands)


def _layer(x2d, p2d, batch, seq, g_mix, w_in, w_pool_mix, pool_scale, w_branch_a, q_gain, k_gain,
           attn_sinks, w_branch_b, w_gate, b_gate, w_out, g_ffn, w_router, router_bias, w_exp_gate,
           w_exp_up, w_exp_down, w_sh_gate, w_sh_up, w_sh_down, g_ple, w_ple_gate, w_ple_proj,
           g_ple_post):
    t = x2d.shape[0]
    x1 = _mixer(x2d, batch, seq, g_mix, w_in, w_pool_mix, pool_scale, w_branch_a, q_gain, k_gain,
                attn_sinks, w_branch_b, w_gate, b_gate, w_out)
    h2p, idx8, wt8, rk8, cnt = _ffn_pre(x1, g_ffn, w_router, router_bias)

    counts = cnt[:, 0].astype(I32)
    padded = (counts + MOE_BLOCK - 1) // MOE_BLOCK * MOE_BLOCK
    pend = jnp.cumsum(padded)
    pstart = pend - padded
    pos = _positions(pstart.astype(I32), idx8, rk8)
    n_rows = (-(-(t * TOP_K) // MOE_BLOCK) + N_EXPERTS) * MOE_BLOCK

    xs = _dispatch(pos, h2p, n_rows)
    ysh = _shared(h2p, w_sh_gate, w_sh_up, w_sh_down)
    blk0 = jnp.concatenate([pstart, pend[-1:]]) // MOE_BLOCK
    ys = _experts((padded // MOE_BLOCK).astype(I32), blk0.astype(I32), counts, n_rows,
                  xs, w_exp_gate, w_exp_up, w_exp_down, ysh)
    wt_t = wt8.T
    tc = t // COMBINE_CHUNKS
    out = None
    for c in range(COMBINE_CHUNKS):
        yt = _gather_back(pos[:, c * tc:(c + 1) * tc], ys)
        out = _combine(x1, ysh, yt, wt_t, p2d, g_ple, w_ple_gate, w_ple_proj, g_ple_post, c, out)
    return out


def kernel(x, p, g_mix, w_in, w_pool_mix, pool_scale, w_branch_a, q_gain, k_gain, attn_sinks,
           w_branch_b, w_gate, b_gate, w_out, g_ffn, w_router, router_bias, w_exp_gate, w_exp_up,
           w_exp_down, w_sh_gate, w_sh_up, w_sh_down, g_ple, w_ple_gate, w_ple_proj, g_ple_post):
    batch, seq, d = x.shape
    depth = p.shape[0]
    x2d = x.reshape(batch * seq, d)
    for i in range(depth):
        x2d = _layer(x2d, p[i].reshape(batch * seq, PLE_DIM), batch, seq, g_mix[i], w_in[i],
                     w_pool_mix[i], pool_scale[i], w_branch_a[i], q_gain[i], k_gain[i], attn_sinks[i],
                     w_branch_b[i], w_gate[i], b_gate[i], w_out[i], g_ffn[i], w_router[i],
                     router_bias[i], w_exp_gate[i], w_exp_up[i], w_exp_down[i], w_sh_gate[i],
                     w_sh_up[i], w_sh_down[i], g_ple[i], w_ple_gate[i], w_ple_proj[i], g_ple_post[i])
    return x2d.reshape(batch, seq, d)
```

```python
import functools

import numpy as np
import jax
import jax.numpy as jnp
from jax import lax
from jax.experimental import pallas as pl
from jax.experimental.pallas import tpu as pltpu
from jax.experimental.pallas import tpu_sc as plsc

F32 = jnp.float32
BF16 = jnp.bfloat16
I32 = jnp.int32

D_MODEL = 1024
PLE_DIM = 256
POOL_WINDOWS = (2, 4, 8, 16)
POOL_GROUP_DIM = 128
POOL_DIM = 512
N_HEADS = 8
N_KV_HEADS = 2
Q_GROUP = 4
HEAD_DIM = 64
HEAD_PAD = 128
Q_DIM = 512
KV_DIM = 128
ATTN_BLOCK = 128
N_EXPERTS = 64
N_EXPERT_GROUPS = 8
GROUP_SIZE = 8
TOPK_GROUPS = 4
TOP_K = 8
EXPERT_DIM = 256
ROUTED_SCALE = 2.5
MOE_BLOCK = 512
EXPERT_SLOTS = 4
EPS = 1e-6
HALF = D_MODEL // 2
MASKED = -1e30

COL_Q = POOL_DIM
COL_K = COL_Q + N_HEADS * HEAD_PAD
COL_V = COL_K + N_KV_HEADS * HEAD_PAD
IN_PAD = COL_V + N_KV_HEADS * HEAD_PAD
HEAD_ORDER = (0, 2, 1, 3)

TM_MIX = 512
TM_FFN = 512
TM_COMBINE = 512
TM_POS = 4096
COMBINE_CHUNKS = 2
PIPELINE_HALVES = 2
SC_SCATTER_ROWS = 128
SC_GATHER_ROWS = 64
POOL_TAIL = 8
VMEM_LIMIT = 56 * 1024 * 1024

_NT = (((1,), (1,)), ((), ()))


def _dot(a, b):
    return jnp.dot(a, b, preferred_element_type=F32)


def _rms(x, g):
    ms = jnp.mean(x * x, axis=-1, keepdims=True)
    return x * lax.rsqrt(ms + EPS) * g


def _sigmoid(x):
    return 0.5 * jnp.tanh(0.5 * x) + 0.5


def _pack(lo, hi):
    lo_bits = lax.bitcast_convert_type(lo.astype(BF16).astype(F32), I32)
    hi_bits = lax.bitcast_convert_type(hi.astype(BF16).astype(F32), I32)
    return (hi_bits & jnp.int32(-65536)) | lax.shift_right_logical(lo_bits, jnp.int32(16))


def _unpack(w):
    lo = lax.bitcast_convert_type(lax.shift_left(w, jnp.int32(16)), F32)
    hi = lax.bitcast_convert_type(w & jnp.int32(-65536), F32)
    return lo, hi


def _mixer_kernel(x_ref, gmix_ref, win_ref, qg_ref, kg_ref, bias_ref, sink_ref, ones_ref, wmix_ref,
                  pscale_ref, wa_ref, wb_ref, wg_ref, bg_ref, wout_ref, *rest):
    x1_ref, kbuf, vt_buf, pbuf, zbt_buf = rest[-5:]
    s = pl.program_id(1)
    tm = x_ref.shape[0]

    @pl.when(s == 0)
    def _():
        kbuf[0:ATTN_BLOCK, :] = jnp.zeros((ATTN_BLOCK, kbuf.shape[1]), BF16)
        vt_buf[:, 0:ATTN_BLOCK] = jnp.zeros((vt_buf.shape[0], ATTN_BLOCK), BF16)
        pbuf[:, 0:POOL_TAIL, :] = jnp.zeros((pbuf.shape[0], POOL_TAIL, POOL_DIM), F32)

    x = x_ref[...]
    hb = _rms(x, gmix_ref[...]).astype(BF16)
    u = _dot(hb, win_ref[...])

    up = u[:, 0:POOL_DIM]
    t0 = POOL_TAIL
    pbuf[0, t0:t0 + tm, :] = up
    s2 = up + pbuf[0, t0 - 1:t0 - 1 + tm, :]
    pbuf[1, t0:t0 + tm, 128:512] = s2[:, 128:512]
    s4 = s2[:, 128:512] + pbuf[1, t0 - 2:t0 - 2 + tm, 128:512]
    pbuf[2, t0:t0 + tm, 256:512] = s4[:, 128:384]
    s8 = s4[:, 128:384] + pbuf[2, t0 - 4:t0 - 4 + tm, 256:512]
    pbuf[3, t0:t0 + tm, 384:512] = s8[:, 128:256]
    s16 = s8[:, 128:256] + pbuf[3, t0 - 8:t0 - 8 + tm, 384:512]
    for lvl in range(4):
        pbuf[lvl, 0:POOL_TAIL, 128 * lvl:512] = pbuf[lvl, tm:tm + POOL_TAIL, 128 * lvl:512]
    wsums = (s2[:, 0:128], s4[:, 0:128], s8[:, 0:128], s16)
    tpos = (s * tm).astype(F32) + lax.broadcasted_iota(I32, (tm, 1), 0).astype(F32)
    za_parts = []
    for g, w in enumerate(POOL_WINDOWS):
        inv_count = 1.0 / jnp.minimum(tpos + 1.0, float(w))
        pooled = wsums[g] * inv_count - up[:, 128 * g:128 * (g + 1)]
        mixed = _dot(pooled.astype(BF16), wmix_ref[g]) * pscale_ref[:, 128 * g:128 * (g + 1)]
        za_parts.append(mixed.astype(BF16))
    za = jnp.concatenate(za_parts, axis=1)

    ones2 = ones_ref[...]
    q = u[:, COL_Q:COL_K]
    q2 = (q * q).astype(BF16)
    qss = jnp.concatenate([_dot(q2[:, 256 * c:256 * (c + 1)], ones2) for c in range(N_HEADS // 2)], axis=1)
    qn = (q * lax.rsqrt(qss * (1.0 / HEAD_DIM) + EPS) * qg_ref[...]).astype(BF16)
    k = u[:, COL_K:COL_V]
    kss = _dot((k * k).astype(BF16), ones2)
    kbuf[ATTN_BLOCK:ATTN_BLOCK + tm, :] = (k * lax.rsqrt(kss * (1.0 / HEAD_DIM) + EPS) * kg_ref[...]).astype(BF16)
    vt_buf[:, ATTN_BLOCK:ATTN_BLOCK + tm] = u[:, COL_V:IN_PAD].T.astype(BF16)

    nq = Q_GROUP * ATTN_BLOCK
    key_j = lax.broadcasted_iota(I32, (ATTN_BLOCK, nq), 0)
    qry_i = lax.broadcasted_iota(I32, (ATTN_BLOCK, nq), 1) & (ATTN_BLOCK - 1)
    from_prev = key_j > qry_i
    first = jnp.where(s == 0, 1, 0)
    for n in range(tm // ATTN_BLOCK):
        r0 = ATTN_BLOCK * n
        for kv in range(N_KV_HEADS):
            c0 = HEAD_PAD * kv
            qs = jnp.concatenate(
                [qn[r0:r0 + ATTN_BLOCK, HEAD_PAD * (kv * Q_GROUP + g):HEAD_PAD * (kv * Q_GROUP + g + 1)]
                 for g in HEAD_ORDER], axis=0)
            kk = kbuf[r0:r0 + 2 * ATTN_BLOCK, c0:c0 + HEAD_PAD]
            st = lax.dot_general(kk, qs, _NT, preferred_element_type=F32)
            sc = jnp.where(from_prev, st[0:ATTN_BLOCK], st[ATTN_BLOCK:2 * ATTN_BLOCK])
            sc = sc + (bias_ref[first, kv] if n == 0 else bias_ref[0, kv])
            sink = sink_ref[kv]
            m = jnp.maximum(jnp.max(sc, axis=0, keepdims=True), sink)
            e = jnp.exp(sc - m)
            inv_den = 1.0 / (jnp.sum(e, axis=0, keepdims=True) + jnp.exp(sink - m))
            pt = jnp.concatenate([jnp.where(from_prev, e, 0.0), jnp.where(from_prev, 0.0, e)],
                                 axis=0).astype(BF16)
            v_even = vt_buf[c0:c0 + HEAD_PAD, r0:r0 + 2 * ATTN_BLOCK]
            v_odd = jnp.concatenate([v_even[HEAD_DIM:HEAD_PAD], v_even[0:HEAD_DIM]], axis=0)
            half = 2 * ATTN_BLOCK
            o = (_dot(v_even, pt[:, 0:half]) * inv_den[:, 0:half]
                 + _dot(v_odd, pt[:, half:2 * half]) * inv_den[:, half:2 * half])
            ch = kv * Q_GROUP * HEAD_DIM
            zbt_buf[ch:ch + HEAD_PAD, r0:r0 + ATTN_BLOCK] = o[:, 0:ATTN_BLOCK]
            zbt_buf[ch + HEAD_PAD:ch + 2 * HEAD_PAD, r0:r0 + ATTN_BLOCK] = o[:, ATTN_BLOCK:half]
    kbuf[0:ATTN_BLOCK, :] = kbuf[tm:tm + ATTN_BLOCK, :]
    vt_buf[:, 0:ATTN_BLOCK] = vt_buf[:, tm:tm + ATTN_BLOCK]

    y_a = _dot(za, wa_ref[...])
    y_b = _dot(zbt_buf[...].T.astype(BF16), wb_ref[...])
    g_a = _sigmoid(_dot(hb, wg_ref[:, 0:D_MODEL]) + bg_ref[:, 0:D_MODEL])
    merged = g_a * y_a
    g_b = _sigmoid(_dot(hb, wg_ref[:, D_MODEL:2 * D_MODEL]) + bg_ref[:, D_MODEL:2 * D_MODEL])
    merged = (merged + g_b * y_b).astype(BF16)
    x1_ref[...] = x + _dot(merged, wout_ref[...])


def _attn_tables(attn_sinks):
    slopes = 2.0 ** (-8.0 * (np.arange(N_HEADS) + 1) / N_HEADS)
    j = np.arange(ATTN_BLOCK)[:, None]
    i = np.arange(ATTN_BLOCK)[None, :]
    from_prev = j > i
    dist = np.where(from_prev, ATTN_BLOCK + i - j, i - j)
    bias = np.empty((2, N_KV_HEADS, ATTN_BLOCK, Q_GROUP * ATTN_BLOCK), np.float32)
    for first in range(2):
        ok = ~from_prev if first else np.ones_like(from_prev)
        for kv in range(N_KV_HEADS):
            for slot, g in enumerate(HEAD_ORDER):
                sl = np.float32(slopes[kv * Q_GROUP + g])
                val = -(sl * dist.astype(np.float32))
                bias[first, kv, :, slot * ATTN_BLOCK:(slot + 1) * ATTN_BLOCK] = np.where(ok, val, MASKED)
    sink = attn_sinks.astype(F32).reshape(N_KV_HEADS, Q_GROUP)[:, np.asarray(HEAD_ORDER)]
    sink = jnp.repeat(sink, ATTN_BLOCK, axis=1)
    return jnp.asarray(bias), sink.reshape(N_KV_HEADS, 1, Q_GROUP * ATTN_BLOCK)


def _pad_heads(w, n_heads):
    k = w.shape[0]
    w = w.reshape(k, n_heads, HEAD_DIM)
    w = jnp.pad(w, ((0, 0), (0, 0), (0, HEAD_PAD - HEAD_DIM)))
    return w.reshape(k, n_heads * HEAD_PAD)


def _const_spec(shape):
    nd = len(shape)
    return pl.BlockSpec(shape, lambda *_: (0,) * nd)


def _mixer(x2d, batch0, batch, seq, g_mix, w_in, w_pool_mix, pool_scale, w_branch_a, q_gain, k_gain,
           attn_sinks, w_branch_b, w_gate, b_gate, w_out, after=None):
    t = batch * seq
    tm = TM_MIX
    ns = seq // tm
    w_q = w_in[:, POOL_DIM:POOL_DIM + Q_DIM]
    w_k = w_in[:, POOL_DIM + Q_DIM:POOL_DIM + Q_DIM + KV_DIM]
    w_v = w_in[:, POOL_DIM + Q_DIM + KV_DIM:]
    win_p = jnp.concatenate([w_in[:, :POOL_DIM], _pad_heads(w_q, N_HEADS), _pad_heads(w_k, N_KV_HEADS),
                             _pad_heads(w_v, N_KV_HEADS)], axis=1).astype(BF16)
    qg = jnp.tile(jnp.pad(q_gain * (HEAD_DIM ** -0.5), (0, HEAD_PAD - HEAD_DIM)), N_HEADS).reshape(1, -1)
    kg = jnp.tile(jnp.pad(k_gain, (0, HEAD_PAD - HEAD_DIM)), N_KV_HEADS).reshape(1, -1)
    bias, sink = _attn_tables(attn_sinks)
    ones2 = jnp.asarray(np.kron(np.eye(2), np.ones((HEAD_PAD, HEAD_PAD))), BF16)
    operands = (x2d, g_mix.reshape(1, D_MODEL), win_p, qg, kg, bias, sink, ones2, w_pool_mix.astype(BF16),
                pool_scale.reshape(1, POOL_DIM), w_branch_a.astype(BF16), w_branch_b.astype(BF16),
                w_gate.astype(BF16), b_gate.reshape(1, 2 * D_MODEL), w_out.astype(BF16))
    in_specs = [pl.BlockSpec((tm, D_MODEL), lambda b, s: ((batch0 + b) * ns + s, 0))]
    in_specs += [_const_spec(op.shape) for op in operands[1:]]
    if after is not None:
        operands += (after,)
        in_specs.append(pl.BlockSpec(memory_space=pl.ANY))
    return pl.pallas_call(
        _mixer_kernel,
        grid=(batch, ns),
        in_specs=in_specs,
        out_specs=pl.BlockSpec((tm, D_MODEL), lambda b, s: (b * ns + s, 0)),
        out_shape=jax.ShapeDtypeStruct((t, D_MODEL), F32),
        scratch_shapes=[
            pltpu.VMEM((ATTN_BLOCK + tm, N_KV_HEADS * HEAD_PAD), BF16),
            pltpu.VMEM((N_KV_HEADS * HEAD_PAD, ATTN_BLOCK + tm), BF16),
            pltpu.VMEM((4, POOL_TAIL + tm, POOL_DIM), F32),
            pltpu.VMEM((Q_DIM, tm), F32),
        ],
        compiler_params=pltpu.CompilerParams(
            dimension_semantics=("arbitrary", "arbitrary"), vmem_limit_bytes=VMEM_LIMIT),
        name="mixer",
    )(*operands)


def _ffn_pre_kernel(x1_ref, gffn_ref, wrh_ref, wrl_ref, rb_ref, tri_ref,
                    h2p_ref, idx_ref, wt_ref, rk_ref, cnt_ref, run_ref):
    i = pl.program_id(0)
    tm = x1_ref.shape[0]

    @pl.when(i == 0)
    def _():
        run_ref[...] = jnp.zeros_like(run_ref)

    h2 = _rms(x1_ref[...], gffn_ref[...])
    hb = h2.astype(BF16)
    h2p_ref[...] = _pack(h2[:, 0:HALF], h2[:, HALF:D_MODEL])

    lo = (h2 - hb.astype(F32)).astype(BF16)
    wrh = wrh_ref[...]
    logits = (lax.dot_general(wrh, hb, _NT, preferred_element_type=F32)
              + lax.dot_general(wrl_ref[...], hb, _NT, preferred_element_type=F32)
              + lax.dot_general(wrh, lo, _NT, preferred_element_type=F32))
    scores = _sigmoid(logits)
    choice = scores + rb_ref[...]

    neg = -jnp.inf
    sub8 = lax.broadcasted_iota(I32, (GROUP_SIZE, tm), 0).astype(F32)
    grp_rows = []
    for gi in range(N_EXPERT_GROUPS):
        cg = choice[GROUP_SIZE * gi:GROUP_SIZE * (gi + 1)]
        m1 = jnp.max(cg, axis=0, keepdims=True)
        first = jnp.min(jnp.where(cg == m1, sub8, float(GROUP_SIZE)), axis=0, keepdims=True)
        m2 = jnp.max(jnp.where(sub8 == first, neg, cg), axis=0, keepdims=True)
        grp_rows.append(m1 + m2)
    gs = jnp.concatenate(grp_rows, axis=0)
    beaten = jnp.zeros((N_EXPERT_GROUPS, tm), F32)
    for gi in range(N_EXPERT_GROUPS):
        row = grp_rows[gi]
        wins = (row > gs) | ((row == gs) & (sub8 > float(gi)))
        beaten = beaten + jnp.where(wins, 1.0, 0.0)
    gsel = beaten < float(TOPK_GROUPS)
    cm = jnp.concatenate(
        [jnp.where(gsel[gi:gi + 1], choice[GROUP_SIZE * gi:GROUP_SIZE * (gi + 1)], neg)
         for gi in range(N_EXPERT_GROUPS)], axis=0)

    iota_e = lax.broadcasted_iota(I32, (N_EXPERTS, tm), 0).astype(F32)
    idx_rows, s_rows = [], []
    sel = jnp.zeros((N_EXPERTS, tm), F32)
    for _ in range(TOP_K):
        m = jnp.max(cm, axis=0, keepdims=True)
        idx = jnp.min(jnp.where(cm == m, iota_e, float(N_EXPERTS)), axis=0, keepdims=True)
        oh = iota_e == idx
        s_rows.append(jnp.sum(jnp.where(oh, scores, 0.0), axis=0, keepdims=True))
        idx_rows.append(idx)
        cm = jnp.where(oh, neg, cm)
        sel = sel + jnp.where(oh, 1.0, 0.0)

    run = run_ref[:, 0:1]
    cum = _dot(sel.astype(BF16), tri_ref[...])
    before = run + cum - sel
    rk_rows = [jnp.sum(jnp.where(iota_e == idx, before, 0.0), axis=0, keepdims=True) for idx in idx_rows]
    new_run = run + jnp.sum(sel, axis=1, keepdims=True)
    run_ref[...] = jnp.broadcast_to(new_run, run_ref.shape)
    cnt_ref[...] = jnp.broadcast_to(new_run, cnt_ref.shape)

    ssum = s_rows[0]
    for r in range(1, TOP_K):
        ssum = ssum + s_rows[r]
    denom = ssum + 1e-20
    idx_ref[...] = jnp.concatenate(idx_rows, axis=0).astype(I32)
    wt_ref[...] = jnp.concatenate([sr / denom * ROUTED_SCALE for sr in s_rows], axis=0)
    rk_ref[...] = jnp.concatenate(rk_rows, axis=0).astype(I32)


def _ffn_pre(x1, g_ffn, w_router, router_bias):
    t = x1.shape[0]
    tm = TM_FFN
    wr_t = w_router.astype(F32).T
    wr_hi = wr_t.astype(BF16)
    wr_lo = (wr_t - wr_hi.astype(F32)).astype(BF16)
    tri = (np.arange(tm)[:, None] <= np.arange(tm)[None, :]).astype(np.float32)
    operands = (x1, g_ffn.reshape(1, D_MODEL), wr_hi, wr_lo,
                router_bias.astype(F32).reshape(N_EXPERTS, 1), jnp.asarray(tri, BF16))
    in_specs = [pl.BlockSpec((tm, D_MODEL), lambda i: (i, 0))]
    in_specs += [_const_spec(op.shape) for op in operands[1:]]
    row8 = pl.BlockSpec((TOP_K, tm), lambda i: (0, i))
    return pl.pallas_call(
        _ffn_pre_kernel,
        grid=(t // tm,),
        in_specs=in_specs,
        out_specs=[pl.BlockSpec((tm, HALF), lambda i: (i, 0)),
                   row8, row8, row8,
                   pl.BlockSpec((N_EXPERTS, 128), lambda i: (0, 0))],
        out_shape=[jax.ShapeDtypeStruct((t, HALF), I32),
                   jax.ShapeDtypeStruct((TOP_K, t), I32),
                   jax.ShapeDtypeStruct((TOP_K, t), F32),
                   jax.ShapeDtypeStruct((TOP_K, t), I32),
                   jax.ShapeDtypeStruct((N_EXPERTS, 128), F32)],
        scratch_shapes=[pltpu.VMEM((N_EXPERTS, 128), F32)],
        compiler_params=pltpu.CompilerParams(
            dimension_semantics=("arbitrary",), vmem_limit_bytes=VMEM_LIMIT),
        name="ffn_pre",
    )(*operands)


def _swiglu_packed(x_packed, wg, wu, wd):
    lo, hi = _unpack(x_packed)
    lo = lo.astype(BF16)
    hi = hi.astype(BF16)
    g = _dot(lo, wg[0:HALF, :]) + _dot(hi, wg[HALF:D_MODEL, :])
    u = _dot(lo, wu[0:HALF, :]) + _dot(hi, wu[HALF:D_MODEL, :])
    a = (g * _sigmoid(g) * u).astype(BF16)
    y = _dot(a, wd[...])
    return _pack(y[:, 0:HALF], y[:, HALF:D_MODEL])


def _shared_kernel(h2p_ref, wg_ref, wu_ref, wd_ref, ysh_ref):
    ysh_ref[...] = _swiglu_packed(h2p_ref[...], wg_ref, wu_ref, wd_ref)


def _shared(h2p, w_sh_gate, w_sh_up, w_sh_down):
    t = h2p.shape[0]
    tm = TM_FFN
    operands = (h2p, w_sh_gate.astype(BF16), w_sh_up.astype(BF16), w_sh_down.astype(BF16))
    in_specs = [pl.BlockSpec((tm, HALF), lambda i: (i, 0))]
    in_specs += [_const_spec(op.shape) for op in operands[1:]]
    return pl.pallas_call(
        _shared_kernel,
        grid=(t // tm,),
        in_specs=in_specs,
        out_specs=pl.BlockSpec((tm, HALF), lambda i: (i, 0)),
        out_shape=jax.ShapeDtypeStruct((t, HALF), I32),
        compiler_params=pltpu.CompilerParams(
            dimension_semantics=("arbitrary",), vmem_limit_bytes=VMEM_LIMIT),
        name="shared",
    )(*operands)


def _positions_kernel(pstart_ref, idx_ref, rk_ref, pos_ref):
    idx = idx_ref[...]
    acc = rk_ref[...]
    for e in range(N_EXPERTS):
        acc = acc + jnp.where(idx == e, pstart_ref[e], 0)
    pos_ref[...] = acc


def _positions(pstart, idx8, rk8):
    t = idx8.shape[1]
    tm = TM_POS
    spec = pl.BlockSpec((TOP_K, tm), lambda i, ps: (0, i))
    return pl.pallas_call(
        _positions_kernel,
        grid_spec=pltpu.PrefetchScalarGridSpec(num_scalar_prefetch=1, grid=(t // tm,),
                                               in_specs=[spec, spec], out_specs=spec),
        out_shape=jax.ShapeDtypeStruct((TOP_K, t), I32),
        name="positions",
    )(pstart, idx8, rk8)


def _sc_workers():
    info = plsc.get_sparse_core_info()
    return info.num_cores, info.num_cores * info.num_subcores


def _dispatch(pos, h2p, n_rows):
    t = h2p.shape[0]
    ch = SC_SCATTER_ROWS
    n_cores, n_workers = _sc_workers()
    n_ch = t // (n_workers * ch)
    pos3 = pos.reshape(TOP_K, t // ch, ch).transpose(1, 0, 2)

    def body(pos_hbm, h_hbm, xs_hbm, idx_v, rows_v, sem):
        wid = lax.axis_index("s") * n_cores + lax.axis_index("c")

        @pl.loop(0, n_ch)
        def _(c):
            chunk = wid * n_ch + c
            pltpu.sync_copy(h_hbm.at[pl.ds(chunk * ch, ch)], rows_v)
            pltpu.sync_copy(pos_hbm.at[chunk], idx_v)
            copies = [pltpu.async_copy(rows_v, xs_hbm.at[idx_v.at[k]], sem) for k in range(TOP_K)]
            for cp in copies:
                cp.wait()

    return pl.kernel(
        body,
        out_type=jax.ShapeDtypeStruct((n_rows, HALF), I32),
        mesh=plsc.VectorSubcoreMesh(core_axis_name="c", subcore_axis_name="s"),
        scratch_types=[pltpu.VMEM((TOP_K, ch), I32), pltpu.VMEM((ch, HALF), I32),
                       pltpu.SemaphoreType.DMA],
        name="dispatch",
    )(pos3, h2p)


def _gather_back(pos, ys):
    t = pos.shape[1]
    ch = SC_GATHER_ROWS
    n_cores, n_workers = _sc_workers()
    n_ch = t // (n_workers * ch)
    posg = pos.reshape(TOP_K, t // ch, ch).transpose(1, 0, 2).reshape(-1)

    def body(pos_hbm, ys_hbm, yt_hbm, idx_v, buf0, buf1, gsem, wsem0, wsem1):
        wid = lax.axis_index("s") * n_cores + lax.axis_index("c")
        bufs = (buf0, buf1)
        wsems = (wsem0, wsem1)

        @pl.loop(0, n_ch)
        def _(c):
            chunk = wid * n_ch + c
            pltpu.sync_copy(pos_hbm.at[pl.ds(chunk * (TOP_K * ch), TOP_K * ch)], idx_v)
            writes = [None, None]
            for k in range(TOP_K):
                b = k % 2
                if writes[b] is not None:
                    writes[b].wait()
                pltpu.async_copy(ys_hbm.at[idx_v.at[pl.ds(k * ch, ch)]], bufs[b], gsem).wait()
                writes[b] = pltpu.async_copy(bufs[b], yt_hbm.at[k, pl.ds(chunk * ch, ch)], wsems[b])
            writes[0].wait()
            writes[1].wait()

    return pl.kernel(
        body,
        out_type=jax.ShapeDtypeStruct((TOP_K, t, HALF), I32),
        mesh=plsc.VectorSubcoreMesh(core_axis_name="c", subcore_axis_name="s"),
        scratch_types=[pltpu.VMEM((TOP_K * ch,), I32), pltpu.VMEM((ch, HALF), I32),
                       pltpu.VMEM((ch, HALF), I32), pltpu.SemaphoreType.DMA,
                       pltpu.SemaphoreType.DMA, pltpu.SemaphoreType.DMA],
        name="gather_back",
    )(posg, ys)


def _expert_kernel(nblk_ref, blk0_ref, cnt_ref, wg_ref, wu_ref, wd_ref, xs_hbm, after_ref, ys_hbm,
                   wg_s, wu_s, wd_s, xbuf, ybuf, in_sem, out_sem):
    del after_ref
    e = pl.program_id(0)
    nb = nblk_ref[e]
    b0 = blk0_ref[e]
    cnt = cnt_ref[e]
    n_all = blk0_ref[N_EXPERTS]
    ahead = EXPERT_SLOTS - 1
    wg_s[...] = wg_ref[0].astype(BF16)
    wu_s[...] = wu_ref[0].astype(BF16)
    wd_s[...] = wd_ref[0].astype(BF16)

    def slot_of(g):
        return g & (EXPERT_SLOTS - 1)

    def rows_of(g):
        return pl.ds(pl.multiple_of(g * MOE_BLOCK, MOE_BLOCK), MOE_BLOCK)

    def in_copy(g):
        return pltpu.make_async_copy(xs_hbm.at[rows_of(g)], xbuf.at[slot_of(g)], in_sem.at[slot_of(g)])

    def out_copy(g):
        return pltpu.make_async_copy(ybuf.at[slot_of(g)], ys_hbm.at[rows_of(g)], out_sem.at[slot_of(g)])

    @pl.when(e == 0)
    def _():
        for g in range(ahead):
            @pl.when(g < n_all)
            def _():
                in_copy(g).start()

    def one_block(j, carry):
        g = b0 + j
        in_copy(g).wait()

        @pl.when(g + ahead < n_all)
        def _():
            in_copy(g + ahead).start()

        @pl.when(g >= EXPERT_SLOTS)
        def _():
            out_copy(g - EXPERT_SLOTS).wait()

        rows = lax.broadcasted_iota(I32, (MOE_BLOCK, HALF), 0)
        x = jnp.where(rows < cnt - j * MOE_BLOCK, xbuf[slot_of(g)], 0)
        ybuf[slot_of(g)] = _swiglu_packed(x, wg_s, wu_s, wd_s)
        out_copy(g).start()
        return carry

    lax.fori_loop(0, nb, one_block, 0)

    @pl.when(e == N_EXPERTS - 1)
    def _():
        for back in range(1, EXPERT_SLOTS + 1):
            @pl.when(n_all >= back)
            def _():
                out_copy(n_all - back).wait()


def _experts(n_blk, blk0, counts, n_rows, xs, w_gate, w_up, w_down, after):
    wspec_in = pl.BlockSpec((1, D_MODEL, EXPERT_DIM), lambda e, *_: (e, 0, 0))
    grid_spec = pltpu.PrefetchScalarGridSpec(
        num_scalar_prefetch=3,
        grid=(N_EXPERTS,),
        in_specs=[wspec_in, wspec_in,
                  pl.BlockSpec((1, EXPERT_DIM, D_MODEL), lambda e, *_: (e, 0, 0)),
                  pl.BlockSpec(memory_space=pl.ANY),
                  pl.BlockSpec(memory_space=pl.ANY)],
        out_specs=pl.BlockSpec(memory_space=pl.ANY),
        scratch_shapes=[pltpu.VMEM((D_MODEL, EXPERT_DIM), BF16), pltpu.VMEM((D_MODEL, EXPERT_DIM), BF16),
                        pltpu.VMEM((EXPERT_DIM, D_MODEL), BF16),
                        pltpu.VMEM((EXPERT_SLOTS, MOE_BLOCK, HALF), I32),
                        pltpu.VMEM((EXPERT_SLOTS, MOE_BLOCK, HALF), I32),
                        pltpu.SemaphoreType.DMA((EXPERT_SLOTS,)), pltpu.SemaphoreType.DMA((EXPERT_SLOTS,))],
    )
    return pl.pallas_call(
        _expert_kernel,
        grid_spec=grid_spec,
        out_shape=jax.ShapeDtypeStruct((n_rows, HALF), I32),
        compiler_params=pltpu.CompilerParams(
            dimension_semantics=("arbitrary",), vmem_limit_bytes=VMEM_LIMIT),
        name="experts",
    )(n_blk, blk0, counts, w_gate, w_up, w_down, xs, after)


def _combine_kernel(x1_ref, ysh_ref, yt_ref, wt_ref, p_ref, gple_ref, wpg_ref, wpp_ref, gpost_ref, *rest):
    out_ref = rest[-1]
    proj = _rms(_dot(p_ref[...].astype(BF16), wpp_ref[...]), gpost_ref[...])
    x1 = x1_ref[...]
    sh_lo, sh_hi = _unpack(ysh_ref[...])
    acc_lo = x1[:, 0:HALF] + sh_lo
    acc_hi = x1[:, HALF:D_MODEL] + sh_hi
    wt = wt_ref[...]
    for k in range(TOP_K):
        lo, hi = _unpack(yt_ref[k])
        wk = wt[:, k:k + 1]
        acc_lo = acc_lo + wk * lo
        acc_hi = acc_hi + wk * hi
    x2 = jnp.concatenate([acc_lo, acc_hi], axis=1)
    gate = _sigmoid(_dot(_rms(x2, gple_ref[...]).astype(BF16), wpg_ref[...]))
    out_ref[...] = x2 + gate * proj


def _combine(x1, ysh, yt, wt_t, p2d, g_ple, w_ple_gate, w_ple_proj, g_ple_post, tok0, tok0_all, prev_out):
    t = p2d.shape[0]
    tm = TM_COMBINE
    steps = yt.shape[1] // tm
    off = tok0 // tm
    off_all = tok0_all // tm
    operands = [x1, ysh, yt, wt_t, p2d, g_ple.reshape(1, D_MODEL), w_ple_gate.astype(BF16),
                w_ple_proj.astype(BF16), g_ple_post.reshape(1, D_MODEL)]
    in_specs = [pl.BlockSpec((tm, D_MODEL), lambda i: (off + i, 0)),
                pl.BlockSpec((tm, HALF), lambda i: (off + i, 0)),
                pl.BlockSpec((TOP_K, tm, HALF), lambda i: (0, i, 0)),
                pl.BlockSpec((tm, TOP_K), lambda i: (off + i, 0)),
                pl.BlockSpec((tm, PLE_DIM), lambda i: (off_all + i, 0)),
                _const_spec((1, D_MODEL)),
                _const_spec((D_MODEL, D_MODEL)),
                _const_spec((PLE_DIM, D_MODEL)),
                _const_spec((1, D_MODEL))]
    aliases = {}
    if prev_out is not None:
        aliases = {len(operands): 0}
        operands.append(prev_out)
        in_specs.append(pl.BlockSpec(memory_space=pl.ANY))
    return pl.pallas_call(
        _combine_kernel,
        grid=(steps,),
        in_specs=in_specs,
        out_specs=pl.BlockSpec((tm, D_MODEL), lambda i: (off_all + i, 0)),
        out_shape=jax.ShapeDtypeStruct((t, D_MODEL), F32),
        input_output_aliases=aliases,
        compiler_params=pltpu.CompilerParams(
            dimension_semantics=("arbitrary",), vmem_limit_bytes=VMEM_LIMIT),
        name="combine",
    )(*operands)


def _layer(x2d, p2d, batch, seq, g_mix, w_in, w_pool_mix, pool_scale, w_branch_a, q_gain, k_gain,
           attn_sinks, w_branch_b, w_gate, b_gate, w_out, g_ffn, w_router, router_bias, w_exp_gate,
           w_exp_up, w_exp_down, w_sh_gate, w_sh_up, w_sh_down, g_ple, w_ple_gate, w_ple_proj,
           g_ple_post):
    bh = batch // PIPELINE_HALVES
    th = bh * seq
    n_rows = (-(-(th * TOP_K) // MOE_BLOCK) + N_EXPERTS) * MOE_BLOCK

    halves = []
    after = None
    for h in range(PIPELINE_HALVES):
        x1 = _mixer(x2d, h * bh, bh, seq, g_mix, w_in, w_pool_mix, pool_scale, w_branch_a, q_gain, k_gain,
                    attn_sinks, w_branch_b, w_gate, b_gate, w_out, after=after)
        h2p, idx8, wt8, rk8, cnt = _ffn_pre(x1, g_ffn, w_router, router_bias)
        counts = cnt[:, 0].astype(I32)
        padded = (counts + MOE_BLOCK - 1) // MOE_BLOCK * MOE_BLOCK
        pend = jnp.cumsum(padded)
        pstart = pend - padded
        pos = _positions(pstart.astype(I32), idx8, rk8)
        xs = _dispatch(pos, h2p, n_rows)
        blk0 = jnp.concatenate([pstart, pend[-1:]]) // MOE_BLOCK
        halves.append((x1, h2p, wt8.T, pos, xs, (padded // MOE_BLOCK).astype(I32), blk0.astype(I32), counts))
        after = pos

    out = None
    tc = th // COMBINE_CHUNKS
    for h, (x1, h2p, wt_t, pos, xs, n_blk, blk0, counts) in enumerate(halves):
        ysh = _shared(h2p, w_sh_gate, w_sh_up, w_sh_down)
        ys = _experts(n_blk, blk0, counts, n_rows, xs, w_exp_gate, w_exp_up, w_exp_down, ysh)
        for c in range(COMBINE_CHUNKS):
            yt = _gather_back(pos[:, c * tc:(c + 1) * tc], ys)
            out = _combine(x1, ysh, yt, wt_t, p2d, g_ple, w_ple_gate, w_ple_proj, g_ple_post,
                           c * tc, h * th + c * tc, out)
    return out


def kernel(x, p, g_mix, w_in, w_pool_mix, pool_scale, w_branch_a, q_gain, k_gain, attn_sinks,
           w_branch_b, w_gate, b_gate, w_out, g_ffn, w_router, router_bias, w_exp_gate, w_exp_up,
           w_exp_down, w_sh_gate, w_sh_up, w_sh_down, g_ple, w_ple_gate, w_ple_proj, g_ple_post):
    batch, seq, d = x.shape
    depth = p.shape[0]
    x2d = x.reshape(batch * seq, d)
    for i in range(depth):
        x2d = _layer(x2d, p[i].reshape(batch * seq, PLE_DIM), batch, seq, g_mix[i], w_in[i],
                     w_pool_mix[i], pool_scale[i], w_branch_a[i], q_gain[i], k_gain[i], attn_sinks[i],
                     w_branch_b[i], w_gate[i], b_gate[i], w_out[i], g_ffn[i], w_router[i],
                     router_bias[i], w_exp_gate[i], w_exp_up[i], w_exp_down[i], w_sh_gate[i],
                     w_sh_up[i], w_sh_down[i], g_ple[i], w_ple_gate[i], w_ple_proj[i], g_ple_post[i])
    return x2d.reshape(batch, seq, d)
```

```python
import functools

import numpy as np
import jax
import jax.numpy as jnp
from jax import lax
from jax.experimental import pallas as pl
from jax.experimental.pallas import tpu as pltpu
from jax.experimental.pallas import tpu_sc as plsc

F32 = jnp.float32
BF16 = jnp.bfloat16
I32 = jnp.int32

D_MODEL = 1024
PLE_DIM = 256
POOL_WINDOWS = (2, 4, 8, 16)
POOL_GROUP_DIM = 128
POOL_DIM = 512
N_HEADS = 8
N_KV_HEADS = 2
Q_GROUP = 4
HEAD_DIM = 64
HEAD_PAD = 128
Q_DIM = 512
KV_DIM = 128
ATTN_BLOCK = 128
N_EXPERTS = 64
N_EXPERT_GROUPS = 8
GROUP_SIZE = 8
TOPK_GROUPS = 4
TOP_K = 8
EXPERT_DIM = 256
ROUTED_SCALE = 2.5
MOE_BLOCK = 256
EXPERT_SLOTS = 8
EPS = 1e-6
HALF = D_MODEL // 2
MASKED = -1e30

COL_Q = POOL_DIM
COL_K = COL_Q + N_HEADS * HEAD_PAD
COL_V = COL_K + N_KV_HEADS * HEAD_PAD
IN_PAD = COL_V + N_KV_HEADS * HEAD_PAD
HEAD_ORDER = (0, 2, 1, 3)

TM_MIX = 512
TM_FFN = 512
TM_COMBINE = 512
TM_POS = 4096
COMBINE_CHUNKS = 2
PIPELINE_HALVES = 2
SC_SCATTER_ROWS = 128
SC_GATHER_ROWS = 64
POOL_TAIL = 8
VMEM_LIMIT = 56 * 1024 * 1024

_NT = (((1,), (1,)), ((), ()))


def _dot(a, b):
    return jnp.dot(a, b, preferred_element_type=F32)


def _rms(x, g):
    ms = jnp.mean(x * x, axis=-1, keepdims=True)
    return x * lax.rsqrt(ms + EPS) * g


def _sigmoid(x):
    return 0.5 * jnp.tanh(0.5 * x) + 0.5


def _pack(lo, hi):
    lo_bits = lax.bitcast_convert_type(lo.astype(BF16).astype(F32), I32)
    hi_bits = lax.bitcast_convert_type(hi.astype(BF16).astype(F32), I32)
    return (hi_bits & jnp.int32(-65536)) | lax.shift_right_logical(lo_bits, jnp.int32(16))


def _unpack(w):
    lo = lax.bitcast_convert_type(lax.shift_left(w, jnp.int32(16)), F32)
    hi = lax.bitcast_convert_type(w & jnp.int32(-65536), F32)
    return lo, hi


def _mixer_kernel(x_ref, gmix_ref, win_ref, qg_ref, kg_ref, bias_ref, sink_ref, ones_ref, wmix_ref,
                  pscale_ref, wa_ref, wb_ref, wg_ref, bg_ref, wout_ref, *rest):
    x1_ref, kbuf, vt_buf, pbuf, zbt_buf = rest[-5:]
    s = pl.program_id(1)
    tm = x_ref.shape[0]

    @pl.when(s == 0)
    def _():
        kbuf[0:ATTN_BLOCK, :] = jnp.zeros((ATTN_BLOCK, kbuf.shape[1]), BF16)
        vt_buf[:, 0:ATTN_BLOCK] = jnp.zeros((vt_buf.shape[0], ATTN_BLOCK), BF16)
        pbuf[:, 0:POOL_TAIL, :] = jnp.zeros((pbuf.shape[0], POOL_TAIL, POOL_DIM), F32)

    x = x_ref[...]
    hb = _rms(x, gmix_ref[...]).astype(BF16)
    u = _dot(hb, win_ref[...])

    up = u[:, 0:POOL_DIM]
    t0 = POOL_TAIL
    pbuf[0, t0:t0 + tm, :] = up
    s2 = up + pbuf[0, t0 - 1:t0 - 1 + tm, :]
    pbuf[1, t0:t0 + tm, 128:512] = s2[:, 128:512]
    s4 = s2[:, 128:512] + pbuf[1, t0 - 2:t0 - 2 + tm, 128:512]
    pbuf[2, t0:t0 + tm, 256:512] = s4[:, 128:384]
    s8 = s4[:, 128:384] + pbuf[2, t0 - 4:t0 - 4 + tm, 256:512]
    pbuf[3, t0:t0 + tm, 384:512] = s8[:, 128:256]
    s16 = s8[:, 128:256] + pbuf[3, t0 - 8:t0 - 8 + tm, 384:512]
    for lvl in range(4):
        pbuf[lvl, 0:POOL_TAIL, 128 * lvl:512] = pbuf[lvl, tm:tm + POOL_TAIL, 128 * lvl:512]
    wsums = (s2[:, 0:128], s4[:, 0:128], s8[:, 0:128], s16)
    tpos = (s * tm).astype(F32) + lax.broadcasted_iota(I32, (tm, 1), 0).astype(F32)
    za_parts = []
    for g, w in enumerate(POOL_WINDOWS):
        inv_count = 1.0 / jnp.minimum(tpos + 1.0, float(w))
        pooled = wsums[g] * inv_count - up[:, 128 * g:128 * (g + 1)]
        mixed = _dot(pooled.astype(BF16), wmix_ref[g]) * pscale_ref[:, 128 * g:128 * (g + 1)]
        za_parts.append(mixed.astype(BF16))
    za = jnp.concatenate(za_parts, axis=1)

    ones2 = ones_ref[...]
    q = u[:, COL_Q:COL_K]
    q2 = (q * q).astype(BF16)
    qss = jnp.concatenate([_dot(q2[:, 256 * c:256 * (c + 1)], ones2) for c in range(N_HEADS // 2)], axis=1)
    qn = (q * lax.rsqrt(qss * (1.0 / HEAD_DIM) + EPS) * qg_ref[...]).astype(BF16)
    k = u[:, COL_K:COL_V]
    kss = _dot((k * k).astype(BF16), ones2)
    kbuf[ATTN_BLOCK:ATTN_BLOCK + tm, :] = (k * lax.rsqrt(kss * (1.0 / HEAD_DIM) + EPS) * kg_ref[...]).astype(BF16)
    vt_buf[:, ATTN_BLOCK:ATTN_BLOCK + tm] = u[:, COL_V:IN_PAD].T.astype(BF16)

    nq = Q_GROUP * ATTN_BLOCK
    key_j = lax.broadcasted_iota(I32, (ATTN_BLOCK, nq), 0)
    qry_i = lax.broadcasted_iota(I32, (ATTN_BLOCK, nq), 1) & (ATTN_BLOCK - 1)
    from_prev = key_j > qry_i
    first = jnp.where(s == 0, 1, 0)
    for n in range(tm // ATTN_BLOCK):
        r0 = ATTN_BLOCK * n
        for kv in range(N_KV_HEADS):
            c0 = HEAD_PAD * kv
            qs = jnp.concatenate(
                [qn[r0:r0 + ATTN_BLOCK, HEAD_PAD * (kv * Q_GROUP + g):HEAD_PAD * (kv * Q_GROUP + g + 1)]
                 for g in HEAD_ORDER], axis=0)
            kk = kbuf[r0:r0 + 2 * ATTN_BLOCK, c0:c0 + HEAD_PAD]
            st = lax.dot_general(kk, qs, _NT, preferred_element_type=F32)
            sc = jnp.where(from_prev, st[0:ATTN_BLOCK], st[ATTN_BLOCK:2 * ATTN_BLOCK])
            sc = sc + (bias_ref[first, kv] if n == 0 else bias_ref[0, kv])
            sink = sink_ref[kv]
            m = jnp.maximum(jnp.max(sc, axis=0, keepdims=True), sink)
            e = jnp.exp(sc - m)
            inv_den = 1.0 / (jnp.sum(e, axis=0, keepdims=True) + jnp.exp(sink - m))
            pt = jnp.concatenate([jnp.where(from_prev, e, 0.0), jnp.where(from_prev, 0.0, e)],
                                 axis=0).astype(BF16)
            v_even = vt_buf[c0:c0 + HEAD_PAD, r0:r0 + 2 * ATTN_BLOCK]
            v_odd = jnp.concatenate([v_even[HEAD_DIM:HEAD_PAD], v_even[0:HEAD_DIM]], axis=0)
            half = 2 * ATTN_BLOCK
            o = (_dot(v_even, pt[:, 0:half]) * inv_den[:, 0:half]
                 + _dot(v_odd, pt[:, half:2 * half]) * inv_den[:, half:2 * half])
            ch = kv * Q_GROUP * HEAD_DIM
            zbt_buf[ch:ch + HEAD_PAD, r0:r0 + ATTN_BLOCK] = o[:, 0:ATTN_BLOCK]
            zbt_buf[ch + HEAD_PAD:ch + 2 * HEAD_PAD, r0:r0 + ATTN_BLOCK] = o[:, ATTN_BLOCK:half]
    kbuf[0:ATTN_BLOCK, :] = kbuf[tm:tm + ATTN_BLOCK, :]
    vt_buf[:, 0:ATTN_BLOCK] = vt_buf[:, tm:tm + ATTN_BLOCK]

    y_a = _dot(za, wa_ref[...])
    y_b = _dot(zbt_buf[...].T.astype(BF16), wb_ref[...])
    g_a = _sigmoid(_dot(hb, wg_ref[:, 0:D_MODEL]) + bg_ref[:, 0:D_MODEL])
    merged = g_a * y_a
    g_b = _sigmoid(_dot(hb, wg_ref[:, D_MODEL:2 * D_MODEL]) + bg_ref[:, D_MODEL:2 * D_MODEL])
    merged = (merged + g_b * y_b).astype(BF16)
    x1_ref[...] = x + _dot(merged, wout_ref[...])


def _attn_tables(attn_sinks):
    slopes = 2.0 ** (-8.0 * (np.arange(N_HEADS) + 1) / N_HEADS)
    j = np.arange(ATTN_BLOCK)[:, None]
    i = np.arange(ATTN_BLOCK)[None, :]
    from_prev = j > i
    dist = np.where(from_prev, ATTN_BLOCK + i - j, i - j)
    bias = np.empty((2, N_KV_HEADS, ATTN_BLOCK, Q_GROUP * ATTN_BLOCK), np.float32)
    for first in range(2):
        ok = ~from_prev if first else np.ones_like(from_prev)
        for kv in range(N_KV_HEADS):
            for slot, g in enumerate(HEAD_ORDER):
                sl = np.float32(slopes[kv * Q_GROUP + g])
                val = -(sl * dist.astype(np.float32))
                bias[first, kv, :, slot * ATTN_BLOCK:(slot + 1) * ATTN_BLOCK] = np.where(ok, val, MASKED)
    sink = attn_sinks.astype(F32).reshape(N_KV_HEADS, Q_GROUP)[:, np.asarray(HEAD_ORDER)]
    sink = jnp.repeat(sink, ATTN_BLOCK, axis=1)
    return jnp.asarray(bias), sink.reshape(N_KV_HEADS, 1, Q_GROUP * ATTN_BLOCK)


def _pad_heads(w, n_heads):
    k = w.shape[0]
    w = w.reshape(k, n_heads, HEAD_DIM)
    w = jnp.pad(w, ((0, 0), (0, 0), (0, HEAD_PAD - HEAD_DIM)))
    return w.reshape(k, n_heads * HEAD_PAD)


def _const_spec(shape):
    nd = len(shape)
    return pl.BlockSpec(shape, lambda *_: (0,) * nd)


def _mixer(x2d, batch0, batch, seq, g_mix, w_in, w_pool_mix, pool_scale, w_branch_a, q_gain, k_gain,
           attn_sinks, w_branch_b, w_gate, b_gate, w_out, after=None):
    t = batch * seq
    tm = TM_MIX
    ns = seq // tm
    w_q = w_in[:, POOL_DIM:POOL_DIM + Q_DIM]
    w_k = w_in[:, POOL_DIM + Q_DIM:POOL_DIM + Q_DIM + KV_DIM]
    w_v = w_in[:, POOL_DIM + Q_DIM + KV_DIM:]
    win_p = jnp.concatenate([w_in[:, :POOL_DIM], _pad_heads(w_q, N_HEADS), _pad_heads(w_k, N_KV_HEADS),
                             _pad_heads(w_v, N_KV_HEADS)], axis=1).astype(BF16)
    qg = jnp.tile(jnp.pad(q_gain * (HEAD_DIM ** -0.5), (0, HEAD_PAD - HEAD_DIM)), N_HEADS).reshape(1, -1)
    kg = jnp.tile(jnp.pad(k_gain, (0, HEAD_PAD - HEAD_DIM)), N_KV_HEADS).reshape(1, -1)
    bias, sink = _attn_tables(attn_sinks)
    ones2 = jnp.asarray(np.kron(np.eye(2), np.ones((HEAD_PAD, HEAD_PAD))), BF16)
    operands = (x2d, g_mix.reshape(1, D_MODEL), win_p, qg, kg, bias, sink, ones2, w_pool_mix.astype(BF16),
                pool_scale.reshape(1, POOL_DIM), w_branch_a.astype(BF16), w_branch_b.astype(BF16),
                w_gate.astype(BF16), b_gate.reshape(1, 2 * D_MODEL), w_out.astype(BF16))
    in_specs = [pl.BlockSpec((tm, D_MODEL), lambda b, s: ((batch0 + b) * ns + s, 0))]
    in_specs += [_const_spec(op.shape) for op in operands[1:]]
    if after is not None:
        operands += (after,)
        in_specs.append(pl.BlockSpec(memory_space=pl.ANY))
    return pl.pallas_call(
        _mixer_kernel,
        grid=(batch, ns),
        in_specs=in_specs,
        out_specs=pl.BlockSpec((tm, D_MODEL), lambda b, s: (b * ns + s, 0)),
        out_shape=jax.ShapeDtypeStruct((t, D_MODEL), F32),
        scratch_shapes=[
            pltpu.VMEM((ATTN_BLOCK + tm, N_KV_HEADS * HEAD_PAD), BF16),
            pltpu.VMEM((N_KV_HEADS * HEAD_PAD, ATTN_BLOCK + tm), BF16),
            pltpu.VMEM((4, POOL_TAIL + tm, POOL_DIM), F32),
            pltpu.VMEM((Q_DIM, tm), F32),
        ],
        compiler_params=pltpu.CompilerParams(
            dimension_semantics=("arbitrary", "arbitrary"), vmem_limit_bytes=VMEM_LIMIT),
        name="mixer",
    )(*operands)


def _ffn_pre_kernel(x1_ref, gffn_ref, wrh_ref, wrl_ref, rb_ref, tri_ref,
                    h2p_ref, idx_ref, wt_ref, rk_ref, cnt_ref, run_ref):
    i = pl.program_id(0)
    tm = x1_ref.shape[0]

    @pl.when(i == 0)
    def _():
        run_ref[...] = jnp.zeros_like(run_ref)

    h2 = _rms(x1_ref[...], gffn_ref[...])
    hb = h2.astype(BF16)
    h2p_ref[...] = _pack(h2[:, 0:HALF], h2[:, HALF:D_MODEL])

    lo = (h2 - hb.astype(F32)).astype(BF16)
    wrh = wrh_ref[...]
    logits = (lax.dot_general(wrh, hb, _NT, preferred_element_type=F32)
              + lax.dot_general(wrl_ref[...], hb, _NT, preferred_element_type=F32)
              + lax.dot_general(wrh, lo, _NT, preferred_element_type=F32))
    scores = _sigmoid(logits)
    choice = scores + rb_ref[...]

    neg = -jnp.inf
    sub8 = lax.broadcasted_iota(I32, (GROUP_SIZE, tm), 0).astype(F32)
    grp_rows = []
    for gi in range(N_EXPERT_GROUPS):
        cg = choice[GROUP_SIZE * gi:GROUP_SIZE * (gi + 1)]
        m1 = jnp.max(cg, axis=0, keepdims=True)
        first = jnp.min(jnp.where(cg == m1, sub8, float(GROUP_SIZE)), axis=0, keepdims=True)
        m2 = jnp.max(jnp.where(sub8 == first, neg, cg), axis=0, keepdims=True)
        grp_rows.append(m1 + m2)
    gs = jnp.concatenate(grp_rows, axis=0)
    beaten = jnp.zeros((N_EXPERT_GROUPS, tm), F32)
    for gi in range(N_EXPERT_GROUPS):
        row = grp_rows[gi]
        wins = (row > gs) | ((row == gs) & (sub8 > float(gi)))
        beaten = beaten + jnp.where(wins, 1.0, 0.0)
    gsel = beaten < float(TOPK_GROUPS)
    cm = jnp.concatenate(
        [jnp.where(gsel[gi:gi + 1], choice[GROUP_SIZE * gi:GROUP_SIZE * (gi + 1)], neg)
         for gi in range(N_EXPERT_GROUPS)], axis=0)

    iota_e = lax.broadcasted_iota(I32, (N_EXPERTS, tm), 0).astype(F32)
    idx_rows, s_rows = [], []
    sel = jnp.zeros((N_EXPERTS, tm), F32)
    for _ in range(TOP_K):
        m = jnp.max(cm, axis=0, keepdims=True)
        idx = jnp.min(jnp.where(cm == m, iota_e, float(N_EXPERTS)), axis=0, keepdims=True)
        oh = iota_e == idx
        s_rows.append(jnp.sum(jnp.where(oh, scores, 0.0), axis=0, keepdims=True))
        idx_rows.append(idx)
        cm = jnp.where(oh, neg, cm)
        sel = sel + jnp.where(oh, 1.0, 0.0)

    run = run_ref[:, 0:1]
    cum = _dot(sel.astype(BF16), tri_ref[...])
    before = run + cum - sel
    rk_rows = [jnp.sum(jnp.where(iota_e == idx, before, 0.0), axis=0, keepdims=True) for idx in idx_rows]
    new_run = run + jnp.sum(sel, axis=1, keepdims=True)
    run_ref[...] = jnp.broadcast_to(new_run, run_ref.shape)
    cnt_ref[...] = jnp.broadcast_to(new_run, cnt_ref.shape)

    ssum = s_rows[0]
    for r in range(1, TOP_K):
        ssum = ssum + s_rows[r]
    denom = ssum + 1e-20
    idx_ref[...] = jnp.concatenate(idx_rows, axis=0).astype(I32)
    wt_ref[...] = jnp.concatenate([sr / denom * ROUTED_SCALE for sr in s_rows], axis=0)
    rk_ref[...] = jnp.concatenate(rk_rows, axis=0).astype(I32)


def _ffn_pre(x1, g_ffn, w_router, router_bias):
    t = x1.shape[0]
    tm = TM_FFN
    wr_t = w_router.astype(F32).T
    wr_hi = wr_t.astype(BF16)
    wr_lo = (wr_t - wr_hi.astype(F32)).astype(BF16)
    tri = (np.arange(tm)[:, None] <= np.arange(tm)[None, :]).astype(np.float32)
    operands = (x1, g_ffn.reshape(1, D_MODEL), wr_hi, wr_lo,
                router_bias.astype(F32).reshape(N_EXPERTS, 1), jnp.asarray(tri, BF16))
    in_specs = [pl.BlockSpec((tm, D_MODEL), lambda i: (i, 0))]
    in_specs += [_const_spec(op.shape) for op in operands[1:]]
    row8 = pl.BlockSpec((TOP_K, tm), lambda i: (0, i))
    return pl.pallas_call(
        _ffn_pre_kernel,
        grid=(t // tm,),
        in_specs=in_specs,
        out_specs=[pl.BlockSpec((tm, HALF), lambda i: (i, 0)),
                   row8, row8, row8,
                   pl.BlockSpec((N_EXPERTS, 128), lambda i: (0, 0))],
        out_shape=[jax.ShapeDtypeStruct((t, HALF), I32),
                   jax.ShapeDtypeStruct((TOP_K, t), I32),
                   jax.ShapeDtypeStruct((TOP_K, t), F32),
                   jax.ShapeDtypeStruct((TOP_K, t), I32),
                   jax.ShapeDtypeStruct((N_EXPERTS, 128), F32)],
        scratch_shapes=[pltpu.VMEM((N_EXPERTS, 128), F32)],
        compiler_params=pltpu.CompilerParams(
            dimension_semantics=("arbitrary",), vmem_limit_bytes=VMEM_LIMIT),
        name="ffn_pre",
    )(*operands)


def _swiglu_packed(x_packed, wg, wu, wd):
    lo, hi = _unpack(x_packed)
    lo = lo.astype(BF16)
    hi = hi.astype(BF16)
    g = _dot(lo, wg[0:HALF, :]) + _dot(hi, wg[HALF:D_MODEL, :])
    u = _dot(lo, wu[0:HALF, :]) + _dot(hi, wu[HALF:D_MODEL, :])
    a = (g * _sigmoid(g) * u).astype(BF16)
    y = _dot(a, wd[...])
    return _pack(y[:, 0:HALF], y[:, HALF:D_MODEL])


def _shared_kernel(h2p_ref, wg_ref, wu_ref, wd_ref, ysh_ref):
    ysh_ref[...] = _swiglu_packed(h2p_ref[...], wg_ref, wu_ref, wd_ref)


def _shared(h2p, w_sh_gate, w_sh_up, w_sh_down):
    t = h2p.shape[0]
    tm = TM_FFN
    operands = (h2p, w_sh_gate.astype(BF16), w_sh_up.astype(BF16), w_sh_down.astype(BF16))
    in_specs = [pl.BlockSpec((tm, HALF), lambda i: (i, 0))]
    in_specs += [_const_spec(op.shape) for op in operands[1:]]
    return pl.pallas_call(
        _shared_kernel,
        grid=(t // tm,),
        in_specs=in_specs,
        out_specs=pl.BlockSpec((tm, HALF), lambda i: (i, 0)),
        out_shape=jax.ShapeDtypeStruct((t, HALF), I32),
        compiler_params=pltpu.CompilerParams(
            dimension_semantics=("arbitrary",), vmem_limit_bytes=VMEM_LIMIT),
        name="shared",
    )(*operands)


def _positions_kernel(pstart_ref, idx_ref, rk_ref, pos_ref):
    idx = idx_ref[...]
    acc = rk_ref[...]
    for e in range(N_EXPERTS):
        acc = acc + jnp.where(idx == e, pstart_ref[e], 0)
    pos_ref[...] = acc


def _positions(pstart, idx8, rk8):
    t = idx8.shape[1]
    tm = TM_POS
    spec = pl.BlockSpec((TOP_K, tm), lambda i, ps: (0, i))
    return pl.pallas_call(
        _positions_kernel,
        grid_spec=pltpu.PrefetchScalarGridSpec(num_scalar_prefetch=1, grid=(t // tm,),
                                               in_specs=[spec, spec], out_specs=spec),
        out_shape=jax.ShapeDtypeStruct((TOP_K, t), I32),
        name="positions",
    )(pstart, idx8, rk8)


def _sc_workers():
    info = plsc.get_sparse_core_info()
    return info.num_cores, info.num_cores * info.num_subcores


def _dispatch(pos, h2p, n_rows):
    t = h2p.shape[0]
    ch = SC_SCATTER_ROWS
    n_cores, n_workers = _sc_workers()
    n_ch = t // (n_workers * ch)
    pos3 = pos.reshape(TOP_K, t // ch, ch).transpose(1, 0, 2)

    def body(pos_hbm, h_hbm, xs_hbm, idx_v, rows_v, sem):
        wid = lax.axis_index("s") * n_cores + lax.axis_index("c")

        @pl.loop(0, n_ch)
        def _(c):
            chunk = wid * n_ch + c
            pltpu.sync_copy(h_hbm.at[pl.ds(chunk * ch, ch)], rows_v)
            pltpu.sync_copy(pos_hbm.at[chunk], idx_v)
            copies = [pltpu.async_copy(rows_v, xs_hbm.at[idx_v.at[k]], sem) for k in range(TOP_K)]
            for cp in copies:
                cp.wait()

    return pl.kernel(
        body,
        out_type=jax.ShapeDtypeStruct((n_rows, HALF), I32),
        mesh=plsc.VectorSubcoreMesh(core_axis_name="c", subcore_axis_name="s"),
        scratch_types=[pltpu.VMEM((TOP_K, ch), I32), pltpu.VMEM((ch, HALF), I32),
                       pltpu.SemaphoreType.DMA],
        name="dispatch",
    )(pos3, h2p)


def _gather_back(pos, ys):
    t = pos.shape[1]
    ch = SC_GATHER_ROWS
    n_cores, n_workers = _sc_workers()
    n_ch = t // (n_workers * ch)
    posg = pos.reshape(TOP_K, t // ch, ch).transpose(1, 0, 2).reshape(-1)

    def body(pos_hbm, ys_hbm, yt_hbm, idx_v, buf0, buf1, gsem, wsem0, wsem1):
        wid = lax.axis_index("s") * n_cores + lax.axis_index("c")
        bufs = (buf0, buf1)
        wsems = (wsem0, wsem1)

        @pl.loop(0, n_ch)
        def _(c):
            chunk = wid * n_ch + c
            pltpu.sync_copy(pos_hbm.at[pl.ds(chunk * (TOP_K * ch), TOP_K * ch)], idx_v)
            writes = [None, None]
            for k in range(TOP_K):
                b = k % 2
                if writes[b] is not None:
                    writes[b].wait()
                pltpu.async_copy(ys_hbm.at[idx_v.at[pl.ds(k * ch, ch)]], bufs[b], gsem).wait()
                writes[b] = pltpu.async_copy(bufs[b], yt_hbm.at[k, pl.ds(chunk * ch, ch)], wsems[b])
            writes[0].wait()
            writes[1].wait()

    return pl.kernel(
        body,
        out_type=jax.ShapeDtypeStruct((TOP_K, t, HALF), I32),
        mesh=plsc.VectorSubcoreMesh(core_axis_name="c", subcore_axis_name="s"),
        scratch_types=[pltpu.VMEM((TOP_K * ch,), I32), pltpu.VMEM((ch, HALF), I32),
                       pltpu.VMEM((ch, HALF), I32), pltpu.SemaphoreType.DMA,
                       pltpu.SemaphoreType.DMA, pltpu.SemaphoreType.DMA],
        name="gather_back",
    )(posg, ys)


def _expert_kernel(nblk_ref, blk0_ref, cnt_ref, wg_ref, wu_ref, wd_ref, xs_hbm, after_ref, ys_hbm,
                   wg_s, wu_s, wd_s, xbuf, ybuf, in_sem, out_sem):
    del after_ref
    e = pl.program_id(0)
    nb = nblk_ref[e]
    b0 = blk0_ref[e]
    cnt = cnt_ref[e]
    n_all = blk0_ref[N_EXPERTS]
    ahead = EXPERT_SLOTS - 2
    wg_s[...] = wg_ref[0].astype(BF16)
    wu_s[...] = wu_ref[0].astype(BF16)
    wd_s[...] = wd_ref[0].astype(BF16)

    def slot_of(g):
        return g & (EXPERT_SLOTS - 1)

    def rows_of(g):
        return pl.ds(pl.multiple_of(g * MOE_BLOCK, MOE_BLOCK), MOE_BLOCK)

    def in_copy(g):
        return pltpu.make_async_copy(xs_hbm.at[rows_of(g)], xbuf.at[slot_of(g)], in_sem.at[slot_of(g)])

    def out_copy(g):
        return pltpu.make_async_copy(ybuf.at[slot_of(g)], ys_hbm.at[rows_of(g)], out_sem.at[slot_of(g)])

    @pl.when(e == 0)
    def _():
        for g in range(ahead):
            @pl.when(g < n_all)
            def _():
                in_copy(g).start()

    def arrive(g):
        in_copy(g).wait()

        @pl.when(g + ahead < n_all)
        def _():
            in_copy(g + ahead).start()

        @pl.when(g >= EXPERT_SLOTS)
        def _():
            out_copy(g - EXPERT_SLOTS).wait()

    def ffn(j):
        g = b0 + j
        rows = lax.broadcasted_iota(I32, (MOE_BLOCK, HALF), 0)
        x = jnp.where(rows < cnt - j * MOE_BLOCK, xbuf[slot_of(g)], 0)
        return _swiglu_packed(x, wg_s, wu_s, wd_s)

    def two_blocks(jj, carry):
        j = 2 * jj
        g = b0 + j
        arrive(g)
        arrive(g + 1)
        y0 = ffn(j)
        y1 = ffn(j + 1)
        ybuf[slot_of(g)] = y0
        ybuf[slot_of(g + 1)] = y1
        out_copy(g).start()
        out_copy(g + 1).start()
        return carry

    lax.fori_loop(0, nb // 2, two_blocks, 0)

    @pl.when(nb % 2 == 1)
    def _():
        g = b0 + nb - 1
        arrive(g)
        ybuf[slot_of(g)] = ffn(nb - 1)
        out_copy(g).start()

    @pl.when(e == N_EXPERTS - 1)
    def _():
        for back in range(1, EXPERT_SLOTS + 1):
            @pl.when(n_all >= back)
            def _():
                out_copy(n_all - back).wait()


def _experts(n_blk, blk0, counts, n_rows, xs, w_gate, w_up, w_down, after):
    wspec_in = pl.BlockSpec((1, D_MODEL, EXPERT_DIM), lambda e, *_: (e, 0, 0))
    grid_spec = pltpu.PrefetchScalarGridSpec(
        num_scalar_prefetch=3,
        grid=(N_EXPERTS,),
        in_specs=[wspec_in, wspec_in,
                  pl.BlockSpec((1, EXPERT_DIM, D_MODEL), lambda e, *_: (e, 0, 0)),
                  pl.BlockSpec(memory_space=pl.ANY),
                  pl.BlockSpec(memory_space=pl.ANY)],
        out_specs=pl.BlockSpec(memory_space=pl.ANY),
        scratch_shapes=[pltpu.VMEM((D_MODEL, EXPERT_DIM), BF16), pltpu.VMEM((D_MODEL, EXPERT_DIM), BF16),
                        pltpu.VMEM((EXPERT_DIM, D_MODEL), BF16),
                        pltpu.VMEM((EXPERT_SLOTS, MOE_BLOCK, HALF), I32),
                        pltpu.VMEM((EXPERT_SLOTS, MOE_BLOCK, HALF), I32),
                        pltpu.SemaphoreType.DMA((EXPERT_SLOTS,)), pltpu.SemaphoreType.DMA((EXPERT_SLOTS,))],
    )
    return pl.pallas_call(
        _expert_kernel,
        grid_spec=grid_spec,
        out_shape=jax.ShapeDtypeStruct((n_rows, HALF), I32),
        compiler_params=pltpu.CompilerParams(
            dimension_semantics=("arbitrary",), vmem_limit_bytes=VMEM_LIMIT),
        name="experts",
    )(n_blk, blk0, counts, w_gate, w_up, w_down, xs, after)


def _combine_kernel(x1_ref, ysh_ref, yt_ref, wt_ref, p_ref, gple_ref, wpg_ref, wpp_ref, gpost_ref, *rest):
    out_ref = rest[-1]
    proj = _rms(_dot(p_ref[...].astype(BF16), wpp_ref[...]), gpost_ref[...])
    x1 = x1_ref[...]
    sh_lo, sh_hi = _unpack(ysh_ref[...])
    acc_lo = x1[:, 0:HALF] + sh_lo
    acc_hi = x1[:, HALF:D_MODEL] + sh_hi
    wt = wt_ref[...]
    for k in range(TOP_K):
        lo, hi = _unpack(yt_ref[k])
        wk = wt[:, k:k + 1]
        acc_lo = acc_lo + wk * lo
        acc_hi = acc_hi + wk * hi
    x2 = jnp.concatenate([acc_lo, acc_hi], axis=1)
    gate = _sigmoid(_dot(_rms(x2, gple_ref[...]).astype(BF16), wpg_ref[...]))
    out_ref[...] = x2 + gate * proj


def _combine(x1, ysh, yt, wt_t, p2d, g_ple, w_ple_gate, w_ple_proj, g_ple_post, tok0, tok0_all, prev_out):
    t = p2d.shape[0]
    tm = TM_COMBINE
    steps = yt.shape[1] // tm
    off = tok0 // tm
    off_all = tok0_all // tm
    operands = [x1, ysh, yt, wt_t, p2d, g_ple.reshape(1, D_MODEL), w_ple_gate.astype(BF16),
                w_ple_proj.astype(BF16), g_ple_post.reshape(1, D_MODEL)]
    in_specs = [pl.BlockSpec((tm, D_MODEL), lambda i: (off + i, 0)),
                pl.BlockSpec((tm, HALF), lambda i: (off + i, 0)),
                pl.BlockSpec((TOP_K, tm, HALF), lambda i: (0, i, 0)),
                pl.BlockSpec((tm, TOP_K), lambda i: (off + i, 0)),
                pl.BlockSpec((tm, PLE_DIM), lambda i: (off_all + i, 0)),
                _const_spec((1, D_MODEL)),
                _const_spec((D_MODEL, D_MODEL)),
                _const_spec((PLE_DIM, D_MODEL)),
                _const_spec((1, D_MODEL))]
    aliases = {}
    if prev_out is not None:
        aliases = {len(operands): 0}
        operands.append(prev_out)
        in_specs.append(pl.BlockSpec(memory_space=pl.ANY))
    return pl.pallas_call(
        _combine_kernel,
        grid=(steps,),
        in_specs=in_specs,
        out_specs=pl.BlockSpec((tm, D_MODEL), lambda i: (off_all + i, 0)),
        out_shape=jax.ShapeDtypeStruct((t, D_MODEL), F32),
        input_output_aliases=aliases,
        compiler_params=pltpu.CompilerParams(
            dimension_semantics=("arbitrary",), vmem_limit_bytes=VMEM_LIMIT),
        name="combine",
    )(*operands)


def _layer(x2d, p2d, batch, seq, g_mix, w_in, w_pool_mix, pool_scale, w_branch_a, q_gain, k_gain,
           attn_sinks, w_branch_b, w_gate, b_gate, w_out, g_ffn, w_router, router_bias, w_exp_gate,
           w_exp_up, w_exp_down, w_sh_gate, w_sh_up, w_sh_down, g_ple, w_ple_gate, w_ple_proj,
           g_ple_post):
    bh = batch // PIPELINE_HALVES
    th = bh * seq
    n_rows = (-(-(th * TOP_K) // MOE_BLOCK) + N_EXPERTS) * MOE_BLOCK

    halves = []
    after = None
    for h in range(PIPELINE_HALVES):
        x1 = _mixer(x2d, h * bh, bh, seq, g_mix, w_in, w_pool_mix, pool_scale, w_branch_a, q_gain, k_gain,
                    attn_sinks, w_branch_b, w_gate, b_gate, w_out, after=after)
        h2p, idx8, wt8, rk8, cnt = _ffn_pre(x1, g_ffn, w_router, router_bias)
        counts = cnt[:, 0].astype(I32)
        padded = (counts + MOE_BLOCK - 1) // MOE_BLOCK * MOE_BLOCK
        pend = jnp.cumsum(padded)
        pstart = pend - padded
        pos = _positions(pstart.astype(I32), idx8, rk8)
        xs = _dispatch(pos, h2p, n_rows)
        blk0 = jnp.concatenate([pstart, pend[-1:]]) // MOE_BLOCK
        halves.append((x1, h2p, wt8.T, pos, xs, (padded // MOE_BLOCK).astype(I32), blk0.astype(I32), counts))
        after = pos

    out = None
    tc = th // COMBINE_CHUNKS
    for h, (x1, h2p, wt_t, pos, xs, n_blk, blk0, counts) in enumerate(halves):
        ysh = _shared(h2p, w_sh_gate, w_sh_up, w_sh_down)
        ys = _experts(n_blk, blk0, counts, n_rows, xs, w_exp_gate, w_exp_up, w_exp_down, ysh)
        for c in range(COMBINE_CHUNKS):
            yt = _gather_back(pos[:, c * tc:(c + 1) * tc], ys)
            out = _combine(x1, ysh, yt, wt_t, p2d, g_ple, w_ple_gate, w_ple_proj, g_ple_post,
                           c * tc, h * th + c * tc, out)
    return out


def kernel(x, p, g_mix, w_in, w_pool_mix, pool_scale, w_branch_a, q_gain, k_gain, attn_sinks,
           w_branch_b, w_gate, b_gate, w_out, g_ffn, w_router, router_bias, w_exp_gate, w_exp_up,
           w_exp_down, w_sh_gate, w_sh_up, w_sh_down, g_ple, w_ple_gate, w_ple_proj, g_ple_post):
    batch, seq, d = x.shape
    depth = p.shape[0]
    x2d = x.reshape(batch * seq, d)
    for i in range(depth):
        x2d = _layer(x2d, p[i].reshape(batch * seq, PLE_DIM), batch, seq, g_mix[i], w_in[i],
                     w_pool_mix[i], pool_scale[i], w_branch_a[i], q_gain[i], k_gain[i], attn_sinks[i],
                     w_branch_b[i], w_gate[i], b_gate[i], w_out[i], g_ffn[i], w_router[i],
                     router_bias[i], w_exp_gate[i], w_exp_up[i], w_exp_down[i], w_sh_gate[i],
                     w_sh_up[i], w_sh_down[i], g_ple[i], w_ple_gate[i], w_ple_proj[i], g_ple_post[i])
    return x2d.reshape(batch, seq, d)
```

```python
import functools

import numpy as np
import jax
import jax.numpy as jnp
from jax import lax
from jax.experimental import pallas as pl
from jax.experimental.pallas import tpu as pltpu
from jax.experimental.pallas import tpu_sc as plsc

F32 = jnp.float32
BF16 = jnp.bfloat16
I32 = jnp.int32

D_MODEL = 1024
PLE_DIM = 256
POOL_WINDOWS = (2, 4, 8, 16)
POOL_GROUP_DIM = 128
POOL_DIM = 512
N_HEADS = 8
N_KV_HEADS = 2
Q_GROUP = 4
HEAD_DIM = 64
HEAD_PAD = 128
Q_DIM = 512
KV_DIM = 128
ATTN_BLOCK = 128
N_EXPERTS = 64
N_EXPERT_GROUPS = 8
GROUP_SIZE = 8
TOPK_GROUPS = 4
TOP_K = 8
EXPERT_DIM = 256
ROUTED_SCALE = 2.5
MOE_BLOCK = 512
EXPERT_SLOTS = 4
EPS = 1e-6
HALF = D_MODEL // 2
MASKED = -1e30

COL_Q = POOL_DIM
COL_K = COL_Q + N_HEADS * HEAD_PAD
COL_V = COL_K + N_KV_HEADS * HEAD_PAD
IN_PAD = COL_V + N_KV_HEADS * HEAD_PAD
HEAD_ORDER = (0, 2, 1, 3)

TM_MIX = 512
TM_FFN = 512
TM_COMBINE = 512
TM_POS = 4096
COMBINE_CHUNKS = 2
PIPELINE_HALVES = 2
SC_SCATTER_ROWS = 128
SC_SUM_ROWS = 8
SC_LANES = 16
POOL_TAIL = 8
VMEM_LIMIT = 56 * 1024 * 1024

_NT = (((1,), (1,)), ((), ()))


def _dot(a, b):
    return jnp.dot(a, b, preferred_element_type=F32)


def _rms(x, g):
    ms = jnp.mean(x * x, axis=-1, keepdims=True)
    return x * lax.rsqrt(ms + EPS) * g


def _sigmoid(x):
    return 0.5 * jnp.tanh(0.5 * x) + 0.5


def _pack(lo, hi):
    lo_bits = lax.bitcast_convert_type(lo.astype(BF16).astype(F32), I32)
    hi_bits = lax.bitcast_convert_type(hi.astype(BF16).astype(F32), I32)
    return (hi_bits & jnp.int32(-65536)) | lax.shift_right_logical(lo_bits, jnp.int32(16))


def _unpack(w):
    lo = lax.bitcast_convert_type(lax.shift_left(w, jnp.int32(16)), F32)
    hi = lax.bitcast_convert_type(w & jnp.int32(-65536), F32)
    return lo, hi


def _mixer_kernel(x_ref, gmix_ref, win_ref, qg_ref, kg_ref, bias_ref, sink_ref, ones_ref, wmix_ref,
                  pscale_ref, wa_ref, wb_ref, wg_ref, bg_ref, wout_ref,
                  gffn_ref, wrh_ref, wrl_ref, rb_ref, tri_ref, *rest):
    (x1_ref, h2p_ref, idx_ref, wt_ref, rk_ref, cnt_ref,
     kbuf, vt_buf, pbuf, zbt_buf, run_ref) = rest[-11:]
    s = pl.program_id(1)
    tm = x_ref.shape[0]

    @pl.when(s == 0)
    def _():
        kbuf[0:ATTN_BLOCK, :] = jnp.zeros((ATTN_BLOCK, kbuf.shape[1]), BF16)
        vt_buf[:, 0:ATTN_BLOCK] = jnp.zeros((vt_buf.shape[0], ATTN_BLOCK), BF16)
        pbuf[:, 0:POOL_TAIL, :] = jnp.zeros((pbuf.shape[0], POOL_TAIL, POOL_DIM), F32)

    x = x_ref[...]
    hb = _rms(x, gmix_ref[...]).astype(BF16)
    u = _dot(hb, win_ref[...])

    up = u[:, 0:POOL_DIM]
    t0 = POOL_TAIL
    pbuf[0, t0:t0 + tm, :] = up
    s2 = up + pbuf[0, t0 - 1:t0 - 1 + tm, :]
    pbuf[1, t0:t0 + tm, 128:512] = s2[:, 128:512]
    s4 = s2[:, 128:512] + pbuf[1, t0 - 2:t0 - 2 + tm, 128:512]
    pbuf[2, t0:t0 + tm, 256:512] = s4[:, 128:384]
    s8 = s4[:, 128:384] + pbuf[2, t0 - 4:t0 - 4 + tm, 256:512]
    pbuf[3, t0:t0 + tm, 384:512] = s8[:, 128:256]
    s16 = s8[:, 128:256] + pbuf[3, t0 - 8:t0 - 8 + tm, 384:512]
    for lvl in range(4):
        pbuf[lvl, 0:POOL_TAIL, 128 * lvl:512] = pbuf[lvl, tm:tm + POOL_TAIL, 128 * lvl:512]
    wsums = (s2[:, 0:128], s4[:, 0:128], s8[:, 0:128], s16)
    tpos = (s * tm).astype(F32) + lax.broadcasted_iota(I32, (tm, 1), 0).astype(F32)
    za_parts = []
    for g, w in enumerate(POOL_WINDOWS):
        inv_count = 1.0 / jnp.minimum(tpos + 1.0, float(w))
        pooled = wsums[g] * inv_count - up[:, 128 * g:128 * (g + 1)]
        mixed = _dot(pooled.astype(BF16), wmix_ref[g]) * pscale_ref[:, 128 * g:128 * (g + 1)]
        za_parts.append(mixed.astype(BF16))
    za = jnp.concatenate(za_parts, axis=1)

    ones2 = ones_ref[...]
    q = u[:, COL_Q:COL_K]
    q2 = (q * q).astype(BF16)
    qss = jnp.concatenate([_dot(q2[:, 256 * c:256 * (c + 1)], ones2) for c in range(N_HEADS // 2)], axis=1)
    qn = (q * lax.rsqrt(qss * (1.0 / HEAD_DIM) + EPS) * qg_ref[...]).astype(BF16)
    k = u[:, COL_K:COL_V]
    kss = _dot((k * k).astype(BF16), ones2)
    kbuf[ATTN_BLOCK:ATTN_BLOCK + tm, :] = (k * lax.rsqrt(kss * (1.0 / HEAD_DIM) + EPS) * kg_ref[...]).astype(BF16)
    vt_buf[:, ATTN_BLOCK:ATTN_BLOCK + tm] = u[:, COL_V:IN_PAD].T.astype(BF16)

    nq = Q_GROUP * ATTN_BLOCK
    key_j = lax.broadcasted_iota(I32, (ATTN_BLOCK, nq), 0)
    qry_i = lax.broadcasted_iota(I32, (ATTN_BLOCK, nq), 1) & (ATTN_BLOCK - 1)
    from_prev = key_j > qry_i
    first = jnp.where(s == 0, 1, 0)
    for n in range(tm // ATTN_BLOCK):
        r0 = ATTN_BLOCK * n
        for kv in range(N_KV_HEADS):
            c0 = HEAD_PAD * kv
            qs = jnp.concatenate(
                [qn[r0:r0 + ATTN_BLOCK, HEAD_PAD * (kv * Q_GROUP + g):HEAD_PAD * (kv * Q_GROUP + g + 1)]
                 for g in HEAD_ORDER], axis=0)
            kk = kbuf[r0:r0 + 2 * ATTN_BLOCK, c0:c0 + HEAD_PAD]
            st = lax.dot_general(kk, qs, _NT, preferred_element_type=F32)
            sc = jnp.where(from_prev, st[0:ATTN_BLOCK], st[ATTN_BLOCK:2 * ATTN_BLOCK])
            sc = sc + (bias_ref[first, kv] if n == 0 else bias_ref[0, kv])
            sink = sink_ref[kv]
            m = jnp.maximum(jnp.max(sc, axis=0, keepdims=True), sink)
            e = jnp.exp(sc - m)
            inv_den = 1.0 / (jnp.sum(e, axis=0, keepdims=True) + jnp.exp(sink - m))
            pt = jnp.concatenate([jnp.where(from_prev, e, 0.0), jnp.where(from_prev, 0.0, e)],
                                 axis=0).astype(BF16)
            v_even = vt_buf[c0:c0 + HEAD_PAD, r0:r0 + 2 * ATTN_BLOCK]
            v_odd = jnp.concatenate([v_even[HEAD_DIM:HEAD_PAD], v_even[0:HEAD_DIM]], axis=0)
            half = 2 * ATTN_BLOCK
            o = (_dot(v_even, pt[:, 0:half]) * inv_den[:, 0:half]
                 + _dot(v_odd, pt[:, half:2 * half]) * inv_den[:, half:2 * half])
            ch = kv * Q_GROUP * HEAD_DIM
            zbt_buf[ch:ch + HEAD_PAD, r0:r0 + ATTN_BLOCK] = o[:, 0:ATTN_BLOCK]
            zbt_buf[ch + HEAD_PAD:ch + 2 * HEAD_PAD, r0:r0 + ATTN_BLOCK] = o[:, ATTN_BLOCK:half]
    kbuf[0:ATTN_BLOCK, :] = kbuf[tm:tm + ATTN_BLOCK, :]
    vt_buf[:, 0:ATTN_BLOCK] = vt_buf[:, tm:tm + ATTN_BLOCK]

    y_a = _dot(za, wa_ref[...])
    y_b = _dot(zbt_buf[...].T.astype(BF16), wb_ref[...])
    g_a = _sigmoid(_dot(hb, wg_ref[:, 0:D_MODEL]) + bg_ref[:, 0:D_MODEL])
    merged = g_a * y_a
    g_b = _sigmoid(_dot(hb, wg_ref[:, D_MODEL:2 * D_MODEL]) + bg_ref[:, D_MODEL:2 * D_MODEL])
    merged = (merged + g_b * y_b).astype(BF16)
    x1 = x + _dot(merged, wout_ref[...])
    x1_ref[...] = x1

    is_first = (pl.program_id(0) == 0) & (s == 0)
    _route_tile(x1, is_first, gffn_ref, wrh_ref, wrl_ref, rb_ref, tri_ref,
                h2p_ref, idx_ref, wt_ref, rk_ref, cnt_ref, run_ref)


def _attn_tables(attn_sinks):
    slopes = 2.0 ** (-8.0 * (np.arange(N_HEADS) + 1) / N_HEADS)
    j = np.arange(ATTN_BLOCK)[:, None]
    i = np.arange(ATTN_BLOCK)[None, :]
    from_prev = j > i
    dist = np.where(from_prev, ATTN_BLOCK + i - j, i - j)
    bias = np.empty((2, N_KV_HEADS, ATTN_BLOCK, Q_GROUP * ATTN_BLOCK), np.float32)
    for first in range(2):
        ok = ~from_prev if first else np.ones_like(from_prev)
        for kv in range(N_KV_HEADS):
            for slot, g in enumerate(HEAD_ORDER):
                sl = np.float32(slopes[kv * Q_GROUP + g])
                val = -(sl * dist.astype(np.float32))
                bias[first, kv, :, slot * ATTN_BLOCK:(slot + 1) * ATTN_BLOCK] = np.where(ok, val, MASKED)
    sink = attn_sinks.astype(F32).reshape(N_KV_HEADS, Q_GROUP)[:, np.asarray(HEAD_ORDER)]
    sink = jnp.repeat(sink, ATTN_BLOCK, axis=1)
    return jnp.asarray(bias), sink.reshape(N_KV_HEADS, 1, Q_GROUP * ATTN_BLOCK)


def _pad_heads(w, n_heads):
    k = w.shape[0]
    w = w.reshape(k, n_heads, HEAD_DIM)
    w = jnp.pad(w, ((0, 0), (0, 0), (0, HEAD_PAD - HEAD_DIM)))
    return w.reshape(k, n_heads * HEAD_PAD)


def _const_spec(shape):
    nd = len(shape)
    return pl.BlockSpec(shape, lambda *_: (0,) * nd)


def _mixer(x2d, batch0, batch, seq, g_mix, w_in, w_pool_mix, pool_scale, w_branch_a, q_gain, k_gain,
           attn_sinks, w_branch_b, w_gate, b_gate, w_out, g_ffn, w_router, router_bias, after=None):
    t = batch * seq
    tm = TM_MIX
    ns = seq // tm
    w_q = w_in[:, POOL_DIM:POOL_DIM + Q_DIM]
    w_k = w_in[:, POOL_DIM + Q_DIM:POOL_DIM + Q_DIM + KV_DIM]
    w_v = w_in[:, POOL_DIM + Q_DIM + KV_DIM:]
    win_p = jnp.concatenate([w_in[:, :POOL_DIM], _pad_heads(w_q, N_HEADS), _pad_heads(w_k, N_KV_HEADS),
                             _pad_heads(w_v, N_KV_HEADS)], axis=1).astype(BF16)
    qg = jnp.tile(jnp.pad(q_gain * (HEAD_DIM ** -0.5), (0, HEAD_PAD - HEAD_DIM)), N_HEADS).reshape(1, -1)
    kg = jnp.tile(jnp.pad(k_gain, (0, HEAD_PAD - HEAD_DIM)), N_KV_HEADS).reshape(1, -1)
    bias, sink = _attn_tables(attn_sinks)
    ones2 = jnp.asarray(np.kron(np.eye(2), np.ones((HEAD_PAD, HEAD_PAD))), BF16)
    operands = (x2d, g_mix.reshape(1, D_MODEL), win_p, qg, kg, bias, sink, ones2, w_pool_mix.astype(BF16),
                pool_scale.reshape(1, POOL_DIM), w_branch_a.astype(BF16), w_branch_b.astype(BF16),
                w_gate.astype(BF16), b_gate.reshape(1, 2 * D_MODEL), w_out.astype(BF16))
    operands += _router_operands(g_ffn, w_router, router_bias, tm)
    in_specs = [pl.BlockSpec((tm, D_MODEL), lambda b, s: ((batch0 + b) * ns + s, 0))]
    in_specs += [_const_spec(op.shape) for op in operands[1:]]
    if after is not None:
        operands += (after,)
        in_specs.append(pl.BlockSpec(memory_space=pl.ANY))
    row8 = pl.BlockSpec((TOP_K, tm), lambda b, s: (0, b * ns + s))
    return pl.pallas_call(
        _mixer_kernel,
        grid=(batch, ns),
        in_specs=in_specs,
        out_specs=[pl.BlockSpec((tm, D_MODEL), lambda b, s: (b * ns + s, 0)),
                   pl.BlockSpec((tm, HALF), lambda b, s: (b * ns + s, 0)),
                   row8,
                   pl.BlockSpec((tm, TOP_K * SC_LANES), lambda b, s: (b * ns + s, 0)),
                   row8,
                   pl.BlockSpec((N_EXPERTS, 128), lambda b, s: (0, 0))],
        out_shape=[jax.ShapeDtypeStruct((t, D_MODEL), F32),
                   jax.ShapeDtypeStruct((t, HALF), I32),
                   jax.ShapeDtypeStruct((TOP_K, t), I32),
                   jax.ShapeDtypeStruct((t, TOP_K * SC_LANES), F32),
                   jax.ShapeDtypeStruct((TOP_K, t), I32),
                   jax.ShapeDtypeStruct((N_EXPERTS, 128), F32)],
        scratch_shapes=[
            pltpu.VMEM((ATTN_BLOCK + tm, N_KV_HEADS * HEAD_PAD), BF16),
            pltpu.VMEM((N_KV_HEADS * HEAD_PAD, ATTN_BLOCK + tm), BF16),
            pltpu.VMEM((4, POOL_TAIL + tm, POOL_DIM), F32),
            pltpu.VMEM((Q_DIM, tm), F32),
            pltpu.VMEM((N_EXPERTS, 128), F32),
        ],
        compiler_params=pltpu.CompilerParams(
            dimension_semantics=("arbitrary", "arbitrary"), vmem_limit_bytes=VMEM_LIMIT),
        name="mixer",
    )(*operands)


def _route_tile(x1, is_first, gffn_ref, wrh_ref, wrl_ref, rb_ref, tri_ref,
                h2p_ref, idx_ref, wt_ref, rk_ref, cnt_ref, run_ref):
    tm = x1.shape[0]

    @pl.when(is_first)
    def _():
        run_ref[...] = jnp.zeros_like(run_ref)

    h2 = _rms(x1, gffn_ref[...])
    hb = h2.astype(BF16)
    h2p_ref[...] = _pack(h2[:, 0:HALF], h2[:, HALF:D_MODEL])

    lo = (h2 - hb.astype(F32)).astype(BF16)
    wrh = wrh_ref[...]
    logits = (lax.dot_general(wrh, hb, _NT, preferred_element_type=F32)
              + lax.dot_general(wrl_ref[...], hb, _NT, preferred_element_type=F32)
              + lax.dot_general(wrh, lo, _NT, preferred_element_type=F32))
    scores = _sigmoid(logits)
    choice = scores + rb_ref[...]

    neg = -jnp.inf
    sub8 = lax.broadcasted_iota(I32, (GROUP_SIZE, tm), 0).astype(F32)
    grp_rows = []
    for gi in range(N_EXPERT_GROUPS):
        cg = choice[GROUP_SIZE * gi:GROUP_SIZE * (gi + 1)]
        m1 = jnp.max(cg, axis=0, keepdims=True)
        first = jnp.min(jnp.where(cg == m1, sub8, float(GROUP_SIZE)), axis=0, keepdims=True)
        m2 = jnp.max(jnp.where(sub8 == first, neg, cg), axis=0, keepdims=True)
        grp_rows.append(m1 + m2)
    gs = jnp.concatenate(grp_rows, axis=0)
    beaten = jnp.zeros((N_EXPERT_GROUPS, tm), F32)
    for gi in range(N_EXPERT_GROUPS):
        row = grp_rows[gi]
        wins = (row > gs) | ((row == gs) & (sub8 > float(gi)))
        beaten = beaten + jnp.where(wins, 1.0, 0.0)
    gsel = beaten < float(TOPK_GROUPS)
    cm = jnp.concatenate(
        [jnp.where(gsel[gi:gi + 1], choice[GROUP_SIZE * gi:GROUP_SIZE * (gi + 1)], neg)
         for gi in range(N_EXPERT_GROUPS)], axis=0)

    iota_e = lax.broadcasted_iota(I32, (N_EXPERTS, tm), 0).astype(F32)
    idx_rows, s_rows = [], []
    sel = jnp.zeros((N_EXPERTS, tm), F32)
    for _ in range(TOP_K):
        m = jnp.max(cm, axis=0, keepdims=True)
        idx = jnp.min(jnp.where(cm == m, iota_e, float(N_EXPERTS)), axis=0, keepdims=True)
        oh = iota_e == idx
        s_rows.append(jnp.sum(jnp.where(oh, scores, 0.0), axis=0, keepdims=True))
        idx_rows.append(idx)
        cm = jnp.where(oh, neg, cm)
        sel = sel + jnp.where(oh, 1.0, 0.0)

    run = run_ref[:, 0:1]
    cum = _dot(sel.astype(BF16), tri_ref[...])
    before = run + cum - sel
    rk_rows = [jnp.sum(jnp.where(iota_e == idx, before, 0.0), axis=0, keepdims=True) for idx in idx_rows]
    new_run = run + jnp.sum(sel, axis=1, keepdims=True)
    run_ref[...] = jnp.broadcast_to(new_run, run_ref.shape)
    cnt_ref[...] = jnp.broadcast_to(new_run, cnt_ref.shape)

    ssum = s_rows[0]
    for r in range(1, TOP_K):
        ssum = ssum + s_rows[r]
    denom = ssum + 1e-20
    idx_ref[...] = jnp.concatenate(idx_rows, axis=0).astype(I32)
    w_rep = jnp.concatenate([jnp.broadcast_to(sr / denom * ROUTED_SCALE, (SC_LANES, tm)) for sr in s_rows],
                            axis=0)
    wt_ref[...] = w_rep.T
    rk_ref[...] = jnp.concatenate(rk_rows, axis=0).astype(I32)


def _router_operands(g_ffn, w_router, router_bias, tm):
    wr_t = w_router.astype(F32).T
    wr_hi = wr_t.astype(BF16)
    wr_lo = (wr_t - wr_hi.astype(F32)).astype(BF16)
    tri = (np.arange(tm)[:, None] <= np.arange(tm)[None, :]).astype(np.float32)
    return (g_ffn.reshape(1, D_MODEL), wr_hi, wr_lo, router_bias.astype(F32).reshape(N_EXPERTS, 1),
            jnp.asarray(tri, BF16))


def _swiglu_packed(x_packed, wg, wu, wd):
    lo, hi = _unpack(x_packed)
    lo = lo.astype(BF16)
    hi = hi.astype(BF16)
    g = _dot(lo, wg[0:HALF, :]) + _dot(hi, wg[HALF:D_MODEL, :])
    u = _dot(lo, wu[0:HALF, :]) + _dot(hi, wu[HALF:D_MODEL, :])
    a = (g * _sigmoid(g) * u).astype(BF16)
    y = _dot(a, wd[...])
    return _pack(y[:, 0:HALF], y[:, HALF:D_MODEL])


def _shared_kernel(h2p_ref, wg_ref, wu_ref, wd_ref, ysh_ref):
    ysh_ref[...] = _swiglu_packed(h2p_ref[...], wg_ref, wu_ref, wd_ref)


def _shared(h2p, w_sh_gate, w_sh_up, w_sh_down):
    t = h2p.shape[0]
    tm = TM_FFN
    operands = (h2p, w_sh_gate.astype(BF16), w_sh_up.astype(BF16), w_sh_down.astype(BF16))
    in_specs = [pl.BlockSpec((tm, HALF), lambda i: (i, 0))]
    in_specs += [_const_spec(op.shape) for op in operands[1:]]
    return pl.pallas_call(
        _shared_kernel,
        grid=(t // tm,),
        in_specs=in_specs,
        out_specs=pl.BlockSpec((tm, HALF), lambda i: (i, 0)),
        out_shape=jax.ShapeDtypeStruct((t, HALF), I32),
        compiler_params=pltpu.CompilerParams(
            dimension_semantics=("arbitrary",), vmem_limit_bytes=VMEM_LIMIT),
        name="shared",
    )(*operands)


def _positions_kernel(pstart_ref, idx_ref, rk_ref, pos_ref):
    idx = idx_ref[...]
    acc = rk_ref[...]
    for e in range(N_EXPERTS):
        acc = acc + jnp.where(idx == e, pstart_ref[e], 0)
    pos_ref[...] = acc


def _positions(pstart, idx8, rk8):
    t = idx8.shape[1]
    tm = TM_POS
    spec = pl.BlockSpec((TOP_K, tm), lambda i, ps: (0, i))
    return pl.pallas_call(
        _positions_kernel,
        grid_spec=pltpu.PrefetchScalarGridSpec(num_scalar_prefetch=1, grid=(t // tm,),
                                               in_specs=[spec, spec], out_specs=spec),
        out_shape=jax.ShapeDtypeStruct((TOP_K, t), I32),
        name="positions",
    )(pstart, idx8, rk8)


def _sc_workers():
    info = plsc.get_sparse_core_info()
    return info.num_cores, info.num_cores * info.num_subcores


def _dispatch(pos_flat, h2p, n_rows):
    t = h2p.shape[0]
    ch = SC_SCATTER_ROWS
    n_cores, n_workers = _sc_workers()
    n_ch = t // (n_workers * ch)

    def body(pos_hbm, h_hbm, xs_hbm, idx_v, rows_v, sem):
        wid = lax.axis_index("s") * n_cores + lax.axis_index("c")

        @pl.loop(0, n_ch)
        def _(c):
            chunk = wid * n_ch + c
            pltpu.sync_copy(h_hbm.at[pl.ds(chunk * ch, ch)], rows_v)
            for k in range(TOP_K):
                pltpu.sync_copy(pos_hbm.at[pl.ds(k * t + chunk * ch, ch)], idx_v.at[k])
            copies = [pltpu.async_copy(rows_v, xs_hbm.at[idx_v.at[k]], sem) for k in range(TOP_K)]
            for cp in copies:
                cp.wait()

    return pl.kernel(
        body,
        out_type=jax.ShapeDtypeStruct((n_rows, HALF), I32),
        mesh=plsc.VectorSubcoreMesh(core_axis_name="c", subcore_axis_name="s"),
        scratch_types=[pltpu.VMEM((TOP_K, ch), I32), pltpu.VMEM((ch, HALF), I32),
                       pltpu.SemaphoreType.DMA],
        name="dispatch",
    )(pos_flat, h2p)


def _pairwise_sum(xs):
    while len(xs) > 1:
        xs = [xs[i] + xs[i + 1] for i in range(0, len(xs), 2)]
    return xs[0]


def _gather_sum(pos_flat, w_rep, ys, tok0, t):
    t_all = w_rep.shape[0]
    ch = SC_SUM_ROWS
    lanes = SC_LANES
    n_cores, n_workers = _sc_workers()
    per_w = t // n_workers
    n_ch = per_w // ch

    def body(pos_hbm, w_hbm, ys_hbm, out_hbm, idx_all, w_all, buf0, buf1, acc0, acc1, gsem0, gsem1, osem0, osem1):
        wid = lax.axis_index("s") * n_cores + lax.axis_index("c")
        base = wid * n_ch
        first = tok0 + wid * per_w
        bufs, accs, gsems, osems = (buf0, buf1), (acc0, acc1), (gsem0, gsem1), (osem0, osem1)
        for k in range(TOP_K):
            pltpu.sync_copy(pos_hbm.at[pl.ds(k * t_all + first, per_w)], idx_all.at[pl.ds(k * per_w, per_w)])
        pltpu.sync_copy(w_hbm.at[pl.ds(first, per_w)], w_all)

        def row_copy(c, s, k):
            rows = idx_all.at[pl.ds(k * per_w + c * ch, ch)]
            return pltpu.make_async_copy(ys_hbm.at[rows], bufs[s].at[k], gsems[s])

        def out_copy(c, s):
            return pltpu.make_async_copy(accs[s], out_hbm.at[pl.ds((base + c) * ch, ch)], osems[s])

        def fetch(c, s):
            for k in range(TOP_K):
                row_copy(c, s, k).start()

        def reduce_rows(c, s):
            @pl.loop(0, ch)
            def _(r):
                wks = [w_all[c * ch + r, pl.ds(k * lanes, lanes)] for k in range(TOP_K)]

                @pl.loop(0, HALF // lanes, step=2)
                def _(v):
                    for dv in range(2):
                        col = (v + dv) * lanes
                        los, his = [], []
                        for k in range(TOP_K):
                            words = bufs[s][k, r, pl.ds(col, lanes)]
                            los.append(wks[k] * lax.bitcast_convert_type(lax.shift_left(words, jnp.int32(16)), F32))
                            his.append(wks[k] * lax.bitcast_convert_type(words & jnp.int32(-65536), F32))
                        accs[s][r, pl.ds(col, lanes)] = _pairwise_sum(los)
                        accs[s][r, pl.ds(HALF + col, lanes)] = _pairwise_sum(his)

        fetch(0, 0)

        @pl.loop(0, n_ch, step=2)
        def _(c):
            for s in range(2):
                cc = c + s

                @pl.when(cc + 1 < n_ch)
                def _():
                    fetch(cc + 1, 1 - s)

                for k in range(TOP_K):
                    row_copy(cc, s, k).wait()

                @pl.when(cc >= 2)
                def _():
                    out_copy(cc - 2, s).wait()

                reduce_rows(cc, s)
                out_copy(cc, s).start()

        out_copy(n_ch - 2, 0).wait()
        out_copy(n_ch - 1, 1).wait()

    return pl.kernel(
        body,
        out_type=jax.ShapeDtypeStruct((t, D_MODEL), F32),
        mesh=plsc.VectorSubcoreMesh(core_axis_name="c", subcore_axis_name="s"),
        scratch_types=[pltpu.VMEM((TOP_K * per_w,), I32), pltpu.VMEM((per_w, TOP_K * lanes), F32),
                       pltpu.VMEM((TOP_K, ch, HALF), I32), pltpu.VMEM((TOP_K, ch, HALF), I32),
                       pltpu.VMEM((ch, D_MODEL), F32), pltpu.VMEM((ch, D_MODEL), F32),
                       pltpu.SemaphoreType.DMA, pltpu.SemaphoreType.DMA,
                       pltpu.SemaphoreType.DMA, pltpu.SemaphoreType.DMA],
        compiler_params=pltpu.CompilerParams(needs_layout_passes=False),
        name="gather_sum",
    )(pos_flat, w_rep, ys)


def _expert_kernel(nblk_ref, blk0_ref, cnt_ref, wg_ref, wu_ref, wd_ref, xs_hbm, after_ref, ys_hbm,
                   wg_s, wu_s, wd_s, xbuf, ybuf, in_sem, out_sem):
    del after_ref
    e = pl.program_id(0)
    nb = nblk_ref[e]
    b0 = blk0_ref[e]
    cnt = cnt_ref[e]
    n_all = blk0_ref[N_EXPERTS]
    ahead = EXPERT_SLOTS - 1
    wg_s[...] = wg_ref[0].astype(BF16)
    wu_s[...] = wu_ref[0].astype(BF16)
    wd_s[...] = wd_ref[0].astype(BF16)

    def slot_of(g):
        return g & (EXPERT_SLOTS - 1)

    def rows_of(g):
        return pl.ds(pl.multiple_of(g * MOE_BLOCK, MOE_BLOCK), MOE_BLOCK)

    def in_copy(g):
        return pltpu.make_async_copy(xs_hbm.at[rows_of(g)], xbuf.at[slot_of(g)], in_sem.at[slot_of(g)])

    def out_copy(g):
        return pltpu.make_async_copy(ybuf.at[slot_of(g)], ys_hbm.at[rows_of(g)], out_sem.at[slot_of(g)])

    @pl.when(e == 0)
    def _():
        for g in range(ahead):
            @pl.when(g < n_all)
            def _():
                in_copy(g).start()

    def arrive(g):
        in_copy(g).wait()

        @pl.when(g + ahead < n_all)
        def _():
            in_copy(g + ahead).start()

        @pl.when(g >= EXPERT_SLOTS)
        def _():
            out_copy(g - EXPERT_SLOTS).wait()

    def ffn(j):
        g = b0 + j
        rows = lax.broadcasted_iota(I32, (MOE_BLOCK, HALF), 0)
        x = jnp.where(rows < cnt - j * MOE_BLOCK, xbuf[slot_of(g)], 0)
        return _swiglu_packed(x, wg_s, wu_s, wd_s)

    def one_block(j, carry):
        g = b0 + j
        arrive(g)
        ybuf[slot_of(g)] = ffn(j)
        out_copy(g).start()
        return carry

    lax.fori_loop(0, nb, one_block, 0)

    @pl.when(e == N_EXPERTS - 1)
    def _():
        for back in range(1, EXPERT_SLOTS + 1):
            @pl.when(n_all >= back)
            def _():
                out_copy(n_all - back).wait()


def _experts(n_blk, blk0, counts, n_rows, xs, w_gate, w_up, w_down, after):
    wspec_in = pl.BlockSpec((1, D_MODEL, EXPERT_DIM), lambda e, *_: (e, 0, 0))
    grid_spec = pltpu.PrefetchScalarGridSpec(
        num_scalar_prefetch=3,
        grid=(N_EXPERTS,),
        in_specs=[wspec_in, wspec_in,
                  pl.BlockSpec((1, EXPERT_DIM, D_MODEL), lambda e, *_: (e, 0, 0)),
                  pl.BlockSpec(memory_space=pl.ANY),
                  pl.BlockSpec(memory_space=pl.ANY)],
        out_specs=pl.BlockSpec(memory_space=pl.ANY),
        scratch_shapes=[pltpu.VMEM((D_MODEL, EXPERT_DIM), BF16), pltpu.VMEM((D_MODEL, EXPERT_DIM), BF16),
                        pltpu.VMEM((EXPERT_DIM, D_MODEL), BF16),
                        pltpu.VMEM((EXPERT_SLOTS, MOE_BLOCK, HALF), I32),
                        pltpu.VMEM((EXPERT_SLOTS, MOE_BLOCK, HALF), I32),
                        pltpu.SemaphoreType.DMA((EXPERT_SLOTS,)), pltpu.SemaphoreType.DMA((EXPERT_SLOTS,))],
    )
    return pl.pallas_call(
        _expert_kernel,
        grid_spec=grid_spec,
        out_shape=jax.ShapeDtypeStruct((n_rows, HALF), I32),
        compiler_params=pltpu.CompilerParams(
            dimension_semantics=("arbitrary",), vmem_limit_bytes=VMEM_LIMIT),
        name="experts",
    )(n_blk, blk0, counts, w_gate, w_up, w_down, xs, after)


def _combine_kernel(x1_ref, ysh_ref, moe_ref, p_ref, gple_ref, wpg_ref, wpp_ref, gpost_ref, *rest):
    out_ref = rest[-1]
    proj = _rms(_dot(p_ref[...].astype(BF16), wpp_ref[...]), gpost_ref[...])
    sh_lo, sh_hi = _unpack(ysh_ref[...])
    x2 = x1_ref[...] + moe_ref[...] + jnp.concatenate([sh_lo, sh_hi], axis=1)
    gate = _sigmoid(_dot(_rms(x2, gple_ref[...]).astype(BF16), wpg_ref[...]))
    out_ref[...] = x2 + gate * proj


def _combine(x1, ysh, moe, p2d, g_ple, w_ple_gate, w_ple_proj, g_ple_post, tok0, tok0_all, prev_out):
    t = p2d.shape[0]
    tm = TM_COMBINE
    steps = moe.shape[0] // tm
    off = tok0 // tm
    off_all = tok0_all // tm
    operands = [x1, ysh, moe, p2d, g_ple.reshape(1, D_MODEL), w_ple_gate.astype(BF16),
                w_ple_proj.astype(BF16), g_ple_post.reshape(1, D_MODEL)]
    in_specs = [pl.BlockSpec((tm, D_MODEL), lambda i: (off + i, 0)),
                pl.BlockSpec((tm, HALF), lambda i: (off + i, 0)),
                pl.BlockSpec((tm, D_MODEL), lambda i: (i, 0)),
                pl.BlockSpec((tm, PLE_DIM), lambda i: (off_all + i, 0)),
                _const_spec((1, D_MODEL)),
                _const_spec((D_MODEL, D_MODEL)),
                _const_spec((PLE_DIM, D_MODEL)),
                _const_spec((1, D_MODEL))]
    aliases = {}
    if prev_out is not None:
        aliases = {len(operands): 0}
        operands.append(prev_out)
        in_specs.append(pl.BlockSpec(memory_space=pl.ANY))
    return pl.pallas_call(
        _combine_kernel,
        grid=(steps,),
        in_specs=in_specs,
        out_specs=pl.BlockSpec((tm, D_MODEL), lambda i: (off_all + i, 0)),
        out_shape=jax.ShapeDtypeStruct((t, D_MODEL), F32),
        input_output_aliases=aliases,
        compiler_params=pltpu.CompilerParams(
            dimension_semantics=("arbitrary",), vmem_limit_bytes=VMEM_LIMIT),
        name="combine",
    )(*operands)


def _layer(x2d, p2d, batch, seq, g_mix, w_in, w_pool_mix, pool_scale, w_branch_a, q_gain, k_gain,
           attn_sinks, w_branch_b, w_gate, b_gate, w_out, g_ffn, w_router, router_bias, w_exp_gate,
           w_exp_up, w_exp_down, w_sh_gate, w_sh_up, w_sh_down, g_ple, w_ple_gate, w_ple_proj,
           g_ple_post):
    bh = batch // PIPELINE_HALVES
    th = bh * seq
    n_rows = (-(-(th * TOP_K) // MOE_BLOCK) + N_EXPERTS) * MOE_BLOCK

    halves = []
    after = None
    for h in range(PIPELINE_HALVES):
        x1, h2p, idx8, w_rep, rk8, cnt = _mixer(
            x2d, h * bh, bh, seq, g_mix, w_in, w_pool_mix, pool_scale, w_branch_a, q_gain, k_gain,
            attn_sinks, w_branch_b, w_gate, b_gate, w_out, g_ffn, w_router, router_bias, after=after)
        counts = cnt[:, 0].astype(I32)
        padded = (counts + MOE_BLOCK - 1) // MOE_BLOCK * MOE_BLOCK
        pend = jnp.cumsum(padded)
        pstart = pend - padded
        pos = _positions(pstart.astype(I32), idx8, rk8).reshape(-1)
        xs = _dispatch(pos, h2p, n_rows)
        blk0 = jnp.concatenate([pstart, pend[-1:]]) // MOE_BLOCK
        halves.append((x1, h2p, w_rep, pos, xs, (padded // MOE_BLOCK).astype(I32), blk0.astype(I32), counts))
        after = pos

    out = None
    tc = th // COMBINE_CHUNKS
    for h, (x1, h2p, w_rep, pos, xs, n_blk, blk0, counts) in enumerate(halves):
        ysh = _shared(h2p, w_sh_gate, w_sh_up, w_sh_down)
        ys = _experts(n_blk, blk0, counts, n_rows, xs, w_exp_gate, w_exp_up, w_exp_down, ysh)
        for c in range(COMBINE_CHUNKS):
            moe = _gather_sum(pos, w_rep, ys, c * tc, tc)
            out = _combine(x1, ysh, moe, p2d, g_ple, w_ple_gate, w_ple_proj, g_ple_post,
                           c * tc, h * th + c * tc, out)
    return out


def kernel(x, p, g_mix, w_in, w_pool_mix, pool_scale, w_branch_a, q_gain, k_gain, attn_sinks,
           w_branch_b, w_gate, b_gate, w_out, g_ffn, w_router, router_bias, w_exp_gate, w_exp_up,
           w_exp_down, w_sh_gate, w_sh_up, w_sh_down, g_ple, w_ple_gate, w_ple_proj, g_ple_post):
    batch, seq, d = x.shape
    depth = p.shape[0]
    x2d = x.reshape(batch * seq, d)
    for i in range(depth):
        x2d = _layer(x2d, p[i].reshape(batch * seq, PLE_DIM), batch, seq, g_mix[i], w_in[i],
                     w_pool_mix[i], pool_scale[i], w_branch_a[i], q_gain[i], k_gain[i], attn_sinks[i],
                     w_branch_b[i], w_gate[i], b_gate[i], w_out[i], g_ffn[i], w_router[i],
                     router_bias[i], w_exp_gate[i], w_exp_up[i], w_exp_down[i], w_sh_gate[i],
                     w_sh_up[i], w_sh_down[i], g_ple[i], w_ple_gate[i], w_ple_proj[i], g_ple_post[i])
    return x2d.reshape(batch, seq, d)
```

```python
import functools

import numpy as np
import jax
import jax.numpy as jnp
from jax import lax
from jax.experimental import pallas as pl
from jax.experimental.pallas import tpu as pltpu
from jax.experimental.pallas import tpu_sc as plsc

F32 = jnp.float32
BF16 = jnp.bfloat16
I32 = jnp.int32

D_MODEL = 1024
PLE_DIM = 256
POOL_WINDOWS = (2, 4, 8, 16)
POOL_GROUP_DIM = 128
POOL_DIM = 512
N_HEADS = 8
N_KV_HEADS = 2
Q_GROUP = 4
HEAD_DIM = 64
HEAD_PAD = 128
Q_DIM = 512
KV_DIM = 128
ATTN_BLOCK = 128
N_EXPERTS = 64
N_EXPERT_GROUPS = 8
GROUP_SIZE = 8
TOPK_GROUPS = 4
TOP_K = 8
EXPERT_DIM = 256
ROUTED_SCALE = 2.5
MOE_BLOCK = 512
EXPERT_SLOTS = 4
EPS = 1e-6
HALF = D_MODEL // 2
MASKED = -1e30

COL_Q = POOL_DIM
COL_K = COL_Q + N_HEADS * HEAD_PAD
COL_V = COL_K + N_KV_HEADS * HEAD_PAD
IN_PAD = COL_V + N_KV_HEADS * HEAD_PAD
HEAD_ORDER = (0, 2, 1, 3)

TM_MIX = 512
TM_FFN = 512
TM_COMBINE = 512
TM_POS = 4096
COMBINE_CHUNKS = 4
PIPELINE_HALVES = 2
SC_SCATTER_ROWS = 128
SC_SUM_ROWS = 8
SC_LANES = 16
POOL_TAIL = 8
VMEM_LIMIT = 56 * 1024 * 1024

_NT = (((1,), (1,)), ((), ()))


def _dot(a, b):
    return jnp.dot(a, b, preferred_element_type=F32)


def _rms(x, g):
    ms = jnp.mean(x * x, axis=-1, keepdims=True)
    return x * lax.rsqrt(ms + EPS) * g


def _sigmoid(x):
    return 0.5 * jnp.tanh(0.5 * x) + 0.5


def _pack(lo, hi):
    lo_bits = lax.bitcast_convert_type(lo.astype(BF16).astype(F32), I32)
    hi_bits = lax.bitcast_convert_type(hi.astype(BF16).astype(F32), I32)
    return (hi_bits & jnp.int32(-65536)) | lax.shift_right_logical(lo_bits, jnp.int32(16))


def _unpack(w):
    lo = lax.bitcast_convert_type(lax.shift_left(w, jnp.int32(16)), F32)
    hi = lax.bitcast_convert_type(w & jnp.int32(-65536), F32)
    return lo, hi


def _mixer_kernel(x_ref, gmix_ref, win_ref, qg_ref, kg_ref, bias_ref, sink_ref, ones_ref, wmix_ref,
                  pscale_ref, wa_ref, wb_ref, wg_ref, bg_ref, wout_ref,
                  gffn_ref, wrh_ref, wrl_ref, rb_ref, tri_ref, *rest):
    (x1_ref, h2p_ref, idx_ref, wt_ref, rk_ref, cnt_ref,
     kbuf, vt_buf, pbuf, zbt_buf, run_ref) = rest[-11:]
    s = pl.program_id(1)
    tm = x_ref.shape[0]

    @pl.when(s == 0)
    def _():
        kbuf[0:ATTN_BLOCK, :] = jnp.zeros((ATTN_BLOCK, kbuf.shape[1]), BF16)
        vt_buf[:, 0:ATTN_BLOCK] = jnp.zeros((vt_buf.shape[0], ATTN_BLOCK), BF16)
        pbuf[:, 0:POOL_TAIL, :] = jnp.zeros((pbuf.shape[0], POOL_TAIL, POOL_DIM), F32)

    x = x_ref[...]
    hb = _rms(x, gmix_ref[...]).astype(BF16)
    u = _dot(hb, win_ref[...])

    up = u[:, 0:POOL_DIM]
    t0 = POOL_TAIL
    pbuf[0, t0:t0 + tm, :] = up
    s2 = up + pbuf[0, t0 - 1:t0 - 1 + tm, :]
    pbuf[1, t0:t0 + tm, 128:512] = s2[:, 128:512]
    s4 = s2[:, 128:512] + pbuf[1, t0 - 2:t0 - 2 + tm, 128:512]
    pbuf[2, t0:t0 + tm, 256:512] = s4[:, 128:384]
    s8 = s4[:, 128:384] + pbuf[2, t0 - 4:t0 - 4 + tm, 256:512]
    pbuf[3, t0:t0 + tm, 384:512] = s8[:, 128:256]
    s16 = s8[:, 128:256] + pbuf[3, t0 - 8:t0 - 8 + tm, 384:512]
    for lvl in range(4):
        pbuf[lvl, 0:POOL_TAIL, 128 * lvl:512] = pbuf[lvl, tm:tm + POOL_TAIL, 128 * lvl:512]
    wsums = (s2[:, 0:128], s4[:, 0:128], s8[:, 0:128], s16)
    tpos = (s * tm).astype(F32) + lax.broadcasted_iota(I32, (tm, 1), 0).astype(F32)
    za_parts = []
    for g, w in enumerate(POOL_WINDOWS):
        inv_count = 1.0 / jnp.minimum(tpos + 1.0, float(w))
        pooled = wsums[g] * inv_count - up[:, 128 * g:128 * (g + 1)]
        mixed = _dot(pooled.astype(BF16), wmix_ref[g]) * pscale_ref[:, 128 * g:128 * (g + 1)]
        za_parts.append(mixed.astype(BF16))
    za = jnp.concatenate(za_parts, axis=1)

    ones2 = ones_ref[...]
    q = u[:, COL_Q:COL_K]
    q2 = (q * q).astype(BF16)
    qss = jnp.concatenate([_dot(q2[:, 256 * c:256 * (c + 1)], ones2) for c in range(N_HEADS // 2)], axis=1)
    qn = (q * lax.rsqrt(qss * (1.0 / HEAD_DIM) + EPS) * qg_ref[...]).astype(BF16)
    k = u[:, COL_K:COL_V]
    kss = _dot((k * k).astype(BF16), ones2)
    kbuf[ATTN_BLOCK:ATTN_BLOCK + tm, :] = (k * lax.rsqrt(kss * (1.0 / HEAD_DIM) + EPS) * kg_ref[...]).astype(BF16)
    vt_buf[:, ATTN_BLOCK:ATTN_BLOCK + tm] = u[:, COL_V:IN_PAD].T.astype(BF16)

    nq = Q_GROUP * ATTN_BLOCK
    key_j = lax.broadcasted_iota(I32, (ATTN_BLOCK, nq), 0)
    qry_i = lax.broadcasted_iota(I32, (ATTN_BLOCK, nq), 1) & (ATTN_BLOCK - 1)
    from_prev = key_j > qry_i
    first = jnp.where(s == 0, 1, 0)
    for n in range(tm // ATTN_BLOCK):
        r0 = ATTN_BLOCK * n
        for kv in range(N_KV_HEADS):
            c0 = HEAD_PAD * kv
            qs = jnp.concatenate(
                [qn[r0:r0 + ATTN_BLOCK, HEAD_PAD * (kv * Q_GROUP + g):HEAD_PAD * (kv * Q_GROUP + g + 1)]
                 for g in HEAD_ORDER], axis=0)
            kk = kbuf[r0:r0 + 2 * ATTN_BLOCK, c0:c0 + HEAD_PAD]
            st = lax.dot_general(kk, qs, _NT, preferred_element_type=F32)
            sc = jnp.where(from_prev, st[0:ATTN_BLOCK], st[ATTN_BLOCK:2 * ATTN_BLOCK])
            sc = sc + (bias_ref[first, kv] if n == 0 else bias_ref[0, kv])
            sink = sink_ref[kv]
            m = jnp.maximum(jnp.max(sc, axis=0, keepdims=True), sink)
            e = jnp.exp(sc - m)
            inv_den = 1.0 / (jnp.sum(e, axis=0, keepdims=True) + jnp.exp(sink - m))
            pt = jnp.concatenate([jnp.where(from_prev, e, 0.0), jnp.where(from_prev, 0.0, e)],
                                 axis=0).astype(BF16)
            v_even = vt_buf[c0:c0 + HEAD_PAD, r0:r0 + 2 * ATTN_BLOCK]
            v_odd = jnp.concatenate([v_even[HEAD_DIM:HEAD_PAD], v_even[0:HEAD_DIM]], axis=0)
            half = 2 * ATTN_BLOCK
            o = (_dot(v_even, pt[:, 0:half]) * inv_den[:, 0:half]
                 + _dot(v_odd, pt[:, half:2 * half]) * inv_den[:, half:2 * half])
            ch = kv * Q_GROUP * HEAD_DIM
            zbt_buf[ch:ch + HEAD_PAD, r0:r0 + ATTN_BLOCK] = o[:, 0:ATTN_BLOCK]
            zbt_buf[ch + HEAD_PAD:ch + 2 * HEAD_PAD, r0:r0 + ATTN_BLOCK] = o[:, ATTN_BLOCK:half]
    kbuf[0:ATTN_BLOCK, :] = kbuf[tm:tm + ATTN_BLOCK, :]
    vt_buf[:, 0:ATTN_BLOCK] = vt_buf[:, tm:tm + ATTN_BLOCK]

    y_a = _dot(za, wa_ref[...])
    y_b = _dot(zbt_buf[...].T.astype(BF16), wb_ref[...])
    g_a = _sigmoid(_dot(hb, wg_ref[:, 0:D_MODEL]) + bg_ref[:, 0:D_MODEL])
    merged = g_a * y_a
    g_b = _sigmoid(_dot(hb, wg_ref[:, D_MODEL:2 * D_MODEL]) + bg_ref[:, D_MODEL:2 * D_MODEL])
    merged = (merged + g_b * y_b).astype(BF16)
    x1 = x + _dot(merged, wout_ref[...])
    x1_ref[...] = x1

    is_first = (pl.program_id(0) == 0) & (s == 0)
    _route_tile(x1, is_first, gffn_ref, wrh_ref, wrl_ref, rb_ref, tri_ref,
                h2p_ref, idx_ref, wt_ref, rk_ref, cnt_ref, run_ref)


def _attn_tables(attn_sinks):
    slopes = 2.0 ** (-8.0 * (np.arange(N_HEADS) + 1) / N_HEADS)
    j = np.arange(ATTN_BLOCK)[:, None]
    i = np.arange(ATTN_BLOCK)[None, :]
    from_prev = j > i
    dist = np.where(from_prev, ATTN_BLOCK + i - j, i - j)
    bias = np.empty((2, N_KV_HEADS, ATTN_BLOCK, Q_GROUP * ATTN_BLOCK), np.float32)
    for first in range(2):
        ok = ~from_prev if first else np.ones_like(from_prev)
        for kv in range(N_KV_HEADS):
            for slot, g in enumerate(HEAD_ORDER):
                sl = np.float32(slopes[kv * Q_GROUP + g])
                val = -(sl * dist.astype(np.float32))
                bias[first, kv, :, slot * ATTN_BLOCK:(slot + 1) * ATTN_BLOCK] = np.where(ok, val, MASKED)
    sink = attn_sinks.astype(F32).reshape(N_KV_HEADS, Q_GROUP)[:, np.asarray(HEAD_ORDER)]
    sink = jnp.repeat(sink, ATTN_BLOCK, axis=1)
    return jnp.asarray(bias), sink.reshape(N_KV_HEADS, 1, Q_GROUP * ATTN_BLOCK)


def _pad_heads(w, n_heads):
    k = w.shape[0]
    w = w.reshape(k, n_heads, HEAD_DIM)
    w = jnp.pad(w, ((0, 0), (0, 0), (0, HEAD_PAD - HEAD_DIM)))
    return w.reshape(k, n_heads * HEAD_PAD)


def _const_spec(shape):
    nd = len(shape)
    return pl.BlockSpec(shape, lambda *_: (0,) * nd)


def _mixer(x2d, batch0, batch, seq, g_mix, w_in, w_pool_mix, pool_scale, w_branch_a, q_gain, k_gain,
           attn_sinks, w_branch_b, w_gate, b_gate, w_out, g_ffn, w_router, router_bias, after=None):
    t = batch * seq
    tm = TM_MIX
    ns = seq // tm
    w_q = w_in[:, POOL_DIM:POOL_DIM + Q_DIM]
    w_k = w_in[:, POOL_DIM + Q_DIM:POOL_DIM + Q_DIM + KV_DIM]
    w_v = w_in[:, POOL_DIM + Q_DIM + KV_DIM:]
    win_p = jnp.concatenate([w_in[:, :POOL_DIM], _pad_heads(w_q, N_HEADS), _pad_heads(w_k, N_KV_HEADS),
                             _pad_heads(w_v, N_KV_HEADS)], axis=1).astype(BF16)
    qg = jnp.tile(jnp.pad(q_gain * (HEAD_DIM ** -0.5), (0, HEAD_PAD - HEAD_DIM)), N_HEADS).reshape(1, -1)
    kg = jnp.tile(jnp.pad(k_gain, (0, HEAD_PAD - HEAD_DIM)), N_KV_HEADS).reshape(1, -1)
    bias, sink = _attn_tables(attn_sinks)
    ones2 = jnp.asarray(np.kron(np.eye(2), np.ones((HEAD_PAD, HEAD_PAD))), BF16)
    operands = (x2d, g_mix.reshape(1, D_MODEL), win_p, qg, kg, bias, sink, ones2, w_pool_mix.astype(BF16),
                pool_scale.reshape(1, POOL_DIM), w_branch_a.astype(BF16), w_branch_b.astype(BF16),
                w_gate.astype(BF16), b_gate.reshape(1, 2 * D_MODEL), w_out.astype(BF16))
    operands += _router_operands(g_ffn, w_router, router_bias, tm)
    in_specs = [pl.BlockSpec((tm, D_MODEL), lambda b, s: ((batch0 + b) * ns + s, 0))]
    in_specs += [_const_spec(op.shape) for op in operands[1:]]
    if after is not None:
        operands += (after,)
        in_specs.append(pl.BlockSpec(memory_space=pl.ANY))
    row8 = pl.BlockSpec((TOP_K, tm), lambda b, s: (0, b * ns + s))
    return pl.pallas_call(
        _mixer_kernel,
        grid=(batch, ns),
        in_specs=in_specs,
        out_specs=[pl.BlockSpec((tm, D_MODEL), lambda b, s: (b * ns + s, 0)),
                   pl.BlockSpec((tm, HALF), lambda b, s: (b * ns + s, 0)),
                   row8,
                   pl.BlockSpec((tm, TOP_K * SC_LANES), lambda b, s: (b * ns + s, 0)),
                   row8,
                   pl.BlockSpec((N_EXPERTS, 128), lambda b, s: (0, 0))],
        out_shape=[jax.ShapeDtypeStruct((t, D_MODEL), F32),
                   jax.ShapeDtypeStruct((t, HALF), I32),
                   jax.ShapeDtypeStruct((TOP_K, t), I32),
                   jax.ShapeDtypeStruct((t, TOP_K * SC_LANES), F32),
                   jax.ShapeDtypeStruct((TOP_K, t), I32),
                   jax.ShapeDtypeStruct((N_EXPERTS, 128), F32)],
        scratch_shapes=[
            pltpu.VMEM((ATTN_BLOCK + tm, N_KV_HEADS * HEAD_PAD), BF16),
            pltpu.VMEM((N_KV_HEADS * HEAD_PAD, ATTN_BLOCK + tm), BF16),
            pltpu.VMEM((4, POOL_TAIL + tm, POOL_DIM), F32),
            pltpu.VMEM((Q_DIM, tm), F32),
            pltpu.VMEM((N_EXPERTS, 128), F32),
        ],
        compiler_params=pltpu.CompilerParams(
            dimension_semantics=("arbitrary", "arbitrary"), vmem_limit_bytes=VMEM_LIMIT),
        name="mixer",
    )(*operands)


def _route_tile(x1, is_first, gffn_ref, wrh_ref, wrl_ref, rb_ref, tri_ref,
                h2p_ref, idx_ref, wt_ref, rk_ref, cnt_ref, run_ref):
    tm = x1.shape[0]

    @pl.when(is_first)
    def _():
        run_ref[...] = jnp.zeros_like(run_ref)

    h2 = _rms(x1, gffn_ref[...])
    hb = h2.astype(BF16)
    h2p_ref[...] = _pack(h2[:, 0:HALF], h2[:, HALF:D_MODEL])

    lo = (h2 - hb.astype(F32)).astype(BF16)
    wrh = wrh_ref[...]
    logits = (lax.dot_general(wrh, hb, _NT, preferred_element_type=F32)
              + lax.dot_general(wrl_ref[...], hb, _NT, preferred_element_type=F32)
              + lax.dot_general(wrh, lo, _NT, preferred_element_type=F32))
    scores = _sigmoid(logits)
    choice = scores + rb_ref[...]

    neg = -jnp.inf
    sub8 = lax.broadcasted_iota(I32, (GROUP_SIZE, tm), 0).astype(F32)
    grp_rows = []
    for gi in range(N_EXPERT_GROUPS):
        cg = choice[GROUP_SIZE * gi:GROUP_SIZE * (gi + 1)]
        m1 = jnp.max(cg, axis=0, keepdims=True)
        first = jnp.min(jnp.where(cg == m1, sub8, float(GROUP_SIZE)), axis=0, keepdims=True)
        m2 = jnp.max(jnp.where(sub8 == first, neg, cg), axis=0, keepdims=True)
        grp_rows.append(m1 + m2)
    gs = jnp.concatenate(grp_rows, axis=0)
    beaten = jnp.zeros((N_EXPERT_GROUPS, tm), F32)
    for gi in range(N_EXPERT_GROUPS):
        row = grp_rows[gi]
        wins = (row > gs) | ((row == gs) & (sub8 > float(gi)))
        beaten = beaten + jnp.where(wins, 1.0, 0.0)
    gsel = beaten < float(TOPK_GROUPS)
    cm = jnp.concatenate(
        [jnp.where(gsel[gi:gi + 1], choice[GROUP_SIZE * gi:GROUP_SIZE * (gi + 1)], neg)
         for gi in range(N_EXPERT_GROUPS)], axis=0)

    iota_e = lax.broadcasted_iota(I32, (N_EXPERTS, tm), 0).astype(F32)
    idx_rows, s_rows = [], []
    sel = jnp.zeros((N_EXPERTS, tm), F32)
    for _ in range(TOP_K):
        m = jnp.max(cm, axis=0, keepdims=True)
        idx = jnp.min(jnp.where(cm == m, iota_e, float(N_EXPERTS)), axis=0, keepdims=True)
        oh = iota_e == idx
        s_rows.append(jnp.sum(jnp.where(oh, scores, 0.0), axis=0, keepdims=True))
        idx_rows.append(idx)
        cm = jnp.where(oh, neg, cm)
        sel = sel + jnp.where(oh, 1.0, 0.0)

    run = run_ref[:, 0:1]
    cum = _dot(sel.astype(BF16), tri_ref[...])
    before = run + cum - sel
    rk_rows = [jnp.sum(jnp.where(iota_e == idx, before, 0.0), axis=0, keepdims=True) for idx in idx_rows]
    new_run = run + jnp.sum(sel, axis=1, keepdims=True)
    run_ref[...] = jnp.broadcast_to(new_run, run_ref.shape)
    cnt_ref[...] = jnp.broadcast_to(new_run, cnt_ref.shape)

    ssum = s_rows[0]
    for r in range(1, TOP_K):
        ssum = ssum + s_rows[r]
    denom = ssum + 1e-20
    idx_ref[...] = jnp.concatenate(idx_rows, axis=0).astype(I32)
    w_rep = jnp.concatenate([jnp.broadcast_to(sr / denom * ROUTED_SCALE, (SC_LANES, tm)) for sr in s_rows],
                            axis=0)
    wt_ref[...] = w_rep.T
    rk_ref[...] = jnp.concatenate(rk_rows, axis=0).astype(I32)


def _router_operands(g_ffn, w_router, router_bias, tm):
    wr_t = w_router.astype(F32).T
    wr_hi = wr_t.astype(BF16)
    wr_lo = (wr_t - wr_hi.astype(F32)).astype(BF16)
    tri = (np.arange(tm)[:, None] <= np.arange(tm)[None, :]).astype(np.float32)
    return (g_ffn.reshape(1, D_MODEL), wr_hi, wr_lo, router_bias.astype(F32).reshape(N_EXPERTS, 1),
            jnp.asarray(tri, BF16))


def _swiglu_packed(x_packed, wg, wu, wd):
    lo, hi = _unpack(x_packed)
    lo = lo.astype(BF16)
    hi = hi.astype(BF16)
    g = _dot(lo, wg[0:HALF, :]) + _dot(hi, wg[HALF:D_MODEL, :])
    u = _dot(lo, wu[0:HALF, :]) + _dot(hi, wu[HALF:D_MODEL, :])
    a = (g * _sigmoid(g) * u).astype(BF16)
    y = _dot(a, wd[...])
    return _pack(y[:, 0:HALF], y[:, HALF:D_MODEL])


def _shared_kernel(h2p_ref, wg_ref, wu_ref, wd_ref, ysh_ref):
    ysh_ref[...] = _swiglu_packed(h2p_ref[...], wg_ref, wu_ref, wd_ref)


def _shared(h2p, w_sh_gate, w_sh_up, w_sh_down):
    t = h2p.shape[0]
    tm = TM_FFN
    operands = (h2p, w_sh_gate.astype(BF16), w_sh_up.astype(BF16), w_sh_down.astype(BF16))
    in_specs = [pl.BlockSpec((tm, HALF), lambda i: (i, 0))]
    in_specs += [_const_spec(op.shape) for op in operands[1:]]
    return pl.pallas_call(
        _shared_kernel,
        grid=(t // tm,),
        in_specs=in_specs,
        out_specs=pl.BlockSpec((tm, HALF), lambda i: (i, 0)),
        out_shape=jax.ShapeDtypeStruct((t, HALF), I32),
        compiler_params=pltpu.CompilerParams(
            dimension_semantics=("arbitrary",), vmem_limit_bytes=VMEM_LIMIT),
        name="shared",
    )(*operands)


def _positions_kernel(pstart_ref, idx_ref, rk_ref, pos_ref):
    idx = idx_ref[...]
    acc = rk_ref[...]
    for e in range(N_EXPERTS):
        acc = acc + jnp.where(idx == e, pstart_ref[e], 0)
    pos_ref[...] = acc


def _positions(pstart, idx8, rk8):
    t = idx8.shape[1]
    tm = TM_POS
    spec = pl.BlockSpec((TOP_K, tm), lambda i, ps: (0, i))
    return pl.pallas_call(
        _positions_kernel,
        grid_spec=pltpu.PrefetchScalarGridSpec(num_scalar_prefetch=1, grid=(t // tm,),
                                               in_specs=[spec, spec], out_specs=spec),
        out_shape=jax.ShapeDtypeStruct((TOP_K, t), I32),
        name="positions",
    )(pstart, idx8, rk8)


def _sc_workers():
    info = plsc.get_sparse_core_info()
    return info.num_cores, info.num_cores * info.num_subcores


def _dispatch(pos_flat, h2p, n_rows):
    t = h2p.shape[0]
    ch = SC_SCATTER_ROWS
    n_cores, n_workers = _sc_workers()
    n_ch = t // (n_workers * ch)

    def body(pos_hbm, h_hbm, xs_hbm, idx_v, rows_v, sem):
        wid = lax.axis_index("s") * n_cores + lax.axis_index("c")

        @pl.loop(0, n_ch)
        def _(c):
            chunk = wid * n_ch + c
            pltpu.sync_copy(h_hbm.at[pl.ds(chunk * ch, ch)], rows_v)
            for k in range(TOP_K):
                pltpu.sync_copy(pos_hbm.at[pl.ds(k * t + chunk * ch, ch)], idx_v.at[k])
            copies = [pltpu.async_copy(rows_v, xs_hbm.at[idx_v.at[k]], sem) for k in range(TOP_K)]
            for cp in copies:
                cp.wait()

    return pl.kernel(
        body,
        out_type=jax.ShapeDtypeStruct((n_rows, HALF), I32),
        mesh=plsc.VectorSubcoreMesh(core_axis_name="c", subcore_axis_name="s"),
        scratch_types=[pltpu.VMEM((TOP_K, ch), I32), pltpu.VMEM((ch, HALF), I32),
                       pltpu.SemaphoreType.DMA],
        name="dispatch",
    )(pos_flat, h2p)


def _pairwise_sum(xs):
    while len(xs) > 1:
        xs = [xs[i] + xs[i + 1] for i in range(0, len(xs), 2)]
    return xs[0]


def _gather_sum(pos_flat, w_rep, ys, tok0, t):
    t_all = w_rep.shape[0]
    ch = SC_SUM_ROWS
    lanes = SC_LANES
    n_cores, n_workers = _sc_workers()
    per_w = t // n_workers
    n_ch = per_w // ch

    def body(pos_hbm, w_hbm, ys_hbm, out_hbm, idx_all, w_all, buf0, buf1, acc0, acc1, gsem0, gsem1, osem0, osem1):
        wid = lax.axis_index("s") * n_cores + lax.axis_index("c")
        base = wid * n_ch
        first = tok0 + wid * per_w
        bufs, accs, gsems, osems = (buf0, buf1), (acc0, acc1), (gsem0, gsem1), (osem0, osem1)
        for k in range(TOP_K):
            pltpu.sync_copy(pos_hbm.at[pl.ds(k * t_all + first, per_w)], idx_all.at[pl.ds(k * per_w, per_w)])
        pltpu.sync_copy(w_hbm.at[pl.ds(first, per_w)], w_all)

        def row_copy(c, s, k):
            rows = idx_all.at[pl.ds(k * per_w + c * ch, ch)]
            return pltpu.make_async_copy(ys_hbm.at[rows], bufs[s].at[k], gsems[s])

        def out_copy(c, s):
            return pltpu.make_async_copy(accs[s], out_hbm.at[pl.ds((base + c) * ch, ch)], osems[s])

        def fetch(c, s):
            for k in range(TOP_K):
                row_copy(c, s, k).start()

        def reduce_rows(c, s):
            @pl.loop(0, ch)
            def _(r):
                wks = [w_all[c * ch + r, pl.ds(k * lanes, lanes)] for k in range(TOP_K)]

                @pl.loop(0, HALF // lanes, step=2)
                def _(v):
                    for dv in range(2):
                        col = (v + dv) * lanes
                        los, his = [], []
                        for k in range(TOP_K):
                            words = bufs[s][k, r, pl.ds(col, lanes)]
                            los.append(wks[k] * lax.bitcast_convert_type(lax.shift_left(words, jnp.int32(16)), F32))
                            his.append(wks[k] * lax.bitcast_convert_type(words & jnp.int32(-65536), F32))
                        accs[s][r, pl.ds(col, lanes)] = _pairwise_sum(los)
                        accs[s][r, pl.ds(HALF + col, lanes)] = _pairwise_sum(his)

        fetch(0, 0)

        @pl.loop(0, n_ch, step=2)
        def _(c):
            for s in range(2):
                cc = c + s

                @pl.when(cc + 1 < n_ch)
                def _():
                    fetch(cc + 1, 1 - s)

                for k in range(TOP_K):
                    row_copy(cc, s, k).wait()

                @pl.when(cc >= 2)
                def _():
                    out_copy(cc - 2, s).wait()

                reduce_rows(cc, s)
                out_copy(cc, s).start()

        out_copy(n_ch - 2, 0).wait()
        out_copy(n_ch - 1, 1).wait()

    return pl.kernel(
        body,
        out_type=jax.ShapeDtypeStruct((t, D_MODEL), F32),
        mesh=plsc.VectorSubcoreMesh(core_axis_name="c", subcore_axis_name="s"),
        scratch_types=[pltpu.VMEM((TOP_K * per_w,), I32), pltpu.VMEM((per_w, TOP_K * lanes), F32),
                       pltpu.VMEM((TOP_K, ch, HALF), I32), pltpu.VMEM((TOP_K, ch, HALF), I32),
                       pltpu.VMEM((ch, D_MODEL), F32), pltpu.VMEM((ch, D_MODEL), F32),
                       pltpu.SemaphoreType.DMA, pltpu.SemaphoreType.DMA,
                       pltpu.SemaphoreType.DMA, pltpu.SemaphoreType.DMA],
        compiler_params=pltpu.CompilerParams(needs_layout_passes=False),
        name="gather_sum",
    )(pos_flat, w_rep, ys)


def _expert_kernel(nblk_ref, blk0_ref, cnt_ref, wg_ref, wu_ref, wd_ref, xs_hbm, after_ref, ys_hbm,
                   wg_s, wu_s, wd_s, xbuf, ybuf, in_sem, out_sem):
    del after_ref
    e = pl.program_id(0)
    nb = nblk_ref[e]
    b0 = blk0_ref[e]
    cnt = cnt_ref[e]
    n_all = blk0_ref[N_EXPERTS]
    ahead = EXPERT_SLOTS - 1
    wg_s[...] = wg_ref[0].astype(BF16)
    wu_s[...] = wu_ref[0].astype(BF16)
    wd_s[...] = wd_ref[0].astype(BF16)

    def slot_of(g):
        return g & (EXPERT_SLOTS - 1)

    def rows_of(g):
        return pl.ds(pl.multiple_of(g * MOE_BLOCK, MOE_BLOCK), MOE_BLOCK)

    def in_copy(g):
        return pltpu.make_async_copy(xs_hbm.at[rows_of(g)], xbuf.at[slot_of(g)], in_sem.at[slot_of(g)])

    def out_copy(g):
        return pltpu.make_async_copy(ybuf.at[slot_of(g)], ys_hbm.at[rows_of(g)], out_sem.at[slot_of(g)])

    @pl.when(e == 0)
    def _():
        for g in range(ahead):
            @pl.when(g < n_all)
            def _():
                in_copy(g).start()

    def arrive(g):
        in_copy(g).wait()

        @pl.when(g + ahead < n_all)
        def _():
            in_copy(g + ahead).start()

        @pl.when(g >= EXPERT_SLOTS)
        def _():
            out_copy(g - EXPERT_SLOTS).wait()

    def one_block(j, carry):
        g = b0 + j
        arrive(g)
        sub = MOE_BLOCK // 2
        rows = lax.broadcasted_iota(I32, (sub, HALF), 0)
        for h in range(2):
            valid = cnt - j * MOE_BLOCK - h * sub
            x = jnp.where(rows < valid, xbuf[slot_of(g), h * sub:(h + 1) * sub], 0)
            ybuf[slot_of(g), h * sub:(h + 1) * sub] = _swiglu_packed(x, wg_s, wu_s, wd_s)
        out_copy(g).start()
        return carry

    lax.fori_loop(0, nb, one_block, 0)

    @pl.when(e == N_EXPERTS - 1)
    def _():
        for back in range(1, EXPERT_SLOTS + 1):
            @pl.when(n_all >= back)
            def _():
                out_copy(n_all - back).wait()


def _experts(n_blk, blk0, counts, n_rows, xs, w_gate, w_up, w_down, after):
    wspec_in = pl.BlockSpec((1, D_MODEL, EXPERT_DIM), lambda e, *_: (e, 0, 0))
    grid_spec = pltpu.PrefetchScalarGridSpec(
        num_scalar_prefetch=3,
        grid=(N_EXPERTS,),
        in_specs=[wspec_in, wspec_in,
                  pl.BlockSpec((1, EXPERT_DIM, D_MODEL), lambda e, *_: (e, 0, 0)),
                  pl.BlockSpec(memory_space=pl.ANY),
                  pl.BlockSpec(memory_space=pl.ANY)],
        out_specs=pl.BlockSpec(memory_space=pl.ANY),
        scratch_shapes=[pltpu.VMEM((D_MODEL, EXPERT_DIM), BF16), pltpu.VMEM((D_MODEL, EXPERT_DIM), BF16),
                        pltpu.VMEM((EXPERT_DIM, D_MODEL), BF16),
                        pltpu.VMEM((EXPERT_SLOTS, MOE_BLOCK, HALF), I32),
                        pltpu.VMEM((EXPERT_SLOTS, MOE_BLOCK, HALF), I32),
                        pltpu.SemaphoreType.DMA((EXPERT_SLOTS,)), pltpu.SemaphoreType.DMA((EXPERT_SLOTS,))],
    )
    return pl.pallas_call(
        _expert_kernel,
        grid_spec=grid_spec,
        out_shape=jax.ShapeDtypeStruct((n_rows, HALF), I32),
        compiler_params=pltpu.CompilerParams(
            dimension_semantics=("arbitrary",), vmem_limit_bytes=VMEM_LIMIT),
        name="experts",
    )(n_blk, blk0, counts, w_gate, w_up, w_down, xs, after)


def _combine_kernel(x1_ref, ysh_ref, moe_ref, p_ref, gple_ref, wpg_ref, wpp_ref, gpost_ref, *rest):
    out_ref = rest[-1]
    proj = _rms(_dot(p_ref[...].astype(BF16), wpp_ref[...]), gpost_ref[...])
    sh_lo, sh_hi = _unpack(ysh_ref[...])
    x2 = x1_ref[...] + moe_ref[...] + jnp.concatenate([sh_lo, sh_hi], axis=1)
    gate = _sigmoid(_dot(_rms(x2, gple_ref[...]).astype(BF16), wpg_ref[...]))
    out_ref[...] = x2 + gate * proj


def _combine(x1, ysh, moe, p2d, g_ple, w_ple_gate, w_ple_proj, g_ple_post, tok0, tok0_all, prev_out):
    t = p2d.shape[0]
    tm = TM_COMBINE
    steps = moe.shape[0] // tm
    off = tok0 // tm
    off_all = tok0_all // tm
    operands = [x1, ysh, moe, p2d, g_ple.reshape(1, D_MODEL), w_ple_gate.astype(BF16),
                w_ple_proj.astype(BF16), g_ple_post.reshape(1, D_MODEL)]
    in_specs = [pl.BlockSpec((tm, D_MODEL), lambda i: (off + i, 0)),
                pl.BlockSpec((tm, HALF), lambda i: (off + i, 0)),
                pl.BlockSpec((tm, D_MODEL), lambda i: (i, 0)),
                pl.BlockSpec((tm, PLE_DIM), lambda i: (off_all + i, 0)),
                _const_spec((1, D_MODEL)),
                _const_spec((D_MODEL, D_MODEL)),
                _const_spec((PLE_DIM, D_MODEL)),
                _const_spec((1, D_MODEL))]
    aliases = {}
    if prev_out is not None:
        aliases = {len(operands): 0}
        operands.append(prev_out)
        in_specs.append(pl.BlockSpec(memory_space=pl.ANY))
    return pl.pallas_call(
        _combine_kernel,
        grid=(steps,),
        in_specs=in_specs,
        out_specs=pl.BlockSpec((tm, D_MODEL), lambda i: (off_all + i, 0)),
        out_shape=jax.ShapeDtypeStruct((t, D_MODEL), F32),
        input_output_aliases=aliases,
        compiler_params=pltpu.CompilerParams(
            dimension_semantics=("arbitrary",), vmem_limit_bytes=VMEM_LIMIT),
        name="combine",
    )(*operands)


def _layer(x2d, p2d, batch, seq, g_mix, w_in, w_pool_mix, pool_scale, w_branch_a, q_gain, k_gain,
           attn_sinks, w_branch_b, w_gate, b_gate, w_out, g_ffn, w_router, router_bias, w_exp_gate,
           w_exp_up, w_exp_down, w_sh_gate, w_sh_up, w_sh_down, g_ple, w_ple_gate, w_ple_proj,
           g_ple_post):
    bh = batch // PIPELINE_HALVES
    th = bh * seq
    n_rows = (-(-(th * TOP_K) // MOE_BLOCK) + N_EXPERTS) * MOE_BLOCK

    halves = []
    after = None
    for h in range(PIPELINE_HALVES):
        x1, h2p, idx8, w_rep, rk8, cnt = _mixer(
            x2d, h * bh, bh, seq, g_mix, w_in, w_pool_mix, pool_scale, w_branch_a, q_gain, k_gain,
            attn_sinks, w_branch_b, w_gate, b_gate, w_out, g_ffn, w_router, router_bias, after=after)
        counts = cnt[:, 0].astype(I32)
        padded = (counts + MOE_BLOCK - 1) // MOE_BLOCK * MOE_BLOCK
        pend = jnp.cumsum(padded)
        pstart = pend - padded
        pos = _positions(pstart.astype(I32), idx8, rk8).reshape(-1)
        xs = _dispatch(pos, h2p, n_rows)
        blk0 = jnp.concatenate([pstart, pend[-1:]]) // MOE_BLOCK
        halves.append((x1, h2p, w_rep, pos, xs, (padded // MOE_BLOCK).astype(I32), blk0.astype(I32), counts))
        after = pos

    out = None
    tc = th // COMBINE_CHUNKS
    for h, (x1, h2p, w_rep, pos, xs, n_blk, blk0, counts) in enumerate(halves):
        ysh = _shared(h2p, w_sh_gate, w_sh_up, w_sh_down)
        ys = _experts(n_blk, blk0, counts, n_rows, xs, w_exp_gate, w_exp_up, w_exp_down, ysh)
        for c in range(COMBINE_CHUNKS):
            moe = _gather_sum(pos, w_rep, ys, c * tc, tc)
            out = _combine(x1, ysh, moe, p2d, g_ple, w_ple_gate, w_ple_proj, g_ple_post,
                           c * tc, h * th + c * tc, out)
    return out


def kernel(x, p, g_mix, w_in, w_pool_mix, pool_scale, w_branch_a, q_gain, k_gain, attn_sinks,
           w_branch_b, w_gate, b_gate, w_out, g_ffn, w_router, router_bias, w_exp_gate, w_exp_up,
           w_exp_down, w_sh_gate, w_sh_up, w_sh_down, g_ple, w_ple_gate, w_ple_proj, g_ple_post):
    batch, seq, d = x.shape
    depth = p.shape[0]
    x2d = x.reshape(batch * seq, d)
    for i in range(depth):
        x2d = _layer(x2d, p[i].reshape(batch * seq, PLE_DIM), batch, seq, g_mix[i], w_in[i],
                     w_pool_mix[i], pool_scale[i], w_branch_a[i], q_gain[i], k_gain[i], attn_sinks[i],
                     w_branch_b[i], w_gate[i], b_gate[i], w_out[i], g_ffn[i], w_router[i],
                     router_bias[i], w_exp_gate[i], w_exp_up[i], w_exp_down[i], w_sh_gate[i],
                     w_sh_up[i], w_sh_down[i], g_ple[i], w_ple_gate[i], w_ple_proj[i], g_ple_post[i])
    return x2d.reshape(batch, seq, d)
```

```python
import functools

import numpy as np
import jax
import jax.numpy as jnp
from jax import lax
from jax.experimental import pallas as pl
from jax.experimental.pallas import tpu as pltpu
from jax.experimental.pallas import tpu_sc as plsc

F32 = jnp.float32
BF16 = jnp.bfloat16
I32 = jnp.int32

D_MODEL = 1024
PLE_DIM = 256
POOL_WINDOWS = (2, 4, 8, 16)
POOL_GROUP_DIM = 128
POOL_DIM = 512
N_HEADS = 8
N_KV_HEADS = 2
Q_GROUP = 4
HEAD_DIM = 64
HEAD_PAD = 128
Q_DIM = 512
KV_DIM = 128
ATTN_BLOCK = 128
N_EXPERTS = 64
N_EXPERT_GROUPS = 8
GROUP_SIZE = 8
TOPK_GROUPS = 4
TOP_K = 8
EXPERT_DIM = 256
ROUTED_SCALE = 2.5
MOE_BLOCK = 1024
EXPERT_SLOTS = 4
EPS = 1e-6
HALF = D_MODEL // 2
MASKED = -1e30

COL_Q = POOL_DIM
COL_K = COL_Q + N_HEADS * HEAD_PAD
COL_V = COL_K + N_KV_HEADS * HEAD_PAD
IN_PAD = COL_V + N_KV_HEADS * HEAD_PAD
HEAD_ORDER = (0, 2, 1, 3)

TM_MIX = 512
TM_FFN = 512
TM_COMBINE = 512
TM_POS = 4096
COMBINE_CHUNKS = 2
PIPELINE_HALVES = 2
SC_SCATTER_ROWS = 128
SC_SUM_ROWS = 8
SC_LANES = 16
POOL_TAIL = 8
VMEM_LIMIT = 56 * 1024 * 1024

_NT = (((1,), (1,)), ((), ()))


def _dot(a, b):
    return jnp.dot(a, b, preferred_element_type=F32)


def _rms(x, g):
    ms = jnp.mean(x * x, axis=-1, keepdims=True)
    return x * lax.rsqrt(ms + EPS) * g


def _sigmoid(x):
    return 0.5 * jnp.tanh(0.5 * x) + 0.5


def _pack(lo, hi):
    lo_bits = lax.bitcast_convert_type(lo.astype(BF16).astype(F32), I32)
    hi_bits = lax.bitcast_convert_type(hi.astype(BF16).astype(F32), I32)
    return (hi_bits & jnp.int32(-65536)) | lax.shift_right_logical(lo_bits, jnp.int32(16))


def _unpack(w):
    lo = lax.bitcast_convert_type(lax.shift_left(w, jnp.int32(16)), F32)
    hi = lax.bitcast_convert_type(w & jnp.int32(-65536), F32)
    return lo, hi


def _mixer_kernel(x_ref, gmix_ref, win_ref, qg_ref, kg_ref, bias_ref, sink_ref, ones_ref, wmix_ref,
                  pscale_ref, wa_ref, wb_ref, wg_ref, bg_ref, wout_ref,
                  gffn_ref, wrh_ref, wrl_ref, rb_ref, tri_ref, *rest):
    (x1_ref, h2p_ref, idx_ref, wt_ref, rk_ref, cnt_ref,
     kbuf, vt_buf, pbuf, zbt_buf, run_ref) = rest[-11:]
    s = pl.program_id(1)
    tm = x_ref.shape[0]

    @pl.when(s == 0)
    def _():
        kbuf[0:ATTN_BLOCK, :] = jnp.zeros((ATTN_BLOCK, kbuf.shape[1]), BF16)
        vt_buf[:, 0:ATTN_BLOCK] = jnp.zeros((vt_buf.shape[0], ATTN_BLOCK), BF16)
        pbuf[:, 0:POOL_TAIL, :] = jnp.zeros((pbuf.shape[0], POOL_TAIL, POOL_DIM), F32)

    x = x_ref[...]
    hb = _rms(x, gmix_ref[...]).astype(BF16)
    u = _dot(hb, win_ref[...])

    up = u[:, 0:POOL_DIM]
    t0 = POOL_TAIL
    pbuf[0, t0:t0 + tm, :] = up
    s2 = up + pbuf[0, t0 - 1:t0 - 1 + tm, :]
    pbuf[1, t0:t0 + tm, 128:512] = s2[:, 128:512]
    s4 = s2[:, 128:512] + pbuf[1, t0 - 2:t0 - 2 + tm, 128:512]
    pbuf[2, t0:t0 + tm, 256:512] = s4[:, 128:384]
    s8 = s4[:, 128:384] + pbuf[2, t0 - 4:t0 - 4 + tm, 256:512]
    pbuf[3, t0:t0 + tm, 384:512] = s8[:, 128:256]
    s16 = s8[:, 128:256] + pbuf[3, t0 - 8:t0 - 8 + tm, 384:512]
    for lvl in range(4):
        pbuf[lvl, 0:POOL_TAIL, 128 * lvl:512] = pbuf[lvl, tm:tm + POOL_TAIL, 128 * lvl:512]
    wsums = (s2[:, 0:128], s4[:, 0:128], s8[:, 0:128], s16)
    tpos = (s * tm).astype(F32) + lax.broadcasted_iota(I32, (tm, 1), 0).astype(F32)
    za_parts = []
    for g, w in enumerate(POOL_WINDOWS):
        inv_count = 1.0 / jnp.minimum(tpos + 1.0, float(w))
        pooled = wsums[g] * inv_count - up[:, 128 * g:128 * (g + 1)]
        mixed = _dot(pooled.astype(BF16), wmix_ref[g]) * pscale_ref[:, 128 * g:128 * (g + 1)]
        za_parts.append(mixed.astype(BF16))
    za = jnp.concatenate(za_parts, axis=1)

    ones2 = ones_ref[...]
    q = u[:, COL_Q:COL_K]
    q2 = (q * q).astype(BF16)
    qss = jnp.concatenate([_dot(q2[:, 256 * c:256 * (c + 1)], ones2) for c in range(N_HEADS // 2)], axis=1)
    qn = (q * lax.rsqrt(qss * (1.0 / HEAD_DIM) + EPS) * qg_ref[...]).astype(BF16)
    k = u[:, COL_K:COL_V]
    kss = _dot((k * k).astype(BF16), ones2)
    kbuf[ATTN_BLOCK:ATTN_BLOCK + tm, :] = (k * lax.rsqrt(kss * (1.0 / HEAD_DIM) + EPS) * kg_ref[...]).astype(BF16)
    vt_buf[:, ATTN_BLOCK:ATTN_BLOCK + tm] = u[:, COL_V:IN_PAD].T.astype(BF16)

    nq = Q_GROUP * ATTN_BLOCK
    key_j = lax.broadcasted_iota(I32, (ATTN_BLOCK, nq), 0)
    qry_i = lax.broadcasted_iota(I32, (ATTN_BLOCK, nq), 1) & (ATTN_BLOCK - 1)
    from_prev = key_j > qry_i
    first = jnp.where(s == 0, 1, 0)
    for n in range(tm // ATTN_BLOCK):
        r0 = ATTN_BLOCK * n
        for kv in range(N_KV_HEADS):
            c0 = HEAD_PAD * kv
            qs = jnp.concatenate(
                [qn[r0:r0 + ATTN_BLOCK, HEAD_PAD * (kv * Q_GROUP + g):HEAD_PAD * (kv * Q_GROUP + g + 1)]
                 for g in HEAD_ORDER], axis=0)
            kk = kbuf[r0:r0 + 2 * ATTN_BLOCK, c0:c0 + HEAD_PAD]
            st = lax.dot_general(kk, qs, _NT, preferred_element_type=F32)
            sc = jnp.where(from_prev, st[0:ATTN_BLOCK], st[ATTN_BLOCK:2 * ATTN_BLOCK])
            sc = sc + (bias_ref[first, kv] if n == 0 else bias_ref[0, kv])
            sink = sink_ref[kv]
            m = jnp.maximum(jnp.max(sc, axis=0, keepdims=True), sink)
            e = jnp.exp(sc - m)
            inv_den = 1.0 / (jnp.sum(e, axis=0, keepdims=True) + jnp.exp(sink - m))
            pt = jnp.concatenate([jnp.where(from_prev, e, 0.0), jnp.where(from_prev, 0.0, e)],
                                 axis=0).astype(BF16)
            v_even = vt_buf[c0:c0 + HEAD_PAD, r0:r0 + 2 * ATTN_BLOCK]
            v_odd = jnp.concatenate([v_even[HEAD_DIM:HEAD_PAD], v_even[0:HEAD_DIM]], axis=0)
            half = 2 * ATTN_BLOCK
            o = (_dot(v_even, pt[:, 0:half]) * inv_den[:, 0:half]
                 + _dot(v_odd, pt[:, half:2 * half]) * inv_den[:, half:2 * half])
            ch = kv * Q_GROUP * HEAD_DIM
            zbt_buf[ch:ch + HEAD_PAD, r0:r0 + ATTN_BLOCK] = o[:, 0:ATTN_BLOCK]
            zbt_buf[ch + HEAD_PAD:ch + 2 * HEAD_PAD, r0:r0 + ATTN_BLOCK] = o[:, ATTN_BLOCK:half]
    kbuf[0:ATTN_BLOCK, :] = kbuf[tm:tm + ATTN_BLOCK, :]
    vt_buf[:, 0:ATTN_BLOCK] = vt_buf[:, tm:tm + ATTN_BLOCK]

    y_a = _dot(za, wa_ref[...])
    y_b = _dot(zbt_buf[...].T.astype(BF16), wb_ref[...])
    g_a = _sigmoid(_dot(hb, wg_ref[:, 0:D_MODEL]) + bg_ref[:, 0:D_MODEL])
    merged = g_a * y_a
    g_b = _sigmoid(_dot(hb, wg_ref[:, D_MODEL:2 * D_MODEL]) + bg_ref[:, D_MODEL:2 * D_MODEL])
    merged = (merged + g_b * y_b).astype(BF16)
    x1 = x + _dot(merged, wout_ref[...])
    x1_ref[...] = x1

    is_first = (pl.program_id(0) == 0) & (s == 0)
    _route_tile(x1, is_first, gffn_ref, wrh_ref, wrl_ref, rb_ref, tri_ref,
                h2p_ref, idx_ref, wt_ref, rk_ref, cnt_ref, run_ref)


def _attn_tables(attn_sinks):
    slopes = 2.0 ** (-8.0 * (np.arange(N_HEADS) + 1) / N_HEADS)
    j = np.arange(ATTN_BLOCK)[:, None]
    i = np.arange(ATTN_BLOCK)[None, :]
    from_prev = j > i
    dist = np.where(from_prev, ATTN_BLOCK + i - j, i - j)
    bias = np.empty((2, N_KV_HEADS, ATTN_BLOCK, Q_GROUP * ATTN_BLOCK), np.float32)
    for first in range(2):
        ok = ~from_prev if first else np.ones_like(from_prev)
        for kv in range(N_KV_HEADS):
            for slot, g in enumerate(HEAD_ORDER):
                sl = np.float32(slopes[kv * Q_GROUP + g])
                val = -(sl * dist.astype(np.float32))
                bias[first, kv, :, slot * ATTN_BLOCK:(slot + 1) * ATTN_BLOCK] = np.where(ok, val, MASKED)
    sink = attn_sinks.astype(F32).reshape(N_KV_HEADS, Q_GROUP)[:, np.asarray(HEAD_ORDER)]
    sink = jnp.repeat(sink, ATTN_BLOCK, axis=1)
    return jnp.asarray(bias), sink.reshape(N_KV_HEADS, 1, Q_GROUP * ATTN_BLOCK)


def _pad_heads(w, n_heads):
    k = w.shape[0]
    w = w.reshape(k, n_heads, HEAD_DIM)
    w = jnp.pad(w, ((0, 0), (0, 0), (0, HEAD_PAD - HEAD_DIM)))
    return w.reshape(k, n_heads * HEAD_PAD)


def _const_spec(shape):
    nd = len(shape)
    return pl.BlockSpec(shape, lambda *_: (0,) * nd)


def _mixer(x2d, batch0, batch, seq, g_mix, w_in, w_pool_mix, pool_scale, w_branch_a, q_gain, k_gain,
           attn_sinks, w_branch_b, w_gate, b_gate, w_out, g_ffn, w_router, router_bias, after=None):
    t = batch * seq
    tm = TM_MIX
    ns = seq // tm
    w_q = w_in[:, POOL_DIM:POOL_DIM + Q_DIM]
    w_k = w_in[:, POOL_DIM + Q_DIM:POOL_DIM + Q_DIM + KV_DIM]
    w_v = w_in[:, POOL_DIM + Q_DIM + KV_DIM:]
    win_p = jnp.concatenate([w_in[:, :POOL_DIM], _pad_heads(w_q, N_HEADS), _pad_heads(w_k, N_KV_HEADS),
                             _pad_heads(w_v, N_KV_HEADS)], axis=1).astype(BF16)
    qg = jnp.tile(jnp.pad(q_gain * (HEAD_DIM ** -0.5), (0, HEAD_PAD - HEAD_DIM)), N_HEADS).reshape(1, -1)
    kg = jnp.tile(jnp.pad(k_gain, (0, HEAD_PAD - HEAD_DIM)), N_KV_HEADS).reshape(1, -1)
    bias, sink = _attn_tables(attn_sinks)
    ones2 = jnp.asarray(np.kron(np.eye(2), np.ones((HEAD_PAD, HEAD_PAD))), BF16)
    operands = (x2d, g_mix.reshape(1, D_MODEL), win_p, qg, kg, bias, sink, ones2, w_pool_mix.astype(BF16),
                pool_scale.reshape(1, POOL_DIM), w_branch_a.astype(BF16), w_branch_b.astype(BF16),
                w_gate.astype(BF16), b_gate.reshape(1, 2 * D_MODEL), w_out.astype(BF16))
    operands += _router_operands(g_ffn, w_router, router_bias, tm)
    in_specs = [pl.BlockSpec((tm, D_MODEL), lambda b, s: ((batch0 + b) * ns + s, 0))]
    in_specs += [_const_spec(op.shape) for op in operands[1:]]
    if after is not None:
        operands += (after,)
        in_specs.append(pl.BlockSpec(memory_space=pl.ANY))
    row8 = pl.BlockSpec((TOP_K, tm), lambda b, s: (0, b * ns + s))
    return pl.pallas_call(
        _mixer_kernel,
        grid=(batch, ns),
        in_specs=in_specs,
        out_specs=[pl.BlockSpec((tm, D_MODEL), lambda b, s: (b * ns + s, 0)),
                   pl.BlockSpec((tm, HALF), lambda b, s: (b * ns + s, 0)),
                   row8,
                   pl.BlockSpec((tm, TOP_K * SC_LANES), lambda b, s: (b * ns + s, 0)),
                   row8,
                   pl.BlockSpec((N_EXPERTS, 128), lambda b, s: (0, 0))],
        out_shape=[jax.ShapeDtypeStruct((t, D_MODEL), F32),
                   jax.ShapeDtypeStruct((t, HALF), I32),
                   jax.ShapeDtypeStruct((TOP_K, t), I32),
                   jax.ShapeDtypeStruct((t, TOP_K * SC_LANES), F32),
                   jax.ShapeDtypeStruct((TOP_K, t), I32),
                   jax.ShapeDtypeStruct((N_EXPERTS, 128), F32)],
        scratch_shapes=[
            pltpu.VMEM((ATTN_BLOCK + tm, N_KV_HEADS * HEAD_PAD), BF16),
            pltpu.VMEM((N_KV_HEADS * HEAD_PAD, ATTN_BLOCK + tm), BF16),
            pltpu.VMEM((4, POOL_TAIL + tm, POOL_DIM), F32),
            pltpu.VMEM((Q_DIM, tm), F32),
            pltpu.VMEM((N_EXPERTS, 128), F32),
        ],
        compiler_params=pltpu.CompilerParams(
            dimension_semantics=("arbitrary", "arbitrary"), vmem_limit_bytes=VMEM_LIMIT),
        name="mixer",
    )(*operands)


def _route_tile(x1, is_first, gffn_ref, wrh_ref, wrl_ref, rb_ref, tri_ref,
                h2p_ref, idx_ref, wt_ref, rk_ref, cnt_ref, run_ref):
    tm = x1.shape[0]

    @pl.when(is_first)
    def _():
        run_ref[...] = jnp.zeros_like(run_ref)

    h2 = _rms(x1, gffn_ref[...])
    hb = h2.astype(BF16)
    h2p_ref[...] = _pack(h2[:, 0:HALF], h2[:, HALF:D_MODEL])

    lo = (h2 - hb.astype(F32)).astype(BF16)
    wrh = wrh_ref[...]
    logits = (lax.dot_general(wrh, hb, _NT, preferred_element_type=F32)
              + lax.dot_general(wrl_ref[...], hb, _NT, preferred_element_type=F32)
              + lax.dot_general(wrh, lo, _NT, preferred_element_type=F32))
    scores = _sigmoid(logits)
    choice = scores + rb_ref[...]

    neg = -jnp.inf
    sub8 = lax.broadcasted_iota(I32, (GROUP_SIZE, tm), 0).astype(F32)
    grp_rows = []
    for gi in range(N_EXPERT_GROUPS):
        cg = choice[GROUP_SIZE * gi:GROUP_SIZE * (gi + 1)]
        m1 = jnp.max(cg, axis=0, keepdims=True)
        first = jnp.min(jnp.where(cg == m1, sub8, float(GROUP_SIZE)), axis=0, keepdims=True)
        m2 = jnp.max(jnp.where(sub8 == first, neg, cg), axis=0, keepdims=True)
        grp_rows.append(m1 + m2)
    gs = jnp.concatenate(grp_rows, axis=0)
    beaten = jnp.zeros((N_EXPERT_GROUPS, tm), F32)
    for gi in range(N_EXPERT_GROUPS):
        row = grp_rows[gi]
        wins = (row > gs) | ((row == gs) & (sub8 > float(gi)))
        beaten = beaten + jnp.where(wins, 1.0, 0.0)
    gsel = beaten < float(TOPK_GROUPS)
    cm = jnp.concatenate(
        [jnp.where(gsel[gi:gi + 1], choice[GROUP_SIZE * gi:GROUP_SIZE * (gi + 1)], neg)
         for gi in range(N_EXPERT_GROUPS)], axis=0)

    iota_e = lax.broadcasted_iota(I32, (N_EXPERTS, tm), 0).astype(F32)
    idx_rows, s_rows = [], []
    sel = jnp.zeros((N_EXPERTS, tm), F32)
    for _ in range(TOP_K):
        m = jnp.max(cm, axis=0, keepdims=True)
        idx = jnp.min(jnp.where(cm == m, iota_e, float(N_EXPERTS)), axis=0, keepdims=True)
        oh = iota_e == idx
        s_rows.append(jnp.sum(jnp.where(oh, scores, 0.0), axis=0, keepdims=True))
        idx_rows.append(idx)
        cm = jnp.where(oh, neg, cm)
        sel = sel + jnp.where(oh, 1.0, 0.0)

    run = run_ref[:, 0:1]
    cum = _dot(sel.astype(BF16), tri_ref[...])
    before = run + cum - sel
    rk_rows = [jnp.sum(jnp.where(iota_e == idx, before, 0.0), axis=0, keepdims=True) for idx in idx_rows]
    new_run = run + jnp.sum(sel, axis=1, keepdims=True)
    run_ref[...] = jnp.broadcast_to(new_run, run_ref.shape)
    cnt_ref[...] = jnp.broadcast_to(new_run, cnt_ref.shape)

    ssum = s_rows[0]
    for r in range(1, TOP_K):
        ssum = ssum + s_rows[r]
    denom = ssum + 1e-20
    idx_ref[...] = jnp.concatenate(idx_rows, axis=0).astype(I32)
    w_rep = jnp.concatenate([jnp.broadcast_to(sr / denom * ROUTED_SCALE, (SC_LANES, tm)) for sr in s_rows],
                            axis=0)
    wt_ref[...] = w_rep.T
    rk_ref[...] = jnp.concatenate(rk_rows, axis=0).astype(I32)


def _router_operands(g_ffn, w_router, router_bias, tm):
    wr_t = w_router.astype(F32).T
    wr_hi = wr_t.astype(BF16)
    wr_lo = (wr_t - wr_hi.astype(F32)).astype(BF16)
    tri = (np.arange(tm)[:, None] <= np.arange(tm)[None, :]).astype(np.float32)
    return (g_ffn.reshape(1, D_MODEL), wr_hi, wr_lo, router_bias.astype(F32).reshape(N_EXPERTS, 1),
            jnp.asarray(tri, BF16))


def _swiglu_packed(x_packed, wg, wu, wd):
    lo, hi = _unpack(x_packed)
    lo = lo.astype(BF16)
    hi = hi.astype(BF16)
    g = _dot(lo, wg[0:HALF, :]) + _dot(hi, wg[HALF:D_MODEL, :])
    u = _dot(lo, wu[0:HALF, :]) + _dot(hi, wu[HALF:D_MODEL, :])
    a = (g * _sigmoid(g) * u).astype(BF16)
    y = _dot(a, wd[...])
    return _pack(y[:, 0:HALF], y[:, HALF:D_MODEL])


def _shared_kernel(h2p_ref, wg_ref, wu_ref, wd_ref, ysh_ref):
    ysh_ref[...] = _swiglu_packed(h2p_ref[...], wg_ref, wu_ref, wd_ref)


def _shared(h2p, w_sh_gate, w_sh_up, w_sh_down):
    t = h2p.shape[0]
    tm = TM_FFN
    operands = (h2p, w_sh_gate.astype(BF16), w_sh_up.astype(BF16), w_sh_down.astype(BF16))
    in_specs = [pl.BlockSpec((tm, HALF), lambda i: (i, 0))]
    in_specs += [_const_spec(op.shape) for op in operands[1:]]
    return pl.pallas_call(
        _shared_kernel,
        grid=(t // tm,),
        in_specs=in_specs,
        out_specs=pl.BlockSpec((tm, HALF), lambda i: (i, 0)),
        out_shape=jax.ShapeDtypeStruct((t, HALF), I32),
        compiler_params=pltpu.CompilerParams(
            dimension_semantics=("arbitrary",), vmem_limit_bytes=VMEM_LIMIT),
        name="shared",
    )(*operands)


def _positions_kernel(pstart_ref, idx_ref, rk_ref, pos_ref):
    idx = idx_ref[...]
    acc = rk_ref[...]
    for e in range(N_EXPERTS):
        acc = acc + jnp.where(idx == e, pstart_ref[e], 0)
    pos_ref[...] = acc


def _positions(pstart, idx8, rk8):
    t = idx8.shape[1]
    tm = TM_POS
    spec = pl.BlockSpec((TOP_K, tm), lambda i, ps: (0, i))
    return pl.pallas_call(
        _positions_kernel,
        grid_spec=pltpu.PrefetchScalarGridSpec(num_scalar_prefetch=1, grid=(t // tm,),
                                               in_specs=[spec, spec], out_specs=spec),
        out_shape=jax.ShapeDtypeStruct((TOP_K, t), I32),
        name="positions",
    )(pstart, idx8, rk8)


def _sc_workers():
    info = plsc.get_sparse_core_info()
    return info.num_cores, info.num_cores * info.num_subcores


def _dispatch(pos_flat, h2p, n_rows):
    t = h2p.shape[0]
    ch = SC_SCATTER_ROWS
    n_cores, n_workers = _sc_workers()
    n_ch = t // (n_workers * ch)

    def body(pos_hbm, h_hbm, xs_hbm, idx_v, rows_v, sem):
        wid = lax.axis_index("s") * n_cores + lax.axis_index("c")

        @pl.loop(0, n_ch)
        def _(c):
            chunk = wid * n_ch + c
            pltpu.sync_copy(h_hbm.at[pl.ds(chunk * ch, ch)], rows_v)
            for k in range(TOP_K):
                pltpu.sync_copy(pos_hbm.at[pl.ds(k * t + chunk * ch, ch)], idx_v.at[k])
            copies = [pltpu.async_copy(rows_v, xs_hbm.at[idx_v.at[k]], sem) for k in range(TOP_K)]
            for cp in copies:
                cp.wait()

    return pl.kernel(
        body,
        out_type=jax.ShapeDtypeStruct((n_rows, HALF), I32),
        mesh=plsc.VectorSubcoreMesh(core_axis_name="c", subcore_axis_name="s"),
        scratch_types=[pltpu.VMEM((TOP_K, ch), I32), pltpu.VMEM((ch, HALF), I32),
                       pltpu.SemaphoreType.DMA],
        name="dispatch",
    )(pos_flat, h2p)


def _pairwise_sum(xs):
    while len(xs) > 1:
        xs = [xs[i] + xs[i + 1] for i in range(0, len(xs), 2)]
    return xs[0]


def _gather_sum(pos_flat, w_rep, ys, tok0, t):
    t_all = w_rep.shape[0]
    ch = SC_SUM_ROWS
    lanes = SC_LANES
    n_cores, n_workers = _sc_workers()
    per_w = t // n_workers
    n_ch = per_w // ch

    def body(pos_hbm, w_hbm, ys_hbm, out_hbm, idx_all, w_all, buf0, buf1, acc0, acc1, gsem0, gsem1, osem0, osem1):
        wid = lax.axis_index("s") * n_cores + lax.axis_index("c")
        base = wid * n_ch
        first = tok0 + wid * per_w
        bufs, accs, gsems, osems = (buf0, buf1), (acc0, acc1), (gsem0, gsem1), (osem0, osem1)
        for k in range(TOP_K):
            pltpu.sync_copy(pos_hbm.at[pl.ds(k * t_all + first, per_w)], idx_all.at[pl.ds(k * per_w, per_w)])
        pltpu.sync_copy(w_hbm.at[pl.ds(first, per_w)], w_all)

        def row_copy(c, s, k):
            rows = idx_all.at[pl.ds(k * per_w + c * ch, ch)]
            return pltpu.make_async_copy(ys_hbm.at[rows], bufs[s].at[k], gsems[s])

        def out_copy(c, s):
            return pltpu.make_async_copy(accs[s], out_hbm.at[pl.ds((base + c) * ch, ch)], osems[s])

        def fetch(c, s):
            for k in range(TOP_K):
                row_copy(c, s, k).start()

        def reduce_rows(c, s):
            @pl.loop(0, ch)
            def _(r):
                wks = [w_all[c * ch + r, pl.ds(k * lanes, lanes)] for k in range(TOP_K)]

                @pl.loop(0, HALF // lanes, step=2)
                def _(v):
                    for dv in range(2):
                        col = (v + dv) * lanes
                        los, his = [], []
                        for k in range(TOP_K):
                            words = bufs[s][k, r, pl.ds(col, lanes)]
                            los.append(wks[k] * lax.bitcast_convert_type(lax.shift_left(words, jnp.int32(16)), F32))
                            his.append(wks[k] * lax.bitcast_convert_type(words & jnp.int32(-65536), F32))
                        accs[s][r, pl.ds(col, lanes)] = _pairwise_sum(los)
                        accs[s][r, pl.ds(HALF + col, lanes)] = _pairwise_sum(his)

        fetch(0, 0)

        @pl.loop(0, n_ch, step=2)
        def _(c):
            for s in range(2):
                cc = c + s

                @pl.when(cc + 1 < n_ch)
                def _():
                    fetch(cc + 1, 1 - s)

                for k in range(TOP_K):
                    row_copy(cc, s, k).wait()

                @pl.when(cc >= 2)
                def _():
                    out_copy(cc - 2, s).wait()

                reduce_rows(cc, s)
                out_copy(cc, s).start()

        out_copy(n_ch - 2, 0).wait()
        out_copy(n_ch - 1, 1).wait()

    return pl.kernel(
        body,
        out_type=jax.ShapeDtypeStruct((t, D_MODEL), F32),
        mesh=plsc.VectorSubcoreMesh(core_axis_name="c", subcore_axis_name="s"),
        scratch_types=[pltpu.VMEM((TOP_K * per_w,), I32), pltpu.VMEM((per_w, TOP_K * lanes), F32),
                       pltpu.VMEM((TOP_K, ch, HALF), I32), pltpu.VMEM((TOP_K, ch, HALF), I32),
                       pltpu.VMEM((ch, D_MODEL), F32), pltpu.VMEM((ch, D_MODEL), F32),
                       pltpu.SemaphoreType.DMA, pltpu.SemaphoreType.DMA,
                       pltpu.SemaphoreType.DMA, pltpu.SemaphoreType.DMA],
        compiler_params=pltpu.CompilerParams(needs_layout_passes=False),
        name="gather_sum",
    )(pos_flat, w_rep, ys)


def _expert_kernel(nblk_ref, blk0_ref, cnt_ref, wg_ref, wu_ref, wd_ref, xs_hbm, after_ref, ys_hbm,
                   wg_s, wu_s, wd_s, xbuf, ybuf, in_sem, out_sem):
    del after_ref
    e = pl.program_id(0)
    nb = nblk_ref[e]
    b0 = blk0_ref[e]
    cnt = cnt_ref[e]
    n_all = blk0_ref[N_EXPERTS]
    ahead = EXPERT_SLOTS - 1
    wg_s[...] = wg_ref[0].astype(BF16)
    wu_s[...] = wu_ref[0].astype(BF16)
    wd_s[...] = wd_ref[0].astype(BF16)

    def slot_of(g):
        return g & (EXPERT_SLOTS - 1)

    def rows_of(g):
        return pl.ds(pl.multiple_of(g * MOE_BLOCK, MOE_BLOCK), MOE_BLOCK)

    def in_copy(g):
        return pltpu.make_async_copy(xs_hbm.at[rows_of(g)], xbuf.at[slot_of(g)], in_sem.at[slot_of(g)])

    def out_copy(g):
        return pltpu.make_async_copy(ybuf.at[slot_of(g)], ys_hbm.at[rows_of(g)], out_sem.at[slot_of(g)])

    @pl.when(e == 0)
    def _():
        for g in range(ahead):
            @pl.when(g < n_all)
            def _():
                in_copy(g).start()

    def arrive(g):
        in_copy(g).wait()

        @pl.when(g + ahead < n_all)
        def _():
            in_copy(g + ahead).start()

        @pl.when(g >= EXPERT_SLOTS)
        def _():
            out_copy(g - EXPERT_SLOTS).wait()

    def ffn(j):
        g = b0 + j
        rows = lax.broadcasted_iota(I32, (MOE_BLOCK, HALF), 0)
        x = jnp.where(rows < cnt - j * MOE_BLOCK, xbuf[slot_of(g)], 0)
        return _swiglu_packed(x, wg_s, wu_s, wd_s)

    def one_block(j, carry):
        g = b0 + j
        arrive(g)
        ybuf[slot_of(g)] = ffn(j)
        out_copy(g).start()
        return carry

    lax.fori_loop(0, nb, one_block, 0)

    @pl.when(e == N_EXPERTS - 1)
    def _():
        for back in range(1, EXPERT_SLOTS + 1):
            @pl.when(n_all >= back)
            def _():
                out_copy(n_all - back).wait()


def _experts(n_blk, blk0, counts, n_rows, xs, w_gate, w_up, w_down, after):
    wspec_in = pl.BlockSpec((1, D_MODEL, EXPERT_DIM), lambda e, *_: (e, 0, 0))
    grid_spec = pltpu.PrefetchScalarGridSpec(
        num_scalar_prefetch=3,
        grid=(N_EXPERTS,),
        in_specs=[wspec_in, wspec_in,
                  pl.BlockSpec((1, EXPERT_DIM, D_MODEL), lambda e, *_: (e, 0, 0)),
                  pl.BlockSpec(memory_space=pl.ANY),
                  pl.BlockSpec(memory_space=pl.ANY)],
        out_specs=pl.BlockSpec(memory_space=pl.ANY),
        scratch_shapes=[pltpu.VMEM((D_MODEL, EXPERT_DIM), BF16), pltpu.VMEM((D_MODEL, EXPERT_DIM), BF16),
                        pltpu.VMEM((EXPERT_DIM, D_MODEL), BF16),
                        pltpu.VMEM((EXPERT_SLOTS, MOE_BLOCK, HALF), I32),
                        pltpu.VMEM((EXPERT_SLOTS, MOE_BLOCK, HALF), I32),
                        pltpu.SemaphoreType.DMA((EXPERT_SLOTS,)), pltpu.SemaphoreType.DMA((EXPERT_SLOTS,))],
    )
    return pl.pallas_call(
        _expert_kernel,
        grid_spec=grid_spec,
        out_shape=jax.ShapeDtypeStruct((n_rows, HALF), I32),
        compiler_params=pltpu.CompilerParams(
            dimension_semantics=("arbitrary",), vmem_limit_bytes=VMEM_LIMIT),
        name="experts",
    )(n_blk, blk0, counts, w_gate, w_up, w_down, xs, after)


def _combine_kernel(x1_ref, ysh_ref, moe_ref, p_ref, gple_ref, wpg_ref, wpp_ref, gpost_ref, *rest):
    out_ref = rest[-1]
    proj = _rms(_dot(p_ref[...].astype(BF16), wpp_ref[...]), gpost_ref[...])
    sh_lo, sh_hi = _unpack(ysh_ref[...])
    x2 = x1_ref[...] + moe_ref[...] + jnp.concatenate([sh_lo, sh_hi], axis=1)
    gate = _sigmoid(_dot(_rms(x2, gple_ref[...]).astype(BF16), wpg_ref[...]))
    out_ref[...] = x2 + gate * proj


def _combine(x1, ysh, moe, p2d, g_ple, w_ple_gate, w_ple_proj, g_ple_post, tok0, tok0_all, prev_out):
    t = p2d.shape[0]
    tm = TM_COMBINE
    steps = moe.shape[0] // tm
    off = tok0 // tm
    off_all = tok0_all // tm
    operands = [x1, ysh, moe, p2d, g_ple.reshape(1, D_MODEL), w_ple_gate.astype(BF16),
                w_ple_proj.astype(BF16), g_ple_post.reshape(1, D_MODEL)]
    in_specs = [pl.BlockSpec((tm, D_MODEL), lambda i: (off + i, 0)),
                pl.BlockSpec((tm, HALF), lambda i: (off + i, 0)),
                pl.BlockSpec((tm, D_MODEL), lambda i: (i, 0)),
                pl.BlockSpec((tm, PLE_DIM), lambda i: (off_all + i, 0)),
                _const_spec((1, D_MODEL)),
                _const_spec((D_MODEL, D_MODEL)),
                _const_spec((PLE_DIM, D_MODEL)),
                _const_spec((1, D_MODEL))]
    aliases = {}
    if prev_out is not None:
        aliases = {len(operands): 0}
        operands.append(prev_out)
        in_specs.append(pl.BlockSpec(memory_space=pl.ANY))
    return pl.pallas_call(
        _combine_kernel,
        grid=(steps,),
        in_specs=in_specs,
        out_specs=pl.BlockSpec((tm, D_MODEL), lambda i: (off_all + i, 0)),
        out_shape=jax.ShapeDtypeStruct((t, D_MODEL), F32),
        input_output_aliases=aliases,
        compiler_params=pltpu.CompilerParams(
            dimension_semantics=("arbitrary",), vmem_limit_bytes=VMEM_LIMIT),
        name="combine",
    )(*operands)


def _layer(x2d, p2d, batch, seq, g_mix, w_in, w_pool_mix, pool_scale, w_branch_a, q_gain, k_gain,
           attn_sinks, w_branch_b, w_gate, b_gate, w_out, g_ffn, w_router, router_bias, w_exp_gate,
           w_exp_up, w_exp_down, w_sh_gate, w_sh_up, w_sh_down, g_ple, w_ple_gate, w_ple_proj,
           g_ple_post):
    bh = batch // PIPELINE_HALVES
    th = bh * seq
    n_rows = (-(-(th * TOP_K) // MOE_BLOCK) + N_EXPERTS) * MOE_BLOCK

    halves = []
    after = None
    for h in range(PIPELINE_HALVES):
        x1, h2p, idx8, w_rep, rk8, cnt = _mixer(
            x2d, h * bh, bh, seq, g_mix, w_in, w_pool_mix, pool_scale, w_branch_a, q_gain, k_gain,
            attn_sinks, w_branch_b, w_gate, b_gate, w_out, g_ffn, w_router, router_bias, after=after)
        counts = cnt[:, 0].astype(I32)
        padded = (counts + MOE_BLOCK - 1) // MOE_BLOCK * MOE_BLOCK
        pend = jnp.cumsum(padded)
        pstart = pend - padded
        pos = _positions(pstart.astype(I32), idx8, rk8).reshape(-1)
        xs = _dispatch(pos, h2p, n_rows)
        blk0 = jnp.concatenate([pstart, pend[-1:]]) // MOE_BLOCK
        halves.append((x1, h2p, w_rep, pos, xs, (padded // MOE_BLOCK).astype(I32), blk0.astype(I32), counts))
        after = pos

    out = None
    tc = th // COMBINE_CHUNKS
    for h, (x1, h2p, w_rep, pos, xs, n_blk, blk0, counts) in enumerate(halves):
        ysh = _shared(h2p, w_sh_gate, w_sh_up, w_sh_down)
        ys = _experts(n_blk, blk0, counts, n_rows, xs, w_exp_gate, w_exp_up, w_exp_down, ysh)
        for c in range(COMBINE_CHUNKS):
            moe = _gather_sum(pos, w_rep, ys, c * tc, tc)
            out = _combine(x1, ysh, moe, p2d, g_ple, w_ple_gate, w_ple_proj, g_ple_post,
                           c * tc, h * th + c * tc, out)
    return out


def kernel(x, p, g_mix, w_in, w_pool_mix, pool_scale, w_branch_a, q_gain, k_gain, attn_sinks,
           w_branch_b, w_gate, b_gate, w_out, g_ffn, w_router, router_bias, w_exp_gate, w_exp_up,
           w_exp_down, w_sh_gate, w_sh_up, w_sh_down, g_ple, w_ple_gate, w_ple_proj, g_ple_post):
    batch, seq, d = x.shape
    depth = p.shape[0]
    x2d = x.reshape(batch * seq, d)
    for i in range(depth):
        x2d = _layer(x2d, p[i].reshape(batch * seq, PLE_DIM), batch, seq, g_mix[i], w_in[i],
                     w_pool_mix[i], pool_scale[i], w_branch_a[i], q_gain[i], k_gain[i], attn_sinks[i],
                     w_branch_b[i], w_gate[i], b_gate[i], w_out[i], g_ffn[i], w_router[i],
                     router_bias[i], w_exp_gate[i], w_exp_up[i], w_exp_down[i], w_sh_gate[i],
                     w_sh_up[i], w_sh_down[i], g_ple[i], w_ple_gate[i], w_ple_proj[i], g_ple_post[i])
    return x2d.reshape(batch, seq, d)
```

```python
import functools

import numpy as np
import jax
import jax.numpy as jnp
from jax import lax
from jax.experimental import pallas as pl
from jax.experimental.pallas import tpu as pltpu
from jax.experimental.pallas import tpu_sc as plsc

F32 = jnp.float32
BF16 = jnp.bfloat16
I32 = jnp.int32

D_MODEL = 1024
PLE_DIM = 256
POOL_WINDOWS = (2, 4, 8, 16)
POOL_GROUP_DIM = 128
POOL_DIM = 512
N_HEADS = 8
N_KV_HEADS = 2
Q_GROUP = 4
HEAD_DIM = 64
HEAD_PAD = 128
Q_DIM = 512
KV_DIM = 128
ATTN_BLOCK = 128
N_EXPERTS = 64
N_EXPERT_GROUPS = 8
GROUP_SIZE = 8
TOPK_GROUPS = 4
TOP_K = 8
EXPERT_DIM = 256
ROUTED_SCALE = 2.5
MOE_BLOCK = 512
EXPERT_SLOTS = 8
EPS = 1e-6
HALF = D_MODEL // 2
MASKED = -1e30

COL_Q = POOL_DIM
COL_K = COL_Q + N_HEADS * HEAD_PAD
COL_V = COL_K + N_KV_HEADS * HEAD_PAD
IN_PAD = COL_V + N_KV_HEADS * HEAD_PAD
HEAD_ORDER = (0, 2, 1, 3)

TM_MIX = 512
TM_FFN = 512
TM_COMBINE = 512
TM_POS = 4096
COMBINE_CHUNKS = 2
PIPELINE_HALVES = 2
SC_SCATTER_ROWS = 128
SC_SUM_ROWS = 8
SC_LANES = 16
POOL_TAIL = 8
VMEM_LIMIT = 56 * 1024 * 1024

_NT = (((1,), (1,)), ((), ()))


def _dot(a, b):
    return jnp.dot(a, b, preferred_element_type=F32)


def _rms(x, g):
    ms = jnp.mean(x * x, axis=-1, keepdims=True)
    return x * lax.rsqrt(ms + EPS) * g


def _sigmoid(x):
    return 0.5 * jnp.tanh(0.5 * x) + 0.5


def _pack(lo, hi):
    lo_bits = lax.bitcast_convert_type(lo.astype(BF16).astype(F32), I32)
    hi_bits = lax.bitcast_convert_type(hi.astype(BF16).astype(F32), I32)
    return (hi_bits & jnp.int32(-65536)) | lax.shift_right_logical(lo_bits, jnp.int32(16))


def _unpack(w):
    lo = lax.bitcast_convert_type(lax.shift_left(w, jnp.int32(16)), F32)
    hi = lax.bitcast_convert_type(w & jnp.int32(-65536), F32)
    return lo, hi


def _mixer_kernel(x_ref, gmix_ref, win_ref, qg_ref, kg_ref, bias_ref, sink_ref, ones_ref, wmix_ref,
                  pscale_ref, wa_ref, wb_ref, wg_ref, bg_ref, wout_ref,
                  gffn_ref, wrh_ref, wrl_ref, rb_ref, tri_ref, *rest):
    (x1_ref, h2p_ref, idx_ref, wt_ref, rk_ref, cnt_ref,
     kbuf, vt_buf, pbuf, zbt_buf, run_ref) = rest[-11:]
    s = pl.program_id(1)
    tm = x_ref.shape[0]

    @pl.when(s == 0)
    def _():
        kbuf[0:ATTN_BLOCK, :] = jnp.zeros((ATTN_BLOCK, kbuf.shape[1]), BF16)
        vt_buf[:, 0:ATTN_BLOCK] = jnp.zeros((vt_buf.shape[0], ATTN_BLOCK), BF16)
        pbuf[:, 0:POOL_TAIL, :] = jnp.zeros((pbuf.shape[0], POOL_TAIL, POOL_DIM), F32)

    x = x_ref[...]
    hb = _rms(x, gmix_ref[...]).astype(BF16)
    u = _dot(hb, win_ref[...])

    up = u[:, 0:POOL_DIM]
    t0 = POOL_TAIL
    pbuf[0, t0:t0 + tm, :] = up
    s2 = up + pbuf[0, t0 - 1:t0 - 1 + tm, :]
    pbuf[1, t0:t0 + tm, 128:512] = s2[:, 128:512]
    s4 = s2[:, 128:512] + pbuf[1, t0 - 2:t0 - 2 + tm, 128:512]
    pbuf[2, t0:t0 + tm, 256:512] = s4[:, 128:384]
    s8 = s4[:, 128:384] + pbuf[2, t0 - 4:t0 - 4 + tm, 256:512]
    pbuf[3, t0:t0 + tm, 384:512] = s8[:, 128:256]
    s16 = s8[:, 128:256] + pbuf[3, t0 - 8:t0 - 8 + tm, 384:512]
    for lvl in range(4):
        pbuf[lvl, 0:POOL_TAIL, 128 * lvl:512] = pbuf[lvl, tm:tm + POOL_TAIL, 128 * lvl:512]
    wsums = (s2[:, 0:128], s4[:, 0:128], s8[:, 0:128], s16)
    tpos = (s * tm).astype(F32) + lax.broadcasted_iota(I32, (tm, 1), 0).astype(F32)
    za_parts = []
    for g, w in enumerate(POOL_WINDOWS):
        inv_count = 1.0 / jnp.minimum(tpos + 1.0, float(w))
        pooled = wsums[g] * inv_count - up[:, 128 * g:128 * (g + 1)]
        mixed = _dot(pooled.astype(BF16), wmix_ref[g]) * pscale_ref[:, 128 * g:128 * (g + 1)]
        za_parts.append(mixed.astype(BF16))
    za = jnp.concatenate(za_parts, axis=1)

    ones2 = ones_ref[...]
    q = u[:, COL_Q:COL_K]
    q2 = (q * q).astype(BF16)
    qss = jnp.concatenate([_dot(q2[:, 256 * c:256 * (c + 1)], ones2) for c in range(N_HEADS // 2)], axis=1)
    qn = (q * lax.rsqrt(qss * (1.0 / HEAD_DIM) + EPS) * qg_ref[...]).astype(BF16)
    k = u[:, COL_K:COL_V]
    kss = _dot((k * k).astype(BF16), ones2)
    kbuf[ATTN_BLOCK:ATTN_BLOCK + tm, :] = (k * lax.rsqrt(kss * (1.0 / HEAD_DIM) + EPS) * kg_ref[...]).astype(BF16)
    vt_buf[:, ATTN_BLOCK:ATTN_BLOCK + tm] = u[:, COL_V:IN_PAD].T.astype(BF16)

    nq = Q_GROUP * ATTN_BLOCK
    key_j = lax.broadcasted_iota(I32, (ATTN_BLOCK, nq), 0)
    qry_i = lax.broadcasted_iota(I32, (ATTN_BLOCK, nq), 1) & (ATTN_BLOCK - 1)
    from_prev = key_j > qry_i
    first = jnp.where(s == 0, 1, 0)
    for n in range(tm // ATTN_BLOCK):
        r0 = ATTN_BLOCK * n
        for kv in range(N_KV_HEADS):
            c0 = HEAD_PAD * kv
            qs = jnp.concatenate(
                [qn[r0:r0 + ATTN_BLOCK, HEAD_PAD * (kv * Q_GROUP + g):HEAD_PAD * (kv * Q_GROUP + g + 1)]
                 for g in HEAD_ORDER], axis=0)
            kk = kbuf[r0:r0 + 2 * ATTN_BLOCK, c0:c0 + HEAD_PAD]
            st = lax.dot_general(kk, qs, _NT, preferred_element_type=F32)
            sc = jnp.where(from_prev, st[0:ATTN_BLOCK], st[ATTN_BLOCK:2 * ATTN_BLOCK])
            sc = sc + (bias_ref[first, kv] if n == 0 else bias_ref[0, kv])
            sink = sink_ref[kv]
            m = jnp.maximum(jnp.max(sc, axis=0, keepdims=True), sink)
            e = jnp.exp(sc - m)
            inv_den = 1.0 / (jnp.sum(e, axis=0, keepdims=True) + jnp.exp(sink - m))
            pt = jnp.concatenate([jnp.where(from_prev, e, 0.0), jnp.where(from_prev, 0.0, e)],
                                 axis=0).astype(BF16)
            v_even = vt_buf[c0:c0 + HEAD_PAD, r0:r0 + 2 * ATTN_BLOCK]
            v_odd = jnp.concatenate([v_even[HEAD_DIM:HEAD_PAD], v_even[0:HEAD_DIM]], axis=0)
            half = 2 * ATTN_BLOCK
            o = (_dot(v_even, pt[:, 0:half]) * inv_den[:, 0:half]
                 + _dot(v_odd, pt[:, half:2 * half]) * inv_den[:, half:2 * half])
            ch = kv * Q_GROUP * HEAD_DIM
            zbt_buf[ch:ch + HEAD_PAD, r0:r0 + ATTN_BLOCK] = o[:, 0:ATTN_BLOCK]
            zbt_buf[ch + HEAD_PAD:ch + 2 * HEAD_PAD, r0:r0 + ATTN_BLOCK] = o[:, ATTN_BLOCK:half]
    kbuf[0:ATTN_BLOCK, :] = kbuf[tm:tm + ATTN_BLOCK, :]
    vt_buf[:, 0:ATTN_BLOCK] = vt_buf[:, tm:tm + ATTN_BLOCK]

    y_a = _dot(za, wa_ref[...])
    y_b = _dot(zbt_buf[...].T.astype(BF16), wb_ref[...])
    g_a = _sigmoid(_dot(hb, wg_ref[:, 0:D_MODEL]) + bg_ref[:, 0:D_MODEL])
    merged = g_a * y_a
    g_b = _sigmoid(_dot(hb, wg_ref[:, D_MODEL:2 * D_MODEL]) + bg_ref[:, D_MODEL:2 * D_MODEL])
    merged = (merged + g_b * y_b).astype(BF16)
    x1 = x + _dot(merged, wout_ref[...])
    x1_ref[...] = x1

    is_first = (pl.program_id(0) == 0) & (s == 0)
    _route_tile(x1, is_first, gffn_ref, wrh_ref, wrl_ref, rb_ref, tri_ref,
                h2p_ref, idx_ref, wt_ref, rk_ref, cnt_ref, run_ref)


def _attn_tables(attn_sinks):
    slopes = 2.0 ** (-8.0 * (np.arange(N_HEADS) + 1) / N_HEADS)
    j = np.arange(ATTN_BLOCK)[:, None]
    i = np.arange(ATTN_BLOCK)[None, :]
    from_prev = j > i
    dist = np.where(from_prev, ATTN_BLOCK + i - j, i - j)
    bias = np.empty((2, N_KV_HEADS, ATTN_BLOCK, Q_GROUP * ATTN_BLOCK), np.float32)
    for first in range(2):
        ok = ~from_prev if first else np.ones_like(from_prev)
        for kv in range(N_KV_HEADS):
            for slot, g in enumerate(HEAD_ORDER):
                sl = np.float32(slopes[kv * Q_GROUP + g])
                val = -(sl * dist.astype(np.float32))
                bias[first, kv, :, slot * ATTN_BLOCK:(slot + 1) * ATTN_BLOCK] = np.where(ok, val, MASKED)
    sink = attn_sinks.astype(F32).reshape(N_KV_HEADS, Q_GROUP)[:, np.asarray(HEAD_ORDER)]
    sink = jnp.repeat(sink, ATTN_BLOCK, axis=1)
    return jnp.asarray(bias), sink.reshape(N_KV_HEADS, 1, Q_GROUP * ATTN_BLOCK)


def _pad_heads(w, n_heads):
    k = w.shape[0]
    w = w.reshape(k, n_heads, HEAD_DIM)
    w = jnp.pad(w, ((0, 0), (0, 0), (0, HEAD_PAD - HEAD_DIM)))
    return w.reshape(k, n_heads * HEAD_PAD)


def _const_spec(shape):
    nd = len(shape)
    return pl.BlockSpec(shape, lambda *_: (0,) * nd)


def _mixer(x2d, batch0, batch, seq, g_mix, w_in, w_pool_mix, pool_scale, w_branch_a, q_gain, k_gain,
           attn_sinks, w_branch_b, w_gate, b_gate, w_out, g_ffn, w_router, router_bias, after=None):
    t = batch * seq
    tm = TM_MIX
    ns = seq // tm
    w_q = w_in[:, POOL_DIM:POOL_DIM + Q_DIM]
    w_k = w_in[:, POOL_DIM + Q_DIM:POOL_DIM + Q_DIM + KV_DIM]
    w_v = w_in[:, POOL_DIM + Q_DIM + KV_DIM:]
    win_p = jnp.concatenate([w_in[:, :POOL_DIM], _pad_heads(w_q, N_HEADS), _pad_heads(w_k, N_KV_HEADS),
                             _pad_heads(w_v, N_KV_HEADS)], axis=1).astype(BF16)
    qg = jnp.tile(jnp.pad(q_gain * (HEAD_DIM ** -0.5), (0, HEAD_PAD - HEAD_DIM)), N_HEADS).reshape(1, -1)
    kg = jnp.tile(jnp.pad(k_gain, (0, HEAD_PAD - HEAD_DIM)), N_KV_HEADS).reshape(1, -1)
    bias, sink = _attn_tables(attn_sinks)
    ones2 = jnp.asarray(np.kron(np.eye(2), np.ones((HEAD_PAD, HEAD_PAD))), BF16)
    operands = (x2d, g_mix.reshape(1, D_MODEL), win_p, qg, kg, bias, sink, ones2, w_pool_mix.astype(BF16),
                pool_scale.reshape(1, POOL_DIM), w_branch_a.astype(BF16), w_branch_b.astype(BF16),
                w_gate.astype(BF16), b_gate.reshape(1, 2 * D_MODEL), w_out.astype(BF16))
    operands += _router_operands(g_ffn, w_router, router_bias, tm)
    in_specs = [pl.BlockSpec((tm, D_MODEL), lambda b, s: ((batch0 + b) * ns + s, 0))]
    in_specs += [_const_spec(op.shape) for op in operands[1:]]
    if after is not None:
        operands += (after,)
        in_specs.append(pl.BlockSpec(memory_space=pl.ANY))
    row8 = pl.BlockSpec((TOP_K, tm), lambda b, s: (0, b * ns + s))
    return pl.pallas_call(
        _mixer_kernel,
        grid=(batch, ns),
        in_specs=in_specs,
        out_specs=[pl.BlockSpec((tm, D_MODEL), lambda b, s: (b * ns + s, 0)),
                   pl.BlockSpec((tm, HALF), lambda b, s: (b * ns + s, 0)),
                   row8,
                   pl.BlockSpec((tm, TOP_K * SC_LANES), lambda b, s: (b * ns + s, 0)),
                   row8,
                   pl.BlockSpec((N_EXPERTS, 128), lambda b, s: (0, 0))],
        out_shape=[jax.ShapeDtypeStruct((t, D_MODEL), F32),
                   jax.ShapeDtypeStruct((t, HALF), I32),
                   jax.ShapeDtypeStruct((TOP_K, t), I32),
                   jax.ShapeDtypeStruct((t, TOP_K * SC_LANES), F32),
                   jax.ShapeDtypeStruct((TOP_K, t), I32),
                   jax.ShapeDtypeStruct((N_EXPERTS, 128), F32)],
        scratch_shapes=[
            pltpu.VMEM((ATTN_BLOCK + tm, N_KV_HEADS * HEAD_PAD), BF16),
            pltpu.VMEM((N_KV_HEADS * HEAD_PAD, ATTN_BLOCK + tm), BF16),
            pltpu.VMEM((4, POOL_TAIL + tm, POOL_DIM), F32),
            pltpu.VMEM((Q_DIM, tm), F32),
            pltpu.VMEM((N_EXPERTS, 128), F32),
        ],
        compiler_params=pltpu.CompilerParams(
            dimension_semantics=("arbitrary", "arbitrary"), vmem_limit_bytes=VMEM_LIMIT),
        name="mixer",
    )(*operands)


def _route_tile(x1, is_first, gffn_ref, wrh_ref, wrl_ref, rb_ref, tri_ref,
                h2p_ref, idx_ref, wt_ref, rk_ref, cnt_ref, run_ref):
    tm = x1.shape[0]

    @pl.when(is_first)
    def _():
        run_ref[...] = jnp.zeros_like(run_ref)

    h2 = _rms(x1, gffn_ref[...])
    hb = h2.astype(BF16)
    h2p_ref[...] = _pack(h2[:, 0:HALF], h2[:, HALF:D_MODEL])

    lo = (h2 - hb.astype(F32)).astype(BF16)
    wrh = wrh_ref[...]
    logits = (lax.dot_general(wrh, hb, _NT, preferred_element_type=F32)
              + lax.dot_general(wrl_ref[...], hb, _NT, preferred_element_type=F32)
              + lax.dot_general(wrh, lo, _NT, preferred_element_type=F32))
    scores = _sigmoid(logits)
    choice = scores + rb_ref[...]

    neg = -jnp.inf
    sub8 = lax.broadcasted_iota(I32, (GROUP_SIZE, tm), 0).astype(F32)
    grp_rows = []
    for gi in range(N_EXPERT_GROUPS):
        cg = choice[GROUP_SIZE * gi:GROUP_SIZE * (gi + 1)]
        m1 = jnp.max(cg, axis=0, keepdims=True)
        first = jnp.min(jnp.where(cg == m1, sub8, float(GROUP_SIZE)), axis=0, keepdims=True)
        m2 = jnp.max(jnp.where(sub8 == first, neg, cg), axis=0, keepdims=True)
        grp_rows.append(m1 + m2)
    gs = jnp.concatenate(grp_rows, axis=0)
    beaten = jnp.zeros((N_EXPERT_GROUPS, tm), F32)
    for gi in range(N_EXPERT_GROUPS):
        row = grp_rows[gi]
        wins = (row > gs) | ((row == gs) & (sub8 > float(gi)))
        beaten = beaten + jnp.where(wins, 1.0, 0.0)
    gsel = beaten < float(TOPK_GROUPS)
    cm = jnp.concatenate(
        [jnp.where(gsel[gi:gi + 1], choice[GROUP_SIZE * gi:GROUP_SIZE * (gi + 1)], neg)
         for gi in range(N_EXPERT_GROUPS)], axis=0)

    iota_e = lax.broadcasted_iota(I32, (N_EXPERTS, tm), 0).astype(F32)
    idx_rows, s_rows = [], []
    sel = jnp.zeros((N_EXPERTS, tm), F32)
    for _ in range(TOP_K):
        m = jnp.max(cm, axis=0, keepdims=True)
        idx = jnp.min(jnp.where(cm == m, iota_e, float(N_EXPERTS)), axis=0, keepdims=True)
        oh = iota_e == idx
        s_rows.append(jnp.sum(jnp.where(oh, scores, 0.0), axis=0, keepdims=True))
        idx_rows.append(idx)
        cm = jnp.where(oh, neg, cm)
        sel = sel + jnp.where(oh, 1.0, 0.0)

    run = run_ref[:, 0:1]
    cum = _dot(sel.astype(BF16), tri_ref[...])
    before = run + cum - sel
    rk_rows = [jnp.sum(jnp.where(iota_e == idx, before, 0.0), axis=0, keepdims=True) for idx in idx_rows]
    new_run = run + jnp.sum(sel, axis=1, keepdims=True)
    run_ref[...] = jnp.broadcast_to(new_run, run_ref.shape)
    cnt_ref[...] = jnp.broadcast_to(new_run, cnt_ref.shape)

    ssum = s_rows[0]
    for r in range(1, TOP_K):
        ssum = ssum + s_rows[r]
    denom = ssum + 1e-20
    idx_ref[...] = jnp.concatenate(idx_rows, axis=0).astype(I32)
    w_rep = jnp.concatenate([jnp.broadcast_to(sr / denom * ROUTED_SCALE, (SC_LANES, tm)) for sr in s_rows],
                            axis=0)
    wt_ref[...] = w_rep.T
    rk_ref[...] = jnp.concatenate(rk_rows, axis=0).astype(I32)


def _router_operands(g_ffn, w_router, router_bias, tm):
    wr_t = w_router.astype(F32).T
    wr_hi = wr_t.astype(BF16)
    wr_lo = (wr_t - wr_hi.astype(F32)).astype(BF16)
    tri = (np.arange(tm)[:, None] <= np.arange(tm)[None, :]).astype(np.float32)
    return (g_ffn.reshape(1, D_MODEL), wr_hi, wr_lo, router_bias.astype(F32).reshape(N_EXPERTS, 1),
            jnp.asarray(tri, BF16))


def _swiglu_packed(x_packed, wg, wu, wd):
    lo, hi = _unpack(x_packed)
    lo = lo.astype(BF16)
    hi = hi.astype(BF16)
    g = _dot(lo, wg[0:HALF, :]) + _dot(hi, wg[HALF:D_MODEL, :])
    u = _dot(lo, wu[0:HALF, :]) + _dot(hi, wu[HALF:D_MODEL, :])
    a = (g * _sigmoid(g) * u).astype(BF16)
    y = _dot(a, wd[...])
    return _pack(y[:, 0:HALF], y[:, HALF:D_MODEL])


def _shared_kernel(h2p_ref, wg_ref, wu_ref, wd_ref, ysh_ref):
    ysh_ref[...] = _swiglu_packed(h2p_ref[...], wg_ref, wu_ref, wd_ref)


def _shared(h2p, w_sh_gate, w_sh_up, w_sh_down):
    t = h2p.shape[0]
    tm = TM_FFN
    operands = (h2p, w_sh_gate.astype(BF16), w_sh_up.astype(BF16), w_sh_down.astype(BF16))
    in_specs = [pl.BlockSpec((tm, HALF), lambda i: (i, 0))]
    in_specs += [_const_spec(op.shape) for op in operands[1:]]
    return pl.pallas_call(
        _shared_kernel,
        grid=(t // tm,),
        in_specs=in_specs,
        out_specs=pl.BlockSpec((tm, HALF), lambda i: (i, 0)),
        out_shape=jax.ShapeDtypeStruct((t, HALF), I32),
        compiler_params=pltpu.CompilerParams(
            dimension_semantics=("arbitrary",), vmem_limit_bytes=VMEM_LIMIT),
        name="shared",
    )(*operands)


def _positions_kernel(pstart_ref, idx_ref, rk_ref, pos_ref):
    idx = idx_ref[...]
    acc = rk_ref[...]
    for e in range(N_EXPERTS):
        acc = acc + jnp.where(idx == e, pstart_ref[e], 0)
    pos_ref[...] = acc


def _positions(pstart, idx8, rk8):
    t = idx8.shape[1]
    tm = TM_POS
    spec = pl.BlockSpec((TOP_K, tm), lambda i, ps: (0, i))
    return pl.pallas_call(
        _positions_kernel,
        grid_spec=pltpu.PrefetchScalarGridSpec(num_scalar_prefetch=1, grid=(t // tm,),
                                               in_specs=[spec, spec], out_specs=spec),
        out_shape=jax.ShapeDtypeStruct((TOP_K, t), I32),
        name="positions",
    )(pstart, idx8, rk8)


def _sc_workers():
    info = plsc.get_sparse_core_info()
    return info.num_cores, info.num_cores * info.num_subcores


def _dispatch(pos_flat, h2p, n_rows):
    t = h2p.shape[0]
    ch = SC_SCATTER_ROWS
    n_cores, n_workers = _sc_workers()
    n_ch = t // (n_workers * ch)

    def body(pos_hbm, h_hbm, xs_hbm, idx_v, rows_v, sem):
        wid = lax.axis_index("s") * n_cores + lax.axis_index("c")

        @pl.loop(0, n_ch)
        def _(c):
            chunk = wid * n_ch + c
            pltpu.sync_copy(h_hbm.at[pl.ds(chunk * ch, ch)], rows_v)
            for k in range(TOP_K):
                pltpu.sync_copy(pos_hbm.at[pl.ds(k * t + chunk * ch, ch)], idx_v.at[k])
            copies = [pltpu.async_copy(rows_v, xs_hbm.at[idx_v.at[k]], sem) for k in range(TOP_K)]
            for cp in copies:
                cp.wait()

    return pl.kernel(
        body,
        out_type=jax.ShapeDtypeStruct((n_rows, HALF), I32),
        mesh=plsc.VectorSubcoreMesh(core_axis_name="c", subcore_axis_name="s"),
        scratch_types=[pltpu.VMEM((TOP_K, ch), I32), pltpu.VMEM((ch, HALF), I32),
                       pltpu.SemaphoreType.DMA],
        name="dispatch",
    )(pos_flat, h2p)


def _pairwise_sum(xs):
    while len(xs) > 1:
        xs = [xs[i] + xs[i + 1] for i in range(0, len(xs), 2)]
    return xs[0]


def _gather_sum(pos_flat, w_rep, ys, tok0, t):
    t_all = w_rep.shape[0]
    ch = SC_SUM_ROWS
    lanes = SC_LANES
    n_cores, n_workers = _sc_workers()
    per_w = t // n_workers
    n_ch = per_w // ch

    def body(pos_hbm, w_hbm, ys_hbm, out_hbm, idx_all, w_all, buf0, buf1, acc0, acc1, gsem0, gsem1, osem0, osem1):
        wid = lax.axis_index("s") * n_cores + lax.axis_index("c")
        base = wid * n_ch
        first = tok0 + wid * per_w
        bufs, accs, gsems, osems = (buf0, buf1), (acc0, acc1), (gsem0, gsem1), (osem0, osem1)
        for k in range(TOP_K):
            pltpu.sync_copy(pos_hbm.at[pl.ds(k * t_all + first, per_w)], idx_all.at[pl.ds(k * per_w, per_w)])
        pltpu.sync_copy(w_hbm.at[pl.ds(first, per_w)], w_all)

        def row_copy(c, s, k):
            rows = idx_all.at[pl.ds(k * per_w + c * ch, ch)]
            return pltpu.make_async_copy(ys_hbm.at[rows], bufs[s].at[k], gsems[s])

        def out_copy(c, s):
            return pltpu.make_async_copy(accs[s], out_hbm.at[pl.ds((base + c) * ch, ch)], osems[s])

        def fetch(c, s):
            for k in range(TOP_K):
                row_copy(c, s, k).start()

        def reduce_rows(c, s):
            @pl.loop(0, ch)
            def _(r):
                wks = [w_all[c * ch + r, pl.ds(k * lanes, lanes)] for k in range(TOP_K)]

                @pl.loop(0, HALF // lanes, step=2)
                def _(v):
                    for dv in range(2):
                        col = (v + dv) * lanes
                        los, his = [], []
                        for k in range(TOP_K):
                            words = bufs[s][k, r, pl.ds(col, lanes)]
                            los.append(wks[k] * lax.bitcast_convert_type(lax.shift_left(words, jnp.int32(16)), F32))
                            his.append(wks[k] * lax.bitcast_convert_type(words & jnp.int32(-65536), F32))
                        accs[s][r, pl.ds(col, lanes)] = _pairwise_sum(los)
                        accs[s][r, pl.ds(HALF + col, lanes)] = _pairwise_sum(his)

        fetch(0, 0)

        @pl.loop(0, n_ch, step=2)
        def _(c):
            for s in range(2):
                cc = c + s

                @pl.when(cc + 1 < n_ch)
                def _():
                    fetch(cc + 1, 1 - s)

                for k in range(TOP_K):
                    row_copy(cc, s, k).wait()

                @pl.when(cc >= 2)
                def _():
                    out_copy(cc - 2, s).wait()

                reduce_rows(cc, s)
                out_copy(cc, s).start()

        out_copy(n_ch - 2, 0).wait()
        out_copy(n_ch - 1, 1).wait()

    return pl.kernel(
        body,
        out_type=jax.ShapeDtypeStruct((t, D_MODEL), F32),
        mesh=plsc.VectorSubcoreMesh(core_axis_name="c", subcore_axis_name="s"),
        scratch_types=[pltpu.VMEM((TOP_K * per_w,), I32), pltpu.VMEM((per_w, TOP_K * lanes), F32),
                       pltpu.VMEM((TOP_K, ch, HALF), I32), pltpu.VMEM((TOP_K, ch, HALF), I32),
                       pltpu.VMEM((ch, D_MODEL), F32), pltpu.VMEM((ch, D_MODEL), F32),
                       pltpu.SemaphoreType.DMA, pltpu.SemaphoreType.DMA,
                       pltpu.SemaphoreType.DMA, pltpu.SemaphoreType.DMA],
        compiler_params=pltpu.CompilerParams(needs_layout_passes=False),
        name="gather_sum",
    )(pos_flat, w_rep, ys)


def _expert_kernel(nblk_ref, blk0_ref, cnt_ref, wg_ref, wu_ref, wd_ref, xs_hbm, after_ref, ys_hbm,
                   wg_s, wu_s, wd_s, xbuf, ybuf, in_sem, out_sem):
    del after_ref
    e = pl.program_id(0)
    nb = nblk_ref[e]
    b0 = blk0_ref[e]
    cnt = cnt_ref[e]
    n_all = blk0_ref[N_EXPERTS]
    ahead = EXPERT_SLOTS - 2
    wg_s[...] = wg_ref[0].astype(BF16)
    wu_s[...] = wu_ref[0].astype(BF16)
    wd_s[...] = wd_ref[0].astype(BF16)

    def slot_rows(g, n_blocks=1):
        start = (g & (EXPERT_SLOTS - 1)) * MOE_BLOCK
        return pl.ds(pl.multiple_of(start, MOE_BLOCK), n_blocks * MOE_BLOCK)

    def rows_of(g):
        return pl.ds(pl.multiple_of(g * MOE_BLOCK, MOE_BLOCK), MOE_BLOCK)

    def in_copy(g):
        return pltpu.make_async_copy(xs_hbm.at[rows_of(g)], xbuf.at[slot_rows(g)],
                                     in_sem.at[g & (EXPERT_SLOTS - 1)])

    def out_copy(g):
        return pltpu.make_async_copy(ybuf.at[slot_rows(g)], ys_hbm.at[rows_of(g)],
                                     out_sem.at[g & (EXPERT_SLOTS - 1)])

    @pl.when(e == 0)
    def _():
        for g in range(ahead):
            @pl.when(g < n_all)
            def _():
                in_copy(g).start()

    def arrive(g):
        in_copy(g).wait()

        @pl.when(g + ahead < n_all)
        def _():
            in_copy(g + ahead).start()

        @pl.when(g >= EXPERT_SLOTS)
        def _():
            out_copy(g - EXPERT_SLOTS).wait()

    def run_blocks(j, n_blocks):
        g = b0 + j
        for i in range(n_blocks):
            arrive(g + i)
        rows = lax.broadcasted_iota(I32, (n_blocks * MOE_BLOCK, HALF), 0)
        x = jnp.where(rows < cnt - j * MOE_BLOCK, xbuf[slot_rows(g, n_blocks)], 0)
        ybuf[slot_rows(g, n_blocks)] = _swiglu_packed(x, wg_s, wu_s, wd_s)
        for i in range(n_blocks):
            out_copy(g + i).start()

    lead = jnp.where(nb > 0, b0 & 1, 0)

    @pl.when(lead == 1)
    def _():
        run_blocks(0, 1)

    n_pairs = (nb - lead) // 2

    def one_pair(p, carry):
        run_blocks(lead + 2 * p, 2)
        return carry

    lax.fori_loop(0, n_pairs, one_pair, 0)

    @pl.when(nb - lead - 2 * n_pairs == 1)
    def _():
        run_blocks(nb - 1, 1)

    @pl.when(e == N_EXPERTS - 1)
    def _():
        for back in range(1, EXPERT_SLOTS + 1):
            @pl.when(n_all >= back)
            def _():
                out_copy(n_all - back).wait()


def _experts(n_blk, blk0, counts, n_rows, xs, w_gate, w_up, w_down, after):
    wspec_in = pl.BlockSpec((1, D_MODEL, EXPERT_DIM), lambda e, *_: (e, 0, 0))
    grid_spec = pltpu.PrefetchScalarGridSpec(
        num_scalar_prefetch=3,
        grid=(N_EXPERTS,),
        in_specs=[wspec_in, wspec_in,
                  pl.BlockSpec((1, EXPERT_DIM, D_MODEL), lambda e, *_: (e, 0, 0)),
                  pl.BlockSpec(memory_space=pl.ANY),
                  pl.BlockSpec(memory_space=pl.ANY)],
        out_specs=pl.BlockSpec(memory_space=pl.ANY),
        scratch_shapes=[pltpu.VMEM((D_MODEL, EXPERT_DIM), BF16), pltpu.VMEM((D_MODEL, EXPERT_DIM), BF16),
                        pltpu.VMEM((EXPERT_DIM, D_MODEL), BF16),
                        pltpu.VMEM((EXPERT_SLOTS * MOE_BLOCK, HALF), I32),
                        pltpu.VMEM((EXPERT_SLOTS * MOE_BLOCK, HALF), I32),
                        pltpu.SemaphoreType.DMA((EXPERT_SLOTS,)), pltpu.SemaphoreType.DMA((EXPERT_SLOTS,))],
    )
    return pl.pallas_call(
        _expert_kernel,
        grid_spec=grid_spec,
        out_shape=jax.ShapeDtypeStruct((n_rows, HALF), I32),
        compiler_params=pltpu.CompilerParams(
            dimension_semantics=("arbitrary",), vmem_limit_bytes=VMEM_LIMIT),
        name="experts",
    )(n_blk, blk0, counts, w_gate, w_up, w_down, xs, after)


def _combine_kernel(x1_ref, ysh_ref, moe_ref, p_ref, gple_ref, wpg_ref, wpp_ref, gpost_ref, *rest):
    out_ref = rest[-1]
    proj = _rms(_dot(p_ref[...].astype(BF16), wpp_ref[...]), gpost_ref[...])
    sh_lo, sh_hi = _unpack(ysh_ref[...])
    x2 = x1_ref[...] + moe_ref[...] + jnp.concatenate([sh_lo, sh_hi], axis=1)
    gate = _sigmoid(_dot(_rms(x2, gple_ref[...]).astype(BF16), wpg_ref[...]))
    out_ref[...] = x2 + gate * proj


def _combine(x1, ysh, moe, p2d, g_ple, w_ple_gate, w_ple_proj, g_ple_post, tok0, tok0_all, prev_out):
    t = p2d.shape[0]
    tm = TM_COMBINE
    steps = moe.shape[0] // tm
    off = tok0 // tm
    off_all = tok0_all // tm
    operands = [x1, ysh, moe, p2d, g_ple.reshape(1, D_MODEL), w_ple_gate.astype(BF16),
                w_ple_proj.astype(BF16), g_ple_post.reshape(1, D_MODEL)]
    in_specs = [pl.BlockSpec((tm, D_MODEL), lambda i: (off + i, 0)),
                pl.BlockSpec((tm, HALF), lambda i: (off + i, 0)),
                pl.BlockSpec((tm, D_MODEL), lambda i: (i, 0)),
                pl.BlockSpec((tm, PLE_DIM), lambda i: (off_all + i, 0)),
                _const_spec((1, D_MODEL)),
                _const_spec((D_MODEL, D_MODEL)),
                _const_spec((PLE_DIM, D_MODEL)),
                _const_spec((1, D_MODEL))]
    aliases = {}
    if prev_out is not None:
        aliases = {len(operands): 0}
        operands.append(prev_out)
        in_specs.append(pl.BlockSpec(memory_space=pl.ANY))
    return pl.pallas_call(
        _combine_kernel,
        grid=(steps,),
        in_specs=in_specs,
        out_specs=pl.BlockSpec((tm, D_MODEL), lambda i: (off_all + i, 0)),
        out_shape=jax.ShapeDtypeStruct((t, D_MODEL), F32),
        input_output_aliases=aliases,
        compiler_params=pltpu.CompilerParams(
            dimension_semantics=("arbitrary",), vmem_limit_bytes=VMEM_LIMIT),
        name="combine",
    )(*operands)


def _layer(x2d, p2d, batch, seq, g_mix, w_in, w_pool_mix, pool_scale, w_branch_a, q_gain, k_gain,
           attn_sinks, w_branch_b, w_gate, b_gate, w_out, g_ffn, w_router, router_bias, w_exp_gate,
           w_exp_up, w_exp_down, w_sh_gate, w_sh_up, w_sh_down, g_ple, w_ple_gate, w_ple_proj,
           g_ple_post):
    bh = batch // PIPELINE_HALVES
    th = bh * seq
    n_rows = (-(-(th * TOP_K) // MOE_BLOCK) + N_EXPERTS) * MOE_BLOCK

    halves = []
    after = None
    for h in range(PIPELINE_HALVES):
        x1, h2p, idx8, w_rep, rk8, cnt = _mixer(
            x2d, h * bh, bh, seq, g_mix, w_in, w_pool_mix, pool_scale, w_branch_a, q_gain, k_gain,
            attn_sinks, w_branch_b, w_gate, b_gate, w_out, g_ffn, w_router, router_bias, after=after)
        counts = cnt[:, 0].astype(I32)
        padded = (counts + MOE_BLOCK - 1) // MOE_BLOCK * MOE_BLOCK
        pend = jnp.cumsum(padded)
        pstart = pend - padded
        pos = _positions(pstart.astype(I32), idx8, rk8).reshape(-1)
        xs = _dispatch(pos, h2p, n_rows)
        blk0 = jnp.concatenate([pstart, pend[-1:]]) // MOE_BLOCK
        halves.append((x1, h2p, w_rep, pos, xs, (padded // MOE_BLOCK).astype(I32), blk0.astype(I32), counts))
        after = pos

    out = None
    tc = th // COMBINE_CHUNKS
    for h, (x1, h2p, w_rep, pos, xs, n_blk, blk0, counts) in enumerate(halves):
        ysh = _shared(h2p, w_sh_gate, w_sh_up, w_sh_down)
        ys = _experts(n_blk, blk0, counts, n_rows, xs, w_exp_gate, w_exp_up, w_exp_down, ysh)
        for c in range(COMBINE_CHUNKS):
            moe = _gather_sum(pos, w_rep, ys, c * tc, tc)
            out = _combine(x1, ysh, moe, p2d, g_ple, w_ple_gate, w_ple_proj, g_ple_post,
                           c * tc, h * th + c * tc, out)
    return out


def kernel(x, p, g_mix, w_in, w_pool_mix, pool_scale, w_branch_a, q_gain, k_gain, attn_sinks,
           w_branch_b, w_gate, b_gate, w_out, g_ffn, w_router, router_bias, w_exp_gate, w_exp_up,
           w_exp_down, w_sh_gate, w_sh_up, w_sh_down, g_ple, w_ple_gate, w_ple_proj, g_ple_post):
    batch, seq, d = x.shape
    depth = p.shape[0]
    x2d = x.reshape(batch * seq, d)
    for i in range(depth):
        x2d = _layer(x2d, p[i].reshape(batch * seq, PLE_DIM), batch, seq, g_mix[i], w_in[i],
                     w_pool_mix[i], pool_scale[i], w_branch_a[i], q_gain[i], k_gain[i], attn_sinks[i],
                     w_branch_b[i], w_gate[i], b_gate[i], w_out[i], g_ffn[i], w_router[i],
                     router_bias[i], w_exp_gate[i], w_exp_up[i], w_exp_down[i], w_sh_gate[i],
                     w_sh_up[i], w_sh_down[i], g_ple[i], w_ple_gate[i], w_ple_proj[i], g_ple_post[i])
    return x2d.reshape(batch, seq, d)
```

```python
import functools

import numpy as np
import jax
import jax.numpy as jnp
from jax import lax
from jax.experimental import pallas as pl
from jax.experimental.pallas import tpu as pltpu
from jax.experimental.pallas import tpu_sc as plsc

F32 = jnp.float32
BF16 = jnp.bfloat16
I32 = jnp.int32

D_MODEL = 1024
PLE_DIM = 256
POOL_WINDOWS = (2, 4, 8, 16)
POOL_GROUP_DIM = 128
POOL_DIM = 512
N_HEADS = 8
N_KV_HEADS = 2
Q_GROUP = 4
HEAD_DIM = 64
HEAD_PAD = 128
Q_DIM = 512
KV_DIM = 128
ATTN_BLOCK = 128
N_EXPERTS = 64
N_EXPERT_GROUPS = 8
GROUP_SIZE = 8
TOPK_GROUPS = 4
TOP_K = 8
EXPERT_DIM = 256
ROUTED_SCALE = 2.5
MOE_BLOCK = 512
EXPERT_SLOTS = 8
EPS = 1e-6
HALF = D_MODEL // 2
MASKED = -1e30

COL_Q = POOL_DIM
COL_K = COL_Q + N_HEADS * HEAD_PAD
COL_V = COL_K + N_KV_HEADS * HEAD_PAD
IN_PAD = COL_V + N_KV_HEADS * HEAD_PAD
HEAD_ORDER = (0, 2, 1, 3)

TM_MIX = 512
TM_FFN = 512
TM_COMBINE = 512
TM_POS = 4096
COMBINE_CHUNKS = 2
PIPELINE_HALVES = 2
SC_SCATTER_ROWS = 128
SC_SUM_ROWS = 8
SC_LANES = 16
POOL_TAIL = 8
VMEM_LIMIT = 56 * 1024 * 1024

_NT = (((1,), (1,)), ((), ()))


def _dot(a, b):
    return jnp.dot(a, b, preferred_element_type=F32)


def _rms(x, g):
    ms = jnp.mean(x * x, axis=-1, keepdims=True)
    return x * lax.rsqrt(ms + EPS) * g


def _sigmoid(x):
    return 0.5 * jnp.tanh(0.5 * x) + 0.5


def _pack(lo, hi):
    lo_bits = lax.bitcast_convert_type(lo.astype(BF16).astype(F32), I32)
    hi_bits = lax.bitcast_convert_type(hi.astype(BF16).astype(F32), I32)
    return (hi_bits & jnp.int32(-65536)) | lax.shift_right_logical(lo_bits, jnp.int32(16))


def _unpack(w):
    lo = lax.bitcast_convert_type(lax.shift_left(w, jnp.int32(16)), F32)
    hi = lax.bitcast_convert_type(w & jnp.int32(-65536), F32)
    return lo, hi


def _mixer_kernel(x_ref, gmix_ref, win_ref, qg_ref, kg_ref, bias_ref, sink_ref, ones_ref, wmix_ref,
                  pscale_ref, wa_ref, wb_ref, wg_ref, bg_ref, wout_ref,
                  gffn_ref, wrh_ref, wrl_ref, rb_ref, tri_ref, *rest):
    (x1_ref, h2p_ref, idx_ref, wt_ref, rk_ref, cnt_ref,
     kbuf, vt_buf, pbuf, zbt_buf, run_ref) = rest[-11:]
    s = pl.program_id(1)
    tm = x_ref.shape[0]

    @pl.when(s == 0)
    def _():
        kbuf[0:ATTN_BLOCK, :] = jnp.zeros((ATTN_BLOCK, kbuf.shape[1]), BF16)
        vt_buf[:, 0:ATTN_BLOCK] = jnp.zeros((vt_buf.shape[0], ATTN_BLOCK), BF16)
        pbuf[:, 0:POOL_TAIL, :] = jnp.zeros((pbuf.shape[0], POOL_TAIL, POOL_DIM), F32)

    x = x_ref[...]
    hb = _rms(x, gmix_ref[...]).astype(BF16)
    u = _dot(hb, win_ref[...])

    up = u[:, 0:POOL_DIM]
    t0 = POOL_TAIL
    pbuf[0, t0:t0 + tm, :] = up
    s2 = up + pbuf[0, t0 - 1:t0 - 1 + tm, :]
    pbuf[1, t0:t0 + tm, 128:512] = s2[:, 128:512]
    s4 = s2[:, 128:512] + pbuf[1, t0 - 2:t0 - 2 + tm, 128:512]
    pbuf[2, t0:t0 + tm, 256:512] = s4[:, 128:384]
    s8 = s4[:, 128:384] + pbuf[2, t0 - 4:t0 - 4 + tm, 256:512]
    pbuf[3, t0:t0 + tm, 384:512] = s8[:, 128:256]
    s16 = s8[:, 128:256] + pbuf[3, t0 - 8:t0 - 8 + tm, 384:512]
    for lvl in range(4):
        pbuf[lvl, 0:POOL_TAIL, 128 * lvl:512] = pbuf[lvl, tm:tm + POOL_TAIL, 128 * lvl:512]
    wsums = (s2[:, 0:128], s4[:, 0:128], s8[:, 0:128], s16)
    tpos = (s * tm).astype(F32) + lax.broadcasted_iota(I32, (tm, 1), 0).astype(F32)
    za_parts = []
    for g, w in enumerate(POOL_WINDOWS):
        inv_count = 1.0 / jnp.minimum(tpos + 1.0, float(w))
        pooled = wsums[g] * inv_count - up[:, 128 * g:128 * (g + 1)]
        mixed = _dot(pooled.astype(BF16), wmix_ref[g]) * pscale_ref[:, 128 * g:128 * (g + 1)]
        za_parts.append(mixed.astype(BF16))
    za = jnp.concatenate(za_parts, axis=1)

    ones2 = ones_ref[...]
    q = u[:, COL_Q:COL_K]
    q2 = (q * q).astype(BF16)
    qss = jnp.concatenate([_dot(q2[:, 256 * c:256 * (c + 1)], ones2) for c in range(N_HEADS // 2)], axis=1)
    qn = (q * lax.rsqrt(qss * (1.0 / HEAD_DIM) + EPS) * qg_ref[...]).astype(BF16)
    k = u[:, COL_K:COL_V]
    kss = _dot((k * k).astype(BF16), ones2)
    kbuf[ATTN_BLOCK:ATTN_BLOCK + tm, :] = (k * lax.rsqrt(kss * (1.0 / HEAD_DIM) + EPS) * kg_ref[...]).astype(BF16)
    vt_buf[:, ATTN_BLOCK:ATTN_BLOCK + tm] = u[:, COL_V:IN_PAD].T.astype(BF16)

    nq = Q_GROUP * ATTN_BLOCK
    key_j = lax.broadcasted_iota(I32, (ATTN_BLOCK, nq), 0)
    qry_i = lax.broadcasted_iota(I32, (ATTN_BLOCK, nq), 1) & (ATTN_BLOCK - 1)
    from_prev = key_j > qry_i
    first = jnp.where(s == 0, 1, 0)
    for n in range(tm // ATTN_BLOCK):
        r0 = ATTN_BLOCK * n
        for kv in range(N_KV_HEADS):
            c0 = HEAD_PAD * kv
            qs = jnp.concatenate(
                [qn[r0:r0 + ATTN_BLOCK, HEAD_PAD * (kv * Q_GROUP + g):HEAD_PAD * (kv * Q_GROUP + g + 1)]
                 for g in HEAD_ORDER], axis=0)
            kk = kbuf[r0:r0 + 2 * ATTN_BLOCK, c0:c0 + HEAD_PAD]
            st = lax.dot_general(kk, qs, _NT, preferred_element_type=F32)
            sc = jnp.where(from_prev, st[0:ATTN_BLOCK], st[ATTN_BLOCK:2 * ATTN_BLOCK])
            sc = sc + (bias_ref[first, kv] if n == 0 else bias_ref[0, kv])
            sink = sink_ref[kv]
            m = jnp.maximum(jnp.max(sc, axis=0, keepdims=True), sink)
            e = jnp.exp(sc - m)
            inv_den = 1.0 / (jnp.sum(e, axis=0, keepdims=True) + jnp.exp(sink - m))
            pt = jnp.concatenate([jnp.where(from_prev, e, 0.0), jnp.where(from_prev, 0.0, e)],
                                 axis=0).astype(BF16)
            v_even = vt_buf[c0:c0 + HEAD_PAD, r0:r0 + 2 * ATTN_BLOCK]
            v_odd = jnp.concatenate([v_even[HEAD_DIM:HEAD_PAD], v_even[0:HEAD_DIM]], axis=0)
            half = 2 * ATTN_BLOCK
            o = (_dot(v_even, pt[:, 0:half]) * inv_den[:, 0:half]
                 + _dot(v_odd, pt[:, half:2 * half]) * inv_den[:, half:2 * half])
            ch = kv * Q_GROUP * HEAD_DIM
            zbt_buf[ch:ch + HEAD_PAD, r0:r0 + ATTN_BLOCK] = o[:, 0:ATTN_BLOCK]
            zbt_buf[ch + HEAD_PAD:ch + 2 * HEAD_PAD, r0:r0 + ATTN_BLOCK] = o[:, ATTN_BLOCK:half]
    kbuf[0:ATTN_BLOCK, :] = kbuf[tm:tm + ATTN_BLOCK, :]
    vt_buf[:, 0:ATTN_BLOCK] = vt_buf[:, tm:tm + ATTN_BLOCK]

    y_a = _dot(za, wa_ref[...])
    y_b = _dot(zbt_buf[...].T.astype(BF16), wb_ref[...])
    g_a = _sigmoid(_dot(hb, wg_ref[:, 0:D_MODEL]) + bg_ref[:, 0:D_MODEL])
    merged = g_a * y_a
    g_b = _sigmoid(_dot(hb, wg_ref[:, D_MODEL:2 * D_MODEL]) + bg_ref[:, D_MODEL:2 * D_MODEL])
    merged = (merged + g_b * y_b).astype(BF16)
    x1 = x + _dot(merged, wout_ref[...])
    x1_ref[...] = x1

    is_first = (pl.program_id(0) == 0) & (s == 0)
    _route_tile(x1, is_first, gffn_ref, wrh_ref, wrl_ref, rb_ref, tri_ref,
                h2p_ref, idx_ref, wt_ref, rk_ref, cnt_ref, run_ref)


def _attn_tables(attn_sinks):
    slopes = 2.0 ** (-8.0 * (np.arange(N_HEADS) + 1) / N_HEADS)
    j = np.arange(ATTN_BLOCK)[:, None]
    i = np.arange(ATTN_BLOCK)[None, :]
    from_prev = j > i
    dist = np.where(from_prev, ATTN_BLOCK + i - j, i - j)
    bias = np.empty((2, N_KV_HEADS, ATTN_BLOCK, Q_GROUP * ATTN_BLOCK), np.float32)
    for first in range(2):
        ok = ~from_prev if first else np.ones_like(from_prev)
        for kv in range(N_KV_HEADS):
            for slot, g in enumerate(HEAD_ORDER):
                sl = np.float32(slopes[kv * Q_GROUP + g])
                val = -(sl * dist.astype(np.float32))
                bias[first, kv, :, slot * ATTN_BLOCK:(slot + 1) * ATTN_BLOCK] = np.where(ok, val, MASKED)
    sink = attn_sinks.astype(F32).reshape(N_KV_HEADS, Q_GROUP)[:, np.asarray(HEAD_ORDER)]
    sink = jnp.repeat(sink, ATTN_BLOCK, axis=1)
    return jnp.asarray(bias), sink.reshape(N_KV_HEADS, 1, Q_GROUP * ATTN_BLOCK)


def _pad_heads(w, n_heads):
    k = w.shape[0]
    w = w.reshape(k, n_heads, HEAD_DIM)
    w = jnp.pad(w, ((0, 0), (0, 0), (0, HEAD_PAD - HEAD_DIM)))
    return w.reshape(k, n_heads * HEAD_PAD)


def _const_spec(shape):
    nd = len(shape)
    return pl.BlockSpec(shape, lambda *_: (0,) * nd)


def _mixer(x2d, batch0, batch, seq, g_mix, w_in, w_pool_mix, pool_scale, w_branch_a, q_gain, k_gain,
           attn_sinks, w_branch_b, w_gate, b_gate, w_out, g_ffn, w_router, router_bias, after=None):
    t = batch * seq
    tm = TM_MIX
    ns = seq // tm
    w_q = w_in[:, POOL_DIM:POOL_DIM + Q_DIM]
    w_k = w_in[:, POOL_DIM + Q_DIM:POOL_DIM + Q_DIM + KV_DIM]
    w_v = w_in[:, POOL_DIM + Q_DIM + KV_DIM:]
    win_p = jnp.concatenate([w_in[:, :POOL_DIM], _pad_heads(w_q, N_HEADS), _pad_heads(w_k, N_KV_HEADS),
                             _pad_heads(w_v, N_KV_HEADS)], axis=1).astype(BF16)
    qg = jnp.tile(jnp.pad(q_gain * (HEAD_DIM ** -0.5), (0, HEAD_PAD - HEAD_DIM)), N_HEADS).reshape(1, -1)
    kg = jnp.tile(jnp.pad(k_gain, (0, HEAD_PAD - HEAD_DIM)), N_KV_HEADS).reshape(1, -1)
    bias, sink = _attn_tables(attn_sinks)
    ones2 = jnp.asarray(np.kron(np.eye(2), np.ones((HEAD_PAD, HEAD_PAD))), BF16)
    operands = (x2d, g_mix.reshape(1, D_MODEL), win_p, qg, kg, bias, sink, ones2, w_pool_mix.astype(BF16),
                pool_scale.reshape(1, POOL_DIM), w_branch_a.astype(BF16), w_branch_b.astype(BF16),
                w_gate.astype(BF16), b_gate.reshape(1, 2 * D_MODEL), w_out.astype(BF16))
    operands += _router_operands(g_ffn, w_router, router_bias, tm)
    in_specs = [pl.BlockSpec((tm, D_MODEL), lambda b, s: ((batch0 + b) * ns + s, 0))]
    in_specs += [_const_spec(op.shape) for op in operands[1:]]
    if after is not None:
        operands += (after,)
        in_specs.append(pl.BlockSpec(memory_space=pl.ANY))
    row8 = pl.BlockSpec((TOP_K, tm), lambda b, s: (0, b * ns + s))
    return pl.pallas_call(
        _mixer_kernel,
        grid=(batch, ns),
        in_specs=in_specs,
        out_specs=[pl.BlockSpec((tm, D_MODEL), lambda b, s: (b * ns + s, 0)),
                   pl.BlockSpec((tm, HALF), lambda b, s: (b * ns + s, 0)),
                   row8,
                   pl.BlockSpec((tm, TOP_K * SC_LANES), lambda b, s: (b * ns + s, 0)),
                   row8,
                   pl.BlockSpec((N_EXPERTS, 128), lambda b, s: (0, 0))],
        out_shape=[jax.ShapeDtypeStruct((t, D_MODEL), F32),
                   jax.ShapeDtypeStruct((t, HALF), I32),
                   jax.ShapeDtypeStruct((TOP_K, t), I32),
                   jax.ShapeDtypeStruct((t, TOP_K * SC_LANES), F32),
                   jax.ShapeDtypeStruct((TOP_K, t), I32),
                   jax.ShapeDtypeStruct((N_EXPERTS, 128), F32)],
        scratch_shapes=[
            pltpu.VMEM((ATTN_BLOCK + tm, N_KV_HEADS * HEAD_PAD), BF16),
            pltpu.VMEM((N_KV_HEADS * HEAD_PAD, ATTN_BLOCK + tm), BF16),
            pltpu.VMEM((4, POOL_TAIL + tm, POOL_DIM), F32),
            pltpu.VMEM((Q_DIM, tm), F32),
            pltpu.VMEM((N_EXPERTS, 128), F32),
        ],
        compiler_params=pltpu.CompilerParams(
            dimension_semantics=("arbitrary", "arbitrary"), vmem_limit_bytes=VMEM_LIMIT),
        name="mixer",
    )(*operands)


def _route_tile(x1, is_first, gffn_ref, wrh_ref, wrl_ref, rb_ref, tri_ref,
                h2p_ref, idx_ref, wt_ref, rk_ref, cnt_ref, run_ref):
    tm = x1.shape[0]

    @pl.when(is_first)
    def _():
        run_ref[...] = jnp.zeros_like(run_ref)

    h2 = _rms(x1, gffn_ref[...])
    hb = h2.astype(BF16)
    h2p_ref[...] = _pack(h2[:, 0:HALF], h2[:, HALF:D_MODEL])

    lo = (h2 - hb.astype(F32)).astype(BF16)
    wrh = wrh_ref[...]
    logits = (lax.dot_general(wrh, hb, _NT, preferred_element_type=F32)
              + lax.dot_general(wrl_ref[...], hb, _NT, preferred_element_type=F32)
              + lax.dot_general(wrh, lo, _NT, preferred_element_type=F32))
    scores = _sigmoid(logits)
    choice = scores + rb_ref[...]

    neg = -jnp.inf
    sub8 = lax.broadcasted_iota(I32, (GROUP_SIZE, tm), 0).astype(F32)
    grp_rows = []
    for gi in range(N_EXPERT_GROUPS):
        cg = choice[GROUP_SIZE * gi:GROUP_SIZE * (gi + 1)]
        m1 = jnp.max(cg, axis=0, keepdims=True)
        first = jnp.min(jnp.where(cg == m1, sub8, float(GROUP_SIZE)), axis=0, keepdims=True)
        m2 = jnp.max(jnp.where(sub8 == first, neg, cg), axis=0, keepdims=True)
        grp_rows.append(m1 + m2)
    gs = jnp.concatenate(grp_rows, axis=0)
    beaten = jnp.zeros((N_EXPERT_GROUPS, tm), F32)
    for gi in range(N_EXPERT_GROUPS):
        row = grp_rows[gi]
        wins = (row > gs) | ((row == gs) & (sub8 > float(gi)))
        beaten = beaten + jnp.where(wins, 1.0, 0.0)
    gsel = beaten < float(TOPK_GROUPS)
    cm = jnp.concatenate(
        [jnp.where(gsel[gi:gi + 1], choice[GROUP_SIZE * gi:GROUP_SIZE * (gi + 1)], neg)
         for gi in range(N_EXPERT_GROUPS)], axis=0)

    iota_e = lax.broadcasted_iota(I32, (N_EXPERTS, tm), 0).astype(F32)
    idx_rows, s_rows = [], []
    sel = jnp.zeros((N_EXPERTS, tm), F32)
    for _ in range(TOP_K):
        m = jnp.max(cm, axis=0, keepdims=True)
        idx = jnp.min(jnp.where(cm == m, iota_e, float(N_EXPERTS)), axis=0, keepdims=True)
        oh = iota_e == idx
        s_rows.append(jnp.sum(jnp.where(oh, scores, 0.0), axis=0, keepdims=True))
        idx_rows.append(idx)
        cm = jnp.where(oh, neg, cm)
        sel = sel + jnp.where(oh, 1.0, 0.0)

    run = run_ref[:, 0:1]
    cum = _dot(sel.astype(BF16), tri_ref[...])
    before = run + cum - sel
    rk_rows = [jnp.sum(jnp.where(iota_e == idx, before, 0.0), axis=0, keepdims=True) for idx in idx_rows]
    new_run = run + jnp.sum(sel, axis=1, keepdims=True)
    run_ref[...] = jnp.broadcast_to(new_run, run_ref.shape)
    cnt_ref[...] = jnp.broadcast_to(new_run, cnt_ref.shape)

    ssum = s_rows[0]
    for r in range(1, TOP_K):
        ssum = ssum + s_rows[r]
    denom = ssum + 1e-20
    idx_ref[...] = jnp.concatenate(idx_rows, axis=0).astype(I32)
    w_rep = jnp.concatenate([jnp.broadcast_to(sr / denom * ROUTED_SCALE, (SC_LANES, tm)) for sr in s_rows],
                            axis=0)
    wt_ref[...] = w_rep.T
    rk_ref[...] = jnp.concatenate(rk_rows, axis=0).astype(I32)


def _router_operands(g_ffn, w_router, router_bias, tm):
    wr_t = w_router.astype(F32).T
    wr_hi = wr_t.astype(BF16)
    wr_lo = (wr_t - wr_hi.astype(F32)).astype(BF16)
    tri = (np.arange(tm)[:, None] <= np.arange(tm)[None, :]).astype(np.float32)
    return (g_ffn.reshape(1, D_MODEL), wr_hi, wr_lo, router_bias.astype(F32).reshape(N_EXPERTS, 1),
            jnp.asarray(tri, BF16))


def _swiglu_packed(x_packed, wg, wu, wd):
    lo, hi = _unpack(x_packed)
    lo = lo.astype(BF16)
    hi = hi.astype(BF16)
    g = _dot(lo, wg[0:HALF, :]) + _dot(hi, wg[HALF:D_MODEL, :])
    u = _dot(lo, wu[0:HALF, :]) + _dot(hi, wu[HALF:D_MODEL, :])
    a = (g * _sigmoid(g) * u).astype(BF16)
    y = _dot(a, wd[...])
    return _pack(y[:, 0:HALF], y[:, HALF:D_MODEL])


def _shared_kernel(h2p_ref, wg_ref, wu_ref, wd_ref, *rest):
    ysh_ref = rest[-1]
    ysh_ref[...] = _swiglu_packed(h2p_ref[...], wg_ref, wu_ref, wd_ref)


def _shared(h2p, w_sh_gate, w_sh_up, w_sh_down, after=None):
    t = h2p.shape[0]
    tm = TM_FFN
    operands = (h2p, w_sh_gate.astype(BF16), w_sh_up.astype(BF16), w_sh_down.astype(BF16))
    in_specs = [pl.BlockSpec((tm, HALF), lambda i: (i, 0))]
    in_specs += [_const_spec(op.shape) for op in operands[1:]]
    if after is not None:
        operands += (after,)
        in_specs.append(pl.BlockSpec(memory_space=pl.ANY))
    return pl.pallas_call(
        _shared_kernel,
        grid=(t // tm,),
        in_specs=in_specs,
        out_specs=pl.BlockSpec((tm, HALF), lambda i: (i, 0)),
        out_shape=jax.ShapeDtypeStruct((t, HALF), I32),
        compiler_params=pltpu.CompilerParams(
            dimension_semantics=("arbitrary",), vmem_limit_bytes=VMEM_LIMIT),
        name="shared",
    )(*operands)


def _positions_kernel(pstart_ref, idx_ref, rk_ref, pos_ref):
    idx = idx_ref[...]
    acc = rk_ref[...]
    for e in range(N_EXPERTS):
        acc = acc + jnp.where(idx == e, pstart_ref[e], 0)
    pos_ref[...] = acc


def _positions(pstart, idx8, rk8):
    t = idx8.shape[1]
    tm = TM_POS
    spec = pl.BlockSpec((TOP_K, tm), lambda i, ps: (0, i))
    return pl.pallas_call(
        _positions_kernel,
        grid_spec=pltpu.PrefetchScalarGridSpec(num_scalar_prefetch=1, grid=(t // tm,),
                                               in_specs=[spec, spec], out_specs=spec),
        out_shape=jax.ShapeDtypeStruct((TOP_K, t), I32),
        name="positions",
    )(pstart, idx8, rk8)


def _sc_workers():
    info = plsc.get_sparse_core_info()
    return info.num_cores, info.num_cores * info.num_subcores


def _dispatch(pos_flat, h2p, n_rows):
    t = h2p.shape[0]
    ch = SC_SCATTER_ROWS
    n_cores, n_workers = _sc_workers()
    n_ch = t // (n_workers * ch)

    def body(pos_hbm, h_hbm, xs_hbm, idx_v, rows_v, sem):
        wid = lax.axis_index("s") * n_cores + lax.axis_index("c")

        @pl.loop(0, n_ch)
        def _(c):
            chunk = wid * n_ch + c
            pltpu.sync_copy(h_hbm.at[pl.ds(chunk * ch, ch)], rows_v)
            for k in range(TOP_K):
                pltpu.sync_copy(pos_hbm.at[pl.ds(k * t + chunk * ch, ch)], idx_v.at[k])
            copies = [pltpu.async_copy(rows_v, xs_hbm.at[idx_v.at[k]], sem) for k in range(TOP_K)]
            for cp in copies:
                cp.wait()

    return pl.kernel(
        body,
        out_type=jax.ShapeDtypeStruct((n_rows, HALF), I32),
        mesh=plsc.VectorSubcoreMesh(core_axis_name="c", subcore_axis_name="s"),
        scratch_types=[pltpu.VMEM((TOP_K, ch), I32), pltpu.VMEM((ch, HALF), I32),
                       pltpu.SemaphoreType.DMA],
        name="dispatch",
    )(pos_flat, h2p)


def _pairwise_sum(xs):
    while len(xs) > 1:
        xs = [xs[i] + xs[i + 1] for i in range(0, len(xs), 2)]
    return xs[0]


def _gather_sum(pos_flat, w_rep, ys, tok0, t):
    t_all = w_rep.shape[0]
    ch = SC_SUM_ROWS
    lanes = SC_LANES
    n_cores, n_workers = _sc_workers()
    per_w = t // n_workers
    n_ch = per_w // ch

    def body(pos_hbm, w_hbm, ys_hbm, out_hbm, idx_all, w_all, buf0, buf1, acc0, acc1, gsem0, gsem1, osem0, osem1):
        wid = lax.axis_index("s") * n_cores + lax.axis_index("c")
        base = wid * n_ch
        first = tok0 + wid * per_w
        bufs, accs, gsems, osems = (buf0, buf1), (acc0, acc1), (gsem0, gsem1), (osem0, osem1)
        for k in range(TOP_K):
            pltpu.sync_copy(pos_hbm.at[pl.ds(k * t_all + first, per_w)], idx_all.at[pl.ds(k * per_w, per_w)])
        pltpu.sync_copy(w_hbm.at[pl.ds(first, per_w)], w_all)

        def row_copy(c, s, k):
            rows = idx_all.at[pl.ds(k * per_w + c * ch, ch)]
            return pltpu.make_async_copy(ys_hbm.at[rows], bufs[s].at[k], gsems[s])

        def out_copy(c, s):
            return pltpu.make_async_copy(accs[s], out_hbm.at[pl.ds((base + c) * ch, ch)], osems[s])

        def fetch(c, s):
            for k in range(TOP_K):
                row_copy(c, s, k).start()

        def reduce_rows(c, s):
            @pl.loop(0, ch)
            def _(r):
                wks = [w_all[c * ch + r, pl.ds(k * lanes, lanes)] for k in range(TOP_K)]

                @pl.loop(0, HALF // lanes, step=2)
                def _(v):
                    for dv in range(2):
                        col = (v + dv) * lanes
                        los, his = [], []
                        for k in range(TOP_K):
                            words = bufs[s][k, r, pl.ds(col, lanes)]
                            los.append(wks[k] * lax.bitcast_convert_type(lax.shift_left(words, jnp.int32(16)), F32))
                            his.append(wks[k] * lax.bitcast_convert_type(words & jnp.int32(-65536), F32))
                        accs[s][r, pl.ds(col, lanes)] = _pairwise_sum(los)
                        accs[s][r, pl.ds(HALF + col, lanes)] = _pairwise_sum(his)

        fetch(0, 0)

        @pl.loop(0, n_ch, step=2)
        def _(c):
            for s in range(2):
                cc = c + s

                @pl.when(cc + 1 < n_ch)
                def _():
                    fetch(cc + 1, 1 - s)

                for k in range(TOP_K):
                    row_copy(cc, s, k).wait()

                @pl.when(cc >= 2)
                def _():
                    out_copy(cc - 2, s).wait()

                reduce_rows(cc, s)
                out_copy(cc, s).start()

        out_copy(n_ch - 2, 0).wait()
        out_copy(n_ch - 1, 1).wait()

    return pl.kernel(
        body,
        out_type=jax.ShapeDtypeStruct((t, D_MODEL), F32),
        mesh=plsc.VectorSubcoreMesh(core_axis_name="c", subcore_axis_name="s"),
        scratch_types=[pltpu.VMEM((TOP_K * per_w,), I32), pltpu.VMEM((per_w, TOP_K * lanes), F32),
                       pltpu.VMEM((TOP_K, ch, HALF), I32), pltpu.VMEM((TOP_K, ch, HALF), I32),
                       pltpu.VMEM((ch, D_MODEL), F32), pltpu.VMEM((ch, D_MODEL), F32),
                       pltpu.SemaphoreType.DMA, pltpu.SemaphoreType.DMA,
                       pltpu.SemaphoreType.DMA, pltpu.SemaphoreType.DMA],
        compiler_params=pltpu.CompilerParams(needs_layout_passes=False),
        name="gather_sum",
    )(pos_flat, w_rep, ys)


def _expert_kernel(nblk_ref, blk0_ref, cnt_ref, wg_ref, wu_ref, wd_ref, xs_hbm, after_ref, ys_hbm,
                   wg_s, wu_s, wd_s, xbuf, ybuf, in_sem, out_sem):
    del after_ref
    e = pl.program_id(0)
    nb = nblk_ref[e]
    b0 = blk0_ref[e]
    cnt = cnt_ref[e]
    n_all = blk0_ref[N_EXPERTS]
    ahead = EXPERT_SLOTS - 2
    wg_s[...] = wg_ref[0].astype(BF16)
    wu_s[...] = wu_ref[0].astype(BF16)
    wd_s[...] = wd_ref[0].astype(BF16)

    def slot_rows(g, n_blocks=1):
        start = (g & (EXPERT_SLOTS - 1)) * MOE_BLOCK
        return pl.ds(pl.multiple_of(start, MOE_BLOCK), n_blocks * MOE_BLOCK)

    def rows_of(g):
        return pl.ds(pl.multiple_of(g * MOE_BLOCK, MOE_BLOCK), MOE_BLOCK)

    def in_copy(g):
        return pltpu.make_async_copy(xs_hbm.at[rows_of(g)], xbuf.at[slot_rows(g)],
                                     in_sem.at[g & (EXPERT_SLOTS - 1)])

    def out_copy(g):
        return pltpu.make_async_copy(ybuf.at[slot_rows(g)], ys_hbm.at[rows_of(g)],
                                     out_sem.at[g & (EXPERT_SLOTS - 1)])

    @pl.when(e == 0)
    def _():
        for g in range(ahead):
            @pl.when(g < n_all)
            def _():
                in_copy(g).start()

    def arrive(g):
        in_copy(g).wait()

        @pl.when(g + ahead < n_all)
        def _():
            in_copy(g + ahead).start()

        @pl.when(g >= EXPERT_SLOTS)
        def _():
            out_copy(g - EXPERT_SLOTS).wait()

    def run_blocks(j, n_blocks):
        g = b0 + j
        for i in range(n_blocks):
            arrive(g + i)
        rows = lax.broadcasted_iota(I32, (n_blocks * MOE_BLOCK, HALF), 0)
        x = jnp.where(rows < cnt - j * MOE_BLOCK, xbuf[slot_rows(g, n_blocks)], 0)
        ybuf[slot_rows(g, n_blocks)] = _swiglu_packed(x, wg_s, wu_s, wd_s)
        for i in range(n_blocks):
            out_copy(g + i).start()

    lead = jnp.where(nb > 0, b0 & 1, 0)

    @pl.when(lead == 1)
    def _():
        run_blocks(0, 1)

    n_pairs = (nb - lead) // 2

    def one_pair(p, carry):
        run_blocks(lead + 2 * p, 2)
        return carry

    lax.fori_loop(0, n_pairs, one_pair, 0)

    @pl.when(nb - lead - 2 * n_pairs == 1)
    def _():
        run_blocks(nb - 1, 1)

    @pl.when(e == N_EXPERTS - 1)
    def _():
        for back in range(1, EXPERT_SLOTS + 1):
            @pl.when(n_all >= back)
            def _():
                out_copy(n_all - back).wait()


def _experts(n_blk, blk0, counts, n_rows, xs, w_gate, w_up, w_down, after):
    wspec_in = pl.BlockSpec((1, D_MODEL, EXPERT_DIM), lambda e, *_: (e, 0, 0))
    grid_spec = pltpu.PrefetchScalarGridSpec(
        num_scalar_prefetch=3,
        grid=(N_EXPERTS,),
        in_specs=[wspec_in, wspec_in,
                  pl.BlockSpec((1, EXPERT_DIM, D_MODEL), lambda e, *_: (e, 0, 0)),
                  pl.BlockSpec(memory_space=pl.ANY),
                  pl.BlockSpec(memory_space=pl.ANY)],
        out_specs=pl.BlockSpec(memory_space=pl.ANY),
        scratch_shapes=[pltpu.VMEM((D_MODEL, EXPERT_DIM), BF16), pltpu.VMEM((D_MODEL, EXPERT_DIM), BF16),
                        pltpu.VMEM((EXPERT_DIM, D_MODEL), BF16),
                        pltpu.VMEM((EXPERT_SLOTS * MOE_BLOCK, HALF), I32),
                        pltpu.VMEM((EXPERT_SLOTS * MOE_BLOCK, HALF), I32),
                        pltpu.SemaphoreType.DMA((EXPERT_SLOTS,)), pltpu.SemaphoreType.DMA((EXPERT_SLOTS,))],
    )
    return pl.pallas_call(
        _expert_kernel,
        grid_spec=grid_spec,
        out_shape=jax.ShapeDtypeStruct((n_rows, HALF), I32),
        compiler_params=pltpu.CompilerParams(
            dimension_semantics=("arbitrary",), vmem_limit_bytes=VMEM_LIMIT),
        name="experts",
    )(n_blk, blk0, counts, w_gate, w_up, w_down, xs, after)


def _combine_kernel(x1_ref, ysh_ref, moe_ref, p_ref, gple_ref, wpg_ref, wpp_ref, gpost_ref, *rest):
    out_ref = rest[-1]
    proj = _rms(_dot(p_ref[...].astype(BF16), wpp_ref[...]), gpost_ref[...])
    sh_lo, sh_hi = _unpack(ysh_ref[...])
    x2 = x1_ref[...] + moe_ref[...] + jnp.concatenate([sh_lo, sh_hi], axis=1)
    gate = _sigmoid(_dot(_rms(x2, gple_ref[...]).astype(BF16), wpg_ref[...]))
    out_ref[...] = x2 + gate * proj


def _combine(x1, ysh, moe, p2d, g_ple, w_ple_gate, w_ple_proj, g_ple_post, tok0, tok0_all, prev_out):
    t = p2d.shape[0]
    tm = TM_COMBINE
    steps = moe.shape[0] // tm
    off = tok0 // tm
    off_all = tok0_all // tm
    operands = [x1, ysh, moe, p2d, g_ple.reshape(1, D_MODEL), w_ple_gate.astype(BF16),
                w_ple_proj.astype(BF16), g_ple_post.reshape(1, D_MODEL)]
    in_specs = [pl.BlockSpec((tm, D_MODEL), lambda i: (off + i, 0)),
                pl.BlockSpec((tm, HALF), lambda i: (off + i, 0)),
                pl.BlockSpec((tm, D_MODEL), lambda i: (i, 0)),
                pl.BlockSpec((tm, PLE_DIM), lambda i: (off_all + i, 0)),
                _const_spec((1, D_MODEL)),
                _const_spec((D_MODEL, D_MODEL)),
                _const_spec((PLE_DIM, D_MODEL)),
                _const_spec((1, D_MODEL))]
    aliases = {}
    if prev_out is not None:
        aliases = {len(operands): 0}
        operands.append(prev_out)
        in_specs.append(pl.BlockSpec(memory_space=pl.ANY))
    return pl.pallas_call(
        _combine_kernel,
        grid=(steps,),
        in_specs=in_specs,
        out_specs=pl.BlockSpec((tm, D_MODEL), lambda i: (off_all + i, 0)),
        out_shape=jax.ShapeDtypeStruct((t, D_MODEL), F32),
        input_output_aliases=aliases,
        compiler_params=pltpu.CompilerParams(
            dimension_semantics=("arbitrary",), vmem_limit_bytes=VMEM_LIMIT),
        name="combine",
    )(*operands)


def _layer(x2d, p2d, batch, seq, g_mix, w_in, w_pool_mix, pool_scale, w_branch_a, q_gain, k_gain,
           attn_sinks, w_branch_b, w_gate, b_gate, w_out, g_ffn, w_router, router_bias, w_exp_gate,
           w_exp_up, w_exp_down, w_sh_gate, w_sh_up, w_sh_down, g_ple, w_ple_gate, w_ple_proj,
           g_ple_post):
    bh = batch // PIPELINE_HALVES
    th = bh * seq
    n_rows = (-(-(th * TOP_K) // MOE_BLOCK) + N_EXPERTS) * MOE_BLOCK

    halves = []
    after = None
    for h in range(PIPELINE_HALVES):
        x1, h2p, idx8, w_rep, rk8, cnt = _mixer(
            x2d, h * bh, bh, seq, g_mix, w_in, w_pool_mix, pool_scale, w_branch_a, q_gain, k_gain,
            attn_sinks, w_branch_b, w_gate, b_gate, w_out, g_ffn, w_router, router_bias, after=after)
        counts = cnt[:, 0].astype(I32)
        padded = (counts + MOE_BLOCK - 1) // MOE_BLOCK * MOE_BLOCK
        pend = jnp.cumsum(padded)
        pstart = pend - padded
        pos = _positions(pstart.astype(I32), idx8, rk8).reshape(-1)
        xs = _dispatch(pos, h2p, n_rows)
        blk0 = jnp.concatenate([pstart, pend[-1:]]) // MOE_BLOCK
        halves.append((x1, h2p, w_rep, pos, xs, (padded // MOE_BLOCK).astype(I32), blk0.astype(I32), counts))
        after = pos

    out = None
    tc = th // COMBINE_CHUNKS
    for h, (x1, h2p, w_rep, pos, xs, n_blk, blk0, counts) in enumerate(halves):
        if h < PIPELINE_HALVES - 1:
            ysh = _shared(h2p, w_sh_gate, w_sh_up, w_sh_down)
            ys = _experts(n_blk, blk0, counts, n_rows, xs, w_exp_gate, w_exp_up, w_exp_down, ysh)
        else:
            ys = _experts(n_blk, blk0, counts, n_rows, xs, w_exp_gate, w_exp_up, w_exp_down, xs)
            ysh = _shared(h2p, w_sh_gate, w_sh_up, w_sh_down, after=out)
        for c in range(COMBINE_CHUNKS):
            moe = _gather_sum(pos, w_rep, ys, c * tc, tc)
            out = _combine(x1, ysh, moe, p2d, g_ple, w_ple_gate, w_ple_proj, g_ple_post,
                           c * tc, h * th + c * tc, out)
    return out


def kernel(x, p, g_mix, w_in, w_pool_mix, pool_scale, w_branch_a, q_gain, k_gain, attn_sinks,
           w_branch_b, w_gate, b_gate, w_out, g_ffn, w_router, router_bias, w_exp_gate, w_exp_up,
           w_exp_down, w_sh_gate, w_sh_up, w_sh_down, g_ple, w_ple_gate, w_ple_proj, g_ple_post):
    batch, seq, d = x.shape
    depth = p.shape[0]
    x2d = x.reshape(batch * seq, d)
    for i in range(depth):
        x2d = _layer(x2d, p[i].reshape(batch * seq, PLE_DIM), batch, seq, g_mix[i], w_in[i],
                     w_pool_mix[i], pool_scale[i], w_branch_a[i], q_gain[i], k_gain[i], attn_sinks[i],
                     w_branch_b[i], w_gate[i], b_gate[i], w_out[i], g_ffn[i], w_router[i],
                     router_bias[i], w_exp_gate[i], w_exp_up[i], w_exp_down[i], w_sh_gate[i],
                     w_sh_up[i], w_sh_down[i], g_ple[i], w_ple_gate[i], w_ple_proj[i], g_ple_post[i])
    return x2d.reshape(batch, seq, d)
```

```python
import functools

import numpy as np
import jax
import jax.numpy as jnp
from jax import lax
from jax.experimental import pallas as pl
from jax.experimental.pallas import tpu as pltpu
from jax.experimental.pallas import tpu_sc as plsc

F32 = jnp.float32
BF16 = jnp.bfloat16
I32 = jnp.int32

D_MODEL = 1024
PLE_DIM = 256
POOL_WINDOWS = (2, 4, 8, 16)
POOL_GROUP_DIM = 128
POOL_DIM = 512
N_HEADS = 8
N_KV_HEADS = 2
Q_GROUP = 4
HEAD_DIM = 64
HEAD_PAD = 128
Q_DIM = 512
KV_DIM = 128
ATTN_BLOCK = 128
N_EXPERTS = 64
N_EXPERT_GROUPS = 8
GROUP_SIZE = 8
TOPK_GROUPS = 4
TOP_K = 8
EXPERT_DIM = 256
ROUTED_SCALE = 2.5
MOE_BLOCK = 512
EXPERT_SLOTS = 8
EPS = 1e-6
HALF = D_MODEL // 2
MASKED = -1e30

COL_Q = POOL_DIM
COL_K = COL_Q + N_HEADS * HEAD_PAD
COL_V = COL_K + N_KV_HEADS * HEAD_PAD
IN_PAD = COL_V + N_KV_HEADS * HEAD_PAD
HEAD_ORDER = (0, 2, 1, 3)

TM_MIX = 512
TM_FFN = 512
TM_COMBINE = 512
TM_POS = 4096
COMBINE_CHUNKS = 2
PIPELINE_HALVES = 2
SC_SCATTER_ROWS = 128
SC_SUM_ROWS = 8
SC_LANES = 16
POOL_TAIL = 8
VMEM_LIMIT = 56 * 1024 * 1024

_NT = (((1,), (1,)), ((), ()))


def _dot(a, b):
    return jnp.dot(a, b, preferred_element_type=F32)


def _rms(x, g):
    ms = jnp.mean(x * x, axis=-1, keepdims=True)
    return x * lax.rsqrt(ms + EPS) * g


def _sigmoid(x):
    return 0.5 * jnp.tanh(0.5 * x) + 0.5


def _pack(lo, hi):
    lo_bits = lax.bitcast_convert_type(lo.astype(BF16).astype(F32), I32)
    hi_bits = lax.bitcast_convert_type(hi.astype(BF16).astype(F32), I32)
    return (hi_bits & jnp.int32(-65536)) | lax.shift_right_logical(lo_bits, jnp.int32(16))


def _unpack(w):
    lo = lax.bitcast_convert_type(lax.shift_left(w, jnp.int32(16)), F32)
    hi = lax.bitcast_convert_type(w & jnp.int32(-65536), F32)
    return lo, hi


def _mixer_kernel(x_ref, gmix_ref, win_ref, qg_ref, kg_ref, bias_ref, sink_ref, ones_ref, wmix_ref,
                  pscale_ref, wa_ref, wb_ref, wg_ref, bg_ref, wout_ref,
                  gffn_ref, wrh_ref, wrl_ref, rb_ref, tri_ref, *rest):
    (x1_ref, h2p_ref, idx_ref, wt_ref, rk_ref, cnt_ref,
     kbuf, vt_buf, pbuf, zbt_buf, run_ref) = rest[-11:]
    s = pl.program_id(1)
    tm = x_ref.shape[0]

    @pl.when(s == 0)
    def _():
        kbuf[0:ATTN_BLOCK, :] = jnp.zeros((ATTN_BLOCK, kbuf.shape[1]), BF16)
        vt_buf[:, 0:ATTN_BLOCK] = jnp.zeros((vt_buf.shape[0], ATTN_BLOCK), BF16)
        pbuf[:, 0:POOL_TAIL, :] = jnp.zeros((pbuf.shape[0], POOL_TAIL, POOL_DIM), F32)

    x = x_ref[...]
    hb = _rms(x, gmix_ref[...]).astype(BF16)
    u = _dot(hb, win_ref[...])

    up = u[:, 0:POOL_DIM]
    t0 = POOL_TAIL
    pbuf[0, t0:t0 + tm, :] = up
    s2 = up + pbuf[0, t0 - 1:t0 - 1 + tm, :]
    pbuf[1, t0:t0 + tm, 128:512] = s2[:, 128:512]
    s4 = s2[:, 128:512] + pbuf[1, t0 - 2:t0 - 2 + tm, 128:512]
    pbuf[2, t0:t0 + tm, 256:512] = s4[:, 128:384]
    s8 = s4[:, 128:384] + pbuf[2, t0 - 4:t0 - 4 + tm, 256:512]
    pbuf[3, t0:t0 + tm, 384:512] = s8[:, 128:256]
    s16 = s8[:, 128:256] + pbuf[3, t0 - 8:t0 - 8 + tm, 384:512]
    for lvl in range(4):
        pbuf[lvl, 0:POOL_TAIL, 128 * lvl:512] = pbuf[lvl, tm:tm + POOL_TAIL, 128 * lvl:512]
    wsums = (s2[:, 0:128], s4[:, 0:128], s8[:, 0:128], s16)
    tpos = (s * tm).astype(F32) + lax.broadcasted_iota(I32, (tm, 1), 0).astype(F32)
    za_parts = []
    for g, w in enumerate(POOL_WINDOWS):
        inv_count = 1.0 / jnp.minimum(tpos + 1.0, float(w))
        pooled = wsums[g] * inv_count - up[:, 128 * g:128 * (g + 1)]
        mixed = _dot(pooled.astype(BF16), wmix_ref[g]) * pscale_ref[:, 128 * g:128 * (g + 1)]
        za_parts.append(mixed.astype(BF16))
    za = jnp.concatenate(za_parts, axis=1)

    ones2 = ones_ref[...]
    q = u[:, COL_Q:COL_K]
    q2 = (q * q).astype(BF16)
    qss = jnp.concatenate([_dot(q2[:, 256 * c:256 * (c + 1)], ones2) for c in range(N_HEADS // 2)], axis=1)
    qn = (q * lax.rsqrt(qss * (1.0 / HEAD_DIM) + EPS) * qg_ref[...]).astype(BF16)
    k = u[:, COL_K:COL_V]
    kss = _dot((k * k).astype(BF16), ones2)
    kbuf[ATTN_BLOCK:ATTN_BLOCK + tm, :] = (k * lax.rsqrt(kss * (1.0 / HEAD_DIM) + EPS) * kg_ref[...]).astype(BF16)
    vt_buf[:, ATTN_BLOCK:ATTN_BLOCK + tm] = u[:, COL_V:IN_PAD].T.astype(BF16)

    nq = Q_GROUP * ATTN_BLOCK
    key_j = lax.broadcasted_iota(I32, (ATTN_BLOCK, nq), 0)
    qry_i = lax.broadcasted_iota(I32, (ATTN_BLOCK, nq), 1) & (ATTN_BLOCK - 1)
    from_prev = key_j > qry_i
    first = jnp.where(s == 0, 1, 0)
    for n in range(tm // ATTN_BLOCK):
        r0 = ATTN_BLOCK * n
        for kv in range(N_KV_HEADS):
            c0 = HEAD_PAD * kv
            qs = jnp.concatenate(
                [qn[r0:r0 + ATTN_BLOCK, HEAD_PAD * (kv * Q_GROUP + g):HEAD_PAD * (kv * Q_GROUP + g + 1)]
                 for g in HEAD_ORDER], axis=0)
            kk = kbuf[r0:r0 + 2 * ATTN_BLOCK, c0:c0 + HEAD_PAD]
            st = lax.dot_general(kk, qs, _NT, preferred_element_type=F32)
            sc = jnp.where(from_prev, st[0:ATTN_BLOCK], st[ATTN_BLOCK:2 * ATTN_BLOCK])
            sc = sc + (bias_ref[first, kv] if n == 0 else bias_ref[0, kv])
            sink = sink_ref[kv]
            m = jnp.maximum(jnp.max(sc, axis=0, keepdims=True), sink)
            e = jnp.exp(sc - m)
            inv_den = 1.0 / (jnp.sum(e, axis=0, keepdims=True) + jnp.exp(sink - m))
            pt = jnp.concatenate([jnp.where(from_prev, e, 0.0), jnp.where(from_prev, 0.0, e)],
                                 axis=0).astype(BF16)
            v_even = vt_buf[c0:c0 + HEAD_PAD, r0:r0 + 2 * ATTN_BLOCK]
            v_odd = jnp.concatenate([v_even[HEAD_DIM:HEAD_PAD], v_even[0:HEAD_DIM]], axis=0)
            half = 2 * ATTN_BLOCK
            o = (_dot(v_even, pt[:, 0:half]) * inv_den[:, 0:half]
                 + _dot(v_odd, pt[:, half:2 * half]) * inv_den[:, half:2 * half])
            ch = kv * Q_GROUP * HEAD_DIM
            zbt_buf[ch:ch + HEAD_PAD, r0:r0 + ATTN_BLOCK] = o[:, 0:ATTN_BLOCK]
            zbt_buf[ch + HEAD_PAD:ch + 2 * HEAD_PAD, r0:r0 + ATTN_BLOCK] = o[:, ATTN_BLOCK:half]
    kbuf[0:ATTN_BLOCK, :] = kbuf[tm:tm + ATTN_BLOCK, :]
    vt_buf[:, 0:ATTN_BLOCK] = vt_buf[:, tm:tm + ATTN_BLOCK]

    y_a = _dot(za, wa_ref[...])
    y_b = _dot(zbt_buf[...].T.astype(BF16), wb_ref[...])
    g_a = _sigmoid(_dot(hb, wg_ref[:, 0:D_MODEL]) + bg_ref[:, 0:D_MODEL])
    merged = g_a * y_a
    g_b = _sigmoid(_dot(hb, wg_ref[:, D_MODEL:2 * D_MODEL]) + bg_ref[:, D_MODEL:2 * D_MODEL])
    merged = (merged + g_b * y_b).astype(BF16)
    x1 = x + _dot(merged, wout_ref[...])
    x1_ref[...] = x1

    is_first = (pl.program_id(0) == 0) & (s == 0)
    _route_tile(x1, is_first, gffn_ref, wrh_ref, wrl_ref, rb_ref, tri_ref,
                h2p_ref, idx_ref, wt_ref, rk_ref, cnt_ref, run_ref)


def _attn_tables(attn_sinks):
    slopes = 2.0 ** (-8.0 * (np.arange(N_HEADS) + 1) / N_HEADS)
    j = np.arange(ATTN_BLOCK)[:, None]
    i = np.arange(ATTN_BLOCK)[None, :]
    from_prev = j > i
    dist = np.where(from_prev, ATTN_BLOCK + i - j, i - j)
    bias = np.empty((2, N_KV_HEADS, ATTN_BLOCK, Q_GROUP * ATTN_BLOCK), np.float32)
    for first in range(2):
        ok = ~from_prev if first else np.ones_like(from_prev)
        for kv in range(N_KV_HEADS):
            for slot, g in enumerate(HEAD_ORDER):
                sl = np.float32(slopes[kv * Q_GROUP + g])
                val = -(sl * dist.astype(np.float32))
                bias[first, kv, :, slot * ATTN_BLOCK:(slot + 1) * ATTN_BLOCK] = np.where(ok, val, MASKED)
    sink = attn_sinks.astype(F32).reshape(N_KV_HEADS, Q_GROUP)[:, np.asarray(HEAD_ORDER)]
    sink = jnp.repeat(sink, ATTN_BLOCK, axis=1)
    return jnp.asarray(bias), sink.reshape(N_KV_HEADS, 1, Q_GROUP * ATTN_BLOCK)


def _pad_heads(w, n_heads):
    k = w.shape[0]
    w = w.reshape(k, n_heads, HEAD_DIM)
    w = jnp.pad(w, ((0, 0), (0, 0), (0, HEAD_PAD - HEAD_DIM)))
    return w.reshape(k, n_heads * HEAD_PAD)


def _const_spec(shape):
    nd = len(shape)
    return pl.BlockSpec(shape, lambda *_: (0,) * nd)


def _mixer(x2d, batch0, batch, seq, g_mix, w_in, w_pool_mix, pool_scale, w_branch_a, q_gain, k_gain,
           attn_sinks, w_branch_b, w_gate, b_gate, w_out, g_ffn, w_router, router_bias, after=None):
    t = batch * seq
    tm = TM_MIX
    ns = seq // tm
    w_q = w_in[:, POOL_DIM:POOL_DIM + Q_DIM]
    w_k = w_in[:, POOL_DIM + Q_DIM:POOL_DIM + Q_DIM + KV_DIM]
    w_v = w_in[:, POOL_DIM + Q_DIM + KV_DIM:]
    win_p = jnp.concatenate([w_in[:, :POOL_DIM], _pad_heads(w_q, N_HEADS), _pad_heads(w_k, N_KV_HEADS),
                             _pad_heads(w_v, N_KV_HEADS)], axis=1).astype(BF16)
    qg = jnp.tile(jnp.pad(q_gain * (HEAD_DIM ** -0.5), (0, HEAD_PAD - HEAD_DIM)), N_HEADS).reshape(1, -1)
    kg = jnp.tile(jnp.pad(k_gain, (0, HEAD_PAD - HEAD_DIM)), N_KV_HEADS).reshape(1, -1)
    bias, sink = _attn_tables(attn_sinks)
    ones2 = jnp.asarray(np.kron(np.eye(2), np.ones((HEAD_PAD, HEAD_PAD))), BF16)
    operands = (x2d, g_mix.reshape(1, D_MODEL), win_p, qg, kg, bias, sink, ones2, w_pool_mix.astype(BF16),
                pool_scale.reshape(1, POOL_DIM), w_branch_a.astype(BF16), w_branch_b.astype(BF16),
                w_gate.astype(BF16), b_gate.reshape(1, 2 * D_MODEL), w_out.astype(BF16))
    operands += _router_operands(g_ffn, w_router, router_bias, tm)
    in_specs = [pl.BlockSpec((tm, D_MODEL), lambda b, s: ((batch0 + b) * ns + s, 0))]
    in_specs += [_const_spec(op.shape) for op in operands[1:]]
    if after is not None:
        operands += (after,)
        in_specs.append(pl.BlockSpec(memory_space=pl.ANY))
    row8 = pl.BlockSpec((TOP_K, tm), lambda b, s: (0, b * ns + s))
    return pl.pallas_call(
        _mixer_kernel,
        grid=(batch, ns),
        in_specs=in_specs,
        out_specs=[pl.BlockSpec((tm, D_MODEL), lambda b, s: (b * ns + s, 0)),
                   pl.BlockSpec((tm, HALF), lambda b, s: (b * ns + s, 0)),
                   row8,
                   pl.BlockSpec((tm, TOP_K * SC_LANES), lambda b, s: (b * ns + s, 0)),
                   row8,
                   pl.BlockSpec((N_EXPERTS, 128), lambda b, s: (0, 0))],
        out_shape=[jax.ShapeDtypeStruct((t, D_MODEL), F32),
                   jax.ShapeDtypeStruct((t, HALF), I32),
                   jax.ShapeDtypeStruct((TOP_K, t), I32),
                   jax.ShapeDtypeStruct((t, TOP_K * SC_LANES), F32),
                   jax.ShapeDtypeStruct((TOP_K, t), I32),
                   jax.ShapeDtypeStruct((N_EXPERTS, 128), F32)],
        scratch_shapes=[
            pltpu.VMEM((ATTN_BLOCK + tm, N_KV_HEADS * HEAD_PAD), BF16),
            pltpu.VMEM((N_KV_HEADS * HEAD_PAD, ATTN_BLOCK + tm), BF16),
            pltpu.VMEM((4, POOL_TAIL + tm, POOL_DIM), F32),
            pltpu.VMEM((Q_DIM, tm), F32),
            pltpu.VMEM((N_EXPERTS, 128), F32),
        ],
        compiler_params=pltpu.CompilerParams(
            dimension_semantics=("arbitrary", "arbitrary"), vmem_limit_bytes=VMEM_LIMIT),
        name="mixer",
    )(*operands)


def _route_tile(x1, is_first, gffn_ref, wrh_ref, wrl_ref, rb_ref, tri_ref,
                h2p_ref, idx_ref, wt_ref, rk_ref, cnt_ref, run_ref):
    tm = x1.shape[0]

    @pl.when(is_first)
    def _():
        run_ref[...] = jnp.zeros_like(run_ref)

    h2 = _rms(x1, gffn_ref[...])
    hb = h2.astype(BF16)
    h2p_ref[...] = _pack(h2[:, 0:HALF], h2[:, HALF:D_MODEL])

    lo = (h2 - hb.astype(F32)).astype(BF16)
    wrh = wrh_ref[...]
    logits = (lax.dot_general(wrh, hb, _NT, preferred_element_type=F32)
              + lax.dot_general(wrl_ref[...], hb, _NT, preferred_element_type=F32)
              + lax.dot_general(wrh, lo, _NT, preferred_element_type=F32))
    scores = _sigmoid(logits)
    choice = scores + rb_ref[...]

    neg = -jnp.inf
    sub8 = lax.broadcasted_iota(I32, (GROUP_SIZE, tm), 0).astype(F32)
    grp_rows = []
    for gi in range(N_EXPERT_GROUPS):
        cg = choice[GROUP_SIZE * gi:GROUP_SIZE * (gi + 1)]
        m1 = jnp.max(cg, axis=0, keepdims=True)
        first = jnp.min(jnp.where(cg == m1, sub8, float(GROUP_SIZE)), axis=0, keepdims=True)
        m2 = jnp.max(jnp.where(sub8 == first, neg, cg), axis=0, keepdims=True)
        grp_rows.append(m1 + m2)
    gs = jnp.concatenate(grp_rows, axis=0)
    beaten = jnp.zeros((N_EXPERT_GROUPS, tm), F32)
    for gi in range(N_EXPERT_GROUPS):
        row = grp_rows[gi]
        wins = (row > gs) | ((row == gs) & (sub8 > float(gi)))
        beaten = beaten + jnp.where(wins, 1.0, 0.0)
    gsel = beaten < float(TOPK_GROUPS)
    cm = jnp.concatenate(
        [jnp.where(gsel[gi:gi + 1], choice[GROUP_SIZE * gi:GROUP_SIZE * (gi + 1)], neg)
         for gi in range(N_EXPERT_GROUPS)], axis=0)

    iota_e = lax.broadcasted_iota(I32, (N_EXPERTS, tm), 0).astype(F32)
    idx_rows, s_rows = [], []
    sel = jnp.zeros((N_EXPERTS, tm), F32)
    for _ in range(TOP_K):
        m = jnp.max(cm, axis=0, keepdims=True)
        idx = jnp.min(jnp.where(cm == m, iota_e, float(N_EXPERTS)), axis=0, keepdims=True)
        oh = iota_e == idx
        s_rows.append(jnp.sum(jnp.where(oh, scores, 0.0), axis=0, keepdims=True))
        idx_rows.append(idx)
        cm = jnp.where(oh, neg, cm)
        sel = sel + jnp.where(oh, 1.0, 0.0)

    run = run_ref[:, 0:1]
    cum = _dot(sel.astype(BF16), tri_ref[...])
    before = run + cum - sel
    rk_rows = [jnp.sum(jnp.where(iota_e == idx, before, 0.0), axis=0, keepdims=True) for idx in idx_rows]
    new_run = run + jnp.sum(sel, axis=1, keepdims=True)
    run_ref[...] = jnp.broadcast_to(new_run, run_ref.shape)
    cnt_ref[...] = jnp.broadcast_to(new_run, cnt_ref.shape)

    ssum = s_rows[0]
    for r in range(1, TOP_K):
        ssum = ssum + s_rows[r]
    denom = ssum + 1e-20
    idx_ref[...] = jnp.concatenate(idx_rows, axis=0).astype(I32)
    w_rep = jnp.concatenate([jnp.broadcast_to(sr / denom * ROUTED_SCALE, (SC_LANES, tm)) for sr in s_rows],
                            axis=0)
    wt_ref[...] = w_rep.T
    rk_ref[...] = jnp.concatenate(rk_rows, axis=0).astype(I32)


def _router_operands(g_ffn, w_router, router_bias, tm):
    wr_t = w_router.astype(F32).T
    wr_hi = wr_t.astype(BF16)
    wr_lo = (wr_t - wr_hi.astype(F32)).astype(BF16)
    tri = (np.arange(tm)[:, None] <= np.arange(tm)[None, :]).astype(np.float32)
    return (g_ffn.reshape(1, D_MODEL), wr_hi, wr_lo, router_bias.astype(F32).reshape(N_EXPERTS, 1),
            jnp.asarray(tri, BF16))


def _swiglu_packed(x_packed, wg, wu, wd):
    lo, hi = _unpack(x_packed)
    lo = lo.astype(BF16)
    hi = hi.astype(BF16)
    g = _dot(lo, wg[0:HALF, :]) + _dot(hi, wg[HALF:D_MODEL, :])
    u = _dot(lo, wu[0:HALF, :]) + _dot(hi, wu[HALF:D_MODEL, :])
    a = (g * _sigmoid(g) * u).astype(BF16)
    y = _dot(a, wd[...])
    return _pack(y[:, 0:HALF], y[:, HALF:D_MODEL])


def _shared_kernel(h2p_ref, wg_ref, wu_ref, wd_ref, *rest):
    ysh_ref = rest[-1]
    ysh_ref[...] = _swiglu_packed(h2p_ref[...], wg_ref, wu_ref, wd_ref)


def _shared(h2p, w_sh_gate, w_sh_up, w_sh_down, after=None):
    t = h2p.shape[0]
    tm = TM_FFN
    operands = (h2p, w_sh_gate.astype(BF16), w_sh_up.astype(BF16), w_sh_down.astype(BF16))
    in_specs = [pl.BlockSpec((tm, HALF), lambda i: (i, 0))]
    in_specs += [_const_spec(op.shape) for op in operands[1:]]
    if after is not None:
        operands += (after,)
        in_specs.append(pl.BlockSpec(memory_space=pl.ANY))
    return pl.pallas_call(
        _shared_kernel,
        grid=(t // tm,),
        in_specs=in_specs,
        out_specs=pl.BlockSpec((tm, HALF), lambda i: (i, 0)),
        out_shape=jax.ShapeDtypeStruct((t, HALF), I32),
        compiler_params=pltpu.CompilerParams(
            dimension_semantics=("arbitrary",), vmem_limit_bytes=VMEM_LIMIT),
        name="shared",
    )(*operands)


def _positions_kernel(pstart_ref, idx_ref, rk_ref, pos_ref):
    idx = idx_ref[...]
    acc = rk_ref[...]
    for e in range(N_EXPERTS):
        acc = acc + jnp.where(idx == e, pstart_ref[e], 0)
    pos_ref[...] = acc


def _positions(pstart, idx8, rk8):
    t = idx8.shape[1]
    tm = TM_POS
    spec = pl.BlockSpec((TOP_K, tm), lambda i, ps: (0, i))
    return pl.pallas_call(
        _positions_kernel,
        grid_spec=pltpu.PrefetchScalarGridSpec(num_scalar_prefetch=1, grid=(t // tm,),
                                               in_specs=[spec, spec], out_specs=spec),
        out_shape=jax.ShapeDtypeStruct((TOP_K, t), I32),
        name="positions",
    )(pstart, idx8, rk8)


def _sc_workers():
    info = plsc.get_sparse_core_info()
    return info.num_cores, info.num_cores * info.num_subcores


def _dispatch(pos_flat, h2p, n_rows):
    t = h2p.shape[0]
    ch = SC_SCATTER_ROWS
    n_cores, n_workers = _sc_workers()
    n_ch = t // (n_workers * ch)

    def body(pos_hbm, h_hbm, xs_hbm, idx_v, rows_v, sem):
        wid = lax.axis_index("s") * n_cores + lax.axis_index("c")

        @pl.loop(0, n_ch)
        def _(c):
            chunk = wid * n_ch + c
            pltpu.sync_copy(h_hbm.at[pl.ds(chunk * ch, ch)], rows_v)
            for k in range(TOP_K):
                pltpu.sync_copy(pos_hbm.at[pl.ds(k * t + chunk * ch, ch)], idx_v.at[k])
            copies = [pltpu.async_copy(rows_v, xs_hbm.at[idx_v.at[k]], sem) for k in range(TOP_K)]
            for cp in copies:
                cp.wait()

    return pl.kernel(
        body,
        out_type=jax.ShapeDtypeStruct((n_rows, HALF), I32),
        mesh=plsc.VectorSubcoreMesh(core_axis_name="c", subcore_axis_name="s"),
        scratch_types=[pltpu.VMEM((TOP_K, ch), I32), pltpu.VMEM((ch, HALF), I32),
                       pltpu.SemaphoreType.DMA],
        name="dispatch",
    )(pos_flat, h2p)


def _pairwise_sum(xs):
    while len(xs) > 1:
        xs = [xs[i] + xs[i + 1] for i in range(0, len(xs), 2)]
    return xs[0]


def _gather_sum(pos_flat, w_rep, ys, tok0, t):
    t_all = w_rep.shape[0]
    ch = SC_SUM_ROWS
    lanes = SC_LANES
    n_cores, n_workers = _sc_workers()
    per_w = t // n_workers
    n_ch = per_w // ch

    def body(pos_hbm, w_hbm, ys_hbm, out_hbm, idx_all, w_all, buf0, buf1, acc0, acc1, gsem0, gsem1, osem0, osem1):
        wid = lax.axis_index("s") * n_cores + lax.axis_index("c")
        base = wid * n_ch
        first = tok0 + wid * per_w
        bufs, accs, gsems, osems = (buf0, buf1), (acc0, acc1), (gsem0, gsem1), (osem0, osem1)
        for k in range(TOP_K):
            pltpu.sync_copy(pos_hbm.at[pl.ds(k * t_all + first, per_w)], idx_all.at[pl.ds(k * per_w, per_w)])
        pltpu.sync_copy(w_hbm.at[pl.ds(first, per_w)], w_all)

        def row_copy(c, s, k):
            rows = idx_all.at[pl.ds(k * per_w + c * ch, ch)]
            return pltpu.make_async_copy(ys_hbm.at[rows], bufs[s].at[k], gsems[s])

        def out_copy(c, s):
            return pltpu.make_async_copy(accs[s], out_hbm.at[pl.ds((base + c) * ch, ch)], osems[s])

        def fetch(c, s):
            for k in range(TOP_K):
                row_copy(c, s, k).start()

        def reduce_rows(c, s):
            @pl.loop(0, ch)
            def _(r):
                wks = [w_all[c * ch + r, pl.ds(k * lanes, lanes)] for k in range(TOP_K)]

                @pl.loop(0, HALF // lanes, step=2)
                def _(v):
                    for dv in range(2):
                        col = (v + dv) * lanes
                        los, his = [], []
                        for k in range(TOP_K):
                            words = bufs[s][k, r, pl.ds(col, lanes)]
                            los.append(wks[k] * lax.bitcast_convert_type(lax.shift_left(words, jnp.int32(16)), F32))
                            his.append(wks[k] * lax.bitcast_convert_type(words & jnp.int32(-65536), F32))
                        accs[s][r, pl.ds(col, lanes)] = _pairwise_sum(los)
                        accs[s][r, pl.ds(HALF + col, lanes)] = _pairwise_sum(his)

        fetch(0, 0)

        @pl.loop(0, n_ch, step=2)
        def _(c):
            for s in range(2):
                cc = c + s

                @pl.when(cc + 1 < n_ch)
                def _():
                    fetch(cc + 1, 1 - s)

                for k in range(TOP_K):
                    row_copy(cc, s, k).wait()

                @pl.when(cc >= 2)
                def _():
                    out_copy(cc - 2, s).wait()

                reduce_rows(cc, s)
                out_copy(cc, s).start()

        out_copy(n_ch - 2, 0).wait()
        out_copy(n_ch - 1, 1).wait()

    return pl.kernel(
        body,
        out_type=jax.ShapeDtypeStruct((t, D_MODEL), F32),
        mesh=plsc.VectorSubcoreMesh(core_axis_name="c", subcore_axis_name="s"),
        scratch_types=[pltpu.VMEM((TOP_K * per_w,), I32), pltpu.VMEM((per_w, TOP_K * lanes), F32),
                       pltpu.VMEM((TOP_K, ch, HALF), I32), pltpu.VMEM((TOP_K, ch, HALF), I32),
                       pltpu.VMEM((ch, D_MODEL), F32), pltpu.VMEM((ch, D_MODEL), F32),
                       pltpu.SemaphoreType.DMA, pltpu.SemaphoreType.DMA,
                       pltpu.SemaphoreType.DMA, pltpu.SemaphoreType.DMA],
        compiler_params=pltpu.CompilerParams(needs_layout_passes=False),
        name="gather_sum",
    )(pos_flat, w_rep, ys)


def _expert_kernel(nblk_ref, blk0_ref, cnt_ref, wg_ref, wu_ref, wd_ref, xs_hbm, after_ref, ys_hbm,
                   wg_s, wu_s, wd_s, xbuf, ybuf, in_sem, out_sem):
    del after_ref
    e = pl.program_id(0)
    nb = nblk_ref[e]
    b0 = blk0_ref[e]
    cnt = cnt_ref[e]
    n_all = blk0_ref[N_EXPERTS]
    ahead = EXPERT_SLOTS - 2
    wg_s[...] = wg_ref[0].astype(BF16)
    wu_s[...] = wu_ref[0].astype(BF16)
    wd_s[...] = wd_ref[0].astype(BF16)

    def slot_rows(g, n_blocks=1):
        start = (g & (EXPERT_SLOTS - 1)) * MOE_BLOCK
        return pl.ds(pl.multiple_of(start, MOE_BLOCK), n_blocks * MOE_BLOCK)

    def rows_of(g):
        return pl.ds(pl.multiple_of(g * MOE_BLOCK, MOE_BLOCK), MOE_BLOCK)

    def in_copy(g):
        return pltpu.make_async_copy(xs_hbm.at[rows_of(g)], xbuf.at[slot_rows(g)],
                                     in_sem.at[g & (EXPERT_SLOTS - 1)])

    def out_copy(g):
        return pltpu.make_async_copy(ybuf.at[slot_rows(g)], ys_hbm.at[rows_of(g)],
                                     out_sem.at[g & (EXPERT_SLOTS - 1)])

    @pl.when(e == 0)
    def _():
        for g in range(ahead):
            @pl.when(g < n_all)
            def _():
                in_copy(g).start()

    def arrive(g):
        in_copy(g).wait()

        @pl.when(g + ahead < n_all)
        def _():
            in_copy(g + ahead).start()

        @pl.when(g >= EXPERT_SLOTS)
        def _():
            out_copy(g - EXPERT_SLOTS).wait()

    def run_blocks(j, n_blocks):
        g = b0 + j
        for i in range(n_blocks):
            arrive(g + i)
        rows = lax.broadcasted_iota(I32, (n_blocks * MOE_BLOCK, HALF), 0)
        x = jnp.where(rows < cnt - j * MOE_BLOCK, xbuf[slot_rows(g, n_blocks)], 0)
        ybuf[slot_rows(g, n_blocks)] = _swiglu_packed(x, wg_s, wu_s, wd_s)
        for i in range(n_blocks):
            out_copy(g + i).start()

    lead = jnp.where(nb > 0, b0 & 1, 0)

    @pl.when(lead == 1)
    def _():
        run_blocks(0, 1)

    n_pairs = (nb - lead) // 2

    def one_pair(p, carry):
        run_blocks(lead + 2 * p, 2)
        return carry

    lax.fori_loop(0, n_pairs, one_pair, 0)

    @pl.when(nb - lead - 2 * n_pairs == 1)
    def _():
        run_blocks(nb - 1, 1)

    @pl.when(e == N_EXPERTS - 1)
    def _():
        for back in range(1, EXPERT_SLOTS + 1):
            @pl.when(n_all >= back)
            def _():
                out_copy(n_all - back).wait()


def _experts(n_blk, blk0, counts, n_rows, xs, w_gate, w_up, w_down, after):
    wspec_in = pl.BlockSpec((1, D_MODEL, EXPERT_DIM), lambda e, *_: (e, 0, 0))
    grid_spec = pltpu.PrefetchScalarGridSpec(
        num_scalar_prefetch=3,
        grid=(N_EXPERTS,),
        in_specs=[wspec_in, wspec_in,
                  pl.BlockSpec((1, EXPERT_DIM, D_MODEL), lambda e, *_: (e, 0, 0)),
                  pl.BlockSpec(memory_space=pl.ANY),
                  pl.BlockSpec(memory_space=pl.ANY)],
        out_specs=pl.BlockSpec(memory_space=pl.ANY),
        scratch_shapes=[pltpu.VMEM((D_MODEL, EXPERT_DIM), BF16), pltpu.VMEM((D_MODEL, EXPERT_DIM), BF16),
                        pltpu.VMEM((EXPERT_DIM, D_MODEL), BF16),
                        pltpu.VMEM((EXPERT_SLOTS * MOE_BLOCK, HALF), I32),
                        pltpu.VMEM((EXPERT_SLOTS * MOE_BLOCK, HALF), I32),
                        pltpu.SemaphoreType.DMA((EXPERT_SLOTS,)), pltpu.SemaphoreType.DMA((EXPERT_SLOTS,))],
    )
    return pl.pallas_call(
        _expert_kernel,
        grid_spec=grid_spec,
        out_shape=jax.ShapeDtypeStruct((n_rows, HALF), I32),
        compiler_params=pltpu.CompilerParams(
            dimension_semantics=("arbitrary",), vmem_limit_bytes=VMEM_LIMIT),
        name="experts",
    )(n_blk, blk0, counts, w_gate, w_up, w_down, xs, after)


def _combine_kernel(x1_ref, ysh_ref, moe_ref, p_ref, gple_ref, wpg_ref, wpp_ref, gpost_ref, *rest):
    out_ref = rest[-1]
    proj = _rms(_dot(p_ref[...].astype(BF16), wpp_ref[...]), gpost_ref[...])
    sh_lo, sh_hi = _unpack(ysh_ref[...])
    x2 = x1_ref[...] + moe_ref[...] + jnp.concatenate([sh_lo, sh_hi], axis=1)
    gate = _sigmoid(_dot(_rms(x2, gple_ref[...]).astype(BF16), wpg_ref[...]))
    out_ref[...] = x2 + gate * proj


def _combine(x1, ysh, moe, p2d, g_ple, w_ple_gate, w_ple_proj, g_ple_post, tok0, tok0_all, prev_out):
    t = p2d.shape[0]
    tm = TM_COMBINE
    steps = moe.shape[0] // tm
    off = tok0 // tm
    off_all = tok0_all // tm
    operands = [x1, ysh, moe, p2d, g_ple.reshape(1, D_MODEL), w_ple_gate.astype(BF16),
                w_ple_proj.astype(BF16), g_ple_post.reshape(1, D_MODEL)]
    in_specs = [pl.BlockSpec((tm, D_MODEL), lambda i: (off + i, 0)),
                pl.BlockSpec((tm, HALF), lambda i: (off + i, 0)),
                pl.BlockSpec((tm, D_MODEL), lambda i: (i, 0)),
                pl.BlockSpec((tm, PLE_DIM), lambda i: (off_all + i, 0)),
                _const_spec((1, D_MODEL)),
                _const_spec((D_MODEL, D_MODEL)),
                _const_spec((PLE_DIM, D_MODEL)),
                _const_spec((1, D_MODEL))]
    aliases = {}
    if prev_out is not None:
        aliases = {len(operands): 0}
        operands.append(prev_out)
        in_specs.append(pl.BlockSpec(memory_space=pl.ANY))
    return pl.pallas_call(
        _combine_kernel,
        grid=(steps,),
        in_specs=in_specs,
        out_specs=pl.BlockSpec((tm, D_MODEL), lambda i: (off_all + i, 0)),
        out_shape=jax.ShapeDtypeStruct((t, D_MODEL), F32),
        input_output_aliases=aliases,
        compiler_params=pltpu.CompilerParams(
            dimension_semantics=("arbitrary",), vmem_limit_bytes=VMEM_LIMIT),
        name="combine",
    )(*operands)


def _layer(x2d, p2d, batch, seq, g_mix, w_in, w_pool_mix, pool_scale, w_branch_a, q_gain, k_gain,
           attn_sinks, w_branch_b, w_gate, b_gate, w_out, g_ffn, w_router, router_bias, w_exp_gate,
           w_exp_up, w_exp_down, w_sh_gate, w_sh_up, w_sh_down, g_ple, w_ple_gate, w_ple_proj,
           g_ple_post):
    bh = batch // PIPELINE_HALVES
    th = bh * seq
    n_rows = (-(-(th * TOP_K) // MOE_BLOCK) + N_EXPERTS) * MOE_BLOCK

    halves = []
    after = None
    for h in range(PIPELINE_HALVES):
        x1, h2p, idx8, w_rep, rk8, cnt = _mixer(
            x2d, h * bh, bh, seq, g_mix, w_in, w_pool_mix, pool_scale, w_branch_a, q_gain, k_gain,
            attn_sinks, w_branch_b, w_gate, b_gate, w_out, g_ffn, w_router, router_bias, after=after)
        counts = cnt[:, 0].astype(I32)
        padded = (counts + MOE_BLOCK - 1) // MOE_BLOCK * MOE_BLOCK
        pend = jnp.cumsum(padded)
        pstart = pend - padded
        pos = _positions(pstart.astype(I32), idx8, rk8).reshape(-1)
        xs = _dispatch(pos, h2p, n_rows)
        blk0 = jnp.concatenate([pstart, pend[-1:]]) // MOE_BLOCK
        halves.append((x1, h2p, w_rep, pos, xs, (padded // MOE_BLOCK).astype(I32), blk0.astype(I32), counts))
        after = pos

    ys_all = []
    for (x1, h2p, w_rep, pos, xs, n_blk, blk0, counts) in halves:
        ys_all.append(_experts(n_blk, blk0, counts, n_rows, xs, w_exp_gate, w_exp_up, w_exp_down,
                               ys_all[-1] if ys_all else xs))

    out = None
    tc = th // COMBINE_CHUNKS
    for h, (x1, h2p, w_rep, pos, xs, n_blk, blk0, counts) in enumerate(halves):
        ys = ys_all[h]
        ysh = _shared(h2p, w_sh_gate, w_sh_up, w_sh_down, after=ys_all[-1] if out is None else out)
        for c in range(COMBINE_CHUNKS):
            moe = _gather_sum(pos, w_rep, ys, c * tc, tc)
            out = _combine(x1, ysh, moe, p2d, g_ple, w_ple_gate, w_ple_proj, g_ple_post,
                           c * tc, h * th + c * tc, out)
    return out


def kernel(x, p, g_mix, w_in, w_pool_mix, pool_scale, w_branch_a, q_gain, k_gain, attn_sinks,
           w_branch_b, w_gate, b_gate, w_out, g_ffn, w_router, router_bias, w_exp_gate, w_exp_up,
           w_exp_down, w_sh_gate, w_sh_up, w_sh_down, g_ple, w_ple_gate, w_ple_proj, g_ple_post):
    batch, seq, d = x.shape
    depth = p.shape[0]
    x2d = x.reshape(batch * seq, d)
    for i in range(depth):
        x2d = _layer(x2d, p[i].reshape(batch * seq, PLE_DIM), batch, seq, g_mix[i], w_in[i],
                     w_pool_mix[i], pool_scale[i], w_branch_a[i], q_gain[i], k_gain[i], attn_sinks[i],
                     w_branch_b[i], w_gate[i], b_gate[i], w_out[i], g_ffn[i], w_router[i],
                     router_bias[i], w_exp_gate[i], w_exp_up[i], w_exp_down[i], w_sh_gate[i],
                     w_sh_up[i], w_sh_down[i], g_ple[i], w_ple_gate[i], w_ple_proj[i], g_ple_post[i])
    return x2d.reshape(batch, seq, d)
```

```python
import functools

import numpy as np
import jax
import jax.numpy as jnp
from jax import lax
from jax.experimental import pallas as pl
from jax.experimental.pallas import tpu as pltpu
from jax.experimental.pallas import tpu_sc as plsc

F32 = jnp.float32
BF16 = jnp.bfloat16
I32 = jnp.int32

D_MODEL = 1024
PLE_DIM = 256
POOL_WINDOWS = (2, 4, 8, 16)
POOL_GROUP_DIM = 128
POOL_DIM = 512
N_HEADS = 8
N_KV_HEADS = 2
Q_GROUP = 4
HEAD_DIM = 64
HEAD_PAD = 128
Q_DIM = 512
KV_DIM = 128
ATTN_BLOCK = 128
N_EXPERTS = 64
N_EXPERT_GROUPS = 8
GROUP_SIZE = 8
TOPK_GROUPS = 4
TOP_K = 8
EXPERT_DIM = 256
ROUTED_SCALE = 2.5
MOE_BLOCK = 512
EXPERT_SLOTS = 8
EPS = 1e-6
HALF = D_MODEL // 2
MASKED = -1e30

COL_Q = POOL_DIM
COL_K = COL_Q + N_HEADS * HEAD_PAD
COL_V = COL_K + N_KV_HEADS * HEAD_PAD
IN_PAD = COL_V + N_KV_HEADS * HEAD_PAD
HEAD_ORDER = (0, 2, 1, 3)

TM_MIX = 512
TM_FFN = 1024
TM_COMBINE = 1024
TM_POS = 4096
COMBINE_CHUNKS = 2
PIPELINE_HALVES = 2
SC_SCATTER_ROWS = 128
SC_SUM_ROWS = 8
SC_LANES = 16
POOL_TAIL = 8
VMEM_LIMIT = 56 * 1024 * 1024

_NT = (((1,), (1,)), ((), ()))


def _dot(a, b):
    return jnp.dot(a, b, preferred_element_type=F32)


def _rms(x, g):
    ms = jnp.mean(x * x, axis=-1, keepdims=True)
    return x * lax.rsqrt(ms + EPS) * g


def _sigmoid(x):
    return 0.5 * jnp.tanh(0.5 * x) + 0.5


def _pack(lo, hi):
    lo_bits = lax.bitcast_convert_type(lo.astype(BF16).astype(F32), I32)
    hi_bits = lax.bitcast_convert_type(hi.astype(BF16).astype(F32), I32)
    return (hi_bits & jnp.int32(-65536)) | lax.shift_right_logical(lo_bits, jnp.int32(16))


def _unpack(w):
    lo = lax.bitcast_convert_type(lax.shift_left(w, jnp.int32(16)), F32)
    hi = lax.bitcast_convert_type(w & jnp.int32(-65536), F32)
    return lo, hi


def _mixer_kernel(x_ref, gmix_ref, win_ref, qg_ref, kg_ref, bias_ref, sink_ref, ones_ref, wmix_ref,
                  pscale_ref, wa_ref, wb_ref, wg_ref, bg_ref, wout_ref,
                  gffn_ref, wrh_ref, wrl_ref, rb_ref, tri_ref, *rest):
    (x1_ref, h2p_ref, idx_ref, wt_ref, rk_ref, cnt_ref,
     kbuf, vt_buf, pbuf, zbt_buf, run_ref) = rest[-11:]
    s = pl.program_id(1)
    tm = x_ref.shape[0]

    @pl.when(s == 0)
    def _():
        kbuf[0:ATTN_BLOCK, :] = jnp.zeros((ATTN_BLOCK, kbuf.shape[1]), BF16)
        vt_buf[:, 0:ATTN_BLOCK] = jnp.zeros((vt_buf.shape[0], ATTN_BLOCK), BF16)
        pbuf[:, 0:POOL_TAIL, :] = jnp.zeros((pbuf.shape[0], POOL_TAIL, POOL_DIM), F32)

    x = x_ref[...]
    hb = _rms(x, gmix_ref[...]).astype(BF16)
    u = _dot(hb, win_ref[...])

    up = u[:, 0:POOL_DIM]
    t0 = POOL_TAIL
    pbuf[0, t0:t0 + tm, :] = up
    s2 = up + pbuf[0, t0 - 1:t0 - 1 + tm, :]
    pbuf[1, t0:t0 + tm, 128:512] = s2[:, 128:512]
    s4 = s2[:, 128:512] + pbuf[1, t0 - 2:t0 - 2 + tm, 128:512]
    pbuf[2, t0:t0 + tm, 256:512] = s4[:, 128:384]
    s8 = s4[:, 128:384] + pbuf[2, t0 - 4:t0 - 4 + tm, 256:512]
    pbuf[3, t0:t0 + tm, 384:512] = s8[:, 128:256]
    s16 = s8[:, 128:256] + pbuf[3, t0 - 8:t0 - 8 + tm, 384:512]
    for lvl in range(4):
        pbuf[lvl, 0:POOL_TAIL, 128 * lvl:512] = pbuf[lvl, tm:tm + POOL_TAIL, 128 * lvl:512]
    wsums = (s2[:, 0:128], s4[:, 0:128], s8[:, 0:128], s16)
    tpos = (s * tm).astype(F32) + lax.broadcasted_iota(I32, (tm, 1), 0).astype(F32)
    za_parts = []
    for g, w in enumerate(POOL_WINDOWS):
        inv_count = 1.0 / jnp.minimum(tpos + 1.0, float(w))
        pooled = wsums[g] * inv_count - up[:, 128 * g:128 * (g + 1)]
        mixed = _dot(pooled.astype(BF16), wmix_ref[g]) * pscale_ref[:, 128 * g:128 * (g + 1)]
        za_parts.append(mixed.astype(BF16))
    za = jnp.concatenate(za_parts, axis=1)

    ones2 = ones_ref[...]
    q = u[:, COL_Q:COL_K]
    q2 = (q * q).astype(BF16)
    qss = jnp.concatenate([_dot(q2[:, 256 * c:256 * (c + 1)], ones2) for c in range(N_HEADS // 2)], axis=1)
    qn = (q * lax.rsqrt(qss * (1.0 / HEAD_DIM) + EPS) * qg_ref[...]).astype(BF16)
    k = u[:, COL_K:COL_V]
    kss = _dot((k * k).astype(BF16), ones2)
    kbuf[ATTN_BLOCK:ATTN_BLOCK + tm, :] = (k * lax.rsqrt(kss * (1.0 / HEAD_DIM) + EPS) * kg_ref[...]).astype(BF16)
    vt_buf[:, ATTN_BLOCK:ATTN_BLOCK + tm] = u[:, COL_V:IN_PAD].T.astype(BF16)

    nq = Q_GROUP * ATTN_BLOCK
    key_j = lax.broadcasted_iota(I32, (ATTN_BLOCK, nq), 0)
    qry_i = lax.broadcasted_iota(I32, (ATTN_BLOCK, nq), 1) & (ATTN_BLOCK - 1)
    from_prev = key_j > qry_i
    first = jnp.where(s == 0, 1, 0)
    for n in range(tm // ATTN_BLOCK):
        r0 = ATTN_BLOCK * n
        for kv in range(N_KV_HEADS):
            c0 = HEAD_PAD * kv
            qs = jnp.concatenate(
                [qn[r0:r0 + ATTN_BLOCK, HEAD_PAD * (kv * Q_GROUP + g):HEAD_PAD * (kv * Q_GROUP + g + 1)]
                 for g in HEAD_ORDER], axis=0)
            kk = kbuf[r0:r0 + 2 * ATTN_BLOCK, c0:c0 + HEAD_PAD]
            st = lax.dot_general(kk, qs, _NT, preferred_element_type=F32)
            sc = jnp.where(from_prev, st[0:ATTN_BLOCK], st[ATTN_BLOCK:2 * ATTN_BLOCK])
            sc = sc + (bias_ref[first, kv] if n == 0 else bias_ref[0, kv])
            sink = sink_ref[kv]
            m = jnp.maximum(jnp.max(sc, axis=0, keepdims=True), sink)
            e = jnp.exp(sc - m)
            inv_den = 1.0 / (jnp.sum(e, axis=0, keepdims=True) + jnp.exp(sink - m))
            pt = jnp.concatenate([jnp.where(from_prev, e, 0.0), jnp.where(from_prev, 0.0, e)],
                                 axis=0).astype(BF16)
            v_even = vt_buf[c0:c0 + HEAD_PAD, r0:r0 + 2 * ATTN_BLOCK]
            v_odd = jnp.concatenate([v_even[HEAD_DIM:HEAD_PAD], v_even[0:HEAD_DIM]], axis=0)
            half = 2 * ATTN_BLOCK
            o = (_dot(v_even, pt[:, 0:half]) * inv_den[:, 0:half]
                 + _dot(v_odd, pt[:, half:2 * half]) * inv_den[:, half:2 * half])
            ch = kv * Q_GROUP * HEAD_DIM
            zbt_buf[ch:ch + HEAD_PAD, r0:r0 + ATTN_BLOCK] = o[:, 0:ATTN_BLOCK]
            zbt_buf[ch + HEAD_PAD:ch + 2 * HEAD_PAD, r0:r0 + ATTN_BLOCK] = o[:, ATTN_BLOCK:half]
    kbuf[0:ATTN_BLOCK, :] = kbuf[tm:tm + ATTN_BLOCK, :]
    vt_buf[:, 0:ATTN_BLOCK] = vt_buf[:, tm:tm + ATTN_BLOCK]

    y_a = _dot(za, wa_ref[...])
    y_b = _dot(zbt_buf[...].T.astype(BF16), wb_ref[...])
    g_a = _sigmoid(_dot(hb, wg_ref[:, 0:D_MODEL]) + bg_ref[:, 0:D_MODEL])
    merged = g_a * y_a
    g_b = _sigmoid(_dot(hb, wg_ref[:, D_MODEL:2 * D_MODEL]) + bg_ref[:, D_MODEL:2 * D_MODEL])
    merged = (merged + g_b * y_b).astype(BF16)
    x1 = x + _dot(merged, wout_ref[...])
    x1_ref[...] = x1

    is_first = (pl.program_id(0) == 0) & (s == 0)
    _route_tile(x1, is_first, gffn_ref, wrh_ref, wrl_ref, rb_ref, tri_ref,
                h2p_ref, idx_ref, wt_ref, rk_ref, cnt_ref, run_ref)


def _attn_tables(attn_sinks):
    slopes = 2.0 ** (-8.0 * (np.arange(N_HEADS) + 1) / N_HEADS)
    j = np.arange(ATTN_BLOCK)[:, None]
    i = np.arange(ATTN_BLOCK)[None, :]
    from_prev = j > i
    dist = np.where(from_prev, ATTN_BLOCK + i - j, i - j)
    bias = np.empty((2, N_KV_HEADS, ATTN_BLOCK, Q_GROUP * ATTN_BLOCK), np.float32)
    for first in range(2):
        ok = ~from_prev if first else np.ones_like(from_prev)
        for kv in range(N_KV_HEADS):
            for slot, g in enumerate(HEAD_ORDER):
                sl = np.float32(slopes[kv * Q_GROUP + g])
                val = -(sl * dist.astype(np.float32))
                bias[first, kv, :, slot * ATTN_BLOCK:(slot + 1) * ATTN_BLOCK] = np.where(ok, val, MASKED)
    sink = attn_sinks.astype(F32).reshape(N_KV_HEADS, Q_GROUP)[:, np.asarray(HEAD_ORDER)]
    sink = jnp.repeat(sink, ATTN_BLOCK, axis=1)
    return jnp.asarray(bias), sink.reshape(N_KV_HEADS, 1, Q_GROUP * ATTN_BLOCK)


def _pad_heads(w, n_heads):
    k = w.shape[0]
    w = w.reshape(k, n_heads, HEAD_DIM)
    w = jnp.pad(w, ((0, 0), (0, 0), (0, HEAD_PAD - HEAD_DIM)))
    return w.reshape(k, n_heads * HEAD_PAD)


def _const_spec(shape):
    nd = len(shape)
    return pl.BlockSpec(shape, lambda *_: (0,) * nd)


def _mixer(x2d, batch0, batch, seq, g_mix, w_in, w_pool_mix, pool_scale, w_branch_a, q_gain, k_gain,
           attn_sinks, w_branch_b, w_gate, b_gate, w_out, g_ffn, w_router, router_bias, after=None):
    t = batch * seq
    tm = TM_MIX
    ns = seq // tm
    w_q = w_in[:, POOL_DIM:POOL_DIM + Q_DIM]
    w_k = w_in[:, POOL_DIM + Q_DIM:POOL_DIM + Q_DIM + KV_DIM]
    w_v = w_in[:, POOL_DIM + Q_DIM + KV_DIM:]
    win_p = jnp.concatenate([w_in[:, :POOL_DIM], _pad_heads(w_q, N_HEADS), _pad_heads(w_k, N_KV_HEADS),
                             _pad_heads(w_v, N_KV_HEADS)], axis=1).astype(BF16)
    qg = jnp.tile(jnp.pad(q_gain * (HEAD_DIM ** -0.5), (0, HEAD_PAD - HEAD_DIM)), N_HEADS).reshape(1, -1)
    kg = jnp.tile(jnp.pad(k_gain, (0, HEAD_PAD - HEAD_DIM)), N_KV_HEADS).reshape(1, -1)
    bias, sink = _attn_tables(attn_sinks)
    ones2 = jnp.asarray(np.kron(np.eye(2), np.ones((HEAD_PAD, HEAD_PAD))), BF16)
    operands = (x2d, g_mix.reshape(1, D_MODEL), win_p, qg, kg, bias, sink, ones2, w_pool_mix.astype(BF16),
                pool_scale.reshape(1, POOL_DIM), w_branch_a.astype(BF16), w_branch_b.astype(BF16),
                w_gate.astype(BF16), b_gate.reshape(1, 2 * D_MODEL), w_out.astype(BF16))
    operands += _router_operands(g_ffn, w_router, router_bias, tm)
    in_specs = [pl.BlockSpec((tm, D_MODEL), lambda b, s: ((batch0 + b) * ns + s, 0))]
    in_specs += [_const_spec(op.shape) for op in operands[1:]]
    if after is not None:
        operands += (after,)
        in_specs.append(pl.BlockSpec(memory_space=pl.ANY))
    row8 = pl.BlockSpec((TOP_K, tm), lambda b, s: (0, b * ns + s))
    return pl.pallas_call(
        _mixer_kernel,
        grid=(batch, ns),
        in_specs=in_specs,
        out_specs=[pl.BlockSpec((tm, D_MODEL), lambda b, s: (b * ns + s, 0)),
                   pl.BlockSpec((tm, HALF), lambda b, s: (b * ns + s, 0)),
                   row8,
                   pl.BlockSpec((tm, TOP_K * SC_LANES), lambda b, s: (b * ns + s, 0)),
                   row8,
                   pl.BlockSpec((N_EXPERTS, 128), lambda b, s: (0, 0))],
        out_shape=[jax.ShapeDtypeStruct((t, D_MODEL), F32),
                   jax.ShapeDtypeStruct((t, HALF), I32),
                   jax.ShapeDtypeStruct((TOP_K, t), I32),
                   jax.ShapeDtypeStruct((t, TOP_K * SC_LANES), F32),
                   jax.ShapeDtypeStruct((TOP_K, t), I32),
                   jax.ShapeDtypeStruct((N_EXPERTS, 128), F32)],
        scratch_shapes=[
            pltpu.VMEM((ATTN_BLOCK + tm, N_KV_HEADS * HEAD_PAD), BF16),
            pltpu.VMEM((N_KV_HEADS * HEAD_PAD, ATTN_BLOCK + tm), BF16),
            pltpu.VMEM((4, POOL_TAIL + tm, POOL_DIM), F32),
            pltpu.VMEM((Q_DIM, tm), F32),
            pltpu.VMEM((N_EXPERTS, 128), F32),
        ],
        compiler_params=pltpu.CompilerParams(
            dimension_semantics=("arbitrary", "arbitrary"), vmem_limit_bytes=VMEM_LIMIT),
        name="mixer",
    )(*operands)


def _route_tile(x1, is_first, gffn_ref, wrh_ref, wrl_ref, rb_ref, tri_ref,
                h2p_ref, idx_ref, wt_ref, rk_ref, cnt_ref, run_ref):
    tm = x1.shape[0]

    @pl.when(is_first)
    def _():
        run_ref[...] = jnp.zeros_like(run_ref)

    h2 = _rms(x1, gffn_ref[...])
    hb = h2.astype(BF16)
    h2p_ref[...] = _pack(h2[:, 0:HALF], h2[:, HALF:D_MODEL])

    lo = (h2 - hb.astype(F32)).astype(BF16)
    wrh = wrh_ref[...]
    logits = (lax.dot_general(wrh, hb, _NT, preferred_element_type=F32)
              + lax.dot_general(wrl_ref[...], hb, _NT, preferred_element_type=F32)
              + lax.dot_general(wrh, lo, _NT, preferred_element_type=F32))
    scores = _sigmoid(logits)
    choice = scores + rb_ref[...]

    neg = -jnp.inf
    sub8 = lax.broadcasted_iota(I32, (GROUP_SIZE, tm), 0).astype(F32)
    grp_rows = []
    for gi in range(N_EXPERT_GROUPS):
        cg = choice[GROUP_SIZE * gi:GROUP_SIZE * (gi + 1)]
        m1 = jnp.max(cg, axis=0, keepdims=True)
        first = jnp.min(jnp.where(cg == m1, sub8, float(GROUP_SIZE)), axis=0, keepdims=True)
        m2 = jnp.max(jnp.where(sub8 == first, neg, cg), axis=0, keepdims=True)
        grp_rows.append(m1 + m2)
    gs = jnp.concatenate(grp_rows, axis=0)
    beaten = jnp.zeros((N_EXPERT_GROUPS, tm), F32)
    for gi in range(N_EXPERT_GROUPS):
        row = grp_rows[gi]
        wins = (row > gs) | ((row == gs) & (sub8 > float(gi)))
        beaten = beaten + jnp.where(wins, 1.0, 0.0)
    gsel = beaten < float(TOPK_GROUPS)
    cm = jnp.concatenate(
        [jnp.where(gsel[gi:gi + 1], choice[GROUP_SIZE * gi:GROUP_SIZE * (gi + 1)], neg)
         for gi in range(N_EXPERT_GROUPS)], axis=0)

    iota_e = lax.broadcasted_iota(I32, (N_EXPERTS, tm), 0).astype(F32)
    idx_rows, s_rows = [], []
    sel = jnp.zeros((N_EXPERTS, tm), F32)
    for _ in range(TOP_K):
        m = jnp.max(cm, axis=0, keepdims=True)
        idx = jnp.min(jnp.where(cm == m, iota_e, float(N_EXPERTS)), axis=0, keepdims=True)
        oh = iota_e == idx
        s_rows.append(jnp.sum(jnp.where(oh, scores, 0.0), axis=0, keepdims=True))
        idx_rows.append(idx)
        cm = jnp.where(oh, neg, cm)
        sel = sel + jnp.where(oh, 1.0, 0.0)

    run = run_ref[:, 0:1]
    cum = _dot(sel.astype(BF16), tri_ref[...])
    before = run + cum - sel
    rk_rows = [jnp.sum(jnp.where(iota_e == idx, before, 0.0), axis=0, keepdims=True) for idx in idx_rows]
    new_run = run + jnp.sum(sel, axis=1, keepdims=True)
    run_ref[...] = jnp.broadcast_to(new_run, run_ref.shape)
    cnt_ref[...] = jnp.broadcast_to(new_run, cnt_ref.shape)

    ssum = s_rows[0]
    for r in range(1, TOP_K):
        ssum = ssum + s_rows[r]
    denom = ssum + 1e-20
    idx_ref[...] = jnp.concatenate(idx_rows, axis=0).astype(I32)
    w_rep = jnp.concatenate([jnp.broadcast_to(sr / denom * ROUTED_SCALE, (SC_LANES, tm)) for sr in s_rows],
                            axis=0)
    wt_ref[...] = w_rep.T
    rk_ref[...] = jnp.concatenate(rk_rows, axis=0).astype(I32)


def _router_operands(g_ffn, w_router, router_bias, tm):
    wr_t = w_router.astype(F32).T
    wr_hi = wr_t.astype(BF16)
    wr_lo = (wr_t - wr_hi.astype(F32)).astype(BF16)
    tri = (np.arange(tm)[:, None] <= np.arange(tm)[None, :]).astype(np.float32)
    return (g_ffn.reshape(1, D_MODEL), wr_hi, wr_lo, router_bias.astype(F32).reshape(N_EXPERTS, 1),
            jnp.asarray(tri, BF16))


def _swiglu_packed(x_packed, wg, wu, wd):
    lo, hi = _unpack(x_packed)
    lo = lo.astype(BF16)
    hi = hi.astype(BF16)
    g = _dot(lo, wg[0:HALF, :]) + _dot(hi, wg[HALF:D_MODEL, :])
    u = _dot(lo, wu[0:HALF, :]) + _dot(hi, wu[HALF:D_MODEL, :])
    a = (g * _sigmoid(g) * u).astype(BF16)
    y = _dot(a, wd[...])
    return _pack(y[:, 0:HALF], y[:, HALF:D_MODEL])


def _shared_kernel(h2p_ref, wg_ref, wu_ref, wd_ref, *rest):
    ysh_ref = rest[-1]
    ysh_ref[...] = _swiglu_packed(h2p_ref[...], wg_ref, wu_ref, wd_ref)


def _shared(h2p, w_sh_gate, w_sh_up, w_sh_down, after=None):
    t = h2p.shape[0]
    tm = TM_FFN
    operands = (h2p, w_sh_gate.astype(BF16), w_sh_up.astype(BF16), w_sh_down.astype(BF16))
    in_specs = [pl.BlockSpec((tm, HALF), lambda i: (i, 0))]
    in_specs += [_const_spec(op.shape) for op in operands[1:]]
    if after is not None:
        operands += (after,)
        in_specs.append(pl.BlockSpec(memory_space=pl.ANY))
    return pl.pallas_call(
        _shared_kernel,
        grid=(t // tm,),
        in_specs=in_specs,
        out_specs=pl.BlockSpec((tm, HALF), lambda i: (i, 0)),
        out_shape=jax.ShapeDtypeStruct((t, HALF), I32),
        compiler_params=pltpu.CompilerParams(
            dimension_semantics=("arbitrary",), vmem_limit_bytes=VMEM_LIMIT),
        name="shared",
    )(*operands)


def _positions_kernel(pstart_ref, idx_ref, rk_ref, pos_ref):
    idx = idx_ref[...]
    acc = rk_ref[...]
    for e in range(N_EXPERTS):
        acc = acc + jnp.where(idx == e, pstart_ref[e], 0)
    pos_ref[...] = acc


def _positions(pstart, idx8, rk8):
    t = idx8.shape[1]
    tm = TM_POS
    spec = pl.BlockSpec((TOP_K, tm), lambda i, ps: (0, i))
    return pl.pallas_call(
        _positions_kernel,
        grid_spec=pltpu.PrefetchScalarGridSpec(num_scalar_prefetch=1, grid=(t // tm,),
                                               in_specs=[spec, spec], out_specs=spec),
        out_shape=jax.ShapeDtypeStruct((TOP_K, t), I32),
        name="positions",
    )(pstart, idx8, rk8)


def _sc_workers():
    info = plsc.get_sparse_core_info()
    return info.num_cores, info.num_cores * info.num_subcores


def _dispatch(pos_flat, h2p, n_rows):
    t = h2p.shape[0]
    ch = SC_SCATTER_ROWS
    n_cores, n_workers = _sc_workers()
    n_ch = t // (n_workers * ch)

    def body(pos_hbm, h_hbm, xs_hbm, idx_v, rows_v, sem):
        wid = lax.axis_index("s") * n_cores + lax.axis_index("c")

        @pl.loop(0, n_ch)
        def _(c):
            chunk = wid * n_ch + c
            pltpu.sync_copy(h_hbm.at[pl.ds(chunk * ch, ch)], rows_v)
            for k in range(TOP_K):
                pltpu.sync_copy(pos_hbm.at[pl.ds(k * t + chunk * ch, ch)], idx_v.at[k])
            copies = [pltpu.async_copy(rows_v, xs_hbm.at[idx_v.at[k]], sem) for k in range(TOP_K)]
            for cp in copies:
                cp.wait()

    return pl.kernel(
        body,
        out_type=jax.ShapeDtypeStruct((n_rows, HALF), I32),
        mesh=plsc.VectorSubcoreMesh(core_axis_name="c", subcore_axis_name="s"),
        scratch_types=[pltpu.VMEM((TOP_K, ch), I32), pltpu.VMEM((ch, HALF), I32),
                       pltpu.SemaphoreType.DMA],
        name="dispatch",
    )(pos_flat, h2p)


def _pairwise_sum(xs):
    while len(xs) > 1:
        xs = [xs[i] + xs[i + 1] for i in range(0, len(xs), 2)]
    return xs[0]


def _gather_sum(pos_flat, w_rep, ys, tok0, t):
    t_all = w_rep.shape[0]
    ch = SC_SUM_ROWS
    lanes = SC_LANES
    n_cores, n_workers = _sc_workers()
    per_w = t // n_workers
    n_ch = per_w // ch

    def body(pos_hbm, w_hbm, ys_hbm, out_hbm, idx_all, w_all, buf0, buf1, acc0, acc1, gsem0, gsem1, osem0, osem1):
        wid = lax.axis_index("s") * n_cores + lax.axis_index("c")
        base = wid * n_ch
        first = tok0 + wid * per_w
        bufs, accs, gsems, osems = (buf0, buf1), (acc0, acc1), (gsem0, gsem1), (osem0, osem1)
        for k in range(TOP_K):
            pltpu.sync_copy(pos_hbm.at[pl.ds(k * t_all + first, per_w)], idx_all.at[pl.ds(k * per_w, per_w)])
        pltpu.sync_copy(w_hbm.at[pl.ds(first, per_w)], w_all)

        def row_copy(c, s, k):
            rows = idx_all.at[pl.ds(k * per_w + c * ch, ch)]
            return pltpu.make_async_copy(ys_hbm.at[rows], bufs[s].at[k], gsems[s])

        def out_copy(c, s):
            return pltpu.make_async_copy(accs[s], out_hbm.at[pl.ds((base + c) * ch, ch)], osems[s])

        def fetch(c, s):
            for k in range(TOP_K):
                row_copy(c, s, k).start()

        def reduce_rows(c, s):
            @pl.loop(0, ch)
            def _(r):
                wks = [w_all[c * ch + r, pl.ds(k * lanes, lanes)] for k in range(TOP_K)]

                @pl.loop(0, HALF // lanes, step=2)
                def _(v):
                    for dv in range(2):
                        col = (v + dv) * lanes
                        los, his = [], []
                        for k in range(TOP_K):
                            words = bufs[s][k, r, pl.ds(col, lanes)]
                            los.append(wks[k] * lax.bitcast_convert_type(lax.shift_left(words, jnp.int32(16)), F32))
                            his.append(wks[k] * lax.bitcast_convert_type(words & jnp.int32(-65536), F32))
                        accs[s][r, pl.ds(col, lanes)] = _pairwise_sum(los)
                        accs[s][r, pl.ds(HALF + col, lanes)] = _pairwise_sum(his)

        fetch(0, 0)

        @pl.loop(0, n_ch, step=2)
        def _(c):
            for s in range(2):
                cc = c + s

                @pl.when(cc + 1 < n_ch)
                def _():
                    fetch(cc + 1, 1 - s)

                for k in range(TOP_K):
                    row_copy(cc, s, k).wait()

                @pl.when(cc >= 2)
                def _():
                    out_copy(cc - 2, s).wait()

                reduce_rows(cc, s)
                out_copy(cc, s).start()

        out_copy(n_ch - 2, 0).wait()
        out_copy(n_ch - 1, 1).wait()

    return pl.kernel(
        body,
        out_type=jax.ShapeDtypeStruct((t, D_MODEL), F32),
        mesh=plsc.VectorSubcoreMesh(core_axis_name="c", subcore_axis_name="s"),
        scratch_types=[pltpu.VMEM((TOP_K * per_w,), I32), pltpu.VMEM((per_w, TOP_K * lanes), F32),
                       pltpu.VMEM((TOP_K, ch, HALF), I32), pltpu.VMEM((TOP_K, ch, HALF), I32),
                       pltpu.VMEM((ch, D_MODEL), F32), pltpu.VMEM((ch, D_MODEL), F32),
                       pltpu.SemaphoreType.DMA, pltpu.SemaphoreType.DMA,
                       pltpu.SemaphoreType.DMA, pltpu.SemaphoreType.DMA],
        compiler_params=pltpu.CompilerParams(needs_layout_passes=False),
        name="gather_sum",
    )(pos_flat, w_rep, ys)


def _expert_kernel(nblk_ref, blk0_ref, cnt_ref, wg_ref, wu_ref, wd_ref, xs_hbm, after_ref, ys_hbm,
                   wg_s, wu_s, wd_s, xbuf, ybuf, in_sem, out_sem):
    del after_ref
    e = pl.program_id(0)
    nb = nblk_ref[e]
    b0 = blk0_ref[e]
    cnt = cnt_ref[e]
    n_all = blk0_ref[N_EXPERTS]
    ahead = EXPERT_SLOTS - 2
    wg_s[...] = wg_ref[0].astype(BF16)
    wu_s[...] = wu_ref[0].astype(BF16)
    wd_s[...] = wd_ref[0].astype(BF16)

    def slot_rows(g, n_blocks=1):
        start = (g & (EXPERT_SLOTS - 1)) * MOE_BLOCK
        return pl.ds(pl.multiple_of(start, MOE_BLOCK), n_blocks * MOE_BLOCK)

    def rows_of(g):
        return pl.ds(pl.multiple_of(g * MOE_BLOCK, MOE_BLOCK), MOE_BLOCK)

    def in_copy(g):
        return pltpu.make_async_copy(xs_hbm.at[rows_of(g)], xbuf.at[slot_rows(g)],
                                     in_sem.at[g & (EXPERT_SLOTS - 1)])

    def out_copy(g):
        return pltpu.make_async_copy(ybuf.at[slot_rows(g)], ys_hbm.at[rows_of(g)],
                                     out_sem.at[g & (EXPERT_SLOTS - 1)])

    @pl.when(e == 0)
    def _():
        for g in range(ahead):
            @pl.when(g < n_all)
            def _():
                in_copy(g).start()

    def arrive(g):
        in_copy(g).wait()

        @pl.when(g + ahead < n_all)
        def _():
            in_copy(g + ahead).start()

        @pl.when(g >= EXPERT_SLOTS)
        def _():
            out_copy(g - EXPERT_SLOTS).wait()

    def run_blocks(j, n_blocks):
        g = b0 + j
        for i in range(n_blocks):
            arrive(g + i)
        rows = lax.broadcasted_iota(I32, (n_blocks * MOE_BLOCK, HALF), 0)
        x = jnp.where(rows < cnt - j * MOE_BLOCK, xbuf[slot_rows(g, n_blocks)], 0)
        ybuf[slot_rows(g, n_blocks)] = _swiglu_packed(x, wg_s, wu_s, wd_s)
        for i in range(n_blocks):
            out_copy(g + i).start()

    lead = jnp.where(nb > 0, b0 & 1, 0)

    @pl.when(lead == 1)
    def _():
        run_blocks(0, 1)

    n_pairs = (nb - lead) // 2

    def one_pair(p, carry):
        run_blocks(lead + 2 * p, 2)
        return carry

    lax.fori_loop(0, n_pairs, one_pair, 0)

    @pl.when(nb - lead - 2 * n_pairs == 1)
    def _():
        run_blocks(nb - 1, 1)

    @pl.when(e == N_EXPERTS - 1)
    def _():
        for back in range(1, EXPERT_SLOTS + 1):
            @pl.when(n_all >= back)
            def _():
                out_copy(n_all - back).wait()


def _experts(n_blk, blk0, counts, n_rows, xs, w_gate, w_up, w_down, after):
    wspec_in = pl.BlockSpec((1, D_MODEL, EXPERT_DIM), lambda e, *_: (e, 0, 0))
    grid_spec = pltpu.PrefetchScalarGridSpec(
        num_scalar_prefetch=3,
        grid=(N_EXPERTS,),
        in_specs=[wspec_in, wspec_in,
                  pl.BlockSpec((1, EXPERT_DIM, D_MODEL), lambda e, *_: (e, 0, 0)),
                  pl.BlockSpec(memory_space=pl.ANY),
                  pl.BlockSpec(memory_space=pl.ANY)],
        out_specs=pl.BlockSpec(memory_space=pl.ANY),
        scratch_shapes=[pltpu.VMEM((D_MODEL, EXPERT_DIM), BF16), pltpu.VMEM((D_MODEL, EXPERT_DIM), BF16),
                        pltpu.VMEM((EXPERT_DIM, D_MODEL), BF16),
                        pltpu.VMEM((EXPERT_SLOTS * MOE_BLOCK, HALF), I32),
                        pltpu.VMEM((EXPERT_SLOTS * MOE_BLOCK, HALF), I32),
                        pltpu.SemaphoreType.DMA((EXPERT_SLOTS,)), pltpu.SemaphoreType.DMA((EXPERT_SLOTS,))],
    )
    return pl.pallas_call(
        _expert_kernel,
        grid_spec=grid_spec,
        out_shape=jax.ShapeDtypeStruct((n_rows, HALF), I32),
        compiler_params=pltpu.CompilerParams(
            dimension_semantics=("arbitrary",), vmem_limit_bytes=VMEM_LIMIT),
        name="experts",
    )(n_blk, blk0, counts, w_gate, w_up, w_down, xs, after)


def _combine_kernel(x1_ref, ysh_ref, moe_ref, p_ref, gple_ref, wpg_ref, wpp_ref, gpost_ref, *rest):
    out_ref = rest[-1]
    proj = _rms(_dot(p_ref[...].astype(BF16), wpp_ref[...]), gpost_ref[...])
    sh_lo, sh_hi = _unpack(ysh_ref[...])
    x2 = x1_ref[...] + moe_ref[...] + jnp.concatenate([sh_lo, sh_hi], axis=1)
    gate = _sigmoid(_dot(_rms(x2, gple_ref[...]).astype(BF16), wpg_ref[...]))
    out_ref[...] = x2 + gate * proj


def _combine(x1, ysh, moe, p2d, g_ple, w_ple_gate, w_ple_proj, g_ple_post, tok0, tok0_all, prev_out):
    t = p2d.shape[0]
    tm = TM_COMBINE
    steps = moe.shape[0] // tm
    off = tok0 // tm
    off_all = tok0_all // tm
    operands = [x1, ysh, moe, p2d, g_ple.reshape(1, D_MODEL), w_ple_gate.astype(BF16),
                w_ple_proj.astype(BF16), g_ple_post.reshape(1, D_MODEL)]
    in_specs = [pl.BlockSpec((tm, D_MODEL), lambda i: (off + i, 0)),
                pl.BlockSpec((tm, HALF), lambda i: (off + i, 0)),
                pl.BlockSpec((tm, D_MODEL), lambda i: (i, 0)),
                pl.BlockSpec((tm, PLE_DIM), lambda i: (off_all + i, 0)),
                _const_spec((1, D_MODEL)),
                _const_spec((D_MODEL, D_MODEL)),
                _const_spec((PLE_DIM, D_MODEL)),
                _const_spec((1, D_MODEL))]
    aliases = {}
    if prev_out is not None:
        aliases = {len(operands): 0}
        operands.append(prev_out)
        in_specs.append(pl.BlockSpec(memory_space=pl.ANY))
    return pl.pallas_call(
        _combine_kernel,
        grid=(steps,),
        in_specs=in_specs,
        out_specs=pl.BlockSpec((tm, D_MODEL), lambda i: (off_all + i, 0)),
        out_shape=jax.ShapeDtypeStruct((t, D_MODEL), F32),
        input_output_aliases=aliases,
        compiler_params=pltpu.CompilerParams(
            dimension_semantics=("arbitrary",), vmem_limit_bytes=VMEM_LIMIT),
        name="combine",
    )(*operands)


def _layer(x2d, p2d, batch, seq, g_mix, w_in, w_pool_mix, pool_scale, w_branch_a, q_gain, k_gain,
           attn_sinks, w_branch_b, w_gate, b_gate, w_out, g_ffn, w_router, router_bias, w_exp_gate,
           w_exp_up, w_exp_down, w_sh_gate, w_sh_up, w_sh_down, g_ple, w_ple_gate, w_ple_proj,
           g_ple_post):
    bh = batch // PIPELINE_HALVES
    th = bh * seq
    n_rows = (-(-(th * TOP_K) // MOE_BLOCK) + N_EXPERTS) * MOE_BLOCK

    halves = []
    after = None
    for h in range(PIPELINE_HALVES):
        x1, h2p, idx8, w_rep, rk8, cnt = _mixer(
            x2d, h * bh, bh, seq, g_mix, w_in, w_pool_mix, pool_scale, w_branch_a, q_gain, k_gain,
            attn_sinks, w_branch_b, w_gate, b_gate, w_out, g_ffn, w_router, router_bias, after=after)
        counts = cnt[:, 0].astype(I32)
        padded = (counts + MOE_BLOCK - 1) // MOE_BLOCK * MOE_BLOCK
        pend = jnp.cumsum(padded)
        pstart = pend - padded
        pos = _positions(pstart.astype(I32), idx8, rk8).reshape(-1)
        xs = _dispatch(pos, h2p, n_rows)
        blk0 = jnp.concatenate([pstart, pend[-1:]]) // MOE_BLOCK
        halves.append((x1, h2p, w_rep, pos, xs, (padded // MOE_BLOCK).astype(I32), blk0.astype(I32), counts))
        after = pos

    ys_all = []
    for (x1, h2p, w_rep, pos, xs, n_blk, blk0, counts) in halves:
        ys_all.append(_experts(n_blk, blk0, counts, n_rows, xs, w_exp_gate, w_exp_up, w_exp_down,
                               ys_all[-1] if ys_all else xs))

    out = None
    tc = th // COMBINE_CHUNKS
    for h, (x1, h2p, w_rep, pos, xs, n_blk, blk0, counts) in enumerate(halves):
        ys = ys_all[h]
        ysh = _shared(h2p, w_sh_gate, w_sh_up, w_sh_down, after=ys_all[-1] if out is None else out)
        for c in range(COMBINE_CHUNKS):
            moe = _gather_sum(pos, w_rep, ys, c * tc, tc)
            out = _combine(x1, ysh, moe, p2d, g_ple, w_ple_gate, w_ple_proj, g_ple_post,
                           c * tc, h * th + c * tc, out)
    return out


def kernel(x, p, g_mix, w_in, w_pool_mix, pool_scale, w_branch_a, q_gain, k_gain, attn_sinks,
           w_branch_b, w_gate, b_gate, w_out, g_ffn, w_router, router_bias, w_exp_gate, w_exp_up,
           w_exp_down, w_sh_gate, w_sh_up, w_sh_down, g_ple, w_ple_gate, w_ple_proj, g_ple_post):
    batch, seq, d = x.shape
    depth = p.shape[0]
    x2d = x.reshape(batch * seq, d)
    for i in range(depth):
        x2d = _layer(x2d, p[i].reshape(batch * seq, PLE_DIM), batch, seq, g_mix[i], w_in[i],
                     w_pool_mix[i], pool_scale[i], w_branch_a[i], q_gain[i], k_gain[i], attn_sinks[i],
                     w_branch_b[i], w_gate[i], b_gate[i], w_out[i], g_ffn[i], w_router[i],
                     router_bias[i], w_exp_gate[i], w_exp_up[i], w_exp_down[i], w_sh_gate[i],
                     w_sh_up[i], w_sh_down[i], g_ple[i], w_ple_gate[i], w_ple_proj[i], g_ple_post[i])
    return x2d.reshape(batch, seq, d)
```

```python
import numpy as np
import jax
import jax.numpy as jnp
from jax import lax
from jax.experimental import pallas as pl
from jax.experimental.pallas import tpu as pltpu
from jax.experimental.pallas import tpu_sc as plsc

F32 = jnp.float32
BF16 = jnp.bfloat16
I32 = jnp.int32

D_MODEL = 1024
PLE_DIM = 256
POOL_WINDOWS = (2, 4, 8, 16)
POOL_GROUP_DIM = 128
POOL_DIM = 512
N_HEADS = 8
N_KV_HEADS = 2
Q_GROUP = 4
HEAD_DIM = 64
HEAD_PAD = 128
Q_DIM = 512
KV_DIM = 128
ATTN_BLOCK = 128
N_EXPERTS = 64
N_EXPERT_GROUPS = 8
GROUP_SIZE = 8
TOPK_GROUPS = 4
TOP_K = 8
EXPERT_DIM = 256
ROUTED_SCALE = 2.5
MOE_BLOCK = 512
EXPERT_SLOTS = 8
EPS = 1e-6
HALF = D_MODEL // 2
MASKED = -1e30

COL_Q = POOL_DIM
COL_K = COL_Q + N_HEADS * HEAD_PAD
COL_V = COL_K + N_KV_HEADS * HEAD_PAD
IN_PAD = COL_V + N_KV_HEADS * HEAD_PAD
HEAD_ORDER = (0, 2, 1, 3)

TM_MIX = 512
TM_FFN = 2048
TM_COMBINE = 1024
TM_POS = 4096
COMBINE_CHUNKS = 2
PIPELINE_HALVES = 2
SC_SCATTER_ROWS = 128
SC_SUM_ROWS = 8
SC_LANES = 16
POOL_TAIL = 8
VMEM_LIMIT = 56 * 1024 * 1024

_NT = (((1,), (1,)), ((), ()))


def _dot(a, b):
    return jnp.dot(a, b, preferred_element_type=F32)


def _rms(x, g):
    ms = jnp.mean(x * x, axis=-1, keepdims=True)
    return x * lax.rsqrt(ms + EPS) * g


def _sigmoid(x):
    return 0.5 * jnp.tanh(0.5 * x) + 0.5


def _pack(lo, hi):
    lo_bits = lax.bitcast_convert_type(lo.astype(BF16).astype(F32), I32)
    hi_bits = lax.bitcast_convert_type(hi.astype(BF16).astype(F32), I32)
    return (hi_bits & jnp.int32(-65536)) | lax.shift_right_logical(lo_bits, jnp.int32(16))


def _unpack(w):
    lo = lax.bitcast_convert_type(lax.shift_left(w, jnp.int32(16)), F32)
    hi = lax.bitcast_convert_type(w & jnp.int32(-65536), F32)
    return lo, hi


def _mixer_kernel(x_ref, gmix_ref, win_ref, qg_ref, kg_ref, bias_ref, sink_ref, ones_ref, wmix_ref,
                  pscale_ref, wa_ref, wb_ref, wg_ref, bg_ref, wout_ref,
                  gffn_ref, wrh_ref, wrl_ref, rb_ref, tri_ref, *rest):
    (x1_ref, h2p_ref, idx_ref, wt_ref, rk_ref, cnt_ref,
     kbuf, vt_buf, pbuf, zbt_buf, run_ref) = rest[-11:]
    s = pl.program_id(1)
    tm = x_ref.shape[0]

    @pl.when(s == 0)
    def _():
        kbuf[0:ATTN_BLOCK, :] = jnp.zeros((ATTN_BLOCK, kbuf.shape[1]), BF16)
        vt_buf[:, 0:ATTN_BLOCK] = jnp.zeros((vt_buf.shape[0], ATTN_BLOCK), BF16)
        pbuf[:, 0:POOL_TAIL, :] = jnp.zeros((pbuf.shape[0], POOL_TAIL, POOL_DIM), F32)

    x = x_ref[...]
    hb = _rms(x, gmix_ref[...]).astype(BF16)
    u = _dot(hb, win_ref[...])

    up = u[:, 0:POOL_DIM]
    t0 = POOL_TAIL
    pbuf[0, t0:t0 + tm, :] = up
    s2 = up + pbuf[0, t0 - 1:t0 - 1 + tm, :]
    pbuf[1, t0:t0 + tm, 128:512] = s2[:, 128:512]
    s4 = s2[:, 128:512] + pbuf[1, t0 - 2:t0 - 2 + tm, 128:512]
    pbuf[2, t0:t0 + tm, 256:512] = s4[:, 128:384]
    s8 = s4[:, 128:384] + pbuf[2, t0 - 4:t0 - 4 + tm, 256:512]
    pbuf[3, t0:t0 + tm, 384:512] = s8[:, 128:256]
    s16 = s8[:, 128:256] + pbuf[3, t0 - 8:t0 - 8 + tm, 384:512]
    for lvl in range(4):
        pbuf[lvl, 0:POOL_TAIL, 128 * lvl:512] = pbuf[lvl, tm:tm + POOL_TAIL, 128 * lvl:512]
    wsums = (s2[:, 0:128], s4[:, 0:128], s8[:, 0:128], s16)
    tpos = (s * tm).astype(F32) + lax.broadcasted_iota(I32, (tm, 1), 0).astype(F32)
    za_parts = []
    for g, w in enumerate(POOL_WINDOWS):
        inv_count = 1.0 / jnp.minimum(tpos + 1.0, float(w))
        pooled = wsums[g] * inv_count - up[:, 128 * g:128 * (g + 1)]
        mixed = _dot(pooled.astype(BF16), wmix_ref[g]) * pscale_ref[:, 128 * g:128 * (g + 1)]
        za_parts.append(mixed.astype(BF16))
    za = jnp.concatenate(za_parts, axis=1)

    ones2 = ones_ref[...]
    q = u[:, COL_Q:COL_K]
    q2 = (q * q).astype(BF16)
    qss = jnp.concatenate([_dot(q2[:, 256 * c:256 * (c + 1)], ones2) for c in range(N_HEADS // 2)], axis=1)
    qn = (q * lax.rsqrt(qss * (1.0 / HEAD_DIM) + EPS) * qg_ref[...]).astype(BF16)
    k = u[:, COL_K:COL_V]
    kss = _dot((k * k).astype(BF16), ones2)
    kbuf[ATTN_BLOCK:ATTN_BLOCK + tm, :] = (k * lax.rsqrt(kss * (1.0 / HEAD_DIM) + EPS) * kg_ref[...]).astype(BF16)
    vt_buf[:, ATTN_BLOCK:ATTN_BLOCK + tm] = u[:, COL_V:IN_PAD].T.astype(BF16)

    nq = Q_GROUP * ATTN_BLOCK
    key_j = lax.broadcasted_iota(I32, (ATTN_BLOCK, nq), 0)
    qry_i = lax.broadcasted_iota(I32, (ATTN_BLOCK, nq), 1) & (ATTN_BLOCK - 1)
    from_prev = key_j > qry_i
    first = jnp.where(s == 0, 1, 0)
    for n in range(tm // ATTN_BLOCK):
        r0 = ATTN_BLOCK * n
        for kv in range(N_KV_HEADS):
            c0 = HEAD_PAD * kv
            qs = jnp.concatenate(
                [qn[r0:r0 + ATTN_BLOCK, HEAD_PAD * (kv * Q_GROUP + g):HEAD_PAD * (kv * Q_GROUP + g + 1)]
                 for g in HEAD_ORDER], axis=0)
            kk = kbuf[r0:r0 + 2 * ATTN_BLOCK, c0:c0 + HEAD_PAD]
            st = lax.dot_general(kk, qs, _NT, preferred_element_type=F32)
            sc = jnp.where(from_prev, st[0:ATTN_BLOCK], st[ATTN_BLOCK:2 * ATTN_BLOCK])
            sc = sc + (bias_ref[first, kv] if n == 0 else bias_ref[0, kv])
            sink = sink_ref[kv]
            m = jnp.maximum(jnp.max(sc, axis=0, keepdims=True), sink)
            e = jnp.exp(sc - m)
            inv_den = 1.0 / (jnp.sum(e, axis=0, keepdims=True) + jnp.exp(sink - m))
            pt = jnp.concatenate([jnp.where(from_prev, e, 0.0), jnp.where(from_prev, 0.0, e)],
                                 axis=0).astype(BF16)
            v_even = vt_buf[c0:c0 + HEAD_PAD, r0:r0 + 2 * ATTN_BLOCK]
            v_odd = jnp.concatenate([v_even[HEAD_DIM:HEAD_PAD], v_even[0:HEAD_DIM]], axis=0)
            half = 2 * ATTN_BLOCK
            o = (_dot(v_even, pt[:, 0:half]) * inv_den[:, 0:half]
                 + _dot(v_odd, pt[:, half:2 * half]) * inv_den[:, half:2 * half])
            ch = kv * Q_GROUP * HEAD_DIM
            zbt_buf[ch:ch + HEAD_PAD, r0:r0 + ATTN_BLOCK] = o[:, 0:ATTN_BLOCK]
            zbt_buf[ch + HEAD_PAD:ch + 2 * HEAD_PAD, r0:r0 + ATTN_BLOCK] = o[:, ATTN_BLOCK:half]
    kbuf[0:ATTN_BLOCK, :] = kbuf[tm:tm + ATTN_BLOCK, :]
    vt_buf[:, 0:ATTN_BLOCK] = vt_buf[:, tm:tm + ATTN_BLOCK]

    y_a = _dot(za, wa_ref[...])
    y_b = _dot(zbt_buf[...].T.astype(BF16), wb_ref[...])
    g_a = _sigmoid(_dot(hb, wg_ref[:, 0:D_MODEL]) + bg_ref[:, 0:D_MODEL])
    merged = g_a * y_a
    g_b = _sigmoid(_dot(hb, wg_ref[:, D_MODEL:2 * D_MODEL]) + bg_ref[:, D_MODEL:2 * D_MODEL])
    merged = (merged + g_b * y_b).astype(BF16)
    x1 = x + _dot(merged, wout_ref[...])
    x1_ref[...] = x1

    is_first = (pl.program_id(0) == 0) & (s == 0)
    _route_tile(x1, is_first, gffn_ref, wrh_ref, wrl_ref, rb_ref, tri_ref,
                h2p_ref, idx_ref, wt_ref, rk_ref, cnt_ref, run_ref)


def _attn_tables(attn_sinks):
    slopes = 2.0 ** (-8.0 * (np.arange(N_HEADS) + 1) / N_HEADS)
    j = np.arange(ATTN_BLOCK)[:, None]
    i = np.arange(ATTN_BLOCK)[None, :]
    from_prev = j > i
    dist = np.where(from_prev, ATTN_BLOCK + i - j, i - j)
    bias = np.empty((2, N_KV_HEADS, ATTN_BLOCK, Q_GROUP * ATTN_BLOCK), np.float32)
    for first in range(2):
        ok = ~from_prev if first else np.ones_like(from_prev)
        for kv in range(N_KV_HEADS):
            for slot, g in enumerate(HEAD_ORDER):
                sl = np.float32(slopes[kv * Q_GROUP + g])
                val = -(sl * dist.astype(np.float32))
                bias[first, kv, :, slot * ATTN_BLOCK:(slot + 1) * ATTN_BLOCK] = np.where(ok, val, MASKED)
    sink = attn_sinks.astype(F32).reshape(N_KV_HEADS, Q_GROUP)[:, np.asarray(HEAD_ORDER)]
    sink = jnp.repeat(sink, ATTN_BLOCK, axis=1)
    return jnp.asarray(bias), sink.reshape(N_KV_HEADS, 1, Q_GROUP * ATTN_BLOCK)


def _pad_heads(w, n_heads):
    k = w.shape[0]
    w = w.reshape(k, n_heads, HEAD_DIM)
    w = jnp.pad(w, ((0, 0), (0, 0), (0, HEAD_PAD - HEAD_DIM)))
    return w.reshape(k, n_heads * HEAD_PAD)


def _const_spec(shape):
    nd = len(shape)
    return pl.BlockSpec(shape, lambda *_: (0,) * nd)


def _mixer(x2d, batch0, batch, seq, g_mix, w_in, w_pool_mix, pool_scale, w_branch_a, q_gain, k_gain,
           attn_sinks, w_branch_b, w_gate, b_gate, w_out, g_ffn, w_router, router_bias, after=None):
    t = batch * seq
    tm = TM_MIX
    ns = seq // tm
    w_q = w_in[:, POOL_DIM:POOL_DIM + Q_DIM]
    w_k = w_in[:, POOL_DIM + Q_DIM:POOL_DIM + Q_DIM + KV_DIM]
    w_v = w_in[:, POOL_DIM + Q_DIM + KV_DIM:]
    win_p = jnp.concatenate([w_in[:, :POOL_DIM], _pad_heads(w_q, N_HEADS), _pad_heads(w_k, N_KV_HEADS),
                             _pad_heads(w_v, N_KV_HEADS)], axis=1).astype(BF16)
    qg = jnp.tile(jnp.pad(q_gain * (HEAD_DIM ** -0.5), (0, HEAD_PAD - HEAD_DIM)), N_HEADS).reshape(1, -1)
    kg = jnp.tile(jnp.pad(k_gain, (0, HEAD_PAD - HEAD_DIM)), N_KV_HEADS).reshape(1, -1)
    bias, sink = _attn_tables(attn_sinks)
    ones2 = jnp.asarray(np.kron(np.eye(2), np.ones((HEAD_PAD, HEAD_PAD))), BF16)
    operands = (x2d, g_mix.reshape(1, D_MODEL), win_p, qg, kg, bias, sink, ones2, w_pool_mix.astype(BF16),
                pool_scale.reshape(1, POOL_DIM), w_branch_a.astype(BF16), w_branch_b.astype(BF16),
                w_gate.astype(BF16), b_gate.reshape(1, 2 * D_MODEL), w_out.astype(BF16))
    operands += _router_operands(g_ffn, w_router, router_bias, tm)
    in_specs = [pl.BlockSpec((tm, D_MODEL), lambda b, s: ((batch0 + b) * ns + s, 0))]
    in_specs += [_const_spec(op.shape) for op in operands[1:]]
    if after is not None:
        operands += (after,)
        in_specs.append(pl.BlockSpec(memory_space=pl.ANY))
    row8 = pl.BlockSpec((TOP_K, tm), lambda b, s: (0, b * ns + s))
    return pl.pallas_call(
        _mixer_kernel,
        grid=(batch, ns),
        in_specs=in_specs,
        out_specs=[pl.BlockSpec((tm, D_MODEL), lambda b, s: (b * ns + s, 0)),
                   pl.BlockSpec((tm, HALF), lambda b, s: (b * ns + s, 0)),
                   row8,
                   pl.BlockSpec((tm, TOP_K * SC_LANES), lambda b, s: (b * ns + s, 0)),
                   row8,
                   pl.BlockSpec((N_EXPERTS, 128), lambda b, s: (0, 0))],
        out_shape=[jax.ShapeDtypeStruct((t, D_MODEL), F32),
                   jax.ShapeDtypeStruct((t, HALF), I32),
                   jax.ShapeDtypeStruct((TOP_K, t), I32),
                   jax.ShapeDtypeStruct((t, TOP_K * SC_LANES), F32),
                   jax.ShapeDtypeStruct((TOP_K, t), I32),
                   jax.ShapeDtypeStruct((N_EXPERTS, 128), F32)],
        scratch_shapes=[
            pltpu.VMEM((ATTN_BLOCK + tm, N_KV_HEADS * HEAD_PAD), BF16),
            pltpu.VMEM((N_KV_HEADS * HEAD_PAD, ATTN_BLOCK + tm), BF16),
            pltpu.VMEM((4, POOL_TAIL + tm, POOL_DIM), F32),
            pltpu.VMEM((Q_DIM, tm), F32),
            pltpu.VMEM((N_EXPERTS, 128), F32),
        ],
        compiler_params=pltpu.CompilerParams(
            dimension_semantics=("arbitrary", "arbitrary"), vmem_limit_bytes=VMEM_LIMIT),
        name="mixer",
    )(*operands)


def _route_tile(x1, is_first, gffn_ref, wrh_ref, wrl_ref, rb_ref, tri_ref,
                h2p_ref, idx_ref, wt_ref, rk_ref, cnt_ref, run_ref):
    tm = x1.shape[0]

    @pl.when(is_first)
    def _():
        run_ref[...] = jnp.zeros_like(run_ref)

    h2 = _rms(x1, gffn_ref[...])
    hb = h2.astype(BF16)
    h2p_ref[...] = _pack(h2[:, 0:HALF], h2[:, HALF:D_MODEL])

    lo = (h2 - hb.astype(F32)).astype(BF16)
    wrh = wrh_ref[...]
    logits = (lax.dot_general(wrh, hb, _NT, preferred_element_type=F32)
              + lax.dot_general(wrl_ref[...], hb, _NT, preferred_element_type=F32)
              + lax.dot_general(wrh, lo, _NT, preferred_element_type=F32))
    scores = _sigmoid(logits)
    choice = scores + rb_ref[...]

    neg = -jnp.inf
    sub8 = lax.broadcasted_iota(I32, (GROUP_SIZE, tm), 0).astype(F32)
    grp_rows = []
    for gi in range(N_EXPERT_GROUPS):
        cg = choice[GROUP_SIZE * gi:GROUP_SIZE * (gi + 1)]
        m1 = jnp.max(cg, axis=0, keepdims=True)
        first = jnp.min(jnp.where(cg == m1, sub8, float(GROUP_SIZE)), axis=0, keepdims=True)
        m2 = jnp.max(jnp.where(sub8 == first, neg, cg), axis=0, keepdims=True)
        grp_rows.append(m1 + m2)
    gs = jnp.concatenate(grp_rows, axis=0)
    beaten = jnp.zeros((N_EXPERT_GROUPS, tm), F32)
    for gi in range(N_EXPERT_GROUPS):
        row = grp_rows[gi]
        wins = (row > gs) | ((row == gs) & (sub8 > float(gi)))
        beaten = beaten + jnp.where(wins, 1.0, 0.0)
    gsel = beaten < float(TOPK_GROUPS)
    cm = jnp.concatenate(
        [jnp.where(gsel[gi:gi + 1], choice[GROUP_SIZE * gi:GROUP_SIZE * (gi + 1)], neg)
         for gi in range(N_EXPERT_GROUPS)], axis=0)

    iota_e = lax.broadcasted_iota(I32, (N_EXPERTS, tm), 0).astype(F32)
    idx_rows, s_rows = [], []
    sel = jnp.zeros((N_EXPERTS, tm), F32)
    for _ in range(TOP_K):
        m = jnp.max(cm, axis=0, keepdims=True)
        idx = jnp.min(jnp.where(cm == m, iota_e, float(N_EXPERTS)), axis=0, keepdims=True)
        oh = iota_e == idx
        s_rows.append(jnp.sum(jnp.where(oh, scores, 0.0), axis=0, keepdims=True))
        idx_rows.append(idx)
        cm = jnp.where(oh, neg, cm)
        sel = sel + jnp.where(oh, 1.0, 0.0)

    run = run_ref[:, 0:1]
    cum = _dot(sel.astype(BF16), tri_ref[...])
    before = run + cum - sel
    rk_rows = [jnp.sum(jnp.where(iota_e == idx, before, 0.0), axis=0, keepdims=True) for idx in idx_rows]
    new_run = run + jnp.sum(sel, axis=1, keepdims=True)
    run_ref[...] = jnp.broadcast_to(new_run, run_ref.shape)
    cnt_ref[...] = jnp.broadcast_to(new_run, cnt_ref.shape)

    ssum = s_rows[0]
    for r in range(1, TOP_K):
        ssum = ssum + s_rows[r]
    denom = ssum + 1e-20
    idx_ref[...] = jnp.concatenate(idx_rows, axis=0).astype(I32)
    w_rep = jnp.concatenate([jnp.broadcast_to(sr / denom * ROUTED_SCALE, (SC_LANES, tm)) for sr in s_rows],
                            axis=0)
    wt_ref[...] = w_rep.T
    rk_ref[...] = jnp.concatenate(rk_rows, axis=0).astype(I32)


def _router_operands(g_ffn, w_router, router_bias, tm):
    wr_t = w_router.astype(F32).T
    wr_hi = wr_t.astype(BF16)
    wr_lo = (wr_t - wr_hi.astype(F32)).astype(BF16)
    tri = (np.arange(tm)[:, None] <= np.arange(tm)[None, :]).astype(np.float32)
    return (g_ffn.reshape(1, D_MODEL), wr_hi, wr_lo, router_bias.astype(F32).reshape(N_EXPERTS, 1),
            jnp.asarray(tri, BF16))


def _swiglu_packed(x_packed, wg, wu, wd):
    lo, hi = _unpack(x_packed)
    lo = lo.astype(BF16)
    hi = hi.astype(BF16)
    g = _dot(lo, wg[0:HALF, :]) + _dot(hi, wg[HALF:D_MODEL, :])
    u = _dot(lo, wu[0:HALF, :]) + _dot(hi, wu[HALF:D_MODEL, :])
    a = (g * _sigmoid(g) * u).astype(BF16)
    y = _dot(a, wd[...])
    return _pack(y[:, 0:HALF], y[:, HALF:D_MODEL])


def _shared_kernel(h2p_ref, wg_ref, wu_ref, wd_ref, *rest):
    ysh_ref = rest[-1]
    ysh_ref[...] = _swiglu_packed(h2p_ref[...], wg_ref, wu_ref, wd_ref)


def _shared(h2p, w_sh_gate, w_sh_up, w_sh_down, after=None):
    t = h2p.shape[0]
    tm = TM_FFN
    operands = (h2p, w_sh_gate.astype(BF16), w_sh_up.astype(BF16), w_sh_down.astype(BF16))
    in_specs = [pl.BlockSpec((tm, HALF), lambda i: (i, 0))]
    in_specs += [_const_spec(op.shape) for op in operands[1:]]
    if after is not None:
        operands += (after,)
        in_specs.append(pl.BlockSpec(memory_space=pl.ANY))
    return pl.pallas_call(
        _shared_kernel,
        grid=(t // tm,),
        in_specs=in_specs,
        out_specs=pl.BlockSpec((tm, HALF), lambda i: (i, 0)),
        out_shape=jax.ShapeDtypeStruct((t, HALF), I32),
        compiler_params=pltpu.CompilerParams(
            dimension_semantics=("arbitrary",), vmem_limit_bytes=VMEM_LIMIT),
        name="shared",
    )(*operands)


def _positions_kernel(pstart_ref, idx_ref, rk_ref, pos_ref):
    idx = idx_ref[...]
    acc = rk_ref[...]
    for e in range(N_EXPERTS):
        acc = acc + jnp.where(idx == e, pstart_ref[e], 0)
    pos_ref[...] = acc


def _positions(pstart, idx8, rk8):
    t = idx8.shape[1]
    tm = TM_POS
    spec = pl.BlockSpec((TOP_K, tm), lambda i, ps: (0, i))
    return pl.pallas_call(
        _positions_kernel,
        grid_spec=pltpu.PrefetchScalarGridSpec(num_scalar_prefetch=1, grid=(t // tm,),
                                               in_specs=[spec, spec], out_specs=spec),
        out_shape=jax.ShapeDtypeStruct((TOP_K, t), I32),
        name="positions",
    )(pstart, idx8, rk8)


def _sc_workers():
    info = plsc.get_sparse_core_info()
    return info.num_cores, info.num_cores * info.num_subcores


def _dispatch(pos_flat, h2p, n_rows):
    t = h2p.shape[0]
    ch = SC_SCATTER_ROWS
    n_cores, n_workers = _sc_workers()
    n_ch = t // (n_workers * ch)

    def body(pos_hbm, h_hbm, xs_hbm, idx_v, rows_v, sem):
        wid = lax.axis_index("s") * n_cores + lax.axis_index("c")

        @pl.loop(0, n_ch)
        def _(c):
            chunk = wid * n_ch + c
            pltpu.sync_copy(h_hbm.at[pl.ds(chunk * ch, ch)], rows_v)
            for k in range(TOP_K):
                pltpu.sync_copy(pos_hbm.at[pl.ds(k * t + chunk * ch, ch)], idx_v.at[k])
            copies = [pltpu.async_copy(rows_v, xs_hbm.at[idx_v.at[k]], sem) for k in range(TOP_K)]
            for cp in copies:
                cp.wait()

    return pl.kernel(
        body,
        out_type=jax.ShapeDtypeStruct((n_rows, HALF), I32),
        mesh=plsc.VectorSubcoreMesh(core_axis_name="c", subcore_axis_name="s"),
        scratch_types=[pltpu.VMEM((TOP_K, ch), I32), pltpu.VMEM((ch, HALF), I32),
                       pltpu.SemaphoreType.DMA],
        name="dispatch",
    )(pos_flat, h2p)


def _pairwise_sum(xs):
    while len(xs) > 1:
        xs = [xs[i] + xs[i + 1] for i in range(0, len(xs), 2)]
    return xs[0]


def _gather_sum(pos_flat, w_rep, ys, tok0, t):
    t_all = w_rep.shape[0]
    ch = SC_SUM_ROWS
    lanes = SC_LANES
    n_cores, n_workers = _sc_workers()
    per_w = t // n_workers
    n_ch = per_w // ch

    def body(pos_hbm, w_hbm, ys_hbm, out_hbm, idx_all, w_all, buf0, buf1, acc0, acc1, gsem0, gsem1, osem0, osem1):
        wid = lax.axis_index("s") * n_cores + lax.axis_index("c")
        base = wid * n_ch
        first = tok0 + wid * per_w
        bufs, accs, gsems, osems = (buf0, buf1), (acc0, acc1), (gsem0, gsem1), (osem0, osem1)
        for k in range(TOP_K):
            pltpu.sync_copy(pos_hbm.at[pl.ds(k * t_all + first, per_w)], idx_all.at[pl.ds(k * per_w, per_w)])
        pltpu.sync_copy(w_hbm.at[pl.ds(first, per_w)], w_all)

        def row_copy(c, s, k):
            rows = idx_all.at[pl.ds(k * per_w + c * ch, ch)]
            return pltpu.make_async_copy(ys_hbm.at[rows], bufs[s].at[k], gsems[s])

        def out_copy(c, s):
            return pltpu.make_async_copy(accs[s], out_hbm.at[pl.ds((base + c) * ch, ch)], osems[s])

        def fetch(c, s):
            for k in range(TOP_K):
                row_copy(c, s, k).start()

        def reduce_rows(c, s):
            @pl.loop(0, ch)
            def _(r):
                wks = [w_all[c * ch + r, pl.ds(k * lanes, lanes)] for k in range(TOP_K)]

                @pl.loop(0, HALF // lanes, step=2)
                def _(v):
                    for dv in range(2):
                        col = (v + dv) * lanes
                        los, his = [], []
                        for k in range(TOP_K):
                            words = bufs[s][k, r, pl.ds(col, lanes)]
                            los.append(wks[k] * lax.bitcast_convert_type(lax.shift_left(words, jnp.int32(16)), F32))
                            his.append(wks[k] * lax.bitcast_convert_type(words & jnp.int32(-65536), F32))
                        accs[s][r, pl.ds(col, lanes)] = _pairwise_sum(los)
                        accs[s][r, pl.ds(HALF + col, lanes)] = _pairwise_sum(his)

        fetch(0, 0)

        @pl.loop(0, n_ch, step=2)
        def _(c):
            for s in range(2):
                cc = c + s

                @pl.when(cc + 1 < n_ch)
                def _():
                    fetch(cc + 1, 1 - s)

                for k in range(TOP_K):
                    row_copy(cc, s, k).wait()

                @pl.when(cc >= 2)
                def _():
                    out_copy(cc - 2, s).wait()

                reduce_rows(cc, s)
                out_copy(cc, s).start()

        out_copy(n_ch - 2, 0).wait()
        out_copy(n_ch - 1, 1).wait()

    return pl.kernel(
        body,
        out_type=jax.ShapeDtypeStruct((t, D_MODEL), F32),
        mesh=plsc.VectorSubcoreMesh(core_axis_name="c", subcore_axis_name="s"),
        scratch_types=[pltpu.VMEM((TOP_K * per_w,), I32), pltpu.VMEM((per_w, TOP_K * lanes), F32),
                       pltpu.VMEM((TOP_K, ch, HALF), I32), pltpu.VMEM((TOP_K, ch, HALF), I32),
                       pltpu.VMEM((ch, D_MODEL), F32), pltpu.VMEM((ch, D_MODEL), F32),
                       pltpu.SemaphoreType.DMA, pltpu.SemaphoreType.DMA,
                       pltpu.SemaphoreType.DMA, pltpu.SemaphoreType.DMA],
        compiler_params=pltpu.CompilerParams(needs_layout_passes=False),
        name="gather_sum",
    )(pos_flat, w_rep, ys)


def _expert_kernel(nblk_ref, blk0_ref, cnt_ref, wg_ref, wu_ref, wd_ref, xs_hbm, after_ref, ys_hbm,
                   wg_s, wu_s, wd_s, xbuf, ybuf, in_sem, out_sem):
    del after_ref
    e = pl.program_id(0)
    nb = nblk_ref[e]
    b0 = blk0_ref[e]
    cnt = cnt_ref[e]
    n_all = blk0_ref[N_EXPERTS]
    ahead = EXPERT_SLOTS - 2
    wg_s[...] = wg_ref[0].astype(BF16)
    wu_s[...] = wu_ref[0].astype(BF16)
    wd_s[...] = wd_ref[0].astype(BF16)

    def slot_rows(g, n_blocks=1):
        start = (g & (EXPERT_SLOTS - 1)) * MOE_BLOCK
        return pl.ds(pl.multiple_of(start, MOE_BLOCK), n_blocks * MOE_BLOCK)

    def rows_of(g):
        return pl.ds(pl.multiple_of(g * MOE_BLOCK, MOE_BLOCK), MOE_BLOCK)

    def in_copy(g):
        return pltpu.make_async_copy(xs_hbm.at[rows_of(g)], xbuf.at[slot_rows(g)],
                                     in_sem.at[g & (EXPERT_SLOTS - 1)])

    def out_copy(g):
        return pltpu.make_async_copy(ybuf.at[slot_rows(g)], ys_hbm.at[rows_of(g)],
                                     out_sem.at[g & (EXPERT_SLOTS - 1)])

    @pl.when(e == 0)
    def _():
        for g in range(ahead):
            @pl.when(g < n_all)
            def _():
                in_copy(g).start()

    def arrive(g):
        in_copy(g).wait()

        @pl.when(g + ahead < n_all)
        def _():
            in_copy(g + ahead).start()

        @pl.when(g >= EXPERT_SLOTS)
        def _():
            out_copy(g - EXPERT_SLOTS).wait()

    def run_blocks(j, n_blocks):
        g = b0 + j
        for i in range(n_blocks):
            arrive(g + i)
        rows = lax.broadcasted_iota(I32, (n_blocks * MOE_BLOCK, HALF), 0)
        x = jnp.where(rows < cnt - j * MOE_BLOCK, xbuf[slot_rows(g, n_blocks)], 0)
        ybuf[slot_rows(g, n_blocks)] = _swiglu_packed(x, wg_s, wu_s, wd_s)
        for i in range(n_blocks):
            out_copy(g + i).start()

    lead = jnp.where(nb > 0, b0 & 1, 0)

    @pl.when(lead == 1)
    def _():
        run_blocks(0, 1)

    n_pairs = (nb - lead) // 2

    def one_pair(p, carry):
        run_blocks(lead + 2 * p, 2)
        return carry

    lax.fori_loop(0, n_pairs, one_pair, 0)

    @pl.when(nb - lead - 2 * n_pairs == 1)
    def _():
        run_blocks(nb - 1, 1)

    @pl.when(e == N_EXPERTS - 1)
    def _():
        for back in range(1, EXPERT_SLOTS + 1):
            @pl.when(n_all >= back)
            def _():
                out_copy(n_all - back).wait()


def _experts(n_blk, blk0, counts, n_rows, xs, w_gate, w_up, w_down, after):
    wspec_in = pl.BlockSpec((1, D_MODEL, EXPERT_DIM), lambda e, *_: (e, 0, 0))
    grid_spec = pltpu.PrefetchScalarGridSpec(
        num_scalar_prefetch=3,
        grid=(N_EXPERTS,),
        in_specs=[wspec_in, wspec_in,
                  pl.BlockSpec((1, EXPERT_DIM, D_MODEL), lambda e, *_: (e, 0, 0)),
                  pl.BlockSpec(memory_space=pl.ANY),
                  pl.BlockSpec(memory_space=pl.ANY)],
        out_specs=pl.BlockSpec(memory_space=pl.ANY),
        scratch_shapes=[pltpu.VMEM((D_MODEL, EXPERT_DIM), BF16), pltpu.VMEM((D_MODEL, EXPERT_DIM), BF16),
                        pltpu.VMEM((EXPERT_DIM, D_MODEL), BF16),
                        pltpu.VMEM((EXPERT_SLOTS * MOE_BLOCK, HALF), I32),
                        pltpu.VMEM((EXPERT_SLOTS * MOE_BLOCK, HALF), I32),
                        pltpu.SemaphoreType.DMA((EXPERT_SLOTS,)), pltpu.SemaphoreType.DMA((EXPERT_SLOTS,))],
    )
    return pl.pallas_call(
        _expert_kernel,
        grid_spec=grid_spec,
        out_shape=jax.ShapeDtypeStruct((n_rows, HALF), I32),
        compiler_params=pltpu.CompilerParams(
            dimension_semantics=("arbitrary",), vmem_limit_bytes=VMEM_LIMIT),
        name="experts",
    )(n_blk, blk0, counts, w_gate, w_up, w_down, xs, after)


def _combine_kernel(x1_ref, ysh_ref, moe_ref, p_ref, gple_ref, wpg_ref, wpp_ref, gpost_ref, *rest):
    out_ref = rest[-1]
    proj = _rms(_dot(p_ref[...].astype(BF16), wpp_ref[...]), gpost_ref[...])
    sh_lo, sh_hi = _unpack(ysh_ref[...])
    x2 = x1_ref[...] + moe_ref[...] + jnp.concatenate([sh_lo, sh_hi], axis=1)
    gate = _sigmoid(_dot(_rms(x2, gple_ref[...]).astype(BF16), wpg_ref[...]))
    out_ref[...] = x2 + gate * proj


def _combine(x1, ysh, moe, p2d, g_ple, w_ple_gate, w_ple_proj, g_ple_post, tok0, tok0_all, prev_out):
    t = p2d.shape[0]
    tm = TM_COMBINE
    steps = moe.shape[0] // tm
    off = tok0 // tm
    off_all = tok0_all // tm
    operands = [x1, ysh, moe, p2d, g_ple.reshape(1, D_MODEL), w_ple_gate.astype(BF16),
                w_ple_proj.astype(BF16), g_ple_post.reshape(1, D_MODEL)]
    in_specs = [pl.BlockSpec((tm, D_MODEL), lambda i: (off + i, 0)),
                pl.BlockSpec((tm, HALF), lambda i: (off + i, 0)),
                pl.BlockSpec((tm, D_MODEL), lambda i: (i, 0)),
                pl.BlockSpec((tm, PLE_DIM), lambda i: (off_all + i, 0)),
                _const_spec((1, D_MODEL)),
                _const_spec((D_MODEL, D_MODEL)),
                _const_spec((PLE_DIM, D_MODEL)),
                _const_spec((1, D_MODEL))]
    aliases = {}
    if prev_out is not None:
        aliases = {len(operands): 0}
        operands.append(prev_out)
        in_specs.append(pl.BlockSpec(memory_space=pl.ANY))
    return pl.pallas_call(
        _combine_kernel,
        grid=(steps,),
        in_specs=in_specs,
        out_specs=pl.BlockSpec((tm, D_MODEL), lambda i: (off_all + i, 0)),
        out_shape=jax.ShapeDtypeStruct((t, D_MODEL), F32),
        input_output_aliases=aliases,
        compiler_params=pltpu.CompilerParams(
            dimension_semantics=("arbitrary",), vmem_limit_bytes=VMEM_LIMIT),
        name="combine",
    )(*operands)


def _layer(x2d, p2d, batch, seq, g_mix, w_in, w_pool_mix, pool_scale, w_branch_a, q_gain, k_gain,
           attn_sinks, w_branch_b, w_gate, b_gate, w_out, g_ffn, w_router, router_bias, w_exp_gate,
           w_exp_up, w_exp_down, w_sh_gate, w_sh_up, w_sh_down, g_ple, w_ple_gate, w_ple_proj,
           g_ple_post):
    bh = batch // PIPELINE_HALVES
    th = bh * seq
    n_rows = (-(-(th * TOP_K) // MOE_BLOCK) + N_EXPERTS) * MOE_BLOCK

    halves = []
    after = None
    for h in range(PIPELINE_HALVES):
        x1, h2p, idx8, w_rep, rk8, cnt = _mixer(
            x2d, h * bh, bh, seq, g_mix, w_in, w_pool_mix, pool_scale, w_branch_a, q_gain, k_gain,
            attn_sinks, w_branch_b, w_gate, b_gate, w_out, g_ffn, w_router, router_bias, after=after)
        counts = cnt[:, 0].astype(I32)
        padded = (counts + MOE_BLOCK - 1) // MOE_BLOCK * MOE_BLOCK
        pend = jnp.cumsum(padded)
        pstart = pend - padded
        pos = _positions(pstart.astype(I32), idx8, rk8).reshape(-1)
        xs = _dispatch(pos, h2p, n_rows)
        blk0 = jnp.concatenate([pstart, pend[-1:]]) // MOE_BLOCK
        halves.append((x1, h2p, w_rep, pos, xs, (padded // MOE_BLOCK).astype(I32), blk0.astype(I32), counts))
        after = pos

    ys_all = []
    for (x1, h2p, w_rep, pos, xs, n_blk, blk0, counts) in halves:
        ys_all.append(_experts(n_blk, blk0, counts, n_rows, xs, w_exp_gate, w_exp_up, w_exp_down,
                               ys_all[-1] if ys_all else xs))

    out = None
    tc = th // COMBINE_CHUNKS
    for h, (x1, h2p, w_rep, pos, xs, n_blk, blk0, counts) in enumerate(halves):
        ys = ys_all[h]
        ysh = _shared(h2p, w_sh_gate, w_sh_up, w_sh_down, after=ys_all[-1] if out is None else out)
        for c in range(COMBINE_CHUNKS):
            moe = _gather_sum(pos, w_rep, ys, c * tc, tc)
            out = _combine(x1, ysh, moe, p2d, g_ple, w_ple_gate, w_ple_proj, g_ple_post,
                           c * tc, h * th + c * tc, out)
    return out


def kernel(x, p, g_mix, w_in, w_pool_mix, pool_scale, w_branch_a, q_gain, k_gain, attn_sinks,
           w_branch_b, w_gate, b_gate, w_out, g_ffn, w_router, router_bias, w_exp_gate, w_exp_up,
           w_exp_down, w_sh_gate, w_sh_up, w_sh_down, g_ple, w_ple_gate, w_ple_proj, g_ple_post):
    batch, seq, d = x.shape
    depth = p.shape[0]
    x2d = x.reshape(batch * seq, d)
    for i in range(depth):
        x2d = _layer(x2d, p[i].reshape(batch * seq, PLE_DIM), batch, seq, g_mix[i], w_in[i],
                     w_pool_mix[i], pool_scale[i], w_branch_a[i], q_gain[i], k_gain[i], attn_sinks[i],
                     w_branch_b[i], w_gate[i], b_gate[i], w_out[i], g_ffn[i], w_router[i],
                     router_bias[i], w_exp_gate[i], w_exp_up[i], w_exp_down[i], w_sh_gate[i],
                     w_sh_up[i], w_sh_down[i], g_ple[i], w_ple_gate[i], w_ple_proj[i], g_ple_post[i])
    return x2d.reshape(batch, seq, d)
```

```python
import numpy as np
import jax
import jax.numpy as jnp
from jax import lax
from jax.experimental import pallas as pl
from jax.experimental.pallas import tpu as pltpu
from jax.experimental.pallas import tpu_sc as plsc

F32 = jnp.float32
BF16 = jnp.bfloat16
I32 = jnp.int32

D_MODEL = 1024
PLE_DIM = 256
POOL_WINDOWS = (2, 4, 8, 16)
POOL_GROUP_DIM = 128
POOL_DIM = 512
N_HEADS = 8
N_KV_HEADS = 2
Q_GROUP = 4
HEAD_DIM = 64
HEAD_PAD = 128
Q_DIM = 512
KV_DIM = 128
ATTN_BLOCK = 128
N_EXPERTS = 64
N_EXPERT_GROUPS = 8
GROUP_SIZE = 8
TOPK_GROUPS = 4
TOP_K = 8
EXPERT_DIM = 256
ROUTED_SCALE = 2.5
MOE_BLOCK = 512
EXPERT_SLOTS = 8
EPS = 1e-6
HALF = D_MODEL // 2
MASKED = -1e30

COL_Q = POOL_DIM
COL_K = COL_Q + N_HEADS * HEAD_PAD
COL_V = COL_K + N_KV_HEADS * HEAD_PAD
IN_PAD = COL_V + N_KV_HEADS * HEAD_PAD
HEAD_ORDER = (0, 2, 1, 3)

TM_MIX = 512
TM_FFN = 1024
TM_COMBINE = 1024
TM_POS = 4096
COMBINE_CHUNKS = 2
PIPELINE_HALVES = 2
SC_SCATTER_ROWS = 128
SC_SUM_ROWS = 8
SC_LANES = 16
POOL_TAIL = 8
VMEM_LIMIT = 56 * 1024 * 1024

_NT = (((1,), (1,)), ((), ()))


def _dot(a, b):
    return jnp.dot(a, b, preferred_element_type=F32)


def _rms(x, g):
    ms = jnp.mean(x * x, axis=-1, keepdims=True)
    return x * lax.rsqrt(ms + EPS) * g


def _sigmoid(x):
    return 0.5 * jnp.tanh(0.5 * x) + 0.5


def _pack(lo, hi):
    lo_bits = lax.bitcast_convert_type(lo.astype(BF16).astype(F32), I32)
    hi_bits = lax.bitcast_convert_type(hi.astype(BF16).astype(F32), I32)
    return (hi_bits & jnp.int32(-65536)) | lax.shift_right_logical(lo_bits, jnp.int32(16))


def _unpack(w):
    lo = lax.bitcast_convert_type(lax.shift_left(w, jnp.int32(16)), F32)
    hi = lax.bitcast_convert_type(w & jnp.int32(-65536), F32)
    return lo, hi


def _mixer_kernel(x_ref, gmix_ref, win_ref, qg_ref, kg_ref, bias_ref, sink_ref, ones_ref, wmix_ref,
                  pscale_ref, wa_ref, wb_ref, wg_ref, bg_ref, wout_ref,
                  gffn_ref, wrh_ref, wrl_ref, rb_ref, tri_ref, *rest):
    (x1_ref, h2p_ref, idx_ref, wt_ref, rk_ref, cnt_ref,
     kbuf, vt_buf, pbuf, zbt_buf, run_ref) = rest[-11:]
    s = pl.program_id(1)
    tm = x_ref.shape[0]

    @pl.when(s == 0)
    def _():
        kbuf[0:ATTN_BLOCK, :] = jnp.zeros((ATTN_BLOCK, kbuf.shape[1]), BF16)
        vt_buf[:, 0:ATTN_BLOCK] = jnp.zeros((vt_buf.shape[0], ATTN_BLOCK), BF16)
        pbuf[:, 0:POOL_TAIL, :] = jnp.zeros((pbuf.shape[0], POOL_TAIL, POOL_DIM), F32)

    x = x_ref[...]
    hb = _rms(x, gmix_ref[...]).astype(BF16)
    u = _dot(hb, win_ref[...])

    up = u[:, 0:POOL_DIM]
    t0 = POOL_TAIL
    pbuf[0, t0:t0 + tm, :] = up
    s2 = up + pbuf[0, t0 - 1:t0 - 1 + tm, :]
    pbuf[1, t0:t0 + tm, 128:512] = s2[:, 128:512]
    s4 = s2[:, 128:512] + pbuf[1, t0 - 2:t0 - 2 + tm, 128:512]
    pbuf[2, t0:t0 + tm, 256:512] = s4[:, 128:384]
    s8 = s4[:, 128:384] + pbuf[2, t0 - 4:t0 - 4 + tm, 256:512]
    pbuf[3, t0:t0 + tm, 384:512] = s8[:, 128:256]
    s16 = s8[:, 128:256] + pbuf[3, t0 - 8:t0 - 8 + tm, 384:512]
    for lvl in range(4):
        pbuf[lvl, 0:POOL_TAIL, 128 * lvl:512] = pbuf[lvl, tm:tm + POOL_TAIL, 128 * lvl:512]
    wsums = (s2[:, 0:128], s4[:, 0:128], s8[:, 0:128], s16)
    tpos = (s * tm).astype(F32) + lax.broadcasted_iota(I32, (tm, 1), 0).astype(F32)
    za_parts = []
    for g, w in enumerate(POOL_WINDOWS):
        inv_count = 1.0 / jnp.minimum(tpos + 1.0, float(w))
        pooled = wsums[g] * inv_count - up[:, 128 * g:128 * (g + 1)]
        mixed = _dot(pooled.astype(BF16), wmix_ref[g]) * pscale_ref[:, 128 * g:128 * (g + 1)]
        za_parts.append(mixed.astype(BF16))
    za = jnp.concatenate(za_parts, axis=1)

    ones2 = ones_ref[...]
    q = u[:, COL_Q:COL_K]
    q2 = (q * q).astype(BF16)
    qss = jnp.concatenate([_dot(q2[:, 256 * c:256 * (c + 1)], ones2) for c in range(N_HEADS // 2)], axis=1)
    qn = (q * lax.rsqrt(qss * (1.0 / HEAD_DIM) + EPS) * qg_ref[...]).astype(BF16)
    k = u[:, COL_K:COL_V]
    kss = _dot((k * k).astype(BF16), ones2)
    kbuf[ATTN_BLOCK:ATTN_BLOCK + tm, :] = (k * lax.rsqrt(kss * (1.0 / HEAD_DIM) + EPS) * kg_ref[...]).astype(BF16)
    vt_buf[:, ATTN_BLOCK:ATTN_BLOCK + tm] = u[:, COL_V:IN_PAD].T.astype(BF16)

    nq = Q_GROUP * ATTN_BLOCK
    key_j = lax.broadcasted_iota(I32, (ATTN_BLOCK, nq), 0)
    qry_i = lax.broadcasted_iota(I32, (ATTN_BLOCK, nq), 1) & (ATTN_BLOCK - 1)
    from_prev = key_j > qry_i
    first = jnp.where(s == 0, 1, 0)
    for n in range(tm // ATTN_BLOCK):
        r0 = ATTN_BLOCK * n
        for kv in range(N_KV_HEADS):
            c0 = HEAD_PAD * kv
            qs = jnp.concatenate(
                [qn[r0:r0 + ATTN_BLOCK, HEAD_PAD * (kv * Q_GROUP + g):HEAD_PAD * (kv * Q_GROUP + g + 1)]
                 for g in HEAD_ORDER], axis=0)
            kk = kbuf[r0:r0 + 2 * ATTN_BLOCK, c0:c0 + HEAD_PAD]
            st = lax.dot_general(kk, qs, _NT, preferred_element_type=F32)
            sc = jnp.where(from_prev, st[0:ATTN_BLOCK], st[ATTN_BLOCK:2 * ATTN_BLOCK])
            sc = sc + (bias_ref[first, kv] if n == 0 else bias_ref[0, kv])
            sink = sink_ref[kv]
            m = jnp.maximum(jnp.max(sc, axis=0, keepdims=True), sink)
            e = jnp.exp(sc - m)
            inv_den = 1.0 / (jnp.sum(e, axis=0, keepdims=True) + jnp.exp(sink - m))
            pt = jnp.concatenate([jnp.where(from_prev, e, 0.0), jnp.where(from_prev, 0.0, e)],
                                 axis=0).astype(BF16)
            v_even = vt_buf[c0:c0 + HEAD_PAD, r0:r0 + 2 * ATTN_BLOCK]
            v_odd = jnp.concatenate([v_even[HEAD_DIM:HEAD_PAD], v_even[0:HEAD_DIM]], axis=0)
            half = 2 * ATTN_BLOCK
            o = (_dot(v_even, pt[:, 0:half]) * inv_den[:, 0:half]
                 + _dot(v_odd, pt[:, half:2 * half]) * inv_den[:, half:2 * half])
            ch = kv * Q_GROUP * HEAD_DIM
            zbt_buf[ch:ch + HEAD_PAD, r0:r0 + ATTN_BLOCK] = o[:, 0:ATTN_BLOCK]
            zbt_buf[ch + HEAD_PAD:ch + 2 * HEAD_PAD, r0:r0 + ATTN_BLOCK] = o[:, ATTN_BLOCK:half]
    kbuf[0:ATTN_BLOCK, :] = kbuf[tm:tm + ATTN_BLOCK, :]
    vt_buf[:, 0:ATTN_BLOCK] = vt_buf[:, tm:tm + ATTN_BLOCK]

    y_a = _dot(za, wa_ref[...])
    y_b = _dot(zbt_buf[...].T.astype(BF16), wb_ref[...])
    g_a = _sigmoid(_dot(hb, wg_ref[:, 0:D_MODEL]) + bg_ref[:, 0:D_MODEL])
    merged = g_a * y_a
    g_b = _sigmoid(_dot(hb, wg_ref[:, D_MODEL:2 * D_MODEL]) + bg_ref[:, D_MODEL:2 * D_MODEL])
    merged = (merged + g_b * y_b).astype(BF16)
    x1 = x + _dot(merged, wout_ref[...])
    x1_ref[...] = x1

    is_first = (pl.program_id(0) == 0) & (s == 0)
    _route_tile(x1, is_first, gffn_ref, wrh_ref, wrl_ref, rb_ref, tri_ref,
                h2p_ref, idx_ref, wt_ref, rk_ref, cnt_ref, run_ref)


def _attn_tables(attn_sinks):
    slopes = 2.0 ** (-8.0 * (np.arange(N_HEADS) + 1) / N_HEADS)
    j = np.arange(ATTN_BLOCK)[:, None]
    i = np.arange(ATTN_BLOCK)[None, :]
    from_prev = j > i
    dist = np.where(from_prev, ATTN_BLOCK + i - j, i - j)
    bias = np.empty((2, N_KV_HEADS, ATTN_BLOCK, Q_GROUP * ATTN_BLOCK), np.float32)
    for first in range(2):
        ok = ~from_prev if first else np.ones_like(from_prev)
        for kv in range(N_KV_HEADS):
            for slot, g in enumerate(HEAD_ORDER):
                sl = np.float32(slopes[kv * Q_GROUP + g])
                val = -(sl * dist.astype(np.float32))
                bias[first, kv, :, slot * ATTN_BLOCK:(slot + 1) * ATTN_BLOCK] = np.where(ok, val, MASKED)
    sink = attn_sinks.astype(F32).reshape(N_KV_HEADS, Q_GROUP)[:, np.asarray(HEAD_ORDER)]
    sink = jnp.repeat(sink, ATTN_BLOCK, axis=1)
    return jnp.asarray(bias), sink.reshape(N_KV_HEADS, 1, Q_GROUP * ATTN_BLOCK)


def _pad_heads(w, n_heads):
    k = w.shape[0]
    w = w.reshape(k, n_heads, HEAD_DIM)
    w = jnp.pad(w, ((0, 0), (0, 0), (0, HEAD_PAD - HEAD_DIM)))
    return w.reshape(k, n_heads * HEAD_PAD)


def _const_spec(shape):
    nd = len(shape)
    return pl.BlockSpec(shape, lambda *_: (0,) * nd)


def _mixer(x2d, batch0, batch, seq, g_mix, w_in, w_pool_mix, pool_scale, w_branch_a, q_gain, k_gain,
           attn_sinks, w_branch_b, w_gate, b_gate, w_out, g_ffn, w_router, router_bias, after=None):
    t = batch * seq
    tm = TM_MIX
    ns = seq // tm
    w_q = w_in[:, POOL_DIM:POOL_DIM + Q_DIM]
    w_k = w_in[:, POOL_DIM + Q_DIM:POOL_DIM + Q_DIM + KV_DIM]
    w_v = w_in[:, POOL_DIM + Q_DIM + KV_DIM:]
    win_p = jnp.concatenate([w_in[:, :POOL_DIM], _pad_heads(w_q, N_HEADS), _pad_heads(w_k, N_KV_HEADS),
                             _pad_heads(w_v, N_KV_HEADS)], axis=1).astype(BF16)
    qg = jnp.tile(jnp.pad(q_gain * (HEAD_DIM ** -0.5), (0, HEAD_PAD - HEAD_DIM)), N_HEADS).reshape(1, -1)
    kg = jnp.tile(jnp.pad(k_gain, (0, HEAD_PAD - HEAD_DIM)), N_KV_HEADS).reshape(1, -1)
    bias, sink = _attn_tables(attn_sinks)
    ones2 = jnp.asarray(np.kron(np.eye(2), np.ones((HEAD_PAD, HEAD_PAD))), BF16)
    operands = (x2d, g_mix.reshape(1, D_MODEL), win_p, qg, kg, bias, sink, ones2, w_pool_mix.astype(BF16),
                pool_scale.reshape(1, POOL_DIM), w_branch_a.astype(BF16), w_branch_b.astype(BF16),
                w_gate.astype(BF16), b_gate.reshape(1, 2 * D_MODEL), w_out.astype(BF16))
    operands += _router_operands(g_ffn, w_router, router_bias, tm)
    in_specs = [pl.BlockSpec((tm, D_MODEL), lambda b, s: ((batch0 + b) * ns + s, 0))]
    in_specs += [_const_spec(op.shape) for op in operands[1:]]
    if after is not None:
        operands += (after,)
        in_specs.append(pl.BlockSpec(memory_space=pl.ANY))
    row8 = pl.BlockSpec((TOP_K, tm), lambda b, s: (0, b * ns + s))
    return pl.pallas_call(
        _mixer_kernel,
        grid=(batch, ns),
        in_specs=in_specs,
        out_specs=[pl.BlockSpec((tm, D_MODEL), lambda b, s: (b * ns + s, 0)),
                   pl.BlockSpec((tm, HALF), lambda b, s: (b * ns + s, 0)),
                   row8,
                   pl.BlockSpec((tm, TOP_K * SC_LANES), lambda b, s: (b * ns + s, 0)),
                   row8,
                   pl.BlockSpec((N_EXPERTS, 128), lambda b, s: (0, 0))],
        out_shape=[jax.ShapeDtypeStruct((t, D_MODEL), F32),
                   jax.ShapeDtypeStruct((t, HALF), I32),
                   jax.ShapeDtypeStruct((TOP_K, t), I32),
                   jax.ShapeDtypeStruct((t, TOP_K * SC_LANES), F32),
                   jax.ShapeDtypeStruct((TOP_K, t), I32),
                   jax.ShapeDtypeStruct((N_EXPERTS, 128), F32)],
        scratch_shapes=[
            pltpu.VMEM((ATTN_BLOCK + tm, N_KV_HEADS * HEAD_PAD), BF16),
            pltpu.VMEM((N_KV_HEADS * HEAD_PAD, ATTN_BLOCK + tm), BF16),
            pltpu.VMEM((4, POOL_TAIL + tm, POOL_DIM), F32),
            pltpu.VMEM((Q_DIM, tm), F32),
            pltpu.VMEM((N_EXPERTS, 128), F32),
        ],
        compiler_params=pltpu.CompilerParams(
            dimension_semantics=("arbitrary", "arbitrary"), vmem_limit_bytes=VMEM_LIMIT),
        name="mixer",
    )(*operands)


def _route_tile(x1, is_first, gffn_ref, wrh_ref, wrl_ref, rb_ref, tri_ref,
                h2p_ref, idx_ref, wt_ref, rk_ref, cnt_ref, run_ref):
    tm = x1.shape[0]

    @pl.when(is_first)
    def _():
        run_ref[...] = jnp.zeros_like(run_ref)

    h2 = _rms(x1, gffn_ref[...])
    hb = h2.astype(BF16)
    h2p_ref[...] = _pack(h2[:, 0:HALF], h2[:, HALF:D_MODEL])

    lo = (h2 - hb.astype(F32)).astype(BF16)
    wrh = wrh_ref[...]
    logits = (lax.dot_general(wrh, hb, _NT, preferred_element_type=F32)
              + lax.dot_general(wrl_ref[...], hb, _NT, preferred_element_type=F32)
              + lax.dot_general(wrh, lo, _NT, preferred_element_type=F32))
    scores = _sigmoid(logits)
    choice = scores + rb_ref[...]

    neg = -jnp.inf
    sub8 = lax.broadcasted_iota(I32, (GROUP_SIZE, tm), 0).astype(F32)
    grp_rows = []
    for gi in range(N_EXPERT_GROUPS):
        cg = choice[GROUP_SIZE * gi:GROUP_SIZE * (gi + 1)]
        m1 = jnp.max(cg, axis=0, keepdims=True)
        first = jnp.min(jnp.where(cg == m1, sub8, float(GROUP_SIZE)), axis=0, keepdims=True)
        m2 = jnp.max(jnp.where(sub8 == first, neg, cg), axis=0, keepdims=True)
        grp_rows.append(m1 + m2)
    gs = jnp.concatenate(grp_rows, axis=0)
    beaten = jnp.zeros((N_EXPERT_GROUPS, tm), F32)
    for gi in range(N_EXPERT_GROUPS):
        row = grp_rows[gi]
        wins = (row > gs) | ((row == gs) & (sub8 > float(gi)))
        beaten = beaten + jnp.where(wins, 1.0, 0.0)
    gsel = beaten < float(TOPK_GROUPS)
    cm = jnp.concatenate(
        [jnp.where(gsel[gi:gi + 1], choice[GROUP_SIZE * gi:GROUP_SIZE * (gi + 1)], neg)
         for gi in range(N_EXPERT_GROUPS)], axis=0)

    iota_e = lax.broadcasted_iota(I32, (N_EXPERTS, tm), 0).astype(F32)
    idx_rows, s_rows = [], []
    sel = jnp.zeros((N_EXPERTS, tm), F32)
    for _ in range(TOP_K):
        m = jnp.max(cm, axis=0, keepdims=True)
        idx = jnp.min(jnp.where(cm == m, iota_e, float(N_EXPERTS)), axis=0, keepdims=True)
        oh = iota_e == idx
        s_rows.append(jnp.sum(jnp.where(oh, scores, 0.0), axis=0, keepdims=True))
        idx_rows.append(idx)
        cm = jnp.where(oh, neg, cm)
        sel = sel + jnp.where(oh, 1.0, 0.0)

    run = run_ref[:, 0:1]
    cum = _dot(sel.astype(BF16), tri_ref[...])
    before = run + cum - sel
    rk_rows = [jnp.sum(jnp.where(iota_e == idx, before, 0.0), axis=0, keepdims=True) for idx in idx_rows]
    new_run = run + jnp.sum(sel, axis=1, keepdims=True)
    run_ref[...] = jnp.broadcast_to(new_run, run_ref.shape)
    cnt_ref[...] = jnp.broadcast_to(new_run, cnt_ref.shape)

    ssum = s_rows[0]
    for r in range(1, TOP_K):
        ssum = ssum + s_rows[r]
    denom = ssum + 1e-20
    idx_ref[...] = jnp.concatenate(idx_rows, axis=0).astype(I32)
    w_rep = jnp.concatenate([jnp.broadcast_to(sr / denom * ROUTED_SCALE, (SC_LANES, tm)) for sr in s_rows],
                            axis=0)
    wt_ref[...] = w_rep.T
    rk_ref[...] = jnp.concatenate(rk_rows, axis=0).astype(I32)


def _router_operands(g_ffn, w_router, router_bias, tm):
    wr_t = w_router.astype(F32).T
    wr_hi = wr_t.astype(BF16)
    wr_lo = (wr_t - wr_hi.astype(F32)).astype(BF16)
    tri = (np.arange(tm)[:, None] <= np.arange(tm)[None, :]).astype(np.float32)
    return (g_ffn.reshape(1, D_MODEL), wr_hi, wr_lo, router_bias.astype(F32).reshape(N_EXPERTS, 1),
            jnp.asarray(tri, BF16))


def _swiglu_packed(x_packed, wg, wu, wd):
    lo, hi = _unpack(x_packed)
    lo = lo.astype(BF16)
    hi = hi.astype(BF16)
    g = _dot(lo, wg[0:HALF, :]) + _dot(hi, wg[HALF:D_MODEL, :])
    u = _dot(lo, wu[0:HALF, :]) + _dot(hi, wu[HALF:D_MODEL, :])
    a = (g * _sigmoid(g) * u).astype(BF16)
    y = _dot(a, wd[...])
    return _pack(y[:, 0:HALF], y[:, HALF:D_MODEL])


def _shared_kernel(h2p_ref, wg_ref, wu_ref, wd_ref, *rest):
    ysh_ref = rest[-1]
    ysh_ref[...] = _swiglu_packed(h2p_ref[...], wg_ref, wu_ref, wd_ref)


def _shared(h2p, w_sh_gate, w_sh_up, w_sh_down, after=None):
    t = h2p.shape[0]
    tm = TM_FFN
    operands = (h2p, w_sh_gate.astype(BF16), w_sh_up.astype(BF16), w_sh_down.astype(BF16))
    in_specs = [pl.BlockSpec((tm, HALF), lambda i: (i, 0))]
    in_specs += [_const_spec(op.shape) for op in operands[1:]]
    if after is not None:
        operands += (after,)
        in_specs.append(pl.BlockSpec(memory_space=pl.ANY))
    return pl.pallas_call(
        _shared_kernel,
        grid=(t // tm,),
        in_specs=in_specs,
        out_specs=pl.BlockSpec((tm, HALF), lambda i: (i, 0)),
        out_shape=jax.ShapeDtypeStruct((t, HALF), I32),
        compiler_params=pltpu.CompilerParams(
            dimension_semantics=("arbitrary",), vmem_limit_bytes=VMEM_LIMIT),
        name="shared",
    )(*operands)


def _positions_kernel(pstart_ref, idx_ref, rk_ref, pos_ref):
    idx = idx_ref[...]
    acc = rk_ref[...]
    for e in range(N_EXPERTS):
        acc = acc + jnp.where(idx == e, pstart_ref[e], 0)
    pos_ref[...] = acc


def _positions(pstart, idx8, rk8):
    t = idx8.shape[1]
    tm = TM_POS
    spec = pl.BlockSpec((TOP_K, tm), lambda i, ps: (0, i))
    return pl.pallas_call(
        _positions_kernel,
        grid_spec=pltpu.PrefetchScalarGridSpec(num_scalar_prefetch=1, grid=(t // tm,),
                                               in_specs=[spec, spec], out_specs=spec),
        out_shape=jax.ShapeDtypeStruct((TOP_K, t), I32),
        name="positions",
    )(pstart, idx8, rk8)


def _sc_workers():
    info = plsc.get_sparse_core_info()
    return info.num_cores, info.num_cores * info.num_subcores


def _dispatch(pos_flat, h2p, n_rows):
    t = h2p.shape[0]
    ch = SC_SCATTER_ROWS
    n_cores, n_workers = _sc_workers()
    n_ch = t // (n_workers * ch)

    def body(pos_hbm, h_hbm, xs_hbm, idx_v, rows_v, sem):
        wid = lax.axis_index("s") * n_cores + lax.axis_index("c")

        @pl.loop(0, n_ch)
        def _(c):
            chunk = wid * n_ch + c
            pltpu.sync_copy(h_hbm.at[pl.ds(chunk * ch, ch)], rows_v)
            for k in range(TOP_K):
                pltpu.sync_copy(pos_hbm.at[pl.ds(k * t + chunk * ch, ch)], idx_v.at[k])
            copies = [pltpu.async_copy(rows_v, xs_hbm.at[idx_v.at[k]], sem) for k in range(TOP_K)]
            for cp in copies:
                cp.wait()

    return pl.kernel(
        body,
        out_type=jax.ShapeDtypeStruct((n_rows, HALF), I32),
        mesh=plsc.VectorSubcoreMesh(core_axis_name="c", subcore_axis_name="s"),
        scratch_types=[pltpu.VMEM((TOP_K, ch), I32), pltpu.VMEM((ch, HALF), I32),
                       pltpu.SemaphoreType.DMA],
        name="dispatch",
    )(pos_flat, h2p)


def _pairwise_sum(xs):
    while len(xs) > 1:
        xs = [xs[i] + xs[i + 1] for i in range(0, len(xs), 2)]
    return xs[0]


def _gather_sum(pos_flat, w_rep, ys, tok0, t):
    t_all = w_rep.shape[0]
    ch = SC_SUM_ROWS
    lanes = SC_LANES
    n_cores, n_workers = _sc_workers()
    per_w = t // n_workers
    n_ch = per_w // ch

    def body(pos_hbm, w_hbm, ys_hbm, out_hbm, idx_all, w_all, buf0, buf1, acc0, acc1, gsem0, gsem1, osem0, osem1):
        wid = lax.axis_index("s") * n_cores + lax.axis_index("c")
        base = wid * n_ch
        first = tok0 + wid * per_w
        bufs, accs, gsems, osems = (buf0, buf1), (acc0, acc1), (gsem0, gsem1), (osem0, osem1)
        for k in range(TOP_K):
            pltpu.sync_copy(pos_hbm.at[pl.ds(k * t_all + first, per_w)], idx_all.at[pl.ds(k * per_w, per_w)])
        pltpu.sync_copy(w_hbm.at[pl.ds(first, per_w)], w_all)

        def row_copy(c, s, k):
            rows = idx_all.at[pl.ds(k * per_w + c * ch, ch)]
            return pltpu.make_async_copy(ys_hbm.at[rows], bufs[s].at[k], gsems[s])

        def out_copy(c, s):
            return pltpu.make_async_copy(accs[s], out_hbm.at[pl.ds((base + c) * ch, ch)], osems[s])

        def fetch(c, s):
            for k in range(TOP_K):
                row_copy(c, s, k).start()

        def reduce_rows(c, s):
            @pl.loop(0, ch)
            def _(r):
                wks = [w_all[c * ch + r, pl.ds(k * lanes, lanes)] for k in range(TOP_K)]

                @pl.loop(0, HALF // lanes, step=2)
                def _(v):
                    for dv in range(2):
                        col = (v + dv) * lanes
                        los, his = [], []
                        for k in range(TOP_K):
                            words = bufs[s][k, r, pl.ds(col, lanes)]
                            los.append(wks[k] * lax.bitcast_convert_type(lax.shift_left(words, jnp.int32(16)), F32))
                            his.append(wks[k] * lax.bitcast_convert_type(words & jnp.int32(-65536), F32))
                        accs[s][r, pl.ds(col, lanes)] = _pairwise_sum(los)
                        accs[s][r, pl.ds(HALF + col, lanes)] = _pairwise_sum(his)

        fetch(0, 0)

        @pl.loop(0, n_ch, step=2)
        def _(c):
            for s in range(2):
                cc = c + s

                @pl.when(cc + 1 < n_ch)
                def _():
                    fetch(cc + 1, 1 - s)

                for k in range(TOP_K):
                    row_copy(cc, s, k).wait()

                @pl.when(cc >= 2)
                def _():
                    out_copy(cc - 2, s).wait()

                reduce_rows(cc, s)
                out_copy(cc, s).start()

        out_copy(n_ch - 2, 0).wait()
        out_copy(n_ch - 1, 1).wait()

    return pl.kernel(
        body,
        out_type=jax.ShapeDtypeStruct((t, D_MODEL), F32),
        mesh=plsc.VectorSubcoreMesh(core_axis_name="c", subcore_axis_name="s"),
        scratch_types=[pltpu.VMEM((TOP_K * per_w,), I32), pltpu.VMEM((per_w, TOP_K * lanes), F32),
                       pltpu.VMEM((TOP_K, ch, HALF), I32), pltpu.VMEM((TOP_K, ch, HALF), I32),
                       pltpu.VMEM((ch, D_MODEL), F32), pltpu.VMEM((ch, D_MODEL), F32),
                       pltpu.SemaphoreType.DMA, pltpu.SemaphoreType.DMA,
                       pltpu.SemaphoreType.DMA, pltpu.SemaphoreType.DMA],
        compiler_params=pltpu.CompilerParams(needs_layout_passes=False),
        name="gather_sum",
    )(pos_flat, w_rep, ys)


def _expert_kernel(nblk_ref, blk0_ref, cnt_ref, wg_ref, wu_ref, wd_ref, xs_hbm, after_ref, ys_hbm,
                   wg_s, wu_s, wd_s, xbuf, ybuf, in_sem, out_sem):
    del after_ref
    e = pl.program_id(0)
    nb = nblk_ref[e]
    b0 = blk0_ref[e]
    cnt = cnt_ref[e]
    n_all = blk0_ref[N_EXPERTS]
    ahead = EXPERT_SLOTS - 2
    wg_s[...] = wg_ref[0].astype(BF16)
    wu_s[...] = wu_ref[0].astype(BF16)
    wd_s[...] = wd_ref[0].astype(BF16)

    def slot_rows(g, n_blocks=1):
        start = (g & (EXPERT_SLOTS - 1)) * MOE_BLOCK
        return pl.ds(pl.multiple_of(start, MOE_BLOCK), n_blocks * MOE_BLOCK)

    def rows_of(g):
        return pl.ds(pl.multiple_of(g * MOE_BLOCK, MOE_BLOCK), MOE_BLOCK)

    def in_copy(g):
        return pltpu.make_async_copy(xs_hbm.at[rows_of(g)], xbuf.at[slot_rows(g)],
                                     in_sem.at[g & (EXPERT_SLOTS - 1)])

    def out_copy(g):
        return pltpu.make_async_copy(ybuf.at[slot_rows(g)], ys_hbm.at[rows_of(g)],
                                     out_sem.at[g & (EXPERT_SLOTS - 1)])

    @pl.when(e == 0)
    def _():
        for g in range(ahead):
            @pl.when(g < n_all)
            def _():
                in_copy(g).start()

    def arrive(g):
        in_copy(g).wait()

        @pl.when(g + ahead < n_all)
        def _():
            in_copy(g + ahead).start()

        @pl.when(g >= EXPERT_SLOTS)
        def _():
            out_copy(g - EXPERT_SLOTS).wait()

    def run_blocks(j, n_blocks):
        g = b0 + j
        for i in range(n_blocks):
            arrive(g + i)
        rows = lax.broadcasted_iota(I32, (n_blocks * MOE_BLOCK, HALF), 0)
        x = jnp.where(rows < cnt - j * MOE_BLOCK, xbuf[slot_rows(g, n_blocks)], 0)
        ybuf[slot_rows(g, n_blocks)] = _swiglu_packed(x, wg_s, wu_s, wd_s)
        for i in range(n_blocks):
            out_copy(g + i).start()

    lead = jnp.where(nb > 0, b0 & 1, 0)

    @pl.when(lead == 1)
    def _():
        run_blocks(0, 1)

    n_pairs = (nb - lead) // 2

    def one_pair(p, carry):
        run_blocks(lead + 2 * p, 2)
        return carry

    lax.fori_loop(0, n_pairs, one_pair, 0)

    @pl.when(nb - lead - 2 * n_pairs == 1)
    def _():
        run_blocks(nb - 1, 1)

    @pl.when(e == N_EXPERTS - 1)
    def _():
        for back in range(1, EXPERT_SLOTS + 1):
            @pl.when(n_all >= back)
            def _():
                out_copy(n_all - back).wait()


def _experts(n_blk, blk0, counts, n_rows, xs, w_gate, w_up, w_down, after):
    wspec_in = pl.BlockSpec((1, D_MODEL, EXPERT_DIM), lambda e, *_: (e, 0, 0))
    grid_spec = pltpu.PrefetchScalarGridSpec(
        num_scalar_prefetch=3,
        grid=(N_EXPERTS,),
        in_specs=[wspec_in, wspec_in,
                  pl.BlockSpec((1, EXPERT_DIM, D_MODEL), lambda e, *_: (e, 0, 0)),
                  pl.BlockSpec(memory_space=pl.ANY),
                  pl.BlockSpec(memory_space=pl.ANY)],
        out_specs=pl.BlockSpec(memory_space=pl.ANY),
        scratch_shapes=[pltpu.VMEM((D_MODEL, EXPERT_DIM), BF16), pltpu.VMEM((D_MODEL, EXPERT_DIM), BF16),
                        pltpu.VMEM((EXPERT_DIM, D_MODEL), BF16),
                        pltpu.VMEM((EXPERT_SLOTS * MOE_BLOCK, HALF), I32),
                        pltpu.VMEM((EXPERT_SLOTS * MOE_BLOCK, HALF), I32),
                        pltpu.SemaphoreType.DMA((EXPERT_SLOTS,)), pltpu.SemaphoreType.DMA((EXPERT_SLOTS,))],
    )
    return pl.pallas_call(
        _expert_kernel,
        grid_spec=grid_spec,
        out_shape=jax.ShapeDtypeStruct((n_rows, HALF), I32),
        compiler_params=pltpu.CompilerParams(
            dimension_semantics=("arbitrary",), vmem_limit_bytes=VMEM_LIMIT),
        name="experts",
    )(n_blk, blk0, counts, w_gate, w_up, w_down, xs, after)


def _combine_kernel(x1_ref, ysh_ref, moe_ref, p_ref, gple_ref, wpg_ref, wpp_ref, gpost_ref, *rest):
    out_ref = rest[-1]
    proj = _rms(_dot(p_ref[...].astype(BF16), wpp_ref[...]), gpost_ref[...])
    sh_lo, sh_hi = _unpack(ysh_ref[...])
    x2 = x1_ref[...] + moe_ref[...] + jnp.concatenate([sh_lo, sh_hi], axis=1)
    gate = _sigmoid(_dot(_rms(x2, gple_ref[...]).astype(BF16), wpg_ref[...]))
    out_ref[...] = x2 + gate * proj


def _combine(x1, ysh, moe, p2d, g_ple, w_ple_gate, w_ple_proj, g_ple_post, tok0, tok0_all, prev_out):
    t = p2d.shape[0]
    tm = TM_COMBINE
    steps = moe.shape[0] // tm
    off = tok0 // tm
    off_all = tok0_all // tm
    operands = [x1, ysh, moe, p2d, g_ple.reshape(1, D_MODEL), w_ple_gate.astype(BF16),
                w_ple_proj.astype(BF16), g_ple_post.reshape(1, D_MODEL)]
    in_specs = [pl.BlockSpec((tm, D_MODEL), lambda i: (off + i, 0)),
                pl.BlockSpec((tm, HALF), lambda i: (off + i, 0)),
                pl.BlockSpec((tm, D_MODEL), lambda i: (i, 0)),
                pl.BlockSpec((tm, PLE_DIM), lambda i: (off_all + i, 0)),
                _const_spec((1, D_MODEL)),
                _const_spec((D_MODEL, D_MODEL)),
                _const_spec((PLE_DIM, D_MODEL)),
                _const_spec((1, D_MODEL))]
    aliases = {}
    if prev_out is not None:
        aliases = {len(operands): 0}
        operands.append(prev_out)
        in_specs.append(pl.BlockSpec(memory_space=pl.ANY))
    return pl.pallas_call(
        _combine_kernel,
        grid=(steps,),
        in_specs=in_specs,
        out_specs=pl.BlockSpec((tm, D_MODEL), lambda i: (off_all + i, 0)),
        out_shape=jax.ShapeDtypeStruct((t, D_MODEL), F32),
        input_output_aliases=aliases,
        compiler_params=pltpu.CompilerParams(
            dimension_semantics=("arbitrary",), vmem_limit_bytes=VMEM_LIMIT),
        name="combine",
    )(*operands)


def _layer(x2d, p2d, batch, seq, g_mix, w_in, w_pool_mix, pool_scale, w_branch_a, q_gain, k_gain,
           attn_sinks, w_branch_b, w_gate, b_gate, w_out, g_ffn, w_router, router_bias, w_exp_gate,
           w_exp_up, w_exp_down, w_sh_gate, w_sh_up, w_sh_down, g_ple, w_ple_gate, w_ple_proj,
           g_ple_post):
    bh = batch // PIPELINE_HALVES
    th = bh * seq
    n_rows = (-(-(th * TOP_K) // MOE_BLOCK) + N_EXPERTS) * MOE_BLOCK

    halves = []
    after = None
    for h in range(PIPELINE_HALVES):
        x1, h2p, idx8, w_rep, rk8, cnt = _mixer(
            x2d, h * bh, bh, seq, g_mix, w_in, w_pool_mix, pool_scale, w_branch_a, q_gain, k_gain,
            attn_sinks, w_branch_b, w_gate, b_gate, w_out, g_ffn, w_router, router_bias, after=after)
        counts = cnt[:, 0].astype(I32)
        padded = (counts + MOE_BLOCK - 1) // MOE_BLOCK * MOE_BLOCK
        pend = jnp.cumsum(padded)
        pstart = pend - padded
        pos = _positions(pstart.astype(I32), idx8, rk8).reshape(-1)
        xs = _dispatch(pos, h2p, n_rows)
        blk0 = jnp.concatenate([pstart, pend[-1:]]) // MOE_BLOCK
        halves.append((x1, h2p, w_rep, pos, xs, (padded // MOE_BLOCK).astype(I32), blk0.astype(I32), counts))
        after = pos

    ys_all = []
    for (x1, h2p, w_rep, pos, xs, n_blk, blk0, counts) in halves:
        ys_all.append(_experts(n_blk, blk0, counts, n_rows, xs, w_exp_gate, w_exp_up, w_exp_down,
                               ys_all[-1] if ys_all else xs))

    out = None
    tc = th // COMBINE_CHUNKS
    for h, (x1, h2p, w_rep, pos, xs, n_blk, blk0, counts) in enumerate(halves):
        ys = ys_all[h]
        ysh = _shared(h2p, w_sh_gate, w_sh_up, w_sh_down, after=ys_all[-1] if out is None else out)
        for c in range(COMBINE_CHUNKS):
            moe = _gather_sum(pos, w_rep, ys, c * tc, tc)
            out = _combine(x1, ysh, moe, p2d, g_ple, w_ple_gate, w_ple_proj, g_ple_post,
                           c * tc, h * th + c * tc, out)
    return out


def kernel(x, p, g_mix, w_in, w_pool_mix, pool_scale, w_branch_a, q_gain, k_gain, attn_sinks,
           w_branch_b, w_gate, b_gate, w_out, g_ffn, w_router, router_bias, w_exp_gate, w_exp_up,
           w_exp_down, w_sh_gate, w_sh_up, w_sh_down, g_ple, w_ple_gate, w_ple_proj, g_ple_post):
    batch, seq, d = x.shape
    depth = p.shape[0]
    x2d = x.reshape(batch * seq, d)
    for i in range(depth):
        x2d = _layer(x2d, p[i].reshape(batch * seq, PLE_DIM), batch, seq, g_mix[i], w_in[i],
                     w_pool_mix[i], pool_scale[i], w_branch_a[i], q_gain[i], k_gain[i], attn_sinks[i],
                     w_branch_b[i], w_gate[i], b_gate[i], w_out[i], g_ffn[i], w_router[i],
                     router_bias[i], w_exp_gate[i], w_exp_up[i], w_exp_down[i], w_sh_gate[i],
                     w_sh_up[i], w_sh_down[i], g_ple[i], w_ple_gate[i], w_ple_proj[i], g_ple_post[i])
    return x2d.reshape(batch, seq, d)
```

```python
import numpy as np
import jax
import jax.numpy as jnp
from jax import lax
from jax.experimental import pallas as pl
from jax.experimental.pallas import tpu as pltpu
from jax.experimental.pallas import tpu_sc as plsc

F32 = jnp.float32
BF16 = jnp.bfloat16
I32 = jnp.int32

D_MODEL = 1024
PLE_DIM = 256
POOL_WINDOWS = (2, 4, 8, 16)
POOL_GROUP_DIM = 128
POOL_DIM = 512
N_HEADS = 8
N_KV_HEADS = 2
Q_GROUP = 4
HEAD_DIM = 64
HEAD_PAD = 128
Q_DIM = 512
KV_DIM = 128
ATTN_BLOCK = 128
N_EXPERTS = 64
N_EXPERT_GROUPS = 8
GROUP_SIZE = 8
TOPK_GROUPS = 4
TOP_K = 8
EXPERT_DIM = 256
ROUTED_SCALE = 2.5
MOE_BLOCK = 512
EXPERT_SLOTS = 8
EPS = 1e-6
HALF = D_MODEL // 2
MASKED = -1e30

COL_Q = POOL_DIM
COL_K = COL_Q + N_HEADS * HEAD_PAD
COL_V = COL_K + N_KV_HEADS * HEAD_PAD
IN_PAD = COL_V + N_KV_HEADS * HEAD_PAD
HEAD_ORDER = (0, 2, 1, 3)

TM_MIX = 512
TM_FFN = 1024
TM_COMBINE = 1024
TM_POS = 4096
COMBINE_CHUNKS = 2
PIPELINE_HALVES = 2
SC_SCATTER_ROWS = 128
SC_SUM_ROWS = 8
SC_LANES = 16
POOL_TAIL = 8
VMEM_LIMIT = 56 * 1024 * 1024

_NT = (((1,), (1,)), ((), ()))


def _dot(a, b):
    return jnp.dot(a, b, preferred_element_type=F32)


def _rms(x, g):
    ms = jnp.mean(x * x, axis=-1, keepdims=True)
    return x * lax.rsqrt(ms + EPS) * g


def _sigmoid(x):
    return 0.5 * jnp.tanh(0.5 * x) + 0.5


def _pack(lo, hi):
    lo_bits = lax.bitcast_convert_type(lo.astype(BF16).astype(F32), I32)
    hi_bits = lax.bitcast_convert_type(hi.astype(BF16).astype(F32), I32)
    return (hi_bits & jnp.int32(-65536)) | lax.shift_right_logical(lo_bits, jnp.int32(16))


def _unpack(w):
    lo = lax.bitcast_convert_type(lax.shift_left(w, jnp.int32(16)), F32)
    hi = lax.bitcast_convert_type(w & jnp.int32(-65536), F32)
    return lo, hi


def _mixer_kernel(x_ref, gmix_ref, win_ref, qg_ref, kg_ref, bias_ref, sink_ref, ones_ref, wmix_ref,
                  pscale_ref, wa_ref, wb_ref, wg_ref, bg_ref, wout_ref,
                  gffn_ref, wrh_ref, wrl_ref, rb_ref, tri_ref, *rest):
    (x1_ref, h2p_ref, idx_ref, wt_ref, rk_ref, cnt_ref,
     kbuf, vt_buf, pbuf, zbt_buf, run_ref) = rest[-11:]
    s = pl.program_id(1)
    tm = x_ref.shape[0]

    @pl.when(s == 0)
    def _():
        kbuf[0:ATTN_BLOCK, :] = jnp.zeros((ATTN_BLOCK, kbuf.shape[1]), BF16)
        vt_buf[:, 0:ATTN_BLOCK] = jnp.zeros((vt_buf.shape[0], ATTN_BLOCK), BF16)
        pbuf[:, 0:POOL_TAIL, :] = jnp.zeros((pbuf.shape[0], POOL_TAIL, POOL_DIM), F32)

    x = x_ref[...]
    hb = _rms(x, gmix_ref[...]).astype(BF16)
    u = _dot(hb, win_ref[...])

    up = u[:, 0:POOL_DIM]
    t0 = POOL_TAIL
    pbuf[0, t0:t0 + tm, :] = up
    s2 = up + pbuf[0, t0 - 1:t0 - 1 + tm, :]
    pbuf[1, t0:t0 + tm, 128:512] = s2[:, 128:512]
    s4 = s2[:, 128:512] + pbuf[1, t0 - 2:t0 - 2 + tm, 128:512]
    pbuf[2, t0:t0 + tm, 256:512] = s4[:, 128:384]
    s8 = s4[:, 128:384] + pbuf[2, t0 - 4:t0 - 4 + tm, 256:512]
    pbuf[3, t0:t0 + tm, 384:512] = s8[:, 128:256]
    s16 = s8[:, 128:256] + pbuf[3, t0 - 8:t0 - 8 + tm, 384:512]
    for lvl in range(4):
        pbuf[lvl, 0:POOL_TAIL, 128 * lvl:512] = pbuf[lvl, tm:tm + POOL_TAIL, 128 * lvl:512]
    wsums = (s2[:, 0:128], s4[:, 0:128], s8[:, 0:128], s16)
    tpos = (s * tm).astype(F32) + lax.broadcasted_iota(I32, (tm, 1), 0).astype(F32)
    za_parts = []
    for g, w in enumerate(POOL_WINDOWS):
        inv_count = 1.0 / jnp.minimum(tpos + 1.0, float(w))
        pooled = wsums[g] * inv_count - up[:, 128 * g:128 * (g + 1)]
        mixed = _dot(pooled.astype(BF16), wmix_ref[g]) * pscale_ref[:, 128 * g:128 * (g + 1)]
        za_parts.append(mixed.astype(BF16))
    za = jnp.concatenate(za_parts, axis=1)

    ones2 = ones_ref[...]
    q = u[:, COL_Q:COL_K]
    q2 = (q * q).astype(BF16)
    qss = jnp.concatenate([_dot(q2[:, 256 * c:256 * (c + 1)], ones2) for c in range(N_HEADS // 2)], axis=1)
    qn = (q * lax.rsqrt(qss * (1.0 / HEAD_DIM) + EPS) * qg_ref[...]).astype(BF16)
    k = u[:, COL_K:COL_V]
    kss = _dot((k * k).astype(BF16), ones2)
    kbuf[ATTN_BLOCK:ATTN_BLOCK + tm, :] = (k * lax.rsqrt(kss * (1.0 / HEAD_DIM) + EPS) * kg_ref[...]).astype(BF16)
    vt_buf[:, ATTN_BLOCK:ATTN_BLOCK + tm] = u[:, COL_V:IN_PAD].T.astype(BF16)

    nq = Q_GROUP * ATTN_BLOCK
    key_j = lax.broadcasted_iota(I32, (ATTN_BLOCK, nq), 0)
    qry_i = lax.broadcasted_iota(I32, (ATTN_BLOCK, nq), 1) & (ATTN_BLOCK - 1)
    from_prev = key_j > qry_i
    first = jnp.where(s == 0, 1, 0)
    for n in range(tm // ATTN_BLOCK):
        r0 = ATTN_BLOCK * n
        for kv in range(N_KV_HEADS):
            c0 = HEAD_PAD * kv
            qs = jnp.concatenate(
                [qn[r0:r0 + ATTN_BLOCK, HEAD_PAD * (kv * Q_GROUP + g):HEAD_PAD * (kv * Q_GROUP + g + 1)]
                 for g in HEAD_ORDER], axis=0)
            kk = kbuf[r0:r0 + 2 * ATTN_BLOCK, c0:c0 + HEAD_PAD]
            st = lax.dot_general(kk, qs, _NT, preferred_element_type=F32)
            sc = jnp.where(from_prev, st[0:ATTN_BLOCK], st[ATTN_BLOCK:2 * ATTN_BLOCK])
            sc = sc + (bias_ref[first, kv] if n == 0 else bias_ref[0, kv])
            sink = sink_ref[kv]
            m = jnp.maximum(jnp.max(sc, axis=0, keepdims=True), sink)
            e = jnp.exp(sc - m)
            inv_den = 1.0 / (jnp.sum(e, axis=0, keepdims=True) + jnp.exp(sink - m))
            pt = jnp.concatenate([jnp.where(from_prev, e, 0.0), jnp.where(from_prev, 0.0, e)],
                                 axis=0).astype(BF16)
            v_even = vt_buf[c0:c0 + HEAD_PAD, r0:r0 + 2 * ATTN_BLOCK]
            v_odd = jnp.concatenate([v_even[HEAD_DIM:HEAD_PAD], v_even[0:HEAD_DIM]], axis=0)
            half = 2 * ATTN_BLOCK
            o = (_dot(v_even, pt[:, 0:half]) * inv_den[:, 0:half]
                 + _dot(v_odd, pt[:, half:2 * half]) * inv_den[:, half:2 * half])
            ch = kv * Q_GROUP * HEAD_DIM
            zbt_buf[ch:ch + HEAD_PAD, r0:r0 + ATTN_BLOCK] = o[:, 0:ATTN_BLOCK]
            zbt_buf[ch + HEAD_PAD:ch + 2 * HEAD_PAD, r0:r0 + ATTN_BLOCK] = o[:, ATTN_BLOCK:half]
    kbuf[0:ATTN_BLOCK, :] = kbuf[tm:tm + ATTN_BLOCK, :]
    vt_buf[:, 0:ATTN_BLOCK] = vt_buf[:, tm:tm + ATTN_BLOCK]

    y_a = _dot(za, wa_ref[...])
    y_b = _dot(zbt_buf[...].T.astype(BF16), wb_ref[...])
    g_a = _sigmoid(_dot(hb, wg_ref[:, 0:D_MODEL]) + bg_ref[:, 0:D_MODEL])
    merged = g_a * y_a
    g_b = _sigmoid(_dot(hb, wg_ref[:, D_MODEL:2 * D_MODEL]) + bg_ref[:, D_MODEL:2 * D_MODEL])
    merged = (merged + g_b * y_b).astype(BF16)
    x1 = x + _dot(merged, wout_ref[...])
    x1_ref[...] = x1

    is_first = (pl.program_id(0) == 0) & (s == 0)
    _route_tile(x1, is_first, gffn_ref, wrh_ref, wrl_ref, rb_ref, tri_ref,
                h2p_ref, idx_ref, wt_ref, rk_ref, cnt_ref, run_ref)


def _attn_tables(attn_sinks):
    slopes = 2.0 ** (-8.0 * (np.arange(N_HEADS) + 1) / N_HEADS)
    j = np.arange(ATTN_BLOCK)[:, None]
    i = np.arange(ATTN_BLOCK)[None, :]
    from_prev = j > i
    dist = np.where(from_prev, ATTN_BLOCK + i - j, i - j)
    bias = np.empty((2, N_KV_HEADS, ATTN_BLOCK, Q_GROUP * ATTN_BLOCK), np.float32)
    for first in range(2):
        ok = ~from_prev if first else np.ones_like(from_prev)
        for kv in range(N_KV_HEADS):
            for slot, g in enumerate(HEAD_ORDER):
                sl = np.float32(slopes[kv * Q_GROUP + g])
                val = -(sl * dist.astype(np.float32))
                bias[first, kv, :, slot * ATTN_BLOCK:(slot + 1) * ATTN_BLOCK] = np.where(ok, val, MASKED)
    sink = attn_sinks.astype(F32).reshape(N_KV_HEADS, Q_GROUP)[:, np.asarray(HEAD_ORDER)]
    sink = jnp.repeat(sink, ATTN_BLOCK, axis=1)
    return jnp.asarray(bias), sink.reshape(N_KV_HEADS, 1, Q_GROUP * ATTN_BLOCK)


def _pad_heads(w, n_heads):
    k = w.shape[0]
    w = w.reshape(k, n_heads, HEAD_DIM)
    w = jnp.pad(w, ((0, 0), (0, 0), (0, HEAD_PAD - HEAD_DIM)))
    return w.reshape(k, n_heads * HEAD_PAD)


def _const_spec(shape):
    nd = len(shape)
    return pl.BlockSpec(shape, lambda *_: (0,) * nd)


def _mixer(x2d, batch0, batch, seq, g_mix, w_in, w_pool_mix, pool_scale, w_branch_a, q_gain, k_gain,
           attn_sinks, w_branch_b, w_gate, b_gate, w_out, g_ffn, w_router, router_bias, after=None):
    t = batch * seq
    tm = TM_MIX
    ns = seq // tm
    w_q = w_in[:, POOL_DIM:POOL_DIM + Q_DIM]
    w_k = w_in[:, POOL_DIM + Q_DIM:POOL_DIM + Q_DIM + KV_DIM]
    w_v = w_in[:, POOL_DIM + Q_DIM + KV_DIM:]
    win_p = jnp.concatenate([w_in[:, :POOL_DIM], _pad_heads(w_q, N_HEADS), _pad_heads(w_k, N_KV_HEADS),
                             _pad_heads(w_v, N_KV_HEADS)], axis=1).astype(BF16)
    qg = jnp.tile(jnp.pad(q_gain * (HEAD_DIM ** -0.5), (0, HEAD_PAD - HEAD_DIM)), N_HEADS).reshape(1, -1)
    kg = jnp.tile(jnp.pad(k_gain, (0, HEAD_PAD - HEAD_DIM)), N_KV_HEADS).reshape(1, -1)
    bias, sink = _attn_tables(attn_sinks)
    ones2 = jnp.asarray(np.kron(np.eye(2), np.ones((HEAD_PAD, HEAD_PAD))), BF16)
    operands = (x2d, g_mix.reshape(1, D_MODEL), win_p, qg, kg, bias, sink, ones2, w_pool_mix.astype(BF16),
                pool_scale.reshape(1, POOL_DIM), w_branch_a.astype(BF16), w_branch_b.astype(BF16),
                w_gate.astype(BF16), b_gate.reshape(1, 2 * D_MODEL), w_out.astype(BF16))
    operands += _router_operands(g_ffn, w_router, router_bias, tm)
    in_specs = [pl.BlockSpec((tm, D_MODEL), lambda b, s: ((batch0 + b) * ns + s, 0))]
    in_specs += [_const_spec(op.shape) for op in operands[1:]]
    if after is not None:
        operands += (after,)
        in_specs.append(pl.BlockSpec(memory_space=pl.ANY))
    row8 = pl.BlockSpec((TOP_K, tm), lambda b, s: (0, b * ns + s))
    return pl.pallas_call(
        _mixer_kernel,
        grid=(batch, ns),
        in_specs=in_specs,
        out_specs=[pl.BlockSpec((tm, D_MODEL), lambda b, s: (b * ns + s, 0)),
                   pl.BlockSpec((tm, HALF), lambda b, s: (b * ns + s, 0)),
                   row8,
                   pl.BlockSpec((tm, TOP_K * SC_LANES), lambda b, s: (b * ns + s, 0)),
                   row8,
                   pl.BlockSpec((N_EXPERTS, 128), lambda b, s: (0, 0))],
        out_shape=[jax.ShapeDtypeStruct((t, D_MODEL), F32),
                   jax.ShapeDtypeStruct((t, HALF), I32),
                   jax.ShapeDtypeStruct((TOP_K, t), I32),
                   jax.ShapeDtypeStruct((t, TOP_K * SC_LANES), F32),
                   jax.ShapeDtypeStruct((TOP_K, t), I32),
                   jax.ShapeDtypeStruct((N_EXPERTS, 128), F32)],
        scratch_shapes=[
            pltpu.VMEM((ATTN_BLOCK + tm, N_KV_HEADS * HEAD_PAD), BF16),
            pltpu.VMEM((N_KV_HEADS * HEAD_PAD, ATTN_BLOCK + tm), BF16),
            pltpu.VMEM((4, POOL_TAIL + tm, POOL_DIM), F32),
            pltpu.VMEM((Q_DIM, tm), F32),
            pltpu.VMEM((N_EXPERTS, 128), F32),
        ],
        compiler_params=pltpu.CompilerParams(
            dimension_semantics=("arbitrary", "arbitrary"), vmem_limit_bytes=VMEM_LIMIT),
        name="mixer",
    )(*operands)


def _route_tile(x1, is_first, gffn_ref, wrh_ref, wrl_ref, rb_ref, tri_ref,
                h2p_ref, idx_ref, wt_ref, rk_ref, cnt_ref, run_ref):
    tm = x1.shape[0]

    @pl.when(is_first)
    def _():
        run_ref[...] = jnp.zeros_like(run_ref)

    h2 = _rms(x1, gffn_ref[...])
    hb = h2.astype(BF16)
    h2p_ref[...] = _pack(h2[:, 0:HALF], h2[:, HALF:D_MODEL])

    lo = (h2 - hb.astype(F32)).astype(BF16)
    wrh = wrh_ref[...]
    logits_t = _dot(hb, wrh) + _dot(hb, wrl_ref[...]) + _dot(lo, wrh)
    logits = logits_t.T[0:N_EXPERTS]
    scores = _sigmoid(logits)
    choice = scores + rb_ref[...]

    neg = -jnp.inf
    sub8 = lax.broadcasted_iota(I32, (GROUP_SIZE, tm), 0).astype(F32)
    grp_rows = []
    for gi in range(N_EXPERT_GROUPS):
        cg = choice[GROUP_SIZE * gi:GROUP_SIZE * (gi + 1)]
        m1 = jnp.max(cg, axis=0, keepdims=True)
        first = jnp.min(jnp.where(cg == m1, sub8, float(GROUP_SIZE)), axis=0, keepdims=True)
        m2 = jnp.max(jnp.where(sub8 == first, neg, cg), axis=0, keepdims=True)
        grp_rows.append(m1 + m2)
    gs = jnp.concatenate(grp_rows, axis=0)
    beaten = jnp.zeros((N_EXPERT_GROUPS, tm), F32)
    for gi in range(N_EXPERT_GROUPS):
        row = grp_rows[gi]
        wins = (row > gs) | ((row == gs) & (sub8 > float(gi)))
        beaten = beaten + jnp.where(wins, 1.0, 0.0)
    gsel = beaten < float(TOPK_GROUPS)
    cm = jnp.concatenate(
        [jnp.where(gsel[gi:gi + 1], choice[GROUP_SIZE * gi:GROUP_SIZE * (gi + 1)], neg)
         for gi in range(N_EXPERT_GROUPS)], axis=0)

    iota_e = lax.broadcasted_iota(I32, (N_EXPERTS, tm), 0).astype(F32)
    idx_rows, s_rows = [], []
    sel = jnp.zeros((N_EXPERTS, tm), F32)
    for _ in range(TOP_K):
        m = jnp.max(cm, axis=0, keepdims=True)
        idx = jnp.min(jnp.where(cm == m, iota_e, float(N_EXPERTS)), axis=0, keepdims=True)
        oh = iota_e == idx
        s_rows.append(jnp.sum(jnp.where(oh, scores, 0.0), axis=0, keepdims=True))
        idx_rows.append(idx)
        cm = jnp.where(oh, neg, cm)
        sel = sel + jnp.where(oh, 1.0, 0.0)

    run = run_ref[:, 0:1]
    cum = _dot(sel.astype(BF16), tri_ref[...])
    before = run + cum - sel
    rk_rows = [jnp.sum(jnp.where(iota_e == idx, before, 0.0), axis=0, keepdims=True) for idx in idx_rows]
    new_run = run + jnp.sum(sel, axis=1, keepdims=True)
    run_ref[...] = jnp.broadcast_to(new_run, run_ref.shape)
    cnt_ref[...] = jnp.broadcast_to(new_run, cnt_ref.shape)

    ssum = s_rows[0]
    for r in range(1, TOP_K):
        ssum = ssum + s_rows[r]
    denom = ssum + 1e-20
    idx_ref[...] = jnp.concatenate(idx_rows, axis=0).astype(I32)
    w_rep = jnp.concatenate([jnp.broadcast_to(sr / denom * ROUTED_SCALE, (SC_LANES, tm)) for sr in s_rows],
                            axis=0)
    wt_ref[...] = w_rep.T
    rk_ref[...] = jnp.concatenate(rk_rows, axis=0).astype(I32)


def _router_operands(g_ffn, w_router, router_bias, tm):
    wr = jnp.pad(w_router.astype(F32), ((0, 0), (0, 128 - N_EXPERTS)))
    wr_hi = wr.astype(BF16)
    wr_lo = (wr - wr_hi.astype(F32)).astype(BF16)
    tri = (np.arange(tm)[:, None] <= np.arange(tm)[None, :]).astype(np.float32)
    return (g_ffn.reshape(1, D_MODEL), wr_hi, wr_lo, router_bias.astype(F32).reshape(N_EXPERTS, 1),
            jnp.asarray(tri, BF16))


def _swiglu_packed(x_packed, wg, wu, wd):
    lo, hi = _unpack(x_packed)
    lo = lo.astype(BF16)
    hi = hi.astype(BF16)
    g = _dot(lo, wg[0:HALF, :]) + _dot(hi, wg[HALF:D_MODEL, :])
    u = _dot(lo, wu[0:HALF, :]) + _dot(hi, wu[HALF:D_MODEL, :])
    a = (g * _sigmoid(g) * u).astype(BF16)
    y = _dot(a, wd[...])
    return _pack(y[:, 0:HALF], y[:, HALF:D_MODEL])


def _shared_kernel(h2p_ref, wg_ref, wu_ref, wd_ref, *rest):
    ysh_ref = rest[-1]
    ysh_ref[...] = _swiglu_packed(h2p_ref[...], wg_ref, wu_ref, wd_ref)


def _shared(h2p, w_sh_gate, w_sh_up, w_sh_down, after=None):
    t = h2p.shape[0]
    tm = TM_FFN
    operands = (h2p, w_sh_gate.astype(BF16), w_sh_up.astype(BF16), w_sh_down.astype(BF16))
    in_specs = [pl.BlockSpec((tm, HALF), lambda i: (i, 0))]
    in_specs += [_const_spec(op.shape) for op in operands[1:]]
    if after is not None:
        operands += (after,)
        in_specs.append(pl.BlockSpec(memory_space=pl.ANY))
    return pl.pallas_call(
        _shared_kernel,
        grid=(t // tm,),
        in_specs=in_specs,
        out_specs=pl.BlockSpec((tm, HALF), lambda i: (i, 0)),
        out_shape=jax.ShapeDtypeStruct((t, HALF), I32),
        compiler_params=pltpu.CompilerParams(
            dimension_semantics=("arbitrary",), vmem_limit_bytes=VMEM_LIMIT),
        name="shared",
    )(*operands)


def _positions_kernel(pstart_ref, idx_ref, rk_ref, pos_ref):
    idx = idx_ref[...]
    acc = rk_ref[...]
    for e in range(N_EXPERTS):
        acc = acc + jnp.where(idx == e, pstart_ref[e], 0)
    pos_ref[...] = acc


def _positions(pstart, idx8, rk8):
    t = idx8.shape[1]
    tm = TM_POS
    spec = pl.BlockSpec((TOP_K, tm), lambda i, ps: (0, i))
    return pl.pallas_call(
        _positions_kernel,
        grid_spec=pltpu.PrefetchScalarGridSpec(num_scalar_prefetch=1, grid=(t // tm,),
                                               in_specs=[spec, spec], out_specs=spec),
        out_shape=jax.ShapeDtypeStruct((TOP_K, t), I32),
        name="positions",
    )(pstart, idx8, rk8)


def _sc_workers():
    info = plsc.get_sparse_core_info()
    return info.num_cores, info.num_cores * info.num_subcores


def _dispatch(pos_flat, h2p, n_rows):
    t = h2p.shape[0]
    ch = SC_SCATTER_ROWS
    n_cores, n_workers = _sc_workers()
    n_ch = t // (n_workers * ch)

    def body(pos_hbm, h_hbm, xs_hbm, idx_v, rows_v, sem):
        wid = lax.axis_index("s") * n_cores + lax.axis_index("c")

        @pl.loop(0, n_ch)
        def _(c):
            chunk = wid * n_ch + c
            pltpu.sync_copy(h_hbm.at[pl.ds(chunk * ch, ch)], rows_v)
            for k in range(TOP_K):
                pltpu.sync_copy(pos_hbm.at[pl.ds(k * t + chunk * ch, ch)], idx_v.at[k])
            copies = [pltpu.async_copy(rows_v, xs_hbm.at[idx_v.at[k]], sem) for k in range(TOP_K)]
            for cp in copies:
                cp.wait()

    return pl.kernel(
        body,
        out_type=jax.ShapeDtypeStruct((n_rows, HALF), I32),
        mesh=plsc.VectorSubcoreMesh(core_axis_name="c", subcore_axis_name="s"),
        scratch_types=[pltpu.VMEM((TOP_K, ch), I32), pltpu.VMEM((ch, HALF), I32),
                       pltpu.SemaphoreType.DMA],
        name="dispatch",
    )(pos_flat, h2p)


def _pairwise_sum(xs):
    while len(xs) > 1:
        xs = [xs[i] + xs[i + 1] for i in range(0, len(xs), 2)]
    return xs[0]


def _gather_sum(pos_flat, w_rep, ys, tok0, t):
    t_all = w_rep.shape[0]
    ch = SC_SUM_ROWS
    lanes = SC_LANES
    n_cores, n_workers = _sc_workers()
    per_w = t // n_workers
    n_ch = per_w // ch

    def body(pos_hbm, w_hbm, ys_hbm, out_hbm, idx_all, w_all, buf0, buf1, acc0, acc1, gsem0, gsem1, osem0, osem1):
        wid = lax.axis_index("s") * n_cores + lax.axis_index("c")
        base = wid * n_ch
        first = tok0 + wid * per_w
        bufs, accs, gsems, osems = (buf0, buf1), (acc0, acc1), (gsem0, gsem1), (osem0, osem1)
        for k in range(TOP_K):
            pltpu.sync_copy(pos_hbm.at[pl.ds(k * t_all + first, per_w)], idx_all.at[pl.ds(k * per_w, per_w)])
        pltpu.sync_copy(w_hbm.at[pl.ds(first, per_w)], w_all)

        def row_copy(c, s, k):
            rows = idx_all.at[pl.ds(k * per_w + c * ch, ch)]
            return pltpu.make_async_copy(ys_hbm.at[rows], bufs[s].at[k], gsems[s])

        def out_copy(c, s):
            return pltpu.make_async_copy(accs[s], out_hbm.at[pl.ds((base + c) * ch, ch)], osems[s])

        def fetch(c, s):
            for k in range(TOP_K):
                row_copy(c, s, k).start()

        def reduce_rows(c, s):
            @pl.loop(0, ch)
            def _(r):
                wks = [w_all[c * ch + r, pl.ds(k * lanes, lanes)] for k in range(TOP_K)]

                @pl.loop(0, HALF // lanes, step=2)
                def _(v):
                    for dv in range(2):
                        col = (v + dv) * lanes
                        los, his = [], []
                        for k in range(TOP_K):
                            words = bufs[s][k, r, pl.ds(col, lanes)]
                            los.append(wks[k] * lax.bitcast_convert_type(lax.shift_left(words, jnp.int32(16)), F32))
                            his.append(wks[k] * lax.bitcast_convert_type(words & jnp.int32(-65536), F32))
                        accs[s][r, pl.ds(col, lanes)] = _pairwise_sum(los)
                        accs[s][r, pl.ds(HALF + col, lanes)] = _pairwise_sum(his)

        fetch(0, 0)

        @pl.loop(0, n_ch, step=2)
        def _(c):
            for s in range(2):
                cc = c + s

                @pl.when(cc + 1 < n_ch)
                def _():
                    fetch(cc + 1, 1 - s)

                for k in range(TOP_K):
                    row_copy(cc, s, k).wait()

                @pl.when(cc >= 2)
                def _():
                    out_copy(cc - 2, s).wait()

                reduce_rows(cc, s)
                out_copy(cc, s).start()

        out_copy(n_ch - 2, 0).wait()
        out_copy(n_ch - 1, 1).wait()

    return pl.kernel(
        body,
        out_type=jax.ShapeDtypeStruct((t, D_MODEL), F32),
        mesh=plsc.VectorSubcoreMesh(core_axis_name="c", subcore_axis_name="s"),
        scratch_types=[pltpu.VMEM((TOP_K * per_w,), I32), pltpu.VMEM((per_w, TOP_K * lanes), F32),
                       pltpu.VMEM((TOP_K, ch, HALF), I32), pltpu.VMEM((TOP_K, ch, HALF), I32),
                       pltpu.VMEM((ch, D_MODEL), F32), pltpu.VMEM((ch, D_MODEL), F32),
                       pltpu.SemaphoreType.DMA, pltpu.SemaphoreType.DMA,
                       pltpu.SemaphoreType.DMA, pltpu.SemaphoreType.DMA],
        compiler_params=pltpu.CompilerParams(needs_layout_passes=False),
        name="gather_sum",
    )(pos_flat, w_rep, ys)


def _expert_kernel(nblk_ref, blk0_ref, cnt_ref, wg_ref, wu_ref, wd_ref, xs_hbm, after_ref, ys_hbm,
                   wg_s, wu_s, wd_s, xbuf, ybuf, in_sem, out_sem):
    del after_ref
    e = pl.program_id(0)
    nb = nblk_ref[e]
    b0 = blk0_ref[e]
    cnt = cnt_ref[e]
    n_all = blk0_ref[N_EXPERTS]
    ahead = EXPERT_SLOTS - 2
    wg_s[...] = wg_ref[0].astype(BF16)
    wu_s[...] = wu_ref[0].astype(BF16)
    wd_s[...] = wd_ref[0].astype(BF16)

    def slot_rows(g, n_blocks=1):
        start = (g & (EXPERT_SLOTS - 1)) * MOE_BLOCK
        return pl.ds(pl.multiple_of(start, MOE_BLOCK), n_blocks * MOE_BLOCK)

    def rows_of(g):
        return pl.ds(pl.multiple_of(g * MOE_BLOCK, MOE_BLOCK), MOE_BLOCK)

    def in_copy(g):
        return pltpu.make_async_copy(xs_hbm.at[rows_of(g)], xbuf.at[slot_rows(g)],
                                     in_sem.at[g & (EXPERT_SLOTS - 1)])

    def out_copy(g):
        return pltpu.make_async_copy(ybuf.at[slot_rows(g)], ys_hbm.at[rows_of(g)],
                                     out_sem.at[g & (EXPERT_SLOTS - 1)])

    @pl.when(e == 0)
    def _():
        for g in range(ahead):
            @pl.when(g < n_all)
            def _():
                in_copy(g).start()

    def arrive(g):
        in_copy(g).wait()

        @pl.when(g + ahead < n_all)
        def _():
            in_copy(g + ahead).start()

        @pl.when(g >= EXPERT_SLOTS)
        def _():
            out_copy(g - EXPERT_SLOTS).wait()

    def run_blocks(j, n_blocks):
        g = b0 + j
        for i in range(n_blocks):
            arrive(g + i)
        rows = lax.broadcasted_iota(I32, (n_blocks * MOE_BLOCK, HALF), 0)
        x = jnp.where(rows < cnt - j * MOE_BLOCK, xbuf[slot_rows(g, n_blocks)], 0)
        ybuf[slot_rows(g, n_blocks)] = _swiglu_packed(x, wg_s, wu_s, wd_s)
        for i in range(n_blocks):
            out_copy(g + i).start()

    lead = jnp.where(nb > 0, b0 & 1, 0)

    @pl.when(lead == 1)
    def _():
        run_blocks(0, 1)

    n_pairs = (nb - lead) // 2

    def one_pair(p, carry):
        run_blocks(lead + 2 * p, 2)
        return carry

    lax.fori_loop(0, n_pairs, one_pair, 0)

    @pl.when(nb - lead - 2 * n_pairs == 1)
    def _():
        run_blocks(nb - 1, 1)

    @pl.when(e == N_EXPERTS - 1)
    def _():
        for back in range(1, EXPERT_SLOTS + 1):
            @pl.when(n_all >= back)
            def _():
                out_copy(n_all - back).wait()


def _experts(n_blk, blk0, counts, n_rows, xs, w_gate, w_up, w_down, after):
    wspec_in = pl.BlockSpec((1, D_MODEL, EXPERT_DIM), lambda e, *_: (e, 0, 0))
    grid_spec = pltpu.PrefetchScalarGridSpec(
        num_scalar_prefetch=3,
        grid=(N_EXPERTS,),
        in_specs=[wspec_in, wspec_in,
                  pl.BlockSpec((1, EXPERT_DIM, D_MODEL), lambda e, *_: (e, 0, 0)),
                  pl.BlockSpec(memory_space=pl.ANY),
                  pl.BlockSpec(memory_space=pl.ANY)],
        out_specs=pl.BlockSpec(memory_space=pl.ANY),
        scratch_shapes=[pltpu.VMEM((D_MODEL, EXPERT_DIM), BF16), pltpu.VMEM((D_MODEL, EXPERT_DIM), BF16),
                        pltpu.VMEM((EXPERT_DIM, D_MODEL), BF16),
                        pltpu.VMEM((EXPERT_SLOTS * MOE_BLOCK, HALF), I32),
                        pltpu.VMEM((EXPERT_SLOTS * MOE_BLOCK, HALF), I32),
                        pltpu.SemaphoreType.DMA((EXPERT_SLOTS,)), pltpu.SemaphoreType.DMA((EXPERT_SLOTS,))],
    )
    return pl.pallas_call(
        _expert_kernel,
        grid_spec=grid_spec,
        out_shape=jax.ShapeDtypeStruct((n_rows, HALF), I32),
        compiler_params=pltpu.CompilerParams(
            dimension_semantics=("arbitrary",), vmem_limit_bytes=VMEM_LIMIT),
        name="experts",
    )(n_blk, blk0, counts, w_gate, w_up, w_down, xs, after)


def _combine_kernel(x1_ref, ysh_ref, moe_ref, p_ref, gple_ref, wpg_ref, wpp_ref, gpost_ref, *rest):
    out_ref = rest[-1]
    proj = _rms(_dot(p_ref[...].astype(BF16), wpp_ref[...]), gpost_ref[...])
    sh_lo, sh_hi = _unpack(ysh_ref[...])
    x2 = x1_ref[...] + moe_ref[...] + jnp.concatenate([sh_lo, sh_hi], axis=1)
    gate = _sigmoid(_dot(_rms(x2, gple_ref[...]).astype(BF16), wpg_ref[...]))
    out_ref[...] = x2 + gate * proj


def _combine(x1, ysh, moe, p2d, g_ple, w_ple_gate, w_ple_proj, g_ple_post, tok0, tok0_all, prev_out):
    t = p2d.shape[0]
    tm = TM_COMBINE
    steps = moe.shape[0] // tm
    off = tok0 // tm
    off_all = tok0_all // tm
    operands = [x1, ysh, moe, p2d, g_ple.reshape(1, D_MODEL), w_ple_gate.astype(BF16),
                w_ple_proj.astype(BF16), g_ple_post.reshape(1, D_MODEL)]
    in_specs = [pl.BlockSpec((tm, D_MODEL), lambda i: (off + i, 0)),
                pl.BlockSpec((tm, HALF), lambda i: (off + i, 0)),
                pl.BlockSpec((tm, D_MODEL), lambda i: (i, 0)),
                pl.BlockSpec((tm, PLE_DIM), lambda i: (off_all + i, 0)),
                _const_spec((1, D_MODEL)),
                _const_spec((D_MODEL, D_MODEL)),
                _const_spec((PLE_DIM, D_MODEL)),
                _const_spec((1, D_MODEL))]
    aliases = {}
    if prev_out is not None:
        aliases = {len(operands): 0}
        operands.append(prev_out)
        in_specs.append(pl.BlockSpec(memory_space=pl.ANY))
    return pl.pallas_call(
        _combine_kernel,
        grid=(steps,),
        in_specs=in_specs,
        out_specs=pl.BlockSpec((tm, D_MODEL), lambda i: (off_all + i, 0)),
        out_shape=jax.ShapeDtypeStruct((t, D_MODEL), F32),
        input_output_aliases=aliases,
        compiler_params=pltpu.CompilerParams(
            dimension_semantics=("arbitrary",), vmem_limit_bytes=VMEM_LIMIT),
        name="combine",
    )(*operands)


def _layer(x2d, p2d, batch, seq, g_mix, w_in, w_pool_mix, pool_scale, w_branch_a, q_gain, k_gain,
           attn_sinks, w_branch_b, w_gate, b_gate, w_out, g_ffn, w_router, router_bias, w_exp_gate,
           w_exp_up, w_exp_down, w_sh_gate, w_sh_up, w_sh_down, g_ple, w_ple_gate, w_ple_proj,
           g_ple_post):
    bh = batch // PIPELINE_HALVES
    th = bh * seq
    n_rows = (-(-(th * TOP_K) // MOE_BLOCK) + N_EXPERTS) * MOE_BLOCK

    halves = []
    after = None
    for h in range(PIPELINE_HALVES):
        x1, h2p, idx8, w_rep, rk8, cnt = _mixer(
            x2d, h * bh, bh, seq, g_mix, w_in, w_pool_mix, pool_scale, w_branch_a, q_gain, k_gain,
            attn_sinks, w_branch_b, w_gate, b_gate, w_out, g_ffn, w_router, router_bias, after=after)
        counts = cnt[:, 0].astype(I32)
        padded = (counts + MOE_BLOCK - 1) // MOE_BLOCK * MOE_BLOCK
        pend = jnp.cumsum(padded)
        pstart = pend - padded
        pos = _positions(pstart.astype(I32), idx8, rk8).reshape(-1)
        xs = _dispatch(pos, h2p, n_rows)
        blk0 = jnp.concatenate([pstart, pend[-1:]]) // MOE_BLOCK
        halves.append((x1, h2p, w_rep, pos, xs, (padded // MOE_BLOCK).astype(I32), blk0.astype(I32), counts))
        after = pos

    ys_all = []
    for (x1, h2p, w_rep, pos, xs, n_blk, blk0, counts) in halves:
        ys_all.append(_experts(n_blk, blk0, counts, n_rows, xs, w_exp_gate, w_exp_up, w_exp_down,
                               ys_all[-1] if ys_all else xs))

    out = None
    tc = th // COMBINE_CHUNKS
    for h, (x1, h2p, w_rep, pos, xs, n_blk, blk0, counts) in enumerate(halves):
        ys = ys_all[h]
        ysh = _shared(h2p, w_sh_gate, w_sh_up, w_sh_down, after=ys_all[-1] if out is None else out)
        for c in range(COMBINE_CHUNKS):
            moe = _gather_sum(pos, w_rep, ys, c * tc, tc)
            out = _combine(x1, ysh, moe, p2d, g_ple, w_ple_gate, w_ple_proj, g_ple_post,
                           c * tc, h * th + c * tc, out)
    return out


def kernel(x, p, g_mix, w_in, w_pool_mix, pool_scale, w_branch_a, q_gain, k_gain, attn_sinks,
           w_branch_b, w_gate, b_gate, w_out, g_ffn, w_router, router_bias, w_exp_gate, w_exp_up,
           w_exp_down, w_sh_gate, w_sh_up, w_sh_down, g_ple, w_ple_gate, w_ple_proj, g_ple_post):
    batch, seq, d = x.shape
    depth = p.shape[0]
    x2d = x.reshape(batch * seq, d)
    for i in range(depth):
        x2d = _layer(x2d, p[i].reshape(batch * seq, PLE_DIM), batch, seq, g_mix[i], w_in[i],
                     w_pool_mix[i], pool_scale[i], w_branch_a[i], q_gain[i], k_gain[i], attn_sinks[i],
                     w_branch_b[i], w_gate[i], b_gate[i], w_out[i], g_ffn[i], w_router[i],
                     router_bias[i], w_exp_gate[i], w_exp_up[i], w_exp_down[i], w_sh_gate[i],
                     w_sh_up[i], w_sh_down[i], g_ple[i], w_ple_gate[i], w_ple_proj[i], g_ple_post[i])
    return x2d.reshape(batch, seq, d)
```

```python
import numpy as np
import jax
import jax.numpy as jnp
from jax import lax
from jax.experimental import pallas as pl
from jax.experimental.pallas import tpu as pltpu
from jax.experimental.pallas import tpu_sc as plsc

F32 = jnp.float32
BF16 = jnp.bfloat16
I32 = jnp.int32

D_MODEL = 1024
PLE_DIM = 256
POOL_WINDOWS = (2, 4, 8, 16)
POOL_GROUP_DIM = 128
POOL_DIM = 512
N_HEADS = 8
N_KV_HEADS = 2
Q_GROUP = 4
HEAD_DIM = 64
HEAD_PAD = 128
Q_DIM = 512
KV_DIM = 128
ATTN_BLOCK = 128
N_EXPERTS = 64
N_EXPERT_GROUPS = 8
GROUP_SIZE = 8
TOPK_GROUPS = 4
TOP_K = 8
EXPERT_DIM = 256
ROUTED_SCALE = 2.5
MOE_BLOCK = 512
EXPERT_SLOTS = 8
EPS = 1e-6
HALF = D_MODEL // 2
MASKED = -1e30

COL_Q = POOL_DIM
COL_K = COL_Q + N_HEADS * HEAD_PAD
COL_V = COL_K + N_KV_HEADS * HEAD_PAD
IN_PAD = COL_V + N_KV_HEADS * HEAD_PAD
HEAD_ORDER = (0, 2, 1, 3)

TM_MIX = 512
TM_FFN = 1024
TM_COMBINE = 1024
TM_POS = 4096
COMBINE_CHUNKS = 2
PIPELINE_HALVES = 2
SC_SCATTER_ROWS = 128
SC_SUM_ROWS = 8
SC_LANES = 16
POOL_TAIL = 8
VMEM_LIMIT = 56 * 1024 * 1024

_NT = (((1,), (1,)), ((), ()))


def _dot(a, b):
    return jnp.dot(a, b, preferred_element_type=F32)


def _rms(x, g):
    ms = jnp.mean(x * x, axis=-1, keepdims=True)
    return x * lax.rsqrt(ms + EPS) * g


def _sigmoid(x):
    return 0.5 * jnp.tanh(0.5 * x) + 0.5


def _pack(lo, hi):
    lo_bits = lax.bitcast_convert_type(lo.astype(BF16).astype(F32), I32)
    hi_bits = lax.bitcast_convert_type(hi.astype(BF16).astype(F32), I32)
    return (hi_bits & jnp.int32(-65536)) | lax.shift_right_logical(lo_bits, jnp.int32(16))


def _unpack(w):
    lo = lax.bitcast_convert_type(lax.shift_left(w, jnp.int32(16)), F32)
    hi = lax.bitcast_convert_type(w & jnp.int32(-65536), F32)
    return lo, hi


def _mixer_kernel(x_ref, gmix_ref, win_ref, qg_ref, kg_ref, bias_ref, sink_ref, ones_ref, wmix_ref,
                  pscale_ref, wa_ref, wb_ref, wg_ref, bg_ref, wout_ref,
                  gffn_ref, wrh_ref, wrl_ref, rb_ref, tri_ref, *rest):
    (x1_ref, h2p_ref, idx_ref, wt_ref, rk_ref, cnt_ref,
     kbuf, vt_buf, pbuf, zbt_buf, run_ref) = rest[-11:]
    s = pl.program_id(1)
    tm = x_ref.shape[0]

    @pl.when(s == 0)
    def _():
        kbuf[0:ATTN_BLOCK, :] = jnp.zeros((ATTN_BLOCK, kbuf.shape[1]), BF16)
        vt_buf[:, 0:ATTN_BLOCK] = jnp.zeros((vt_buf.shape[0], ATTN_BLOCK), BF16)
        pbuf[:, 0:POOL_TAIL, :] = jnp.zeros((pbuf.shape[0], POOL_TAIL, POOL_DIM), F32)

    x = x_ref[...]
    hb = _rms(x, gmix_ref[...]).astype(BF16)
    u = _dot(hb, win_ref[...])

    up = u[:, 0:POOL_DIM]
    t0 = POOL_TAIL
    pbuf[0, t0:t0 + tm, :] = up
    s2 = up + pbuf[0, t0 - 1:t0 - 1 + tm, :]
    pbuf[1, t0:t0 + tm, 128:512] = s2[:, 128:512]
    s4 = s2[:, 128:512] + pbuf[1, t0 - 2:t0 - 2 + tm, 128:512]
    pbuf[2, t0:t0 + tm, 256:512] = s4[:, 128:384]
    s8 = s4[:, 128:384] + pbuf[2, t0 - 4:t0 - 4 + tm, 256:512]
    pbuf[3, t0:t0 + tm, 384:512] = s8[:, 128:256]
    s16 = s8[:, 128:256] + pbuf[3, t0 - 8:t0 - 8 + tm, 384:512]
    for lvl in range(4):
        pbuf[lvl, 0:POOL_TAIL, 128 * lvl:512] = pbuf[lvl, tm:tm + POOL_TAIL, 128 * lvl:512]
    wsums = (s2[:, 0:128], s4[:, 0:128], s8[:, 0:128], s16)
    tpos = (s * tm).astype(F32) + lax.broadcasted_iota(I32, (tm, 1), 0).astype(F32)
    za_parts = []
    for g, w in enumerate(POOL_WINDOWS):
        inv_count = 1.0 / jnp.minimum(tpos + 1.0, float(w))
        pooled = wsums[g] * inv_count - up[:, 128 * g:128 * (g + 1)]
        mixed = _dot(pooled.astype(BF16), wmix_ref[g]) * pscale_ref[:, 128 * g:128 * (g + 1)]
        za_parts.append(mixed.astype(BF16))
    za = jnp.concatenate(za_parts, axis=1)

    ones2 = ones_ref[...]
    q = u[:, COL_Q:COL_K]
    q2 = (q * q).astype(BF16)
    qss = jnp.concatenate([_dot(q2[:, 256 * c:256 * (c + 1)], ones2) for c in range(N_HEADS // 2)], axis=1)
    qn = (q * lax.rsqrt(qss * (1.0 / HEAD_DIM) + EPS) * qg_ref[...]).astype(BF16)
    k = u[:, COL_K:COL_V]
    kss = _dot((k * k).astype(BF16), ones2)
    kbuf[ATTN_BLOCK:ATTN_BLOCK + tm, :] = (k * lax.rsqrt(kss * (1.0 / HEAD_DIM) + EPS) * kg_ref[...]).astype(BF16)
    vt_buf[:, ATTN_BLOCK:ATTN_BLOCK + tm] = u[:, COL_V:IN_PAD].T.astype(BF16)

    nq = Q_GROUP * ATTN_BLOCK
    key_j = lax.broadcasted_iota(I32, (ATTN_BLOCK, nq), 0)
    qry_i = lax.broadcasted_iota(I32, (ATTN_BLOCK, nq), 1) & (ATTN_BLOCK - 1)
    from_prev = key_j > qry_i
    first = jnp.where(s == 0, 1, 0)
    for n in range(tm // ATTN_BLOCK):
        r0 = ATTN_BLOCK * n
        for kv in range(N_KV_HEADS):
            c0 = HEAD_PAD * kv
            qs = jnp.concatenate(
                [qn[r0:r0 + ATTN_BLOCK, HEAD_PAD * (kv * Q_GROUP + g):HEAD_PAD * (kv * Q_GROUP + g + 1)]
                 for g in HEAD_ORDER], axis=0)
            kk = kbuf[r0:r0 + 2 * ATTN_BLOCK, c0:c0 + HEAD_PAD]
            st = lax.dot_general(kk, qs, _NT, preferred_element_type=F32)
            sc = jnp.where(from_prev, st[0:ATTN_BLOCK], st[ATTN_BLOCK:2 * ATTN_BLOCK])
            sc = sc + (bias_ref[first, kv] if n == 0 else bias_ref[0, kv])
            sink = sink_ref[kv]
            m = jnp.maximum(jnp.max(sc, axis=0, keepdims=True), sink)
            e = jnp.exp(sc - m)
            inv_den = 1.0 / (jnp.sum(e, axis=0, keepdims=True) + jnp.exp(sink - m))
            pt = jnp.concatenate([jnp.where(from_prev, e, 0.0), jnp.where(from_prev, 0.0, e)],
                                 axis=0).astype(BF16)
            v_even = vt_buf[c0:c0 + HEAD_PAD, r0:r0 + 2 * ATTN_BLOCK]
            v_odd = jnp.concatenate([v_even[HEAD_DIM:HEAD_PAD], v_even[0:HEAD_DIM]], axis=0)
            half = 2 * ATTN_BLOCK
            o = (_dot(v_even, pt[:, 0:half]) * inv_den[:, 0:half]
                 + _dot(v_odd, pt[:, half:2 * half]) * inv_den[:, half:2 * half])
            ch = kv * Q_GROUP * HEAD_DIM
            zbt_buf[ch:ch + HEAD_PAD, r0:r0 + ATTN_BLOCK] = o[:, 0:ATTN_BLOCK]
            zbt_buf[ch + HEAD_PAD:ch + 2 * HEAD_PAD, r0:r0 + ATTN_BLOCK] = o[:, ATTN_BLOCK:half]
    kbuf[0:ATTN_BLOCK, :] = kbuf[tm:tm + ATTN_BLOCK, :]
    vt_buf[:, 0:ATTN_BLOCK] = vt_buf[:, tm:tm + ATTN_BLOCK]

    y_a = _dot(za, wa_ref[...])
    y_b = _dot(zbt_buf[...].T.astype(BF16), wb_ref[...])
    g_a = _sigmoid(_dot(hb, wg_ref[:, 0:D_MODEL]) + bg_ref[:, 0:D_MODEL])
    merged = g_a * y_a
    g_b = _sigmoid(_dot(hb, wg_ref[:, D_MODEL:2 * D_MODEL]) + bg_ref[:, D_MODEL:2 * D_MODEL])
    merged = (merged + g_b * y_b).astype(BF16)
    x1 = x + _dot(merged, wout_ref[...])
    x1_ref[...] = x1

    is_first = (pl.program_id(0) == 0) & (s == 0)
    _route_tile(x1, is_first, gffn_ref, wrh_ref, wrl_ref, rb_ref, tri_ref,
                h2p_ref, idx_ref, wt_ref, rk_ref, cnt_ref, run_ref)


def _attn_tables(attn_sinks):
    slopes = 2.0 ** (-8.0 * (np.arange(N_HEADS) + 1) / N_HEADS)
    j = np.arange(ATTN_BLOCK)[:, None]
    i = np.arange(ATTN_BLOCK)[None, :]
    from_prev = j > i
    dist = np.where(from_prev, ATTN_BLOCK + i - j, i - j)
    bias = np.empty((2, N_KV_HEADS, ATTN_BLOCK, Q_GROUP * ATTN_BLOCK), np.float32)
    for first in range(2):
        ok = ~from_prev if first else np.ones_like(from_prev)
        for kv in range(N_KV_HEADS):
            for slot, g in enumerate(HEAD_ORDER):
                sl = np.float32(slopes[kv * Q_GROUP + g])
                val = -(sl * dist.astype(np.float32))
                bias[first, kv, :, slot * ATTN_BLOCK:(slot + 1) * ATTN_BLOCK] = np.where(ok, val, MASKED)
    sink = attn_sinks.astype(F32).reshape(N_KV_HEADS, Q_GROUP)[:, np.asarray(HEAD_ORDER)]
    sink = jnp.repeat(sink, ATTN_BLOCK, axis=1)
    return jnp.asarray(bias), sink.reshape(N_KV_HEADS, 1, Q_GROUP * ATTN_BLOCK)


def _pad_heads(w, n_heads):
    k = w.shape[0]
    w = w.reshape(k, n_heads, HEAD_DIM)
    w = jnp.pad(w, ((0, 0), (0, 0), (0, HEAD_PAD - HEAD_DIM)))
    return w.reshape(k, n_heads * HEAD_PAD)


def _const_spec(shape):
    nd = len(shape)
    return pl.BlockSpec(shape, lambda *_: (0,) * nd)


def _mixer(x2d, batch0, batch, seq, g_mix, w_in, w_pool_mix, pool_scale, w_branch_a, q_gain, k_gain,
           attn_sinks, w_branch_b, w_gate, b_gate, w_out, g_ffn, w_router, router_bias, after=None):
    t = batch * seq
    tm = TM_MIX
    ns = seq // tm
    w_q = w_in[:, POOL_DIM:POOL_DIM + Q_DIM]
    w_k = w_in[:, POOL_DIM + Q_DIM:POOL_DIM + Q_DIM + KV_DIM]
    w_v = w_in[:, POOL_DIM + Q_DIM + KV_DIM:]
    win_p = jnp.concatenate([w_in[:, :POOL_DIM], _pad_heads(w_q, N_HEADS), _pad_heads(w_k, N_KV_HEADS),
                             _pad_heads(w_v, N_KV_HEADS)], axis=1).astype(BF16)
    qg = jnp.tile(jnp.pad(q_gain * (HEAD_DIM ** -0.5), (0, HEAD_PAD - HEAD_DIM)), N_HEADS).reshape(1, -1)
    kg = jnp.tile(jnp.pad(k_gain, (0, HEAD_PAD - HEAD_DIM)), N_KV_HEADS).reshape(1, -1)
    bias, sink = _attn_tables(attn_sinks)
    ones2 = jnp.asarray(np.kron(np.eye(2), np.ones((HEAD_PAD, HEAD_PAD))), BF16)
    operands = (x2d, g_mix.reshape(1, D_MODEL), win_p, qg, kg, bias, sink, ones2, w_pool_mix.astype(BF16),
                pool_scale.reshape(1, POOL_DIM), w_branch_a.astype(BF16), w_branch_b.astype(BF16),
                w_gate.astype(BF16), b_gate.reshape(1, 2 * D_MODEL), w_out.astype(BF16))
    operands += _router_operands(g_ffn, w_router, router_bias, tm)
    in_specs = [pl.BlockSpec((tm, D_MODEL), lambda b, s: ((batch0 + b) * ns + s, 0))]
    in_specs += [_const_spec(op.shape) for op in operands[1:]]
    if after is not None:
        operands += (after,)
        in_specs.append(pl.BlockSpec(memory_space=pl.ANY))
    row8 = pl.BlockSpec((TOP_K, tm), lambda b, s: (0, b * ns + s))
    return pl.pallas_call(
        _mixer_kernel,
        grid=(batch, ns),
        in_specs=in_specs,
        out_specs=[pl.BlockSpec((tm, D_MODEL), lambda b, s: (b * ns + s, 0)),
                   pl.BlockSpec((tm, HALF), lambda b, s: (b * ns + s, 0)),
                   row8,
                   pl.BlockSpec((tm, TOP_K * SC_LANES), lambda b, s: (b * ns + s, 0)),
                   row8,
                   pl.BlockSpec((N_EXPERTS, 128), lambda b, s: (0, 0))],
        out_shape=[jax.ShapeDtypeStruct((t, D_MODEL), F32),
                   jax.ShapeDtypeStruct((t, HALF), I32),
                   jax.ShapeDtypeStruct((TOP_K, t), I32),
                   jax.ShapeDtypeStruct((t, TOP_K * SC_LANES), F32),
                   jax.ShapeDtypeStruct((TOP_K, t), I32),
                   jax.ShapeDtypeStruct((N_EXPERTS, 128), F32)],
        scratch_shapes=[
            pltpu.VMEM((ATTN_BLOCK + tm, N_KV_HEADS * HEAD_PAD), BF16),
            pltpu.VMEM((N_KV_HEADS * HEAD_PAD, ATTN_BLOCK + tm), BF16),
            pltpu.VMEM((4, POOL_TAIL + tm, POOL_DIM), F32),
            pltpu.VMEM((Q_DIM, tm), F32),
            pltpu.VMEM((N_EXPERTS, 128), F32),
        ],
        compiler_params=pltpu.CompilerParams(
            dimension_semantics=("arbitrary", "arbitrary"), vmem_limit_bytes=VMEM_LIMIT),
        name="mixer",
    )(*operands)


def _route_tile(x1, is_first, gffn_ref, wrh_ref, wrl_ref, rb_ref, tri_ref,
                h2p_ref, idx_ref, wt_ref, rk_ref, cnt_ref, run_ref):
    tm = x1.shape[0]

    @pl.when(is_first)
    def _():
        run_ref[...] = jnp.zeros_like(run_ref)

    h2 = _rms(x1, gffn_ref[...])
    hb = h2.astype(BF16)
    h2p_ref[...] = _pack(h2[:, 0:HALF], h2[:, HALF:D_MODEL])

    logits = (lax.dot_general(wrh_ref[...], hb, _NT, preferred_element_type=F32)
              + lax.dot_general(wrl_ref[...], hb, _NT, preferred_element_type=F32))
    scores = _sigmoid(logits)
    choice = scores + rb_ref[...]

    neg = -jnp.inf
    sub8 = lax.broadcasted_iota(I32, (GROUP_SIZE, tm), 0).astype(F32)
    grp_rows = []
    for gi in range(N_EXPERT_GROUPS):
        cg = choice[GROUP_SIZE * gi:GROUP_SIZE * (gi + 1)]
        m1 = jnp.max(cg, axis=0, keepdims=True)
        first = jnp.min(jnp.where(cg == m1, sub8, float(GROUP_SIZE)), axis=0, keepdims=True)
        m2 = jnp.max(jnp.where(sub8 == first, neg, cg), axis=0, keepdims=True)
        grp_rows.append(m1 + m2)
    gs = jnp.concatenate(grp_rows, axis=0)
    beaten = jnp.zeros((N_EXPERT_GROUPS, tm), F32)
    for gi in range(N_EXPERT_GROUPS):
        row = grp_rows[gi]
        wins = (row > gs) | ((row == gs) & (sub8 > float(gi)))
        beaten = beaten + jnp.where(wins, 1.0, 0.0)
    gsel = beaten < float(TOPK_GROUPS)
    cm = jnp.concatenate(
        [jnp.where(gsel[gi:gi + 1], choice[GROUP_SIZE * gi:GROUP_SIZE * (gi + 1)], neg)
         for gi in range(N_EXPERT_GROUPS)], axis=0)

    iota_e = lax.broadcasted_iota(I32, (N_EXPERTS, tm), 0).astype(F32)
    idx_rows, s_rows = [], []
    sel = jnp.zeros((N_EXPERTS, tm), F32)
    for _ in range(TOP_K):
        m = jnp.max(cm, axis=0, keepdims=True)
        idx = jnp.min(jnp.where(cm == m, iota_e, float(N_EXPERTS)), axis=0, keepdims=True)
        oh = iota_e == idx
        s_rows.append(jnp.sum(jnp.where(oh, scores, 0.0), axis=0, keepdims=True))
        idx_rows.append(idx)
        cm = jnp.where(oh, neg, cm)
        sel = sel + jnp.where(oh, 1.0, 0.0)

    run = run_ref[:, 0:1]
    cum = _dot(sel.astype(BF16), tri_ref[...])
    before = run + cum - sel
    rk_rows = [jnp.sum(jnp.where(iota_e == idx, before, 0.0), axis=0, keepdims=True) for idx in idx_rows]
    new_run = run + jnp.sum(sel, axis=1, keepdims=True)
    run_ref[...] = jnp.broadcast_to(new_run, run_ref.shape)
    cnt_ref[...] = jnp.broadcast_to(new_run, cnt_ref.shape)

    ssum = s_rows[0]
    for r in range(1, TOP_K):
        ssum = ssum + s_rows[r]
    denom = ssum + 1e-20
    idx_ref[...] = jnp.concatenate(idx_rows, axis=0).astype(I32)
    w_rep = jnp.concatenate([jnp.broadcast_to(sr / denom * ROUTED_SCALE, (SC_LANES, tm)) for sr in s_rows],
                            axis=0)
    wt_ref[...] = w_rep.T
    rk_ref[...] = jnp.concatenate(rk_rows, axis=0).astype(I32)


def _router_operands(g_ffn, w_router, router_bias, tm):
    wr_t = w_router.astype(F32).T
    wr_hi = wr_t.astype(BF16)
    wr_lo = (wr_t - wr_hi.astype(F32)).astype(BF16)
    tri = (np.arange(tm)[:, None] <= np.arange(tm)[None, :]).astype(np.float32)
    return (g_ffn.reshape(1, D_MODEL), wr_hi, wr_lo, router_bias.astype(F32).reshape(N_EXPERTS, 1),
            jnp.asarray(tri, BF16))


def _swiglu_packed(x_packed, wg, wu, wd):
    lo, hi = _unpack(x_packed)
    lo = lo.astype(BF16)
    hi = hi.astype(BF16)
    g = _dot(lo, wg[0:HALF, :]) + _dot(hi, wg[HALF:D_MODEL, :])
    u = _dot(lo, wu[0:HALF, :]) + _dot(hi, wu[HALF:D_MODEL, :])
    a = (g * _sigmoid(g) * u).astype(BF16)
    y = _dot(a, wd[...])
    return _pack(y[:, 0:HALF], y[:, HALF:D_MODEL])


def _shared_kernel(h2p_ref, wg_ref, wu_ref, wd_ref, *rest):
    ysh_ref = rest[-1]
    ysh_ref[...] = _swiglu_packed(h2p_ref[...], wg_ref, wu_ref, wd_ref)


def _shared(h2p, w_sh_gate, w_sh_up, w_sh_down, after=None):
    t = h2p.shape[0]
    tm = TM_FFN
    operands = (h2p, w_sh_gate.astype(BF16), w_sh_up.astype(BF16), w_sh_down.astype(BF16))
    in_specs = [pl.BlockSpec((tm, HALF), lambda i: (i, 0))]
    in_specs += [_const_spec(op.shape) for op in operands[1:]]
    if after is not None:
        operands += (after,)
        in_specs.append(pl.BlockSpec(memory_space=pl.ANY))
    return pl.pallas_call(
        _shared_kernel,
        grid=(t // tm,),
        in_specs=in_specs,
        out_specs=pl.BlockSpec((tm, HALF), lambda i: (i, 0)),
        out_shape=jax.ShapeDtypeStruct((t, HALF), I32),
        compiler_params=pltpu.CompilerParams(
            dimension_semantics=("arbitrary",), vmem_limit_bytes=VMEM_LIMIT),
        name="shared",
    )(*operands)


def _positions_kernel(pstart_ref, idx_ref, rk_ref, pos_ref):
    idx = idx_ref[...]
    acc = rk_ref[...]
    for e in range(N_EXPERTS):
        acc = acc + jnp.where(idx == e, pstart_ref[e], 0)
    pos_ref[...] = acc


def _positions(pstart, idx8, rk8):
    t = idx8.shape[1]
    tm = TM_POS
    spec = pl.BlockSpec((TOP_K, tm), lambda i, ps: (0, i))
    return pl.pallas_call(
        _positions_kernel,
        grid_spec=pltpu.PrefetchScalarGridSpec(num_scalar_prefetch=1, grid=(t // tm,),
                                               in_specs=[spec, spec], out_specs=spec),
        out_shape=jax.ShapeDtypeStruct((TOP_K, t), I32),
        name="positions",
    )(pstart, idx8, rk8)


def _sc_workers():
    info = plsc.get_sparse_core_info()
    return info.num_cores, info.num_cores * info.num_subcores


def _dispatch(pos_flat, h2p, n_rows):
    t = h2p.shape[0]
    ch = SC_SCATTER_ROWS
    n_cores, n_workers = _sc_workers()
    n_ch = t // (n_workers * ch)

    def body(pos_hbm, h_hbm, xs_hbm, idx_v, rows_v, sem):
        wid = lax.axis_index("s") * n_cores + lax.axis_index("c")

        @pl.loop(0, n_ch)
        def _(c):
            chunk = wid * n_ch + c
            pltpu.sync_copy(h_hbm.at[pl.ds(chunk * ch, ch)], rows_v)
            for k in range(TOP_K):
                pltpu.sync_copy(pos_hbm.at[pl.ds(k * t + chunk * ch, ch)], idx_v.at[k])
            copies = [pltpu.async_copy(rows_v, xs_hbm.at[idx_v.at[k]], sem) for k in range(TOP_K)]
            for cp in copies:
                cp.wait()

    return pl.kernel(
        body,
        out_type=jax.ShapeDtypeStruct((n_rows, HALF), I32),
        mesh=plsc.VectorSubcoreMesh(core_axis_name="c", subcore_axis_name="s"),
        scratch_types=[pltpu.VMEM((TOP_K, ch), I32), pltpu.VMEM((ch, HALF), I32),
                       pltpu.SemaphoreType.DMA],
        name="dispatch",
    )(pos_flat, h2p)


def _pairwise_sum(xs):
    while len(xs) > 1:
        xs = [xs[i] + xs[i + 1] for i in range(0, len(xs), 2)]
    return xs[0]


def _gather_sum(pos_flat, w_rep, ys, tok0, t):
    t_all = w_rep.shape[0]
    ch = SC_SUM_ROWS
    lanes = SC_LANES
    n_cores, n_workers = _sc_workers()
    per_w = t // n_workers
    n_ch = per_w // ch

    def body(pos_hbm, w_hbm, ys_hbm, out_hbm, idx_all, w_all, buf0, buf1, acc0, acc1, gsem0, gsem1, osem0, osem1):
        wid = lax.axis_index("s") * n_cores + lax.axis_index("c")
        base = wid * n_ch
        first = tok0 + wid * per_w
        bufs, accs, gsems, osems = (buf0, buf1), (acc0, acc1), (gsem0, gsem1), (osem0, osem1)
        for k in range(TOP_K):
            pltpu.sync_copy(pos_hbm.at[pl.ds(k * t_all + first, per_w)], idx_all.at[pl.ds(k * per_w, per_w)])
        pltpu.sync_copy(w_hbm.at[pl.ds(first, per_w)], w_all)

        def row_copy(c, s, k):
            rows = idx_all.at[pl.ds(k * per_w + c * ch, ch)]
            return pltpu.make_async_copy(ys_hbm.at[rows], bufs[s].at[k], gsems[s])

        def out_copy(c, s):
            return pltpu.make_async_copy(accs[s], out_hbm.at[pl.ds((base + c) * ch, ch)], osems[s])

        def fetch(c, s):
            for k in range(TOP_K):
                row_copy(c, s, k).start()

        def reduce_rows(c, s):
            @pl.loop(0, ch)
            def _(r):
                wks = [w_all[c * ch + r, pl.ds(k * lanes, lanes)] for k in range(TOP_K)]

                @pl.loop(0, HALF // lanes, step=2)
                def _(v):
                    for dv in range(2):
                        col = (v + dv) * lanes
                        los, his = [], []
                        for k in range(TOP_K):
                            words = bufs[s][k, r, pl.ds(col, lanes)]
                            los.append(wks[k] * lax.bitcast_convert_type(lax.shift_left(words, jnp.int32(16)), F32))
                            his.append(wks[k] * lax.bitcast_convert_type(words & jnp.int32(-65536), F32))
                        accs[s][r, pl.ds(col, lanes)] = _pairwise_sum(los)
                        accs[s][r, pl.ds(HALF + col, lanes)] = _pairwise_sum(his)

        fetch(0, 0)

        @pl.loop(0, n_ch, step=2)
        def _(c):
            for s in range(2):
                cc = c + s

                @pl.when(cc + 1 < n_ch)
                def _():
                    fetch(cc + 1, 1 - s)

                for k in range(TOP_K):
                    row_copy(cc, s, k).wait()

                @pl.when(cc >= 2)
                def _():
                    out_copy(cc - 2, s).wait()

                reduce_rows(cc, s)
                out_copy(cc, s).start()

        out_copy(n_ch - 2, 0).wait()
        out_copy(n_ch - 1, 1).wait()

    return pl.kernel(
        body,
        out_type=jax.ShapeDtypeStruct((t, D_MODEL), F32),
        mesh=plsc.VectorSubcoreMesh(core_axis_name="c", subcore_axis_name="s"),
        scratch_types=[pltpu.VMEM((TOP_K * per_w,), I32), pltpu.VMEM((per_w, TOP_K * lanes), F32),
                       pltpu.VMEM((TOP_K, ch, HALF), I32), pltpu.VMEM((TOP_K, ch, HALF), I32),
                       pltpu.VMEM((ch, D_MODEL), F32), pltpu.VMEM((ch, D_MODEL), F32),
                       pltpu.SemaphoreType.DMA, pltpu.SemaphoreType.DMA,
                       pltpu.SemaphoreType.DMA, pltpu.SemaphoreType.DMA],
        compiler_params=pltpu.CompilerParams(needs_layout_passes=False),
        name="gather_sum",
    )(pos_flat, w_rep, ys)


def _expert_kernel(nblk_ref, blk0_ref, cnt_ref, wg_ref, wu_ref, wd_ref, xs_hbm, after_ref, ys_hbm,
                   wg_s, wu_s, wd_s, xbuf, ybuf, in_sem, out_sem):
    del after_ref
    e = pl.program_id(0)
    nb = nblk_ref[e]
    b0 = blk0_ref[e]
    cnt = cnt_ref[e]
    n_all = blk0_ref[N_EXPERTS]
    ahead = EXPERT_SLOTS - 2
    wg_s[...] = wg_ref[0].astype(BF16)
    wu_s[...] = wu_ref[0].astype(BF16)
    wd_s[...] = wd_ref[0].astype(BF16)

    def slot_rows(g, n_blocks=1):
        start = (g & (EXPERT_SLOTS - 1)) * MOE_BLOCK
        return pl.ds(pl.multiple_of(start, MOE_BLOCK), n_blocks * MOE_BLOCK)

    def rows_of(g):
        return pl.ds(pl.multiple_of(g * MOE_BLOCK, MOE_BLOCK), MOE_BLOCK)

    def in_copy(g):
        return pltpu.make_async_copy(xs_hbm.at[rows_of(g)], xbuf.at[slot_rows(g)],
                                     in_sem.at[g & (EXPERT_SLOTS - 1)])

    def out_copy(g):
        return pltpu.make_async_copy(ybuf.at[slot_rows(g)], ys_hbm.at[rows_of(g)],
                                     out_sem.at[g & (EXPERT_SLOTS - 1)])

    @pl.when(e == 0)
    def _():
        for g in range(ahead):
            @pl.when(g < n_all)
            def _():
                in_copy(g).start()

    def arrive(g):
        in_copy(g).wait()

        @pl.when(g + ahead < n_all)
        def _():
            in_copy(g + ahead).start()

        @pl.when(g >= EXPERT_SLOTS)
        def _():
            out_copy(g - EXPERT_SLOTS).wait()

    def run_blocks(j, n_blocks):
        g = b0 + j
        for i in range(n_blocks):
            arrive(g + i)
        rows = lax.broadcasted_iota(I32, (n_blocks * MOE_BLOCK, HALF), 0)
        x = jnp.where(rows < cnt - j * MOE_BLOCK, xbuf[slot_rows(g, n_blocks)], 0)
        ybuf[slot_rows(g, n_blocks)] = _swiglu_packed(x, wg_s, wu_s, wd_s)
        for i in range(n_blocks):
            out_copy(g + i).start()

    lead = jnp.where(nb > 0, b0 & 1, 0)

    @pl.when(lead == 1)
    def _():
        run_blocks(0, 1)

    n_pairs = (nb - lead) // 2

    def one_pair(p, carry):
        run_blocks(lead + 2 * p, 2)
        return carry

    lax.fori_loop(0, n_pairs, one_pair, 0)

    @pl.when(nb - lead - 2 * n_pairs == 1)
    def _():
        run_blocks(nb - 1, 1)

    @pl.when(e == N_EXPERTS - 1)
    def _():
        for back in range(1, EXPERT_SLOTS + 1):
            @pl.when(n_all >= back)
            def _():
                out_copy(n_all - back).wait()


def _experts(n_blk, blk0, counts, n_rows, xs, w_gate, w_up, w_down, after):
    wspec_in = pl.BlockSpec((1, D_MODEL, EXPERT_DIM), lambda e, *_: (e, 0, 0))
    grid_spec = pltpu.PrefetchScalarGridSpec(
        num_scalar_prefetch=3,
        grid=(N_EXPERTS,),
        in_specs=[wspec_in, wspec_in,
                  pl.BlockSpec((1, EXPERT_DIM, D_MODEL), lambda e, *_: (e, 0, 0)),
                  pl.BlockSpec(memory_space=pl.ANY),
                  pl.BlockSpec(memory_space=pl.ANY)],
        out_specs=pl.BlockSpec(memory_space=pl.ANY),
        scratch_shapes=[pltpu.VMEM((D_MODEL, EXPERT_DIM), BF16), pltpu.VMEM((D_MODEL, EXPERT_DIM), BF16),
                        pltpu.VMEM((EXPERT_DIM, D_MODEL), BF16),
                        pltpu.VMEM((EXPERT_SLOTS * MOE_BLOCK, HALF), I32),
                        pltpu.VMEM((EXPERT_SLOTS * MOE_BLOCK, HALF), I32),
                        pltpu.SemaphoreType.DMA((EXPERT_SLOTS,)), pltpu.SemaphoreType.DMA((EXPERT_SLOTS,))],
    )
    return pl.pallas_call(
        _expert_kernel,
        grid_spec=grid_spec,
        out_shape=jax.ShapeDtypeStruct((n_rows, HALF), I32),
        compiler_params=pltpu.CompilerParams(
            dimension_semantics=("arbitrary",), vmem_limit_bytes=VMEM_LIMIT),
        name="experts",
    )(n_blk, blk0, counts, w_gate, w_up, w_down, xs, after)


def _combine_kernel(x1_ref, ysh_ref, moe_ref, p_ref, gple_ref, wpg_ref, wpp_ref, gpost_ref, *rest):
    out_ref = rest[-1]
    proj = _rms(_dot(p_ref[...].astype(BF16), wpp_ref[...]), gpost_ref[...])
    sh_lo, sh_hi = _unpack(ysh_ref[...])
    x2 = x1_ref[...] + moe_ref[...] + jnp.concatenate([sh_lo, sh_hi], axis=1)
    gate = _sigmoid(_dot(_rms(x2, gple_ref[...]).astype(BF16), wpg_ref[...]))
    out_ref[...] = x2 + gate * proj


def _combine(x1, ysh, moe, p2d, g_ple, w_ple_gate, w_ple_proj, g_ple_post, tok0, tok0_all, prev_out):
    t = p2d.shape[0]
    tm = TM_COMBINE
    steps = moe.shape[0] // tm
    off = tok0 // tm
    off_all = tok0_all // tm
    operands = [x1, ysh, moe, p2d, g_ple.reshape(1, D_MODEL), w_ple_gate.astype(BF16),
                w_ple_proj.astype(BF16), g_ple_post.reshape(1, D_MODEL)]
    in_specs = [pl.BlockSpec((tm, D_MODEL), lambda i: (off + i, 0)),
                pl.BlockSpec((tm, HALF), lambda i: (off + i, 0)),
                pl.BlockSpec((tm, D_MODEL), lambda i: (i, 0)),
                pl.BlockSpec((tm, PLE_DIM), lambda i: (off_all + i, 0)),
                _const_spec((1, D_MODEL)),
                _const_spec((D_MODEL, D_MODEL)),
                _const_spec((PLE_DIM, D_MODEL)),
                _const_spec((1, D_MODEL))]
    aliases = {}
    if prev_out is not None:
        aliases = {len(operands): 0}
        operands.append(prev_out)
        in_specs.append(pl.BlockSpec(memory_space=pl.ANY))
    return pl.pallas_call(
        _combine_kernel,
        grid=(steps,),
        in_specs=in_specs,
        out_specs=pl.BlockSpec((tm, D_MODEL), lambda i: (off_all + i, 0)),
        out_shape=jax.ShapeDtypeStruct((t, D_MODEL), F32),
        input_output_aliases=aliases,
        compiler_params=pltpu.CompilerParams(
            dimension_semantics=("arbitrary",), vmem_limit_bytes=VMEM_LIMIT),
        name="combine",
    )(*operands)


def _layer(x2d, p2d, batch, seq, g_mix, w_in, w_pool_mix, pool_scale, w_branch_a, q_gain, k_gain,
           attn_sinks, w_branch_b, w_gate, b_gate, w_out, g_ffn, w_router, router_bias, w_exp_gate,
           w_exp_up, w_exp_down, w_sh_gate, w_sh_up, w_sh_down, g_ple, w_ple_gate, w_ple_proj,
           g_ple_post):
    bh = batch // PIPELINE_HALVES
    th = bh * seq
    n_rows = (-(-(th * TOP_K) // MOE_BLOCK) + N_EXPERTS) * MOE_BLOCK

    halves = []
    after = None
    for h in range(PIPELINE_HALVES):
        x1, h2p, idx8, w_rep, rk8, cnt = _mixer(
            x2d, h * bh, bh, seq, g_mix, w_in, w_pool_mix, pool_scale, w_branch_a, q_gain, k_gain,
            attn_sinks, w_branch_b, w_gate, b_gate, w_out, g_ffn, w_router, router_bias, after=after)
        counts = cnt[:, 0].astype(I32)
        padded = (counts + MOE_BLOCK - 1) // MOE_BLOCK * MOE_BLOCK
        pend = jnp.cumsum(padded)
        pstart = pend - padded
        pos = _positions(pstart.astype(I32), idx8, rk8).reshape(-1)
        xs = _dispatch(pos, h2p, n_rows)
        blk0 = jnp.concatenate([pstart, pend[-1:]]) // MOE_BLOCK
        halves.append((x1, h2p, w_rep, pos, xs, (padded // MOE_BLOCK).astype(I32), blk0.astype(I32), counts))
        after = pos

    ys_all = []
    for (x1, h2p, w_rep, pos, xs, n_blk, blk0, counts) in halves:
        ys_all.append(_experts(n_blk, blk0, counts, n_rows, xs, w_exp_gate, w_exp_up, w_exp_down,
                               ys_all[-1] if ys_all else xs))

    out = None
    tc = th // COMBINE_CHUNKS
    for h, (x1, h2p, w_rep, pos, xs, n_blk, blk0, counts) in enumerate(halves):
        ys = ys_all[h]
        ysh = _shared(h2p, w_sh_gate, w_sh_up, w_sh_down, after=ys_all[-1] if out is None else out)
        for c in range(COMBINE_CHUNKS):
            moe = _gather_sum(pos, w_rep, ys, c * tc, tc)
            out = _combine(x1, ysh, moe, p2d, g_ple, w_ple_gate, w_ple_proj, g_ple_post,
                           c * tc, h * th + c * tc, out)
    return out


def kernel(x, p, g_mix, w_in, w_pool_mix, pool_scale, w_branch_a, q_gain, k_gain, attn_sinks,
           w_branch_b, w_gate, b_gate, w_out, g_ffn, w_router, router_bias, w_exp_gate, w_exp_up,
           w_exp_down, w_sh_gate, w_sh_up, w_sh_down, g_ple, w_ple_gate, w_ple_proj, g_ple_post):
    batch, seq, d = x.shape
    depth = p.shape[0]
    x2d = x.reshape(batch * seq, d)
    for i in range(depth):
        x2d = _layer(x2d, p[i].reshape(batch * seq, PLE_DIM), batch, seq, g_mix[i], w_in[i],
                     w_pool_mix[i], pool_scale[i], w_branch_a[i], q_gain[i], k_gain[i], attn_sinks[i],
                     w_branch_b[i], w_gate[i], b_gate[i], w_out[i], g_ffn[i], w_router[i],
                     router_bias[i], w_exp_gate[i], w_exp_up[i], w_exp_down[i], w_sh_gate[i],
                     w_sh_up[i], w_sh_down[i], g_ple[i], w_ple_gate[i], w_ple_proj[i], g_ple_post[i])
    return x2d.reshape(batch, seq, d)
```

```python
import numpy as np
import jax
import jax.numpy as jnp
from jax import lax
from jax.experimental import pallas as pl
from jax.experimental.pallas import tpu as pltpu
from jax.experimental.pallas import tpu_sc as plsc

F32 = jnp.float32
BF16 = jnp.bfloat16
I32 = jnp.int32

D_MODEL = 1024
PLE_DIM = 256
POOL_WINDOWS = (2, 4, 8, 16)
POOL_GROUP_DIM = 128
POOL_DIM = 512
N_HEADS = 8
N_KV_HEADS = 2
Q_GROUP = 4
HEAD_DIM = 64
HEAD_PAD = 128
Q_DIM = 512
KV_DIM = 128
ATTN_BLOCK = 128
N_EXPERTS = 64
N_EXPERT_GROUPS = 8
GROUP_SIZE = 8
TOPK_GROUPS = 4
TOP_K = 8
EXPERT_DIM = 256
ROUTED_SCALE = 2.5
MOE_BLOCK = 512
EXPERT_SLOTS = 8
EPS = 1e-6
HALF = D_MODEL // 2
MASKED = -1e30

COL_Q = POOL_DIM
COL_K = COL_Q + N_HEADS * HEAD_PAD
COL_V = COL_K + N_KV_HEADS * HEAD_PAD
IN_PAD = COL_V + N_KV_HEADS * HEAD_PAD
HEAD_ORDER = (0, 2, 1, 3)

TM_MIX = 512
TM_FFN = 1024
TM_COMBINE = 1024
TM_POS = 4096
COMBINE_CHUNKS = 2
PIPELINE_HALVES = 2
SC_SCATTER_ROWS = 128
SC_SUM_ROWS = 8
SC_LANES = 16
POOL_TAIL = 8
VMEM_LIMIT = 56 * 1024 * 1024

_NT = (((1,), (1,)), ((), ()))


def _dot(a, b):
    return jnp.dot(a, b, preferred_element_type=F32)


def _rms(x, g):
    ms = jnp.mean(x * x, axis=-1, keepdims=True)
    return x * lax.rsqrt(ms + EPS) * g


def _sigmoid(x):
    return 0.5 * jnp.tanh(0.5 * x) + 0.5


def _pack(lo, hi):
    lo_bits = lax.bitcast_convert_type(lo.astype(BF16).astype(F32), I32)
    hi_bits = lax.bitcast_convert_type(hi.astype(BF16).astype(F32), I32)
    return (hi_bits & jnp.int32(-65536)) | lax.shift_right_logical(lo_bits, jnp.int32(16))


def _unpack(w):
    lo = lax.bitcast_convert_type(lax.shift_left(w, jnp.int32(16)), F32)
    hi = lax.bitcast_convert_type(w & jnp.int32(-65536), F32)
    return lo, hi


def _mixer_kernel(x_ref, gmix_ref, win_ref, qg_ref, kg_ref, bias_ref, sink_ref, ones_ref, wmix_ref,
                  pscale_ref, wa_ref, wb_ref, wg_ref, bg_ref, wout_ref,
                  gffn_ref, wr_ref, rb_ref, tri_ref, *rest):
    (x1_ref, h2p_ref, idx_ref, wt_ref, rk_ref, cnt_ref,
     kbuf, vt_buf, pbuf, zbt_buf, run_ref) = rest[-11:]
    s = pl.program_id(1)
    tm = x_ref.shape[0]

    @pl.when(s == 0)
    def _():
        kbuf[0:ATTN_BLOCK, :] = jnp.zeros((ATTN_BLOCK, kbuf.shape[1]), BF16)
        vt_buf[:, 0:ATTN_BLOCK] = jnp.zeros((vt_buf.shape[0], ATTN_BLOCK), BF16)
        pbuf[:, 0:POOL_TAIL, :] = jnp.zeros((pbuf.shape[0], POOL_TAIL, POOL_DIM), F32)

    x = x_ref[...]
    hb = _rms(x, gmix_ref[...]).astype(BF16)
    u = _dot(hb, win_ref[...])

    up = u[:, 0:POOL_DIM]
    t0 = POOL_TAIL
    pbuf[0, t0:t0 + tm, :] = up
    s2 = up + pbuf[0, t0 - 1:t0 - 1 + tm, :]
    pbuf[1, t0:t0 + tm, 128:512] = s2[:, 128:512]
    s4 = s2[:, 128:512] + pbuf[1, t0 - 2:t0 - 2 + tm, 128:512]
    pbuf[2, t0:t0 + tm, 256:512] = s4[:, 128:384]
    s8 = s4[:, 128:384] + pbuf[2, t0 - 4:t0 - 4 + tm, 256:512]
    pbuf[3, t0:t0 + tm, 384:512] = s8[:, 128:256]
    s16 = s8[:, 128:256] + pbuf[3, t0 - 8:t0 - 8 + tm, 384:512]
    for lvl in range(4):
        pbuf[lvl, 0:POOL_TAIL, 128 * lvl:512] = pbuf[lvl, tm:tm + POOL_TAIL, 128 * lvl:512]
    wsums = (s2[:, 0:128], s4[:, 0:128], s8[:, 0:128], s16)
    tpos = (s * tm).astype(F32) + lax.broadcasted_iota(I32, (tm, 1), 0).astype(F32)
    za_parts = []
    for g, w in enumerate(POOL_WINDOWS):
        inv_count = 1.0 / jnp.minimum(tpos + 1.0, float(w))
        pooled = wsums[g] * inv_count - up[:, 128 * g:128 * (g + 1)]
        mixed = _dot(pooled.astype(BF16), wmix_ref[g]) * pscale_ref[:, 128 * g:128 * (g + 1)]
        za_parts.append(mixed.astype(BF16))
    za = jnp.concatenate(za_parts, axis=1)

    ones2 = ones_ref[...]
    q = u[:, COL_Q:COL_K]
    q2 = (q * q).astype(BF16)
    qss = jnp.concatenate([_dot(q2[:, 256 * c:256 * (c + 1)], ones2) for c in range(N_HEADS // 2)], axis=1)
    qn = (q * lax.rsqrt(qss * (1.0 / HEAD_DIM) + EPS) * qg_ref[...]).astype(BF16)
    k = u[:, COL_K:COL_V]
    kss = _dot((k * k).astype(BF16), ones2)
    kbuf[ATTN_BLOCK:ATTN_BLOCK + tm, :] = (k * lax.rsqrt(kss * (1.0 / HEAD_DIM) + EPS) * kg_ref[...]).astype(BF16)
    vt_buf[:, ATTN_BLOCK:ATTN_BLOCK + tm] = u[:, COL_V:IN_PAD].T.astype(BF16)

    nq = Q_GROUP * ATTN_BLOCK
    key_j = lax.broadcasted_iota(I32, (ATTN_BLOCK, nq), 0)
    qry_i = lax.broadcasted_iota(I32, (ATTN_BLOCK, nq), 1) & (ATTN_BLOCK - 1)
    from_prev = key_j > qry_i
    first = jnp.where(s == 0, 1, 0)
    for n in range(tm // ATTN_BLOCK):
        r0 = ATTN_BLOCK * n
        for kv in range(N_KV_HEADS):
            c0 = HEAD_PAD * kv
            qs = jnp.concatenate(
                [qn[r0:r0 + ATTN_BLOCK, HEAD_PAD * (kv * Q_GROUP + g):HEAD_PAD * (kv * Q_GROUP + g + 1)]
                 for g in HEAD_ORDER], axis=0)
            kk = kbuf[r0:r0 + 2 * ATTN_BLOCK, c0:c0 + HEAD_PAD]
            st = lax.dot_general(kk, qs, _NT, preferred_element_type=F32)
            sc = jnp.where(from_prev, st[0:ATTN_BLOCK], st[ATTN_BLOCK:2 * ATTN_BLOCK])
            sc = sc + (bias_ref[first, kv] if n == 0 else bias_ref[0, kv])
            sink = sink_ref[kv]
            m = jnp.maximum(jnp.max(sc, axis=0, keepdims=True), sink)
            e = jnp.exp(sc - m)
            inv_den = 1.0 / (jnp.sum(e, axis=0, keepdims=True) + jnp.exp(sink - m))
            pt = jnp.concatenate([jnp.where(from_prev, e, 0.0), jnp.where(from_prev, 0.0, e)],
                                 axis=0).astype(BF16)
            v_even = vt_buf[c0:c0 + HEAD_PAD, r0:r0 + 2 * ATTN_BLOCK]
            v_odd = jnp.concatenate([v_even[HEAD_DIM:HEAD_PAD], v_even[0:HEAD_DIM]], axis=0)
            half = 2 * ATTN_BLOCK
            o = (_dot(v_even, pt[:, 0:half]) * inv_den[:, 0:half]
                 + _dot(v_odd, pt[:, half:2 * half]) * inv_den[:, half:2 * half])
            ch = kv * Q_GROUP * HEAD_DIM
            zbt_buf[ch:ch + HEAD_PAD, r0:r0 + ATTN_BLOCK] = o[:, 0:ATTN_BLOCK]
            zbt_buf[ch + HEAD_PAD:ch + 2 * HEAD_PAD, r0:r0 + ATTN_BLOCK] = o[:, ATTN_BLOCK:half]
    kbuf[0:ATTN_BLOCK, :] = kbuf[tm:tm + ATTN_BLOCK, :]
    vt_buf[:, 0:ATTN_BLOCK] = vt_buf[:, tm:tm + ATTN_BLOCK]

    y_a = _dot(za, wa_ref[...])
    y_b = _dot(zbt_buf[...].T.astype(BF16), wb_ref[...])
    g_a = _sigmoid(_dot(hb, wg_ref[:, 0:D_MODEL]) + bg_ref[:, 0:D_MODEL])
    merged = g_a * y_a
    g_b = _sigmoid(_dot(hb, wg_ref[:, D_MODEL:2 * D_MODEL]) + bg_ref[:, D_MODEL:2 * D_MODEL])
    merged = (merged + g_b * y_b).astype(BF16)
    x1 = x + _dot(merged, wout_ref[...])
    x1_ref[...] = x1

    is_first = (pl.program_id(0) == 0) & (s == 0)
    _route_tile(x1, is_first, gffn_ref, wr_ref, rb_ref, tri_ref,
                h2p_ref, idx_ref, wt_ref, rk_ref, cnt_ref, run_ref)


def _attn_tables(attn_sinks):
    slopes = 2.0 ** (-8.0 * (np.arange(N_HEADS) + 1) / N_HEADS)
    j = np.arange(ATTN_BLOCK)[:, None]
    i = np.arange(ATTN_BLOCK)[None, :]
    from_prev = j > i
    dist = np.where(from_prev, ATTN_BLOCK + i - j, i - j)
    bias = np.empty((2, N_KV_HEADS, ATTN_BLOCK, Q_GROUP * ATTN_BLOCK), np.float32)
    for first in range(2):
        ok = ~from_prev if first else np.ones_like(from_prev)
        for kv in range(N_KV_HEADS):
            for slot, g in enumerate(HEAD_ORDER):
                sl = np.float32(slopes[kv * Q_GROUP + g])
                val = -(sl * dist.astype(np.float32))
                bias[first, kv, :, slot * ATTN_BLOCK:(slot + 1) * ATTN_BLOCK] = np.where(ok, val, MASKED)
    sink = attn_sinks.astype(F32).reshape(N_KV_HEADS, Q_GROUP)[:, np.asarray(HEAD_ORDER)]
    sink = jnp.repeat(sink, ATTN_BLOCK, axis=1)
    return jnp.asarray(bias), sink.reshape(N_KV_HEADS, 1, Q_GROUP * ATTN_BLOCK)


def _pad_heads(w, n_heads):
    k = w.shape[0]
    w = w.reshape(k, n_heads, HEAD_DIM)
    w = jnp.pad(w, ((0, 0), (0, 0), (0, HEAD_PAD - HEAD_DIM)))
    return w.reshape(k, n_heads * HEAD_PAD)


def _const_spec(shape):
    nd = len(shape)
    return pl.BlockSpec(shape, lambda *_: (0,) * nd)


def _mixer(x2d, batch0, batch, seq, g_mix, w_in, w_pool_mix, pool_scale, w_branch_a, q_gain, k_gain,
           attn_sinks, w_branch_b, w_gate, b_gate, w_out, g_ffn, w_router, router_bias, after=None):
    t = batch * seq
    tm = TM_MIX
    ns = seq // tm
    w_q = w_in[:, POOL_DIM:POOL_DIM + Q_DIM]
    w_k = w_in[:, POOL_DIM + Q_DIM:POOL_DIM + Q_DIM + KV_DIM]
    w_v = w_in[:, POOL_DIM + Q_DIM + KV_DIM:]
    win_p = jnp.concatenate([w_in[:, :POOL_DIM], _pad_heads(w_q, N_HEADS), _pad_heads(w_k, N_KV_HEADS),
                             _pad_heads(w_v, N_KV_HEADS)], axis=1).astype(BF16)
    qg = jnp.tile(jnp.pad(q_gain * (HEAD_DIM ** -0.5), (0, HEAD_PAD - HEAD_DIM)), N_HEADS).reshape(1, -1)
    kg = jnp.tile(jnp.pad(k_gain, (0, HEAD_PAD - HEAD_DIM)), N_KV_HEADS).reshape(1, -1)
    bias, sink = _attn_tables(attn_sinks)
    ones2 = jnp.asarray(np.kron(np.eye(2), np.ones((HEAD_PAD, HEAD_PAD))), BF16)
    operands = (x2d, g_mix.reshape(1, D_MODEL), win_p, qg, kg, bias, sink, ones2, w_pool_mix.astype(BF16),
                pool_scale.reshape(1, POOL_DIM), w_branch_a.astype(BF16), w_branch_b.astype(BF16),
                w_gate.astype(BF16), b_gate.reshape(1, 2 * D_MODEL), w_out.astype(BF16))
    operands += _router_operands(g_ffn, w_router, router_bias, tm)
    in_specs = [pl.BlockSpec((tm, D_MODEL), lambda b, s: ((batch0 + b) * ns + s, 0))]
    in_specs += [_const_spec(op.shape) for op in operands[1:]]
    if after is not None:
        operands += (after,)
        in_specs.append(pl.BlockSpec(memory_space=pl.ANY))
    row8 = pl.BlockSpec((TOP_K, tm), lambda b, s: (0, b * ns + s))
    return pl.pallas_call(
        _mixer_kernel,
        grid=(batch, ns),
        in_specs=in_specs,
        out_specs=[pl.BlockSpec((tm, D_MODEL), lambda b, s: (b * ns + s, 0)),
                   pl.BlockSpec((tm, HALF), lambda b, s: (b * ns + s, 0)),
                   row8,
                   pl.BlockSpec((tm, TOP_K * SC_LANES), lambda b, s: (b * ns + s, 0)),
                   row8,
                   pl.BlockSpec((N_EXPERTS, 128), lambda b, s: (0, 0))],
        out_shape=[jax.ShapeDtypeStruct((t, D_MODEL), F32),
                   jax.ShapeDtypeStruct((t, HALF), I32),
                   jax.ShapeDtypeStruct((TOP_K, t), I32),
                   jax.ShapeDtypeStruct((t, TOP_K * SC_LANES), F32),
                   jax.ShapeDtypeStruct((TOP_K, t), I32),
                   jax.ShapeDtypeStruct((N_EXPERTS, 128), F32)],
        scratch_shapes=[
            pltpu.VMEM((ATTN_BLOCK + tm, N_KV_HEADS * HEAD_PAD), BF16),
            pltpu.VMEM((N_KV_HEADS * HEAD_PAD, ATTN_BLOCK + tm), BF16),
            pltpu.VMEM((4, POOL_TAIL + tm, POOL_DIM), F32),
            pltpu.VMEM((Q_DIM, tm), F32),
            pltpu.VMEM((N_EXPERTS, 128), F32),
        ],
        compiler_params=pltpu.CompilerParams(
            dimension_semantics=("arbitrary", "arbitrary"), vmem_limit_bytes=VMEM_LIMIT),
        name="mixer",
    )(*operands)


def _route_tile(x1, is_first, gffn_ref, wr_ref, rb_ref, tri_ref,
                h2p_ref, idx_ref, wt_ref, rk_ref, cnt_ref, run_ref):
    tm = x1.shape[0]

    @pl.when(is_first)
    def _():
        run_ref[...] = jnp.zeros_like(run_ref)

    h2 = _rms(x1, gffn_ref[...])
    hb = h2.astype(BF16)
    h2p_ref[...] = _pack(h2[:, 0:HALF], h2[:, HALF:D_MODEL])

    logits = lax.dot_general(wr_ref[...], hb, _NT, preferred_element_type=F32)
    scores = _sigmoid(logits)
    choice = scores + rb_ref[...]

    neg = -jnp.inf
    sub8 = lax.broadcasted_iota(I32, (GROUP_SIZE, tm), 0).astype(F32)
    grp_rows = []
    for gi in range(N_EXPERT_GROUPS):
        cg = choice[GROUP_SIZE * gi:GROUP_SIZE * (gi + 1)]
        m1 = jnp.max(cg, axis=0, keepdims=True)
        first = jnp.min(jnp.where(cg == m1, sub8, float(GROUP_SIZE)), axis=0, keepdims=True)
        m2 = jnp.max(jnp.where(sub8 == first, neg, cg), axis=0, keepdims=True)
        grp_rows.append(m1 + m2)
    gs = jnp.concatenate(grp_rows, axis=0)
    beaten = jnp.zeros((N_EXPERT_GROUPS, tm), F32)
    for gi in range(N_EXPERT_GROUPS):
        row = grp_rows[gi]
        wins = (row > gs) | ((row == gs) & (sub8 > float(gi)))
        beaten = beaten + jnp.where(wins, 1.0, 0.0)
    gsel = beaten < float(TOPK_GROUPS)
    cm = jnp.concatenate(
        [jnp.where(gsel[gi:gi + 1], choice[GROUP_SIZE * gi:GROUP_SIZE * (gi + 1)], neg)
         for gi in range(N_EXPERT_GROUPS)], axis=0)

    iota_e = lax.broadcasted_iota(I32, (N_EXPERTS, tm), 0).astype(F32)
    idx_rows, s_rows = [], []
    sel = jnp.zeros((N_EXPERTS, tm), F32)
    for _ in range(TOP_K):
        m = jnp.max(cm, axis=0, keepdims=True)
        idx = jnp.min(jnp.where(cm == m, iota_e, float(N_EXPERTS)), axis=0, keepdims=True)
        oh = iota_e == idx
        s_rows.append(jnp.sum(jnp.where(oh, scores, 0.0), axis=0, keepdims=True))
        idx_rows.append(idx)
        cm = jnp.where(oh, neg, cm)
        sel = sel + jnp.where(oh, 1.0, 0.0)

    run = run_ref[:, 0:1]
    cum = _dot(sel.astype(BF16), tri_ref[...])
    before = run + cum - sel
    rk_rows = [jnp.sum(jnp.where(iota_e == idx, before, 0.0), axis=0, keepdims=True) for idx in idx_rows]
    new_run = run + jnp.sum(sel, axis=1, keepdims=True)
    run_ref[...] = jnp.broadcast_to(new_run, run_ref.shape)
    cnt_ref[...] = jnp.broadcast_to(new_run, cnt_ref.shape)

    ssum = s_rows[0]
    for r in range(1, TOP_K):
        ssum = ssum + s_rows[r]
    denom = ssum + 1e-20
    idx_ref[...] = jnp.concatenate(idx_rows, axis=0).astype(I32)
    w_rep = jnp.concatenate([jnp.broadcast_to(sr / denom * ROUTED_SCALE, (SC_LANES, tm)) for sr in s_rows],
                            axis=0)
    wt_ref[...] = w_rep.T
    rk_ref[...] = jnp.concatenate(rk_rows, axis=0).astype(I32)


def _router_operands(g_ffn, w_router, router_bias, tm):
    wr_t = w_router.astype(F32).T.astype(BF16)
    tri = (np.arange(tm)[:, None] <= np.arange(tm)[None, :]).astype(np.float32)
    return (g_ffn.reshape(1, D_MODEL), wr_t, router_bias.astype(F32).reshape(N_EXPERTS, 1),
            jnp.asarray(tri, BF16))


def _swiglu_packed(x_packed, wg, wu, wd):
    lo, hi = _unpack(x_packed)
    lo = lo.astype(BF16)
    hi = hi.astype(BF16)
    g = _dot(lo, wg[0:HALF, :]) + _dot(hi, wg[HALF:D_MODEL, :])
    u = _dot(lo, wu[0:HALF, :]) + _dot(hi, wu[HALF:D_MODEL, :])
    a = (g * _sigmoid(g) * u).astype(BF16)
    y = _dot(a, wd[...])
    return _pack(y[:, 0:HALF], y[:, HALF:D_MODEL])


def _shared_kernel(h2p_ref, wg_ref, wu_ref, wd_ref, *rest):
    ysh_ref = rest[-1]
    ysh_ref[...] = _swiglu_packed(h2p_ref[...], wg_ref, wu_ref, wd_ref)


def _shared(h2p, w_sh_gate, w_sh_up, w_sh_down, after=None):
    t = h2p.shape[0]
    tm = TM_FFN
    operands = (h2p, w_sh_gate.astype(BF16), w_sh_up.astype(BF16), w_sh_down.astype(BF16))
    in_specs = [pl.BlockSpec((tm, HALF), lambda i: (i, 0))]
    in_specs += [_const_spec(op.shape) for op in operands[1:]]
    if after is not None:
        operands += (after,)
        in_specs.append(pl.BlockSpec(memory_space=pl.ANY))
    return pl.pallas_call(
        _shared_kernel,
        grid=(t // tm,),
        in_specs=in_specs,
        out_specs=pl.BlockSpec((tm, HALF), lambda i: (i, 0)),
        out_shape=jax.ShapeDtypeStruct((t, HALF), I32),
        compiler_params=pltpu.CompilerParams(
            dimension_semantics=("arbitrary",), vmem_limit_bytes=VMEM_LIMIT),
        name="shared",
    )(*operands)


def _positions_kernel(pstart_ref, idx_ref, rk_ref, pos_ref):
    idx = idx_ref[...]
    acc = rk_ref[...]
    for e in range(N_EXPERTS):
        acc = acc + jnp.where(idx == e, pstart_ref[e], 0)
    pos_ref[...] = acc


def _positions(pstart, idx8, rk8):
    t = idx8.shape[1]
    tm = TM_POS
    spec = pl.BlockSpec((TOP_K, tm), lambda i, ps: (0, i))
    return pl.pallas_call(
        _positions_kernel,
        grid_spec=pltpu.PrefetchScalarGridSpec(num_scalar_prefetch=1, grid=(t // tm,),
                                               in_specs=[spec, spec], out_specs=spec),
        out_shape=jax.ShapeDtypeStruct((TOP_K, t), I32),
        name="positions",
    )(pstart, idx8, rk8)


def _sc_workers():
    info = plsc.get_sparse_core_info()
    return info.num_cores, info.num_cores * info.num_subcores


def _dispatch(pos_flat, h2p, n_rows):
    t = h2p.shape[0]
    ch = SC_SCATTER_ROWS
    n_cores, n_workers = _sc_workers()
    n_ch = t // (n_workers * ch)

    def body(pos_hbm, h_hbm, xs_hbm, idx_v, rows_v, sem):
        wid = lax.axis_index("s") * n_cores + lax.axis_index("c")

        @pl.loop(0, n_ch)
        def _(c):
            chunk = wid * n_ch + c
            pltpu.sync_copy(h_hbm.at[pl.ds(chunk * ch, ch)], rows_v)
            for k in range(TOP_K):
                pltpu.sync_copy(pos_hbm.at[pl.ds(k * t + chunk * ch, ch)], idx_v.at[k])
            copies = [pltpu.async_copy(rows_v, xs_hbm.at[idx_v.at[k]], sem) for k in range(TOP_K)]
            for cp in copies:
                cp.wait()

    return pl.kernel(
        body,
        out_type=jax.ShapeDtypeStruct((n_rows, HALF), I32),
        mesh=plsc.VectorSubcoreMesh(core_axis_name="c", subcore_axis_name="s"),
        scratch_types=[pltpu.VMEM((TOP_K, ch), I32), pltpu.VMEM((ch, HALF), I32),
                       pltpu.SemaphoreType.DMA],
        name="dispatch",
    )(pos_flat, h2p)


def _pairwise_sum(xs):
    while len(xs) > 1:
        xs = [xs[i] + xs[i + 1] for i in range(0, len(xs), 2)]
    return xs[0]


def _gather_sum(pos_flat, w_rep, ys, tok0, t):
    t_all = w_rep.shape[0]
    ch = SC_SUM_ROWS
    lanes = SC_LANES
    n_cores, n_workers = _sc_workers()
    per_w = t // n_workers
    n_ch = per_w // ch

    def body(pos_hbm, w_hbm, ys_hbm, out_hbm, idx_all, w_all, buf0, buf1, acc0, acc1, gsem0, gsem1, osem0, osem1):
        wid = lax.axis_index("s") * n_cores + lax.axis_index("c")
        base = wid * n_ch
        first = tok0 + wid * per_w
        bufs, accs, gsems, osems = (buf0, buf1), (acc0, acc1), (gsem0, gsem1), (osem0, osem1)
        for k in range(TOP_K):
            pltpu.sync_copy(pos_hbm.at[pl.ds(k * t_all + first, per_w)], idx_all.at[pl.ds(k * per_w, per_w)])
        pltpu.sync_copy(w_hbm.at[pl.ds(first, per_w)], w_all)

        def row_copy(c, s, k):
            rows = idx_all.at[pl.ds(k * per_w + c * ch, ch)]
            return pltpu.make_async_copy(ys_hbm.at[rows], bufs[s].at[k], gsems[s])

        def out_copy(c, s):
            return pltpu.make_async_copy(accs[s], out_hbm.at[pl.ds((base + c) * ch, ch)], osems[s])

        def fetch(c, s):
            for k in range(TOP_K):
                row_copy(c, s, k).start()

        def reduce_rows(c, s):
            @pl.loop(0, ch)
            def _(r):
                wks = [w_all[c * ch + r, pl.ds(k * lanes, lanes)] for k in range(TOP_K)]

                @pl.loop(0, HALF // lanes, step=2)
                def _(v):
                    for dv in range(2):
                        col = (v + dv) * lanes
                        los, his = [], []
                        for k in range(TOP_K):
                            words = bufs[s][k, r, pl.ds(col, lanes)]
                            los.append(wks[k] * lax.bitcast_convert_type(lax.shift_left(words, jnp.int32(16)), F32))
                            his.append(wks[k] * lax.bitcast_convert_type(words & jnp.int32(-65536), F32))
                        accs[s][r, pl.ds(col, lanes)] = _pairwise_sum(los)
                        accs[s][r, pl.ds(HALF + col, lanes)] = _pairwise_sum(his)

        fetch(0, 0)

        @pl.loop(0, n_ch, step=2)
        def _(c):
            for s in range(2):
                cc = c + s

                @pl.when(cc + 1 < n_ch)
                def _():
                    fetch(cc + 1, 1 - s)

                for k in range(TOP_K):
                    row_copy(cc, s, k).wait()

                @pl.when(cc >= 2)
                def _():
                    out_copy(cc - 2, s).wait()

                reduce_rows(cc, s)
                out_copy(cc, s).start()

        out_copy(n_ch - 2, 0).wait()
        out_copy(n_ch - 1, 1).wait()

    return pl.kernel(
        body,
        out_type=jax.ShapeDtypeStruct((t, D_MODEL), F32),
        mesh=plsc.VectorSubcoreMesh(core_axis_name="c", subcore_axis_name="s"),
        scratch_types=[pltpu.VMEM((TOP_K * per_w,), I32), pltpu.VMEM((per_w, TOP_K * lanes), F32),
                       pltpu.VMEM((TOP_K, ch, HALF), I32), pltpu.VMEM((TOP_K, ch, HALF), I32),
                       pltpu.VMEM((ch, D_MODEL), F32), pltpu.VMEM((ch, D_MODEL), F32),
                       pltpu.SemaphoreType.DMA, pltpu.SemaphoreType.DMA,
                       pltpu.SemaphoreType.DMA, pltpu.SemaphoreType.DMA],
        compiler_params=pltpu.CompilerParams(needs_layout_passes=False),
        name="gather_sum",
    )(pos_flat, w_rep, ys)


def _expert_kernel(nblk_ref, blk0_ref, cnt_ref, wg_ref, wu_ref, wd_ref, xs_hbm, after_ref, ys_hbm,
                   wg_s, wu_s, wd_s, xbuf, ybuf, in_sem, out_sem):
    del after_ref
    e = pl.program_id(0)
    nb = nblk_ref[e]
    b0 = blk0_ref[e]
    cnt = cnt_ref[e]
    n_all = blk0_ref[N_EXPERTS]
    ahead = EXPERT_SLOTS - 2
    wg_s[...] = wg_ref[0].astype(BF16)
    wu_s[...] = wu_ref[0].astype(BF16)
    wd_s[...] = wd_ref[0].astype(BF16)

    def slot_rows(g, n_blocks=1):
        start = (g & (EXPERT_SLOTS - 1)) * MOE_BLOCK
        return pl.ds(pl.multiple_of(start, MOE_BLOCK), n_blocks * MOE_BLOCK)

    def rows_of(g):
        return pl.ds(pl.multiple_of(g * MOE_BLOCK, MOE_BLOCK), MOE_BLOCK)

    def in_copy(g):
        return pltpu.make_async_copy(xs_hbm.at[rows_of(g)], xbuf.at[slot_rows(g)],
                                     in_sem.at[g & (EXPERT_SLOTS - 1)])

    def out_copy(g):
        return pltpu.make_async_copy(ybuf.at[slot_rows(g)], ys_hbm.at[rows_of(g)],
                                     out_sem.at[g & (EXPERT_SLOTS - 1)])

    @pl.when(e == 0)
    def _():
        for g in range(ahead):
            @pl.when(g < n_all)
            def _():
                in_copy(g).start()

    def arrive(g):
        in_copy(g).wait()

        @pl.when(g + ahead < n_all)
        def _():
            in_copy(g + ahead).start()

        @pl.when(g >= EXPERT_SLOTS)
        def _():
            out_copy(g - EXPERT_SLOTS).wait()

    def run_blocks(j, n_blocks):
        g = b0 + j
        for i in range(n_blocks):
            arrive(g + i)
        rows = lax.broadcasted_iota(I32, (n_blocks * MOE_BLOCK, HALF), 0)
        x = jnp.where(rows < cnt - j * MOE_BLOCK, xbuf[slot_rows(g, n_blocks)], 0)
        ybuf[slot_rows(g, n_blocks)] = _swiglu_packed(x, wg_s, wu_s, wd_s)
        for i in range(n_blocks):
            out_copy(g + i).start()

    lead = jnp.where(nb > 0, b0 & 1, 0)

    @pl.when(lead == 1)
    def _():
        run_blocks(0, 1)

    n_pairs = (nb - lead) // 2

    def one_pair(p, carry):
        run_blocks(lead + 2 * p, 2)
        return carry

    lax.fori_loop(0, n_pairs, one_pair, 0)

    @pl.when(nb - lead - 2 * n_pairs == 1)
    def _():
        run_blocks(nb - 1, 1)

    @pl.when(e == N_EXPERTS - 1)
    def _():
        for back in range(1, EXPERT_SLOTS + 1):
            @pl.when(n_all >= back)
            def _():
                out_copy(n_all - back).wait()


def _experts(n_blk, blk0, counts, n_rows, xs, w_gate, w_up, w_down, after):
    wspec_in = pl.BlockSpec((1, D_MODEL, EXPERT_DIM), lambda e, *_: (e, 0, 0))
    grid_spec = pltpu.PrefetchScalarGridSpec(
        num_scalar_prefetch=3,
        grid=(N_EXPERTS,),
        in_specs=[wspec_in, wspec_in,
                  pl.BlockSpec((1, EXPERT_DIM, D_MODEL), lambda e, *_: (e, 0, 0)),
                  pl.BlockSpec(memory_space=pl.ANY),
                  pl.BlockSpec(memory_space=pl.ANY)],
        out_specs=pl.BlockSpec(memory_space=pl.ANY),
        scratch_shapes=[pltpu.VMEM((D_MODEL, EXPERT_DIM), BF16), pltpu.VMEM((D_MODEL, EXPERT_DIM), BF16),
                        pltpu.VMEM((EXPERT_DIM, D_MODEL), BF16),
                        pltpu.VMEM((EXPERT_SLOTS * MOE_BLOCK, HALF), I32),
                        pltpu.VMEM((EXPERT_SLOTS * MOE_BLOCK, HALF), I32),
                        pltpu.SemaphoreType.DMA((EXPERT_SLOTS,)), pltpu.SemaphoreType.DMA((EXPERT_SLOTS,))],
    )
    return pl.pallas_call(
        _expert_kernel,
        grid_spec=grid_spec,
        out_shape=jax.ShapeDtypeStruct((n_rows, HALF), I32),
        compiler_params=pltpu.CompilerParams(
            dimension_semantics=("arbitrary",), vmem_limit_bytes=VMEM_LIMIT),
        name="experts",
    )(n_blk, blk0, counts, w_gate, w_up, w_down, xs, after)


def _combine_kernel(x1_ref, ysh_ref, moe_ref, p_ref, gple_ref, wpg_ref, wpp_ref, gpost_ref, *rest):
    out_ref = rest[-1]
    proj = _rms(_dot(p_ref[...].astype(BF16), wpp_ref[...]), gpost_ref[...])
    sh_lo, sh_hi = _unpack(ysh_ref[...])
    x2 = x1_ref[...] + moe_ref[...] + jnp.concatenate([sh_lo, sh_hi], axis=1)
    gate = _sigmoid(_dot(_rms(x2, gple_ref[...]).astype(BF16), wpg_ref[...]))
    out_ref[...] = x2 + gate * proj


def _combine(x1, ysh, moe, p2d, g_ple, w_ple_gate, w_ple_proj, g_ple_post, tok0, tok0_all, prev_out):
    t = p2d.shape[0]
    tm = TM_COMBINE
    steps = moe.shape[0] // tm
    off = tok0 // tm
    off_all = tok0_all // tm
    operands = [x1, ysh, moe, p2d, g_ple.reshape(1, D_MODEL), w_ple_gate.astype(BF16),
                w_ple_proj.astype(BF16), g_ple_post.reshape(1, D_MODEL)]
    in_specs = [pl.BlockSpec((tm, D_MODEL), lambda i: (off + i, 0)),
                pl.BlockSpec((tm, HALF), lambda i: (off + i, 0)),
                pl.BlockSpec((tm, D_MODEL), lambda i: (i, 0)),
                pl.BlockSpec((tm, PLE_DIM), lambda i: (off_all + i, 0)),
                _const_spec((1, D_MODEL)),
                _const_spec((D_MODEL, D_MODEL)),
                _const_spec((PLE_DIM, D_MODEL)),
                _const_spec((1, D_MODEL))]
    aliases = {}
    if prev_out is not None:
        aliases = {len(operands): 0}
        operands.append(prev_out)
        in_specs.append(pl.BlockSpec(memory_space=pl.ANY))
    return pl.pallas_call(
        _combine_kernel,
        grid=(steps,),
        in_specs=in_specs,
        out_specs=pl.BlockSpec((tm, D_MODEL), lambda i: (off_all + i, 0)),
        out_shape=jax.ShapeDtypeStruct((t, D_MODEL), F32),
        input_output_aliases=aliases,
        compiler_params=pltpu.CompilerParams(
            dimension_semantics=("arbitrary",), vmem_limit_bytes=VMEM_LIMIT),
        name="combine",
    )(*operands)


def _layer(x2d, p2d, batch, seq, g_mix, w_in, w_pool_mix, pool_scale, w_branch_a, q_gain, k_gain,
           attn_sinks, w_branch_b, w_gate, b_gate, w_out, g_ffn, w_router, router_bias, w_exp_gate,
           w_exp_up, w_exp_down, w_sh_gate, w_sh_up, w_sh_down, g_ple, w_ple_gate, w_ple_proj,
           g_ple_post):
    bh = batch // PIPELINE_HALVES
    th = bh * seq
    n_rows = (-(-(th * TOP_K) // MOE_BLOCK) + N_EXPERTS) * MOE_BLOCK

    halves = []
    after = None
    for h in range(PIPELINE_HALVES):
        x1, h2p, idx8, w_rep, rk8, cnt = _mixer(
            x2d, h * bh, bh, seq, g_mix, w_in, w_pool_mix, pool_scale, w_branch_a, q_gain, k_gain,
            attn_sinks, w_branch_b, w_gate, b_gate, w_out, g_ffn, w_router, router_bias, after=after)
        counts = cnt[:, 0].astype(I32)
        padded = (counts + MOE_BLOCK - 1) // MOE_BLOCK * MOE_BLOCK
        pend = jnp.cumsum(padded)
        pstart = pend - padded
        pos = _positions(pstart.astype(I32), idx8, rk8).reshape(-1)
        xs = _dispatch(pos, h2p, n_rows)
        blk0 = jnp.concatenate([pstart, pend[-1:]]) // MOE_BLOCK
        halves.append((x1, h2p, w_rep, pos, xs, (padded // MOE_BLOCK).astype(I32), blk0.astype(I32), counts))
        after = pos

    ys_all = []
    for (x1, h2p, w_rep, pos, xs, n_blk, blk0, counts) in halves:
        ys_all.append(_experts(n_blk, blk0, counts, n_rows, xs, w_exp_gate, w_exp_up, w_exp_down,
                               ys_all[-1] if ys_all else xs))

    out = None
    tc = th // COMBINE_CHUNKS
    for h, (x1, h2p, w_rep, pos, xs, n_blk, blk0, counts) in enumerate(halves):
        ys = ys_all[h]
        ysh = _shared(h2p, w_sh_gate, w_sh_up, w_sh_down, after=ys_all[-1] if out is None else out)
        for c in range(COMBINE_CHUNKS):
            moe = _gather_sum(pos, w_rep, ys, c * tc, tc)
            out = _combine(x1, ysh, moe, p2d, g_ple, w_ple_gate, w_ple_proj, g_ple_post,
                           c * tc, h * th + c * tc, out)
    return out


def kernel(x, p, g_mix, w_in, w_pool_mix, pool_scale, w_branch_a, q_gain, k_gain, attn_sinks,
           w_branch_b, w_gate, b_gate, w_out, g_ffn, w_router, router_bias, w_exp_gate, w_exp_up,
           w_exp_down, w_sh_gate, w_sh_up, w_sh_down, g_ple, w_ple_gate, w_ple_proj, g_ple_post):
    batch, seq, d = x.shape
    depth = p.shape[0]
    x2d = x.reshape(batch * seq, d)
    for i in range(depth):
        x2d = _layer(x2d, p[i].reshape(batch * seq, PLE_DIM), batch, seq, g_mix[i], w_in[i],
                     w_pool_mix[i], pool_scale[i], w_branch_a[i], q_gain[i], k_gain[i], attn_sinks[i],
                     w_branch_b[i], w_gate[i], b_gate[i], w_out[i], g_ffn[i], w_router[i],
                     router_bias[i], w_exp_gate[i], w_exp_up[i], w_exp_down[i], w_sh_gate[i],
                     w_sh_up[i], w_sh_down[i], g_ple[i], w_ple_gate[i], w_ple_proj[i], g_ple_post[i])
    return x2d.reshape(batch, seq, d)
```

```python
import numpy as np
import jax
import jax.numpy as jnp
from jax import lax
from jax.experimental import pallas as pl
from jax.experimental.pallas import tpu as pltpu
from jax.experimental.pallas import tpu_sc as plsc

F32 = jnp.float32
BF16 = jnp.bfloat16
I32 = jnp.int32

D_MODEL = 1024
PLE_DIM = 256
POOL_WINDOWS = (2, 4, 8, 16)
POOL_GROUP_DIM = 128
POOL_DIM = 512
N_HEADS = 8
N_KV_HEADS = 2
Q_GROUP = 4
HEAD_DIM = 64
HEAD_PAD = 128
Q_DIM = 512
KV_DIM = 128
ATTN_BLOCK = 128
N_EXPERTS = 64
N_EXPERT_GROUPS = 8
GROUP_SIZE = 8
TOPK_GROUPS = 4
TOP_K = 8
EXPERT_DIM = 256
ROUTED_SCALE = 2.5
MOE_BLOCK = 512
EXPERT_SLOTS = 8
EPS = 1e-6
HALF = D_MODEL // 2
MASKED = -1e30

COL_Q = POOL_DIM
COL_K = COL_Q + N_HEADS * HEAD_PAD
COL_V = COL_K + N_KV_HEADS * HEAD_PAD
IN_PAD = COL_V + N_KV_HEADS * HEAD_PAD
HEAD_ORDER = (0, 2, 1, 3)

TM_MIX = 512
TM_FFN = 1024
TM_COMBINE = 1024
TM_POS = 4096
COMBINE_CHUNKS = 2
PIPELINE_HALVES = 2
SC_SCATTER_ROWS = 128
SC_SUM_ROWS = 8
SC_LANES = 16
POOL_TAIL = 8
VMEM_LIMIT = 56 * 1024 * 1024

_NT = (((1,), (1,)), ((), ()))


def _dot(a, b):
    return jnp.dot(a, b, preferred_element_type=F32)


def _rms(x, g):
    ms = jnp.mean(x * x, axis=-1, keepdims=True)
    return x * lax.rsqrt(ms + EPS) * g


def _sigmoid(x):
    return 0.5 * jnp.tanh(0.5 * x) + 0.5


def _pack(lo, hi):
    lo_bits = lax.bitcast_convert_type(lo.astype(BF16).astype(F32), I32)
    hi_bits = lax.bitcast_convert_type(hi.astype(BF16).astype(F32), I32)
    return (hi_bits & jnp.int32(-65536)) | lax.shift_right_logical(lo_bits, jnp.int32(16))


def _unpack(w):
    lo = lax.bitcast_convert_type(lax.shift_left(w, jnp.int32(16)), F32)
    hi = lax.bitcast_convert_type(w & jnp.int32(-65536), F32)
    return lo, hi


def _mixer_kernel(x_ref, gmix_ref, win_ref, qg_ref, kg_ref, bias_ref, sink_ref, ones_ref,
                  wa_ref, wb_ref, wg_ref, bg_ref, wout_ref,
                  gffn_ref, wr_ref, rb_ref, tri_ref, *rest):
    (x1_ref, h2p_ref, idx_ref, wt_ref, rk_ref, cnt_ref,
     kbuf, vt_buf, pbuf, zbt_buf, run_ref) = rest[-11:]
    s = pl.program_id(1)
    tm = x_ref.shape[0]

    @pl.when(s == 0)
    def _():
        kbuf[0:ATTN_BLOCK, :] = jnp.zeros((ATTN_BLOCK, kbuf.shape[1]), BF16)
        vt_buf[:, 0:ATTN_BLOCK] = jnp.zeros((vt_buf.shape[0], ATTN_BLOCK), BF16)
        pbuf[:, 0:POOL_TAIL, :] = jnp.zeros((pbuf.shape[0], POOL_TAIL, POOL_DIM), F32)

    x = x_ref[...]
    hb = _rms(x, gmix_ref[...]).astype(BF16)
    u = _dot(hb, win_ref[...])

    up = u[:, 0:POOL_DIM]
    t0 = POOL_TAIL
    pbuf[0, t0:t0 + tm, :] = up
    s2 = up + pbuf[0, t0 - 1:t0 - 1 + tm, :]
    pbuf[1, t0:t0 + tm, 128:512] = s2[:, 128:512]
    s4 = s2[:, 128:512] + pbuf[1, t0 - 2:t0 - 2 + tm, 128:512]
    pbuf[2, t0:t0 + tm, 256:512] = s4[:, 128:384]
    s8 = s4[:, 128:384] + pbuf[2, t0 - 4:t0 - 4 + tm, 256:512]
    pbuf[3, t0:t0 + tm, 384:512] = s8[:, 128:256]
    s16 = s8[:, 128:256] + pbuf[3, t0 - 8:t0 - 8 + tm, 384:512]
    for lvl in range(4):
        pbuf[lvl, 0:POOL_TAIL, 128 * lvl:512] = pbuf[lvl, tm:tm + POOL_TAIL, 128 * lvl:512]
    wsums = (s2[:, 0:128], s4[:, 0:128], s8[:, 0:128], s16)
    tpos = (s * tm).astype(F32) + lax.broadcasted_iota(I32, (tm, 1), 0).astype(F32)
    za_parts = []
    for g, w in enumerate(POOL_WINDOWS):
        inv_count = 1.0 / jnp.minimum(tpos + 1.0, float(w))
        pooled = wsums[g] * inv_count - up[:, 128 * g:128 * (g + 1)]
        za_parts.append(pooled.astype(BF16))
    za = jnp.concatenate(za_parts, axis=1)

    ones2 = ones_ref[...]
    q = u[:, COL_Q:COL_K]
    q2 = (q * q).astype(BF16)
    qss = jnp.concatenate([_dot(q2[:, 256 * c:256 * (c + 1)], ones2) for c in range(N_HEADS // 2)], axis=1)
    qn = (q * lax.rsqrt(qss * (1.0 / HEAD_DIM) + EPS) * qg_ref[...]).astype(BF16)
    k = u[:, COL_K:COL_V]
    kss = _dot((k * k).astype(BF16), ones2)
    kbuf[ATTN_BLOCK:ATTN_BLOCK + tm, :] = (k * lax.rsqrt(kss * (1.0 / HEAD_DIM) + EPS) * kg_ref[...]).astype(BF16)
    vt_buf[:, ATTN_BLOCK:ATTN_BLOCK + tm] = u[:, COL_V:IN_PAD].T.astype(BF16)

    nq = Q_GROUP * ATTN_BLOCK
    key_j = lax.broadcasted_iota(I32, (ATTN_BLOCK, nq), 0)
    qry_i = lax.broadcasted_iota(I32, (ATTN_BLOCK, nq), 1) & (ATTN_BLOCK - 1)
    from_prev = key_j > qry_i
    first = jnp.where(s == 0, 1, 0)
    for n in range(tm // ATTN_BLOCK):
        r0 = ATTN_BLOCK * n
        for kv in range(N_KV_HEADS):
            c0 = HEAD_PAD * kv
            qs = jnp.concatenate(
                [qn[r0:r0 + ATTN_BLOCK, HEAD_PAD * (kv * Q_GROUP + g):HEAD_PAD * (kv * Q_GROUP + g + 1)]
                 for g in HEAD_ORDER], axis=0)
            kk = kbuf[r0:r0 + 2 * ATTN_BLOCK, c0:c0 + HEAD_PAD]
            st = lax.dot_general(kk, qs, _NT, preferred_element_type=F32)
            sc = jnp.where(from_prev, st[0:ATTN_BLOCK], st[ATTN_BLOCK:2 * ATTN_BLOCK])
            sc = sc + (bias_ref[first, kv] if n == 0 else bias_ref[0, kv])
            sink = sink_ref[kv]
            m = jnp.maximum(jnp.max(sc, axis=0, keepdims=True), sink)
            e = jnp.exp(sc - m)
            inv_den = 1.0 / (jnp.sum(e, axis=0, keepdims=True) + jnp.exp(sink - m))
            pt = jnp.concatenate([jnp.where(from_prev, e, 0.0), jnp.where(from_prev, 0.0, e)],
                                 axis=0).astype(BF16)
            v_even = vt_buf[c0:c0 + HEAD_PAD, r0:r0 + 2 * ATTN_BLOCK]
            v_odd = jnp.concatenate([v_even[HEAD_DIM:HEAD_PAD], v_even[0:HEAD_DIM]], axis=0)
            half = 2 * ATTN_BLOCK
            o = (_dot(v_even, pt[:, 0:half]) * inv_den[:, 0:half]
                 + _dot(v_odd, pt[:, half:2 * half]) * inv_den[:, half:2 * half])
            ch = kv * Q_GROUP * HEAD_DIM
            zbt_buf[ch:ch + HEAD_PAD, r0:r0 + ATTN_BLOCK] = o[:, 0:ATTN_BLOCK]
            zbt_buf[ch + HEAD_PAD:ch + 2 * HEAD_PAD, r0:r0 + ATTN_BLOCK] = o[:, ATTN_BLOCK:half]
    kbuf[0:ATTN_BLOCK, :] = kbuf[tm:tm + ATTN_BLOCK, :]
    vt_buf[:, 0:ATTN_BLOCK] = vt_buf[:, tm:tm + ATTN_BLOCK]

    y_a = _dot(za, wa_ref[...])
    y_b = _dot(zbt_buf[...].T.astype(BF16), wb_ref[...])
    g_a = _sigmoid(_dot(hb, wg_ref[:, 0:D_MODEL]) + bg_ref[:, 0:D_MODEL])
    merged = g_a * y_a
    g_b = _sigmoid(_dot(hb, wg_ref[:, D_MODEL:2 * D_MODEL]) + bg_ref[:, D_MODEL:2 * D_MODEL])
    merged = (merged + g_b * y_b).astype(BF16)
    x1 = x + _dot(merged, wout_ref[...])
    x1_ref[...] = x1

    is_first = (pl.program_id(0) == 0) & (s == 0)
    _route_tile(x1, is_first, gffn_ref, wr_ref, rb_ref, tri_ref,
                h2p_ref, idx_ref, wt_ref, rk_ref, cnt_ref, run_ref)


def _attn_tables(attn_sinks):
    slopes = 2.0 ** (-8.0 * (np.arange(N_HEADS) + 1) / N_HEADS)
    j = np.arange(ATTN_BLOCK)[:, None]
    i = np.arange(ATTN_BLOCK)[None, :]
    from_prev = j > i
    dist = np.where(from_prev, ATTN_BLOCK + i - j, i - j)
    bias = np.empty((2, N_KV_HEADS, ATTN_BLOCK, Q_GROUP * ATTN_BLOCK), np.float32)
    for first in range(2):
        ok = ~from_prev if first else np.ones_like(from_prev)
        for kv in range(N_KV_HEADS):
            for slot, g in enumerate(HEAD_ORDER):
                sl = np.float32(slopes[kv * Q_GROUP + g])
                val = -(sl * dist.astype(np.float32))
                bias[first, kv, :, slot * ATTN_BLOCK:(slot + 1) * ATTN_BLOCK] = np.where(ok, val, MASKED)
    sink = attn_sinks.astype(F32).reshape(N_KV_HEADS, Q_GROUP)[:, np.asarray(HEAD_ORDER)]
    sink = jnp.repeat(sink, ATTN_BLOCK, axis=1)
    return jnp.asarray(bias), sink.reshape(N_KV_HEADS, 1, Q_GROUP * ATTN_BLOCK)


def _pad_heads(w, n_heads):
    k = w.shape[0]
    w = w.reshape(k, n_heads, HEAD_DIM)
    w = jnp.pad(w, ((0, 0), (0, 0), (0, HEAD_PAD - HEAD_DIM)))
    return w.reshape(k, n_heads * HEAD_PAD)


def _fold_pool_kernel(wmix_ref, ps_ref, wa_ref, out_ref):
    out_ref[...] = _dot(wmix_ref[0] * ps_ref[0], wa_ref[...])


def _fold_pool(w_pool_mix, pool_scale, w_branch_a):
    n_g = w_pool_mix.shape[0]
    gd = POOL_GROUP_DIM
    return pl.pallas_call(
        _fold_pool_kernel,
        grid=(n_g,),
        in_specs=[pl.BlockSpec((1, gd, gd), lambda g: (g, 0, 0)),
                  pl.BlockSpec((1, 1, gd), lambda g: (g, 0, 0)),
                  pl.BlockSpec((gd, D_MODEL), lambda g: (g, 0))],
        out_specs=pl.BlockSpec((gd, D_MODEL), lambda g: (g, 0)),
        out_shape=jax.ShapeDtypeStruct((n_g * gd, D_MODEL), F32),
        name="fold_pool",
    )(w_pool_mix, pool_scale.reshape(n_g, 1, gd), w_branch_a)


def _const_spec(shape):
    nd = len(shape)
    return pl.BlockSpec(shape, lambda *_: (0,) * nd)


def _mixer(x2d, batch0, batch, seq, g_mix, w_in, w_pool_mix, pool_scale, w_branch_a, q_gain, k_gain,
           attn_sinks, w_branch_b, w_gate, b_gate, w_out, g_ffn, w_router, router_bias, after=None):
    t = batch * seq
    tm = TM_MIX
    ns = seq // tm
    w_q = w_in[:, POOL_DIM:POOL_DIM + Q_DIM]
    w_k = w_in[:, POOL_DIM + Q_DIM:POOL_DIM + Q_DIM + KV_DIM]
    w_v = w_in[:, POOL_DIM + Q_DIM + KV_DIM:]
    win_p = jnp.concatenate([w_in[:, :POOL_DIM], _pad_heads(w_q, N_HEADS), _pad_heads(w_k, N_KV_HEADS),
                             _pad_heads(w_v, N_KV_HEADS)], axis=1).astype(BF16)
    qg = jnp.tile(jnp.pad(q_gain * (HEAD_DIM ** -0.5), (0, HEAD_PAD - HEAD_DIM)), N_HEADS).reshape(1, -1)
    kg = jnp.tile(jnp.pad(k_gain, (0, HEAD_PAD - HEAD_DIM)), N_KV_HEADS).reshape(1, -1)
    bias, sink = _attn_tables(attn_sinks)
    ones2 = jnp.asarray(np.kron(np.eye(2), np.ones((HEAD_PAD, HEAD_PAD))), BF16)
    w_pool_a = _fold_pool(w_pool_mix, pool_scale, w_branch_a).astype(BF16)
    operands = (x2d, g_mix.reshape(1, D_MODEL), win_p, qg, kg, bias, sink, ones2,
                w_pool_a, w_branch_b.astype(BF16),
                w_gate.astype(BF16), b_gate.reshape(1, 2 * D_MODEL), w_out.astype(BF16))
    operands += _router_operands(g_ffn, w_router, router_bias, tm)
    in_specs = [pl.BlockSpec((tm, D_MODEL), lambda b, s: ((batch0 + b) * ns + s, 0))]
    in_specs += [_const_spec(op.shape) for op in operands[1:]]
    if after is not None:
        operands += (after,)
        in_specs.append(pl.BlockSpec(memory_space=pl.ANY))
    row8 = pl.BlockSpec((TOP_K, tm), lambda b, s: (0, b * ns + s))
    return pl.pallas_call(
        _mixer_kernel,
        grid=(batch, ns),
        in_specs=in_specs,
        out_specs=[pl.BlockSpec((tm, D_MODEL), lambda b, s: (b * ns + s, 0)),
                   pl.BlockSpec((tm, HALF), lambda b, s: (b * ns + s, 0)),
                   row8,
                   pl.BlockSpec((tm, TOP_K * SC_LANES), lambda b, s: (b * ns + s, 0)),
                   row8,
                   pl.BlockSpec((N_EXPERTS, 128), lambda b, s: (0, 0))],
        out_shape=[jax.ShapeDtypeStruct((t, D_MODEL), F32),
                   jax.ShapeDtypeStruct((t, HALF), I32),
                   jax.ShapeDtypeStruct((TOP_K, t), I32),
                   jax.ShapeDtypeStruct((t, TOP_K * SC_LANES), F32),
                   jax.ShapeDtypeStruct((TOP_K, t), I32),
                   jax.ShapeDtypeStruct((N_EXPERTS, 128), F32)],
        scratch_shapes=[
            pltpu.VMEM((ATTN_BLOCK + tm, N_KV_HEADS * HEAD_PAD), BF16),
            pltpu.VMEM((N_KV_HEADS * HEAD_PAD, ATTN_BLOCK + tm), BF16),
            pltpu.VMEM((4, POOL_TAIL + tm, POOL_DIM), F32),
            pltpu.VMEM((Q_DIM, tm), F32),
            pltpu.VMEM((N_EXPERTS, 128), F32),
        ],
        compiler_params=pltpu.CompilerParams(
            dimension_semantics=("arbitrary", "arbitrary"), vmem_limit_bytes=VMEM_LIMIT),
        name="mixer",
    )(*operands)


def _route_tile(x1, is_first, gffn_ref, wr_ref, rb_ref, tri_ref,
                h2p_ref, idx_ref, wt_ref, rk_ref, cnt_ref, run_ref):
    tm = x1.shape[0]

    @pl.when(is_first)
    def _():
        run_ref[...] = jnp.zeros_like(run_ref)

    h2 = _rms(x1, gffn_ref[...])
    hb = h2.astype(BF16)
    h2p_ref[...] = _pack(h2[:, 0:HALF], h2[:, HALF:D_MODEL])

    logits = lax.dot_general(wr_ref[...], hb, _NT, preferred_element_type=F32)
    scores = _sigmoid(logits)
    choice = scores + rb_ref[...]

    neg = -jnp.inf
    sub8 = lax.broadcasted_iota(I32, (GROUP_SIZE, tm), 0).astype(F32)
    grp_rows = []
    for gi in range(N_EXPERT_GROUPS):
        cg = choice[GROUP_SIZE * gi:GROUP_SIZE * (gi + 1)]
        m1 = jnp.max(cg, axis=0, keepdims=True)
        first = jnp.min(jnp.where(cg == m1, sub8, float(GROUP_SIZE)), axis=0, keepdims=True)
        m2 = jnp.max(jnp.where(sub8 == first, neg, cg), axis=0, keepdims=True)
        grp_rows.append(m1 + m2)
    gs = jnp.concatenate(grp_rows, axis=0)
    beaten = jnp.zeros((N_EXPERT_GROUPS, tm), F32)
    for gi in range(N_EXPERT_GROUPS):
        row = grp_rows[gi]
        wins = (row > gs) | ((row == gs) & (sub8 > float(gi)))
        beaten = beaten + jnp.where(wins, 1.0, 0.0)
    gsel = beaten < float(TOPK_GROUPS)
    cm = jnp.concatenate(
        [jnp.where(gsel[gi:gi + 1], choice[GROUP_SIZE * gi:GROUP_SIZE * (gi + 1)], neg)
         for gi in range(N_EXPERT_GROUPS)], axis=0)

    iota_e = lax.broadcasted_iota(I32, (N_EXPERTS, tm), 0).astype(F32)
    idx_rows, s_rows = [], []
    sel = jnp.zeros((N_EXPERTS, tm), F32)
    for _ in range(TOP_K):
        m = jnp.max(cm, axis=0, keepdims=True)
        idx = jnp.min(jnp.where(cm == m, iota_e, float(N_EXPERTS)), axis=0, keepdims=True)
        oh = iota_e == idx
        s_rows.append(jnp.sum(jnp.where(oh, scores, 0.0), axis=0, keepdims=True))
        idx_rows.append(idx)
        cm = jnp.where(oh, neg, cm)
        sel = sel + jnp.where(oh, 1.0, 0.0)

    run = run_ref[:, 0:1]
    cum = _dot(sel.astype(BF16), tri_ref[...])
    before = run + cum - sel
    rk_rows = [jnp.sum(jnp.where(iota_e == idx, before, 0.0), axis=0, keepdims=True) for idx in idx_rows]
    new_run = run + jnp.sum(sel, axis=1, keepdims=True)
    run_ref[...] = jnp.broadcast_to(new_run, run_ref.shape)
    cnt_ref[...] = jnp.broadcast_to(new_run, cnt_ref.shape)

    ssum = s_rows[0]
    for r in range(1, TOP_K):
        ssum = ssum + s_rows[r]
    denom = ssum + 1e-20
    idx_ref[...] = jnp.concatenate(idx_rows, axis=0).astype(I32)
    w_rep = jnp.concatenate([jnp.broadcast_to(sr / denom * ROUTED_SCALE, (SC_LANES, tm)) for sr in s_rows],
                            axis=0)
    wt_ref[...] = w_rep.T
    rk_ref[...] = jnp.concatenate(rk_rows, axis=0).astype(I32)


def _router_operands(g_ffn, w_router, router_bias, tm):
    wr_t = w_router.astype(F32).T.astype(BF16)
    tri = (np.arange(tm)[:, None] <= np.arange(tm)[None, :]).astype(np.float32)
    return (g_ffn.reshape(1, D_MODEL), wr_t, router_bias.astype(F32).reshape(N_EXPERTS, 1),
            jnp.asarray(tri, BF16))


def _swiglu_packed(x_packed, wg, wu, wd):
    lo, hi = _unpack(x_packed)
    lo = lo.astype(BF16)
    hi = hi.astype(BF16)
    g = _dot(lo, wg[0:HALF, :]) + _dot(hi, wg[HALF:D_MODEL, :])
    u = _dot(lo, wu[0:HALF, :]) + _dot(hi, wu[HALF:D_MODEL, :])
    a = (g * _sigmoid(g) * u).astype(BF16)
    y = _dot(a, wd[...])
    return _pack(y[:, 0:HALF], y[:, HALF:D_MODEL])


def _shared_kernel(h2p_ref, wg_ref, wu_ref, wd_ref, *rest):
    ysh_ref = rest[-1]
    ysh_ref[...] = _swiglu_packed(h2p_ref[...], wg_ref, wu_ref, wd_ref)


def _shared(h2p, w_sh_gate, w_sh_up, w_sh_down, after=None):
    t = h2p.shape[0]
    tm = TM_FFN
    operands = (h2p, w_sh_gate.astype(BF16), w_sh_up.astype(BF16), w_sh_down.astype(BF16))
    in_specs = [pl.BlockSpec((tm, HALF), lambda i: (i, 0))]
    in_specs += [_const_spec(op.shape) for op in operands[1:]]
    if after is not None:
        operands += (after,)
        in_specs.append(pl.BlockSpec(memory_space=pl.ANY))
    return pl.pallas_call(
        _shared_kernel,
        grid=(t // tm,),
        in_specs=in_specs,
        out_specs=pl.BlockSpec((tm, HALF), lambda i: (i, 0)),
        out_shape=jax.ShapeDtypeStruct((t, HALF), I32),
        compiler_params=pltpu.CompilerParams(
            dimension_semantics=("arbitrary",), vmem_limit_bytes=VMEM_LIMIT),
        name="shared",
    )(*operands)


def _positions_kernel(pstart_ref, idx_ref, rk_ref, pos_ref):
    idx = idx_ref[...]
    acc = rk_ref[...]
    for e in range(N_EXPERTS):
        acc = acc + jnp.where(idx == e, pstart_ref[e], 0)
    pos_ref[...] = acc


def _positions(pstart, idx8, rk8):
    t = idx8.shape[1]
    tm = TM_POS
    spec = pl.BlockSpec((TOP_K, tm), lambda i, ps: (0, i))
    return pl.pallas_call(
        _positions_kernel,
        grid_spec=pltpu.PrefetchScalarGridSpec(num_scalar_prefetch=1, grid=(t // tm,),
                                               in_specs=[spec, spec], out_specs=spec),
        out_shape=jax.ShapeDtypeStruct((TOP_K, t), I32),
        name="positions",
    )(pstart, idx8, rk8)


def _sc_workers():
    info = plsc.get_sparse_core_info()
    return info.num_cores, info.num_cores * info.num_subcores


def _dispatch(pos_flat, h2p, n_rows):
    t = h2p.shape[0]
    ch = SC_SCATTER_ROWS
    n_cores, n_workers = _sc_workers()
    n_ch = t // (n_workers * ch)

    def body(pos_hbm, h_hbm, xs_hbm, idx_v, rows_v, sem):
        wid = lax.axis_index("s") * n_cores + lax.axis_index("c")

        @pl.loop(0, n_ch)
        def _(c):
            chunk = wid * n_ch + c
            pltpu.sync_copy(h_hbm.at[pl.ds(chunk * ch, ch)], rows_v)
            for k in range(TOP_K):
                pltpu.sync_copy(pos_hbm.at[pl.ds(k * t + chunk * ch, ch)], idx_v.at[k])
            copies = [pltpu.async_copy(rows_v, xs_hbm.at[idx_v.at[k]], sem) for k in range(TOP_K)]
            for cp in copies:
                cp.wait()

    return pl.kernel(
        body,
        out_type=jax.ShapeDtypeStruct((n_rows, HALF), I32),
        mesh=plsc.VectorSubcoreMesh(core_axis_name="c", subcore_axis_name="s"),
        scratch_types=[pltpu.VMEM((TOP_K, ch), I32), pltpu.VMEM((ch, HALF), I32),
                       pltpu.SemaphoreType.DMA],
        name="dispatch",
    )(pos_flat, h2p)


def _pairwise_sum(xs):
    while len(xs) > 1:
        xs = [xs[i] + xs[i + 1] for i in range(0, len(xs), 2)]
    return xs[0]


def _gather_sum(pos_flat, w_rep, ys, tok0, t):
    t_all = w_rep.shape[0]
    ch = SC_SUM_ROWS
    lanes = SC_LANES
    n_cores, n_workers = _sc_workers()
    per_w = t // n_workers
    n_ch = per_w // ch

    def body(pos_hbm, w_hbm, ys_hbm, out_hbm, idx_all, w_all, buf0, buf1, acc0, acc1, gsem0, gsem1, osem0, osem1):
        wid = lax.axis_index("s") * n_cores + lax.axis_index("c")
        base = wid * n_ch
        first = tok0 + wid * per_w
        bufs, accs, gsems, osems = (buf0, buf1), (acc0, acc1), (gsem0, gsem1), (osem0, osem1)
        for k in range(TOP_K):
            pltpu.sync_copy(pos_hbm.at[pl.ds(k * t_all + first, per_w)], idx_all.at[pl.ds(k * per_w, per_w)])
        pltpu.sync_copy(w_hbm.at[pl.ds(first, per_w)], w_all)

        def row_copy(c, s, k):
            rows = idx_all.at[pl.ds(k * per_w + c * ch, ch)]
            return pltpu.make_async_copy(ys_hbm.at[rows], bufs[s].at[k], gsems[s])

        def out_copy(c, s):
            return pltpu.make_async_copy(accs[s], out_hbm.at[pl.ds((base + c) * ch, ch)], osems[s])

        def fetch(c, s):
            for k in range(TOP_K):
                row_copy(c, s, k).start()

        def reduce_rows(c, s):
            @pl.loop(0, ch)
            def _(r):
                wks = [w_all[c * ch + r, pl.ds(k * lanes, lanes)] for k in range(TOP_K)]

                @pl.loop(0, HALF // lanes, step=2)
                def _(v):
                    for dv in range(2):
                        col = (v + dv) * lanes
                        los, his = [], []
                        for k in range(TOP_K):
                            words = bufs[s][k, r, pl.ds(col, lanes)]
                            los.append(wks[k] * lax.bitcast_convert_type(lax.shift_left(words, jnp.int32(16)), F32))
                            his.append(wks[k] * lax.bitcast_convert_type(words & jnp.int32(-65536), F32))
                        accs[s][r, pl.ds(col, lanes)] = _pairwise_sum(los)
                        accs[s][r, pl.ds(HALF + col, lanes)] = _pairwise_sum(his)

        fetch(0, 0)

        @pl.loop(0, n_ch, step=2)
        def _(c):
            for s in range(2):
                cc = c + s

                @pl.when(cc + 1 < n_ch)
                def _():
                    fetch(cc + 1, 1 - s)

                for k in range(TOP_K):
                    row_copy(cc, s, k).wait()

                @pl.when(cc >= 2)
                def _():
                    out_copy(cc - 2, s).wait()

                reduce_rows(cc, s)
                out_copy(cc, s).start()

        out_copy(n_ch - 2, 0).wait()
        out_copy(n_ch - 1, 1).wait()

    return pl.kernel(
        body,
        out_type=jax.ShapeDtypeStruct((t, D_MODEL), F32),
        mesh=plsc.VectorSubcoreMesh(core_axis_name="c", subcore_axis_name="s"),
        scratch_types=[pltpu.VMEM((TOP_K * per_w,), I32), pltpu.VMEM((per_w, TOP_K * lanes), F32),
                       pltpu.VMEM((TOP_K, ch, HALF), I32), pltpu.VMEM((TOP_K, ch, HALF), I32),
                       pltpu.VMEM((ch, D_MODEL), F32), pltpu.VMEM((ch, D_MODEL), F32),
                       pltpu.SemaphoreType.DMA, pltpu.SemaphoreType.DMA,
                       pltpu.SemaphoreType.DMA, pltpu.SemaphoreType.DMA],
        compiler_params=pltpu.CompilerParams(needs_layout_passes=False),
        name="gather_sum",
    )(pos_flat, w_rep, ys)


def _expert_kernel(nblk_ref, blk0_ref, cnt_ref, wg_ref, wu_ref, wd_ref, xs_hbm, after_ref, ys_hbm,
                   wg_s, wu_s, wd_s, xbuf, ybuf, in_sem, out_sem):
    del after_ref
    e = pl.program_id(0)
    nb = nblk_ref[e]
    b0 = blk0_ref[e]
    cnt = cnt_ref[e]
    n_all = blk0_ref[N_EXPERTS]
    ahead = EXPERT_SLOTS - 2
    wg_s[...] = wg_ref[0].astype(BF16)
    wu_s[...] = wu_ref[0].astype(BF16)
    wd_s[...] = wd_ref[0].astype(BF16)

    def slot_rows(g, n_blocks=1):
        start = (g & (EXPERT_SLOTS - 1)) * MOE_BLOCK
        return pl.ds(pl.multiple_of(start, MOE_BLOCK), n_blocks * MOE_BLOCK)

    def rows_of(g):
        return pl.ds(pl.multiple_of(g * MOE_BLOCK, MOE_BLOCK), MOE_BLOCK)

    def in_copy(g):
        return pltpu.make_async_copy(xs_hbm.at[rows_of(g)], xbuf.at[slot_rows(g)],
                                     in_sem.at[g & (EXPERT_SLOTS - 1)])

    def out_copy(g):
        return pltpu.make_async_copy(ybuf.at[slot_rows(g)], ys_hbm.at[rows_of(g)],
                                     out_sem.at[g & (EXPERT_SLOTS - 1)])

    @pl.when(e == 0)
    def _():
        for g in range(ahead):
            @pl.when(g < n_all)
            def _():
                in_copy(g).start()

    def arrive(g):
        in_copy(g).wait()

        @pl.when(g + ahead < n_all)
        def _():
            in_copy(g + ahead).start()

        @pl.when(g >= EXPERT_SLOTS)
        def _():
            out_copy(g - EXPERT_SLOTS).wait()

    def run_blocks(j, n_blocks):
        g = b0 + j
        for i in range(n_blocks):
            arrive(g + i)
        rows = lax.broadcasted_iota(I32, (n_blocks * MOE_BLOCK, HALF), 0)
        x = jnp.where(rows < cnt - j * MOE_BLOCK, xbuf[slot_rows(g, n_blocks)], 0)
        ybuf[slot_rows(g, n_blocks)] = _swiglu_packed(x, wg_s, wu_s, wd_s)
        for i in range(n_blocks):
            out_copy(g + i).start()

    lead = jnp.where(nb > 0, b0 & 1, 0)

    @pl.when(lead == 1)
    def _():
        run_blocks(0, 1)

    n_pairs = (nb - lead) // 2

    def one_pair(p, carry):
        run_blocks(lead + 2 * p, 2)
        return carry

    lax.fori_loop(0, n_pairs, one_pair, 0)

    @pl.when(nb - lead - 2 * n_pairs == 1)
    def _():
        run_blocks(nb - 1, 1)

    @pl.when(e == N_EXPERTS - 1)
    def _():
        for back in range(1, EXPERT_SLOTS + 1):
            @pl.when(n_all >= back)
            def _():
                out_copy(n_all - back).wait()


def _experts(n_blk, blk0, counts, n_rows, xs, w_gate, w_up, w_down, after):
    wspec_in = pl.BlockSpec((1, D_MODEL, EXPERT_DIM), lambda e, *_: (e, 0, 0))
    grid_spec = pltpu.PrefetchScalarGridSpec(
        num_scalar_prefetch=3,
        grid=(N_EXPERTS,),
        in_specs=[wspec_in, wspec_in,
                  pl.BlockSpec((1, EXPERT_DIM, D_MODEL), lambda e, *_: (e, 0, 0)),
                  pl.BlockSpec(memory_space=pl.ANY),
                  pl.BlockSpec(memory_space=pl.ANY)],
        out_specs=pl.BlockSpec(memory_space=pl.ANY),
        scratch_shapes=[pltpu.VMEM((D_MODEL, EXPERT_DIM), BF16), pltpu.VMEM((D_MODEL, EXPERT_DIM), BF16),
                        pltpu.VMEM((EXPERT_DIM, D_MODEL), BF16),
                        pltpu.VMEM((EXPERT_SLOTS * MOE_BLOCK, HALF), I32),
                        pltpu.VMEM((EXPERT_SLOTS * MOE_BLOCK, HALF), I32),
                        pltpu.SemaphoreType.DMA((EXPERT_SLOTS,)), pltpu.SemaphoreType.DMA((EXPERT_SLOTS,))],
    )
    return pl.pallas_call(
        _expert_kernel,
        grid_spec=grid_spec,
        out_shape=jax.ShapeDtypeStruct((n_rows, HALF), I32),
        compiler_params=pltpu.CompilerParams(
            dimension_semantics=("arbitrary",), vmem_limit_bytes=VMEM_LIMIT),
        name="experts",
    )(n_blk, blk0, counts, w_gate, w_up, w_down, xs, after)


def _combine_kernel(x1_ref, ysh_ref, moe_ref, p_ref, gple_ref, wpg_ref, wpp_ref, gpost_ref, *rest):
    out_ref = rest[-1]
    proj = _rms(_dot(p_ref[...].astype(BF16), wpp_ref[...]), gpost_ref[...])
    sh_lo, sh_hi = _unpack(ysh_ref[...])
    x2 = x1_ref[...] + moe_ref[...] + jnp.concatenate([sh_lo, sh_hi], axis=1)
    gate = _sigmoid(_dot(_rms(x2, gple_ref[...]).astype(BF16), wpg_ref[...]))
    out_ref[...] = x2 + gate * proj


def _combine(x1, ysh, moe, p2d, g_ple, w_ple_gate, w_ple_proj, g_ple_post, tok0, tok0_all, prev_out):
    t = p2d.shape[0]
    tm = TM_COMBINE
    steps = moe.shape[0] // tm
    off = tok0 // tm
    off_all = tok0_all // tm
    operands = [x1, ysh, moe, p2d, g_ple.reshape(1, D_MODEL), w_ple_gate.astype(BF16),
                w_ple_proj.astype(BF16), g_ple_post.reshape(1, D_MODEL)]
    in_specs = [pl.BlockSpec((tm, D_MODEL), lambda i: (off + i, 0)),
                pl.BlockSpec((tm, HALF), lambda i: (off + i, 0)),
                pl.BlockSpec((tm, D_MODEL), lambda i: (i, 0)),
                pl.BlockSpec((tm, PLE_DIM), lambda i: (off_all + i, 0)),
                _const_spec((1, D_MODEL)),
                _const_spec((D_MODEL, D_MODEL)),
                _const_spec((PLE_DIM, D_MODEL)),
                _const_spec((1, D_MODEL))]
    aliases = {}
    if prev_out is not None:
        aliases = {len(operands): 0}
        operands.append(prev_out)
        in_specs.append(pl.BlockSpec(memory_space=pl.ANY))
    return pl.pallas_call(
        _combine_kernel,
        grid=(steps,),
        in_specs=in_specs,
        out_specs=pl.BlockSpec((tm, D_MODEL), lambda i: (off_all + i, 0)),
        out_shape=jax.ShapeDtypeStruct((t, D_MODEL), F32),
        input_output_aliases=aliases,
        compiler_params=pltpu.CompilerParams(
            dimension_semantics=("arbitrary",), vmem_limit_bytes=VMEM_LIMIT),
        name="combine",
    )(*operands)


def _layer(x2d, p2d, batch, seq, g_mix, w_in, w_pool_mix, pool_scale, w_branch_a, q_gain, k_gain,
           attn_sinks, w_branch_b, w_gate, b_gate, w_out, g_ffn, w_router, router_bias, w_exp_gate,
           w_exp_up, w_exp_down, w_sh_gate, w_sh_up, w_sh_down, g_ple, w_ple_gate, w_ple_proj,
           g_ple_post):
    bh = batch // PIPELINE_HALVES
    th = bh * seq
    n_rows = (-(-(th * TOP_K) // MOE_BLOCK) + N_EXPERTS) * MOE_BLOCK

    halves = []
    after = None
    for h in range(PIPELINE_HALVES):
        x1, h2p, idx8, w_rep, rk8, cnt = _mixer(
            x2d, h * bh, bh, seq, g_mix, w_in, w_pool_mix, pool_scale, w_branch_a, q_gain, k_gain,
            attn_sinks, w_branch_b, w_gate, b_gate, w_out, g_ffn, w_router, router_bias, after=after)
        counts = cnt[:, 0].astype(I32)
        padded = (counts + MOE_BLOCK - 1) // MOE_BLOCK * MOE_BLOCK
        pend = jnp.cumsum(padded)
        pstart = pend - padded
        pos = _positions(pstart.astype(I32), idx8, rk8).reshape(-1)
        xs = _dispatch(pos, h2p, n_rows)
        blk0 = jnp.concatenate([pstart, pend[-1:]]) // MOE_BLOCK
        halves.append((x1, h2p, w_rep, pos, xs, (padded // MOE_BLOCK).astype(I32), blk0.astype(I32), counts))
        after = pos

    ys_all = []
    for (x1, h2p, w_rep, pos, xs, n_blk, blk0, counts) in halves:
        ys_all.append(_experts(n_blk, blk0, counts, n_rows, xs, w_exp_gate, w_exp_up, w_exp_down,
                               ys_all[-1] if ys_all else xs))

    out = None
    tc = th // COMBINE_CHUNKS
    for h, (x1, h2p, w_rep, pos, xs, n_blk, blk0, counts) in enumerate(halves):
        ys = ys_all[h]
        ysh = _shared(h2p, w_sh_gate, w_sh_up, w_sh_down, after=ys_all[-1] if out is None else out)
        for c in range(COMBINE_CHUNKS):
            moe = _gather_sum(pos, w_rep, ys, c * tc, tc)
            out = _combine(x1, ysh, moe, p2d, g_ple, w_ple_gate, w_ple_proj, g_ple_post,
                           c * tc, h * th + c * tc, out)
    return out


def kernel(x, p, g_mix, w_in, w_pool_mix, pool_scale, w_branch_a, q_gain, k_gain, attn_sinks,
           w_branch_b, w_gate, b_gate, w_out, g_ffn, w_router, router_bias, w_exp_gate, w_exp_up,
           w_exp_down, w_sh_gate, w_sh_up, w_sh_down, g_ple, w_ple_gate, w_ple_proj, g_ple_post):
    batch, seq, d = x.shape
    depth = p.shape[0]
    x2d = x.reshape(batch * seq, d)
    for i in range(depth):
        x2d = _layer(x2d, p[i].reshape(batch * seq, PLE_DIM), batch, seq, g_mix[i], w_in[i],
                     w_pool_mix[i], pool_scale[i], w_branch_a[i], q_gain[i], k_gain[i], attn_sinks[i],
                     w_branch_b[i], w_gate[i], b_gate[i], w_out[i], g_ffn[i], w_router[i],
                     router_bias[i], w_exp_gate[i], w_exp_up[i], w_exp_down[i], w_sh_gate[i],
                     w_sh_up[i], w_sh_down[i], g_ple[i], w_ple_gate[i], w_ple_proj[i], g_ple_post[i])
    return x2d.reshape(batch, seq, d)
```
